```python
import jax, jax.numpy as jnp
from jax import lax
import numpy as np

D_MODEL = 1024
BATCH = 32
SEQ = 2048
DEPTH = 2

N_MIXERS = 2
EXPAND = 2
D_INNER = EXPAND * D_MODEL
HEAD_DIM = 64
N_HEADS = D_INNER // HEAD_DIM
Q_BLOCK = 128
NORM_EPS = 1e-6
N_FOX = (DEPTH + N_MIXERS - 1) // N_MIXERS
N_SB = DEPTH // N_MIXERS
FOX_IN = 4 * D_INNER + N_HEADS
SB_IN = 4 * D_INNER

kernel_name = "hybrid_fox_stickbreaking_adaln"


def rmsnorm(x, g):
    xf = x.astype(jnp.float32)
    y = xf * lax.rsqrt(jnp.mean(xf * xf, axis=-1, keepdims=True) + NORM_EPS)
    return (y * g.astype(jnp.float32)).astype(x.dtype)


def adaln(c, w_ada, b_ada):
    mod = jax.nn.silu(c) @ w_ada + b_ada
    shift, scale, gate = jnp.split(mod, 3, axis=-1)
    return shift[:, None, :], scale[:, None, :], gate[:, None, :]


def split_heads(t):
    b, s, _ = t.shape
    return t.reshape(b, s, N_HEADS, HEAD_DIM).transpose(0, 2, 1, 3)


def merge_heads(t):
    b, h, s, d = t.shape
    return t.transpose(0, 2, 1, 3).reshape(b, s, h * d)


def forgetting_attention(q, k, v, log_f):
    seq = q.shape[2]
    cum = jnp.cumsum(log_f, axis=-1)
    sm_scale = HEAD_DIM ** -0.5
    outs = []
    for blk in range(seq // Q_BLOCK):
        q0, q1 = blk * Q_BLOCK, (blk + 1) * Q_BLOCK
        qb, kp, vp = q[:, :, q0:q1], k[:, :, :q1], v[:, :, :q1]
        logits = jnp.einsum("bhtd,bhsd->bhts", qb, kp,
                            preferred_element_type=jnp.float32) * sm_scale
        logits = logits + cum[:, :, q0:q1, None] - cum[:, :, None, :q1]
        t_idx = jnp.arange(q0, q1)[:, None]
        s_idx = jnp.arange(q1)[None, :]
        logits = jnp.where(s_idx <= t_idx, logits, -jnp.inf)
        p = jax.nn.softmax(logits, axis=-1)
        outs.append(jnp.einsum("bhts,bhsd->bhtd", p.astype(vp.dtype), vp))
    return jnp.concatenate(outs, axis=2)


def stick_breaking_attention(q, k, v):
    seq = q.shape[2]
    sm_scale = HEAD_DIM ** -0.5
    outs = []
    for blk in range(seq // Q_BLOCK):
        q0, q1 = blk * Q_BLOCK, (blk + 1) * Q_BLOCK
        qb, kp, vp = q[:, :, q0:q1], k[:, :, :q1], v[:, :, :q1]
        z = jnp.einsum("bhtd,bhsd->bhts", qb, kp,
                       preferred_element_type=jnp.float32) * sm_scale
        t_idx = jnp.arange(q0, q1)[:, None]
        s_idx = jnp.arange(q1)[None, :]
        strict = s_idx < t_idx
        log_beta = jax.nn.log_sigmoid(z)
        log_keep = jnp.where(strict, jax.nn.log_sigmoid(-z), 0.0)
        after = lax.cumsum(log_keep, axis=3, reverse=True) - log_keep
        a = jnp.where(strict, jnp.exp(log_beta + after), 0.0)
        outs.append(jnp.einsum("bhts,bhsd->bhtd", a.astype(vp.dtype), vp))
    return jnp.concatenate(outs, axis=2)


def fox_layer(x, c, norm_g, w_ada, b_ada, w_in, b_f, w_out):
    shift, scale, gate = adaln(c, w_ada, b_ada)
    h = rmsnorm(x, norm_g) * (1.0 + scale) + shift
    proj = h @ w_in
    q, k, v, zg = (proj[..., i * D_INNER:(i + 1) * D_INNER] for i in range(4))
    f_logit = proj[..., 4 * D_INNER:] + b_f
    log_f = jax.nn.log_sigmoid(f_logit.astype(jnp.float32)).transpose(0, 2, 1)
    o = forgetting_attention(split_heads(q), split_heads(k), split_heads(v), log_f)
    y = (merge_heads(o) * jax.nn.silu(zg)) @ w_out
    return x + gate * y


def sb_layer(x, c, norm_g, w_ada, b_ada, w_in, w_out):
    shift, scale, gate = adaln(c, w_ada, b_ada)
    h = rmsnorm(x, norm_g) * (1.0 + scale) + shift
    proj = h @ w_in
    q, k, v, zg = (proj[..., i * D_INNER:(i + 1) * D_INNER] for i in range(4))
    o = stick_breaking_attention(split_heads(q), split_heads(k), split_heads(v))
    y = (merge_heads(o) * jax.nn.silu(zg)) @ w_out
    return x + gate * y


def _fwd_setup_inputs(seed: int = 0) -> dict:
    key = jax.random.key(seed)
    ks = jax.random.split(key, 16)
    f32 = jnp.float32
    din = D_MODEL ** -0.5
    dinner = D_INNER ** -0.5
    x = jax.random.normal(ks[0], (BATCH, SEQ, D_MODEL), f32)
    c = jax.random.normal(ks[1], (BATCH, D_MODEL), f32)
    fox_norm_g = 1.0 + 0.02 * jax.random.normal(ks[2], (N_FOX, D_MODEL), f32)
    fox_w_ada = 0.5 * din * jax.random.normal(ks[3], (N_FOX, D_MODEL, 3 * D_MODEL), f32)
    fox_b_ada = 0.02 * jax.random.normal(ks[4], (N_FOX, 3 * D_MODEL), f32)
    fox_w_in = din * jax.random.normal(ks[5], (N_FOX, D_MODEL, FOX_IN), f32)
    fox_b_f = 1.0 + 0.5 * jax.random.normal(ks[6], (N_FOX, N_HEADS), f32)
    fox_w_out = dinner * jax.random.normal(ks[7], (N_FOX, D_INNER, D_MODEL), f32)
    sb_norm_g = 1.0 + 0.02 * jax.random.normal(ks[8], (N_SB, D_MODEL), f32)
    sb_w_ada = 0.5 * din * jax.random.normal(ks[9], (N_SB, D_MODEL, 3 * D_MODEL), f32)
    sb_b_ada = 0.02 * jax.random.normal(ks[10], (N_SB, 3 * D_MODEL), f32)
    sb_w_in = din * jax.random.normal(ks[11], (N_SB, D_MODEL, SB_IN), f32)
    sb_w_out = dinner * jax.random.normal(ks[12], (N_SB, D_INNER, D_MODEL), f32)
    final_norm_g = 1.0 + 0.02 * jax.random.normal(ks[13], (D_MODEL,), f32)
    return {"x": x, "c": c,
            "fox_norm_g": fox_norm_g, "fox_w_ada": fox_w_ada, "fox_b_ada": fox_b_ada,
            "fox_w_in": fox_w_in, "fox_b_f": fox_b_f, "fox_w_out": fox_w_out,
            "sb_norm_g": sb_norm_g, "sb_w_ada": sb_w_ada, "sb_b_ada": sb_b_ada,
            "sb_w_in": sb_w_in, "sb_w_out": sb_w_out,
            "final_norm_g": final_norm_g}


def _fwd_reference(x, c, fox_norm_g, fox_w_ada, fox_b_ada, fox_w_in, fox_b_f, fox_w_out,
              sb_norm_g, sb_w_ada, sb_b_ada, sb_w_in, sb_w_out, final_norm_g):
    for i in range(DEPTH):
        j = i // N_MIXERS
        if i % N_MIXERS == 0:
            x = fox_layer(x, c, fox_norm_g[j], fox_w_ada[j], fox_b_ada[j],
                          fox_w_in[j], fox_b_f[j], fox_w_out[j])
        else:
            x = sb_layer(x, c, sb_norm_g[j], sb_w_ada[j], sb_b_ada[j],
                         sb_w_in[j], sb_w_out[j])
    return rmsnorm(x, final_norm_g)


import jax as _jax
import jax.numpy as _jnp

TWIN_FORMAT = 'train_step'
FWD_PARAMS = ['x', 'c', 'fox_norm_g', 'fox_w_ada', 'fox_b_ada', 'fox_w_in', 'fox_b_f', 'fox_w_out', 'sb_norm_g', 'sb_w_ada', 'sb_b_ada', 'sb_w_in', 'sb_w_out', 'final_norm_g']
TWIN_WEIGHTS = ['fox_norm_g', 'fox_w_ada', 'fox_b_ada', 'fox_w_in', 'fox_b_f', 'fox_w_out', 'sb_norm_g', 'sb_w_ada', 'sb_b_ada', 'sb_w_in', 'sb_w_out', 'final_norm_g']
TWIN_DIFF_INPUT = 'x'
TWIN_INPUTS = ['x', 'c', 'fox_norm_g', 'fox_w_ada', 'fox_b_ada', 'fox_w_in', 'fox_b_f', 'fox_w_out', 'sb_norm_g', 'sb_w_ada', 'sb_b_ada', 'sb_w_in', 'sb_w_out', 'final_norm_g', 'loss_target', 'm_fox_norm_g', 'm_fox_w_ada', 'm_fox_b_ada', 'm_fox_w_in', 'm_fox_b_f', 'm_fox_w_out', 'm_sb_norm_g', 'm_sb_w_ada', 'm_sb_b_ada', 'm_sb_w_in', 'm_sb_w_out', 'm_final_norm_g', 'v_fox_norm_g', 'v_fox_w_ada', 'v_fox_b_ada', 'v_fox_w_in', 'v_fox_b_f', 'v_fox_w_out', 'v_sb_norm_g', 'v_sb_w_ada', 'v_sb_b_ada', 'v_sb_w_in', 'v_sb_w_out', 'v_final_norm_g']
TWIN_OUTPUTS = ['loss', 'grad_x', 'grad_fox_norm_g', 'grad_fox_w_ada', 'grad_fox_b_ada', 'grad_fox_w_in', 'grad_fox_b_f', 'grad_fox_w_out', 'grad_sb_norm_g', 'grad_sb_w_ada', 'grad_sb_b_ada', 'grad_sb_w_in', 'grad_sb_w_out', 'grad_final_norm_g', 'delta_fox_norm_g', 'delta_fox_w_ada', 'delta_fox_b_ada', 'delta_fox_w_in', 'delta_fox_b_f', 'delta_fox_w_out', 'delta_sb_norm_g', 'delta_sb_w_ada', 'delta_sb_b_ada', 'delta_sb_w_in', 'delta_sb_w_out', 'delta_final_norm_g', 'new_m_fox_norm_g', 'new_m_fox_w_ada', 'new_m_fox_b_ada', 'new_m_fox_w_in', 'new_m_fox_b_f', 'new_m_fox_w_out', 'new_m_sb_norm_g', 'new_m_sb_w_ada', 'new_m_sb_b_ada', 'new_m_sb_w_in', 'new_m_sb_w_out', 'new_m_final_norm_g', 'new_v_fox_norm_g', 'new_v_fox_w_ada', 'new_v_fox_b_ada', 'new_v_fox_w_in', 'new_v_fox_b_f', 'new_v_fox_w_out', 'new_v_sb_norm_g', 'new_v_sb_w_ada', 'new_v_sb_b_ada', 'new_v_sb_w_in', 'new_v_sb_w_out', 'new_v_final_norm_g']
TWIN_LEAF_KINDS = {'loss': 'loss', 'grad_x': 'grad_x', 'grad_fox_norm_g': 'grad_w', 'grad_fox_w_ada': 'grad_w', 'grad_fox_b_ada': 'grad_w', 'grad_fox_w_in': 'grad_w', 'grad_fox_b_f': 'grad_w', 'grad_fox_w_out': 'grad_w', 'grad_sb_norm_g': 'grad_w', 'grad_sb_w_ada': 'grad_w', 'grad_sb_b_ada': 'grad_w', 'grad_sb_w_in': 'grad_w', 'grad_sb_w_out': 'grad_w', 'grad_final_norm_g': 'grad_w', 'delta_fox_norm_g': 'delta_w', 'delta_fox_w_ada': 'delta_w', 'delta_fox_b_ada': 'delta_w', 'delta_fox_w_in': 'delta_w', 'delta_fox_b_f': 'delta_w', 'delta_fox_w_out': 'delta_w', 'delta_sb_norm_g': 'delta_w', 'delta_sb_w_ada': 'delta_w', 'delta_sb_b_ada': 'delta_w', 'delta_sb_w_in': 'delta_w', 'delta_sb_w_out': 'delta_w', 'delta_final_norm_g': 'delta_w', 'new_m_fox_norm_g': 'new_m', 'new_m_fox_w_ada': 'new_m', 'new_m_fox_b_ada': 'new_m', 'new_m_fox_w_in': 'new_m', 'new_m_fox_b_f': 'new_m', 'new_m_fox_w_out': 'new_m', 'new_m_sb_norm_g': 'new_m', 'new_m_sb_w_ada': 'new_m', 'new_m_sb_b_ada': 'new_m', 'new_m_sb_w_in': 'new_m', 'new_m_sb_w_out': 'new_m', 'new_m_final_norm_g': 'new_m', 'new_v_fox_norm_g': 'new_v', 'new_v_fox_w_ada': 'new_v', 'new_v_fox_b_ada': 'new_v', 'new_v_fox_w_in': 'new_v', 'new_v_fox_b_f': 'new_v', 'new_v_fox_w_out': 'new_v', 'new_v_sb_norm_g': 'new_v', 'new_v_sb_w_ada': 'new_v', 'new_v_sb_b_ada': 'new_v', 'new_v_sb_w_in': 'new_v', 'new_v_sb_w_out': 'new_v', 'new_v_final_norm_g': 'new_v'}


def _forward(args):
    return _fwd_reference(*[args[k] for k in FWD_PARAMS])


def _output_shape():
    out = _jax.eval_shape(lambda: _forward(_fwd_setup_inputs(0)))
    return out.shape, out.dtype

N_MICROBATCH = 1
ADAM_LR = 0.001
ADAM_B1 = 0.9
ADAM_B2 = 0.999
ADAM_EPS = 1e-08
ADAM_WD = 0.01
ADAM_STEP = 10
PER_EXAMPLE_BATCH_AXIS = {'x': 0, 'c': 0, 'loss_target': 0}
SHARED_INPUTS = []
_WEIGHT_DTYPES = {'fox_norm_g': _jnp.float32, 'fox_w_ada': _jnp.float32, 'fox_b_ada': _jnp.float32, 'fox_w_in': _jnp.float32, 'fox_b_f': _jnp.float32, 'fox_w_out': _jnp.float32, 'sb_norm_g': _jnp.float32, 'sb_w_ada': _jnp.float32, 'sb_b_ada': _jnp.float32, 'sb_w_in': _jnp.float32, 'sb_w_out': _jnp.float32, 'final_norm_g': _jnp.float32}
MOMENT_SCALE = {'fox_norm_g': 5.352423e-02, 'fox_w_ada': 5.055871e-02, 'fox_b_ada': 8.203298e-02, 'fox_w_in': 2.089463e-02, 'fox_b_f': 9.590000e-02, 'fox_w_out': 3.458812e-02, 'sb_norm_g': 6.102934e-02, 'sb_w_ada': 5.406168e-02, 'sb_b_ada': 8.676244e-02, 'sb_w_in': 2.144854e-02, 'sb_w_out': 3.819519e-02, 'final_norm_g': 6.393746e+01}


def _to_microbatches(a, axis):
    t = _jnp.moveaxis(a, axis, 0)
    t = t.reshape((N_MICROBATCH, t.shape[0] // N_MICROBATCH) + t.shape[1:])
    return _jnp.moveaxis(t, 1, axis + 1)


def setup_inputs(seed: int = 0) -> dict:
    inp = _fwd_setup_inputs(seed)
    key = _jax.random.fold_in(_jax.random.key(seed), 7919)
    shape, _ = _output_shape()
    out = dict(inp)
    out["loss_target"] = _jax.random.normal(_jax.random.fold_in(key, 0), shape, _jnp.float32)
    for i, name in enumerate(TWIN_WEIGHTS):
        w = inp[name].astype(_jnp.float32)
        if MOMENT_SCALE is None:
            s = _jnp.sqrt(_jnp.mean(_jnp.square(w)) + 1e-30)
        else:
            s = MOMENT_SCALE[name]
        km, kv = _jax.random.split(_jax.random.fold_in(key, i + 1))
        out[name] = w
        out["m_" + name] = s * _jax.random.normal(km, w.shape, _jnp.float32)
        out["v_" + name] = (s * s) * _jax.random.uniform(kv, w.shape, _jnp.float32, 0.5, 1.5)
    if N_MICROBATCH > 1:
        for name, axis in PER_EXAMPLE_BATCH_AXIS.items():
            out[name] = _to_microbatches(out[name], axis)
    return {'x': out['x'], 'c': out['c'], 'fox_norm_g': out['fox_norm_g'], 'fox_w_ada': out['fox_w_ada'], 'fox_b_ada': out['fox_b_ada'], 'fox_w_in': out['fox_w_in'], 'fox_b_f': out['fox_b_f'], 'fox_w_out': out['fox_w_out'], 'sb_norm_g': out['sb_norm_g'], 'sb_w_ada': out['sb_w_ada'], 'sb_b_ada': out['sb_b_ada'], 'sb_w_in': out['sb_w_in'], 'sb_w_out': out['sb_w_out'], 'final_norm_g': out['final_norm_g'], 'loss_target': out['loss_target'], 'm_fox_norm_g': out['m_fox_norm_g'], 'm_fox_w_ada': out['m_fox_w_ada'], 'm_fox_b_ada': out['m_fox_b_ada'], 'm_fox_w_in': out['m_fox_w_in'], 'm_fox_b_f': out['m_fox_b_f'], 'm_fox_w_out': out['m_fox_w_out'], 'm_sb_norm_g': out['m_sb_norm_g'], 'm_sb_w_ada': out['m_sb_w_ada'], 'm_sb_b_ada': out['m_sb_b_ada'], 'm_sb_w_in': out['m_sb_w_in'], 'm_sb_w_out': out['m_sb_w_out'], 'm_final_norm_g': out['m_final_norm_g'], 'v_fox_norm_g': out['v_fox_norm_g'], 'v_fox_w_ada': out['v_fox_w_ada'], 'v_fox_b_ada': out['v_fox_b_ada'], 'v_fox_w_in': out['v_fox_w_in'], 'v_fox_b_f': out['v_fox_b_f'], 'v_fox_w_out': out['v_fox_w_out'], 'v_sb_norm_g': out['v_sb_norm_g'], 'v_sb_w_ada': out['v_sb_w_ada'], 'v_sb_b_ada': out['v_sb_b_ada'], 'v_sb_w_in': out['v_sb_w_in'], 'v_sb_w_out': out['v_sb_w_out'], 'v_final_norm_g': out['v_final_norm_g']}


def _loss(weights, diff, rest, loss_target):
    with _jax.named_scope("forward"):
        args = {**rest, TWIN_DIFF_INPUT: diff, **{k: w.astype(_WEIGHT_DTYPES[k]) for k, w in weights.items()}}
        y = _forward(args)
    with _jax.named_scope("loss_head"):
        err = _jnp.square(y.astype(_jnp.float32) - loss_target)
        return 0.5 * _jnp.sum(_jnp.mean(err, axis=-1)) if err.ndim else 0.5 * err


def _adamw(w, g, m, v):
    m = ADAM_B1 * m + (1.0 - ADAM_B1) * g
    v = ADAM_B2 * v + (1.0 - ADAM_B2) * _jnp.square(g)
    m_hat = m / (1.0 - ADAM_B1 ** ADAM_STEP)
    v_hat = v / (1.0 - ADAM_B2 ** ADAM_STEP)
    delta = -ADAM_LR * (m_hat / (_jnp.sqrt(v_hat) + ADAM_EPS) + ADAM_WD * w)
    return delta, m, v


def reference(x, c, fox_norm_g, fox_w_ada, fox_b_ada, fox_w_in, fox_b_f, fox_w_out, sb_norm_g, sb_w_ada, sb_b_ada, sb_w_in, sb_w_out, final_norm_g, loss_target, m_fox_norm_g, m_fox_w_ada, m_fox_b_ada, m_fox_w_in, m_fox_b_f, m_fox_w_out, m_sb_norm_g, m_sb_w_ada, m_sb_b_ada, m_sb_w_in, m_sb_w_out, m_final_norm_g, v_fox_norm_g, v_fox_w_ada, v_fox_b_ada, v_fox_w_in, v_fox_b_f, v_fox_w_out, v_sb_norm_g, v_sb_w_ada, v_sb_b_ada, v_sb_w_in, v_sb_w_out, v_final_norm_g):
    given = dict(x=x, c=c, fox_norm_g=fox_norm_g, fox_w_ada=fox_w_ada, fox_b_ada=fox_b_ada, fox_w_in=fox_w_in, fox_b_f=fox_b_f, fox_w_out=fox_w_out, sb_norm_g=sb_norm_g, sb_w_ada=sb_w_ada, sb_b_ada=sb_b_ada, sb_w_in=sb_w_in, sb_w_out=sb_w_out, final_norm_g=final_norm_g, loss_target=loss_target, m_fox_norm_g=m_fox_norm_g, m_fox_w_ada=m_fox_w_ada, m_fox_b_ada=m_fox_b_ada, m_fox_w_in=m_fox_w_in, m_fox_b_f=m_fox_b_f, m_fox_w_out=m_fox_w_out, m_sb_norm_g=m_sb_norm_g, m_sb_w_ada=m_sb_w_ada, m_sb_b_ada=m_sb_b_ada, m_sb_w_in=m_sb_w_in, m_sb_w_out=m_sb_w_out, m_final_norm_g=m_final_norm_g, v_fox_norm_g=v_fox_norm_g, v_fox_w_ada=v_fox_w_ada, v_fox_b_ada=v_fox_b_ada, v_fox_w_in=v_fox_w_in, v_fox_b_f=v_fox_b_f, v_fox_w_out=v_fox_w_out, v_sb_norm_g=v_sb_norm_g, v_sb_w_ada=v_sb_w_ada, v_sb_b_ada=v_sb_b_ada, v_sb_w_in=v_sb_w_in, v_sb_w_out=v_sb_w_out, v_final_norm_g=v_final_norm_g)
    weights = {n: given[n] for n in TWIN_WEIGHTS}
    shared = {n: given[n] for n in SHARED_INPUTS}
    per_example = {n: given[n] for n in ['x', 'c']}
    grad_fn = _jax.value_and_grad(_loss, argnums=(0, 1))

    def one_microbatch(ex, loss_target):
        ex = dict(ex)
        diff = ex.pop(TWIN_DIFF_INPUT)
        return grad_fn(weights, diff, {**shared, **ex}, loss_target)

    if N_MICROBATCH == 1:
        loss, (grad_w, grad_x) = one_microbatch(per_example, given["loss_target"])
    else:
        def body(carry, xs):
            loss_sum, grad_sum = carry
            l_k, (gw_k, gx_k) = one_microbatch(xs[0], xs[1])
            with _jax.named_scope("update"):
                return (loss_sum + l_k, _jax.tree.map(_jnp.add, grad_sum, gw_k)), gx_k

        init = (_jnp.zeros((), _jnp.float32), _jax.tree.map(_jnp.zeros_like, weights))
        (loss, grad_w), grad_x = _jax.lax.scan(body, init, (per_example, given["loss_target"]))
    with _jax.named_scope("update"):
        delta_w, new_m, new_v = {}, {}, {}
        for n in TWIN_WEIGHTS:
            delta_w[n], new_m[n], new_v[n] = _adamw(weights[n], grad_w[n], given["m_" + n], given["v_" + n])
    return (loss, grad_x, *[grad_w[n] for n in TWIN_WEIGHTS], *[delta_w[n] for n in TWIN_WEIGHTS],
            *[new_m[n] for n in TWIN_WEIGHTS], *[new_v[n] for n in TWIN_WEIGHTS])
```

```python
import functools

import jax
import jax.numpy as jnp
from jax import lax
from jax.experimental import pallas as pl
from jax.experimental.pallas import tpu as pltpu

F32 = jnp.float32
BF16 = jnp.bfloat16
HEAD_DIM = 64
LANES = 128
AUX_LANE = HEAD_DIM
NORM_EPS = 1e-6
ADAM_LR = 0.001
ADAM_B1 = 0.9
ADAM_B2 = 0.999
ADAM_EPS = 1e-08
ADAM_WD = 0.01
ADAM_STEP = 10
VMEM_LIMIT = 56 * 1024 * 1024
ATT_BLOCK = 256
MESH = pl.DeviceIdType.MESH
HBM = pl.BlockSpec(memory_space=pltpu.HBM)
NT = (((1,), (1,)), ((), ()))
TN = (((0,), (0,)), ((), ()))


def _call(body, **kw):
    return pl.pallas_call(body, **kw)


def _params(**kw):
    return pltpu.CompilerParams(vmem_limit_bytes=VMEM_LIMIT, **kw)


def _tile(dim, pref, mult=128):
    if dim <= pref:
        return dim
    t = (pref // mult) * mult
    while t >= mult:
        if dim % t == 0:
            return t
        t -= mult
    return dim


def _sigmoid(x):
    return 1.0 / (1.0 + jnp.exp(-x))


def _split3(x):
    hi = x.astype(BF16)
    r = x - hi.astype(F32)
    mid = r.astype(BF16)
    lo = (r - mid.astype(F32)).astype(BF16)
    return hi, mid, lo


def _mm(a, b, mode, out_dtype, name, tm=512, tn=512, tk=512):
    if mode == "nn":
        (M, K), (_, N) = a.shape, b.shape
    elif mode == "nt":
        (M, K), (N, _) = a.shape, b.shape
    else:
        (K, M), (_, N) = a.shape, b.shape
    tm, tn, tk = _tile(M, tm), _tile(N, tn), _tile(K, tk)
    nk = K // tk
    dims = {"nn": (((1,), (0,)), ((), ())), "nt": NT, "tn": TN}[mode]

    def body(a_ref, b_ref, o_ref, acc_ref):
        k = pl.program_id(2)

        @pl.when(k == 0)
        def _():
            acc_ref[...] = jnp.zeros_like(acc_ref)

        acc_ref[...] += lax.dot_general(a_ref[...], b_ref[...], dims, preferred_element_type=F32)

        @pl.when(k == nk - 1)
        def _():
            o_ref[...] = acc_ref[...].astype(out_dtype)

    a_spec = (pl.BlockSpec((tk, tm), lambda i, j, k: (k, i)) if mode == "tn"
              else pl.BlockSpec((tm, tk), lambda i, j, k: (i, k)))
    b_spec = (pl.BlockSpec((tn, tk), lambda i, j, k: (j, k)) if mode == "nt"
              else pl.BlockSpec((tk, tn), lambda i, j, k: (k, j)))
    return _call(
        body, name=name, grid=(M // tm, N // tn, nk),
        in_specs=[a_spec, b_spec], out_specs=pl.BlockSpec((tm, tn), lambda i, j, k: (i, j)),
        out_shape=jax.ShapeDtypeStruct((M, N), out_dtype),
        scratch_shapes=[pltpu.VMEM((tm, tn), F32)], compiler_params=_params(),
    )(a, b)


def _mod_fwd(c8, w_ada, b_ada, name):
    D, N = w_ada.shape
    tn = _tile(N, 512)

    def body(c_ref, w_ref, b_ref, o_ref):
        c = c_ref[...]
        sc = (c * _sigmoid(c)).astype(BF16)
        o_ref[...] = jnp.dot(sc, w_ref[...], preferred_element_type=F32) + b_ref[...]

    return _call(
        body, name=name, grid=(N // tn,),
        in_specs=[pl.BlockSpec((8, D), lambda j: (0, 0)), pl.BlockSpec((D, tn), lambda j: (0, j)),
                  pl.BlockSpec((1, tn), lambda j: (0, j))],
        out_specs=pl.BlockSpec((8, tn), lambda j: (0, j)),
        out_shape=jax.ShapeDtypeStruct((8, N), F32), compiler_params=_params(),
    )(c8, w_ada, b_ada)


def _mod_bwd(cT, dmod8, nb, name):
    D = cT.shape[0]
    N = dmod8.shape[1]
    tn = _tile(N, 512)

    def body(c_ref, d_ref, w_ref, b_ref):
        c = c_ref[...]
        sc = c * _sigmoid(c)
        d = d_ref[...]
        acc = sc[:, 0:1] * d[0:1, :]
        bsum = d[0:1, :]
        for b in range(1, nb):
            acc = acc + sc[:, b:b + 1] * d[b:b + 1, :]
            bsum = bsum + d[b:b + 1, :]
        w_ref[...] = acc
        b_ref[...] = bsum

    return _call(
        body, name=name, grid=(N // tn,),
        in_specs=[pl.BlockSpec((D, 8), lambda j: (0, 0)), pl.BlockSpec((8, tn), lambda j: (0, j))],
        out_specs=[pl.BlockSpec((D, tn), lambda j: (0, j)), pl.BlockSpec((1, tn), lambda j: (0, j))],
        out_shape=[jax.ShapeDtypeStruct((D, N), F32), jax.ShapeDtypeStruct((1, N), F32)],
        compiler_params=_params(),
    )(cT, dmod8)


def _ln_proj(x, shift, scale, g, w, S, name):
    T, D = x.shape
    N = w.shape[1]
    tm = _tile(S, 512)
    tn = _tile(N, 1024)
    per_b = S // tm

    def body(x_ref, sh_ref, sc_ref, g_ref, w_ref, p_ref, h_ref):
        @pl.when(pl.program_id(1) == 0)
        def _():
            xv = x_ref[...]
            r = lax.rsqrt(jnp.mean(xv * xv, axis=-1, keepdims=True) + NORM_EPS)
            h = (xv * r) * g_ref[...] * (1.0 + sc_ref[0]) + sh_ref[0]
            h_ref[...] = h.astype(BF16)

        p_ref[...] = jnp.dot(h_ref[...], w_ref[...], preferred_element_type=F32).astype(BF16)

    return _call(
        body, name=name, grid=(T // tm, N // tn),
        in_specs=[pl.BlockSpec((tm, D), lambda i, j: (i, 0)),
                  pl.BlockSpec((1, 1, D), lambda i, j: (i // per_b, 0, 0)),
                  pl.BlockSpec((1, 1, D), lambda i, j: (i // per_b, 0, 0)),
                  pl.BlockSpec((1, D), lambda i, j: (0, 0)),
                  pl.BlockSpec((D, tn), lambda i, j: (0, j))],
        out_specs=[pl.BlockSpec((tm, tn), lambda i, j: (i, j)), pl.BlockSpec((tm, D), lambda i, j: (i, 0))],
        out_shape=[jax.ShapeDtypeStruct((T, N), BF16), jax.ShapeDtypeStruct((T, D), BF16)],
        compiler_params=_params(),
    )(x, shift, scale, g, w)


def _ln_bwd(dh, x, dxo, scale, g, S, name):
    T, D = x.shape
    B = T // S
    tm = _tile(S, 512)
    per_b = S // tm

    def body(dh_ref, x_ref, dxo_ref, sc_ref, g_ref, dx_ref, dsh_ref, dsc_ref, dg_ref):
        i = pl.program_id(0)
        xv = x_ref[...]
        dh_v = dh_ref[...]
        r = lax.rsqrt(jnp.mean(xv * xv, axis=-1, keepdims=True) + NORM_EPS)
        xn = xv * r
        gv = g_ref[...]
        one_sc = 1.0 + sc_ref[0]
        dhxn = dh_v * xn

        @pl.when(i % per_b == 0)
        def _():
            dsh_ref[...] = jnp.zeros_like(dsh_ref)
            dsc_ref[...] = jnp.zeros_like(dsc_ref)

        @pl.when(i == 0)
        def _():
            dg_ref[...] = jnp.zeros_like(dg_ref)

        dsh_ref[0] += jnp.sum(dh_v, axis=0, keepdims=True)
        dsc_ref[0] += jnp.sum(dhxn, axis=0, keepdims=True) * gv
        dg_ref[...] += jnp.sum(dhxn, axis=0, keepdims=True) * one_sc
        dxn = dh_v * (gv * one_sc)
        dx_ref[...] = r * (dxn - xn * jnp.mean(dxn * xn, axis=-1, keepdims=True)) + dxo_ref[...]

    row = pl.BlockSpec((tm, D), lambda i: (i, 0))
    per = pl.BlockSpec((1, 1, D), lambda i: (i // per_b, 0, 0))
    vec = pl.BlockSpec((1, D), lambda i: (0, 0))
    return _call(
        body, name=name, grid=(T // tm,),
        in_specs=[row, row, row, per, vec], out_specs=[row, per, per, vec],
        out_shape=[jax.ShapeDtypeStruct((T, D), F32), jax.ShapeDtypeStruct((B, 1, D), F32),
                   jax.ShapeDtypeStruct((B, 1, D), F32), jax.ShapeDtypeStruct((1, D), F32)],
        compiler_params=_params(),
    )(dh, x, dxo, scale, g)


def _gate_out(o, proj, w_out, x, gate, S, name):
    T, DI = o.shape
    D = w_out.shape[1]
    tm = _tile(S, 256)
    per_b = S // tm

    def body(o_ref, z_ref, w_ref, x_ref, g_ref, xo_ref, y_ref, u_ref):
        z = z_ref[...].astype(F32)
        u = (o_ref[...] * (z * _sigmoid(z))).astype(BF16)
        u_ref[...] = u
        y = jnp.dot(u, w_ref[...], preferred_element_type=F32)
        y_ref[...] = y
        xo_ref[...] = x_ref[...] + g_ref[0] * y

    wide = pl.BlockSpec((tm, DI), lambda i: (i, 0))
    row = pl.BlockSpec((tm, D), lambda i: (i, 0))
    return _call(
        body, name=name, grid=(T // tm,),
        in_specs=[wide, pl.BlockSpec((tm, DI), lambda i: (i, 3)), pl.BlockSpec((DI, D), lambda i: (0, 0)), row,
                  pl.BlockSpec((1, 1, D), lambda i: (i // per_b, 0, 0))],
        out_specs=[row, row, wide],
        out_shape=[jax.ShapeDtypeStruct((T, D), F32), jax.ShapeDtypeStruct((T, D), F32),
                   jax.ShapeDtypeStruct((T, DI), BF16)],
        compiler_params=_params(),
    )(o, proj, w_out, x, gate)


def _out_bwd(dxo, y, gate, w_out, o, proj, S, name):
    T, D = dxo.shape
    DI = o.shape[1]
    B = T // S
    tm = _tile(S, 256)
    per_b = S // tm

    def body(dxo_ref, y_ref, g_ref, w_ref, o_ref, z_ref, dy_ref, do_ref, dz_ref, dg_ref):
        dxo_v = dxo_ref[...]
        dy = (dxo_v * g_ref[0]).astype(BF16)
        dy_ref[...] = dy
        du = lax.dot_general(dy, w_ref[...], NT, preferred_element_type=F32)
        z = z_ref[...].astype(F32)
        sg = _sigmoid(z)
        do_ref[...] = (du * (z * sg)).astype(BF16)
        dz_ref[...] = (du * o_ref[...] * (sg * (1.0 + z * (1.0 - sg)))).astype(BF16)

        @pl.when(pl.program_id(0) % per_b == 0)
        def _():
            dg_ref[...] = jnp.zeros_like(dg_ref)

        dg_ref[0] += jnp.sum(dxo_v * y_ref[...], axis=0, keepdims=True)

    wide = pl.BlockSpec((tm, DI), lambda i: (i, 0))
    row = pl.BlockSpec((tm, D), lambda i: (i, 0))
    per = pl.BlockSpec((1, 1, D), lambda i: (i // per_b, 0, 0))
    return _call(
        body, name=name, grid=(T // tm,),
        in_specs=[row, row, per, pl.BlockSpec((DI, D), lambda i: (0, 0)), wide,
                  pl.BlockSpec((tm, DI), lambda i: (i, 3))],
        out_specs=[row, wide, wide, per],
        out_shape=[jax.ShapeDtypeStruct((T, D), BF16), jax.ShapeDtypeStruct((T, DI), BF16),
                   jax.ShapeDtypeStruct((T, DI), BF16), jax.ShapeDtypeStruct((B, 1, D), F32)],
        compiler_params=_params(),
    )(dxo, y, gate, w_out, o, proj)


def _final_loss(x, tgt, g, S, name):
    T, D = x.shape
    tm = _tile(S, 512)

    def body(x_ref, t_ref, g_ref, dx_ref, dg_ref, l_ref):
        @pl.when(pl.program_id(0) == 0)
        def _():
            dg_ref[...] = jnp.zeros_like(dg_ref)
            l_ref[...] = jnp.zeros_like(l_ref)

        xv = x_ref[...]
        gv = g_ref[...]
        r = lax.rsqrt(jnp.mean(xv * xv, axis=-1, keepdims=True) + NORM_EPS)
        xn = xv * r
        e = xn * gv - t_ref[...]
        part = jnp.sum(jnp.sum(e * e, axis=0, keepdims=True), axis=1, keepdims=True)
        l_ref[...] += (0.5 / D) * part
        dy = e * (1.0 / D)
        dg_ref[...] += jnp.sum(dy * xn, axis=0, keepdims=True)
        dxn = dy * gv
        dx_ref[...] = r * (dxn - xn * jnp.mean(dxn * xn, axis=-1, keepdims=True))

    row = pl.BlockSpec((tm, D), lambda i: (i, 0))
    return _call(
        body, name=name, grid=(T // tm,),
        in_specs=[row, row, pl.BlockSpec((1, D), lambda i: (0, 0))],
        out_specs=[row, pl.BlockSpec((1, D), lambda i: (0, 0)), pl.BlockSpec((1, LANES), lambda i: (0, 0))],
        out_shape=[jax.ShapeDtypeStruct((T, D), F32), jax.ShapeDtypeStruct((1, D), F32),
                   jax.ShapeDtypeStruct((1, LANES), F32)],
        compiler_params=_params(),
    )(x, tgt, g)


def _cum_fwd(fl, bf, name):
    B, S, _ = fl.shape
    ch = _tile(S, 256, 8)

    def body(fl_ref, b_ref, cum_ref):
        ri = lax.broadcasted_iota(jnp.int32, (ch, ch), 0)
        ci = lax.broadcasted_iota(jnp.int32, (ch, ch), 1)
        tri = jnp.where(ri >= ci, 1.0, 0.0).astype(BF16)

        def step(i, carry):
            r0 = pl.multiple_of(i * ch, ch)
            z = fl_ref[0, pl.ds(r0, ch), :] + b_ref[...]
            lf = jnp.minimum(z, 0.0) - jnp.log(1.0 + jnp.exp(-jnp.abs(z)))
            hi, mid, lo = _split3(lf)
            cs = (jnp.dot(tri, hi, preferred_element_type=F32) + jnp.dot(tri, mid, preferred_element_type=F32)
                  + jnp.dot(tri, lo, preferred_element_type=F32)) + carry
            cum_ref[0, pl.ds(r0, ch), :] = cs
            return cs[ch - 1:ch, :]

        lax.fori_loop(0, S // ch, step, jnp.zeros((1, LANES), F32))

    blk = pl.BlockSpec((1, S, LANES), lambda b: (b, 0, 0))
    return _call(
        body, name=name, grid=(B,), in_specs=[blk, pl.BlockSpec((1, LANES), lambda b: (0, 0))], out_specs=blk,
        out_shape=jax.ShapeDtypeStruct((B, S, LANES), F32), compiler_params=_params(),
    )(fl, bf)


def _cum_bwd(dcs, fl, bf, name):
    B, S, _ = fl.shape
    ch = _tile(S, 256, 8)
    n = S // ch

    def body(d_ref, fl_ref, b_ref, o_ref, db_ref):
        ri = lax.broadcasted_iota(jnp.int32, (ch, ch), 0)
        ci = lax.broadcasted_iota(jnp.int32, (ch, ch), 1)
        tri = jnp.where(ci >= ri, 1.0, 0.0).astype(BF16)

        @pl.when(pl.program_id(0) == 0)
        def _():
            db_ref[...] = jnp.zeros_like(db_ref)

        def step(t, carry):
            tail, dbsum = carry
            r0 = pl.multiple_of((n - 1 - t) * ch, ch)
            hi, mid, lo = _split3(d_ref[0, pl.ds(r0, ch), :])
            suf = (jnp.dot(tri, hi, preferred_element_type=F32) + jnp.dot(tri, mid, preferred_element_type=F32)
                   + jnp.dot(tri, lo, preferred_element_type=F32)) + tail
            z = fl_ref[0, pl.ds(r0, ch), :] + b_ref[...]
            dfl = -suf * _sigmoid(-z)
            o_ref[0, pl.ds(r0, ch), :] = dfl
            return suf[0:1, :], dbsum + jnp.sum(dfl, axis=0, keepdims=True)

        z1 = jnp.zeros((1, LANES), F32)
        _, dbsum = lax.fori_loop(0, n, step, (z1, z1))
        db_ref[...] += dbsum

    blk = pl.BlockSpec((1, S, LANES), lambda b: (b, 0, 0))
    vec = pl.BlockSpec((1, LANES), lambda b: (0, 0))
    return _call(
        body, name=name, grid=(B,), in_specs=[blk, blk, vec], out_specs=[blk, vec],
        out_shape=[jax.ShapeDtypeStruct((B, S, LANES), F32), jax.ShapeDtypeStruct((1, LANES), F32)],
        compiler_params=_params(),
    )(dcs, fl, bf)


def _head_spec(S):
    return pl.BlockSpec((1, 1, S, LANES), lambda b, h: (b, h, 0, 0))


def _aux_col(x):
    lane = lax.broadcasted_iota(jnp.int32, x.shape, 1)
    return jnp.sum(jnp.where(lane == AUX_LANE, x, 0.0), axis=1, keepdims=True)


def _fox_fwd(qa, ka, va, cumcol, cumrow, name):
    B, H, S, _ = qa.shape
    tq = cumrow.shape[-1]
    nq = S // tq

    def body(q_ref, k_ref, v_ref, cc_ref, cr_ref, o_ref):
        h = pl.program_id(1)
        lane = lax.broadcasted_iota(jnp.int32, (tq, LANES), 1)
        ri = lax.broadcasted_iota(jnp.int32, (tq, tq), 0)
        ci = lax.broadcasted_iota(jnp.int32, (tq, tq), 1)

        def qloop(qi, _):
            r0 = pl.multiple_of(qi * tq, tq)
            q = q_ref[0, 0, pl.ds(r0, tq), :]
            ccol = jnp.sum(jnp.where(lane == h, cc_ref[0, pl.ds(r0, tq), :], 0.0), axis=1, keepdims=True)

            def kstep(kj, carry, masked):
                m, l, acc = carry
                c0 = pl.multiple_of(kj * tq, tq)
                s = lax.dot_general(q, k_ref[0, 0, pl.ds(c0, tq), :], NT, preferred_element_type=F32)
                s = s + ccol - cr_ref[0, 0, kj]
                if masked:
                    s = jnp.where(ci <= ri, s, -jnp.inf)
                m_new = jnp.maximum(m, jnp.max(s, axis=1, keepdims=True))
                alpha = jnp.exp(m - m_new)
                p = jnp.exp(s - m_new)
                l = alpha * l + jnp.sum(p, axis=1, keepdims=True)
                acc = alpha * acc + _tri_right(p, v_ref[0, 0, pl.ds(c0, tq), :])
                return m_new, l, acc

            init = (jnp.full((tq, 1), -jnp.inf, F32), jnp.zeros((tq, 1), F32), jnp.zeros((tq, LANES), F32))
            carry = lax.fori_loop(0, qi, lambda kj, c: kstep(kj, c, False), init)
            m, l, acc = kstep(qi, carry, True)
            o_ref[0, 0, pl.ds(r0, tq), :] = jnp.where(lane < HEAD_DIM, acc / l, m + jnp.log(l))
            return 0

        lax.fori_loop(0, nq, qloop, 0)

    hs = _head_spec(S)
    return _call(
        body, name=name, grid=(B, H),
        in_specs=[hs, hs, hs, pl.BlockSpec((1, S, LANES), lambda b, h: (b, 0, 0)),
                  pl.BlockSpec((1, 1, nq, 1, tq), lambda b, h: (b, h, 0, 0, 0))],
        out_specs=hs, out_shape=jax.ShapeDtypeStruct((B, H, S, LANES), F32), compiler_params=_params(),
    )(qa, ka, va, cumcol, cumrow)


def _fox_bwd(qa, ka, va, doa, ol, cumcol, cumrow, name):
    B, H, S, _ = qa.shape
    tq = cumrow.shape[-1]
    nq = S // tq

    def body(q_ref, k_ref, v_ref, do_ref, ol_ref, cc_ref, cr_ref, dq_ref, dk_ref, dv_ref, dcs_ref):
        h = pl.program_id(1)
        lane = lax.broadcasted_iota(jnp.int32, (tq, LANES), 1)
        ri = lax.broadcasted_iota(jnp.int32, (tq, tq), 0)
        ci = lax.broadcasted_iota(jnp.int32, (tq, tq), 1)
        dk_ref[...] = jnp.zeros_like(dk_ref)
        dv_ref[...] = jnp.zeros_like(dv_ref)
        dcs_ref[...] = jnp.zeros_like(dcs_ref)

        def qloop(qi, _):
            r0 = pl.multiple_of(qi * tq, tq)
            q = q_ref[0, 0, pl.ds(r0, tq), :]
            do = do_ref[0, 0, pl.ds(r0, tq), :]
            olb = ol_ref[0, 0, pl.ds(r0, tq), :]
            lse = _aux_col(olb)
            delta = jnp.sum(olb * do.astype(F32), axis=1, keepdims=True)
            ccol = jnp.sum(jnp.where(lane == h, cc_ref[0, pl.ds(r0, tq), :], 0.0), axis=1, keepdims=True)

            def kstep(kj, dq, masked):
                c0 = pl.multiple_of(kj * tq, tq)
                k = k_ref[0, 0, pl.ds(c0, tq), :]
                s = lax.dot_general(q, k, NT, preferred_element_type=F32)
                s = s + ccol - cr_ref[0, 0, kj]
                if masked:
                    s = jnp.where(ci <= ri, s, -jnp.inf)
                p = jnp.exp(s - lse)
                dp = lax.dot_general(do, v_ref[0, 0, pl.ds(c0, tq), :], NT, preferred_element_type=F32)
                ds = p * (dp - delta)
                dcs_ref[0, 0, kj] += jnp.sum(ds, axis=0, keepdims=True)
                ds = ds.astype(BF16)
                dv_ref[0, 0, pl.ds(c0, tq), :] += lax.dot_general(p.astype(BF16), do, TN,
                                                                  preferred_element_type=F32)
                dk_ref[0, 0, pl.ds(c0, tq), :] += lax.dot_general(ds, q, TN, preferred_element_type=F32)
                return dq + jnp.dot(ds, k, preferred_element_type=F32)

            dq = lax.fori_loop(0, qi, lambda kj, c: kstep(kj, c, False), jnp.zeros((tq, LANES), F32))
            dq_ref[0, 0, pl.ds(r0, tq), :] = kstep(qi, dq, True)
            return 0

        lax.fori_loop(0, nq, qloop, 0)

    hs = _head_spec(S)
    out = jax.ShapeDtypeStruct((B, H, S, LANES), F32)
    return _call(
        body, name=name, grid=(B, H),
        in_specs=[hs, hs, hs, hs, hs, pl.BlockSpec((1, S, LANES), lambda b, h: (b, 0, 0)),
                  pl.BlockSpec((1, 1, nq, 1, tq), lambda b, h: (b, h, 0, 0, 0))],
        out_specs=[hs, hs, hs, pl.BlockSpec((1, 1, nq, 1, tq), lambda b, h: (b, h, 0, 0, 0))],
        out_shape=[out, out, out, jax.ShapeDtypeStruct((B, H, nq, 1, tq), F32)], compiler_params=_params(),
    )(qa, ka, va, doa, ol, cumcol, cumrow)


def _softplus_parts(z):
    e = jnp.exp(-jnp.abs(z))
    return jnp.maximum(z, 0.0) + jnp.log(1.0 + e), e


def _tri_right(x, tri):
    hi = x.astype(BF16)
    lo = (x - hi.astype(F32)).astype(BF16)
    return jnp.dot(hi, tri, preferred_element_type=F32) + jnp.dot(lo, tri, preferred_element_type=F32)


def _sb_fwd(qa, ka, va, tq, name):
    B, H, S, _ = qa.shape
    nq = S // tq

    def body(q_ref, k_ref, v_ref, o_ref):
        lane = lax.broadcasted_iota(jnp.int32, (tq, LANES), 1)
        ri = lax.broadcasted_iota(jnp.int32, (tq, tq), 0)
        ci = lax.broadcasted_iota(jnp.int32, (tq, tq), 1)
        strict = ci < ri
        above = jnp.where(ri > ci, 1.0, 0.0).astype(BF16)

        def qloop(qi, _):
            r0 = pl.multiple_of(qi * tq, tq)
            q = q_ref[0, 0, pl.ds(r0, tq), :]

            def kstep(kj, carry, masked):
                c, acc = carry
                c0 = pl.multiple_of(kj * tq, tq)
                z = lax.dot_general(q, k_ref[0, 0, pl.ds(c0, tq), :], NT, preferred_element_type=F32)
                sp, _ = _softplus_parts(z)
                lk = -sp
                if masked:
                    lk = jnp.where(strict, lk, 0.0)
                a = jnp.exp((z - sp) + _tri_right(lk, above) + c)
                if masked:
                    a = jnp.where(strict, a, 0.0)
                acc = acc + jnp.dot(a.astype(BF16), v_ref[0, 0, pl.ds(c0, tq), :], preferred_element_type=F32)
                return c + jnp.sum(lk, axis=1, keepdims=True), acc

            carry = kstep(qi, (jnp.zeros((tq, 1), F32), jnp.zeros((tq, LANES), F32)), True)
            c, acc = lax.fori_loop(0, qi, lambda t, cr: kstep(qi - 1 - t, cr, False), carry)
            o_ref[0, 0, pl.ds(r0, tq), :] = jnp.where(lane < HEAD_DIM, acc, c)
            return 0

        lax.fori_loop(0, nq, qloop, 0)

    hs = _head_spec(S)
    return _call(
        body, name=name, grid=(B, H), in_specs=[hs, hs, hs], out_specs=hs,
        out_shape=jax.ShapeDtypeStruct((B, H, S, LANES), F32), compiler_params=_params(),
    )(qa, ka, va)


def _sb_bwd(qa, ka, va, doa, ol, tq, name):
    B, H, S, _ = qa.shape
    nq = S // tq

    def body(q_ref, k_ref, v_ref, do_ref, ol_ref, dq_ref, dk_ref, dv_ref):
        ri = lax.broadcasted_iota(jnp.int32, (tq, tq), 0)
        ci = lax.broadcasted_iota(jnp.int32, (tq, tq), 1)
        strict = ci < ri
        upto = jnp.where(ri <= ci, 1.0, 0.0).astype(BF16)
        before = jnp.where(ri < ci, 1.0, 0.0).astype(BF16)
        dk_ref[...] = jnp.zeros_like(dk_ref)
        dv_ref[...] = jnp.zeros_like(dv_ref)

        def qloop(qi, _):
            r0 = pl.multiple_of(qi * tq, tq)
            q = q_ref[0, 0, pl.ds(r0, tq), :]
            do = do_ref[0, 0, pl.ds(r0, tq), :]
            tot = _aux_col(ol_ref[0, 0, pl.ds(r0, tq), :])

            def kstep(kj, carry, masked):
                cpre, pg, dq = carry
                c0 = pl.multiple_of(kj * tq, tq)
                k = k_ref[0, 0, pl.ds(c0, tq), :]
                z = lax.dot_general(q, k, NT, preferred_element_type=F32)
                sp, e = _softplus_parts(z)
                inv = 1.0 / (1.0 + e)
                sig = jnp.where(z >= 0.0, inv, e * inv)
                lk = -sp
                if masked:
                    lk = jnp.where(strict, lk, 0.0)
                a = jnp.exp((z - sp) + (tot - (cpre + _tri_right(lk, upto))))
                if masked:
                    a = jnp.where(strict, a, 0.0)
                g = lax.dot_general(do, v_ref[0, 0, pl.ds(c0, tq), :], NT, preferred_element_type=F32) * a
                pfx = jnp.dot(g.astype(BF16), before, preferred_element_type=F32) + pg
                dz = g * (1.0 - sig) - pfx * sig
                if masked:
                    dz = jnp.where(strict, dz, 0.0)
                dz = dz.astype(BF16)
                dv_ref[0, 0, pl.ds(c0, tq), :] += lax.dot_general(a.astype(BF16), do, TN,
                                                                  preferred_element_type=F32)
                dk_ref[0, 0, pl.ds(c0, tq), :] += lax.dot_general(dz, q, TN, preferred_element_type=F32)
                return (cpre + jnp.sum(lk, axis=1, keepdims=True), pg + jnp.sum(g, axis=1, keepdims=True),
                        dq + jnp.dot(dz, k, preferred_element_type=F32))

            z1 = jnp.zeros((tq, 1), F32)
            carry = lax.fori_loop(0, qi, lambda kj, cr: kstep(kj, cr, False), (z1, z1, jnp.zeros((tq, LANES), F32)))
            dq_ref[0, 0, pl.ds(r0, tq), :] = kstep(qi, carry, True)[2]
            return 0

        lax.fori_loop(0, nq, qloop, 0)

    hs = _head_spec(S)
    out = jax.ShapeDtypeStruct((B, H, S, LANES), F32)
    return _call(
        body, name=name, grid=(B, H), in_specs=[hs, hs, hs, hs, hs], out_specs=[hs, hs, hs],
        out_shape=[out, out, out], compiler_params=_params(),
    )(qa, ka, va, doa, ol)


def _row_tile(R, C, n_arrays):
    budget = 24 * 1024 * 1024 // (2 * n_arrays * 4 * max(C, LANES))
    return _tile(R, max(8, budget), 8)


def _ew_sum(parts, name):
    R, C = parts[0].shape
    tr = _row_tile(R, C, len(parts) + 1)

    def body(*refs):
        acc = refs[0][...] + refs[1][...]
        for r in refs[2:-1]:
            acc = acc + r[...]
        refs[-1][...] = acc

    blk = pl.BlockSpec((tr, C), lambda i: (i, 0))
    return _call(
        body, name=name, grid=(R // tr,), in_specs=[blk] * len(parts), out_specs=blk,
        out_shape=jax.ShapeDtypeStruct((R, C), F32), compiler_params=_params(),
    )(*parts)


def _adamw(w, g, m, v, name):
    R, C = w.shape
    tr = _row_tile(R, C, 7)
    c1 = 1.0 / (1.0 - ADAM_B1 ** ADAM_STEP)
    c2 = 1.0 / (1.0 - ADAM_B2 ** ADAM_STEP)

    def body(w_ref, g_ref, m_ref, v_ref, d_ref, m2_ref, v2_ref):
        gv = g_ref[...]
        m2 = ADAM_B1 * m_ref[...] + (1.0 - ADAM_B1) * gv
        v2 = ADAM_B2 * v_ref[...] + (1.0 - ADAM_B2) * (gv * gv)
        m2_ref[...] = m2
        v2_ref[...] = v2
        d_ref[...] = -ADAM_LR * ((m2 * c1) / (jnp.sqrt(v2 * c2) + ADAM_EPS) + ADAM_WD * w_ref[...])

    blk = pl.BlockSpec((tr, C), lambda i: (i, 0))
    out = jax.ShapeDtypeStruct((R, C), F32)
    return _call(
        body, name=name, grid=(R // tr,), in_specs=[blk] * 4, out_specs=[blk] * 3, out_shape=[out] * 3,
        compiler_params=_params(),
    )(w, g, m, v)


def _me():
    return lax.axis_index("x"), lax.axis_index("y"), lax.axis_index("c")


def _chip_of(x, y):
    return 2 * x + y


def _other_chips(x, y):
    return [(x, 1 - y), (1 - x, y), (1 - x, 1 - y)]


def _gather_weights(halves, smalls):
    nh, ns = len(halves), len(smalls)

    def body(*refs):
        ins_h, ins_s = refs[:nh], refs[nh:nh + ns]
        outs_h, outs_s = refs[nh + ns:2 * nh + ns], refs[2 * nh + ns:2 * (nh + ns)]
        send1, recv1, send2, recv2, send3, recv3, loc = refs[2 * (nh + ns):]
        x, y, c = _me()
        mine = _chip_of(x, y)
        chips = _other_chips(x, y)
        sib = (x, y, 1 - c)
        local = []
        for i in range(nh):
            cp = pltpu.make_async_copy(ins_h[i], outs_h[i].at[mine], loc.at[i])
            cp.start()
            local.append(cp)
        for i in range(ns):
            cp = pltpu.make_async_copy(ins_s[i], outs_s[i].at[mine], loc.at[nh + i])
            cp.start()
            local.append(cp)

        def first(i, k):
            return pltpu.make_async_remote_copy(
                src_ref=ins_h[i].at[c], dst_ref=outs_h[i].at[mine, c], send_sem=send1.at[i, k], recv_sem=recv1.at[i, k],
                device_id=(*chips[k], c), device_id_type=MESH)

        def landed(i, k, half):
            return outs_h[i].at[_chip_of(*chips[k]), half]

        def passed(i, k):
            return pltpu.make_async_remote_copy(
                src_ref=landed(i, k, c), dst_ref=landed(i, k, c), send_sem=send2.at[i, k], recv_sem=recv2.at[i, k],
                device_id=sib, device_id_type=MESH)

        def small(i, k):
            return pltpu.make_async_remote_copy(
                src_ref=ins_s[i], dst_ref=outs_s[i].at[mine], send_sem=send3.at[i, k], recv_sem=recv3.at[i, k],
                device_id=(*chips[k], c), device_id_type=MESH)

        for i in range(nh):
            for k in range(3):
                first(i, k).start()
        for i in range(ns):
            for k in range(3):
                small(i, k).start()
        for i in range(nh):
            for k in range(3):
                pltpu.make_async_remote_copy(
                    src_ref=ins_h[i].at[c], dst_ref=landed(i, k, c), send_sem=send1.at[i, k], recv_sem=recv1.at[i, k],
                    device_id=(*chips[k], c), device_id_type=MESH).wait_recv()
                passed(i, k).start()
        for i in range(nh):
            for k in range(3):
                pltpu.make_async_remote_copy(
                    src_ref=landed(i, k, c), dst_ref=landed(i, k, 1 - c), send_sem=send2.at[i, k],
                    recv_sem=recv2.at[i, k], device_id=sib, device_id_type=MESH).wait_recv()
        for i in range(ns):
            for k in range(3):
                pltpu.make_async_remote_copy(
                    src_ref=ins_s[i], dst_ref=outs_s[i].at[_chip_of(*chips[k])], send_sem=send3.at[i, k],
                    recv_sem=recv3.at[i, k], device_id=(*chips[k], c), device_id_type=MESH).wait_recv()
        for i in range(nh):
            for k in range(3):
                first(i, k).wait_send()
                passed(i, k).wait_send()
        for i in range(ns):
            for k in range(3):
                small(i, k).wait_send()
        for cp in local:
            cp.wait()

    out_shape = ([jax.ShapeDtypeStruct((4,) + a.shape, a.dtype) for a in halves]
                 + [jax.ShapeDtypeStruct((4,) + a.shape, a.dtype) for a in smalls])
    n = nh + ns
    res = _call(
        body, name="gather_weights", in_specs=[HBM] * n, out_specs=[HBM] * n, out_shape=out_shape,
        scratch_shapes=[pltpu.SemaphoreType.DMA((nh, 3)), pltpu.SemaphoreType.DMA((nh, 3)),
                        pltpu.SemaphoreType.DMA((nh, 3)), pltpu.SemaphoreType.DMA((nh, 3)),
                        pltpu.SemaphoreType.DMA((max(ns, 1), 3)), pltpu.SemaphoreType.DMA((max(ns, 1), 3)),
                        pltpu.SemaphoreType.DMA((n,))],
        compiler_params=_params(),
    )(*halves, *smalls)
    return res[:nh], res[nh:]


def _pair_exchange(grads):
    n = len(grads)

    def body(*refs):
        ins, own, got = refs[:n], refs[n:2 * n], refs[2 * n:3 * n]
        send, recv, loc = refs[3 * n:]
        x, y, c = _me()
        sib = (x, y, 1 - c)
        cps = []
        for i in range(n):
            for j in range(4):
                r = pltpu.make_async_remote_copy(
                    src_ref=ins[i].at[j, 1 - c], dst_ref=got[i].at[j], send_sem=send.at[i, j], recv_sem=recv.at[i, j],
                    device_id=sib, device_id_type=MESH)
                r.start()
                l = pltpu.make_async_copy(ins[i].at[j, c], own[i].at[j], loc.at[i, j])
                l.start()
                cps.append((r, l))
        for r, l in cps:
            r.wait()
            l.wait()

    shapes = [jax.ShapeDtypeStruct((4,) + g.shape[2:], g.dtype) for g in grads]
    res = _call(
        body, name="grad_pair_exchange", in_specs=[HBM] * n, out_specs=[HBM] * (2 * n), out_shape=shapes + shapes,
        scratch_shapes=[pltpu.SemaphoreType.DMA((n, 4)), pltpu.SemaphoreType.DMA((n, 4)),
                        pltpu.SemaphoreType.DMA((n, 4))],
        compiler_params=_params(),
    )(*grads)
    return res[:n], res[n:]


def _chip_exchange(sums):
    n = len(sums)

    def body(*refs):
        ins, own, got = refs[:n], refs[n:2 * n], refs[2 * n:3 * n]
        send, recv, loc = refs[3 * n:]
        x, y, c = _me()
        mine = _chip_of(x, y)
        chips = _other_chips(x, y)
        cps = []
        for i in range(n):
            l = pltpu.make_async_copy(ins[i].at[mine], own[i], loc.at[i])
            l.start()
            cps.append(l)
            for k in range(3):
                r = pltpu.make_async_remote_copy(
                    src_ref=ins[i].at[_chip_of(*chips[k])], dst_ref=got[i].at[k], send_sem=send.at[i, k],
                    recv_sem=recv.at[i, k], device_id=(*chips[k], c), device_id_type=MESH)
                r.start()
                cps.append(r)
        for cp in cps:
            cp.wait()

    res = _call(
        body, name="grad_chip_exchange", in_specs=[HBM] * n, out_specs=[HBM] * (2 * n),
        out_shape=([jax.ShapeDtypeStruct(s.shape[1:], s.dtype) for s in sums]
                   + [jax.ShapeDtypeStruct((3,) + s.shape[1:], s.dtype) for s in sums]),
        scratch_shapes=[pltpu.SemaphoreType.DMA((n, 3)), pltpu.SemaphoreType.DMA((n, 3)), pltpu.SemaphoreType.DMA((n,))],
        compiler_params=_params(),
    )(*sums)
    return res[:n], res[n:]


def _pair_share(halves):
    n = len(halves)

    def body(*refs):
        ins, outs = refs[:n], refs[n:2 * n]
        send, recv, loc = refs[2 * n:]
        x, y, c = _me()
        cps = []
        for i in range(n):
            r = pltpu.make_async_remote_copy(
                src_ref=ins[i], dst_ref=outs[i].at[c], send_sem=send.at[i], recv_sem=recv.at[i],
                device_id=(x, y, 1 - c), device_id_type=MESH)
            r.start()
            l = pltpu.make_async_copy(ins[i], outs[i].at[c], loc.at[i])
            l.start()
            cps += [r, l]
        for i in range(n):
            pltpu.make_async_remote_copy(
                src_ref=ins[i], dst_ref=outs[i].at[1 - c], send_sem=send.at[i], recv_sem=recv.at[i],
                device_id=(x, y, 1 - c), device_id_type=MESH).wait_recv()
            cps[2 * i].wait_send()
            cps[2 * i + 1].wait()

    return _call(
        body, name="grad_pair_share", in_specs=[HBM] * n, out_specs=[HBM] * n,
        out_shape=[jax.ShapeDtypeStruct((2,) + h.shape, h.dtype) for h in halves],
        scratch_shapes=[pltpu.SemaphoreType.DMA((n,)), pltpu.SemaphoreType.DMA((n,)), pltpu.SemaphoreType.DMA((n,))],
        compiler_params=_params(),
    )(*halves)


def _allreduce_small(vec):
    P = vec.shape[1]

    def body(v_ref, sum_ref, all_ref, send, recv):
        x, y, c = _me()
        me = 4 * x + 2 * y + c
        all_ref[pl.ds(me, 1)] = v_ref[...][None]
        cps = []
        for d in range(1, 8):
            peer = (jnp.bitwise_xor(x, d >> 2), jnp.bitwise_xor(y, (d >> 1) & 1), jnp.bitwise_xor(c, d & 1))
            r = pltpu.make_async_remote_copy(
                src_ref=v_ref, dst_ref=all_ref.at[me], send_sem=send.at[d - 1], recv_sem=recv.at[d - 1],
                device_id=peer, device_id_type=MESH)
            r.start()
            cps.append(r)
        for d in range(1, 8):
            src = jnp.bitwise_xor(me, d)
            pltpu.make_async_remote_copy(
                src_ref=v_ref, dst_ref=all_ref.at[src], send_sem=send.at[d - 1], recv_sem=recv.at[d - 1],
                device_id=(x, y, c), device_id_type=MESH).wait_recv()
        for r in cps:
            r.wait_send()
        acc = all_ref[0]
        for i in range(1, 8):
            acc = acc + all_ref[i]
        sum_ref[...] = acc

    vm = pl.BlockSpec(memory_space=pltpu.VMEM)
    return _call(
        body, name="allreduce_small", in_specs=[vm], out_specs=[vm, vm],
        out_shape=[jax.ShapeDtypeStruct((8, P), F32), jax.ShapeDtypeStruct((8, 8, P), F32)],
        scratch_shapes=[pltpu.SemaphoreType.DMA((7,)), pltpu.SemaphoreType.DMA((7,))],
        compiler_params=_params(),
    )(vec)[0]


def _to_heads(t, B, S):
    H = t.shape[1] // HEAD_DIM
    t = t.astype(BF16).reshape(B, S, H, HEAD_DIM).transpose(0, 2, 1, 3)
    pad = jnp.zeros((B, H, S, LANES - HEAD_DIM), BF16)
    return jnp.concatenate([t, pad], axis=-1)


def _from_heads(t):
    B, H, S, _ = t.shape
    return t[..., :HEAD_DIM].transpose(0, 2, 1, 3).reshape(B * S, H * HEAD_DIM)


def _per_batch(mod, B, D):
    return [mod[:B, i * D:(i + 1) * D].reshape(B, 1, D) for i in range(3)]


def _pad_rows8(a):
    return jnp.concatenate([a, jnp.zeros((8 - a.shape[0],) + a.shape[1:], a.dtype)], axis=0)


def _layer_fwd(x, c8, w, S, fox, tag):
    T, D = x.shape
    B = T // S
    DI = w["w_out"].shape[0]
    H = DI // HEAD_DIM
    tq = _tile(S, ATT_BLOCK, 8)
    mod = _mod_fwd(c8, w["w_ada"], w["b_ada"], tag + "_mod_fwd")
    shift, scale, gate = _per_batch(mod, B, D)
    proj, h = _ln_proj(x, shift, scale, w["norm_g"], w["w_in"], S, tag + "_ln_proj")
    qa = _to_heads(proj[:, :DI] * 0.125, B, S)
    ka = _to_heads(proj[:, DI:2 * DI], B, S)
    va = _to_heads(proj[:, 2 * DI:3 * DI], B, S)
    saved = dict(x=x, h=h, proj=proj, qa=qa, ka=ka, va=va, scale=scale, gate=gate)
    if fox:
        fl = _mm(h, w["w_f"], "nn", F32, tag + "_flogit").reshape(B, S, LANES)
        cum = _cum_fwd(fl, w["b_f"], tag + "_cum_fwd")
        cumrow = cum[:, :, :H].transpose(0, 2, 1).reshape(B, H, S // tq, 1, tq)
        ol = _fox_fwd(qa, ka, va, cum, cumrow, tag + "_attn_fwd")
        saved.update(fl=fl, cum=cum, cumrow=cumrow)
    else:
        ol = _sb_fwd(qa, ka, va, tq, tag + "_attn_fwd")
    o = _from_heads(ol)
    xo, y, u = _gate_out(o, proj, w["w_out"], x, gate, S, tag + "_gate_out")
    saved.update(ol=ol, o=o, y=y, u=u)
    return xo, saved


def _layer_bwd(dxo, sv, w, cT, S, fox, tag):
    T, D = dxo.shape
    B = T // S
    DI = w["w_out"].shape[0]
    H = DI // HEAD_DIM
    tq = _tile(S, ATT_BLOCK, 8)
    dy, do, dzg, dgate = _out_bwd(dxo, sv["y"], sv["gate"], w["w_out"], sv["o"], sv["proj"], S, tag + "_out_bwd")
    g = {"w_out": _mm(sv["u"], dy, "tn", F32, tag + "_dw_out", tm=1024, tn=1024)}
    doa = _to_heads(do, B, S)
    if fox:
        dqa, dka, dva, dcs = _fox_bwd(sv["qa"], sv["ka"], sv["va"], doa, sv["ol"], sv["cum"], sv["cumrow"],
                                      tag + "_attn_bwd")
        dcs = dcs.reshape(B, H, S).transpose(0, 2, 1)
        dcs = jnp.concatenate([dcs, jnp.zeros((B, S, LANES - H), F32)], axis=-1)
        dfl, db_f = _cum_bwd(dcs, sv["fl"], w["b_f"], tag + "_cum_bwd")
        g["b_f"] = db_f[:, :H]
        tail = [dfl.reshape(T, LANES).astype(BF16)]
        w_in = jnp.concatenate([w["w_in"], w["w_f"]], axis=1)
    else:
        dqa, dka, dva = _sb_bwd(sv["qa"], sv["ka"], sv["va"], doa, sv["ol"], tq, tag + "_attn_bwd")
        tail = []
        w_in = w["w_in"]
    dproj = jnp.concatenate([(_from_heads(dqa) * 0.125).astype(BF16), _from_heads(dka).astype(BF16),
                             _from_heads(dva).astype(BF16), dzg] + tail, axis=1)
    N = dproj.shape[1]
    dw_in = _mm(sv["h"], dproj, "tn", F32, tag + "_dw_in", tm=1024, tn=640 if N % 640 == 0 else 512)
    g["w_in"] = dw_in[:, :4 * DI + H] if fox else dw_in
    dh = _mm(dproj, w_in, "nt", F32, tag + "_dh", tm=512, tn=1024, tk=640 if N % 640 == 0 else 512)
    dx, dshift, dscale, dg = _ln_bwd(dh, sv["x"], dxo, sv["scale"], w["norm_g"], S, tag + "_ln_bwd")
    g["norm_g"] = dg
    dmod = jnp.concatenate([dshift, dscale, dgate], axis=-1).reshape(B, 3 * D)
    g["w_ada"], g["b_ada"] = _mod_bwd(cT, _pad_rows8(dmod), B, tag + "_mod_bwd")
    return dx, g


def _local_step(x3, c, tgt3, wf, ws, final_g):
    B, S, D = x3.shape
    T = B * S
    x = x3.reshape(T, D)
    c8 = _pad_rows8(c)
    cT = c8.T
    x1, sv1 = _layer_fwd(x, c8, wf, S, True, "fox")
    x2, sv2 = _layer_fwd(x1, c8, ws, S, False, "sb")
    dx2, dgf, loss = _final_loss(x2, tgt3.reshape(T, D), final_g, S, "final_loss")
    dx1, gs = _layer_bwd(dx2, sv2, ws, cT, S, False, "sb")
    dx0, gf = _layer_bwd(dx1, sv1, wf, cT, S, True, "fox")
    return loss, dx0.reshape(B, S, D), gf, gs, dgf


def _cols_to_shards(a):
    R, C4 = a.shape
    return a.reshape(R, 4, C4 // 4).transpose(1, 0, 2)


def _shards_to_cols(a):
    n, R, C = a.shape
    return a.transpose(1, 0, 2).reshape(R, n * C)


def kernel(x, c, fox_norm_g, fox_w_ada, fox_b_ada, fox_w_in, fox_b_f, fox_w_out, sb_norm_g, sb_w_ada, sb_b_ada, sb_w_in, sb_w_out, final_norm_g, loss_target, m_fox_norm_g, m_fox_w_ada, m_fox_b_ada, m_fox_w_in, m_fox_b_f, m_fox_w_out, m_sb_norm_g, m_sb_w_ada, m_sb_b_ada, m_sb_w_in, m_sb_w_out, m_final_norm_g, v_fox_norm_g, v_fox_w_ada, v_fox_b_ada, v_fox_w_in, v_fox_b_f, v_fox_w_out, v_sb_norm_g, v_sb_w_ada, v_sb_b_ada, v_sb_w_in, v_sb_w_out, v_final_norm_g):
    B, S, D = x.shape
    DI = 4 * fox_w_out.shape[1]
    H = DI // HEAD_DIM
    chip = _chip_of(lax.axis_index("x"), lax.axis_index("y"))

    big_names = ["fox_w_ada", "fox_w_in", "fox_w_out", "sb_w_ada", "sb_w_in", "sb_w_out"]
    big = dict(fox_w_ada=fox_w_ada[0], fox_w_in=fox_w_in[0], fox_w_out=fox_w_out[0],
               sb_w_ada=sb_w_ada[0], sb_w_in=sb_w_in[0], sb_w_out=sb_w_out[0])
    halves = [big[n].astype(BF16).reshape(2, big[n].shape[0] // 2, big[n].shape[1]) for n in big_names]
    gathered, gsmall = _gather_weights(halves, [sb_norm_g, sb_b_ada])
    full = {}
    for n, a in zip(big_names, gathered):
        a = a.reshape(4, a.shape[1] * a.shape[2], a.shape[3])
        full[n] = a.reshape(4 * a.shape[1], a.shape[2]) if n.endswith("w_out") else _shards_to_cols(a)
    sb_norm_full = gsmall[0].reshape(1, D)
    sb_b_ada_full = gsmall[1].reshape(1, 3 * D)
    w_f = jnp.concatenate([full["fox_w_in"][:, 4 * DI:], jnp.zeros((D, LANES - H), BF16)], axis=1)
    b_f = jnp.concatenate([fox_b_f, jnp.zeros((1, LANES - H), F32)], axis=1)
    wf = dict(w_ada=full["fox_w_ada"], b_ada=fox_b_ada, norm_g=fox_norm_g, w_in=full["fox_w_in"][:, :4 * DI],
              w_f=w_f, b_f=b_f, w_out=full["fox_w_out"])
    ws = dict(w_ada=full["sb_w_ada"], b_ada=sb_b_ada_full, norm_g=sb_norm_full, w_in=full["sb_w_in"],
              w_out=full["sb_w_out"])

    loss, grad_x, gf, gs, dgf = _local_step(x, c, loss_target, wf, ws, final_norm_g.reshape(1, D))

    part = dict(fox_w_ada=gf["w_ada"], fox_w_in=gf["w_in"], fox_w_out=gf["w_out"],
                sb_w_ada=gs["w_ada"], sb_w_in=gs["w_in"], sb_w_out=gs["w_out"])
    shard_major = []
    for n in big_names:
        a = part[n]
        a = a.reshape(4, a.shape[0] // 4, a.shape[1]) if n.endswith("w_out") else _cols_to_shards(a)
        shard_major.append(a.reshape(4, 2, a.shape[1] // 2, a.shape[2]))
    own, got = _pair_exchange(shard_major)
    pair_sums = []
    for n, a, b in zip(big_names, own, got):
        r, C = a.shape[1:]
        pair_sums.append(_ew_sum([a.reshape(4 * r, C), b.reshape(4 * r, C)], n + "_pair_sum").reshape(4, r, C))
    mine, others = _chip_exchange(pair_sums)
    reduced_halves = [_ew_sum([a, b[0], b[1], b[2]], n + "_chip_sum") for n, a, b in zip(big_names, mine, others)]
    shared = _pair_share(reduced_halves)
    grad_big = {n: a.reshape(2 * a.shape[1], a.shape[2]) for n, a in zip(big_names, shared)}

    pieces = [loss, gf["norm_g"], gf["b_ada"], jnp.concatenate([gf["b_f"], jnp.zeros((1, LANES - H), F32)], axis=1),
              gs["norm_g"], gs["b_ada"], dgf]
    vec = jnp.concatenate(pieces, axis=1)
    red = _allreduce_small(_pad_rows8(vec))[0:1]
    offs = [0]
    for p in pieces:
        offs.append(offs[-1] + p.shape[1])
    r_loss, r_fng, r_fba, r_fbf, r_sng, r_sba, r_fin = [red[:, offs[i]:offs[i + 1]] for i in range(7)]
    small_grads = dict(
        fox_norm_g=r_fng, fox_b_ada=r_fba, fox_b_f=r_fbf[:, :H],
        sb_norm_g=lax.dynamic_slice_in_dim(r_sng, chip * (D // 4), D // 4, axis=1),
        sb_b_ada=lax.dynamic_slice_in_dim(r_sba, chip * (3 * D // 4), 3 * D // 4, axis=1),
        final_norm_g=r_fin)

    weights = dict(fox_norm_g=fox_norm_g, fox_w_ada=fox_w_ada, fox_b_ada=fox_b_ada, fox_w_in=fox_w_in, fox_b_f=fox_b_f,
                   fox_w_out=fox_w_out, sb_norm_g=sb_norm_g, sb_w_ada=sb_w_ada, sb_b_ada=sb_b_ada, sb_w_in=sb_w_in,
                   sb_w_out=sb_w_out, final_norm_g=final_norm_g)
    ms = dict(fox_norm_g=m_fox_norm_g, fox_w_ada=m_fox_w_ada, fox_b_ada=m_fox_b_ada, fox_w_in=m_fox_w_in,
              fox_b_f=m_fox_b_f, fox_w_out=m_fox_w_out, sb_norm_g=m_sb_norm_g, sb_w_ada=m_sb_w_ada,
              sb_b_ada=m_sb_b_ada, sb_w_in=m_sb_w_in, sb_w_out=m_sb_w_out, final_norm_g=m_final_norm_g)
    vs = dict(fox_norm_g=v_fox_norm_g, fox_w_ada=v_fox_w_ada, fox_b_ada=v_fox_b_ada, fox_w_in=v_fox_w_in,
              fox_b_f=v_fox_b_f, fox_w_out=v_fox_w_out, sb_norm_g=v_sb_norm_g, sb_w_ada=v_sb_w_ada,
              sb_b_ada=v_sb_b_ada, sb_w_in=v_sb_w_in, sb_w_out=v_sb_w_out, final_norm_g=v_final_norm_g)
    order = ["fox_norm_g", "fox_w_ada", "fox_b_ada", "fox_w_in", "fox_b_f", "fox_w_out", "sb_norm_g", "sb_w_ada",
             "sb_b_ada", "sb_w_in", "sb_w_out", "final_norm_g"]
    grads, deltas, new_m, new_v = {}, {}, {}, {}
    for n in big_names:
        shp = weights[n].shape
        g2 = grad_big[n]
        d, m2, v2 = _adamw(weights[n][0], g2, ms[n][0], vs[n][0], n + "_adamw")
        grads[n], deltas[n], new_m[n], new_v[n] = g2.reshape(shp), d.reshape(shp), m2.reshape(shp), v2.reshape(shp)
    small_names = [n for n in order if n not in big_names]
    sizes = [small_grads[n].shape[1] for n in small_names]
    total = sum(sizes)
    padn = (-total) % LANES

    def pack(d):
        return jnp.concatenate([d[n].reshape(1, -1) for n in small_names] + [jnp.ones((1, padn), F32)], axis=1)

    sd, sm, sv_ = _adamw(pack(weights), pack(small_grads), pack(ms), pack(vs), "small_adamw")
    o = 0
    for n, sz in zip(small_names, sizes):
        shp = weights[n].shape
        grads[n] = small_grads[n].reshape(shp)
        deltas[n], new_m[n], new_v[n] = (t[:, o:o + sz].reshape(shp) for t in (sd, sm, sv_))
        o += sz
    return (r_loss[0, 0], grad_x, *[grads[n] for n in order], *[deltas[n] for n in order],
            *[new_m[n] for n in order], *[new_v[n] for n in order])
```

```python
import functools

import jax
import jax.numpy as jnp
from jax import lax
from jax.experimental import pallas as pl
from jax.experimental.pallas import tpu as pltpu

F32 = jnp.float32
BF16 = jnp.bfloat16
HEAD_DIM = 64
LANES = 128
AUX_LANE = HEAD_DIM
NORM_EPS = 1e-6
ADAM_LR = 0.001
ADAM_B1 = 0.9
ADAM_B2 = 0.999
ADAM_EPS = 1e-08
ADAM_WD = 0.01
ADAM_STEP = 10
VMEM_LIMIT = 56 * 1024 * 1024
ATT_BLOCK = 256
MESH = pl.DeviceIdType.MESH
HBM = pl.BlockSpec(memory_space=pltpu.HBM)
NT = (((1,), (1,)), ((), ()))
TN = (((0,), (0,)), ((), ()))


def _call(body, **kw):
    return pl.pallas_call(body, **kw)


def _params(**kw):
    return pltpu.CompilerParams(vmem_limit_bytes=VMEM_LIMIT, **kw)


def _tile(dim, pref, mult=128):
    if dim <= pref:
        return dim
    t = (pref // mult) * mult
    while t >= mult:
        if dim % t == 0:
            return t
        t -= mult
    return dim


def _sigmoid(x):
    return 1.0 / (1.0 + jnp.exp(-x))


def _split3(x):
    hi = x.astype(BF16)
    r = x - hi.astype(F32)
    mid = r.astype(BF16)
    lo = (r - mid.astype(F32)).astype(BF16)
    return hi, mid, lo


def _mm(a, b, mode, out_dtype, name, tm=512, tn=512, tk=512):
    if mode == "nn":
        (M, K), (_, N) = a.shape, b.shape
    elif mode == "nt":
        (M, K), (N, _) = a.shape, b.shape
    else:
        (K, M), (_, N) = a.shape, b.shape
    tm, tn, tk = _tile(M, tm), _tile(N, tn), _tile(K, tk)
    nk = K // tk
    dims = {"nn": (((1,), (0,)), ((), ())), "nt": NT, "tn": TN}[mode]

    def body(a_ref, b_ref, o_ref, acc_ref):
        k = pl.program_id(2)

        @pl.when(k == 0)
        def _():
            acc_ref[...] = jnp.zeros_like(acc_ref)

        acc_ref[...] += lax.dot_general(a_ref[...], b_ref[...], dims, preferred_element_type=F32)

        @pl.when(k == nk - 1)
        def _():
            o_ref[...] = acc_ref[...].astype(out_dtype)

    a_spec = (pl.BlockSpec((tk, tm), lambda i, j, k: (k, i)) if mode == "tn"
              else pl.BlockSpec((tm, tk), lambda i, j, k: (i, k)))
    b_spec = (pl.BlockSpec((tn, tk), lambda i, j, k: (j, k)) if mode == "nt"
              else pl.BlockSpec((tk, tn), lambda i, j, k: (k, j)))
    return _call(
        body, name=name, grid=(M // tm, N // tn, nk),
        in_specs=[a_spec, b_spec], out_specs=pl.BlockSpec((tm, tn), lambda i, j, k: (i, j)),
        out_shape=jax.ShapeDtypeStruct((M, N), out_dtype),
        scratch_shapes=[pltpu.VMEM((tm, tn), F32)], compiler_params=_params(),
    )(a, b)


def _mod_fwd(c8, w_ada, b_ada, name):
    D, N = w_ada.shape
    tn = _tile(N, 512)

    def body(c_ref, w_ref, b_ref, o_ref):
        c = c_ref[...]
        sc = (c * _sigmoid(c)).astype(BF16)
        o_ref[...] = jnp.dot(sc, w_ref[...], preferred_element_type=F32) + b_ref[...]

    return _call(
        body, name=name, grid=(N // tn,),
        in_specs=[pl.BlockSpec((8, D), lambda j: (0, 0)), pl.BlockSpec((D, tn), lambda j: (0, j)),
                  pl.BlockSpec((1, tn), lambda j: (0, j))],
        out_specs=pl.BlockSpec((8, tn), lambda j: (0, j)),
        out_shape=jax.ShapeDtypeStruct((8, N), F32), compiler_params=_params(),
    )(c8, w_ada, b_ada)


def _mod_bwd(cT, dmod8, nb, name):
    D = cT.shape[0]
    N = dmod8.shape[1]
    tn = _tile(N, 512)

    def body(c_ref, d_ref, w_ref, b_ref):
        c = c_ref[...]
        sc = c * _sigmoid(c)
        d = d_ref[...]
        acc = sc[:, 0:1] * d[0:1, :]
        bsum = d[0:1, :]
        for b in range(1, nb):
            acc = acc + sc[:, b:b + 1] * d[b:b + 1, :]
            bsum = bsum + d[b:b + 1, :]
        w_ref[...] = acc
        b_ref[...] = bsum

    return _call(
        body, name=name, grid=(N // tn,),
        in_specs=[pl.BlockSpec((D, 8), lambda j: (0, 0)), pl.BlockSpec((8, tn), lambda j: (0, j))],
        out_specs=[pl.BlockSpec((D, tn), lambda j: (0, j)), pl.BlockSpec((1, tn), lambda j: (0, j))],
        out_shape=[jax.ShapeDtypeStruct((D, N), F32), jax.ShapeDtypeStruct((1, N), F32)],
        compiler_params=_params(),
    )(cT, dmod8)


def _ln_proj(x, shift, scale, g, w, S, name):
    T, D = x.shape
    N = w.shape[1]
    tm = _tile(S, 512)
    tn = _tile(N, 1024)
    per_b = S // tm

    def body(x_ref, sh_ref, sc_ref, g_ref, w_ref, p_ref, h_ref):
        @pl.when(pl.program_id(1) == 0)
        def _():
            xv = x_ref[...]
            r = lax.rsqrt(jnp.mean(xv * xv, axis=-1, keepdims=True) + NORM_EPS)
            h = (xv * r) * g_ref[...] * (1.0 + sc_ref[0]) + sh_ref[0]
            h_ref[...] = h.astype(BF16)

        p_ref[...] = jnp.dot(h_ref[...], w_ref[...], preferred_element_type=F32).astype(BF16)

    return _call(
        body, name=name, grid=(T // tm, N // tn),
        in_specs=[pl.BlockSpec((tm, D), lambda i, j: (i, 0)),
                  pl.BlockSpec((1, 1, D), lambda i, j: (i // per_b, 0, 0)),
                  pl.BlockSpec((1, 1, D), lambda i, j: (i // per_b, 0, 0)),
                  pl.BlockSpec((1, D), lambda i, j: (0, 0)),
                  pl.BlockSpec((D, tn), lambda i, j: (0, j))],
        out_specs=[pl.BlockSpec((tm, tn), lambda i, j: (i, j)), pl.BlockSpec((tm, D), lambda i, j: (i, 0))],
        out_shape=[jax.ShapeDtypeStruct((T, N), BF16), jax.ShapeDtypeStruct((T, D), BF16)],
        compiler_params=_params(),
    )(x, shift, scale, g, w)


def _ln_bwd(dh, x, dxo, scale, g, S, name):
    T, D = x.shape
    B = T // S
    tm = _tile(S, 512)
    per_b = S // tm

    def body(dh_ref, x_ref, dxo_ref, sc_ref, g_ref, dx_ref, dsh_ref, dsc_ref, dg_ref):
        i = pl.program_id(0)
        xv = x_ref[...]
        dh_v = dh_ref[...]
        r = lax.rsqrt(jnp.mean(xv * xv, axis=-1, keepdims=True) + NORM_EPS)
        xn = xv * r
        gv = g_ref[...]
        one_sc = 1.0 + sc_ref[0]
        dhxn = dh_v * xn

        @pl.when(i % per_b == 0)
        def _():
            dsh_ref[...] = jnp.zeros_like(dsh_ref)
            dsc_ref[...] = jnp.zeros_like(dsc_ref)

        @pl.when(i == 0)
        def _():
            dg_ref[...] = jnp.zeros_like(dg_ref)

        dsh_ref[0] += jnp.sum(dh_v, axis=0, keepdims=True)
        dsc_ref[0] += jnp.sum(dhxn, axis=0, keepdims=True) * gv
        dg_ref[...] += jnp.sum(dhxn, axis=0, keepdims=True) * one_sc
        dxn = dh_v * (gv * one_sc)
        dx_ref[...] = r * (dxn - xn * jnp.mean(dxn * xn, axis=-1, keepdims=True)) + dxo_ref[...]

    row = pl.BlockSpec((tm, D), lambda i: (i, 0))
    per = pl.BlockSpec((1, 1, D), lambda i: (i // per_b, 0, 0))
    vec = pl.BlockSpec((1, D), lambda i: (0, 0))
    return _call(
        body, name=name, grid=(T // tm,),
        in_specs=[row, row, row, per, vec], out_specs=[row, per, per, vec],
        out_shape=[jax.ShapeDtypeStruct((T, D), F32), jax.ShapeDtypeStruct((B, 1, D), F32),
                   jax.ShapeDtypeStruct((B, 1, D), F32), jax.ShapeDtypeStruct((1, D), F32)],
        compiler_params=_params(),
    )(dh, x, dxo, scale, g)


def _gate_out(o, proj, w_out, x, gate, S, name):
    T, DI = o.shape
    D = w_out.shape[1]
    tm = _tile(S, 256)
    per_b = S // tm

    def body(o_ref, z_ref, w_ref, x_ref, g_ref, xo_ref, y_ref, u_ref):
        z = z_ref[...].astype(F32)
        u = (o_ref[...] * (z * _sigmoid(z))).astype(BF16)
        u_ref[...] = u
        y = jnp.dot(u, w_ref[...], preferred_element_type=F32)
        y_ref[...] = y
        xo_ref[...] = x_ref[...] + g_ref[0] * y

    wide = pl.BlockSpec((tm, DI), lambda i: (i, 0))
    row = pl.BlockSpec((tm, D), lambda i: (i, 0))
    return _call(
        body, name=name, grid=(T // tm,),
        in_specs=[wide, pl.BlockSpec((tm, DI), lambda i: (i, 3)), pl.BlockSpec((DI, D), lambda i: (0, 0)), row,
                  pl.BlockSpec((1, 1, D), lambda i: (i // per_b, 0, 0))],
        out_specs=[row, row, wide],
        out_shape=[jax.ShapeDtypeStruct((T, D), F32), jax.ShapeDtypeStruct((T, D), F32),
                   jax.ShapeDtypeStruct((T, DI), BF16)],
        compiler_params=_params(),
    )(o, proj, w_out, x, gate)


def _out_bwd(dxo, y, gate, w_out, o, proj, S, name):
    T, D = dxo.shape
    DI = o.shape[1]
    B = T // S
    tm = _tile(S, 256)
    per_b = S // tm

    def body(dxo_ref, y_ref, g_ref, w_ref, o_ref, z_ref, dy_ref, do_ref, dz_ref, dg_ref):
        dxo_v = dxo_ref[...]
        dy = (dxo_v * g_ref[0]).astype(BF16)
        dy_ref[...] = dy
        du = lax.dot_general(dy, w_ref[...], NT, preferred_element_type=F32)
        z = z_ref[...].astype(F32)
        sg = _sigmoid(z)
        do_ref[...] = (du * (z * sg)).astype(BF16)
        dz_ref[...] = (du * o_ref[...] * (sg * (1.0 + z * (1.0 - sg)))).astype(BF16)

        @pl.when(pl.program_id(0) % per_b == 0)
        def _():
            dg_ref[...] = jnp.zeros_like(dg_ref)

        dg_ref[0] += jnp.sum(dxo_v * y_ref[...], axis=0, keepdims=True)

    wide = pl.BlockSpec((tm, DI), lambda i: (i, 0))
    row = pl.BlockSpec((tm, D), lambda i: (i, 0))
    per = pl.BlockSpec((1, 1, D), lambda i: (i // per_b, 0, 0))
    return _call(
        body, name=name, grid=(T // tm,),
        in_specs=[row, row, per, pl.BlockSpec((DI, D), lambda i: (0, 0)), wide,
                  pl.BlockSpec((tm, DI), lambda i: (i, 3))],
        out_specs=[row, wide, wide, per],
        out_shape=[jax.ShapeDtypeStruct((T, D), BF16), jax.ShapeDtypeStruct((T, DI), BF16),
                   jax.ShapeDtypeStruct((T, DI), BF16), jax.ShapeDtypeStruct((B, 1, D), F32)],
        compiler_params=_params(),
    )(dxo, y, gate, w_out, o, proj)


def _final_loss(x, tgt, g, S, name):
    T, D = x.shape
    tm = _tile(S, 512)

    def body(x_ref, t_ref, g_ref, dx_ref, dg_ref, l_ref):
        @pl.when(pl.program_id(0) == 0)
        def _():
            dg_ref[...] = jnp.zeros_like(dg_ref)
            l_ref[...] = jnp.zeros_like(l_ref)

        xv = x_ref[...]
        gv = g_ref[...]
        r = lax.rsqrt(jnp.mean(xv * xv, axis=-1, keepdims=True) + NORM_EPS)
        xn = xv * r
        e = xn * gv - t_ref[...]
        part = jnp.sum(jnp.sum(e * e, axis=0, keepdims=True), axis=1, keepdims=True)
        l_ref[...] += (0.5 / D) * part
        dy = e * (1.0 / D)
        dg_ref[...] += jnp.sum(dy * xn, axis=0, keepdims=True)
        dxn = dy * gv
        dx_ref[...] = r * (dxn - xn * jnp.mean(dxn * xn, axis=-1, keepdims=True))

    row = pl.BlockSpec((tm, D), lambda i: (i, 0))
    return _call(
        body, name=name, grid=(T // tm,),
        in_specs=[row, row, pl.BlockSpec((1, D), lambda i: (0, 0))],
        out_specs=[row, pl.BlockSpec((1, D), lambda i: (0, 0)), pl.BlockSpec((1, LANES), lambda i: (0, 0))],
        out_shape=[jax.ShapeDtypeStruct((T, D), F32), jax.ShapeDtypeStruct((1, D), F32),
                   jax.ShapeDtypeStruct((1, LANES), F32)],
        compiler_params=_params(),
    )(x, tgt, g)


def _cum_fwd(fl, bf, name):
    B, S, _ = fl.shape
    ch = _tile(S, 256, 8)

    def body(fl_ref, b_ref, cum_ref):
        ri = lax.broadcasted_iota(jnp.int32, (ch, ch), 0)
        ci = lax.broadcasted_iota(jnp.int32, (ch, ch), 1)
        tri = jnp.where(ri >= ci, 1.0, 0.0).astype(BF16)

        def step(i, carry):
            r0 = pl.multiple_of(i * ch, ch)
            z = fl_ref[0, pl.ds(r0, ch), :] + b_ref[...]
            lf = jnp.minimum(z, 0.0) - jnp.log(1.0 + jnp.exp(-jnp.abs(z)))
            hi, mid, lo = _split3(lf)
            cs = (jnp.dot(tri, hi, preferred_element_type=F32) + jnp.dot(tri, mid, preferred_element_type=F32)
                  + jnp.dot(tri, lo, preferred_element_type=F32)) + carry
            cum_ref[0, pl.ds(r0, ch), :] = cs
            return cs[ch - 1:ch, :]

        lax.fori_loop(0, S // ch, step, jnp.zeros((1, LANES), F32))

    blk = pl.BlockSpec((1, S, LANES), lambda b: (b, 0, 0))
    return _call(
        body, name=name, grid=(B,), in_specs=[blk, pl.BlockSpec((1, LANES), lambda b: (0, 0))], out_specs=blk,
        out_shape=jax.ShapeDtypeStruct((B, S, LANES), F32), compiler_params=_params(),
    )(fl, bf)


def _cum_bwd(dcs, fl, bf, name):
    B, S, _ = fl.shape
    ch = _tile(S, 256, 8)
    n = S // ch

    def body(d_ref, fl_ref, b_ref, o_ref, db_ref):
        ri = lax.broadcasted_iota(jnp.int32, (ch, ch), 0)
        ci = lax.broadcasted_iota(jnp.int32, (ch, ch), 1)
        tri = jnp.where(ci >= ri, 1.0, 0.0).astype(BF16)

        @pl.when(pl.program_id(0) == 0)
        def _():
            db_ref[...] = jnp.zeros_like(db_ref)

        def step(t, carry):
            tail, dbsum = carry
            r0 = pl.multiple_of((n - 1 - t) * ch, ch)
            hi, mid, lo = _split3(d_ref[0, pl.ds(r0, ch), :])
            suf = (jnp.dot(tri, hi, preferred_element_type=F32) + jnp.dot(tri, mid, preferred_element_type=F32)
                   + jnp.dot(tri, lo, preferred_element_type=F32)) + tail
            z = fl_ref[0, pl.ds(r0, ch), :] + b_ref[...]
            dfl = -suf * _sigmoid(-z)
            o_ref[0, pl.ds(r0, ch), :] = dfl
            return suf[0:1, :], dbsum + jnp.sum(dfl, axis=0, keepdims=True)

        z1 = jnp.zeros((1, LANES), F32)
        _, dbsum = lax.fori_loop(0, n, step, (z1, z1))
        db_ref[...] += dbsum

    blk = pl.BlockSpec((1, S, LANES), lambda b: (b, 0, 0))
    vec = pl.BlockSpec((1, LANES), lambda b: (0, 0))
    return _call(
        body, name=name, grid=(B,), in_specs=[blk, blk, vec], out_specs=[blk, vec],
        out_shape=[jax.ShapeDtypeStruct((B, S, LANES), F32), jax.ShapeDtypeStruct((1, LANES), F32)],
        compiler_params=_params(),
    )(dcs, fl, bf)


HEADS_PER_STEP = 2
ROW_SPLITS = 2
CHAINS = [(hh, rr) for hh in range(HEADS_PER_STEP) for rr in range(ROW_SPLITS)]


def _head_spec(S):
    return pl.BlockSpec((1, HEADS_PER_STEP, S, LANES), lambda b, h: (b, h, 0, 0))


def _row_spec(nq, tq):
    return pl.BlockSpec((1, HEADS_PER_STEP, nq, 1, tq), lambda b, h: (b, h, 0, 0, 0))


def _aux_col(x):
    lane = lax.broadcasted_iota(jnp.int32, x.shape, 1)
    return jnp.sum(jnp.where(lane == AUX_LANE, x, 0.0), axis=1, keepdims=True)


def _hi_lo(x):
    hi = x.astype(BF16)
    return hi, (x - hi.astype(F32)).astype(BF16)


def _dot(a, b, dims=None):
    if dims is None:
        return jnp.dot(a, b, preferred_element_type=F32)
    return lax.dot_general(a, b, dims, preferred_element_type=F32)


def _causal_blocks(nq, prep, init, stages, finish, combine=None, descending=False):
    def qloop(qi, _):
        ctx = [prep(ch, qi) for ch in CHAINS]

        def step(kj, carry, masked):
            st = list(carry)
            for n, stage in enumerate(stages):
                if combine is not None and n == len(stages) - 1:
                    combine(kj, ctx, st)
                st = [stage(ch, ctx[i], kj, masked, st[i]) for i, ch in enumerate(CHAINS)]
            return tuple(st)

        carry = tuple(init() for _ in CHAINS)
        if descending:
            carry = step(qi, carry, True)
            carry = lax.fori_loop(0, qi, lambda t, cr: step(qi - 1 - t, cr, False), carry)
        else:
            carry = lax.fori_loop(0, qi, lambda kj, cr: step(kj, cr, False), carry)
            carry = step(qi, carry, True)
        for i, ch in enumerate(CHAINS):
            finish(ch, ctx[i], qi, carry[i])
        return 0

    lax.fori_loop(0, nq, qloop, 0)


def _block_iotas(tq, rows):
    return lax.broadcasted_iota(jnp.int32, (rows, tq), 0), lax.broadcasted_iota(jnp.int32, (rows, tq), 1)


def _fox_fwd(qa, ka, va, cumcol, cumrow, name):
    B, H, S, _ = qa.shape
    tq = cumrow.shape[-1]
    nq = S // tq
    rows = tq // ROW_SPLITS

    def body(q_ref, k_ref, v_ref, cc_ref, cr_ref, o_ref):
        h0 = pl.program_id(1) * HEADS_PER_STEP
        lane = lax.broadcasted_iota(jnp.int32, (rows, LANES), 1)
        ri, ci = _block_iotas(tq, rows)

        def qrows(qi, rr):
            return pl.ds(pl.multiple_of(qi * tq + rr * rows, rows), rows)

        def prep(ch, qi):
            hh, rr = ch
            ccol = jnp.sum(jnp.where(lane == h0 + hh, cc_ref[0, qrows(qi, rr), :], 0.0), axis=1, keepdims=True)
            return q_ref[0, hh, qrows(qi, rr), :], ccol

        def init():
            return jnp.full((rows, 1), -jnp.inf, F32), jnp.zeros((rows, 1), F32), jnp.zeros((rows, LANES), F32)

        def kblock(kj):
            return pl.ds(pl.multiple_of(kj * tq, tq), tq)

        def scores(ch, ctx, kj, masked, st):
            return st + (_dot(ctx[0], k_ref[0, ch[0], kblock(kj), :], NT),)

        def softmax(ch, ctx, kj, masked, st):
            m, l, acc, s = st
            s = s + ctx[1] - cr_ref[0, ch[0], kj]
            if masked:
                s = jnp.where(ci <= ri + ch[1] * rows, s, -jnp.inf)
            m_new = jnp.maximum(m, jnp.max(s, axis=1, keepdims=True))
            alpha = jnp.exp(m - m_new)
            p = jnp.exp(s - m_new)
            return (m_new, alpha * l + jnp.sum(p, axis=1, keepdims=True), acc, alpha) + _hi_lo(p)

        def values(ch, ctx, kj, masked, st):
            m, l, acc, alpha, hi, lo = st
            v = v_ref[0, ch[0], kblock(kj), :]
            return m, l, alpha * acc + (_dot(hi, v) + _dot(lo, v))

        def finish(ch, ctx, qi, carry):
            m, l, acc = carry
            o_ref[0, ch[0], qrows(qi, ch[1]), :] = jnp.where(lane < HEAD_DIM, acc / l, m + jnp.log(l))

        _causal_blocks(nq, prep, init, [scores, softmax, values], finish)

    hs = _head_spec(S)
    return _call(
        body, name=name, grid=(B, H // HEADS_PER_STEP),
        in_specs=[hs, hs, hs, pl.BlockSpec((1, S, LANES), lambda b, h: (b, 0, 0)), _row_spec(nq, tq)],
        out_specs=hs, out_shape=jax.ShapeDtypeStruct((B, H, S, LANES), F32), compiler_params=_params(),
    )(qa, ka, va, cumcol, cumrow)


def _fox_bwd(qa, ka, va, doa, ol, cumcol, cumrow, name):
    B, H, S, _ = qa.shape
    tq = cumrow.shape[-1]
    nq = S // tq
    rows = tq // ROW_SPLITS

    def body(q_ref, k_ref, v_ref, do_ref, ol_ref, cc_ref, cr_ref, dq_ref, dk_ref, dv_ref, dcs_ref):
        h0 = pl.program_id(1) * HEADS_PER_STEP
        lane = lax.broadcasted_iota(jnp.int32, (rows, LANES), 1)
        ri, ci = _block_iotas(tq, rows)
        dk_ref[...] = jnp.zeros_like(dk_ref)
        dv_ref[...] = jnp.zeros_like(dv_ref)
        dcs_ref[...] = jnp.zeros_like(dcs_ref)

        def qrows(qi, rr):
            return pl.ds(pl.multiple_of(qi * tq + rr * rows, rows), rows)

        def kblock(kj):
            return pl.ds(pl.multiple_of(kj * tq, tq), tq)

        def prep(ch, qi):
            hh, rr = ch
            do = do_ref[0, hh, qrows(qi, rr), :]
            olb = ol_ref[0, hh, qrows(qi, rr), :]
            delta = jnp.sum(olb * do.astype(F32), axis=1, keepdims=True)
            ccol = jnp.sum(jnp.where(lane == h0 + hh, cc_ref[0, qrows(qi, rr), :], 0.0), axis=1, keepdims=True)
            return q_ref[0, hh, qrows(qi, rr), :], do, _aux_col(olb), delta, ccol

        def init():
            return (jnp.zeros((rows, LANES), F32),)

        def scores(ch, ctx, kj, masked, st):
            q, do = ctx[0], ctx[1]
            return st + (_dot(q, k_ref[0, ch[0], kblock(kj), :], NT), _dot(do, v_ref[0, ch[0], kblock(kj), :], NT))

        def softmax_bwd(ch, ctx, kj, masked, st):
            dq, s, dp = st
            _, _, lse, delta, ccol = ctx
            s = s + ccol - cr_ref[0, ch[0], kj]
            if masked:
                s = jnp.where(ci <= ri + ch[1] * rows, s, -jnp.inf)
            p = jnp.exp(s - lse)
            ds = p * (dp - delta)
            return dq, p.astype(BF16), ds.astype(BF16), jnp.sum(ds, axis=0, keepdims=True)

        def combine(kj, ctx, st):
            for hh in range(HEADS_PER_STEP):
                mine = [i for i, ch in enumerate(CHAINS) if ch[0] == hh]
                dv = dk = dcs = None
                for i in mine:
                    q, do = ctx[i][0], ctx[i][1]
                    _, pb, dsb, col = st[i]
                    dv = _dot(pb, do, TN) if dv is None else dv + _dot(pb, do, TN)
                    dk = _dot(dsb, q, TN) if dk is None else dk + _dot(dsb, q, TN)
                    dcs = col if dcs is None else dcs + col
                dv_ref[0, hh, kblock(kj), :] += dv
                dk_ref[0, hh, kblock(kj), :] += dk
                dcs_ref[0, hh, kj] += dcs

        def queries(ch, ctx, kj, masked, st):
            dq, _, dsb, _ = st
            return (dq + _dot(dsb, k_ref[0, ch[0], kblock(kj), :]),)

        def finish(ch, ctx, qi, carry):
            dq_ref[0, ch[0], qrows(qi, ch[1]), :] = carry[0]

        _causal_blocks(nq, prep, init, [scores, softmax_bwd, queries], finish, combine=combine)

    hs = _head_spec(S)
    out = jax.ShapeDtypeStruct((B, H, S, LANES), F32)
    return _call(
        body, name=name, grid=(B, H // HEADS_PER_STEP),
        in_specs=[hs, hs, hs, hs, hs, pl.BlockSpec((1, S, LANES), lambda b, h: (b, 0, 0)), _row_spec(nq, tq)],
        out_specs=[hs, hs, hs, _row_spec(nq, tq)],
        out_shape=[out, out, out, jax.ShapeDtypeStruct((B, H, nq, 1, tq), F32)], compiler_params=_params(),
    )(qa, ka, va, doa, ol, cumcol, cumrow)


def _softplus_parts(z):
    e = jnp.exp(-jnp.abs(z))
    return jnp.maximum(z, 0.0) + jnp.log(1.0 + e), e


def _sb_fwd(qa, ka, va, tq, name):
    B, H, S, _ = qa.shape
    nq = S // tq
    rows = tq // ROW_SPLITS

    def body(q_ref, k_ref, v_ref, o_ref):
        lane = lax.broadcasted_iota(jnp.int32, (rows, LANES), 1)
        ri, ci = _block_iotas(tq, rows)
        sq_r, sq_c = _block_iotas(tq, tq)
        above = jnp.where(sq_r > sq_c, 1.0, 0.0).astype(BF16)

        def qrows(qi, rr):
            return pl.ds(pl.multiple_of(qi * tq + rr * rows, rows), rows)

        def kblock(kj):
            return pl.ds(pl.multiple_of(kj * tq, tq), tq)

        def prep(ch, qi):
            return q_ref[0, ch[0], qrows(qi, ch[1]), :]

        def init():
            return jnp.zeros((rows, 1), F32), jnp.zeros((rows, LANES), F32)

        def scores(ch, q, kj, masked, st):
            return st + (_dot(q, k_ref[0, ch[0], kblock(kj), :], NT),)

        def logs(ch, q, kj, masked, st):
            c, acc, z = st
            sp, _ = _softplus_parts(z)
            lk = -sp
            if masked:
                lk = jnp.where(ci < ri + ch[1] * rows, lk, 0.0)
            return (c, acc, z - sp, jnp.sum(lk, axis=1, keepdims=True)) + _hi_lo(lk)

        def suffix(ch, q, kj, masked, st):
            c, acc, lb, lksum, hi, lo = st
            return c, acc, lb, lksum, _dot(hi, above) + _dot(lo, above)

        def weights(ch, q, kj, masked, st):
            c, acc, lb, lksum, after = st
            a = jnp.exp(lb + after + c)
            if masked:
                a = jnp.where(ci < ri + ch[1] * rows, a, 0.0)
            return c + lksum, acc, a.astype(BF16)

        def values(ch, q, kj, masked, st):
            c, acc, ab = st
            return c, acc + _dot(ab, v_ref[0, ch[0], kblock(kj), :])

        def finish(ch, q, qi, carry):
            c, acc = carry
            o_ref[0, ch[0], qrows(qi, ch[1]), :] = jnp.where(lane < HEAD_DIM, acc, c)

        _causal_blocks(nq, prep, init, [scores, logs, suffix, weights, values], finish, descending=True)

    hs = _head_spec(S)
    return _call(
        body, name=name, grid=(B, H // HEADS_PER_STEP), in_specs=[hs, hs, hs], out_specs=hs,
        out_shape=jax.ShapeDtypeStruct((B, H, S, LANES), F32), compiler_params=_params(),
    )(qa, ka, va)


def _sb_bwd(qa, ka, va, doa, ol, tq, name):
    B, H, S, _ = qa.shape
    nq = S // tq
    rows = tq // ROW_SPLITS

    def body(q_ref, k_ref, v_ref, do_ref, ol_ref, dq_ref, dk_ref, dv_ref):
        ri, ci = _block_iotas(tq, rows)
        sq_r, sq_c = _block_iotas(tq, tq)
        upto = jnp.where(sq_r <= sq_c, 1.0, 0.0).astype(BF16)
        before = jnp.where(sq_r < sq_c, 1.0, 0.0).astype(BF16)
        dk_ref[...] = jnp.zeros_like(dk_ref)
        dv_ref[...] = jnp.zeros_like(dv_ref)

        def qrows(qi, rr):
            return pl.ds(pl.multiple_of(qi * tq + rr * rows, rows), rows)

        def kblock(kj):
            return pl.ds(pl.multiple_of(kj * tq, tq), tq)

        def prep(ch, qi):
            hh, rr = ch
            return q_ref[0, hh, qrows(qi, rr), :], do_ref[0, hh, qrows(qi, rr), :], _aux_col(ol_ref[0, hh, qrows(qi, rr), :])

        def init():
            return jnp.zeros((rows, 1), F32), jnp.zeros((rows, 1), F32), jnp.zeros((rows, LANES), F32)

        def scores(ch, ctx, kj, masked, st):
            q, do, _ = ctx
            return st + (_dot(q, k_ref[0, ch[0], kblock(kj), :], NT), _dot(do, v_ref[0, ch[0], kblock(kj), :], NT))

        def logs(ch, ctx, kj, masked, st):
            cpre, pg, dq, z, da = st
            sp, e = _softplus_parts(z)
            inv = 1.0 / (1.0 + e)
            sig = jnp.where(z >= 0.0, inv, e * inv)
            lk = -sp
            if masked:
                lk = jnp.where(ci < ri + ch[1] * rows, lk, 0.0)
            return (cpre, pg, dq, da, z - sp, sig, jnp.sum(lk, axis=1, keepdims=True)) + _hi_lo(lk)

        def prefix(ch, ctx, kj, masked, st):
            cpre, pg, dq, da, lb, sig, lksum, hi, lo = st
            return cpre, pg, dq, da, lb, sig, lksum, _dot(hi, upto) + _dot(lo, upto)

        def weights(ch, ctx, kj, masked, st):
            cpre, pg, dq, da, lb, sig, lksum, pre = st
            a = jnp.exp(lb + (ctx[2] - (cpre + pre)))
            if masked:
                a = jnp.where(ci < ri + ch[1] * rows, a, 0.0)
            g = da * a
            return cpre + lksum, pg, dq, sig, a.astype(BF16), g, g.astype(BF16)

        def grad_prefix(ch, ctx, kj, masked, st):
            cpre, pg, dq, sig, ab, g, gb = st
            return cpre, pg, dq, sig, ab, g, _dot(gb, before)

        def dlogits(ch, ctx, kj, masked, st):
            cpre, pg, dq, sig, ab, g, pfx = st
            dz = g * (1.0 - sig) - (pfx + pg) * sig
            if masked:
                dz = jnp.where(ci < ri + ch[1] * rows, dz, 0.0)
            return cpre, pg + jnp.sum(g, axis=1, keepdims=True), dq, ab, dz.astype(BF16)

        def combine(kj, ctx, st):
            for hh in range(HEADS_PER_STEP):
                mine = [i for i, ch in enumerate(CHAINS) if ch[0] == hh]
                dv = dk = None
                for i in mine:
                    q, do, _ = ctx[i]
                    _, _, _, ab, dzb = st[i]
                    dv = _dot(ab, do, TN) if dv is None else dv + _dot(ab, do, TN)
                    dk = _dot(dzb, q, TN) if dk is None else dk + _dot(dzb, q, TN)
                dv_ref[0, hh, kblock(kj), :] += dv
                dk_ref[0, hh, kblock(kj), :] += dk

        def queries(ch, ctx, kj, masked, st):
            cpre, pg, dq, _, dzb = st
            return cpre, pg, dq + _dot(dzb, k_ref[0, ch[0], kblock(kj), :])

        def finish(ch, ctx, qi, carry):
            dq_ref[0, ch[0], qrows(qi, ch[1]), :] = carry[2]

        _causal_blocks(nq, prep, init, [scores, logs, prefix, weights, grad_prefix, dlogits, queries], finish,
                       combine=combine)

    hs = _head_spec(S)
    out = jax.ShapeDtypeStruct((B, H, S, LANES), F32)
    return _call(
        body, name=name, grid=(B, H // HEADS_PER_STEP), in_specs=[hs, hs, hs, hs, hs], out_specs=[hs, hs, hs],
        out_shape=[out, out, out], compiler_params=_params(),
    )(qa, ka, va, doa, ol)


def _row_tile(R, C, n_arrays):
    budget = 24 * 1024 * 1024 // (2 * n_arrays * 4 * max(C, LANES))
    return _tile(R, max(8, budget), 8)


def _ew_sum(parts, name, also_bf16=False):
    R, C = parts[0].shape
    tr = _row_tile(R, C, len(parts) + 2)
    n = len(parts)

    def body(*refs):
        acc = refs[0][...].astype(F32) + refs[1][...].astype(F32)
        for r in refs[2:n]:
            acc = acc + r[...].astype(F32)
        refs[n][...] = acc
        if also_bf16:
            refs[n + 1][...] = acc.astype(BF16)

    blk = pl.BlockSpec((tr, C), lambda i: (i, 0))
    out_shape = [jax.ShapeDtypeStruct((R, C), F32)] + ([jax.ShapeDtypeStruct((R, C), BF16)] if also_bf16 else [])
    return _call(
        body, name=name, grid=(R // tr,), in_specs=[blk] * n, out_specs=[blk] * len(out_shape),
        out_shape=out_shape, compiler_params=_params(),
    )(*parts)


def _adamw(w, g, m, v, name):
    R, C = w.shape
    tr = _row_tile(R, C, 7)
    c1 = 1.0 / (1.0 - ADAM_B1 ** ADAM_STEP)
    c2 = 1.0 / (1.0 - ADAM_B2 ** ADAM_STEP)

    def body(w_ref, g_ref, m_ref, v_ref, d_ref, m2_ref, v2_ref):
        gv = g_ref[...]
        m2 = ADAM_B1 * m_ref[...] + (1.0 - ADAM_B1) * gv
        v2 = ADAM_B2 * v_ref[...] + (1.0 - ADAM_B2) * (gv * gv)
        m2_ref[...] = m2
        v2_ref[...] = v2
        d_ref[...] = -ADAM_LR * ((m2 * c1) / (jnp.sqrt(v2 * c2) + ADAM_EPS) + ADAM_WD * w_ref[...])

    blk = pl.BlockSpec((tr, C), lambda i: (i, 0))
    out = jax.ShapeDtypeStruct((R, C), F32)
    return _call(
        body, name=name, grid=(R // tr,), in_specs=[blk] * 4, out_specs=[blk] * 3, out_shape=[out] * 3,
        compiler_params=_params(),
    )(w, g, m, v)


def _me():
    return lax.axis_index("x"), lax.axis_index("y"), lax.axis_index("c")


def _chip_of(x, y):
    return 2 * x + y


def _other_chips(x, y):
    return [(x, 1 - y), (1 - x, y), (1 - x, 1 - y)]


def _gather_weights(halves, smalls):
    nh, ns = len(halves), len(smalls)

    def body(*refs):
        ins_h, ins_s = refs[:nh], refs[nh:nh + ns]
        outs_h, outs_s = refs[nh + ns:2 * nh + ns], refs[2 * nh + ns:2 * (nh + ns)]
        send1, recv1, send2, recv2, send3, recv3 = refs[2 * (nh + ns):]
        x, y, c = _me()
        mine = _chip_of(x, y)
        chips = _other_chips(x, y)
        sib = (x, y, 1 - c)

        def landed(i, k, half):
            return outs_h[i].at[_chip_of(*chips[k]), half]

        def first(i, k):
            return pltpu.make_async_remote_copy(
                src_ref=ins_h[i].at[c], dst_ref=outs_h[i].at[mine, c], send_sem=send1.at[i, k], recv_sem=recv1.at[i, k],
                device_id=(*chips[k], c), device_id_type=MESH)

        def passed(i, k):
            return pltpu.make_async_remote_copy(
                src_ref=landed(i, k, c), dst_ref=landed(i, k, c), send_sem=send2.at[i, k], recv_sem=recv2.at[i, k],
                device_id=sib, device_id_type=MESH)

        def small(i, k):
            return pltpu.make_async_remote_copy(
                src_ref=ins_s[i], dst_ref=outs_s[i].at[mine], send_sem=send3.at[i, k], recv_sem=recv3.at[i, k],
                device_id=(*chips[k], c), device_id_type=MESH)

        for i in range(nh):
            for k in range(3):
                first(i, k).start()
        for i in range(ns):
            for k in range(3):
                small(i, k).start()
        for i in range(nh):
            for k in range(3):
                pltpu.make_async_remote_copy(
                    src_ref=ins_h[i].at[c], dst_ref=landed(i, k, c), send_sem=send1.at[i, k], recv_sem=recv1.at[i, k],
                    device_id=(*chips[k], c), device_id_type=MESH).wait_recv()
                passed(i, k).start()
        for i in range(nh):
            for k in range(3):
                pltpu.make_async_remote_copy(
                    src_ref=landed(i, k, c), dst_ref=landed(i, k, 1 - c), send_sem=send2.at[i, k],
                    recv_sem=recv2.at[i, k], device_id=sib, device_id_type=MESH).wait_recv()
        for i in range(ns):
            for k in range(3):
                pltpu.make_async_remote_copy(
                    src_ref=ins_s[i], dst_ref=outs_s[i].at[_chip_of(*chips[k])], send_sem=send3.at[i, k],
                    recv_sem=recv3.at[i, k], device_id=(*chips[k], c), device_id_type=MESH).wait_recv()
        for i in range(nh):
            for k in range(3):
                first(i, k).wait_send()
                passed(i, k).wait_send()
        for i in range(ns):
            for k in range(3):
                small(i, k).wait_send()

    out_shape = ([jax.ShapeDtypeStruct((4,) + a.shape, a.dtype) for a in halves]
                 + [jax.ShapeDtypeStruct((4,) + a.shape, a.dtype) for a in smalls])
    n = nh + ns
    res = _call(
        body, name="gather_weights", in_specs=[HBM] * n, out_specs=[HBM] * n, out_shape=out_shape,
        scratch_shapes=[pltpu.SemaphoreType.DMA((nh, 3)), pltpu.SemaphoreType.DMA((nh, 3)),
                        pltpu.SemaphoreType.DMA((nh, 3)), pltpu.SemaphoreType.DMA((nh, 3)),
                        pltpu.SemaphoreType.DMA((max(ns, 1), 3)), pltpu.SemaphoreType.DMA((max(ns, 1), 3))],
        compiler_params=_params(),
    )(*halves, *smalls)
    return res[:nh], res[nh:]


def _pair_exchange(grads):
    n = len(grads)

    def body(*refs):
        ins, got = refs[:n], refs[n:2 * n]
        send, recv = refs[2 * n:]
        x, y, c = _me()
        cps = []
        for i in range(n):
            for j in range(4):
                r = pltpu.make_async_remote_copy(
                    src_ref=ins[i].at[j, 1 - c], dst_ref=got[i].at[j], send_sem=send.at[i, j], recv_sem=recv.at[i, j],
                    device_id=(x, y, 1 - c), device_id_type=MESH)
                r.start()
                cps.append(r)
        for r in cps:
            r.wait()

    return _call(
        body, name="grad_pair_exchange", in_specs=[HBM] * n, out_specs=[HBM] * n,
        out_shape=[jax.ShapeDtypeStruct((4,) + g.shape[2:], g.dtype) for g in grads],
        scratch_shapes=[pltpu.SemaphoreType.DMA((n, 4)), pltpu.SemaphoreType.DMA((n, 4))],
        compiler_params=_params(),
    )(*grads)


def _chip_exchange(sums):
    n = len(sums)

    def body(*refs):
        ins, got = refs[:n], refs[n:2 * n]
        send, recv = refs[2 * n:]
        x, y, c = _me()
        chips = _other_chips(x, y)
        cps = []
        for i in range(n):
            for k in range(3):
                r = pltpu.make_async_remote_copy(
                    src_ref=ins[i].at[_chip_of(*chips[k])], dst_ref=got[i].at[k], send_sem=send.at[i, k],
                    recv_sem=recv.at[i, k], device_id=(*chips[k], c), device_id_type=MESH)
                r.start()
                cps.append(r)
        for r in cps:
            r.wait()

    return _call(
        body, name="grad_chip_exchange", in_specs=[HBM] * n, out_specs=[HBM] * n,
        out_shape=[jax.ShapeDtypeStruct((3,) + s.shape[1:], s.dtype) for s in sums],
        scratch_shapes=[pltpu.SemaphoreType.DMA((n, 3)), pltpu.SemaphoreType.DMA((n, 3))],
        compiler_params=_params(),
    )(*sums)


def _pair_share(halves):
    n = len(halves)

    def body(*refs):
        ins, outs = refs[:n], refs[n:2 * n]
        send, recv = refs[2 * n:]
        x, y, c = _me()
        cps = []
        for i in range(n):
            r = pltpu.make_async_remote_copy(
                src_ref=ins[i], dst_ref=outs[i], send_sem=send.at[i], recv_sem=recv.at[i],
                device_id=(x, y, 1 - c), device_id_type=MESH)
            r.start()
            cps.append(r)
        for r in cps:
            r.wait()

    return _call(
        body, name="grad_pair_share", in_specs=[HBM] * n, out_specs=[HBM] * n,
        out_shape=[jax.ShapeDtypeStruct(h.shape, h.dtype) for h in halves],
        scratch_shapes=[pltpu.SemaphoreType.DMA((n,)), pltpu.SemaphoreType.DMA((n,))],
        compiler_params=_params(),
    )(*halves)


def _allreduce_small(vec):
    P = vec.shape[1]

    def body(v_ref, sum_ref, all_ref, send, recv):
        x, y, c = _me()
        me = 4 * x + 2 * y + c
        all_ref[pl.ds(me, 1)] = v_ref[...][None]
        cps = []
        for d in range(1, 8):
            peer = (jnp.bitwise_xor(x, d >> 2), jnp.bitwise_xor(y, (d >> 1) & 1), jnp.bitwise_xor(c, d & 1))
            r = pltpu.make_async_remote_copy(
                src_ref=v_ref, dst_ref=all_ref.at[me], send_sem=send.at[d - 1], recv_sem=recv.at[d - 1],
                device_id=peer, device_id_type=MESH)
            r.start()
            cps.append(r)
        for d in range(1, 8):
            src = jnp.bitwise_xor(me, d)
            pltpu.make_async_remote_copy(
                src_ref=v_ref, dst_ref=all_ref.at[src], send_sem=send.at[d - 1], recv_sem=recv.at[d - 1],
                device_id=(x, y, c), device_id_type=MESH).wait_recv()
        for r in cps:
            r.wait_send()
        acc = all_ref[0]
        for i in range(1, 8):
            acc = acc + all_ref[i]
        sum_ref[...] = acc

    vm = pl.BlockSpec(memory_space=pltpu.VMEM)
    return _call(
        body, name="allreduce_small", in_specs=[vm], out_specs=[vm, vm],
        out_shape=[jax.ShapeDtypeStruct((8, P), F32), jax.ShapeDtypeStruct((8, 8, P), F32)],
        scratch_shapes=[pltpu.SemaphoreType.DMA((7,)), pltpu.SemaphoreType.DMA((7,))],
        compiler_params=_params(),
    )(vec)[0]


def _to_heads(t, B, S):
    H = t.shape[1] // HEAD_DIM
    t = t.astype(BF16).reshape(B, S, H, HEAD_DIM).transpose(0, 2, 1, 3)
    pad = jnp.zeros((B, H, S, LANES - HEAD_DIM), BF16)
    return jnp.concatenate([t, pad], axis=-1)


def _from_heads(t):
    B, H, S, _ = t.shape
    return t[..., :HEAD_DIM].transpose(0, 2, 1, 3).reshape(B * S, H * HEAD_DIM)


def _per_batch(mod, B, D):
    return [mod[:B, i * D:(i + 1) * D].reshape(B, 1, D) for i in range(3)]


def _pad_rows8(a):
    return jnp.concatenate([a, jnp.zeros((8 - a.shape[0],) + a.shape[1:], a.dtype)], axis=0)


def _layer_fwd(x, c8, w, S, fox, tag):
    T, D = x.shape
    B = T // S
    DI = w["w_out"].shape[0]
    H = DI // HEAD_DIM
    tq = _tile(S, ATT_BLOCK, 8)
    mod = _mod_fwd(c8, w["w_ada"], w["b_ada"], tag + "_mod_fwd")
    shift, scale, gate = _per_batch(mod, B, D)
    proj, h = _ln_proj(x, shift, scale, w["norm_g"], w["w_in"], S, tag + "_ln_proj")
    qa = _to_heads(proj[:, :DI] * 0.125, B, S)
    ka = _to_heads(proj[:, DI:2 * DI], B, S)
    va = _to_heads(proj[:, 2 * DI:3 * DI], B, S)
    saved = dict(x=x, h=h, proj=proj, qa=qa, ka=ka, va=va, scale=scale, gate=gate)
    if fox:
        fl = _mm(h, w["w_f"], "nn", F32, tag + "_flogit").reshape(B, S, LANES)
        cum = _cum_fwd(fl, w["b_f"], tag + "_cum_fwd")
        cumrow = cum[:, :, :H].transpose(0, 2, 1).reshape(B, H, S // tq, 1, tq)
        ol = _fox_fwd(qa, ka, va, cum, cumrow, tag + "_attn_fwd")
        saved.update(fl=fl, cum=cum, cumrow=cumrow)
    else:
        ol = _sb_fwd(qa, ka, va, tq, tag + "_attn_fwd")
    o = _from_heads(ol)
    xo, y, u = _gate_out(o, proj, w["w_out"], x, gate, S, tag + "_gate_out")
    saved.update(ol=ol, o=o, y=y, u=u)
    return xo, saved


def _layer_bwd(dxo, sv, w, cT, S, fox, tag):
    T, D = dxo.shape
    B = T // S
    DI = w["w_out"].shape[0]
    H = DI // HEAD_DIM
    tq = _tile(S, ATT_BLOCK, 8)
    dy, do, dzg, dgate = _out_bwd(dxo, sv["y"], sv["gate"], w["w_out"], sv["o"], sv["proj"], S, tag + "_out_bwd")
    g = {"w_out": _mm(sv["u"], dy, "tn", F32, tag + "_dw_out", tm=1024, tn=1024)}
    doa = _to_heads(do, B, S)
    if fox:
        dqa, dka, dva, dcs = _fox_bwd(sv["qa"], sv["ka"], sv["va"], doa, sv["ol"], sv["cum"], sv["cumrow"],
                                      tag + "_attn_bwd")
        dcs = dcs.reshape(B, H, S).transpose(0, 2, 1)
        dcs = jnp.concatenate([dcs, jnp.zeros((B, S, LANES - H), F32)], axis=-1)
        dfl, db_f = _cum_bwd(dcs, sv["fl"], w["b_f"], tag + "_cum_bwd")
        g["b_f"] = db_f[:, :H]
        tail = [dfl.reshape(T, LANES).astype(BF16)]
        w_in = jnp.concatenate([w["w_in"], w["w_f"]], axis=1)
    else:
        dqa, dka, dva = _sb_bwd(sv["qa"], sv["ka"], sv["va"], doa, sv["ol"], tq, tag + "_attn_bwd")
        tail = []
        w_in = w["w_in"]
    dproj = jnp.concatenate([(_from_heads(dqa) * 0.125).astype(BF16), _from_heads(dka).astype(BF16),
                             _from_heads(dva).astype(BF16), dzg] + tail, axis=1)
    N = dproj.shape[1]
    dw_in = _mm(sv["h"], dproj, "tn", F32, tag + "_dw_in", tm=1024, tn=640 if N % 640 == 0 else 512)
    g["w_in"] = dw_in[:, :4 * DI + H] if fox else dw_in
    dh = _mm(dproj, w_in, "nt", F32, tag + "_dh", tm=512, tn=1024, tk=640 if N % 640 == 0 else 512)
    dx, dshift, dscale, dg = _ln_bwd(dh, sv["x"], dxo, sv["scale"], w["norm_g"], S, tag + "_ln_bwd")
    g["norm_g"] = dg
    dmod = jnp.concatenate([dshift, dscale, dgate], axis=-1).reshape(B, 3 * D)
    g["w_ada"], g["b_ada"] = _mod_bwd(cT, _pad_rows8(dmod), B, tag + "_mod_bwd")
    return dx, g


def _local_step(x3, c, tgt3, wf, ws, final_g):
    B, S, D = x3.shape
    T = B * S
    x = x3.reshape(T, D)
    c8 = _pad_rows8(c)
    cT = c8.T
    x1, sv1 = _layer_fwd(x, c8, wf, S, True, "fox")
    x2, sv2 = _layer_fwd(x1, c8, ws, S, False, "sb")
    dx2, dgf, loss = _final_loss(x2, tgt3.reshape(T, D), final_g, S, "final_loss")
    dx1, gs = _layer_bwd(dx2, sv2, ws, cT, S, False, "sb")
    dx0, gf = _layer_bwd(dx1, sv1, wf, cT, S, True, "fox")
    return loss, dx0.reshape(B, S, D), gf, gs, dgf


def _cols_to_shards(a):
    R, C4 = a.shape
    return a.reshape(R, 4, C4 // 4).transpose(1, 0, 2)


def _shards_to_cols(a):
    n, R, C = a.shape
    return a.transpose(1, 0, 2).reshape(R, n * C)


def kernel(x, c, fox_norm_g, fox_w_ada, fox_b_ada, fox_w_in, fox_b_f, fox_w_out, sb_norm_g, sb_w_ada, sb_b_ada, sb_w_in, sb_w_out, final_norm_g, loss_target, m_fox_norm_g, m_fox_w_ada, m_fox_b_ada, m_fox_w_in, m_fox_b_f, m_fox_w_out, m_sb_norm_g, m_sb_w_ada, m_sb_b_ada, m_sb_w_in, m_sb_w_out, m_final_norm_g, v_fox_norm_g, v_fox_w_ada, v_fox_b_ada, v_fox_w_in, v_fox_b_f, v_fox_w_out, v_sb_norm_g, v_sb_w_ada, v_sb_b_ada, v_sb_w_in, v_sb_w_out, v_final_norm_g):
    B, S, D = x.shape
    DI = 4 * fox_w_out.shape[1]
    H = DI // HEAD_DIM
    chip = _chip_of(lax.axis_index("x"), lax.axis_index("y"))

    big_names = ["fox_w_ada", "fox_w_in", "fox_w_out", "sb_w_ada", "sb_w_in", "sb_w_out"]
    big = dict(fox_w_ada=fox_w_ada[0], fox_w_in=fox_w_in[0], fox_w_out=fox_w_out[0],
               sb_w_ada=sb_w_ada[0], sb_w_in=sb_w_in[0], sb_w_out=sb_w_out[0])
    halves = [big[n].astype(BF16).reshape(2, big[n].shape[0] // 2, big[n].shape[1]) for n in big_names]
    gathered, gsmall = _gather_weights(halves, [sb_norm_g, sb_b_ada])
    gathered = [lax.dynamic_update_index_in_dim(a, own, chip, 0) for a, own in zip(gathered, halves)]
    gsmall = [lax.dynamic_update_index_in_dim(a, own, chip, 0) for a, own in zip(gsmall, [sb_norm_g, sb_b_ada])]
    full = {}
    for n, a in zip(big_names, gathered):
        a = a.reshape(4, a.shape[1] * a.shape[2], a.shape[3])
        full[n] = a.reshape(4 * a.shape[1], a.shape[2]) if n.endswith("w_out") else _shards_to_cols(a)
    sb_norm_full = gsmall[0].reshape(1, D)
    sb_b_ada_full = gsmall[1].reshape(1, 3 * D)
    w_f = jnp.concatenate([full["fox_w_in"][:, 4 * DI:], jnp.zeros((D, LANES - H), BF16)], axis=1)
    b_f = jnp.concatenate([fox_b_f, jnp.zeros((1, LANES - H), F32)], axis=1)
    wf = dict(w_ada=full["fox_w_ada"], b_ada=fox_b_ada, norm_g=fox_norm_g, w_in=full["fox_w_in"][:, :4 * DI],
              w_f=w_f, b_f=b_f, w_out=full["fox_w_out"])
    ws = dict(w_ada=full["sb_w_ada"], b_ada=sb_b_ada_full, norm_g=sb_norm_full, w_in=full["sb_w_in"],
              w_out=full["sb_w_out"])

    loss, grad_x, gf, gs, dgf = _local_step(x, c, loss_target, wf, ws, final_norm_g.reshape(1, D))

    part = dict(fox_w_ada=gf["w_ada"], fox_w_in=gf["w_in"], fox_w_out=gf["w_out"],
                sb_w_ada=gs["w_ada"], sb_w_in=gs["w_in"], sb_w_out=gs["w_out"])
    shard_major = []
    for n in big_names:
        a = part[n]
        a = a.reshape(4, a.shape[0] // 4, a.shape[1]) if n.endswith("w_out") else _cols_to_shards(a)
        shard_major.append(a.reshape(4, 2, a.shape[1] // 2, a.shape[2]))
    core = lax.axis_index("c")
    got = _pair_exchange(shard_major)
    pair_f32, pair_bf16 = [], []
    for n, g4, b in zip(big_names, shard_major, got):
        a = lax.dynamic_index_in_dim(g4, core, axis=1, keepdims=False)
        r, C = a.shape[1:]
        s32, s16 = _ew_sum([a.reshape(4 * r, C), b.reshape(4 * r, C)], n + "_pair_sum", also_bf16=True)
        pair_f32.append(s32.reshape(4, r, C))
        pair_bf16.append(s16.reshape(4, r, C))
    others = _chip_exchange(pair_bf16)
    reduced_halves = [_ew_sum([lax.dynamic_index_in_dim(a, chip, axis=0, keepdims=False), b[0], b[1], b[2]],
                              n + "_chip_sum")[0] for n, a, b in zip(big_names, pair_f32, others)]
    theirs = _pair_share(reduced_halves)
    grad_big = {}
    for n, a, b in zip(big_names, reduced_halves, theirs):
        grad_big[n] = jnp.concatenate([jnp.where(core == 0, a, b), jnp.where(core == 0, b, a)], axis=0)

    pieces = [loss, gf["norm_g"], gf["b_ada"], jnp.concatenate([gf["b_f"], jnp.zeros((1, LANES - H), F32)], axis=1),
              gs["norm_g"], gs["b_ada"], dgf]
    vec = jnp.concatenate(pieces, axis=1)
    red = _allreduce_small(_pad_rows8(vec))[0:1]
    offs = [0]
    for p in pieces:
        offs.append(offs[-1] + p.shape[1])
    r_loss, r_fng, r_fba, r_fbf, r_sng, r_sba, r_fin = [red[:, offs[i]:offs[i + 1]] for i in range(7)]
    small_grads = dict(
        fox_norm_g=r_fng, fox_b_ada=r_fba, fox_b_f=r_fbf[:, :H],
        sb_norm_g=lax.dynamic_slice_in_dim(r_sng, chip * (D // 4), D // 4, axis=1),
        sb_b_ada=lax.dynamic_slice_in_dim(r_sba, chip * (3 * D // 4), 3 * D // 4, axis=1),
        final_norm_g=r_fin)

    weights = dict(fox_norm_g=fox_norm_g, fox_w_ada=fox_w_ada, fox_b_ada=fox_b_ada, fox_w_in=fox_w_in, fox_b_f=fox_b_f,
                   fox_w_out=fox_w_out, sb_norm_g=sb_norm_g, sb_w_ada=sb_w_ada, sb_b_ada=sb_b_ada, sb_w_in=sb_w_in,
                   sb_w_out=sb_w_out, final_norm_g=final_norm_g)
    ms = dict(fox_norm_g=m_fox_norm_g, fox_w_ada=m_fox_w_ada, fox_b_ada=m_fox_b_ada, fox_w_in=m_fox_w_in,
              fox_b_f=m_fox_b_f, fox_w_out=m_fox_w_out, sb_norm_g=m_sb_norm_g, sb_w_ada=m_sb_w_ada,
              sb_b_ada=m_sb_b_ada, sb_w_in=m_sb_w_in, sb_w_out=m_sb_w_out, final_norm_g=m_final_norm_g)
    vs = dict(fox_norm_g=v_fox_norm_g, fox_w_ada=v_fox_w_ada, fox_b_ada=v_fox_b_ada, fox_w_in=v_fox_w_in,
              fox_b_f=v_fox_b_f, fox_w_out=v_fox_w_out, sb_norm_g=v_sb_norm_g, sb_w_ada=v_sb_w_ada,
              sb_b_ada=v_sb_b_ada, sb_w_in=v_sb_w_in, sb_w_out=v_sb_w_out, final_norm_g=v_final_norm_g)
    order = ["fox_norm_g", "fox_w_ada", "fox_b_ada", "fox_w_in", "fox_b_f", "fox_w_out", "sb_norm_g", "sb_w_ada",
             "sb_b_ada", "sb_w_in", "sb_w_out", "final_norm_g"]
    grads, deltas, new_m, new_v = {}, {}, {}, {}
    for n in big_names:
        shp = weights[n].shape
        g2 = grad_big[n]
        d, m2, v2 = _adamw(weights[n][0], g2, ms[n][0], vs[n][0], n + "_adamw")
        grads[n], deltas[n], new_m[n], new_v[n] = g2.reshape(shp), d.reshape(shp), m2.reshape(shp), v2.reshape(shp)
    small_names = [n for n in order if n not in big_names]
    sizes = [small_grads[n].shape[1] for n in small_names]
    total = sum(sizes)
    padn = (-total) % LANES

    def pack(d):
        return jnp.concatenate([d[n].reshape(1, -1) for n in small_names] + [jnp.ones((1, padn), F32)], axis=1)

    sd, sm, sv_ = _adamw(pack(weights), pack(small_grads), pack(ms), pack(vs), "small_adamw")
    o = 0
    for n, sz in zip(small_names, sizes):
        shp = weights[n].shape
        grads[n] = small_grads[n].reshape(shp)
        deltas[n], new_m[n], new_v[n] = (t[:, o:o + sz].reshape(shp) for t in (sd, sm, sv_))
        o += sz
    return (r_loss[0, 0], grad_x, *[grads[n] for n in order], *[deltas[n] for n in order],
            *[new_m[n] for n in order], *[new_v[n] for n in order])
```

```python
import functools

import jax
import jax.numpy as jnp
from jax import lax
from jax.experimental import pallas as pl
from jax.experimental.pallas import tpu as pltpu

F32 = jnp.float32
BF16 = jnp.bfloat16
HEAD_DIM = 64
LANES = 128
NORM_EPS = 1e-6
ADAM_LR = 0.001
ADAM_B1 = 0.9
ADAM_B2 = 0.999
ADAM_EPS = 1e-08
ADAM_WD = 0.01
ADAM_STEP = 10
VMEM_LIMIT = 56 * 1024 * 1024
ATT_BLOCK = 256
MESH = pl.DeviceIdType.MESH
HBM = pl.BlockSpec(memory_space=pltpu.HBM)
NT = (((1,), (1,)), ((), ()))
TN = (((0,), (0,)), ((), ()))


def _call(body, **kw):
    return pl.pallas_call(body, **kw)


def _params(**kw):
    return pltpu.CompilerParams(vmem_limit_bytes=VMEM_LIMIT, **kw)


def _tile(dim, pref, mult=128):
    if dim <= pref:
        return dim
    t = (pref // mult) * mult
    while t >= mult:
        if dim % t == 0:
            return t
        t -= mult
    return dim


def _sigmoid(x):
    return 1.0 / (1.0 + jnp.exp(-x))


def _split3(x):
    hi = x.astype(BF16)
    r = x - hi.astype(F32)
    mid = r.astype(BF16)
    lo = (r - mid.astype(F32)).astype(BF16)
    return hi, mid, lo


def _mm(a, b, mode, out_dtype, name, tm=512, tn=512, tk=512):
    if mode == "nn":
        (M, K), (_, N) = a.shape, b.shape
    elif mode == "nt":
        (M, K), (N, _) = a.shape, b.shape
    else:
        (K, M), (_, N) = a.shape, b.shape
    tm, tn, tk = _tile(M, tm), _tile(N, tn), _tile(K, tk)
    nk = K // tk
    dims = {"nn": (((1,), (0,)), ((), ())), "nt": NT, "tn": TN}[mode]

    def body(a_ref, b_ref, o_ref, acc_ref):
        k = pl.program_id(2)

        @pl.when(k == 0)
        def _():
            acc_ref[...] = jnp.zeros_like(acc_ref)

        acc_ref[...] += lax.dot_general(a_ref[...], b_ref[...], dims, preferred_element_type=F32)

        @pl.when(k == nk - 1)
        def _():
            o_ref[...] = acc_ref[...].astype(out_dtype)

    a_spec = (pl.BlockSpec((tk, tm), lambda i, j, k: (k, i)) if mode == "tn"
              else pl.BlockSpec((tm, tk), lambda i, j, k: (i, k)))
    b_spec = (pl.BlockSpec((tn, tk), lambda i, j, k: (j, k)) if mode == "nt"
              else pl.BlockSpec((tk, tn), lambda i, j, k: (k, j)))
    return _call(
        body, name=name, grid=(M // tm, N // tn, nk),
        in_specs=[a_spec, b_spec], out_specs=pl.BlockSpec((tm, tn), lambda i, j, k: (i, j)),
        out_shape=jax.ShapeDtypeStruct((M, N), out_dtype),
        scratch_shapes=[pltpu.VMEM((tm, tn), F32)], compiler_params=_params(),
    )(a, b)


def _mod_fwd(c8, w_ada, b_ada, name):
    D, N = w_ada.shape
    tn = _tile(N, 512)

    def body(c_ref, w_ref, b_ref, o_ref):
        c = c_ref[...]
        sc = (c * _sigmoid(c)).astype(BF16)
        o_ref[...] = jnp.dot(sc, w_ref[...], preferred_element_type=F32) + b_ref[...]

    return _call(
        body, name=name, grid=(N // tn,),
        in_specs=[pl.BlockSpec((8, D), lambda j: (0, 0)), pl.BlockSpec((D, tn), lambda j: (0, j)),
                  pl.BlockSpec((1, tn), lambda j: (0, j))],
        out_specs=pl.BlockSpec((8, tn), lambda j: (0, j)),
        out_shape=jax.ShapeDtypeStruct((8, N), F32), compiler_params=_params(),
    )(c8, w_ada, b_ada)


def _mod_bwd(cT, dmod8, nb, name):
    D = cT.shape[0]
    N = dmod8.shape[1]
    tn = _tile(N, 512)

    def body(c_ref, d_ref, w_ref, b_ref):
        c = c_ref[...]
        sc = c * _sigmoid(c)
        d = d_ref[...]
        acc = sc[:, 0:1] * d[0:1, :]
        bsum = d[0:1, :]
        for b in range(1, nb):
            acc = acc + sc[:, b:b + 1] * d[b:b + 1, :]
            bsum = bsum + d[b:b + 1, :]
        w_ref[...] = acc
        b_ref[...] = bsum

    return _call(
        body, name=name, grid=(N // tn,),
        in_specs=[pl.BlockSpec((D, 8), lambda j: (0, 0)), pl.BlockSpec((8, tn), lambda j: (0, j))],
        out_specs=[pl.BlockSpec((D, tn), lambda j: (0, j)), pl.BlockSpec((1, tn), lambda j: (0, j))],
        out_shape=[jax.ShapeDtypeStruct((D, N), F32), jax.ShapeDtypeStruct((1, N), F32)],
        compiler_params=_params(),
    )(cT, dmod8)


def _ln_proj(x, shift, scale, g, w, S, name):
    T, D = x.shape
    N = w.shape[1]
    tm = _tile(S, 512)
    tn = _tile(N, 1024)
    per_b = S // tm

    def body(x_ref, sh_ref, sc_ref, g_ref, w_ref, p_ref, h_ref):
        @pl.when(pl.program_id(1) == 0)
        def _():
            xv = x_ref[...]
            r = lax.rsqrt(jnp.mean(xv * xv, axis=-1, keepdims=True) + NORM_EPS)
            h = (xv * r) * g_ref[...] * (1.0 + sc_ref[0]) + sh_ref[0]
            h_ref[...] = h.astype(BF16)

        p_ref[...] = jnp.dot(h_ref[...], w_ref[...], preferred_element_type=F32).astype(BF16)

    return _call(
        body, name=name, grid=(T // tm, N // tn),
        in_specs=[pl.BlockSpec((tm, D), lambda i, j: (i, 0)),
                  pl.BlockSpec((1, 1, D), lambda i, j: (i // per_b, 0, 0)),
                  pl.BlockSpec((1, 1, D), lambda i, j: (i // per_b, 0, 0)),
                  pl.BlockSpec((1, D), lambda i, j: (0, 0)),
                  pl.BlockSpec((D, tn), lambda i, j: (0, j))],
        out_specs=[pl.BlockSpec((tm, tn), lambda i, j: (i, j)), pl.BlockSpec((tm, D), lambda i, j: (i, 0))],
        out_shape=[jax.ShapeDtypeStruct((T, N), BF16), jax.ShapeDtypeStruct((T, D), BF16)],
        compiler_params=_params(),
    )(x, shift, scale, g, w)


def _ln_bwd(dh, x, dxo, scale, g, S, name):
    T, D = x.shape
    B = T // S
    tm = _tile(S, 512)
    per_b = S // tm

    def body(dh_ref, x_ref, dxo_ref, sc_ref, g_ref, dx_ref, dsh_ref, dsc_ref, dg_ref):
        i = pl.program_id(0)
        xv = x_ref[...]
        dh_v = dh_ref[...]
        r = lax.rsqrt(jnp.mean(xv * xv, axis=-1, keepdims=True) + NORM_EPS)
        xn = xv * r
        gv = g_ref[...]
        one_sc = 1.0 + sc_ref[0]
        dhxn = dh_v * xn

        @pl.when(i % per_b == 0)
        def _():
            dsh_ref[...] = jnp.zeros_like(dsh_ref)
            dsc_ref[...] = jnp.zeros_like(dsc_ref)

        @pl.when(i == 0)
        def _():
            dg_ref[...] = jnp.zeros_like(dg_ref)

        dsh_ref[0] += jnp.sum(dh_v, axis=0, keepdims=True)
        dsc_ref[0] += jnp.sum(dhxn, axis=0, keepdims=True) * gv
        dg_ref[...] += jnp.sum(dhxn, axis=0, keepdims=True) * one_sc
        dxn = dh_v * (gv * one_sc)
        dx_ref[...] = r * (dxn - xn * jnp.mean(dxn * xn, axis=-1, keepdims=True)) + dxo_ref[...]

    row = pl.BlockSpec((tm, D), lambda i: (i, 0))
    per = pl.BlockSpec((1, 1, D), lambda i: (i // per_b, 0, 0))
    vec = pl.BlockSpec((1, D), lambda i: (0, 0))
    return _call(
        body, name=name, grid=(T // tm,),
        in_specs=[row, row, row, per, vec], out_specs=[row, per, per, vec],
        out_shape=[jax.ShapeDtypeStruct((T, D), F32), jax.ShapeDtypeStruct((B, 1, D), F32),
                   jax.ShapeDtypeStruct((B, 1, D), F32), jax.ShapeDtypeStruct((1, D), F32)],
        compiler_params=_params(),
    )(dh, x, dxo, scale, g)


def _gate_out(o, proj, w_out, x, gate, S, name):
    T, DI = o.shape
    D = w_out.shape[1]
    tm = _tile(S, 256)
    per_b = S // tm

    def body(o_ref, z_ref, w_ref, x_ref, g_ref, xo_ref, y_ref, u_ref):
        z = z_ref[...].astype(F32)
        u = (o_ref[...] * (z * _sigmoid(z))).astype(BF16)
        u_ref[...] = u
        y = jnp.dot(u, w_ref[...], preferred_element_type=F32)
        y_ref[...] = y
        xo_ref[...] = x_ref[...] + g_ref[0] * y

    wide = pl.BlockSpec((tm, DI), lambda i: (i, 0))
    row = pl.BlockSpec((tm, D), lambda i: (i, 0))
    return _call(
        body, name=name, grid=(T // tm,),
        in_specs=[wide, pl.BlockSpec((tm, DI), lambda i: (i, 3)), pl.BlockSpec((DI, D), lambda i: (0, 0)), row,
                  pl.BlockSpec((1, 1, D), lambda i: (i // per_b, 0, 0))],
        out_specs=[row, row, wide],
        out_shape=[jax.ShapeDtypeStruct((T, D), F32), jax.ShapeDtypeStruct((T, D), F32),
                   jax.ShapeDtypeStruct((T, DI), BF16)],
        compiler_params=_params(),
    )(o, proj, w_out, x, gate)


def _out_bwd(dxo, y, gate, w_out, o, proj, S, name):
    T, D = dxo.shape
    DI = o.shape[1]
    B = T // S
    tm = _tile(S, 256)
    per_b = S // tm

    def body(dxo_ref, y_ref, g_ref, w_ref, o_ref, z_ref, dy_ref, do_ref, dz_ref, dg_ref):
        dxo_v = dxo_ref[...]
        dy = (dxo_v * g_ref[0]).astype(BF16)
        dy_ref[...] = dy
        du = lax.dot_general(dy, w_ref[...], NT, preferred_element_type=F32)
        z = z_ref[...].astype(F32)
        sg = _sigmoid(z)
        do_ref[...] = (du * (z * sg)).astype(BF16)
        dz_ref[...] = (du * o_ref[...] * (sg * (1.0 + z * (1.0 - sg)))).astype(BF16)

        @pl.when(pl.program_id(0) % per_b == 0)
        def _():
            dg_ref[...] = jnp.zeros_like(dg_ref)

        dg_ref[0] += jnp.sum(dxo_v * y_ref[...], axis=0, keepdims=True)

    wide = pl.BlockSpec((tm, DI), lambda i: (i, 0))
    row = pl.BlockSpec((tm, D), lambda i: (i, 0))
    per = pl.BlockSpec((1, 1, D), lambda i: (i // per_b, 0, 0))
    return _call(
        body, name=name, grid=(T // tm,),
        in_specs=[row, row, per, pl.BlockSpec((DI, D), lambda i: (0, 0)), wide,
                  pl.BlockSpec((tm, DI), lambda i: (i, 3))],
        out_specs=[row, wide, wide, per],
        out_shape=[jax.ShapeDtypeStruct((T, D), BF16), jax.ShapeDtypeStruct((T, DI), BF16),
                   jax.ShapeDtypeStruct((T, DI), BF16), jax.ShapeDtypeStruct((B, 1, D), F32)],
        compiler_params=_params(),
    )(dxo, y, gate, w_out, o, proj)


def _final_loss(x, tgt, g, S, name):
    T, D = x.shape
    tm = _tile(S, 512)

    def body(x_ref, t_ref, g_ref, dx_ref, dg_ref, l_ref):
        @pl.when(pl.program_id(0) == 0)
        def _():
            dg_ref[...] = jnp.zeros_like(dg_ref)
            l_ref[...] = jnp.zeros_like(l_ref)

        xv = x_ref[...]
        gv = g_ref[...]
        r = lax.rsqrt(jnp.mean(xv * xv, axis=-1, keepdims=True) + NORM_EPS)
        xn = xv * r
        e = xn * gv - t_ref[...]
        part = jnp.sum(jnp.sum(e * e, axis=0, keepdims=True), axis=1, keepdims=True)
        l_ref[...] += (0.5 / D) * part
        dy = e * (1.0 / D)
        dg_ref[...] += jnp.sum(dy * xn, axis=0, keepdims=True)
        dxn = dy * gv
        dx_ref[...] = r * (dxn - xn * jnp.mean(dxn * xn, axis=-1, keepdims=True))

    row = pl.BlockSpec((tm, D), lambda i: (i, 0))
    return _call(
        body, name=name, grid=(T // tm,),
        in_specs=[row, row, pl.BlockSpec((1, D), lambda i: (0, 0))],
        out_specs=[row, pl.BlockSpec((1, D), lambda i: (0, 0)), pl.BlockSpec((1, LANES), lambda i: (0, 0))],
        out_shape=[jax.ShapeDtypeStruct((T, D), F32), jax.ShapeDtypeStruct((1, D), F32),
                   jax.ShapeDtypeStruct((1, LANES), F32)],
        compiler_params=_params(),
    )(x, tgt, g)


def _cum_fwd(fl, bf, name):
    B, S, _ = fl.shape
    ch = _tile(S, 256, 8)

    def body(fl_ref, b_ref, cum_ref):
        ri = lax.broadcasted_iota(jnp.int32, (ch, ch), 0)
        ci = lax.broadcasted_iota(jnp.int32, (ch, ch), 1)
        tri = jnp.where(ri >= ci, 1.0, 0.0).astype(BF16)

        def step(i, carry):
            r0 = pl.multiple_of(i * ch, ch)
            z = fl_ref[0, pl.ds(r0, ch), :] + b_ref[...]
            lf = jnp.minimum(z, 0.0) - jnp.log(1.0 + jnp.exp(-jnp.abs(z)))
            hi, mid, lo = _split3(lf)
            cs = (jnp.dot(tri, hi, preferred_element_type=F32) + jnp.dot(tri, mid, preferred_element_type=F32)
                  + jnp.dot(tri, lo, preferred_element_type=F32)) + carry
            cum_ref[0, pl.ds(r0, ch), :] = cs
            return cs[ch - 1:ch, :]

        lax.fori_loop(0, S // ch, step, jnp.zeros((1, LANES), F32))

    blk = pl.BlockSpec((1, S, LANES), lambda b: (b, 0, 0))
    return _call(
        body, name=name, grid=(B,), in_specs=[blk, pl.BlockSpec((1, LANES), lambda b: (0, 0))], out_specs=blk,
        out_shape=jax.ShapeDtypeStruct((B, S, LANES), F32), compiler_params=_params(),
    )(fl, bf)


def _cum_bwd(dcs, fl, bf, name):
    B, S, _ = fl.shape
    ch = _tile(S, 256, 8)
    n = S // ch

    def body(d_ref, fl_ref, b_ref, o_ref, db_ref):
        ri = lax.broadcasted_iota(jnp.int32, (ch, ch), 0)
        ci = lax.broadcasted_iota(jnp.int32, (ch, ch), 1)
        tri = jnp.where(ci >= ri, 1.0, 0.0).astype(BF16)

        @pl.when(pl.program_id(0) == 0)
        def _():
            db_ref[...] = jnp.zeros_like(db_ref)

        def step(t, carry):
            tail, dbsum = carry
            r0 = pl.multiple_of((n - 1 - t) * ch, ch)
            hi, mid, lo = _split3(d_ref[0, pl.ds(r0, ch), :])
            suf = (jnp.dot(tri, hi, preferred_element_type=F32) + jnp.dot(tri, mid, preferred_element_type=F32)
                   + jnp.dot(tri, lo, preferred_element_type=F32)) + tail
            z = fl_ref[0, pl.ds(r0, ch), :] + b_ref[...]
            dfl = -suf * _sigmoid(-z)
            o_ref[0, pl.ds(r0, ch), :] = dfl
            return suf[0:1, :], dbsum + jnp.sum(dfl, axis=0, keepdims=True)

        z1 = jnp.zeros((1, LANES), F32)
        _, dbsum = lax.fori_loop(0, n, step, (z1, z1))
        db_ref[...] += dbsum

    blk = pl.BlockSpec((1, S, LANES), lambda b: (b, 0, 0))
    vec = pl.BlockSpec((1, LANES), lambda b: (0, 0))
    return _call(
        body, name=name, grid=(B,), in_specs=[blk, blk, vec], out_specs=[blk, vec],
        out_shape=[jax.ShapeDtypeStruct((B, S, LANES), F32), jax.ShapeDtypeStruct((1, LANES), F32)],
        compiler_params=_params(),
    )(dcs, fl, bf)


HEADS_PER_STEP = 4
GROUP = 2 * HEAD_DIM


def _step_width():
    return HEAD_DIM * HEADS_PER_STEP


def _cols(S, offset_blocks=0):
    return pl.BlockSpec((S, _step_width()), lambda b, h: (b, offset_blocks + h))


def _row_spec(nq, tq):
    return pl.BlockSpec((1, HEADS_PER_STEP, nq, 1, tq), lambda b, h: (b, h, 0, 0, 0))


def _lanes(g):
    return slice(GROUP * (g // 2), GROUP * (g // 2) + GROUP)


def _hi_lo(x):
    hi = x.astype(BF16)
    return hi, (x - hi.astype(F32)).astype(BF16)


def _dot(a, b, dims=None):
    if dims is None:
        return jnp.dot(a, b, preferred_element_type=F32)
    return lax.dot_general(a, b, dims, preferred_element_type=F32)


def _causal_blocks(nq, prep, init, stages, finish, combine=None, descending=False):
    heads = range(HEADS_PER_STEP)

    def qloop(qi, _):
        ctx = [prep(g, qi) for g in heads]

        def step(kj, carry, masked):
            st = list(carry)
            for n, stage in enumerate(stages):
                if combine is not None and n == len(stages) - 1:
                    combine(kj, ctx, st)
                st = [stage(g, ctx[g], kj, masked, st[g]) for g in heads]
            return tuple(st)

        carry = tuple(init() for _ in heads)
        if descending:
            carry = step(qi, carry, True)
            carry = lax.fori_loop(0, qi, lambda t, cr: step(qi - 1 - t, cr, False), carry)
        else:
            carry = lax.fori_loop(0, qi, lambda kj, cr: step(kj, cr, False), carry)
            carry = step(qi, carry, True)
        finish(qi, ctx, carry)
        return 0

    lax.fori_loop(0, nq, qloop, 0)


class _Block:
    def __init__(self, tq):
        self.tq = tq
        self.lane = lax.broadcasted_iota(jnp.int32, (tq, GROUP), 1)
        self.low = self.lane < HEAD_DIM
        self.ri = lax.broadcasted_iota(jnp.int32, (tq, tq), 0)
        self.ci = lax.broadcasted_iota(jnp.int32, (tq, tq), 1)

    def rows(self, i):
        return pl.ds(pl.multiple_of(i * self.tq, self.tq), self.tq)

    def own(self, g, x):
        return jnp.where(self.low if g % 2 == 0 else jnp.logical_not(self.low), x, jnp.zeros_like(x))

    def pair(self, a, b):
        return jnp.where(self.low, a, b)

    def stat(self, g, x):
        return jnp.sum(jnp.where(self.lane == HEAD_DIM * (g % 2), x, 0.0), axis=1, keepdims=True)


def _fox_fwd(proj, cumcol, cumrow, name):
    T, DI = proj.shape[0], proj.shape[1] // 4
    B, H, nq, _, tq = cumrow.shape
    S = nq * tq
    nb = DI // _step_width()

    def body(q_ref, k_ref, v_ref, cc_ref, cr_ref, o_ref, st_ref):
        h0 = pl.program_id(1) * HEADS_PER_STEP
        blk = _Block(tq)

        def prep(g, qi):
            q = blk.own(g, q_ref[blk.rows(qi), _lanes(g)]) * 0.125
            ccol = jnp.sum(jnp.where(blk.lane == h0 + g, cc_ref[0, blk.rows(qi), :], 0.0), axis=1, keepdims=True)
            return q, ccol

        def init():
            return jnp.full((tq, 1), -jnp.inf, F32), jnp.zeros((tq, 1), F32), jnp.zeros((tq, GROUP), F32)

        def scores(g, ctx, kj, masked, st):
            return st + (_dot(ctx[0], k_ref[blk.rows(kj), _lanes(g)], NT),)

        def softmax(g, ctx, kj, masked, st):
            m, l, acc, s = st
            s = s + ctx[1] - cr_ref[0, g, kj]
            if masked:
                s = jnp.where(blk.ci <= blk.ri, s, -jnp.inf)
            m_new = jnp.maximum(m, jnp.max(s, axis=1, keepdims=True))
            alpha = jnp.exp(m - m_new)
            p = jnp.exp(s - m_new)
            return (m_new, alpha * l + jnp.sum(p, axis=1, keepdims=True), acc, alpha) + _hi_lo(p)

        def values(g, ctx, kj, masked, st):
            m, l, acc, alpha, hi, lo = st
            v = v_ref[blk.rows(kj), _lanes(g)]
            return m, l, alpha * acc + (_dot(hi, v) + _dot(lo, v))

        def finish(qi, ctx, carry):
            for g in range(0, HEADS_PER_STEP, 2):
                (m0, l0, a0), (m1, l1, a1) = carry[g], carry[g + 1]
                o_ref[blk.rows(qi), _lanes(g)] = blk.pair(a0 / l0, a1 / l1)
                st_ref[blk.rows(qi), _lanes(g)] = blk.pair(m0 + jnp.log(l0), m1 + jnp.log(l1))

        _causal_blocks(nq, prep, init, [scores, softmax, values], finish)

    out = jax.ShapeDtypeStruct((T, DI), F32)
    return _call(
        body, name=name, grid=(B, H // HEADS_PER_STEP),
        in_specs=[_cols(S), _cols(S, nb), _cols(S, 2 * nb), pl.BlockSpec((1, S, LANES), lambda b, h: (b, 0, 0)),
                  _row_spec(nq, tq)],
        out_specs=[_cols(S), _cols(S)], out_shape=[out, out], compiler_params=_params(),
    )(proj, proj, proj, cumcol, cumrow)


def _fox_bwd(proj, do, o, stat, cumcol, cumrow, name):
    T, DI = do.shape
    B, H, nq, _, tq = cumrow.shape
    S = nq * tq
    nb = DI // _step_width()

    def body(q_ref, k_ref, v_ref, do_ref, o_ref, st_ref, cc_ref, cr_ref, dqkv_ref, dcs_ref, dk_acc, dv_acc):
        h0 = pl.program_id(1) * HEADS_PER_STEP
        blk = _Block(tq)
        dk_acc[...] = jnp.zeros_like(dk_acc)
        dv_acc[...] = jnp.zeros_like(dv_acc)
        dcs_ref[...] = jnp.zeros_like(dcs_ref)

        def prep(g, qi):
            q = blk.own(g, q_ref[blk.rows(qi), _lanes(g)]) * 0.125
            dout = blk.own(g, do_ref[blk.rows(qi), _lanes(g)])
            delta = jnp.sum(o_ref[blk.rows(qi), _lanes(g)] * dout.astype(F32), axis=1, keepdims=True)
            lse = blk.stat(g, st_ref[blk.rows(qi), _lanes(g)])
            ccol = jnp.sum(jnp.where(blk.lane == h0 + g, cc_ref[0, blk.rows(qi), :], 0.0), axis=1, keepdims=True)
            return q, dout, lse, delta, ccol

        def init():
            return (jnp.zeros((tq, GROUP), F32),)

        def scores(g, ctx, kj, masked, st):
            return st + (_dot(ctx[0], k_ref[blk.rows(kj), _lanes(g)], NT),
                         _dot(ctx[1], v_ref[blk.rows(kj), _lanes(g)], NT))

        def softmax_bwd(g, ctx, kj, masked, st):
            dq, s, dp = st
            _, _, lse, delta, ccol = ctx
            s = s + ccol - cr_ref[0, g, kj]
            if masked:
                s = jnp.where(blk.ci <= blk.ri, s, -jnp.inf)
            p = jnp.exp(s - lse)
            ds = p * (dp - delta)
            return dq, p.astype(BF16), ds.astype(BF16), jnp.sum(ds, axis=0, keepdims=True)

        def combine(kj, ctx, st):
            for g in range(0, HEADS_PER_STEP, 2):
                dv_acc[blk.rows(kj), _lanes(g)] += _dot(st[g][1], ctx[g][1], TN) + _dot(st[g + 1][1], ctx[g + 1][1], TN)
                dk_acc[blk.rows(kj), _lanes(g)] += _dot(st[g][2], ctx[g][0], TN) + _dot(st[g + 1][2], ctx[g + 1][0], TN)
            for g in range(HEADS_PER_STEP):
                dcs_ref[0, g, kj] += st[g][3]

        def queries(g, ctx, kj, masked, st):
            dq, _, dsb, _ = st
            return (dq + _dot(dsb, blk.own(g, k_ref[blk.rows(kj), _lanes(g)])),)

        def finish(qi, ctx, carry):
            for g in range(0, HEADS_PER_STEP, 2):
                dqkv_ref[0, blk.rows(qi), _lanes(g)] = ((carry[g][0] + carry[g + 1][0]) * 0.125).astype(BF16)

        _causal_blocks(nq, prep, init, [scores, softmax_bwd, queries], finish, combine=combine)
        dqkv_ref[1] = dk_acc[...].astype(BF16)
        dqkv_ref[2] = dv_acc[...].astype(BF16)

    W = _step_width()
    return _call(
        body, name=name, grid=(B, H // HEADS_PER_STEP),
        in_specs=[_cols(S), _cols(S, nb), _cols(S, 2 * nb), _cols(S), _cols(S), _cols(S),
                  pl.BlockSpec((1, S, LANES), lambda b, h: (b, 0, 0)), _row_spec(nq, tq)],
        out_specs=[pl.BlockSpec((3, S, W), lambda b, h: (0, b, h)), _row_spec(nq, tq)],
        out_shape=[jax.ShapeDtypeStruct((3, T, DI), BF16), jax.ShapeDtypeStruct((B, H, nq, 1, tq), F32)],
        scratch_shapes=[pltpu.VMEM((S, W), F32), pltpu.VMEM((S, W), F32)], compiler_params=_params(),
    )(proj, proj, proj, do, o, stat, cumcol, cumrow)


def _softplus_parts(z):
    e = jnp.exp(-jnp.abs(z))
    return jnp.maximum(z, 0.0) + jnp.log(1.0 + e), e


def _sb_fwd(proj, B, tq, name):
    T, DI = proj.shape[0], proj.shape[1] // 4
    S = T // B
    H = DI // HEAD_DIM
    nq = S // tq
    nb = DI // _step_width()

    def body(q_ref, k_ref, v_ref, o_ref, st_ref):
        blk = _Block(tq)
        strict = blk.ci < blk.ri
        above = jnp.where(blk.ri > blk.ci, 1.0, 0.0).astype(BF16)

        def prep(g, qi):
            return blk.own(g, q_ref[blk.rows(qi), _lanes(g)]) * 0.125

        def init():
            return jnp.zeros((tq, 1), F32), jnp.zeros((tq, GROUP), F32)

        def scores(g, q, kj, masked, st):
            return st + (_dot(q, k_ref[blk.rows(kj), _lanes(g)], NT),)

        def logs(g, q, kj, masked, st):
            c, acc, z = st
            sp, _ = _softplus_parts(z)
            lk = -sp
            if masked:
                lk = jnp.where(strict, lk, 0.0)
            return (c, acc, z - sp, jnp.sum(lk, axis=1, keepdims=True)) + _hi_lo(lk)

        def suffix(g, q, kj, masked, st):
            c, acc, lb, lksum, hi, lo = st
            return c, acc, lb, lksum, _dot(hi, above) + _dot(lo, above)

        def weights(g, q, kj, masked, st):
            c, acc, lb, lksum, after = st
            a = jnp.exp(lb + after + c)
            if masked:
                a = jnp.where(strict, a, 0.0)
            return c + lksum, acc, a.astype(BF16)

        def values(g, q, kj, masked, st):
            c, acc, ab = st
            return c, acc + _dot(ab, v_ref[blk.rows(kj), _lanes(g)])

        def finish(qi, ctx, carry):
            for g in range(0, HEADS_PER_STEP, 2):
                (c0, a0), (c1, a1) = carry[g], carry[g + 1]
                o_ref[blk.rows(qi), _lanes(g)] = blk.pair(a0, a1)
                st_ref[blk.rows(qi), _lanes(g)] = blk.pair(c0, c1)

        _causal_blocks(nq, prep, init, [scores, logs, suffix, weights, values], finish, descending=True)

    out = jax.ShapeDtypeStruct((T, DI), F32)
    return _call(
        body, name=name, grid=(B, H // HEADS_PER_STEP), in_specs=[_cols(S), _cols(S, nb), _cols(S, 2 * nb)],
        out_specs=[_cols(S), _cols(S)], out_shape=[out, out], compiler_params=_params(),
    )(proj, proj, proj)


def _sb_bwd(proj, do, stat, B, tq, name):
    T, DI = do.shape
    S = T // B
    H = DI // HEAD_DIM
    nq = S // tq
    nb = DI // _step_width()

    def body(q_ref, k_ref, v_ref, do_ref, st_ref, dqkv_ref, dk_acc, dv_acc):
        blk = _Block(tq)
        strict = blk.ci < blk.ri
        upto = jnp.where(blk.ri <= blk.ci, 1.0, 0.0).astype(BF16)
        before = jnp.where(blk.ri < blk.ci, 1.0, 0.0).astype(BF16)
        dk_acc[...] = jnp.zeros_like(dk_acc)
        dv_acc[...] = jnp.zeros_like(dv_acc)

        def prep(g, qi):
            return (blk.own(g, q_ref[blk.rows(qi), _lanes(g)]) * 0.125, blk.own(g, do_ref[blk.rows(qi), _lanes(g)]),
                    blk.stat(g, st_ref[blk.rows(qi), _lanes(g)]))

        def init():
            return jnp.zeros((tq, 1), F32), jnp.zeros((tq, 1), F32), jnp.zeros((tq, GROUP), F32)

        def scores(g, ctx, kj, masked, st):
            return st + (_dot(ctx[0], k_ref[blk.rows(kj), _lanes(g)], NT),
                         _dot(ctx[1], v_ref[blk.rows(kj), _lanes(g)], NT))

        def logs(g, ctx, kj, masked, st):
            cpre, pg, dq, z, da = st
            sp, e = _softplus_parts(z)
            inv = 1.0 / (1.0 + e)
            sig = jnp.where(z >= 0.0, inv, e * inv)
            lk = -sp
            if masked:
                lk = jnp.where(strict, lk, 0.0)
            return (cpre, pg, dq, da, z - sp, sig, jnp.sum(lk, axis=1, keepdims=True)) + _hi_lo(lk)

        def prefix(g, ctx, kj, masked, st):
            cpre, pg, dq, da, lb, sig, lksum, hi, lo = st
            return cpre, pg, dq, da, lb, sig, lksum, _dot(hi, upto) + _dot(lo, upto)

        def weights(g, ctx, kj, masked, st):
            cpre, pg, dq, da, lb, sig, lksum, pre = st
            a = jnp.exp(lb + (ctx[2] - (cpre + pre)))
            if masked:
                a = jnp.where(strict, a, 0.0)
            gr = da * a
            return cpre + lksum, pg, dq, sig, a.astype(BF16), gr, gr.astype(BF16)

        def grad_prefix(g, ctx, kj, masked, st):
            cpre, pg, dq, sig, ab, gr, gb = st
            return cpre, pg, dq, sig, ab, gr, _dot(gb, before)

        def dlogits(g, ctx, kj, masked, st):
            cpre, pg, dq, sig, ab, gr, pfx = st
            dz = gr * (1.0 - sig) - (pfx + pg) * sig
            if masked:
                dz = jnp.where(strict, dz, 0.0)
            return cpre, pg + jnp.sum(gr, axis=1, keepdims=True), dq, ab, dz.astype(BF16)

        def combine(kj, ctx, st):
            for g in range(0, HEADS_PER_STEP, 2):
                dv_acc[blk.rows(kj), _lanes(g)] += _dot(st[g][3], ctx[g][1], TN) + _dot(st[g + 1][3], ctx[g + 1][1], TN)
                dk_acc[blk.rows(kj), _lanes(g)] += _dot(st[g][4], ctx[g][0], TN) + _dot(st[g + 1][4], ctx[g + 1][0], TN)

        def queries(g, ctx, kj, masked, st):
            cpre, pg, dq, _, dzb = st
            return cpre, pg, dq + _dot(dzb, blk.own(g, k_ref[blk.rows(kj), _lanes(g)]))

        def finish(qi, ctx, carry):
            for g in range(0, HEADS_PER_STEP, 2):
                dqkv_ref[0, blk.rows(qi), _lanes(g)] = ((carry[g][2] + carry[g + 1][2]) * 0.125).astype(BF16)

        _causal_blocks(nq, prep, init, [scores, logs, prefix, weights, grad_prefix, dlogits, queries], finish,
                       combine=combine)
        dqkv_ref[1] = dk_acc[...].astype(BF16)
        dqkv_ref[2] = dv_acc[...].astype(BF16)

    W = _step_width()
    return _call(
        body, name=name, grid=(B, H // HEADS_PER_STEP),
        in_specs=[_cols(S), _cols(S, nb), _cols(S, 2 * nb), _cols(S), _cols(S)],
        out_specs=pl.BlockSpec((3, S, W), lambda b, h: (0, b, h)),
        out_shape=jax.ShapeDtypeStruct((3, T, DI), BF16),
        scratch_shapes=[pltpu.VMEM((S, W), F32), pltpu.VMEM((S, W), F32)], compiler_params=_params(),
    )(proj, proj, proj, do, stat)


def _row_tile(R, C, n_arrays):
    budget = 24 * 1024 * 1024 // (2 * n_arrays * 4 * max(C, LANES))
    return _tile(R, max(8, budget), 8)


def _ew_sum(parts, name, also_bf16=False):
    R, C = parts[0].shape
    tr = _row_tile(R, C, len(parts) + 2)
    n = len(parts)

    def body(*refs):
        acc = refs[0][...].astype(F32) + refs[1][...].astype(F32)
        for r in refs[2:n]:
            acc = acc + r[...].astype(F32)
        refs[n][...] = acc
        if also_bf16:
            refs[n + 1][...] = acc.astype(BF16)

    blk = pl.BlockSpec((tr, C), lambda i: (i, 0))
    out_shape = [jax.ShapeDtypeStruct((R, C), F32)] + ([jax.ShapeDtypeStruct((R, C), BF16)] if also_bf16 else [])
    return _call(
        body, name=name, grid=(R // tr,), in_specs=[blk] * n, out_specs=[blk] * len(out_shape),
        out_shape=out_shape, compiler_params=_params(),
    )(*parts)


def _adamw(w, g, m, v, name):
    R, C = w.shape
    tr = _row_tile(R, C, 7)
    c1 = 1.0 / (1.0 - ADAM_B1 ** ADAM_STEP)
    c2 = 1.0 / (1.0 - ADAM_B2 ** ADAM_STEP)

    def body(w_ref, g_ref, m_ref, v_ref, d_ref, m2_ref, v2_ref):
        gv = g_ref[...]
        m2 = ADAM_B1 * m_ref[...] + (1.0 - ADAM_B1) * gv
        v2 = ADAM_B2 * v_ref[...] + (1.0 - ADAM_B2) * (gv * gv)
        m2_ref[...] = m2
        v2_ref[...] = v2
        d_ref[...] = -ADAM_LR * ((m2 * c1) / (jnp.sqrt(v2 * c2) + ADAM_EPS) + ADAM_WD * w_ref[...])

    blk = pl.BlockSpec((tr, C), lambda i: (i, 0))
    out = jax.ShapeDtypeStruct((R, C), F32)
    return _call(
        body, name=name, grid=(R // tr,), in_specs=[blk] * 4, out_specs=[blk] * 3, out_shape=[out] * 3,
        compiler_params=_params(),
    )(w, g, m, v)


def _me():
    return lax.axis_index("x"), lax.axis_index("y"), lax.axis_index("c")


def _chip_of(x, y):
    return 2 * x + y


def _other_chips(x, y):
    return [(x, 1 - y), (1 - x, y), (1 - x, 1 - y)]


def _gather_weights(halves, smalls):
    nh, ns = len(halves), len(smalls)

    def body(*refs):
        ins_h, ins_s = refs[:nh], refs[nh:nh + ns]
        outs_h, outs_s = refs[nh + ns:2 * nh + ns], refs[2 * nh + ns:2 * (nh + ns)]
        send1, recv1, send2, recv2, send3, recv3 = refs[2 * (nh + ns):]
        x, y, c = _me()
        mine = _chip_of(x, y)
        chips = _other_chips(x, y)
        sib = (x, y, 1 - c)

        def landed(i, k, half):
            return outs_h[i].at[_chip_of(*chips[k]), half]

        def first(i, k):
            return pltpu.make_async_remote_copy(
                src_ref=ins_h[i].at[c], dst_ref=outs_h[i].at[mine, c], send_sem=send1.at[i, k], recv_sem=recv1.at[i, k],
                device_id=(*chips[k], c), device_id_type=MESH)

        def passed(i, k):
            return pltpu.make_async_remote_copy(
                src_ref=landed(i, k, c), dst_ref=landed(i, k, c), send_sem=send2.at[i, k], recv_sem=recv2.at[i, k],
                device_id=sib, device_id_type=MESH)

        def small(i, k):
            return pltpu.make_async_remote_copy(
                src_ref=ins_s[i], dst_ref=outs_s[i].at[mine], send_sem=send3.at[i, k], recv_sem=recv3.at[i, k],
                device_id=(*chips[k], c), device_id_type=MESH)

        for i in range(nh):
            for k in range(3):
                first(i, k).start()
        for i in range(ns):
            for k in range(3):
                small(i, k).start()
        for i in range(nh):
            for k in range(3):
                pltpu.make_async_remote_copy(
                    src_ref=ins_h[i].at[c], dst_ref=landed(i, k, c), send_sem=send1.at[i, k], recv_sem=recv1.at[i, k],
                    device_id=(*chips[k], c), device_id_type=MESH).wait_recv()
                passed(i, k).start()
        for i in range(nh):
            for k in range(3):
                pltpu.make_async_remote_copy(
                    src_ref=landed(i, k, c), dst_ref=landed(i, k, 1 - c), send_sem=send2.at[i, k],
                    recv_sem=recv2.at[i, k], device_id=sib, device_id_type=MESH).wait_recv()
        for i in range(ns):
            for k in range(3):
                pltpu.make_async_remote_copy(
                    src_ref=ins_s[i], dst_ref=outs_s[i].at[_chip_of(*chips[k])], send_sem=send3.at[i, k],
                    recv_sem=recv3.at[i, k], device_id=(*chips[k], c), device_id_type=MESH).wait_recv()
        for i in range(nh):
            for k in range(3):
                first(i, k).wait_send()
                passed(i, k).wait_send()
        for i in range(ns):
            for k in range(3):
                small(i, k).wait_send()

    out_shape = ([jax.ShapeDtypeStruct((4,) + a.shape, a.dtype) for a in halves]
                 + [jax.ShapeDtypeStruct((4,) + a.shape, a.dtype) for a in smalls])
    n = nh + ns
    res = _call(
        body, name="gather_weights", in_specs=[HBM] * n, out_specs=[HBM] * n, out_shape=out_shape,
        scratch_shapes=[pltpu.SemaphoreType.DMA((nh, 3)), pltpu.SemaphoreType.DMA((nh, 3)),
                        pltpu.SemaphoreType.DMA((nh, 3)), pltpu.SemaphoreType.DMA((nh, 3)),
                        pltpu.SemaphoreType.DMA((max(ns, 1), 3)), pltpu.SemaphoreType.DMA((max(ns, 1), 3))],
        compiler_params=_params(),
    )(*halves, *smalls)
    return res[:nh], res[nh:]


def _pair_exchange(grads):
    n = len(grads)

    def body(*refs):
        ins, got = refs[:n], refs[n:2 * n]
        send, recv = refs[2 * n:]
        x, y, c = _me()
        cps = []
        for i in range(n):
            for j in range(4):
                r = pltpu.make_async_remote_copy(
                    src_ref=ins[i].at[j, 1 - c], dst_ref=got[i].at[j], send_sem=send.at[i, j], recv_sem=recv.at[i, j],
                    device_id=(x, y, 1 - c), device_id_type=MESH)
                r.start()
                cps.append(r)
        for r in cps:
            r.wait()

    return _call(
        body, name="grad_pair_exchange", in_specs=[HBM] * n, out_specs=[HBM] * n,
        out_shape=[jax.ShapeDtypeStruct((4,) + g.shape[2:], g.dtype) for g in grads],
        scratch_shapes=[pltpu.SemaphoreType.DMA((n, 4)), pltpu.SemaphoreType.DMA((n, 4))],
        compiler_params=_params(),
    )(*grads)


def _chip_exchange(sums):
    n = len(sums)

    def body(*refs):
        ins, got = refs[:n], refs[n:2 * n]
        send, recv = refs[2 * n:]
        x, y, c = _me()
        chips = _other_chips(x, y)
        cps = []
        for i in range(n):
            for k in range(3):
                r = pltpu.make_async_remote_copy(
                    src_ref=ins[i].at[_chip_of(*chips[k])], dst_ref=got[i].at[k], send_sem=send.at[i, k],
                    recv_sem=recv.at[i, k], device_id=(*chips[k], c), device_id_type=MESH)
                r.start()
                cps.append(r)
        for r in cps:
            r.wait()

    return _call(
        body, name="grad_chip_exchange", in_specs=[HBM] * n, out_specs=[HBM] * n,
        out_shape=[jax.ShapeDtypeStruct((3,) + s.shape[1:], s.dtype) for s in sums],
        scratch_shapes=[pltpu.SemaphoreType.DMA((n, 3)), pltpu.SemaphoreType.DMA((n, 3))],
        compiler_params=_params(),
    )(*sums)


def _pair_share(halves):
    n = len(halves)

    def body(*refs):
        ins, outs = refs[:n], refs[n:2 * n]
        send, recv = refs[2 * n:]
        x, y, c = _me()
        cps = []
        for i in range(n):
            r = pltpu.make_async_remote_copy(
                src_ref=ins[i], dst_ref=outs[i], send_sem=send.at[i], recv_sem=recv.at[i],
                device_id=(x, y, 1 - c), device_id_type=MESH)
            r.start()
            cps.append(r)
        for r in cps:
            r.wait()

    return _call(
        body, name="grad_pair_share", in_specs=[HBM] * n, out_specs=[HBM] * n,
        out_shape=[jax.ShapeDtypeStruct(h.shape, h.dtype) for h in halves],
        scratch_shapes=[pltpu.SemaphoreType.DMA((n,)), pltpu.SemaphoreType.DMA((n,))],
        compiler_params=_params(),
    )(*halves)


def _allreduce_small(vec):
    P = vec.shape[1]

    def body(v_ref, sum_ref, all_ref, send, recv):
        x, y, c = _me()
        me = 4 * x + 2 * y + c
        all_ref[pl.ds(me, 1)] = v_ref[...][None]
        cps = []
        for d in range(1, 8):
            peer = (jnp.bitwise_xor(x, d >> 2), jnp.bitwise_xor(y, (d >> 1) & 1), jnp.bitwise_xor(c, d & 1))
            r = pltpu.make_async_remote_copy(
                src_ref=v_ref, dst_ref=all_ref.at[me], send_sem=send.at[d - 1], recv_sem=recv.at[d - 1],
                device_id=peer, device_id_type=MESH)
            r.start()
            cps.append(r)
        for d in range(1, 8):
            src = jnp.bitwise_xor(me, d)
            pltpu.make_async_remote_copy(
                src_ref=v_ref, dst_ref=all_ref.at[src], send_sem=send.at[d - 1], recv_sem=recv.at[d - 1],
                device_id=(x, y, c), device_id_type=MESH).wait_recv()
        for r in cps:
            r.wait_send()
        acc = all_ref[0]
        for i in range(1, 8):
            acc = acc + all_ref[i]
        sum_ref[...] = acc

    vm = pl.BlockSpec(memory_space=pltpu.VMEM)
    return _call(
        body, name="allreduce_small", in_specs=[vm], out_specs=[vm, vm],
        out_shape=[jax.ShapeDtypeStruct((8, P), F32), jax.ShapeDtypeStruct((8, 8, P), F32)],
        scratch_shapes=[pltpu.SemaphoreType.DMA((7,)), pltpu.SemaphoreType.DMA((7,))],
        compiler_params=_params(),
    )(vec)[0]


def _per_batch(mod, B, D):
    return [mod[:B, i * D:(i + 1) * D].reshape(B, 1, D) for i in range(3)]


def _pad_rows8(a):
    return jnp.concatenate([a, jnp.zeros((8 - a.shape[0],) + a.shape[1:], a.dtype)], axis=0)


def _layer_fwd(x, c8, w, S, fox, tag):
    T, D = x.shape
    B = T // S
    DI = w["w_out"].shape[0]
    H = DI // HEAD_DIM
    tq = _tile(S, ATT_BLOCK, 8)
    mod = _mod_fwd(c8, w["w_ada"], w["b_ada"], tag + "_mod_fwd")
    shift, scale, gate = _per_batch(mod, B, D)
    proj, h = _ln_proj(x, shift, scale, w["norm_g"], w["w_in"], S, tag + "_ln_proj")
    saved = dict(x=x, h=h, proj=proj, scale=scale, gate=gate)
    if fox:
        fl = _mm(h, w["w_f"], "nn", F32, tag + "_flogit").reshape(B, S, LANES)
        cum = _cum_fwd(fl, w["b_f"], tag + "_cum_fwd")
        cumrow = cum[:, :, :H].transpose(0, 2, 1).reshape(B, H, S // tq, 1, tq)
        o, stat = _fox_fwd(proj, cum, cumrow, tag + "_attn_fwd")
        saved.update(fl=fl, cum=cum, cumrow=cumrow)
    else:
        o, stat = _sb_fwd(proj, B, tq, tag + "_attn_fwd")
    xo, y, u = _gate_out(o, proj, w["w_out"], x, gate, S, tag + "_gate_out")
    saved.update(o=o, stat=stat, y=y, u=u)
    return xo, saved


def _layer_bwd(dxo, sv, w, cT, S, fox, tag):
    T, D = dxo.shape
    B = T // S
    DI = w["w_out"].shape[0]
    H = DI // HEAD_DIM
    tq = _tile(S, ATT_BLOCK, 8)
    dy, do, dzg, dgate = _out_bwd(dxo, sv["y"], sv["gate"], w["w_out"], sv["o"], sv["proj"], S, tag + "_out_bwd")
    g = {"w_out": _mm(sv["u"], dy, "tn", F32, tag + "_dw_out", tm=1024, tn=1024)}
    if fox:
        dqkv, dcs = _fox_bwd(sv["proj"], do, sv["o"], sv["stat"], sv["cum"], sv["cumrow"], tag + "_attn_bwd")
        dcs = dcs.reshape(B, H, S).transpose(0, 2, 1)
        dcs = jnp.concatenate([dcs, jnp.zeros((B, S, LANES - H), F32)], axis=-1)
        dfl, db_f = _cum_bwd(dcs, sv["fl"], w["b_f"], tag + "_cum_bwd")
        g["b_f"] = db_f[:, :H]
        tail = [dfl.reshape(T, LANES).astype(BF16)]
        w_in = jnp.concatenate([w["w_in"], w["w_f"]], axis=1)
    else:
        dqkv = _sb_bwd(sv["proj"], do, sv["stat"], B, tq, tag + "_attn_bwd")
        tail = []
        w_in = w["w_in"]
    dproj = jnp.concatenate([dqkv[0], dqkv[1], dqkv[2], dzg] + tail, axis=1)
    N = dproj.shape[1]
    dw_in = _mm(sv["h"], dproj, "tn", F32, tag + "_dw_in", tm=1024, tn=640 if N % 640 == 0 else 512)
    g["w_in"] = dw_in[:, :4 * DI + H] if fox else dw_in
    dh = _mm(dproj, w_in, "nt", F32, tag + "_dh", tm=512, tn=1024, tk=640 if N % 640 == 0 else 512)
    dx, dshift, dscale, dg = _ln_bwd(dh, sv["x"], dxo, sv["scale"], w["norm_g"], S, tag + "_ln_bwd")
    g["norm_g"] = dg
    dmod = jnp.concatenate([dshift, dscale, dgate], axis=-1).reshape(B, 3 * D)
    g["w_ada"], g["b_ada"] = _mod_bwd(cT, _pad_rows8(dmod), B, tag + "_mod_bwd")
    return dx, g


def _local_step(x3, c, tgt3, wf, ws, final_g):
    B, S, D = x3.shape
    T = B * S
    x = x3.reshape(T, D)
    c8 = _pad_rows8(c)
    cT = c8.T
    x1, sv1 = _layer_fwd(x, c8, wf, S, True, "fox")
    x2, sv2 = _layer_fwd(x1, c8, ws, S, False, "sb")
    dx2, dgf, loss = _final_loss(x2, tgt3.reshape(T, D), final_g, S, "final_loss")
    dx1, gs = _layer_bwd(dx2, sv2, ws, cT, S, False, "sb")
    dx0, gf = _layer_bwd(dx1, sv1, wf, cT, S, True, "fox")
    return loss, dx0.reshape(B, S, D), gf, gs, dgf


def _cols_to_shards(a):
    R, C4 = a.shape
    return a.reshape(R, 4, C4 // 4).transpose(1, 0, 2)


def _shards_to_cols(a):
    n, R, C = a.shape
    return a.transpose(1, 0, 2).reshape(R, n * C)


def kernel(x, c, fox_norm_g, fox_w_ada, fox_b_ada, fox_w_in, fox_b_f, fox_w_out, sb_norm_g, sb_w_ada, sb_b_ada, sb_w_in, sb_w_out, final_norm_g, loss_target, m_fox_norm_g, m_fox_w_ada, m_fox_b_ada, m_fox_w_in, m_fox_b_f, m_fox_w_out, m_sb_norm_g, m_sb_w_ada, m_sb_b_ada, m_sb_w_in, m_sb_w_out, m_final_norm_g, v_fox_norm_g, v_fox_w_ada, v_fox_b_ada, v_fox_w_in, v_fox_b_f, v_fox_w_out, v_sb_norm_g, v_sb_w_ada, v_sb_b_ada, v_sb_w_in, v_sb_w_out, v_final_norm_g):
    B, S, D = x.shape
    DI = 4 * fox_w_out.shape[1]
    H = DI // HEAD_DIM
    chip = _chip_of(lax.axis_index("x"), lax.axis_index("y"))

    big_names = ["fox_w_ada", "fox_w_in", "fox_w_out", "sb_w_ada", "sb_w_in", "sb_w_out"]
    big = dict(fox_w_ada=fox_w_ada[0], fox_w_in=fox_w_in[0], fox_w_out=fox_w_out[0],
               sb_w_ada=sb_w_ada[0], sb_w_in=sb_w_in[0], sb_w_out=sb_w_out[0])
    halves = [big[n].astype(BF16).reshape(2, big[n].shape[0] // 2, big[n].shape[1]) for n in big_names]
    gathered, gsmall = _gather_weights(halves, [sb_norm_g, sb_b_ada])
    gathered = [lax.dynamic_update_index_in_dim(a, own, chip, 0) for a, own in zip(gathered, halves)]
    gsmall = [lax.dynamic_update_index_in_dim(a, own, chip, 0) for a, own in zip(gsmall, [sb_norm_g, sb_b_ada])]
    full = {}
    for n, a in zip(big_names, gathered):
        a = a.reshape(4, a.shape[1] * a.shape[2], a.shape[3])
        full[n] = a.reshape(4 * a.shape[1], a.shape[2]) if n.endswith("w_out") else _shards_to_cols(a)
    sb_norm_full = gsmall[0].reshape(1, D)
    sb_b_ada_full = gsmall[1].reshape(1, 3 * D)
    w_f = jnp.concatenate([full["fox_w_in"][:, 4 * DI:], jnp.zeros((D, LANES - H), BF16)], axis=1)
    b_f = jnp.concatenate([fox_b_f, jnp.zeros((1, LANES - H), F32)], axis=1)
    wf = dict(w_ada=full["fox_w_ada"], b_ada=fox_b_ada, norm_g=fox_norm_g, w_in=full["fox_w_in"][:, :4 * DI],
              w_f=w_f, b_f=b_f, w_out=full["fox_w_out"])
    ws = dict(w_ada=full["sb_w_ada"], b_ada=sb_b_ada_full, norm_g=sb_norm_full, w_in=full["sb_w_in"],
              w_out=full["sb_w_out"])

    loss, grad_x, gf, gs, dgf = _local_step(x, c, loss_target, wf, ws, final_norm_g.reshape(1, D))

    part = dict(fox_w_ada=gf["w_ada"], fox_w_in=gf["w_in"], fox_w_out=gf["w_out"],
                sb_w_ada=gs["w_ada"], sb_w_in=gs["w_in"], sb_w_out=gs["w_out"])
    shard_major = []
    for n in big_names:
        a = part[n]
        a = a.reshape(4, a.shape[0] // 4, a.shape[1]) if n.endswith("w_out") else _cols_to_shards(a)
        shard_major.append(a.reshape(4, 2, a.shape[1] // 2, a.shape[2]))
    core = lax.axis_index("c")
    got = _pair_exchange(shard_major)
    pair_f32, pair_bf16 = [], []
    for n, g4, b in zip(big_names, shard_major, got):
        a = lax.dynamic_index_in_dim(g4, core, axis=1, keepdims=False)
        r, C = a.shape[1:]
        s32, s16 = _ew_sum([a.reshape(4 * r, C), b.reshape(4 * r, C)], n + "_pair_sum", also_bf16=True)
        pair_f32.append(s32.reshape(4, r, C))
        pair_bf16.append(s16.reshape(4, r, C))
    others = _chip_exchange(pair_bf16)
    reduced_halves = [_ew_sum([lax.dynamic_index_in_dim(a, chip, axis=0, keepdims=False), b[0], b[1], b[2]],
                              n + "_chip_sum")[0] for n, a, b in zip(big_names, pair_f32, others)]
    theirs = _pair_share(reduced_halves)
    grad_big = {}
    for n, a, b in zip(big_names, reduced_halves, theirs):
        grad_big[n] = jnp.concatenate([jnp.where(core == 0, a, b), jnp.where(core == 0, b, a)], axis=0)

    pieces = [loss, gf["norm_g"], gf["b_ada"], jnp.concatenate([gf["b_f"], jnp.zeros((1, LANES - H), F32)], axis=1),
              gs["norm_g"], gs["b_ada"], dgf]
    vec = jnp.concatenate(pieces, axis=1)
    red = _allreduce_small(_pad_rows8(vec))[0:1]
    offs = [0]
    for p in pieces:
        offs.append(offs[-1] + p.shape[1])
    r_loss, r_fng, r_fba, r_fbf, r_sng, r_sba, r_fin = [red[:, offs[i]:offs[i + 1]] for i in range(7)]
    small_grads = dict(
        fox_norm_g=r_fng, fox_b_ada=r_fba, fox_b_f=r_fbf[:, :H],
        sb_norm_g=lax.dynamic_slice_in_dim(r_sng, chip * (D // 4), D // 4, axis=1),
        sb_b_ada=lax.dynamic_slice_in_dim(r_sba, chip * (3 * D // 4), 3 * D // 4, axis=1),
        final_norm_g=r_fin)

    weights = dict(fox_norm_g=fox_norm_g, fox_w_ada=fox_w_ada, fox_b_ada=fox_b_ada, fox_w_in=fox_w_in, fox_b_f=fox_b_f,
                   fox_w_out=fox_w_out, sb_norm_g=sb_norm_g, sb_w_ada=sb_w_ada, sb_b_ada=sb_b_ada, sb_w_in=sb_w_in,
                   sb_w_out=sb_w_out, final_norm_g=final_norm_g)
    ms = dict(fox_norm_g=m_fox_norm_g, fox_w_ada=m_fox_w_ada, fox_b_ada=m_fox_b_ada, fox_w_in=m_fox_w_in,
              fox_b_f=m_fox_b_f, fox_w_out=m_fox_w_out, sb_norm_g=m_sb_norm_g, sb_w_ada=m_sb_w_ada,
              sb_b_ada=m_sb_b_ada, sb_w_in=m_sb_w_in, sb_w_out=m_sb_w_out, final_norm_g=m_final_norm_g)
    vs = dict(fox_norm_g=v_fox_norm_g, fox_w_ada=v_fox_w_ada, fox_b_ada=v_fox_b_ada, fox_w_in=v_fox_w_in,
              fox_b_f=v_fox_b_f, fox_w_out=v_fox_w_out, sb_norm_g=v_sb_norm_g, sb_w_ada=v_sb_w_ada,
              sb_b_ada=v_sb_b_ada, sb_w_in=v_sb_w_in, sb_w_out=v_sb_w_out, final_norm_g=v_final_norm_g)
    order = ["fox_norm_g", "fox_w_ada", "fox_b_ada", "fox_w_in", "fox_b_f", "fox_w_out", "sb_norm_g", "sb_w_ada",
             "sb_b_ada", "sb_w_in", "sb_w_out", "final_norm_g"]
    grads, deltas, new_m, new_v = {}, {}, {}, {}
    for n in big_names:
        shp = weights[n].shape
        g2 = grad_big[n]
        d, m2, v2 = _adamw(weights[n][0], g2, ms[n][0], vs[n][0], n + "_adamw")
        grads[n], deltas[n], new_m[n], new_v[n] = g2.reshape(shp), d.reshape(shp), m2.reshape(shp), v2.reshape(shp)
    small_names = [n for n in order if n not in big_names]
    sizes = [small_grads[n].shape[1] for n in small_names]
    total = sum(sizes)
    padn = (-total) % LANES

    def pack(d):
        return jnp.concatenate([d[n].reshape(1, -1) for n in small_names] + [jnp.ones((1, padn), F32)], axis=1)

    sd, sm, sv_ = _adamw(pack(weights), pack(small_grads), pack(ms), pack(vs), "small_adamw")
    o = 0
    for n, sz in zip(small_names, sizes):
        shp = weights[n].shape
        grads[n] = small_grads[n].reshape(shp)
        deltas[n], new_m[n], new_v[n] = (t[:, o:o + sz].reshape(shp) for t in (sd, sm, sv_))
        o += sz
    return (r_loss[0, 0], grad_x, *[grads[n] for n in order], *[deltas[n] for n in order],
            *[new_m[n] for n in order], *[new_v[n] for n in order])
```

```python
import functools

import jax
import jax.numpy as jnp
from jax import lax
from jax.experimental import pallas as pl
from jax.experimental.pallas import tpu as pltpu

F32 = jnp.float32
BF16 = jnp.bfloat16
HEAD_DIM = 64
LANES = 128
NORM_EPS = 1e-6
ADAM_LR = 0.001
ADAM_B1 = 0.9
ADAM_B2 = 0.999
ADAM_EPS = 1e-08
ADAM_WD = 0.01
ADAM_STEP = 10
VMEM_LIMIT = 56 * 1024 * 1024
ATT_BLOCK = 256
MESH = pl.DeviceIdType.MESH
HBM = pl.BlockSpec(memory_space=pltpu.HBM)
NT = (((1,), (1,)), ((), ()))
TN = (((0,), (0,)), ((), ()))


def _call(body, **kw):
    return pl.pallas_call(body, **kw)


def _params(**kw):
    return pltpu.CompilerParams(vmem_limit_bytes=VMEM_LIMIT, **kw)


def _tile(dim, pref, mult=128):
    if dim <= pref:
        return dim
    t = (pref // mult) * mult
    while t >= mult:
        if dim % t == 0:
            return t
        t -= mult
    return dim


def _sigmoid(x):
    return 1.0 / (1.0 + jnp.exp(-x))


def _split3(x):
    hi = x.astype(BF16)
    r = x - hi.astype(F32)
    mid = r.astype(BF16)
    lo = (r - mid.astype(F32)).astype(BF16)
    return hi, mid, lo


def _mm(a, b, mode, out_dtype, name, tm=512, tn=512, tk=512):
    if mode == "nn":
        (M, K), (_, N) = a.shape, b.shape
    elif mode == "nt":
        (M, K), (N, _) = a.shape, b.shape
    else:
        (K, M), (_, N) = a.shape, b.shape
    tm, tn, tk = _tile(M, tm), _tile(N, tn), _tile(K, tk)
    nk = K // tk
    dims = {"nn": (((1,), (0,)), ((), ())), "nt": NT, "tn": TN}[mode]

    def body(a_ref, b_ref, o_ref, acc_ref):
        k = pl.program_id(2)

        @pl.when(k == 0)
        def _():
            acc_ref[...] = jnp.zeros_like(acc_ref)

        acc_ref[...] += lax.dot_general(a_ref[...], b_ref[...], dims, preferred_element_type=F32)

        @pl.when(k == nk - 1)
        def _():
            o_ref[...] = acc_ref[...].astype(out_dtype)

    a_spec = (pl.BlockSpec((tk, tm), lambda i, j, k: (k, i)) if mode == "tn"
              else pl.BlockSpec((tm, tk), lambda i, j, k: (i, k)))
    b_spec = (pl.BlockSpec((tn, tk), lambda i, j, k: (j, k)) if mode == "nt"
              else pl.BlockSpec((tk, tn), lambda i, j, k: (k, j)))
    return _call(
        body, name=name, grid=(M // tm, N // tn, nk),
        in_specs=[a_spec, b_spec], out_specs=pl.BlockSpec((tm, tn), lambda i, j, k: (i, j)),
        out_shape=jax.ShapeDtypeStruct((M, N), out_dtype),
        scratch_shapes=[pltpu.VMEM((tm, tn), F32)], compiler_params=_params(),
    )(a, b)


def _mod_fwd(c8, w_ada, b_ada, name):
    D, N = w_ada.shape
    tn = _tile(N, 512)

    def body(c_ref, w_ref, b_ref, o_ref):
        c = c_ref[...]
        sc = (c * _sigmoid(c)).astype(BF16)
        o_ref[...] = jnp.dot(sc, w_ref[...], preferred_element_type=F32) + b_ref[...]

    return _call(
        body, name=name, grid=(N // tn,),
        in_specs=[pl.BlockSpec((8, D), lambda j: (0, 0)), pl.BlockSpec((D, tn), lambda j: (0, j)),
                  pl.BlockSpec((1, tn), lambda j: (0, j))],
        out_specs=pl.BlockSpec((8, tn), lambda j: (0, j)),
        out_shape=jax.ShapeDtypeStruct((8, N), F32), compiler_params=_params(),
    )(c8, w_ada, b_ada)


def _mod_bwd(cT, dmod8, nb, name):
    D = cT.shape[0]
    N = dmod8.shape[1]
    tn = _tile(N, 512)

    def body(c_ref, d_ref, w_ref, b_ref):
        c = c_ref[...]
        sc = c * _sigmoid(c)
        d = d_ref[...]
        acc = sc[:, 0:1] * d[0:1, :]
        bsum = d[0:1, :]
        for b in range(1, nb):
            acc = acc + sc[:, b:b + 1] * d[b:b + 1, :]
            bsum = bsum + d[b:b + 1, :]
        w_ref[...] = acc
        b_ref[...] = bsum

    return _call(
        body, name=name, grid=(N // tn,),
        in_specs=[pl.BlockSpec((D, 8), lambda j: (0, 0)), pl.BlockSpec((8, tn), lambda j: (0, j))],
        out_specs=[pl.BlockSpec((D, tn), lambda j: (0, j)), pl.BlockSpec((1, tn), lambda j: (0, j))],
        out_shape=[jax.ShapeDtypeStruct((D, N), F32), jax.ShapeDtypeStruct((1, N), F32)],
        compiler_params=_params(),
    )(cT, dmod8)


def _ln_proj(x, shift, scale, g, w, S, name):
    T, D = x.shape
    N = w.shape[1]
    tm = _tile(S, 512)
    tn = _tile(N, 1024)
    per_b = S // tm

    def body(x_ref, sh_ref, sc_ref, g_ref, w_ref, p_ref, h_ref):
        @pl.when(pl.program_id(1) == 0)
        def _():
            xv = x_ref[...]
            r = lax.rsqrt(jnp.mean(xv * xv, axis=-1, keepdims=True) + NORM_EPS)
            h = (xv * r) * g_ref[...] * (1.0 + sc_ref[0]) + sh_ref[0]
            h_ref[...] = h.astype(BF16)

        p_ref[...] = jnp.dot(h_ref[...], w_ref[...], preferred_element_type=F32).astype(BF16)

    return _call(
        body, name=name, grid=(T // tm, N // tn),
        in_specs=[pl.BlockSpec((tm, D), lambda i, j: (i, 0)),
                  pl.BlockSpec((1, 1, D), lambda i, j: (i // per_b, 0, 0)),
                  pl.BlockSpec((1, 1, D), lambda i, j: (i // per_b, 0, 0)),
                  pl.BlockSpec((1, D), lambda i, j: (0, 0)),
                  pl.BlockSpec((D, tn), lambda i, j: (0, j))],
        out_specs=[pl.BlockSpec((tm, tn), lambda i, j: (i, j)), pl.BlockSpec((tm, D), lambda i, j: (i, 0))],
        out_shape=[jax.ShapeDtypeStruct((T, N), BF16), jax.ShapeDtypeStruct((T, D), BF16)],
        compiler_params=_params(),
    )(x, shift, scale, g, w)


def _ln_bwd(dh, x, dxo, scale, g, S, name):
    T, D = x.shape
    B = T // S
    tm = _tile(S, 512)
    per_b = S // tm

    def body(dh_ref, x_ref, dxo_ref, sc_ref, g_ref, dx_ref, dsh_ref, dsc_ref, dg_ref):
        i = pl.program_id(0)
        xv = x_ref[...]
        dh_v = dh_ref[...]
        r = lax.rsqrt(jnp.mean(xv * xv, axis=-1, keepdims=True) + NORM_EPS)
        xn = xv * r
        gv = g_ref[...]
        one_sc = 1.0 + sc_ref[0]
        dhxn = dh_v * xn

        @pl.when(i % per_b == 0)
        def _():
            dsh_ref[...] = jnp.zeros_like(dsh_ref)
            dsc_ref[...] = jnp.zeros_like(dsc_ref)

        @pl.when(i == 0)
        def _():
            dg_ref[...] = jnp.zeros_like(dg_ref)

        dsh_ref[0] += jnp.sum(dh_v, axis=0, keepdims=True)
        dsc_ref[0] += jnp.sum(dhxn, axis=0, keepdims=True) * gv
        dg_ref[...] += jnp.sum(dhxn, axis=0, keepdims=True) * one_sc
        dxn = dh_v * (gv * one_sc)
        dx_ref[...] = r * (dxn - xn * jnp.mean(dxn * xn, axis=-1, keepdims=True)) + dxo_ref[...]

    row = pl.BlockSpec((tm, D), lambda i: (i, 0))
    per = pl.BlockSpec((1, 1, D), lambda i: (i // per_b, 0, 0))
    vec = pl.BlockSpec((1, D), lambda i: (0, 0))
    return _call(
        body, name=name, grid=(T // tm,),
        in_specs=[row, row, row, per, vec], out_specs=[row, per, per, vec],
        out_shape=[jax.ShapeDtypeStruct((T, D), F32), jax.ShapeDtypeStruct((B, 1, D), F32),
                   jax.ShapeDtypeStruct((B, 1, D), F32), jax.ShapeDtypeStruct((1, D), F32)],
        compiler_params=_params(),
    )(dh, x, dxo, scale, g)


def _gate_out(o, proj, w_out, x, gate, S, name):
    T, DI = o.shape
    D = w_out.shape[1]
    tm = _tile(S, 256)
    per_b = S // tm

    def body(o_ref, z_ref, w_ref, x_ref, g_ref, xo_ref, y_ref, u_ref):
        z = z_ref[...].astype(F32)
        u = (o_ref[...] * (z * _sigmoid(z))).astype(BF16)
        u_ref[...] = u
        y = jnp.dot(u, w_ref[...], preferred_element_type=F32)
        y_ref[...] = y
        xo_ref[...] = x_ref[...] + g_ref[0] * y

    wide = pl.BlockSpec((tm, DI), lambda i: (i, 0))
    row = pl.BlockSpec((tm, D), lambda i: (i, 0))
    return _call(
        body, name=name, grid=(T // tm,),
        in_specs=[wide, pl.BlockSpec((tm, DI), lambda i: (i, 3)), pl.BlockSpec((DI, D), lambda i: (0, 0)), row,
                  pl.BlockSpec((1, 1, D), lambda i: (i // per_b, 0, 0))],
        out_specs=[row, row, wide],
        out_shape=[jax.ShapeDtypeStruct((T, D), F32), jax.ShapeDtypeStruct((T, D), F32),
                   jax.ShapeDtypeStruct((T, DI), BF16)],
        compiler_params=_params(),
    )(o, proj, w_out, x, gate)


def _out_bwd(dxo, y, gate, w_out, o, proj, S, name):
    T, D = dxo.shape
    DI = o.shape[1]
    B = T // S
    tm = _tile(S, 256)
    per_b = S // tm

    def body(dxo_ref, y_ref, g_ref, w_ref, o_ref, z_ref, dy_ref, do_ref, dz_ref, dg_ref):
        dxo_v = dxo_ref[...]
        dy = (dxo_v * g_ref[0]).astype(BF16)
        dy_ref[...] = dy
        du = lax.dot_general(dy, w_ref[...], NT, preferred_element_type=F32)
        z = z_ref[...].astype(F32)
        sg = _sigmoid(z)
        do_ref[...] = (du * (z * sg)).astype(BF16)
        dz_ref[...] = (du * o_ref[...] * (sg * (1.0 + z * (1.0 - sg)))).astype(BF16)

        @pl.when(pl.program_id(0) % per_b == 0)
        def _():
            dg_ref[...] = jnp.zeros_like(dg_ref)

        dg_ref[0] += jnp.sum(dxo_v * y_ref[...], axis=0, keepdims=True)

    wide = pl.BlockSpec((tm, DI), lambda i: (i, 0))
    row = pl.BlockSpec((tm, D), lambda i: (i, 0))
    per = pl.BlockSpec((1, 1, D), lambda i: (i // per_b, 0, 0))
    return _call(
        body, name=name, grid=(T // tm,),
        in_specs=[row, row, per, pl.BlockSpec((DI, D), lambda i: (0, 0)), wide,
                  pl.BlockSpec((tm, DI), lambda i: (i, 3))],
        out_specs=[row, wide, wide, per],
        out_shape=[jax.ShapeDtypeStruct((T, D), BF16), jax.ShapeDtypeStruct((T, DI), BF16),
                   jax.ShapeDtypeStruct((T, DI), BF16), jax.ShapeDtypeStruct((B, 1, D), F32)],
        compiler_params=_params(),
    )(dxo, y, gate, w_out, o, proj)


def _final_loss(x, tgt, g, S, name):
    T, D = x.shape
    tm = _tile(S, 512)

    def body(x_ref, t_ref, g_ref, dx_ref, dg_ref, l_ref):
        @pl.when(pl.program_id(0) == 0)
        def _():
            dg_ref[...] = jnp.zeros_like(dg_ref)
            l_ref[...] = jnp.zeros_like(l_ref)

        xv = x_ref[...]
        gv = g_ref[...]
        r = lax.rsqrt(jnp.mean(xv * xv, axis=-1, keepdims=True) + NORM_EPS)
        xn = xv * r
        e = xn * gv - t_ref[...]
        part = jnp.sum(jnp.sum(e * e, axis=0, keepdims=True), axis=1, keepdims=True)
        l_ref[...] += (0.5 / D) * part
        dy = e * (1.0 / D)
        dg_ref[...] += jnp.sum(dy * xn, axis=0, keepdims=True)
        dxn = dy * gv
        dx_ref[...] = r * (dxn - xn * jnp.mean(dxn * xn, axis=-1, keepdims=True))

    row = pl.BlockSpec((tm, D), lambda i: (i, 0))
    return _call(
        body, name=name, grid=(T // tm,),
        in_specs=[row, row, pl.BlockSpec((1, D), lambda i: (0, 0))],
        out_specs=[row, pl.BlockSpec((1, D), lambda i: (0, 0)), pl.BlockSpec((1, LANES), lambda i: (0, 0))],
        out_shape=[jax.ShapeDtypeStruct((T, D), F32), jax.ShapeDtypeStruct((1, D), F32),
                   jax.ShapeDtypeStruct((1, LANES), F32)],
        compiler_params=_params(),
    )(x, tgt, g)


def _cum_fwd(fl, bf, name):
    B, S, _ = fl.shape
    ch = _tile(S, 256, 8)

    def body(fl_ref, b_ref, cum_ref):
        ri = lax.broadcasted_iota(jnp.int32, (ch, ch), 0)
        ci = lax.broadcasted_iota(jnp.int32, (ch, ch), 1)
        tri = jnp.where(ri >= ci, 1.0, 0.0).astype(BF16)

        def step(i, carry):
            r0 = pl.multiple_of(i * ch, ch)
            z = fl_ref[0, pl.ds(r0, ch), :] + b_ref[...]
            lf = jnp.minimum(z, 0.0) - jnp.log(1.0 + jnp.exp(-jnp.abs(z)))
            hi, mid, lo = _split3(lf)
            cs = (jnp.dot(tri, hi, preferred_element_type=F32) + jnp.dot(tri, mid, preferred_element_type=F32)
                  + jnp.dot(tri, lo, preferred_element_type=F32)) + carry
            cum_ref[0, pl.ds(r0, ch), :] = cs
            return cs[ch - 1:ch, :]

        lax.fori_loop(0, S // ch, step, jnp.zeros((1, LANES), F32))

    blk = pl.BlockSpec((1, S, LANES), lambda b: (b, 0, 0))
    return _call(
        body, name=name, grid=(B,), in_specs=[blk, pl.BlockSpec((1, LANES), lambda b: (0, 0))], out_specs=blk,
        out_shape=jax.ShapeDtypeStruct((B, S, LANES), F32), compiler_params=_params(),
    )(fl, bf)


def _cum_bwd(dcs, fl, bf, name):
    B, S, _ = fl.shape
    ch = _tile(S, 256, 8)
    n = S // ch

    def body(d_ref, fl_ref, b_ref, o_ref, db_ref):
        ri = lax.broadcasted_iota(jnp.int32, (ch, ch), 0)
        ci = lax.broadcasted_iota(jnp.int32, (ch, ch), 1)
        tri = jnp.where(ci >= ri, 1.0, 0.0).astype(BF16)

        @pl.when(pl.program_id(0) == 0)
        def _():
            db_ref[...] = jnp.zeros_like(db_ref)

        def step(t, carry):
            tail, dbsum = carry
            r0 = pl.multiple_of((n - 1 - t) * ch, ch)
            hi, mid, lo = _split3(d_ref[0, pl.ds(r0, ch), :])
            suf = (jnp.dot(tri, hi, preferred_element_type=F32) + jnp.dot(tri, mid, preferred_element_type=F32)
                   + jnp.dot(tri, lo, preferred_element_type=F32)) + tail
            z = fl_ref[0, pl.ds(r0, ch), :] + b_ref[...]
            dfl = -suf * _sigmoid(-z)
            o_ref[0, pl.ds(r0, ch), :] = dfl
            return suf[0:1, :], dbsum + jnp.sum(dfl, axis=0, keepdims=True)

        z1 = jnp.zeros((1, LANES), F32)
        _, dbsum = lax.fori_loop(0, n, step, (z1, z1))
        db_ref[...] += dbsum

    blk = pl.BlockSpec((1, S, LANES), lambda b: (b, 0, 0))
    vec = pl.BlockSpec((1, LANES), lambda b: (0, 0))
    return _call(
        body, name=name, grid=(B,), in_specs=[blk, blk, vec], out_specs=[blk, vec],
        out_shape=[jax.ShapeDtypeStruct((B, S, LANES), F32), jax.ShapeDtypeStruct((1, LANES), F32)],
        compiler_params=_params(),
    )(dcs, fl, bf)


HEADS_PER_STEP = 4
GROUP = 2 * HEAD_DIM


def _step_width():
    return HEAD_DIM * HEADS_PER_STEP


def _cols(S, offset_blocks=0):
    return pl.BlockSpec((S, _step_width()), lambda b, h: (b, offset_blocks + h))


def _row_spec(nq, tq):
    return pl.BlockSpec((1, HEADS_PER_STEP, nq, 1, tq), lambda b, h: (b, h, 0, 0, 0))


def _lanes(g):
    return slice(GROUP * (g // 2), GROUP * (g // 2) + GROUP)


def _hi_lo(x):
    hi = x.astype(BF16)
    return hi, (x - hi.astype(F32)).astype(BF16)


def _dot(a, b, dims=None):
    if dims is None:
        return jnp.dot(a, b, preferred_element_type=F32)
    return lax.dot_general(a, b, dims, preferred_element_type=F32)


def _causal_blocks(nq, prep, init, stages, finish, combine=None, descending=False):
    heads = range(HEADS_PER_STEP)

    def qloop(qi, _):
        ctx = [prep(g, qi) for g in heads]

        def step(kj, carry, masked):
            st = list(carry)
            for n, stage in enumerate(stages):
                if combine is not None and n == len(stages) - 1:
                    combine(kj, ctx, st)
                st = [stage(g, ctx[g], kj, masked, st[g]) for g in heads]
            return tuple(st)

        carry = tuple(init() for _ in heads)
        if descending:
            carry = step(qi, carry, True)
            carry = lax.fori_loop(0, qi, lambda t, cr: step(qi - 1 - t, cr, False), carry)
        else:
            carry = lax.fori_loop(0, qi, lambda kj, cr: step(kj, cr, False), carry)
            carry = step(qi, carry, True)
        finish(qi, ctx, carry)
        return 0

    lax.fori_loop(0, nq, qloop, 0)


class _Block:
    def __init__(self, tq):
        self.tq = tq
        self.lane = lax.broadcasted_iota(jnp.int32, (tq, GROUP), 1)
        self.low = self.lane < HEAD_DIM
        self.ri = lax.broadcasted_iota(jnp.int32, (tq, tq), 0)
        self.ci = lax.broadcasted_iota(jnp.int32, (tq, tq), 1)

    def rows(self, i):
        return pl.ds(pl.multiple_of(i * self.tq, self.tq), self.tq)

    def own(self, g, x):
        return jnp.where(self.low if g % 2 == 0 else jnp.logical_not(self.low), x, jnp.zeros_like(x))

    def pair(self, a, b):
        return jnp.where(self.low, a, b)

    def stat(self, g, x):
        return jnp.sum(jnp.where(self.lane == HEAD_DIM * (g % 2), x, 0.0), axis=1, keepdims=True)


def _fox_fwd(proj, cumcol, cumrow, name):
    T, DI = proj.shape[0], proj.shape[1] // 4
    B, H, nq, _, tq = cumrow.shape
    S = nq * tq
    nb = DI // _step_width()

    def body(q_ref, k_ref, v_ref, cc_ref, cr_ref, o_ref, st_ref):
        h0 = pl.program_id(1) * HEADS_PER_STEP
        blk = _Block(tq)

        def prep(g, qi):
            q = blk.own(g, q_ref[blk.rows(qi), _lanes(g)]) * 0.125
            ccol = jnp.sum(jnp.where(blk.lane == h0 + g, cc_ref[0, blk.rows(qi), :], 0.0), axis=1, keepdims=True)
            return q, ccol

        def init():
            return jnp.full((tq, 1), -jnp.inf, F32), jnp.zeros((tq, 1), F32), jnp.zeros((tq, GROUP), F32)

        def scores(g, ctx, kj, masked, st):
            return st + (_dot(ctx[0], k_ref[blk.rows(kj), _lanes(g)], NT),)

        def softmax(g, ctx, kj, masked, st):
            m, l, acc, s = st
            s = s + ctx[1] - cr_ref[0, g, kj]
            if masked:
                s = jnp.where(blk.ci <= blk.ri, s, -jnp.inf)
            m_new = jnp.maximum(m, jnp.max(s, axis=1, keepdims=True))
            alpha = jnp.exp(m - m_new)
            p = jnp.exp(s - m_new)
            return (m_new, alpha * l + jnp.sum(p, axis=1, keepdims=True), acc, alpha) + _hi_lo(p)

        def values(g, ctx, kj, masked, st):
            m, l, acc, alpha, hi, lo = st
            v = v_ref[blk.rows(kj), _lanes(g)]
            return m, l, alpha * acc + (_dot(hi, v) + _dot(lo, v))

        def finish(qi, ctx, carry):
            for g in range(0, HEADS_PER_STEP, 2):
                (m0, l0, a0), (m1, l1, a1) = carry[g], carry[g + 1]
                o_ref[blk.rows(qi), _lanes(g)] = blk.pair(a0 / l0, a1 / l1)
                st_ref[blk.rows(qi), _lanes(g)] = blk.pair(m0 + jnp.log(l0), m1 + jnp.log(l1))

        _causal_blocks(nq, prep, init, [scores, softmax, values], finish)

    out = jax.ShapeDtypeStruct((T, DI), F32)
    return _call(
        body, name=name, grid=(B, H // HEADS_PER_STEP),
        in_specs=[_cols(S), _cols(S, nb), _cols(S, 2 * nb), pl.BlockSpec((1, S, LANES), lambda b, h: (b, 0, 0)),
                  _row_spec(nq, tq)],
        out_specs=[_cols(S), _cols(S)], out_shape=[out, out], compiler_params=_params(),
    )(proj, proj, proj, cumcol, cumrow)


def _fox_bwd(proj, do, o, stat, cumcol, cumrow, name):
    T, DI = do.shape
    B, H, nq, _, tq = cumrow.shape
    S = nq * tq
    nb = DI // _step_width()

    def body(q_ref, k_ref, v_ref, do_ref, o_ref, st_ref, cc_ref, cr_ref, dqkv_ref, dcs_ref, dk_acc, dv_acc):
        h0 = pl.program_id(1) * HEADS_PER_STEP
        blk = _Block(tq)
        dk_acc[...] = jnp.zeros_like(dk_acc)
        dv_acc[...] = jnp.zeros_like(dv_acc)
        dcs_ref[...] = jnp.zeros_like(dcs_ref)

        def prep(g, qi):
            q = blk.own(g, q_ref[blk.rows(qi), _lanes(g)]) * 0.125
            dout = blk.own(g, do_ref[blk.rows(qi), _lanes(g)])
            delta = jnp.sum(o_ref[blk.rows(qi), _lanes(g)] * dout.astype(F32), axis=1, keepdims=True)
            lse = blk.stat(g, st_ref[blk.rows(qi), _lanes(g)])
            ccol = jnp.sum(jnp.where(blk.lane == h0 + g, cc_ref[0, blk.rows(qi), :], 0.0), axis=1, keepdims=True)
            return q, dout, lse, delta, ccol

        def init():
            return (jnp.zeros((tq, GROUP), F32),)

        def scores(g, ctx, kj, masked, st):
            return st + (_dot(ctx[0], k_ref[blk.rows(kj), _lanes(g)], NT),
                         _dot(ctx[1], v_ref[blk.rows(kj), _lanes(g)], NT))

        def softmax_bwd(g, ctx, kj, masked, st):
            dq, s, dp = st
            _, _, lse, delta, ccol = ctx
            s = s + ccol - cr_ref[0, g, kj]
            if masked:
                s = jnp.where(blk.ci <= blk.ri, s, -jnp.inf)
            p = jnp.exp(s - lse)
            ds = p * (dp - delta)
            return dq, p.astype(BF16), ds.astype(BF16), jnp.sum(ds, axis=0, keepdims=True)

        def combine(kj, ctx, st):
            for g in range(0, HEADS_PER_STEP, 2):
                dv_acc[blk.rows(kj), _lanes(g)] += _dot(st[g][1], ctx[g][1], TN) + _dot(st[g + 1][1], ctx[g + 1][1], TN)
                dk_acc[blk.rows(kj), _lanes(g)] += _dot(st[g][2], ctx[g][0], TN) + _dot(st[g + 1][2], ctx[g + 1][0], TN)
            for g in range(HEADS_PER_STEP):
                dcs_ref[0, g, kj] += st[g][3]

        def queries(g, ctx, kj, masked, st):
            dq, _, dsb, _ = st
            return (dq + _dot(dsb, blk.own(g, k_ref[blk.rows(kj), _lanes(g)])),)

        def finish(qi, ctx, carry):
            for g in range(0, HEADS_PER_STEP, 2):
                dqkv_ref[0, blk.rows(qi), _lanes(g)] = ((carry[g][0] + carry[g + 1][0]) * 0.125).astype(BF16)

        _causal_blocks(nq, prep, init, [scores, softmax_bwd, queries], finish, combine=combine)
        dqkv_ref[1] = dk_acc[...].astype(BF16)
        dqkv_ref[2] = dv_acc[...].astype(BF16)

    W = _step_width()
    return _call(
        body, name=name, grid=(B, H // HEADS_PER_STEP),
        in_specs=[_cols(S), _cols(S, nb), _cols(S, 2 * nb), _cols(S), _cols(S), _cols(S),
                  pl.BlockSpec((1, S, LANES), lambda b, h: (b, 0, 0)), _row_spec(nq, tq)],
        out_specs=[pl.BlockSpec((3, S, W), lambda b, h: (0, b, h)), _row_spec(nq, tq)],
        out_shape=[jax.ShapeDtypeStruct((3, T, DI), BF16), jax.ShapeDtypeStruct((B, H, nq, 1, tq), F32)],
        scratch_shapes=[pltpu.VMEM((S, W), F32), pltpu.VMEM((S, W), F32)], compiler_params=_params(),
    )(proj, proj, proj, do, o, stat, cumcol, cumrow)


def _softplus_parts(z):
    e = jnp.exp(-jnp.abs(z))
    return jnp.maximum(z, 0.0) + jnp.log(1.0 + e), e


def _sb_fwd(proj, B, tq, name):
    T, DI = proj.shape[0], proj.shape[1] // 4
    S = T // B
    H = DI // HEAD_DIM
    nq = S // tq
    nb = DI // _step_width()

    def body(q_ref, k_ref, v_ref, o_ref, st_ref):
        blk = _Block(tq)
        strict = blk.ci < blk.ri
        above = jnp.where(blk.ri > blk.ci, 1.0, 0.0).astype(BF16)

        def prep(g, qi):
            return blk.own(g, q_ref[blk.rows(qi), _lanes(g)]) * 0.125

        def init():
            return jnp.zeros((tq, 1), F32), jnp.zeros((tq, GROUP), F32)

        def scores(g, q, kj, masked, st):
            return st + (_dot(q, k_ref[blk.rows(kj), _lanes(g)], NT),)

        def logs(g, q, kj, masked, st):
            c, acc, z = st
            sp, _ = _softplus_parts(z)
            lk = -sp
            if masked:
                lk = jnp.where(strict, lk, 0.0)
            return (c, acc, z - sp, jnp.sum(lk, axis=1, keepdims=True)) + _hi_lo(lk)

        def suffix(g, q, kj, masked, st):
            c, acc, lb, lksum, hi, lo = st
            return c, acc, lb, lksum, _dot(hi, above) + _dot(lo, above)

        def weights(g, q, kj, masked, st):
            c, acc, lb, lksum, after = st
            a = jnp.exp(lb + after + c)
            if masked:
                a = jnp.where(strict, a, 0.0)
            return c + lksum, acc, a.astype(BF16)

        def values(g, q, kj, masked, st):
            c, acc, ab = st
            return c, acc + _dot(ab, v_ref[blk.rows(kj), _lanes(g)])

        def finish(qi, ctx, carry):
            for g in range(0, HEADS_PER_STEP, 2):
                (c0, a0), (c1, a1) = carry[g], carry[g + 1]
                o_ref[blk.rows(qi), _lanes(g)] = blk.pair(a0, a1)
                st_ref[blk.rows(qi), _lanes(g)] = blk.pair(c0, c1)

        _causal_blocks(nq, prep, init, [scores, logs, suffix, weights, values], finish, descending=True)

    out = jax.ShapeDtypeStruct((T, DI), F32)
    return _call(
        body, name=name, grid=(B, H // HEADS_PER_STEP), in_specs=[_cols(S), _cols(S, nb), _cols(S, 2 * nb)],
        out_specs=[_cols(S), _cols(S)], out_shape=[out, out], compiler_params=_params(),
    )(proj, proj, proj)


def _sb_bwd(proj, do, stat, B, tq, name):
    T, DI = do.shape
    S = T // B
    H = DI // HEAD_DIM
    nq = S // tq
    nb = DI // _step_width()

    def body(q_ref, k_ref, v_ref, do_ref, st_ref, dqkv_ref, dk_acc, dv_acc):
        blk = _Block(tq)
        strict = blk.ci < blk.ri
        upto = jnp.where(blk.ri <= blk.ci, 1.0, 0.0).astype(BF16)
        before = jnp.where(blk.ri < blk.ci, 1.0, 0.0).astype(BF16)
        dk_acc[...] = jnp.zeros_like(dk_acc)
        dv_acc[...] = jnp.zeros_like(dv_acc)

        def prep(g, qi):
            return (blk.own(g, q_ref[blk.rows(qi), _lanes(g)]) * 0.125, blk.own(g, do_ref[blk.rows(qi), _lanes(g)]),
                    blk.stat(g, st_ref[blk.rows(qi), _lanes(g)]))

        def init():
            return jnp.zeros((tq, 1), F32), jnp.zeros((tq, 1), F32), jnp.zeros((tq, GROUP), F32)

        def scores(g, ctx, kj, masked, st):
            return st + (_dot(ctx[0], k_ref[blk.rows(kj), _lanes(g)], NT),
                         _dot(ctx[1], v_ref[blk.rows(kj), _lanes(g)], NT))

        def logs(g, ctx, kj, masked, st):
            cpre, pg, dq, z, da = st
            sp, e = _softplus_parts(z)
            inv = 1.0 / (1.0 + e)
            sig = jnp.where(z >= 0.0, inv, e * inv)
            lk = -sp
            if masked:
                lk = jnp.where(strict, lk, 0.0)
            return (cpre, pg, dq, da, z - sp, sig, jnp.sum(lk, axis=1, keepdims=True)) + _hi_lo(lk)

        def prefix(g, ctx, kj, masked, st):
            cpre, pg, dq, da, lb, sig, lksum, hi, lo = st
            return cpre, pg, dq, da, lb, sig, lksum, _dot(hi, upto) + _dot(lo, upto)

        def weights(g, ctx, kj, masked, st):
            cpre, pg, dq, da, lb, sig, lksum, pre = st
            a = jnp.exp(lb + (ctx[2] - (cpre + pre)))
            if masked:
                a = jnp.where(strict, a, 0.0)
            gr = da * a
            return cpre + lksum, pg, dq, sig, a.astype(BF16), gr, gr.astype(BF16)

        def grad_prefix(g, ctx, kj, masked, st):
            cpre, pg, dq, sig, ab, gr, gb = st
            return cpre, pg, dq, sig, ab, gr, _dot(gb, before)

        def dlogits(g, ctx, kj, masked, st):
            cpre, pg, dq, sig, ab, gr, pfx = st
            dz = gr * (1.0 - sig) - (pfx + pg) * sig
            if masked:
                dz = jnp.where(strict, dz, 0.0)
            return cpre, pg + jnp.sum(gr, axis=1, keepdims=True), dq, ab, dz.astype(BF16)

        def combine(kj, ctx, st):
            for g in range(0, HEADS_PER_STEP, 2):
                dv_acc[blk.rows(kj), _lanes(g)] += _dot(st[g][3], ctx[g][1], TN) + _dot(st[g + 1][3], ctx[g + 1][1], TN)
                dk_acc[blk.rows(kj), _lanes(g)] += _dot(st[g][4], ctx[g][0], TN) + _dot(st[g + 1][4], ctx[g + 1][0], TN)

        def queries(g, ctx, kj, masked, st):
            cpre, pg, dq, _, dzb = st
            return cpre, pg, dq + _dot(dzb, blk.own(g, k_ref[blk.rows(kj), _lanes(g)]))

        def finish(qi, ctx, carry):
            for g in range(0, HEADS_PER_STEP, 2):
                dqkv_ref[0, blk.rows(qi), _lanes(g)] = ((carry[g][2] + carry[g + 1][2]) * 0.125).astype(BF16)

        _causal_blocks(nq, prep, init, [scores, logs, prefix, weights, grad_prefix, dlogits, queries], finish,
                       combine=combine)
        dqkv_ref[1] = dk_acc[...].astype(BF16)
        dqkv_ref[2] = dv_acc[...].astype(BF16)

    W = _step_width()
    return _call(
        body, name=name, grid=(B, H // HEADS_PER_STEP),
        in_specs=[_cols(S), _cols(S, nb), _cols(S, 2 * nb), _cols(S), _cols(S)],
        out_specs=pl.BlockSpec((3, S, W), lambda b, h: (0, b, h)),
        out_shape=jax.ShapeDtypeStruct((3, T, DI), BF16),
        scratch_shapes=[pltpu.VMEM((S, W), F32), pltpu.VMEM((S, W), F32)], compiler_params=_params(),
    )(proj, proj, proj, do, stat)


def _row_tile(R, C, n_arrays):
    budget = 24 * 1024 * 1024 // (2 * n_arrays * 4 * max(C, LANES))
    return _tile(R, max(8, budget), 8)


def _ew_sum(parts, name, also_bf16=False):
    R, C = parts[0].shape
    tr = _row_tile(R, C, len(parts) + 2)
    n = len(parts)

    def body(*refs):
        acc = refs[0][...].astype(F32) + refs[1][...].astype(F32)
        for r in refs[2:n]:
            acc = acc + r[...].astype(F32)
        refs[n][...] = acc
        if also_bf16:
            refs[n + 1][...] = acc.astype(BF16)

    blk = pl.BlockSpec((tr, C), lambda i: (i, 0))
    out_shape = [jax.ShapeDtypeStruct((R, C), F32)] + ([jax.ShapeDtypeStruct((R, C), BF16)] if also_bf16 else [])
    return _call(
        body, name=name, grid=(R // tr,), in_specs=[blk] * n, out_specs=[blk] * len(out_shape),
        out_shape=out_shape, compiler_params=_params(),
    )(*parts)


def _adamw(w, g, m, v, name):
    R, C = w.shape
    tr = _row_tile(R, C, 7)
    c1 = 1.0 / (1.0 - ADAM_B1 ** ADAM_STEP)
    c2 = 1.0 / (1.0 - ADAM_B2 ** ADAM_STEP)

    def body(w_ref, g_ref, m_ref, v_ref, d_ref, m2_ref, v2_ref):
        gv = g_ref[...]
        m2 = ADAM_B1 * m_ref[...] + (1.0 - ADAM_B1) * gv
        v2 = ADAM_B2 * v_ref[...] + (1.0 - ADAM_B2) * (gv * gv)
        m2_ref[...] = m2
        v2_ref[...] = v2
        d_ref[...] = -ADAM_LR * ((m2 * c1) / (jnp.sqrt(v2 * c2) + ADAM_EPS) + ADAM_WD * w_ref[...])

    blk = pl.BlockSpec((tr, C), lambda i: (i, 0))
    out = jax.ShapeDtypeStruct((R, C), F32)
    return _call(
        body, name=name, grid=(R // tr,), in_specs=[blk] * 4, out_specs=[blk] * 3, out_shape=[out] * 3,
        compiler_params=_params(),
    )(w, g, m, v)


def _me():
    return lax.axis_index("x"), lax.axis_index("y"), lax.axis_index("c")


def _chip_of(x, y):
    return 2 * x + y


def _other_chips(x, y):
    return [(x, 1 - y), (1 - x, y), (1 - x, 1 - y)]


def _gather_weights(halves, smalls):
    nh, ns = len(halves), len(smalls)

    def body(*refs):
        ins_h, ins_s = refs[:nh], refs[nh:nh + ns]
        outs_h, outs_s = refs[nh + ns:2 * nh + ns], refs[2 * nh + ns:2 * (nh + ns)]
        send1, recv1, send2, recv2, send3, recv3 = refs[2 * (nh + ns):]
        x, y, c = _me()
        mine = _chip_of(x, y)
        chips = _other_chips(x, y)
        sib = (x, y, 1 - c)

        def landed(i, k, half):
            return outs_h[i].at[_chip_of(*chips[k]), half]

        def first(i, k):
            return pltpu.make_async_remote_copy(
                src_ref=ins_h[i].at[c], dst_ref=outs_h[i].at[mine, c], send_sem=send1.at[i, k], recv_sem=recv1.at[i, k],
                device_id=(*chips[k], c), device_id_type=MESH)

        def passed(i, k):
            return pltpu.make_async_remote_copy(
                src_ref=landed(i, k, c), dst_ref=landed(i, k, c), send_sem=send2.at[i, k], recv_sem=recv2.at[i, k],
                device_id=sib, device_id_type=MESH)

        def small(i, k):
            return pltpu.make_async_remote_copy(
                src_ref=ins_s[i], dst_ref=outs_s[i].at[mine], send_sem=send3.at[i, k], recv_sem=recv3.at[i, k],
                device_id=(*chips[k], c), device_id_type=MESH)

        for i in range(nh):
            for k in range(3):
                first(i, k).start()
        for i in range(ns):
            for k in range(3):
                small(i, k).start()
        for i in range(nh):
            for k in range(3):
                pltpu.make_async_remote_copy(
                    src_ref=ins_h[i].at[c], dst_ref=landed(i, k, c), send_sem=send1.at[i, k], recv_sem=recv1.at[i, k],
                    device_id=(*chips[k], c), device_id_type=MESH).wait_recv()
                passed(i, k).start()
        for i in range(nh):
            for k in range(3):
                pltpu.make_async_remote_copy(
                    src_ref=landed(i, k, c), dst_ref=landed(i, k, 1 - c), send_sem=send2.at[i, k],
                    recv_sem=recv2.at[i, k], device_id=sib, device_id_type=MESH).wait_recv()
        for i in range(ns):
            for k in range(3):
                pltpu.make_async_remote_copy(
                    src_ref=ins_s[i], dst_ref=outs_s[i].at[_chip_of(*chips[k])], send_sem=send3.at[i, k],
                    recv_sem=recv3.at[i, k], device_id=(*chips[k], c), device_id_type=MESH).wait_recv()
        for i in range(nh):
            for k in range(3):
                first(i, k).wait_send()
                passed(i, k).wait_send()
        for i in range(ns):
            for k in range(3):
                small(i, k).wait_send()

    out_shape = ([jax.ShapeDtypeStruct((4,) + a.shape, a.dtype) for a in halves]
                 + [jax.ShapeDtypeStruct((4,) + a.shape, a.dtype) for a in smalls])
    n = nh + ns
    res = _call(
        body, name="gather_weights", in_specs=[HBM] * n, out_specs=[HBM] * n, out_shape=out_shape,
        scratch_shapes=[pltpu.SemaphoreType.DMA((nh, 3)), pltpu.SemaphoreType.DMA((nh, 3)),
                        pltpu.SemaphoreType.DMA((nh, 3)), pltpu.SemaphoreType.DMA((nh, 3)),
                        pltpu.SemaphoreType.DMA((max(ns, 1), 3)), pltpu.SemaphoreType.DMA((max(ns, 1), 3))],
        compiler_params=_params(),
    )(*halves, *smalls)
    return res[:nh], res[nh:]


def _pair_exchange(grads):
    n = len(grads)

    def body(*refs):
        ins, got = refs[:n], refs[n:2 * n]
        send, recv = refs[2 * n:]
        x, y, c = _me()
        cps = []
        for i in range(n):
            for j in range(4):
                r = pltpu.make_async_remote_copy(
                    src_ref=ins[i].at[j, 1 - c], dst_ref=got[i].at[j], send_sem=send.at[i, j], recv_sem=recv.at[i, j],
                    device_id=(x, y, 1 - c), device_id_type=MESH)
                r.start()
                cps.append(r)
        for r in cps:
            r.wait()

    return _call(
        body, name="grad_pair_exchange", in_specs=[HBM] * n, out_specs=[HBM] * n,
        out_shape=[jax.ShapeDtypeStruct((4,) + g.shape[2:], g.dtype) for g in grads],
        scratch_shapes=[pltpu.SemaphoreType.DMA((n, 4)), pltpu.SemaphoreType.DMA((n, 4))],
        compiler_params=_params(),
    )(*grads)


def _chip_exchange(sums):
    n = len(sums)

    def body(*refs):
        ins, got = refs[:n], refs[n:2 * n]
        send, recv = refs[2 * n:]
        x, y, c = _me()
        chips = _other_chips(x, y)
        cps = []
        for i in range(n):
            for k in range(3):
                r = pltpu.make_async_remote_copy(
                    src_ref=ins[i].at[_chip_of(*chips[k])], dst_ref=got[i].at[k], send_sem=send.at[i, k],
                    recv_sem=recv.at[i, k], device_id=(*chips[k], c), device_id_type=MESH)
                r.start()
                cps.append(r)
        for r in cps:
            r.wait()

    return _call(
        body, name="grad_chip_exchange", in_specs=[HBM] * n, out_specs=[HBM] * n,
        out_shape=[jax.ShapeDtypeStruct((3,) + s.shape[1:], s.dtype) for s in sums],
        scratch_shapes=[pltpu.SemaphoreType.DMA((n, 3)), pltpu.SemaphoreType.DMA((n, 3))],
        compiler_params=_params(),
    )(*sums)


def _pair_share(halves):
    n = len(halves)

    def body(*refs):
        ins, outs = refs[:n], refs[n:2 * n]
        send, recv = refs[2 * n:]
        x, y, c = _me()
        cps = []
        for i in range(n):
            r = pltpu.make_async_remote_copy(
                src_ref=ins[i], dst_ref=outs[i], send_sem=send.at[i], recv_sem=recv.at[i],
                device_id=(x, y, 1 - c), device_id_type=MESH)
            r.start()
            cps.append(r)
        for r in cps:
            r.wait()

    return _call(
        body, name="grad_pair_share", in_specs=[HBM] * n, out_specs=[HBM] * n,
        out_shape=[jax.ShapeDtypeStruct(h.shape, h.dtype) for h in halves],
        scratch_shapes=[pltpu.SemaphoreType.DMA((n,)), pltpu.SemaphoreType.DMA((n,))],
        compiler_params=_params(),
    )(*halves)


def _allreduce_small(vec):
    P = vec.shape[1]

    def body(v_ref, sum_ref, all_ref, send, recv):
        x, y, c = _me()
        me = 4 * x + 2 * y + c
        all_ref[pl.ds(me, 1)] = v_ref[...][None]
        cps = []
        for d in range(1, 8):
            peer = (jnp.bitwise_xor(x, d >> 2), jnp.bitwise_xor(y, (d >> 1) & 1), jnp.bitwise_xor(c, d & 1))
            r = pltpu.make_async_remote_copy(
                src_ref=v_ref, dst_ref=all_ref.at[me], send_sem=send.at[d - 1], recv_sem=recv.at[d - 1],
                device_id=peer, device_id_type=MESH)
            r.start()
            cps.append(r)
        for d in range(1, 8):
            src = jnp.bitwise_xor(me, d)
            pltpu.make_async_remote_copy(
                src_ref=v_ref, dst_ref=all_ref.at[src], send_sem=send.at[d - 1], recv_sem=recv.at[d - 1],
                device_id=(x, y, c), device_id_type=MESH).wait_recv()
        for r in cps:
            r.wait_send()
        acc = all_ref[0]
        for i in range(1, 8):
            acc = acc + all_ref[i]
        sum_ref[...] = acc

    vm = pl.BlockSpec(memory_space=pltpu.VMEM)
    return _call(
        body, name="allreduce_small", in_specs=[vm], out_specs=[vm, vm],
        out_shape=[jax.ShapeDtypeStruct((8, P), F32), jax.ShapeDtypeStruct((8, 8, P), F32)],
        scratch_shapes=[pltpu.SemaphoreType.DMA((7,)), pltpu.SemaphoreType.DMA((7,))],
        compiler_params=_params(),
    )(vec)[0]


def _per_batch(mod, B, D):
    return [mod[:B, i * D:(i + 1) * D].reshape(B, 1, D) for i in range(3)]


def _pad_rows8(a):
    return jnp.concatenate([a, jnp.zeros((8 - a.shape[0],) + a.shape[1:], a.dtype)], axis=0)


def _layer_fwd(x, c8, w, S, fox, tag):
    T, D = x.shape
    B = T // S
    DI = w["w_out"].shape[0]
    H = DI // HEAD_DIM
    tq = _tile(S, ATT_BLOCK, 8)
    mod = _mod_fwd(c8, w["w_ada"], w["b_ada"], tag + "_mod_fwd")
    shift, scale, gate = _per_batch(mod, B, D)
    proj, h = _ln_proj(x, shift, scale, w["norm_g"], w["w_in"], S, tag + "_ln_proj")
    saved = dict(x=x, h=h, proj=proj, scale=scale, gate=gate)
    if fox:
        fl = _mm(h, w["w_f"], "nn", F32, tag + "_flogit").reshape(B, S, LANES)
        cum = _cum_fwd(fl, w["b_f"], tag + "_cum_fwd")
        cumrow = cum[:, :, :H].transpose(0, 2, 1).reshape(B, H, S // tq, 1, tq)
        o, stat = _fox_fwd(proj, cum, cumrow, tag + "_attn_fwd")
        saved.update(fl=fl, cum=cum, cumrow=cumrow)
    else:
        o, stat = _sb_fwd(proj, B, tq, tag + "_attn_fwd")
    xo, y, u = _gate_out(o, proj, w["w_out"], x, gate, S, tag + "_gate_out")
    saved.update(o=o, stat=stat, y=y, u=u)
    return xo, saved


def _layer_bwd(dxo, sv, w, cT, S, fox, tag):
    T, D = dxo.shape
    B = T // S
    DI = w["w_out"].shape[0]
    H = DI // HEAD_DIM
    tq = _tile(S, ATT_BLOCK, 8)
    dy, do, dzg, dgate = _out_bwd(dxo, sv["y"], sv["gate"], w["w_out"], sv["o"], sv["proj"], S, tag + "_out_bwd")
    g = {"w_out": _mm(sv["u"], dy, "tn", F32, tag + "_dw_out", tm=1024, tn=1024, tk=2048)}
    if fox:
        dqkv, dcs = _fox_bwd(sv["proj"], do, sv["o"], sv["stat"], sv["cum"], sv["cumrow"], tag + "_attn_bwd")
        dcs = dcs.reshape(B, H, S).transpose(0, 2, 1)
        dcs = jnp.concatenate([dcs, jnp.zeros((B, S, LANES - H), F32)], axis=-1)
        dfl, db_f = _cum_bwd(dcs, sv["fl"], w["b_f"], tag + "_cum_bwd")
        g["b_f"] = db_f[:, :H]
        tail = [dfl.reshape(T, LANES).astype(BF16)]
        w_in = jnp.concatenate([w["w_in"], w["w_f"]], axis=1)
    else:
        dqkv = _sb_bwd(sv["proj"], do, sv["stat"], B, tq, tag + "_attn_bwd")
        tail = []
        w_in = w["w_in"]
    dproj = jnp.concatenate([dqkv[0], dqkv[1], dqkv[2], dzg] + tail, axis=1)
    N = dproj.shape[1]
    dw_in = _mm(sv["h"], dproj, "tn", F32, tag + "_dw_in", tm=1024, tn=640 if N % 640 == 0 else 512, tk=2048)
    g["w_in"] = dw_in[:, :4 * DI + H] if fox else dw_in
    dh = _mm(dproj, w_in, "nt", F32, tag + "_dh", tm=512, tn=1024, tk=1664 if N % 1664 == 0 else 2048)
    dx, dshift, dscale, dg = _ln_bwd(dh, sv["x"], dxo, sv["scale"], w["norm_g"], S, tag + "_ln_bwd")
    g["norm_g"] = dg
    dmod = jnp.concatenate([dshift, dscale, dgate], axis=-1).reshape(B, 3 * D)
    g["w_ada"], g["b_ada"] = _mod_bwd(cT, _pad_rows8(dmod), B, tag + "_mod_bwd")
    return dx, g


def _local_step(x3, c, tgt3, wf, ws, final_g):
    B, S, D = x3.shape
    T = B * S
    x = x3.reshape(T, D)
    c8 = _pad_rows8(c)
    cT = c8.T
    x1, sv1 = _layer_fwd(x, c8, wf, S, True, "fox")
    x2, sv2 = _layer_fwd(x1, c8, ws, S, False, "sb")
    dx2, dgf, loss = _final_loss(x2, tgt3.reshape(T, D), final_g, S, "final_loss")
    dx1, gs = _layer_bwd(dx2, sv2, ws, cT, S, False, "sb")
    dx0, gf = _layer_bwd(dx1, sv1, wf, cT, S, True, "fox")
    return loss, dx0.reshape(B, S, D), gf, gs, dgf


def _cols_to_shards(a):
    R, C4 = a.shape
    return a.reshape(R, 4, C4 // 4).transpose(1, 0, 2)


def _shards_to_cols(a):
    n, R, C = a.shape
    return a.transpose(1, 0, 2).reshape(R, n * C)


def kernel(x, c, fox_norm_g, fox_w_ada, fox_b_ada, fox_w_in, fox_b_f, fox_w_out, sb_norm_g, sb_w_ada, sb_b_ada, sb_w_in, sb_w_out, final_norm_g, loss_target, m_fox_norm_g, m_fox_w_ada, m_fox_b_ada, m_fox_w_in, m_fox_b_f, m_fox_w_out, m_sb_norm_g, m_sb_w_ada, m_sb_b_ada, m_sb_w_in, m_sb_w_out, m_final_norm_g, v_fox_norm_g, v_fox_w_ada, v_fox_b_ada, v_fox_w_in, v_fox_b_f, v_fox_w_out, v_sb_norm_g, v_sb_w_ada, v_sb_b_ada, v_sb_w_in, v_sb_w_out, v_final_norm_g):
    B, S, D = x.shape
    DI = 4 * fox_w_out.shape[1]
    H = DI // HEAD_DIM
    chip = _chip_of(lax.axis_index("x"), lax.axis_index("y"))

    big_names = ["fox_w_ada", "fox_w_in", "fox_w_out", "sb_w_ada", "sb_w_in", "sb_w_out"]
    big = dict(fox_w_ada=fox_w_ada[0], fox_w_in=fox_w_in[0], fox_w_out=fox_w_out[0],
               sb_w_ada=sb_w_ada[0], sb_w_in=sb_w_in[0], sb_w_out=sb_w_out[0])
    halves = [big[n].astype(BF16).reshape(2, big[n].shape[0] // 2, big[n].shape[1]) for n in big_names]
    gathered, gsmall = _gather_weights(halves, [sb_norm_g, sb_b_ada])
    gathered = [lax.dynamic_update_index_in_dim(a, own, chip, 0) for a, own in zip(gathered, halves)]
    gsmall = [lax.dynamic_update_index_in_dim(a, own, chip, 0) for a, own in zip(gsmall, [sb_norm_g, sb_b_ada])]
    full = {}
    for n, a in zip(big_names, gathered):
        a = a.reshape(4, a.shape[1] * a.shape[2], a.shape[3])
        full[n] = a.reshape(4 * a.shape[1], a.shape[2]) if n.endswith("w_out") else _shards_to_cols(a)
    sb_norm_full = gsmall[0].reshape(1, D)
    sb_b_ada_full = gsmall[1].reshape(1, 3 * D)
    w_f = jnp.concatenate([full["fox_w_in"][:, 4 * DI:], jnp.zeros((D, LANES - H), BF16)], axis=1)
    b_f = jnp.concatenate([fox_b_f, jnp.zeros((1, LANES - H), F32)], axis=1)
    wf = dict(w_ada=full["fox_w_ada"], b_ada=fox_b_ada, norm_g=fox_norm_g, w_in=full["fox_w_in"][:, :4 * DI],
              w_f=w_f, b_f=b_f, w_out=full["fox_w_out"])
    ws = dict(w_ada=full["sb_w_ada"], b_ada=sb_b_ada_full, norm_g=sb_norm_full, w_in=full["sb_w_in"],
              w_out=full["sb_w_out"])

    loss, grad_x, gf, gs, dgf = _local_step(x, c, loss_target, wf, ws, final_norm_g.reshape(1, D))

    part = dict(fox_w_ada=gf["w_ada"], fox_w_in=gf["w_in"], fox_w_out=gf["w_out"],
                sb_w_ada=gs["w_ada"], sb_w_in=gs["w_in"], sb_w_out=gs["w_out"])
    shard_major = []
    for n in big_names:
        a = part[n]
        a = a.reshape(4, a.shape[0] // 4, a.shape[1]) if n.endswith("w_out") else _cols_to_shards(a)
        shard_major.append(a.reshape(4, 2, a.shape[1] // 2, a.shape[2]))
    core = lax.axis_index("c")
    got = _pair_exchange(shard_major)
    pair_f32, pair_bf16 = [], []
    for n, g4, b in zip(big_names, shard_major, got):
        a = lax.dynamic_index_in_dim(g4, core, axis=1, keepdims=False)
        r, C = a.shape[1:]
        s32, s16 = _ew_sum([a.reshape(4 * r, C), b.reshape(4 * r, C)], n + "_pair_sum", also_bf16=True)
        pair_f32.append(s32.reshape(4, r, C))
        pair_bf16.append(s16.reshape(4, r, C))
    others = _chip_exchange(pair_bf16)
    reduced_halves = [_ew_sum([lax.dynamic_index_in_dim(a, chip, axis=0, keepdims=False), b[0], b[1], b[2]],
                              n + "_chip_sum")[0] for n, a, b in zip(big_names, pair_f32, others)]
    theirs = _pair_share(reduced_halves)
    grad_big = {}
    for n, a, b in zip(big_names, reduced_halves, theirs):
        grad_big[n] = jnp.concatenate([jnp.where(core == 0, a, b), jnp.where(core == 0, b, a)], axis=0)

    pieces = [loss, gf["norm_g"], gf["b_ada"], jnp.concatenate([gf["b_f"], jnp.zeros((1, LANES - H), F32)], axis=1),
              gs["norm_g"], gs["b_ada"], dgf]
    vec = jnp.concatenate(pieces, axis=1)
    red = _allreduce_small(_pad_rows8(vec))[0:1]
    offs = [0]
    for p in pieces:
        offs.append(offs[-1] + p.shape[1])
    r_loss, r_fng, r_fba, r_fbf, r_sng, r_sba, r_fin = [red[:, offs[i]:offs[i + 1]] for i in range(7)]
    small_grads = dict(
        fox_norm_g=r_fng, fox_b_ada=r_fba, fox_b_f=r_fbf[:, :H],
        sb_norm_g=lax.dynamic_slice_in_dim(r_sng, chip * (D // 4), D // 4, axis=1),
        sb_b_ada=lax.dynamic_slice_in_dim(r_sba, chip * (3 * D // 4), 3 * D // 4, axis=1),
        final_norm_g=r_fin)

    weights = dict(fox_norm_g=fox_norm_g, fox_w_ada=fox_w_ada, fox_b_ada=fox_b_ada, fox_w_in=fox_w_in, fox_b_f=fox_b_f,
                   fox_w_out=fox_w_out, sb_norm_g=sb_norm_g, sb_w_ada=sb_w_ada, sb_b_ada=sb_b_ada, sb_w_in=sb_w_in,
                   sb_w_out=sb_w_out, final_norm_g=final_norm_g)
    ms = dict(fox_norm_g=m_fox_norm_g, fox_w_ada=m_fox_w_ada, fox_b_ada=m_fox_b_ada, fox_w_in=m_fox_w_in,
              fox_b_f=m_fox_b_f, fox_w_out=m_fox_w_out, sb_norm_g=m_sb_norm_g, sb_w_ada=m_sb_w_ada,
              sb_b_ada=m_sb_b_ada, sb_w_in=m_sb_w_in, sb_w_out=m_sb_w_out, final_norm_g=m_final_norm_g)
    vs = dict(fox_norm_g=v_fox_norm_g, fox_w_ada=v_fox_w_ada, fox_b_ada=v_fox_b_ada, fox_w_in=v_fox_w_in,
              fox_b_f=v_fox_b_f, fox_w_out=v_fox_w_out, sb_norm_g=v_sb_norm_g, sb_w_ada=v_sb_w_ada,
              sb_b_ada=v_sb_b_ada, sb_w_in=v_sb_w_in, sb_w_out=v_sb_w_out, final_norm_g=v_final_norm_g)
    order = ["fox_norm_g", "fox_w_ada", "fox_b_ada", "fox_w_in", "fox_b_f", "fox_w_out", "sb_norm_g", "sb_w_ada",
             "sb_b_ada", "sb_w_in", "sb_w_out", "final_norm_g"]
    grads, deltas, new_m, new_v = {}, {}, {}, {}
    for n in big_names:
        shp = weights[n].shape
        g2 = grad_big[n]
        d, m2, v2 = _adamw(weights[n][0], g2, ms[n][0], vs[n][0], n + "_adamw")
        grads[n], deltas[n], new_m[n], new_v[n] = g2.reshape(shp), d.reshape(shp), m2.reshape(shp), v2.reshape(shp)
    small_names = [n for n in order if n not in big_names]
    sizes = [small_grads[n].shape[1] for n in small_names]
    total = sum(sizes)
    padn = (-total) % LANES

    def pack(d):
        return jnp.concatenate([d[n].reshape(1, -1) for n in small_names] + [jnp.ones((1, padn), F32)], axis=1)

    sd, sm, sv_ = _adamw(pack(weights), pack(small_grads), pack(ms), pack(vs), "small_adamw")
    o = 0
    for n, sz in zip(small_names, sizes):
        shp = weights[n].shape
        grads[n] = small_grads[n].reshape(shp)
        deltas[n], new_m[n], new_v[n] = (t[:, o:o + sz].reshape(shp) for t in (sd, sm, sv_))
        o += sz
    return (r_loss[0, 0], grad_x, *[grads[n] for n in order], *[deltas[n] for n in order],
            *[new_m[n] for n in order], *[new_v[n] for n in order])
```

```python
import functools

import jax
import jax.numpy as jnp
from jax import lax
from jax.experimental import pallas as pl
from jax.experimental.pallas import tpu as pltpu

F32 = jnp.float32
BF16 = jnp.bfloat16
HEAD_DIM = 64
LANES = 128
NORM_EPS = 1e-6
ADAM_LR = 0.001
ADAM_B1 = 0.9
ADAM_B2 = 0.999
ADAM_EPS = 1e-08
ADAM_WD = 0.01
ADAM_STEP = 10
VMEM_LIMIT = 56 * 1024 * 1024
ATT_BLOCK = 256
MESH = pl.DeviceIdType.MESH
HBM = pl.BlockSpec(memory_space=pltpu.HBM)
NT = (((1,), (1,)), ((), ()))
TN = (((0,), (0,)), ((), ()))


def _call(body, **kw):
    return pl.pallas_call(body, **kw)


def _params(**kw):
    return pltpu.CompilerParams(vmem_limit_bytes=VMEM_LIMIT, **kw)


def _tile(dim, pref, mult=128):
    if dim <= pref:
        return dim
    t = (pref // mult) * mult
    while t >= mult:
        if dim % t == 0:
            return t
        t -= mult
    return dim


def _sigmoid(x):
    return 1.0 / (1.0 + jnp.exp(-x))


def _split3(x):
    hi = x.astype(BF16)
    r = x - hi.astype(F32)
    mid = r.astype(BF16)
    lo = (r - mid.astype(F32)).astype(BF16)
    return hi, mid, lo


def _mm(a, b, mode, out_dtype, name, tm=512, tn=512, tk=512):
    if mode == "nn":
        (M, K), (_, N) = a.shape, b.shape
    elif mode == "nt":
        (M, K), (N, _) = a.shape, b.shape
    else:
        (K, M), (_, N) = a.shape, b.shape
    tm, tn, tk = _tile(M, tm), _tile(N, tn), _tile(K, tk)
    nk = K // tk
    dims = {"nn": (((1,), (0,)), ((), ())), "nt": NT, "tn": TN}[mode]

    def body(a_ref, b_ref, o_ref, acc_ref):
        k = pl.program_id(2)

        @pl.when(k == 0)
        def _():
            acc_ref[...] = jnp.zeros_like(acc_ref)

        acc_ref[...] += lax.dot_general(a_ref[...], b_ref[...], dims, preferred_element_type=F32)

        @pl.when(k == nk - 1)
        def _():
            o_ref[...] = acc_ref[...].astype(out_dtype)

    a_spec = (pl.BlockSpec((tk, tm), lambda i, j, k: (k, i)) if mode == "tn"
              else pl.BlockSpec((tm, tk), lambda i, j, k: (i, k)))
    b_spec = (pl.BlockSpec((tn, tk), lambda i, j, k: (j, k)) if mode == "nt"
              else pl.BlockSpec((tk, tn), lambda i, j, k: (k, j)))
    return _call(
        body, name=name, grid=(M // tm, N // tn, nk),
        in_specs=[a_spec, b_spec], out_specs=pl.BlockSpec((tm, tn), lambda i, j, k: (i, j)),
        out_shape=jax.ShapeDtypeStruct((M, N), out_dtype),
        scratch_shapes=[pltpu.VMEM((tm, tn), F32)], compiler_params=_params(),
    )(a, b)


def _mod_fwd(c8, w_ada, b_ada, name):
    D, N = w_ada.shape
    tn = _tile(N, 512)

    def body(c_ref, w_ref, b_ref, o_ref):
        c = c_ref[...]
        sc = (c * _sigmoid(c)).astype(BF16)
        o_ref[...] = jnp.dot(sc, w_ref[...], preferred_element_type=F32) + b_ref[...]

    return _call(
        body, name=name, grid=(N // tn,),
        in_specs=[pl.BlockSpec((8, D), lambda j: (0, 0)), pl.BlockSpec((D, tn), lambda j: (0, j)),
                  pl.BlockSpec((1, tn), lambda j: (0, j))],
        out_specs=pl.BlockSpec((8, tn), lambda j: (0, j)),
        out_shape=jax.ShapeDtypeStruct((8, N), F32), compiler_params=_params(),
    )(c8, w_ada, b_ada)


def _mod_bwd(cT, dmod8, nb, name):
    D = cT.shape[0]
    N = dmod8.shape[1]
    tn = _tile(N, 512)

    def body(c_ref, d_ref, w_ref, b_ref):
        c = c_ref[...]
        sc = c * _sigmoid(c)
        d = d_ref[...]
        acc = sc[:, 0:1] * d[0:1, :]
        bsum = d[0:1, :]
        for b in range(1, nb):
            acc = acc + sc[:, b:b + 1] * d[b:b + 1, :]
            bsum = bsum + d[b:b + 1, :]
        w_ref[...] = acc
        b_ref[...] = bsum

    return _call(
        body, name=name, grid=(N // tn,),
        in_specs=[pl.BlockSpec((D, 8), lambda j: (0, 0)), pl.BlockSpec((8, tn), lambda j: (0, j))],
        out_specs=[pl.BlockSpec((D, tn), lambda j: (0, j)), pl.BlockSpec((1, tn), lambda j: (0, j))],
        out_shape=[jax.ShapeDtypeStruct((D, N), F32), jax.ShapeDtypeStruct((1, N), F32)],
        compiler_params=_params(),
    )(cT, dmod8)


def _ln_proj(x, shift, scale, g, w, S, name):
    T, D = x.shape
    N = w.shape[1]
    tm = _tile(S, 512)
    tn = _tile(N, 1024)
    per_b = S // tm

    def body(x_ref, sh_ref, sc_ref, g_ref, w_ref, p_ref, h_ref):
        @pl.when(pl.program_id(1) == 0)
        def _():
            xv = x_ref[...]
            r = lax.rsqrt(jnp.mean(xv * xv, axis=-1, keepdims=True) + NORM_EPS)
            h = (xv * r) * g_ref[...] * (1.0 + sc_ref[0]) + sh_ref[0]
            h_ref[...] = h.astype(BF16)

        p_ref[...] = jnp.dot(h_ref[...], w_ref[...], preferred_element_type=F32).astype(BF16)

    return _call(
        body, name=name, grid=(T // tm, N // tn),
        in_specs=[pl.BlockSpec((tm, D), lambda i, j: (i, 0)),
                  pl.BlockSpec((1, 1, D), lambda i, j: (i // per_b, 0, 0)),
                  pl.BlockSpec((1, 1, D), lambda i, j: (i // per_b, 0, 0)),
                  pl.BlockSpec((1, D), lambda i, j: (0, 0)),
                  pl.BlockSpec((D, tn), lambda i, j: (0, j))],
        out_specs=[pl.BlockSpec((tm, tn), lambda i, j: (i, j)), pl.BlockSpec((tm, D), lambda i, j: (i, 0))],
        out_shape=[jax.ShapeDtypeStruct((T, N), BF16), jax.ShapeDtypeStruct((T, D), BF16)],
        compiler_params=_params(),
    )(x, shift, scale, g, w)


def _ln_bwd(dh, x, dxo, scale, g, S, name):
    T, D = x.shape
    B = T // S
    tm = _tile(S, 512)
    per_b = S // tm

    def body(dh_ref, x_ref, dxo_ref, sc_ref, g_ref, dx_ref, dsh_ref, dsc_ref, dg_ref):
        i = pl.program_id(0)
        xv = x_ref[...]
        dh_v = dh_ref[...]
        r = lax.rsqrt(jnp.mean(xv * xv, axis=-1, keepdims=True) + NORM_EPS)
        xn = xv * r
        gv = g_ref[...]
        one_sc = 1.0 + sc_ref[0]
        dhxn = dh_v * xn

        @pl.when(i % per_b == 0)
        def _():
            dsh_ref[...] = jnp.zeros_like(dsh_ref)
            dsc_ref[...] = jnp.zeros_like(dsc_ref)

        @pl.when(i == 0)
        def _():
            dg_ref[...] = jnp.zeros_like(dg_ref)

        dsh_ref[0] += jnp.sum(dh_v, axis=0, keepdims=True)
        dsc_ref[0] += jnp.sum(dhxn, axis=0, keepdims=True) * gv
        dg_ref[...] += jnp.sum(dhxn, axis=0, keepdims=True) * one_sc
        dxn = dh_v * (gv * one_sc)
        dx_ref[...] = r * (dxn - xn * jnp.mean(dxn * xn, axis=-1, keepdims=True)) + dxo_ref[...]

    row = pl.BlockSpec((tm, D), lambda i: (i, 0))
    per = pl.BlockSpec((1, 1, D), lambda i: (i // per_b, 0, 0))
    vec = pl.BlockSpec((1, D), lambda i: (0, 0))
    return _call(
        body, name=name, grid=(T // tm,),
        in_specs=[row, row, row, per, vec], out_specs=[row, per, per, vec],
        out_shape=[jax.ShapeDtypeStruct((T, D), F32), jax.ShapeDtypeStruct((B, 1, D), F32),
                   jax.ShapeDtypeStruct((B, 1, D), F32), jax.ShapeDtypeStruct((1, D), F32)],
        compiler_params=_params(),
    )(dh, x, dxo, scale, g)


def _gate_out(o, proj, w_out, x, gate, S, name):
    T, DI = o.shape
    D = w_out.shape[1]
    tm = _tile(S, 256)
    per_b = S // tm

    def body(o_ref, z_ref, w_ref, x_ref, g_ref, xo_ref, y_ref, u_ref):
        z = z_ref[...].astype(F32)
        u = (o_ref[...] * (z * _sigmoid(z))).astype(BF16)
        u_ref[...] = u
        y = jnp.dot(u, w_ref[...], preferred_element_type=F32)
        y_ref[...] = y
        xo_ref[...] = x_ref[...] + g_ref[0] * y

    wide = pl.BlockSpec((tm, DI), lambda i: (i, 0))
    row = pl.BlockSpec((tm, D), lambda i: (i, 0))
    return _call(
        body, name=name, grid=(T // tm,),
        in_specs=[wide, pl.BlockSpec((tm, DI), lambda i: (i, 3)), pl.BlockSpec((DI, D), lambda i: (0, 0)), row,
                  pl.BlockSpec((1, 1, D), lambda i: (i // per_b, 0, 0))],
        out_specs=[row, row, wide],
        out_shape=[jax.ShapeDtypeStruct((T, D), F32), jax.ShapeDtypeStruct((T, D), F32),
                   jax.ShapeDtypeStruct((T, DI), BF16)],
        compiler_params=_params(),
    )(o, proj, w_out, x, gate)


def _out_bwd(dxo, y, gate, w_out, o, proj, S, name):
    T, D = dxo.shape
    DI = o.shape[1]
    B = T // S
    tm = _tile(S, 256)
    per_b = S // tm

    def body(dxo_ref, y_ref, g_ref, w_ref, o_ref, z_ref, dy_ref, do_ref, dz_ref, dg_ref):
        dxo_v = dxo_ref[...]
        dy = (dxo_v * g_ref[0]).astype(BF16)
        dy_ref[...] = dy
        du = lax.dot_general(dy, w_ref[...], NT, preferred_element_type=F32)
        z = z_ref[...].astype(F32)
        sg = _sigmoid(z)
        do_ref[...] = (du * (z * sg)).astype(BF16)
        dz_ref[...] = (du * o_ref[...] * (sg * (1.0 + z * (1.0 - sg)))).astype(BF16)

        @pl.when(pl.program_id(0) % per_b == 0)
        def _():
            dg_ref[...] = jnp.zeros_like(dg_ref)

        dg_ref[0] += jnp.sum(dxo_v * y_ref[...], axis=0, keepdims=True)

    wide = pl.BlockSpec((tm, DI), lambda i: (i, 0))
    row = pl.BlockSpec((tm, D), lambda i: (i, 0))
    per = pl.BlockSpec((1, 1, D), lambda i: (i // per_b, 0, 0))
    return _call(
        body, name=name, grid=(T // tm,),
        in_specs=[row, row, per, pl.BlockSpec((DI, D), lambda i: (0, 0)), wide,
                  pl.BlockSpec((tm, DI), lambda i: (i, 3))],
        out_specs=[row, wide, wide, per],
        out_shape=[jax.ShapeDtypeStruct((T, D), BF16), jax.ShapeDtypeStruct((T, DI), BF16),
                   jax.ShapeDtypeStruct((T, DI), BF16), jax.ShapeDtypeStruct((B, 1, D), F32)],
        compiler_params=_params(),
    )(dxo, y, gate, w_out, o, proj)


def _final_loss(x, tgt, g, S, name):
    T, D = x.shape
    tm = _tile(S, 512)

    def body(x_ref, t_ref, g_ref, dx_ref, dg_ref, l_ref):
        @pl.when(pl.program_id(0) == 0)
        def _():
            dg_ref[...] = jnp.zeros_like(dg_ref)
            l_ref[...] = jnp.zeros_like(l_ref)

        xv = x_ref[...]
        gv = g_ref[...]
        r = lax.rsqrt(jnp.mean(xv * xv, axis=-1, keepdims=True) + NORM_EPS)
        xn = xv * r
        e = xn * gv - t_ref[...]
        part = jnp.sum(jnp.sum(e * e, axis=0, keepdims=True), axis=1, keepdims=True)
        l_ref[...] += (0.5 / D) * part
        dy = e * (1.0 / D)
        dg_ref[...] += jnp.sum(dy * xn, axis=0, keepdims=True)
        dxn = dy * gv
        dx_ref[...] = r * (dxn - xn * jnp.mean(dxn * xn, axis=-1, keepdims=True))

    row = pl.BlockSpec((tm, D), lambda i: (i, 0))
    return _call(
        body, name=name, grid=(T // tm,),
        in_specs=[row, row, pl.BlockSpec((1, D), lambda i: (0, 0))],
        out_specs=[row, pl.BlockSpec((1, D), lambda i: (0, 0)), pl.BlockSpec((1, LANES), lambda i: (0, 0))],
        out_shape=[jax.ShapeDtypeStruct((T, D), F32), jax.ShapeDtypeStruct((1, D), F32),
                   jax.ShapeDtypeStruct((1, LANES), F32)],
        compiler_params=_params(),
    )(x, tgt, g)


def _cum_fwd(fl, bf, name):
    B, S, _ = fl.shape
    ch = _tile(S, 256, 8)

    def body(fl_ref, b_ref, cum_ref):
        ri = lax.broadcasted_iota(jnp.int32, (ch, ch), 0)
        ci = lax.broadcasted_iota(jnp.int32, (ch, ch), 1)
        tri = jnp.where(ri >= ci, 1.0, 0.0).astype(BF16)

        def step(i, carry):
            r0 = pl.multiple_of(i * ch, ch)
            z = fl_ref[0, pl.ds(r0, ch), :] + b_ref[...]
            lf = jnp.minimum(z, 0.0) - jnp.log(1.0 + jnp.exp(-jnp.abs(z)))
            hi, mid, lo = _split3(lf)
            cs = (jnp.dot(tri, hi, preferred_element_type=F32) + jnp.dot(tri, mid, preferred_element_type=F32)
                  + jnp.dot(tri, lo, preferred_element_type=F32)) + carry
            cum_ref[0, pl.ds(r0, ch), :] = cs
            return cs[ch - 1:ch, :]

        lax.fori_loop(0, S // ch, step, jnp.zeros((1, LANES), F32))

    blk = pl.BlockSpec((1, S, LANES), lambda b: (b, 0, 0))
    return _call(
        body, name=name, grid=(B,), in_specs=[blk, pl.BlockSpec((1, LANES), lambda b: (0, 0))], out_specs=blk,
        out_shape=jax.ShapeDtypeStruct((B, S, LANES), F32), compiler_params=_params(),
    )(fl, bf)


def _cum_bwd(dcs, fl, bf, name):
    B, S, _ = fl.shape
    ch = _tile(S, 256, 8)
    n = S // ch

    def body(d_ref, fl_ref, b_ref, o_ref, db_ref):
        ri = lax.broadcasted_iota(jnp.int32, (ch, ch), 0)
        ci = lax.broadcasted_iota(jnp.int32, (ch, ch), 1)
        tri = jnp.where(ci >= ri, 1.0, 0.0).astype(BF16)

        @pl.when(pl.program_id(0) == 0)
        def _():
            db_ref[...] = jnp.zeros_like(db_ref)

        def step(t, carry):
            tail, dbsum = carry
            r0 = pl.multiple_of((n - 1 - t) * ch, ch)
            hi, mid, lo = _split3(d_ref[0, pl.ds(r0, ch), :])
            suf = (jnp.dot(tri, hi, preferred_element_type=F32) + jnp.dot(tri, mid, preferred_element_type=F32)
                   + jnp.dot(tri, lo, preferred_element_type=F32)) + tail
            z = fl_ref[0, pl.ds(r0, ch), :] + b_ref[...]
            dfl = -suf * _sigmoid(-z)
            o_ref[0, pl.ds(r0, ch), :] = dfl
            return suf[0:1, :], dbsum + jnp.sum(dfl, axis=0, keepdims=True)

        z1 = jnp.zeros((1, LANES), F32)
        _, dbsum = lax.fori_loop(0, n, step, (z1, z1))
        db_ref[...] += dbsum

    blk = pl.BlockSpec((1, S, LANES), lambda b: (b, 0, 0))
    vec = pl.BlockSpec((1, LANES), lambda b: (0, 0))
    return _call(
        body, name=name, grid=(B,), in_specs=[blk, blk, vec], out_specs=[blk, vec],
        out_shape=[jax.ShapeDtypeStruct((B, S, LANES), F32), jax.ShapeDtypeStruct((1, LANES), F32)],
        compiler_params=_params(),
    )(dcs, fl, bf)


HEADS_PER_STEP = 4
GROUP = 2 * HEAD_DIM


def _step_width():
    return HEAD_DIM * HEADS_PER_STEP


def _cols(S, offset_blocks=0):
    return pl.BlockSpec((S, _step_width()), lambda b, h: (b, offset_blocks + h))


def _row_spec(nq, tq):
    return pl.BlockSpec((1, HEADS_PER_STEP, nq, 1, tq), lambda b, h: (b, h, 0, 0, 0))


def _lanes(g):
    return slice(GROUP * (g // 2), GROUP * (g // 2) + GROUP)


def _hi_lo(x):
    hi = x.astype(BF16)
    return hi, (x - hi.astype(F32)).astype(BF16)


def _dot(a, b, dims=None):
    if dims is None:
        return jnp.dot(a, b, preferred_element_type=F32)
    return lax.dot_general(a, b, dims, preferred_element_type=F32)


def _causal_blocks(nq, prep, init, stages, finish, combine=None, descending=False):
    heads = range(HEADS_PER_STEP)

    def qloop(qi, _):
        ctx = [prep(g, qi) for g in heads]

        def step(kj, carry, masked):
            st = list(carry)
            for n, stage in enumerate(stages):
                if combine is not None and n == len(stages) - 1:
                    combine(kj, ctx, st)
                st = [stage(g, ctx[g], kj, masked, st[g]) for g in heads]
            return tuple(st)

        carry = tuple(init() for _ in heads)
        if descending:
            carry = step(qi, carry, True)
            carry = lax.fori_loop(0, qi, lambda t, cr: step(qi - 1 - t, cr, False), carry)
        else:
            carry = lax.fori_loop(0, qi, lambda kj, cr: step(kj, cr, False), carry)
            carry = step(qi, carry, True)
        finish(qi, ctx, carry)
        return 0

    lax.fori_loop(0, nq, qloop, 0)


class _Block:
    def __init__(self, tq):
        self.tq = tq
        self.lane = lax.broadcasted_iota(jnp.int32, (tq, GROUP), 1)
        self.low = self.lane < HEAD_DIM
        self.ri = lax.broadcasted_iota(jnp.int32, (tq, tq), 0)
        self.ci = lax.broadcasted_iota(jnp.int32, (tq, tq), 1)

    def rows(self, i):
        return pl.ds(pl.multiple_of(i * self.tq, self.tq), self.tq)

    def own(self, g, x):
        return jnp.where(self.low if g % 2 == 0 else jnp.logical_not(self.low), x, jnp.zeros_like(x))

    def pair(self, a, b):
        return jnp.where(self.low, a, b)

    def stat(self, g, x):
        return jnp.sum(jnp.where(self.lane == HEAD_DIM * (g % 2), x, 0.0), axis=1, keepdims=True)


def _fox_fwd(proj, cumcol, cumrow, name):
    T, DI = proj.shape[0], proj.shape[1] // 4
    B, H, nq, _, tq = cumrow.shape
    S = nq * tq
    nb = DI // _step_width()

    def body(q_ref, k_ref, v_ref, cc_ref, cr_ref, o_ref, st_ref, acc_scr):
        h0 = pl.program_id(1) * HEADS_PER_STEP
        blk = _Block(tq)

        def prep(g, qi):
            acc_scr[g] = jnp.zeros((tq, GROUP), F32)
            q = blk.own(g, q_ref[blk.rows(qi), _lanes(g)]) * 0.125
            ccol = jnp.sum(jnp.where(blk.lane == h0 + g, cc_ref[0, blk.rows(qi), :], 0.0), axis=1, keepdims=True)
            return q, ccol

        def init():
            return jnp.full((tq, 1), -jnp.inf, F32), jnp.zeros((tq, 1), F32)

        def scores(g, ctx, kj, masked, st):
            return st + (_dot(ctx[0], k_ref[blk.rows(kj), _lanes(g)], NT),)

        def softmax(g, ctx, kj, masked, st):
            m, l, s = st
            s = s + ctx[1] - cr_ref[0, g, kj]
            if masked:
                s = jnp.where(blk.ci <= blk.ri, s, -jnp.inf)
            m_new = jnp.maximum(m, jnp.max(s, axis=1, keepdims=True))
            alpha = jnp.exp(m - m_new)
            p = jnp.exp(s - m_new)
            return (m_new, alpha * l + jnp.sum(p, axis=1, keepdims=True), alpha) + _hi_lo(p)

        def values(g, ctx, kj, masked, st):
            m, l, alpha, hi, lo = st
            v = v_ref[blk.rows(kj), _lanes(g)]
            acc_scr[g] = alpha * acc_scr[g] + (_dot(hi, v) + _dot(lo, v))
            return m, l

        def finish(qi, ctx, carry):
            for g in range(0, HEADS_PER_STEP, 2):
                (m0, l0), (m1, l1) = carry[g], carry[g + 1]
                o_ref[blk.rows(qi), _lanes(g)] = blk.pair(acc_scr[g] / l0, acc_scr[g + 1] / l1)
                st_ref[blk.rows(qi), _lanes(g)] = blk.pair(m0 + jnp.log(l0), m1 + jnp.log(l1))

        _causal_blocks(nq, prep, init, [scores, softmax, values], finish)

    out = jax.ShapeDtypeStruct((T, DI), F32)
    return _call(
        body, name=name, grid=(B, H // HEADS_PER_STEP),
        in_specs=[_cols(S), _cols(S, nb), _cols(S, 2 * nb), pl.BlockSpec((1, S, LANES), lambda b, h: (b, 0, 0)),
                  _row_spec(nq, tq)],
        out_specs=[_cols(S), _cols(S)], out_shape=[out, out],
        scratch_shapes=[pltpu.VMEM((HEADS_PER_STEP, tq, GROUP), F32)], compiler_params=_params(),
    )(proj, proj, proj, cumcol, cumrow)


def _fox_bwd(proj, do, o, stat, cumcol, cumrow, name):
    T, DI = do.shape
    B, H, nq, _, tq = cumrow.shape
    S = nq * tq
    nb = DI // _step_width()

    def body(q_ref, k_ref, v_ref, do_ref, o_ref, st_ref, cc_ref, cr_ref, dqkv_ref, dcs_ref, dk_acc, dv_acc, dq_scr):
        h0 = pl.program_id(1) * HEADS_PER_STEP
        blk = _Block(tq)
        dk_acc[...] = jnp.zeros_like(dk_acc)
        dv_acc[...] = jnp.zeros_like(dv_acc)
        dcs_ref[...] = jnp.zeros_like(dcs_ref)

        def prep(g, qi):
            dq_scr[g] = jnp.zeros((tq, GROUP), F32)
            q = blk.own(g, q_ref[blk.rows(qi), _lanes(g)]) * 0.125
            dout = blk.own(g, do_ref[blk.rows(qi), _lanes(g)])
            delta = jnp.sum(o_ref[blk.rows(qi), _lanes(g)] * dout.astype(F32), axis=1, keepdims=True)
            lse = blk.stat(g, st_ref[blk.rows(qi), _lanes(g)])
            ccol = jnp.sum(jnp.where(blk.lane == h0 + g, cc_ref[0, blk.rows(qi), :], 0.0), axis=1, keepdims=True)
            return q, dout, lse, delta, ccol

        def init():
            return ()

        def scores(g, ctx, kj, masked, st):
            return (_dot(ctx[0], k_ref[blk.rows(kj), _lanes(g)], NT), _dot(ctx[1], v_ref[blk.rows(kj), _lanes(g)], NT))

        def softmax_bwd(g, ctx, kj, masked, st):
            s, dp = st
            _, _, lse, delta, ccol = ctx
            s = s + ccol - cr_ref[0, g, kj]
            if masked:
                s = jnp.where(blk.ci <= blk.ri, s, -jnp.inf)
            p = jnp.exp(s - lse)
            ds = p * (dp - delta)
            return p.astype(BF16), ds.astype(BF16), jnp.sum(ds, axis=0, keepdims=True)

        def combine(kj, ctx, st):
            for g in range(0, HEADS_PER_STEP, 2):
                dv_acc[blk.rows(kj), _lanes(g)] += _dot(st[g][0], ctx[g][1], TN) + _dot(st[g + 1][0], ctx[g + 1][1], TN)
                dk_acc[blk.rows(kj), _lanes(g)] += _dot(st[g][1], ctx[g][0], TN) + _dot(st[g + 1][1], ctx[g + 1][0], TN)
            for g in range(HEADS_PER_STEP):
                dcs_ref[0, g, kj] += st[g][2]

        def queries(g, ctx, kj, masked, st):
            dq_scr[g] += _dot(st[1], blk.own(g, k_ref[blk.rows(kj), _lanes(g)]))
            return ()

        def finish(qi, ctx, carry):
            for g in range(0, HEADS_PER_STEP, 2):
                dqkv_ref[0, blk.rows(qi), _lanes(g)] = ((dq_scr[g] + dq_scr[g + 1]) * 0.125).astype(BF16)

        _causal_blocks(nq, prep, init, [scores, softmax_bwd, queries], finish, combine=combine)
        dqkv_ref[1] = dk_acc[...].astype(BF16)
        dqkv_ref[2] = dv_acc[...].astype(BF16)

    W = _step_width()
    return _call(
        body, name=name, grid=(B, H // HEADS_PER_STEP),
        in_specs=[_cols(S), _cols(S, nb), _cols(S, 2 * nb), _cols(S), _cols(S), _cols(S),
                  pl.BlockSpec((1, S, LANES), lambda b, h: (b, 0, 0)), _row_spec(nq, tq)],
        out_specs=[pl.BlockSpec((3, S, W), lambda b, h: (0, b, h)), _row_spec(nq, tq)],
        out_shape=[jax.ShapeDtypeStruct((3, T, DI), BF16), jax.ShapeDtypeStruct((B, H, nq, 1, tq), F32)],
        scratch_shapes=[pltpu.VMEM((S, W), F32), pltpu.VMEM((S, W), F32), pltpu.VMEM((HEADS_PER_STEP, tq, GROUP), F32)],
        compiler_params=_params(),
    )(proj, proj, proj, do, o, stat, cumcol, cumrow)


def _softplus_parts(z):
    e = jnp.exp(-jnp.abs(z))
    return jnp.maximum(z, 0.0) + jnp.log(1.0 + e), e


def _sb_fwd(proj, B, tq, name):
    T, DI = proj.shape[0], proj.shape[1] // 4
    S = T // B
    H = DI // HEAD_DIM
    nq = S // tq
    nb = DI // _step_width()

    def body(q_ref, k_ref, v_ref, o_ref, st_ref, acc_scr, c_scr):
        blk = _Block(tq)
        strict = blk.ci < blk.ri
        above = jnp.where(blk.ri > blk.ci, 1.0, 0.0).astype(BF16)

        def prep(g, qi):
            acc_scr[g] = jnp.zeros((tq, GROUP), F32)
            c_scr[g] = jnp.zeros((tq, 1), F32)
            return blk.own(g, q_ref[blk.rows(qi), _lanes(g)]) * 0.125

        def init():
            return ()

        def scores(g, q, kj, masked, st):
            return (_dot(q, k_ref[blk.rows(kj), _lanes(g)], NT),)

        def logs(g, q, kj, masked, st):
            (z,) = st
            sp, _ = _softplus_parts(z)
            lk = -sp
            if masked:
                lk = jnp.where(strict, lk, 0.0)
            c = c_scr[g]
            c_scr[g] = c + jnp.sum(lk, axis=1, keepdims=True)
            return ((z - sp) + c,) + _hi_lo(lk)

        def suffix(g, q, kj, masked, st):
            lbc, hi, lo = st
            return lbc, _dot(hi, above) + _dot(lo, above)

        def weights(g, q, kj, masked, st):
            lbc, after = st
            a = jnp.exp(lbc + after)
            if masked:
                a = jnp.where(strict, a, 0.0)
            return (a.astype(BF16),)

        def values(g, q, kj, masked, st):
            acc_scr[g] += _dot(st[0], v_ref[blk.rows(kj), _lanes(g)])
            return ()

        def finish(qi, ctx, carry):
            for g in range(0, HEADS_PER_STEP, 2):
                o_ref[blk.rows(qi), _lanes(g)] = blk.pair(acc_scr[g], acc_scr[g + 1])
                st_ref[blk.rows(qi), _lanes(g)] = blk.pair(c_scr[g], c_scr[g + 1])

        _causal_blocks(nq, prep, init, [scores, logs, suffix, weights, values], finish, descending=True)

    out = jax.ShapeDtypeStruct((T, DI), F32)
    return _call(
        body, name=name, grid=(B, H // HEADS_PER_STEP), in_specs=[_cols(S), _cols(S, nb), _cols(S, 2 * nb)],
        out_specs=[_cols(S), _cols(S)], out_shape=[out, out],
        scratch_shapes=[pltpu.VMEM((HEADS_PER_STEP, tq, GROUP), F32), pltpu.VMEM((HEADS_PER_STEP, tq, 1), F32)],
        compiler_params=_params(),
    )(proj, proj, proj)


def _sb_bwd(proj, do, stat, B, tq, name):
    T, DI = do.shape
    S = T // B
    H = DI // HEAD_DIM
    nq = S // tq
    nb = DI // _step_width()

    def body(q_ref, k_ref, v_ref, do_ref, st_ref, dqkv_ref, dk_acc, dv_acc, dq_scr):
        blk = _Block(tq)
        strict = blk.ci < blk.ri
        upto = jnp.where(blk.ri <= blk.ci, 1.0, 0.0).astype(BF16)
        before = jnp.where(blk.ri < blk.ci, 1.0, 0.0).astype(BF16)
        dk_acc[...] = jnp.zeros_like(dk_acc)
        dv_acc[...] = jnp.zeros_like(dv_acc)

        def prep(g, qi):
            dq_scr[g] = jnp.zeros((tq, GROUP), F32)
            return (blk.own(g, q_ref[blk.rows(qi), _lanes(g)]) * 0.125, blk.own(g, do_ref[blk.rows(qi), _lanes(g)]),
                    blk.stat(g, st_ref[blk.rows(qi), _lanes(g)]))

        def init():
            return jnp.zeros((tq, 1), F32), jnp.zeros((tq, 1), F32)

        def scores(g, ctx, kj, masked, st):
            return st + (_dot(ctx[0], k_ref[blk.rows(kj), _lanes(g)], NT),
                         _dot(ctx[1], v_ref[blk.rows(kj), _lanes(g)], NT))

        def logs(g, ctx, kj, masked, st):
            cpre, pg, z, da = st
            sp, e = _softplus_parts(z)
            inv = 1.0 / (1.0 + e)
            sig = jnp.where(z >= 0.0, inv, e * inv)
            lk = -sp
            if masked:
                lk = jnp.where(strict, lk, 0.0)
            return (cpre + jnp.sum(lk, axis=1, keepdims=True), pg, da, (z - sp) + (ctx[2] - cpre), sig) + _hi_lo(lk)

        def prefix(g, ctx, kj, masked, st):
            cpre, pg, da, lbt, sig, hi, lo = st
            return cpre, pg, da, lbt, sig, _dot(hi, upto) + _dot(lo, upto)

        def weights(g, ctx, kj, masked, st):
            cpre, pg, da, lbt, sig, pre = st
            a = jnp.exp(lbt - pre)
            if masked:
                a = jnp.where(strict, a, 0.0)
            gr = da * a
            return cpre, pg, sig, a.astype(BF16), gr, gr.astype(BF16)

        def grad_prefix(g, ctx, kj, masked, st):
            cpre, pg, sig, ab, gr, gb = st
            return cpre, pg, sig, ab, gr, _dot(gb, before)

        def dlogits(g, ctx, kj, masked, st):
            cpre, pg, sig, ab, gr, pfx = st
            dz = gr * (1.0 - sig) - (pfx + pg) * sig
            if masked:
                dz = jnp.where(strict, dz, 0.0)
            return cpre, pg + jnp.sum(gr, axis=1, keepdims=True), ab, dz.astype(BF16)

        def combine(kj, ctx, st):
            for g in range(0, HEADS_PER_STEP, 2):
                dv_acc[blk.rows(kj), _lanes(g)] += _dot(st[g][2], ctx[g][1], TN) + _dot(st[g + 1][2], ctx[g + 1][1], TN)
                dk_acc[blk.rows(kj), _lanes(g)] += _dot(st[g][3], ctx[g][0], TN) + _dot(st[g + 1][3], ctx[g + 1][0], TN)

        def queries(g, ctx, kj, masked, st):
            cpre, pg, _, dzb = st
            dq_scr[g] += _dot(dzb, blk.own(g, k_ref[blk.rows(kj), _lanes(g)]))
            return cpre, pg

        def finish(qi, ctx, carry):
            for g in range(0, HEADS_PER_STEP, 2):
                dqkv_ref[0, blk.rows(qi), _lanes(g)] = ((dq_scr[g] + dq_scr[g + 1]) * 0.125).astype(BF16)

        _causal_blocks(nq, prep, init, [scores, logs, prefix, weights, grad_prefix, dlogits, queries], finish,
                       combine=combine)
        dqkv_ref[1] = dk_acc[...].astype(BF16)
        dqkv_ref[2] = dv_acc[...].astype(BF16)

    W = _step_width()
    return _call(
        body, name=name, grid=(B, H // HEADS_PER_STEP),
        in_specs=[_cols(S), _cols(S, nb), _cols(S, 2 * nb), _cols(S), _cols(S)],
        out_specs=pl.BlockSpec((3, S, W), lambda b, h: (0, b, h)),
        out_shape=jax.ShapeDtypeStruct((3, T, DI), BF16),
        scratch_shapes=[pltpu.VMEM((S, W), F32), pltpu.VMEM((S, W), F32), pltpu.VMEM((HEADS_PER_STEP, tq, GROUP), F32)],
        compiler_params=_params(),
    )(proj, proj, proj, do, stat)


def _row_tile(R, C, n_arrays):
    budget = 24 * 1024 * 1024 // (2 * n_arrays * 4 * max(C, LANES))
    return _tile(R, max(8, budget), 8)


def _ew_sum(parts, name, also_bf16=False):
    R, C = parts[0].shape
    tr = _row_tile(R, C, len(parts) + 2)
    n = len(parts)

    def body(*refs):
        acc = refs[0][...].astype(F32) + refs[1][...].astype(F32)
        for r in refs[2:n]:
            acc = acc + r[...].astype(F32)
        refs[n][...] = acc
        if also_bf16:
            refs[n + 1][...] = acc.astype(BF16)

    blk = pl.BlockSpec((tr, C), lambda i: (i, 0))
    out_shape = [jax.ShapeDtypeStruct((R, C), F32)] + ([jax.ShapeDtypeStruct((R, C), BF16)] if also_bf16 else [])
    return _call(
        body, name=name, grid=(R // tr,), in_specs=[blk] * n, out_specs=[blk] * len(out_shape),
        out_shape=out_shape, compiler_params=_params(),
    )(*parts)


def _adamw(w, g, m, v, name):
    R, C = w.shape
    tr = _row_tile(R, C, 7)
    c1 = 1.0 / (1.0 - ADAM_B1 ** ADAM_STEP)
    c2 = 1.0 / (1.0 - ADAM_B2 ** ADAM_STEP)

    def body(w_ref, g_ref, m_ref, v_ref, d_ref, m2_ref, v2_ref):
        gv = g_ref[...]
        m2 = ADAM_B1 * m_ref[...] + (1.0 - ADAM_B1) * gv
        v2 = ADAM_B2 * v_ref[...] + (1.0 - ADAM_B2) * (gv * gv)
        m2_ref[...] = m2
        v2_ref[...] = v2
        d_ref[...] = -ADAM_LR * ((m2 * c1) / (jnp.sqrt(v2 * c2) + ADAM_EPS) + ADAM_WD * w_ref[...])

    blk = pl.BlockSpec((tr, C), lambda i: (i, 0))
    out = jax.ShapeDtypeStruct((R, C), F32)
    return _call(
        body, name=name, grid=(R // tr,), in_specs=[blk] * 4, out_specs=[blk] * 3, out_shape=[out] * 3,
        compiler_params=_params(),
    )(w, g, m, v)


def _me():
    return lax.axis_index("x"), lax.axis_index("y"), lax.axis_index("c")


def _chip_of(x, y):
    return 2 * x + y


def _other_chips(x, y):
    return [(x, 1 - y), (1 - x, y), (1 - x, 1 - y)]


def _gather_weights(halves, smalls):
    nh, ns = len(halves), len(smalls)

    def body(*refs):
        ins_h, ins_s = refs[:nh], refs[nh:nh + ns]
        outs_h, outs_s = refs[nh + ns:2 * nh + ns], refs[2 * nh + ns:2 * (nh + ns)]
        send1, recv1, send2, recv2, send3, recv3 = refs[2 * (nh + ns):]
        x, y, c = _me()
        mine = _chip_of(x, y)
        chips = _other_chips(x, y)
        sib = (x, y, 1 - c)

        def landed(i, k, half):
            return outs_h[i].at[_chip_of(*chips[k]), half]

        def first(i, k):
            return pltpu.make_async_remote_copy(
                src_ref=ins_h[i].at[c], dst_ref=outs_h[i].at[mine, c], send_sem=send1.at[i, k], recv_sem=recv1.at[i, k],
                device_id=(*chips[k], c), device_id_type=MESH)

        def passed(i, k):
            return pltpu.make_async_remote_copy(
                src_ref=landed(i, k, c), dst_ref=landed(i, k, c), send_sem=send2.at[i, k], recv_sem=recv2.at[i, k],
                device_id=sib, device_id_type=MESH)

        def small(i, k):
            return pltpu.make_async_remote_copy(
                src_ref=ins_s[i], dst_ref=outs_s[i].at[mine], send_sem=send3.at[i, k], recv_sem=recv3.at[i, k],
                device_id=(*chips[k], c), device_id_type=MESH)

        for i in range(nh):
            for k in range(3):
                first(i, k).start()
        for i in range(ns):
            for k in range(3):
                small(i, k).start()
        for i in range(nh):
            for k in range(3):
                pltpu.make_async_remote_copy(
                    src_ref=ins_h[i].at[c], dst_ref=landed(i, k, c), send_sem=send1.at[i, k], recv_sem=recv1.at[i, k],
                    device_id=(*chips[k], c), device_id_type=MESH).wait_recv()
                passed(i, k).start()
        for i in range(nh):
            for k in range(3):
                pltpu.make_async_remote_copy(
                    src_ref=landed(i, k, c), dst_ref=landed(i, k, 1 - c), send_sem=send2.at[i, k],
                    recv_sem=recv2.at[i, k], device_id=sib, device_id_type=MESH).wait_recv()
        for i in range(ns):
            for k in range(3):
                pltpu.make_async_remote_copy(
                    src_ref=ins_s[i], dst_ref=outs_s[i].at[_chip_of(*chips[k])], send_sem=send3.at[i, k],
                    recv_sem=recv3.at[i, k], device_id=(*chips[k], c), device_id_type=MESH).wait_recv()
        for i in range(nh):
            for k in range(3):
                first(i, k).wait_send()
                passed(i, k).wait_send()
        for i in range(ns):
            for k in range(3):
                small(i, k).wait_send()

    out_shape = ([jax.ShapeDtypeStruct((4,) + a.shape, a.dtype) for a in halves]
                 + [jax.ShapeDtypeStruct((4,) + a.shape, a.dtype) for a in smalls])
    n = nh + ns
    res = _call(
        body, name="gather_weights", in_specs=[HBM] * n, out_specs=[HBM] * n, out_shape=out_shape,
        scratch_shapes=[pltpu.SemaphoreType.DMA((nh, 3)), pltpu.SemaphoreType.DMA((nh, 3)),
                        pltpu.SemaphoreType.DMA((nh, 3)), pltpu.SemaphoreType.DMA((nh, 3)),
                        pltpu.SemaphoreType.DMA((max(ns, 1), 3)), pltpu.SemaphoreType.DMA((max(ns, 1), 3))],
        compiler_params=_params(),
    )(*halves, *smalls)
    return res[:nh], res[nh:]


def _pair_exchange(grads):
    n = len(grads)

    def body(*refs):
        ins, got = refs[:n], refs[n:2 * n]
        send, recv = refs[2 * n:]
        x, y, c = _me()
        cps = []
        for i in range(n):
            for j in range(4):
                r = pltpu.make_async_remote_copy(
                    src_ref=ins[i].at[j, 1 - c], dst_ref=got[i].at[j], send_sem=send.at[i, j], recv_sem=recv.at[i, j],
                    device_id=(x, y, 1 - c), device_id_type=MESH)
                r.start()
                cps.append(r)
        for r in cps:
            r.wait()

    return _call(
        body, name="grad_pair_exchange", in_specs=[HBM] * n, out_specs=[HBM] * n,
        out_shape=[jax.ShapeDtypeStruct((4,) + g.shape[2:], g.dtype) for g in grads],
        scratch_shapes=[pltpu.SemaphoreType.DMA((n, 4)), pltpu.SemaphoreType.DMA((n, 4))],
        compiler_params=_params(),
    )(*grads)


def _chip_exchange(sums):
    n = len(sums)

    def body(*refs):
        ins, got = refs[:n], refs[n:2 * n]
        send, recv = refs[2 * n:]
        x, y, c = _me()
        chips = _other_chips(x, y)
        cps = []
        for i in range(n):
            for k in range(3):
                r = pltpu.make_async_remote_copy(
                    src_ref=ins[i].at[_chip_of(*chips[k])], dst_ref=got[i].at[k], send_sem=send.at[i, k],
                    recv_sem=recv.at[i, k], device_id=(*chips[k], c), device_id_type=MESH)
                r.start()
                cps.append(r)
        for r in cps:
            r.wait()

    return _call(
        body, name="grad_chip_exchange", in_specs=[HBM] * n, out_specs=[HBM] * n,
        out_shape=[jax.ShapeDtypeStruct((3,) + s.shape[1:], s.dtype) for s in sums],
        scratch_shapes=[pltpu.SemaphoreType.DMA((n, 3)), pltpu.SemaphoreType.DMA((n, 3))],
        compiler_params=_params(),
    )(*sums)


def _pair_share(halves):
    n = len(halves)

    def body(*refs):
        ins, outs = refs[:n], refs[n:2 * n]
        send, recv = refs[2 * n:]
        x, y, c = _me()
        cps = []
        for i in range(n):
            r = pltpu.make_async_remote_copy(
                src_ref=ins[i], dst_ref=outs[i], send_sem=send.at[i], recv_sem=recv.at[i],
                device_id=(x, y, 1 - c), device_id_type=MESH)
            r.start()
            cps.append(r)
        for r in cps:
            r.wait()

    return _call(
        body, name="grad_pair_share", in_specs=[HBM] * n, out_specs=[HBM] * n,
        out_shape=[jax.ShapeDtypeStruct(h.shape, h.dtype) for h in halves],
        scratch_shapes=[pltpu.SemaphoreType.DMA((n,)), pltpu.SemaphoreType.DMA((n,))],
        compiler_params=_params(),
    )(*halves)


def _allreduce_small(vec):
    P = vec.shape[1]

    def body(v_ref, sum_ref, all_ref, send, recv):
        x, y, c = _me()
        me = 4 * x + 2 * y + c
        all_ref[pl.ds(me, 1)] = v_ref[...][None]
        cps = []
        for d in range(1, 8):
            peer = (jnp.bitwise_xor(x, d >> 2), jnp.bitwise_xor(y, (d >> 1) & 1), jnp.bitwise_xor(c, d & 1))
            r = pltpu.make_async_remote_copy(
                src_ref=v_ref, dst_ref=all_ref.at[me], send_sem=send.at[d - 1], recv_sem=recv.at[d - 1],
                device_id=peer, device_id_type=MESH)
            r.start()
            cps.append(r)
        for d in range(1, 8):
            src = jnp.bitwise_xor(me, d)
            pltpu.make_async_remote_copy(
                src_ref=v_ref, dst_ref=all_ref.at[src], send_sem=send.at[d - 1], recv_sem=recv.at[d - 1],
                device_id=(x, y, c), device_id_type=MESH).wait_recv()
        for r in cps:
            r.wait_send()
        acc = all_ref[0]
        for i in range(1, 8):
            acc = acc + all_ref[i]
        sum_ref[...] = acc

    vm = pl.BlockSpec(memory_space=pltpu.VMEM)
    return _call(
        body, name="allreduce_small", in_specs=[vm], out_specs=[vm, vm],
        out_shape=[jax.ShapeDtypeStruct((8, P), F32), jax.ShapeDtypeStruct((8, 8, P), F32)],
        scratch_shapes=[pltpu.SemaphoreType.DMA((7,)), pltpu.SemaphoreType.DMA((7,))],
        compiler_params=_params(),
    )(vec)[0]


def _per_batch(mod, B, D):
    return [mod[:B, i * D:(i + 1) * D].reshape(B, 1, D) for i in range(3)]


def _pad_rows8(a):
    return jnp.concatenate([a, jnp.zeros((8 - a.shape[0],) + a.shape[1:], a.dtype)], axis=0)


def _layer_fwd(x, c8, w, S, fox, tag):
    T, D = x.shape
    B = T // S
    DI = w["w_out"].shape[0]
    H = DI // HEAD_DIM
    tq = _tile(S, ATT_BLOCK, 8)
    mod = _mod_fwd(c8, w["w_ada"], w["b_ada"], tag + "_mod_fwd")
    shift, scale, gate = _per_batch(mod, B, D)
    proj, h = _ln_proj(x, shift, scale, w["norm_g"], w["w_in"], S, tag + "_ln_proj")
    saved = dict(x=x, h=h, proj=proj, scale=scale, gate=gate)
    if fox:
        fl = _mm(h, w["w_f"], "nn", F32, tag + "_flogit").reshape(B, S, LANES)
        cum = _cum_fwd(fl, w["b_f"], tag + "_cum_fwd")
        cumrow = cum[:, :, :H].transpose(0, 2, 1).reshape(B, H, S // tq, 1, tq)
        o, stat = _fox_fwd(proj, cum, cumrow, tag + "_attn_fwd")
        saved.update(fl=fl, cum=cum, cumrow=cumrow)
    else:
        o, stat = _sb_fwd(proj, B, tq, tag + "_attn_fwd")
    xo, y, u = _gate_out(o, proj, w["w_out"], x, gate, S, tag + "_gate_out")
    saved.update(o=o, stat=stat, y=y, u=u)
    return xo, saved


def _layer_bwd(dxo, sv, w, cT, S, fox, tag):
    T, D = dxo.shape
    B = T // S
    DI = w["w_out"].shape[0]
    H = DI // HEAD_DIM
    tq = _tile(S, ATT_BLOCK, 8)
    dy, do, dzg, dgate = _out_bwd(dxo, sv["y"], sv["gate"], w["w_out"], sv["o"], sv["proj"], S, tag + "_out_bwd")
    g = {"w_out": _mm(sv["u"], dy, "tn", F32, tag + "_dw_out", tm=1024, tn=1024, tk=2048)}
    if fox:
        dqkv, dcs = _fox_bwd(sv["proj"], do, sv["o"], sv["stat"], sv["cum"], sv["cumrow"], tag + "_attn_bwd")
        dcs = dcs.reshape(B, H, S).transpose(0, 2, 1)
        dcs = jnp.concatenate([dcs, jnp.zeros((B, S, LANES - H), F32)], axis=-1)
        dfl, db_f = _cum_bwd(dcs, sv["fl"], w["b_f"], tag + "_cum_bwd")
        g["b_f"] = db_f[:, :H]
        tail = [dfl.reshape(T, LANES).astype(BF16)]
        w_in = jnp.concatenate([w["w_in"], w["w_f"]], axis=1)
    else:
        dqkv = _sb_bwd(sv["proj"], do, sv["stat"], B, tq, tag + "_attn_bwd")
        tail = []
        w_in = w["w_in"]
    dproj = jnp.concatenate([dqkv[0], dqkv[1], dqkv[2], dzg] + tail, axis=1)
    N = dproj.shape[1]
    dw_in = _mm(sv["h"], dproj, "tn", F32, tag + "_dw_in", tm=1024, tn=640 if N % 640 == 0 else 512, tk=2048)
    g["w_in"] = dw_in[:, :4 * DI + H] if fox else dw_in
    dh = _mm(dproj, w_in, "nt", F32, tag + "_dh", tm=512, tn=1024, tk=1664 if N % 1664 == 0 else 2048)
    dx, dshift, dscale, dg = _ln_bwd(dh, sv["x"], dxo, sv["scale"], w["norm_g"], S, tag + "_ln_bwd")
    g["norm_g"] = dg
    dmod = jnp.concatenate([dshift, dscale, dgate], axis=-1).reshape(B, 3 * D)
    g["w_ada"], g["b_ada"] = _mod_bwd(cT, _pad_rows8(dmod), B, tag + "_mod_bwd")
    return dx, g


def _local_step(x3, c, tgt3, wf, ws, final_g):
    B, S, D = x3.shape
    T = B * S
    x = x3.reshape(T, D)
    c8 = _pad_rows8(c)
    cT = c8.T
    x1, sv1 = _layer_fwd(x, c8, wf, S, True, "fox")
    x2, sv2 = _layer_fwd(x1, c8, ws, S, False, "sb")
    dx2, dgf, loss = _final_loss(x2, tgt3.reshape(T, D), final_g, S, "final_loss")
    dx1, gs = _layer_bwd(dx2, sv2, ws, cT, S, False, "sb")
    dx0, gf = _layer_bwd(dx1, sv1, wf, cT, S, True, "fox")
    return loss, dx0.reshape(B, S, D), gf, gs, dgf


def _cols_to_shards(a):
    R, C4 = a.shape
    return a.reshape(R, 4, C4 // 4).transpose(1, 0, 2)


def _shards_to_cols(a):
    n, R, C = a.shape
    return a.transpose(1, 0, 2).reshape(R, n * C)


def kernel(x, c, fox_norm_g, fox_w_ada, fox_b_ada, fox_w_in, fox_b_f, fox_w_out, sb_norm_g, sb_w_ada, sb_b_ada, sb_w_in, sb_w_out, final_norm_g, loss_target, m_fox_norm_g, m_fox_w_ada, m_fox_b_ada, m_fox_w_in, m_fox_b_f, m_fox_w_out, m_sb_norm_g, m_sb_w_ada, m_sb_b_ada, m_sb_w_in, m_sb_w_out, m_final_norm_g, v_fox_norm_g, v_fox_w_ada, v_fox_b_ada, v_fox_w_in, v_fox_b_f, v_fox_w_out, v_sb_norm_g, v_sb_w_ada, v_sb_b_ada, v_sb_w_in, v_sb_w_out, v_final_norm_g):
    B, S, D = x.shape
    DI = 4 * fox_w_out.shape[1]
    H = DI // HEAD_DIM
    chip = _chip_of(lax.axis_index("x"), lax.axis_index("y"))

    big_names = ["fox_w_ada", "fox_w_in", "fox_w_out", "sb_w_ada", "sb_w_in", "sb_w_out"]
    big = dict(fox_w_ada=fox_w_ada[0], fox_w_in=fox_w_in[0], fox_w_out=fox_w_out[0],
               sb_w_ada=sb_w_ada[0], sb_w_in=sb_w_in[0], sb_w_out=sb_w_out[0])
    halves = [big[n].astype(BF16).reshape(2, big[n].shape[0] // 2, big[n].shape[1]) for n in big_names]
    gathered, gsmall = _gather_weights(halves, [sb_norm_g, sb_b_ada])
    gathered = [lax.dynamic_update_index_in_dim(a, own, chip, 0) for a, own in zip(gathered, halves)]
    gsmall = [lax.dynamic_update_index_in_dim(a, own, chip, 0) for a, own in zip(gsmall, [sb_norm_g, sb_b_ada])]
    full = {}
    for n, a in zip(big_names, gathered):
        a = a.reshape(4, a.shape[1] * a.shape[2], a.shape[3])
        full[n] = a.reshape(4 * a.shape[1], a.shape[2]) if n.endswith("w_out") else _shards_to_cols(a)
    sb_norm_full = gsmall[0].reshape(1, D)
    sb_b_ada_full = gsmall[1].reshape(1, 3 * D)
    w_f = jnp.concatenate([full["fox_w_in"][:, 4 * DI:], jnp.zeros((D, LANES - H), BF16)], axis=1)
    b_f = jnp.concatenate([fox_b_f, jnp.zeros((1, LANES - H), F32)], axis=1)
    wf = dict(w_ada=full["fox_w_ada"], b_ada=fox_b_ada, norm_g=fox_norm_g, w_in=full["fox_w_in"][:, :4 * DI],
              w_f=w_f, b_f=b_f, w_out=full["fox_w_out"])
    ws = dict(w_ada=full["sb_w_ada"], b_ada=sb_b_ada_full, norm_g=sb_norm_full, w_in=full["sb_w_in"],
              w_out=full["sb_w_out"])

    loss, grad_x, gf, gs, dgf = _local_step(x, c, loss_target, wf, ws, final_norm_g.reshape(1, D))

    part = dict(fox_w_ada=gf["w_ada"], fox_w_in=gf["w_in"], fox_w_out=gf["w_out"],
                sb_w_ada=gs["w_ada"], sb_w_in=gs["w_in"], sb_w_out=gs["w_out"])
    shard_major = []
    for n in big_names:
        a = part[n]
        a = a.reshape(4, a.shape[0] // 4, a.shape[1]) if n.endswith("w_out") else _cols_to_shards(a)
        shard_major.append(a.reshape(4, 2, a.shape[1] // 2, a.shape[2]))
    core = lax.axis_index("c")
    got = _pair_exchange(shard_major)
    pair_f32, pair_bf16 = [], []
    for n, g4, b in zip(big_names, shard_major, got):
        a = lax.dynamic_index_in_dim(g4, core, axis=1, keepdims=False)
        r, C = a.shape[1:]
        s32, s16 = _ew_sum([a.reshape(4 * r, C), b.reshape(4 * r, C)], n + "_pair_sum", also_bf16=True)
        pair_f32.append(s32.reshape(4, r, C))
        pair_bf16.append(s16.reshape(4, r, C))
    others = _chip_exchange(pair_bf16)
    reduced_halves = [_ew_sum([lax.dynamic_index_in_dim(a, chip, axis=0, keepdims=False), b[0], b[1], b[2]],
                              n + "_chip_sum")[0] for n, a, b in zip(big_names, pair_f32, others)]
    theirs = _pair_share(reduced_halves)
    grad_big = {}
    for n, a, b in zip(big_names, reduced_halves, theirs):
        grad_big[n] = jnp.concatenate([jnp.where(core == 0, a, b), jnp.where(core == 0, b, a)], axis=0)

    pieces = [loss, gf["norm_g"], gf["b_ada"], jnp.concatenate([gf["b_f"], jnp.zeros((1, LANES - H), F32)], axis=1),
              gs["norm_g"], gs["b_ada"], dgf]
    vec = jnp.concatenate(pieces, axis=1)
    red = _allreduce_small(_pad_rows8(vec))[0:1]
    offs = [0]
    for p in pieces:
        offs.append(offs[-1] + p.shape[1])
    r_loss, r_fng, r_fba, r_fbf, r_sng, r_sba, r_fin = [red[:, offs[i]:offs[i + 1]] for i in range(7)]
    small_grads = dict(
        fox_norm_g=r_fng, fox_b_ada=r_fba, fox_b_f=r_fbf[:, :H],
        sb_norm_g=lax.dynamic_slice_in_dim(r_sng, chip * (D // 4), D // 4, axis=1),
        sb_b_ada=lax.dynamic_slice_in_dim(r_sba, chip * (3 * D // 4), 3 * D // 4, axis=1),
        final_norm_g=r_fin)

    weights = dict(fox_norm_g=fox_norm_g, fox_w_ada=fox_w_ada, fox_b_ada=fox_b_ada, fox_w_in=fox_w_in, fox_b_f=fox_b_f,
                   fox_w_out=fox_w_out, sb_norm_g=sb_norm_g, sb_w_ada=sb_w_ada, sb_b_ada=sb_b_ada, sb_w_in=sb_w_in,
                   sb_w_out=sb_w_out, final_norm_g=final_norm_g)
    ms = dict(fox_norm_g=m_fox_norm_g, fox_w_ada=m_fox_w_ada, fox_b_ada=m_fox_b_ada, fox_w_in=m_fox_w_in,
              fox_b_f=m_fox_b_f, fox_w_out=m_fox_w_out, sb_norm_g=m_sb_norm_g, sb_w_ada=m_sb_w_ada,
              sb_b_ada=m_sb_b_ada, sb_w_in=m_sb_w_in, sb_w_out=m_sb_w_out, final_norm_g=m_final_norm_g)
    vs = dict(fox_norm_g=v_fox_norm_g, fox_w_ada=v_fox_w_ada, fox_b_ada=v_fox_b_ada, fox_w_in=v_fox_w_in,
              fox_b_f=v_fox_b_f, fox_w_out=v_fox_w_out, sb_norm_g=v_sb_norm_g, sb_w_ada=v_sb_w_ada,
              sb_b_ada=v_sb_b_ada, sb_w_in=v_sb_w_in, sb_w_out=v_sb_w_out, final_norm_g=v_final_norm_g)
    order = ["fox_norm_g", "fox_w_ada", "fox_b_ada", "fox_w_in", "fox_b_f", "fox_w_out", "sb_norm_g", "sb_w_ada",
             "sb_b_ada", "sb_w_in", "sb_w_out", "final_norm_g"]
    grads, deltas, new_m, new_v = {}, {}, {}, {}
    for n in big_names:
        shp = weights[n].shape
        g2 = grad_big[n]
        d, m2, v2 = _adamw(weights[n][0], g2, ms[n][0], vs[n][0], n + "_adamw")
        grads[n], deltas[n], new_m[n], new_v[n] = g2.reshape(shp), d.reshape(shp), m2.reshape(shp), v2.reshape(shp)
    small_names = [n for n in order if n not in big_names]
    sizes = [small_grads[n].shape[1] for n in small_names]
    total = sum(sizes)
    padn = (-total) % LANES

    def pack(d):
        return jnp.concatenate([d[n].reshape(1, -1) for n in small_names] + [jnp.ones((1, padn), F32)], axis=1)

    sd, sm, sv_ = _adamw(pack(weights), pack(small_grads), pack(ms), pack(vs), "small_adamw")
    o = 0
    for n, sz in zip(small_names, sizes):
        shp = weights[n].shape
        grads[n] = small_grads[n].reshape(shp)
        deltas[n], new_m[n], new_v[n] = (t[:, o:o + sz].reshape(shp) for t in (sd, sm, sv_))
        o += sz
    return (r_loss[0, 0], grad_x, *[grads[n] for n in order], *[deltas[n] for n in order],
            *[new_m[n] for n in order], *[new_v[n] for n in order])
```

```python
import functools

import jax
import jax.numpy as jnp
from jax import lax
from jax.experimental import pallas as pl
from jax.experimental.pallas import tpu as pltpu

F32 = jnp.float32
BF16 = jnp.bfloat16
HEAD_DIM = 64
LOG2E = 1.4426950408889634
LN2 = 0.6931471805599453
Q_SCALE = HEAD_DIM ** -0.5 * LOG2E
LANES = 128
NORM_EPS = 1e-6
ADAM_LR = 0.001
ADAM_B1 = 0.9
ADAM_B2 = 0.999
ADAM_EPS = 1e-08
ADAM_WD = 0.01
ADAM_STEP = 10
VMEM_LIMIT = 56 * 1024 * 1024
ATT_BLOCK = 256
MESH = pl.DeviceIdType.MESH
HBM = pl.BlockSpec(memory_space=pltpu.HBM)
NT = (((1,), (1,)), ((), ()))
TN = (((0,), (0,)), ((), ()))


def _call(body, **kw):
    return pl.pallas_call(body, **kw)


def _params(**kw):
    return pltpu.CompilerParams(vmem_limit_bytes=VMEM_LIMIT, **kw)


def _tile(dim, pref, mult=128):
    if dim <= pref:
        return dim
    t = (pref // mult) * mult
    while t >= mult:
        if dim % t == 0:
            return t
        t -= mult
    return dim


def _sigmoid(x):
    return 1.0 / (1.0 + jnp.exp(-x))


def _split3(x):
    hi = x.astype(BF16)
    r = x - hi.astype(F32)
    mid = r.astype(BF16)
    lo = (r - mid.astype(F32)).astype(BF16)
    return hi, mid, lo


def _mm(a, b, mode, out_dtype, name, tm=512, tn=512, tk=512, col_scale=None):
    if mode == "nn":
        (M, K), (_, N) = a.shape, b.shape
    elif mode == "nt":
        (M, K), (N, _) = a.shape, b.shape
    else:
        (K, M), (_, N) = a.shape, b.shape
    tm, tn, tk = _tile(M, tm), _tile(N, tn), _tile(K, tk)
    nk = K // tk
    dims = {"nn": (((1,), (0,)), ((), ())), "nt": NT, "tn": TN}[mode]

    def body(a_ref, b_ref, *rest):
        o_ref, acc_ref = rest[-2:]
        k = pl.program_id(2)

        @pl.when(k == 0)
        def _():
            acc_ref[...] = jnp.zeros_like(acc_ref)

        acc_ref[...] += lax.dot_general(a_ref[...], b_ref[...], dims, preferred_element_type=F32)

        @pl.when(k == nk - 1)
        def _():
            acc = acc_ref[...]
            if col_scale is not None:
                acc = acc * rest[0][...]
            o_ref[...] = acc.astype(out_dtype)

    a_spec = (pl.BlockSpec((tk, tm), lambda i, j, k: (k, i)) if mode == "tn"
              else pl.BlockSpec((tm, tk), lambda i, j, k: (i, k)))
    b_spec = (pl.BlockSpec((tn, tk), lambda i, j, k: (j, k)) if mode == "nt"
              else pl.BlockSpec((tk, tn), lambda i, j, k: (k, j)))
    extra_specs = [] if col_scale is None else [pl.BlockSpec((1, tn), lambda i, j, k: (0, j))]
    extra = [] if col_scale is None else [col_scale]
    return _call(
        body, name=name, grid=(M // tm, N // tn, nk),
        in_specs=[a_spec, b_spec] + extra_specs, out_specs=pl.BlockSpec((tm, tn), lambda i, j, k: (i, j)),
        out_shape=jax.ShapeDtypeStruct((M, N), out_dtype),
        scratch_shapes=[pltpu.VMEM((tm, tn), F32)], compiler_params=_params(),
    )(a, b, *extra)


def _mod_fwd(c8, w_ada, b_ada, name):
    D, N = w_ada.shape
    tn = _tile(N, 512)

    def body(c_ref, w_ref, b_ref, o_ref):
        c = c_ref[...]
        sc = (c * _sigmoid(c)).astype(BF16)
        o_ref[...] = jnp.dot(sc, w_ref[...], preferred_element_type=F32) + b_ref[...]

    return _call(
        body, name=name, grid=(N // tn,),
        in_specs=[pl.BlockSpec((8, D), lambda j: (0, 0)), pl.BlockSpec((D, tn), lambda j: (0, j)),
                  pl.BlockSpec((1, tn), lambda j: (0, j))],
        out_specs=pl.BlockSpec((8, tn), lambda j: (0, j)),
        out_shape=jax.ShapeDtypeStruct((8, N), F32), compiler_params=_params(),
    )(c8, w_ada, b_ada)


def _mod_bwd(cT, dmod8, nb, name):
    D = cT.shape[0]
    N = dmod8.shape[1]
    tn = _tile(N, 512)

    def body(c_ref, d_ref, w_ref, b_ref):
        c = c_ref[...]
        sc = c * _sigmoid(c)
        d = d_ref[...]
        acc = sc[:, 0:1] * d[0:1, :]
        bsum = d[0:1, :]
        for b in range(1, nb):
            acc = acc + sc[:, b:b + 1] * d[b:b + 1, :]
            bsum = bsum + d[b:b + 1, :]
        w_ref[...] = acc
        b_ref[...] = bsum

    return _call(
        body, name=name, grid=(N // tn,),
        in_specs=[pl.BlockSpec((D, 8), lambda j: (0, 0)), pl.BlockSpec((8, tn), lambda j: (0, j))],
        out_specs=[pl.BlockSpec((D, tn), lambda j: (0, j)), pl.BlockSpec((1, tn), lambda j: (0, j))],
        out_shape=[jax.ShapeDtypeStruct((D, N), F32), jax.ShapeDtypeStruct((1, N), F32)],
        compiler_params=_params(),
    )(cT, dmod8)


def _ln_proj(x, shift, scale, g, w, S, name):
    T, D = x.shape
    N = w.shape[1]
    tm = _tile(S, 2048)
    tn = _tile(N, 1024)
    per_b = S // tm

    def body(x_ref, sh_ref, sc_ref, g_ref, w_ref, p_ref, h_ref):
        @pl.when(pl.program_id(1) == 0)
        def _():
            xv = x_ref[...]
            r = lax.rsqrt(jnp.mean(xv * xv, axis=-1, keepdims=True) + NORM_EPS)
            h = (xv * r) * g_ref[...] * (1.0 + sc_ref[0]) + sh_ref[0]
            h_ref[...] = h.astype(BF16)

        p_ref[...] = jnp.dot(h_ref[...], w_ref[...], preferred_element_type=F32).astype(BF16)

    return _call(
        body, name=name, grid=(T // tm, N // tn),
        in_specs=[pl.BlockSpec((tm, D), lambda i, j: (i, 0)),
                  pl.BlockSpec((1, 1, D), lambda i, j: (i // per_b, 0, 0)),
                  pl.BlockSpec((1, 1, D), lambda i, j: (i // per_b, 0, 0)),
                  pl.BlockSpec((1, D), lambda i, j: (0, 0)),
                  pl.BlockSpec((D, tn), lambda i, j: (0, j))],
        out_specs=[pl.BlockSpec((tm, tn), lambda i, j: (i, j)), pl.BlockSpec((tm, D), lambda i, j: (i, 0))],
        out_shape=[jax.ShapeDtypeStruct((T, N), BF16), jax.ShapeDtypeStruct((T, D), BF16)],
        compiler_params=_params(),
    )(x, shift, scale, g, w)


def _ln_bwd(dh, x, dxo, scale, g, S, name):
    T, D = x.shape
    B = T // S
    tm = _tile(S, 512)
    per_b = S // tm

    def body(dh_ref, x_ref, dxo_ref, sc_ref, g_ref, dx_ref, dsh_ref, dsc_ref, dg_ref):
        i = pl.program_id(0)
        xv = x_ref[...]
        dh_v = dh_ref[...]
        r = lax.rsqrt(jnp.mean(xv * xv, axis=-1, keepdims=True) + NORM_EPS)
        xn = xv * r
        gv = g_ref[...]
        one_sc = 1.0 + sc_ref[0]
        dhxn = dh_v * xn

        @pl.when(i % per_b == 0)
        def _():
            dsh_ref[...] = jnp.zeros_like(dsh_ref)
            dsc_ref[...] = jnp.zeros_like(dsc_ref)

        @pl.when(i == 0)
        def _():
            dg_ref[...] = jnp.zeros_like(dg_ref)

        dsh_ref[0] += jnp.sum(dh_v, axis=0, keepdims=True)
        dsc_ref[0] += jnp.sum(dhxn, axis=0, keepdims=True) * gv
        dg_ref[...] += jnp.sum(dhxn, axis=0, keepdims=True) * one_sc
        dxn = dh_v * (gv * one_sc)
        dx_ref[...] = r * (dxn - xn * jnp.mean(dxn * xn, axis=-1, keepdims=True)) + dxo_ref[...]

    row = pl.BlockSpec((tm, D), lambda i: (i, 0))
    per = pl.BlockSpec((1, 1, D), lambda i: (i // per_b, 0, 0))
    vec = pl.BlockSpec((1, D), lambda i: (0, 0))
    return _call(
        body, name=name, grid=(T // tm,),
        in_specs=[row, row, row, per, vec], out_specs=[row, per, per, vec],
        out_shape=[jax.ShapeDtypeStruct((T, D), F32), jax.ShapeDtypeStruct((B, 1, D), F32),
                   jax.ShapeDtypeStruct((B, 1, D), F32), jax.ShapeDtypeStruct((1, D), F32)],
        compiler_params=_params(),
    )(dh, x, dxo, scale, g)


def _gate_out(o, proj, w_out, x, gate, S, name):
    T, DI = o.shape
    D = w_out.shape[1]
    tm = _tile(S, 256)
    per_b = S // tm

    def body(o_ref, z_ref, w_ref, x_ref, g_ref, xo_ref, y_ref, u_ref):
        z = z_ref[...].astype(F32)
        u = (o_ref[...] * (z * _sigmoid(z))).astype(BF16)
        u_ref[...] = u
        y = jnp.dot(u, w_ref[...], preferred_element_type=F32)
        y_ref[...] = y
        xo_ref[...] = x_ref[...] + g_ref[0] * y

    wide = pl.BlockSpec((tm, DI), lambda i: (i, 0))
    row = pl.BlockSpec((tm, D), lambda i: (i, 0))
    return _call(
        body, name=name, grid=(T // tm,),
        in_specs=[wide, pl.BlockSpec((tm, DI), lambda i: (i, 3)), pl.BlockSpec((DI, D), lambda i: (0, 0)), row,
                  pl.BlockSpec((1, 1, D), lambda i: (i // per_b, 0, 0))],
        out_specs=[row, row, wide],
        out_shape=[jax.ShapeDtypeStruct((T, D), F32), jax.ShapeDtypeStruct((T, D), F32),
                   jax.ShapeDtypeStruct((T, DI), BF16)],
        compiler_params=_params(),
    )(o, proj, w_out, x, gate)


def _out_bwd(dxo, y, gate, w_out, o, proj, S, name):
    T, D = dxo.shape
    DI = o.shape[1]
    B = T // S
    tm = _tile(S, 256)
    per_b = S // tm

    def body(dxo_ref, y_ref, g_ref, w_ref, o_ref, z_ref, dy_ref, do_ref, dz_ref, dg_ref):
        dxo_v = dxo_ref[...]
        dy = (dxo_v * g_ref[0]).astype(BF16)
        dy_ref[...] = dy
        du = lax.dot_general(dy, w_ref[...], NT, preferred_element_type=F32)
        z = z_ref[...].astype(F32)
        sg = _sigmoid(z)
        do_ref[...] = (du * (z * sg)).astype(BF16)
        dz_ref[...] = (du * o_ref[...] * (sg * (1.0 + z * (1.0 - sg)))).astype(BF16)

        @pl.when(pl.program_id(0) % per_b == 0)
        def _():
            dg_ref[...] = jnp.zeros_like(dg_ref)

        dg_ref[0] += jnp.sum(dxo_v * y_ref[...], axis=0, keepdims=True)

    wide = pl.BlockSpec((tm, DI), lambda i: (i, 0))
    row = pl.BlockSpec((tm, D), lambda i: (i, 0))
    per = pl.BlockSpec((1, 1, D), lambda i: (i // per_b, 0, 0))
    return _call(
        body, name=name, grid=(T // tm,),
        in_specs=[row, row, per, pl.BlockSpec((DI, D), lambda i: (0, 0)), wide,
                  pl.BlockSpec((tm, DI), lambda i: (i, 3))],
        out_specs=[row, wide, wide, per],
        out_shape=[jax.ShapeDtypeStruct((T, D), BF16), jax.ShapeDtypeStruct((T, DI), BF16),
                   jax.ShapeDtypeStruct((T, DI), BF16), jax.ShapeDtypeStruct((B, 1, D), F32)],
        compiler_params=_params(),
    )(dxo, y, gate, w_out, o, proj)


def _final_loss(x, tgt, g, S, name):
    T, D = x.shape
    tm = _tile(S, 512)

    def body(x_ref, t_ref, g_ref, dx_ref, dg_ref, l_ref):
        @pl.when(pl.program_id(0) == 0)
        def _():
            dg_ref[...] = jnp.zeros_like(dg_ref)
            l_ref[...] = jnp.zeros_like(l_ref)

        xv = x_ref[...]
        gv = g_ref[...]
        r = lax.rsqrt(jnp.mean(xv * xv, axis=-1, keepdims=True) + NORM_EPS)
        xn = xv * r
        e = xn * gv - t_ref[...]
        part = jnp.sum(jnp.sum(e * e, axis=0, keepdims=True), axis=1, keepdims=True)
        l_ref[...] += (0.5 / D) * part
        dy = e * (1.0 / D)
        dg_ref[...] += jnp.sum(dy * xn, axis=0, keepdims=True)
        dxn = dy * gv
        dx_ref[...] = r * (dxn - xn * jnp.mean(dxn * xn, axis=-1, keepdims=True))

    row = pl.BlockSpec((tm, D), lambda i: (i, 0))
    return _call(
        body, name=name, grid=(T // tm,),
        in_specs=[row, row, pl.BlockSpec((1, D), lambda i: (0, 0))],
        out_specs=[row, pl.BlockSpec((1, D), lambda i: (0, 0)), pl.BlockSpec((1, LANES), lambda i: (0, 0))],
        out_shape=[jax.ShapeDtypeStruct((T, D), F32), jax.ShapeDtypeStruct((1, D), F32),
                   jax.ShapeDtypeStruct((1, LANES), F32)],
        compiler_params=_params(),
    )(x, tgt, g)


def _cum_fwd(fl, bf, name):
    B, S, _ = fl.shape
    ch = _tile(S, 256, 8)

    def body(fl_ref, b_ref, cum_ref):
        ri = lax.broadcasted_iota(jnp.int32, (ch, ch), 0)
        ci = lax.broadcasted_iota(jnp.int32, (ch, ch), 1)
        tri = jnp.where(ri >= ci, 1.0, 0.0).astype(BF16)

        def step(i, carry):
            r0 = pl.multiple_of(i * ch, ch)
            z = fl_ref[0, pl.ds(r0, ch), :] + b_ref[...]
            lf = (jnp.minimum(z, 0.0) - jnp.log(1.0 + jnp.exp(-jnp.abs(z)))) * LOG2E
            hi, mid, lo = _split3(lf)
            cs = (jnp.dot(tri, hi, preferred_element_type=F32) + jnp.dot(tri, mid, preferred_element_type=F32)
                  + jnp.dot(tri, lo, preferred_element_type=F32)) + carry
            cum_ref[0, pl.ds(r0, ch), :] = cs
            return cs[ch - 1:ch, :]

        lax.fori_loop(0, S // ch, step, jnp.zeros((1, LANES), F32))

    blk = pl.BlockSpec((1, S, LANES), lambda b: (b, 0, 0))
    return _call(
        body, name=name, grid=(B,), in_specs=[blk, pl.BlockSpec((1, LANES), lambda b: (0, 0))], out_specs=blk,
        out_shape=jax.ShapeDtypeStruct((B, S, LANES), F32), compiler_params=_params(),
    )(fl, bf)


def _cum_bwd(dcs, fl, bf, name):
    B, S, _ = fl.shape
    ch = _tile(S, 256, 8)
    n = S // ch

    def body(d_ref, fl_ref, b_ref, o_ref, db_ref):
        ri = lax.broadcasted_iota(jnp.int32, (ch, ch), 0)
        ci = lax.broadcasted_iota(jnp.int32, (ch, ch), 1)
        tri = jnp.where(ci >= ri, 1.0, 0.0).astype(BF16)

        @pl.when(pl.program_id(0) == 0)
        def _():
            db_ref[...] = jnp.zeros_like(db_ref)

        def step(t, carry):
            tail, dbsum = carry
            r0 = pl.multiple_of((n - 1 - t) * ch, ch)
            hi, mid, lo = _split3(d_ref[0, pl.ds(r0, ch), :])
            suf = (jnp.dot(tri, hi, preferred_element_type=F32) + jnp.dot(tri, mid, preferred_element_type=F32)
                   + jnp.dot(tri, lo, preferred_element_type=F32)) + tail
            z = fl_ref[0, pl.ds(r0, ch), :] + b_ref[...]
            dfl = -suf * _sigmoid(-z)
            o_ref[0, pl.ds(r0, ch), :] = dfl
            return suf[0:1, :], dbsum + jnp.sum(dfl, axis=0, keepdims=True)

        z1 = jnp.zeros((1, LANES), F32)
        _, dbsum = lax.fori_loop(0, n, step, (z1, z1))
        db_ref[...] += dbsum

    blk = pl.BlockSpec((1, S, LANES), lambda b: (b, 0, 0))
    vec = pl.BlockSpec((1, LANES), lambda b: (0, 0))
    return _call(
        body, name=name, grid=(B,), in_specs=[blk, blk, vec], out_specs=[blk, vec],
        out_shape=[jax.ShapeDtypeStruct((B, S, LANES), F32), jax.ShapeDtypeStruct((1, LANES), F32)],
        compiler_params=_params(),
    )(dcs, fl, bf)


HEADS_PER_STEP = 4
GROUP = 2 * HEAD_DIM


def _step_width():
    return HEAD_DIM * HEADS_PER_STEP


def _cols(S, offset_blocks=0):
    return pl.BlockSpec((S, _step_width()), lambda b, h: (b, offset_blocks + h))


def _row_spec(nq, tq):
    return pl.BlockSpec((1, HEADS_PER_STEP, nq, 1, tq), lambda b, h: (b, h, 0, 0, 0))


def _lanes(g):
    return slice(GROUP * (g // 2), GROUP * (g // 2) + GROUP)


def _hi_lo(x):
    hi = x.astype(BF16)
    return hi, (x - hi.astype(F32)).astype(BF16)


def _dot(a, b, dims=None):
    if dims is None:
        return jnp.dot(a, b, preferred_element_type=F32)
    return lax.dot_general(a, b, dims, preferred_element_type=F32)


def _causal_blocks(nq, prep, init, stages, finish, combine=None, descending=False):
    heads = range(HEADS_PER_STEP)

    def qloop(qi, _):
        ctx = [prep(g, qi) for g in heads]

        def step(kj, carry, masked):
            st = list(carry)
            for n, stage in enumerate(stages):
                if combine is not None and n == len(stages) - 1:
                    combine(kj, ctx, st)
                st = [stage(g, ctx[g], kj, masked, st[g]) for g in heads]
            return tuple(st)

        carry = tuple(init() for _ in heads)
        if descending:
            carry = step(qi, carry, True)
            carry = lax.fori_loop(0, qi, lambda t, cr: step(qi - 1 - t, cr, False), carry)
        else:
            carry = lax.fori_loop(0, qi, lambda kj, cr: step(kj, cr, False), carry)
            carry = step(qi, carry, True)
        finish(qi, ctx, carry)
        return 0

    lax.fori_loop(0, nq, qloop, 0)


class _Block:
    def __init__(self, tq):
        self.tq = tq
        self.lane = lax.broadcasted_iota(jnp.int32, (tq, GROUP), 1)
        self.low = self.lane < HEAD_DIM
        self.ri = lax.broadcasted_iota(jnp.int32, (tq, tq), 0)
        self.ci = lax.broadcasted_iota(jnp.int32, (tq, tq), 1)

    def rows(self, i):
        return pl.ds(pl.multiple_of(i * self.tq, self.tq), self.tq)

    def own(self, g, x):
        return jnp.where(self.low if g % 2 == 0 else jnp.logical_not(self.low), x, jnp.zeros_like(x))

    def pair(self, a, b):
        return jnp.where(self.low, a, b)

    def stat(self, g, x):
        return jnp.sum(jnp.where(self.lane == HEAD_DIM * (g % 2), x, 0.0), axis=1, keepdims=True)


def _fox_fwd(proj, cumcol, cumrow, name):
    T, DI = proj.shape[0], proj.shape[1] // 4
    B, H, nq, _, tq = cumrow.shape
    S = nq * tq
    nb = DI // _step_width()

    def body(q_ref, k_ref, v_ref, cc_ref, cr_ref, o_ref, st_ref, acc_scr):
        h0 = pl.program_id(1) * HEADS_PER_STEP
        blk = _Block(tq)

        def prep(g, qi):
            acc_scr[g] = jnp.zeros((tq, GROUP), F32)
            q = blk.own(g, q_ref[blk.rows(qi), _lanes(g)])
            ccol = jnp.sum(jnp.where(blk.lane == h0 + g, cc_ref[0, blk.rows(qi), :], 0.0), axis=1, keepdims=True)
            return q, ccol

        def init():
            return jnp.full((tq, 1), -jnp.inf, F32), jnp.zeros((tq, 1), F32)

        def scores(g, ctx, kj, masked, st):
            return st + (_dot(ctx[0], k_ref[blk.rows(kj), _lanes(g)], NT),)

        def softmax(g, ctx, kj, masked, st):
            m, l, s = st
            s = s + ctx[1] - cr_ref[0, g, kj]
            if masked:
                s = jnp.where(blk.ci <= blk.ri, s, -jnp.inf)
            m_new = jnp.maximum(m, jnp.max(s, axis=1, keepdims=True))
            alpha = jnp.exp2(m - m_new)
            p = jnp.exp2(s - m_new)
            return (m_new, alpha * l + jnp.sum(p, axis=1, keepdims=True), alpha) + _hi_lo(p)

        def values(g, ctx, kj, masked, st):
            m, l, alpha, hi, lo = st
            v = v_ref[blk.rows(kj), _lanes(g)]
            acc_scr[g] = alpha * acc_scr[g] + (_dot(hi, v) + _dot(lo, v))
            return m, l

        def finish(qi, ctx, carry):
            for g in range(0, HEADS_PER_STEP, 2):
                (m0, l0), (m1, l1) = carry[g], carry[g + 1]
                o_ref[blk.rows(qi), _lanes(g)] = blk.pair(acc_scr[g] / l0, acc_scr[g + 1] / l1)
                st_ref[blk.rows(qi), _lanes(g)] = blk.pair(m0 + jnp.log2(l0), m1 + jnp.log2(l1))

        _causal_blocks(nq, prep, init, [scores, softmax, values], finish)

    out = jax.ShapeDtypeStruct((T, DI), F32)
    return _call(
        body, name=name, grid=(B, H // HEADS_PER_STEP),
        in_specs=[_cols(S), _cols(S, nb), _cols(S, 2 * nb), pl.BlockSpec((1, S, LANES), lambda b, h: (b, 0, 0)),
                  _row_spec(nq, tq)],
        out_specs=[_cols(S), _cols(S)], out_shape=[out, out],
        scratch_shapes=[pltpu.VMEM((HEADS_PER_STEP, tq, GROUP), F32)], compiler_params=_params(),
    )(proj, proj, proj, cumcol, cumrow)


def _fox_bwd(proj, do, o, stat, cumcol, cumrow, name):
    T, DI = do.shape
    B, H, nq, _, tq = cumrow.shape
    S = nq * tq
    nb = DI // _step_width()

    def body(q_ref, k_ref, v_ref, do_ref, o_ref, st_ref, cc_ref, cr_ref, dqkv_ref, dcs_ref, dk_acc, dv_acc, dq_scr):
        h0 = pl.program_id(1) * HEADS_PER_STEP
        blk = _Block(tq)
        dk_acc[...] = jnp.zeros_like(dk_acc)
        dv_acc[...] = jnp.zeros_like(dv_acc)
        dcs_ref[...] = jnp.zeros_like(dcs_ref)

        def prep(g, qi):
            dq_scr[g] = jnp.zeros((tq, GROUP), F32)
            q = blk.own(g, q_ref[blk.rows(qi), _lanes(g)])
            dout =blk.own(g, do_ref[blk.rows(qi), _lanes(g)])
            delta = jnp.sum(o_ref[blk.rows(qi), _lanes(g)] * dout.astype(F32), axis=1, keepdims=True)
            lse = blk.stat(g, st_ref[blk.rows(qi), _lanes(g)])
            ccol = jnp.sum(jnp.where(blk.lane == h0 + g, cc_ref[0, blk.rows(qi), :], 0.0), axis=1, keepdims=True)
            return q, dout, lse, delta, ccol

        def init():
            return ()

        def scores(g, ctx, kj, masked, st):
            return (_dot(ctx[0], k_ref[blk.rows(kj), _lanes(g)], NT), _dot(ctx[1], v_ref[blk.rows(kj), _lanes(g)], NT))

        def softmax_bwd(g, ctx, kj, masked, st):
            s, dp = st
            _, _, lse, delta, ccol = ctx
            s = s + ccol - cr_ref[0, g, kj]
            if masked:
                s = jnp.where(blk.ci <= blk.ri, s, -jnp.inf)
            p = jnp.exp2(s - lse)
            ds = p * (dp - delta)
            return p.astype(BF16), ds.astype(BF16), jnp.sum(ds, axis=0, keepdims=True)

        def combine(kj, ctx, st):
            for g in range(0, HEADS_PER_STEP, 2):
                dv_acc[blk.rows(kj), _lanes(g)] += _dot(st[g][0], ctx[g][1], TN) + _dot(st[g + 1][0], ctx[g + 1][1], TN)
                dk_acc[blk.rows(kj), _lanes(g)] += _dot(st[g][1], ctx[g][0], TN) + _dot(st[g + 1][1], ctx[g + 1][0], TN)
            for g in range(HEADS_PER_STEP):
                dcs_ref[0, g, kj] += st[g][2]

        def queries(g, ctx, kj, masked, st):
            dq_scr[g] += _dot(st[1], blk.own(g, k_ref[blk.rows(kj), _lanes(g)]))
            return ()

        def finish(qi, ctx, carry):
            for g in range(0, HEADS_PER_STEP, 2):
                dqkv_ref[0, blk.rows(qi), _lanes(g)] = ((dq_scr[g] + dq_scr[g + 1]) * LN2).astype(BF16)

        _causal_blocks(nq, prep, init, [scores, softmax_bwd, queries], finish, combine=combine)
        dqkv_ref[1] = (dk_acc[...] * LN2).astype(BF16)
        dqkv_ref[2] = dv_acc[...].astype(BF16)

    W = _step_width()
    return _call(
        body, name=name, grid=(B, H // HEADS_PER_STEP),
        in_specs=[_cols(S), _cols(S, nb), _cols(S, 2 * nb), _cols(S), _cols(S), _cols(S),
                  pl.BlockSpec((1, S, LANES), lambda b, h: (b, 0, 0)), _row_spec(nq, tq)],
        out_specs=[pl.BlockSpec((3, S, W), lambda b, h: (0, b, h)), _row_spec(nq, tq)],
        out_shape=[jax.ShapeDtypeStruct((3, T, DI), BF16), jax.ShapeDtypeStruct((B, H, nq, 1, tq), F32)],
        scratch_shapes=[pltpu.VMEM((S, W), F32), pltpu.VMEM((S, W), F32), pltpu.VMEM((HEADS_PER_STEP, tq, GROUP), F32)],
        compiler_params=_params(),
    )(proj, proj, proj, do, o, stat, cumcol, cumrow)


def _log2_keep(z2):
    nz = -z2
    e = jnp.exp2(jnp.minimum(z2, nz))
    return jnp.minimum(nz, 0.0) - jnp.log2(1.0 + e), e


def _sb_fwd(proj, B, tq, name):
    T, DI = proj.shape[0], proj.shape[1] // 4
    S = T // B
    H = DI // HEAD_DIM
    nq = S // tq
    nb = DI // _step_width()

    def body(q_ref, k_ref, v_ref, o_ref, st_ref, acc_scr, c_scr):
        blk = _Block(tq)
        strict = blk.ci < blk.ri
        above = jnp.where(blk.ri > blk.ci, 1.0, 0.0).astype(BF16)

        def prep(g, qi):
            acc_scr[g] = jnp.zeros((tq, GROUP), F32)
            c_scr[g] = jnp.zeros((tq, 1), F32)
            return blk.own(g, q_ref[blk.rows(qi), _lanes(g)])

        def init():
            return ()

        def scores(g, q, kj, masked, st):
            return (_dot(q, k_ref[blk.rows(kj), _lanes(g)], NT),)

        def logs(g, q, kj, masked, st):
            (z,) = st
            lk, _ = _log2_keep(z)
            lb = z + lk
            if masked:
                lk = jnp.where(strict, lk, 0.0)
            c = c_scr[g]
            c_scr[g] = c + jnp.sum(lk, axis=1, keepdims=True)
            return (lb + c,) + _hi_lo(lk)

        def suffix(g, q, kj, masked, st):
            lbc, hi, lo = st
            return lbc, _dot(hi, above) + _dot(lo, above)

        def weights(g, q, kj, masked, st):
            lbc, after = st
            a = jnp.exp2(lbc + after)
            if masked:
                a = jnp.where(strict, a, 0.0)
            return (a.astype(BF16),)

        def values(g, q, kj, masked, st):
            acc_scr[g] += _dot(st[0], v_ref[blk.rows(kj), _lanes(g)])
            return ()

        def finish(qi, ctx, carry):
            for g in range(0, HEADS_PER_STEP, 2):
                o_ref[blk.rows(qi), _lanes(g)] = blk.pair(acc_scr[g], acc_scr[g + 1])
                st_ref[blk.rows(qi), _lanes(g)] = blk.pair(c_scr[g], c_scr[g + 1])

        _causal_blocks(nq, prep, init, [scores, logs, suffix, weights, values], finish, descending=True)

    out = jax.ShapeDtypeStruct((T, DI), F32)
    return _call(
        body, name=name, grid=(B, H // HEADS_PER_STEP), in_specs=[_cols(S), _cols(S, nb), _cols(S, 2 * nb)],
        out_specs=[_cols(S), _cols(S)], out_shape=[out, out],
        scratch_shapes=[pltpu.VMEM((HEADS_PER_STEP, tq, GROUP), F32), pltpu.VMEM((HEADS_PER_STEP, tq, 1), F32)],
        compiler_params=_params(),
    )(proj, proj, proj)


def _sb_bwd(proj, do, stat, B, tq, name):
    T, DI = do.shape
    S = T // B
    H = DI // HEAD_DIM
    nq = S // tq
    nb = DI // _step_width()

    def body(q_ref, k_ref, v_ref, do_ref, st_ref, dqkv_ref, dk_acc, dv_acc, dq_scr):
        blk = _Block(tq)
        strict = blk.ci < blk.ri
        upto = jnp.where(blk.ri <= blk.ci, 1.0, 0.0).astype(BF16)
        before = jnp.where(blk.ri < blk.ci, 1.0, 0.0).astype(BF16)
        dk_acc[...] = jnp.zeros_like(dk_acc)
        dv_acc[...] = jnp.zeros_like(dv_acc)

        def prep(g, qi):
            dq_scr[g] = jnp.zeros((tq, GROUP), F32)
            return (blk.own(g, q_ref[blk.rows(qi), _lanes(g)]), blk.own(g, do_ref[blk.rows(qi), _lanes(g)]),
                    blk.stat(g, st_ref[blk.rows(qi), _lanes(g)]))

        def init():
            return jnp.zeros((tq, 1), F32), jnp.zeros((tq, 1), F32)

        def scores(g, ctx, kj, masked, st):
            return st + (_dot(ctx[0], k_ref[blk.rows(kj), _lanes(g)], NT),
                         _dot(ctx[1], v_ref[blk.rows(kj), _lanes(g)], NT))

        def logs(g, ctx, kj, masked, st):
            cpre, pg, z, da = st
            lk, e = _log2_keep(z)
            inv = 1.0 / (1.0 + e)
            sig = jnp.where(z >= 0.0, inv, e * inv)
            lbt = (z + lk) + (ctx[2] - cpre)
            if masked:
                lk = jnp.where(strict, lk, 0.0)
            return (cpre + jnp.sum(lk, axis=1, keepdims=True), pg, da, lbt, sig) + _hi_lo(lk)

        def prefix(g, ctx, kj, masked, st):
            cpre, pg, da, lbt, sig, hi, lo = st
            return cpre, pg, da, lbt, sig, _dot(hi, upto) + _dot(lo, upto)

        def weights(g, ctx, kj, masked, st):
            cpre, pg, da, lbt, sig, pre = st
            a = jnp.exp2(lbt - pre)
            if masked:
                a = jnp.where(strict, a, 0.0)
            gr = da * a
            return cpre, pg, sig, a.astype(BF16), gr, gr.astype(BF16)

        def grad_prefix(g, ctx, kj, masked, st):
            cpre, pg, sig, ab, gr, gb = st
            return cpre, pg, sig, ab, gr, _dot(gb, before)

        def dlogits(g, ctx, kj, masked, st):
            cpre, pg, sig, ab, gr, pfx = st
            dz = gr * (1.0 - sig) - (pfx + pg) * sig
            if masked:
                dz = jnp.where(strict, dz, 0.0)
            return cpre, pg + jnp.sum(gr, axis=1, keepdims=True), ab, dz.astype(BF16)

        def combine(kj, ctx, st):
            for g in range(0, HEADS_PER_STEP, 2):
                dv_acc[blk.rows(kj), _lanes(g)] += _dot(st[g][2], ctx[g][1], TN) + _dot(st[g + 1][2], ctx[g + 1][1], TN)
                dk_acc[blk.rows(kj), _lanes(g)] += _dot(st[g][3], ctx[g][0], TN) + _dot(st[g + 1][3], ctx[g + 1][0], TN)

        def queries(g, ctx, kj, masked, st):
            cpre, pg, _, dzb = st
            dq_scr[g] += _dot(dzb, blk.own(g, k_ref[blk.rows(kj), _lanes(g)]))
            return cpre, pg

        def finish(qi, ctx, carry):
            for g in range(0, HEADS_PER_STEP, 2):
                dqkv_ref[0, blk.rows(qi), _lanes(g)] = ((dq_scr[g] + dq_scr[g + 1]) * LN2).astype(BF16)

        _causal_blocks(nq, prep, init, [scores, logs, prefix, weights, grad_prefix, dlogits, queries], finish,
                       combine=combine)
        dqkv_ref[1] = (dk_acc[...] * LN2).astype(BF16)
        dqkv_ref[2] = dv_acc[...].astype(BF16)

    W = _step_width()
    return _call(
        body, name=name, grid=(B, H // HEADS_PER_STEP),
        in_specs=[_cols(S), _cols(S, nb), _cols(S, 2 * nb), _cols(S), _cols(S)],
        out_specs=pl.BlockSpec((3, S, W), lambda b, h: (0, b, h)),
        out_shape=jax.ShapeDtypeStruct((3, T, DI), BF16),
        scratch_shapes=[pltpu.VMEM((S, W), F32), pltpu.VMEM((S, W), F32), pltpu.VMEM((HEADS_PER_STEP, tq, GROUP), F32)],
        compiler_params=_params(),
    )(proj, proj, proj, do, stat)


def _row_tile(R, C, n_arrays):
    budget = 24 * 1024 * 1024 // (2 * n_arrays * 4 * max(C, LANES))
    return _tile(R, max(8, budget), 8)


def _ew_sum(parts, name, also_bf16=False):
    R, C = parts[0].shape
    tr = _row_tile(R, C, len(parts) + 2)
    n = len(parts)

    def body(*refs):
        acc = refs[0][...].astype(F32) + refs[1][...].astype(F32)
        for r in refs[2:n]:
            acc = acc + r[...].astype(F32)
        refs[n][...] = acc
        if also_bf16:
            refs[n + 1][...] = acc.astype(BF16)

    blk = pl.BlockSpec((tr, C), lambda i: (i, 0))
    out_shape = [jax.ShapeDtypeStruct((R, C), F32)] + ([jax.ShapeDtypeStruct((R, C), BF16)] if also_bf16 else [])
    return _call(
        body, name=name, grid=(R // tr,), in_specs=[blk] * n, out_specs=[blk] * len(out_shape),
        out_shape=out_shape, compiler_params=_params(),
    )(*parts)


def _adamw(w, g, m, v, name):
    R, C = w.shape
    tr = _row_tile(R, C, 7)
    c1 = 1.0 / (1.0 - ADAM_B1 ** ADAM_STEP)
    c2 = 1.0 / (1.0 - ADAM_B2 ** ADAM_STEP)

    def body(w_ref, g_ref, m_ref, v_ref, d_ref, m2_ref, v2_ref):
        gv = g_ref[...]
        m2 = ADAM_B1 * m_ref[...] + (1.0 - ADAM_B1) * gv
        v2 = ADAM_B2 * v_ref[...] + (1.0 - ADAM_B2) * (gv * gv)
        m2_ref[...] = m2
        v2_ref[...] = v2
        d_ref[...] = -ADAM_LR * ((m2 * c1) / (jnp.sqrt(v2 * c2) + ADAM_EPS) + ADAM_WD * w_ref[...])

    blk = pl.BlockSpec((tr, C), lambda i: (i, 0))
    out = jax.ShapeDtypeStruct((R, C), F32)
    return _call(
        body, name=name, grid=(R // tr,), in_specs=[blk] * 4, out_specs=[blk] * 3, out_shape=[out] * 3,
        compiler_params=_params(),
    )(w, g, m, v)


def _me():
    return lax.axis_index("x"), lax.axis_index("y"), lax.axis_index("c")


def _chip_of(x, y):
    return 2 * x + y


def _other_chips(x, y):
    return [(x, 1 - y), (1 - x, y), (1 - x, 1 - y)]


def _gather_weights(halves, smalls):
    nh, ns = len(halves), len(smalls)

    def body(*refs):
        ins_h, ins_s = refs[:nh], refs[nh:nh + ns]
        outs_h, outs_s = refs[nh + ns:2 * nh + ns], refs[2 * nh + ns:2 * (nh + ns)]
        send1, recv1, send2, recv2, send3, recv3 = refs[2 * (nh + ns):]
        x, y, c = _me()
        mine = _chip_of(x, y)
        chips = _other_chips(x, y)
        sib = (x, y, 1 - c)

        def landed(i, k, half):
            return outs_h[i].at[_chip_of(*chips[k]), half]

        def first(i, k):
            return pltpu.make_async_remote_copy(
                src_ref=ins_h[i].at[c], dst_ref=outs_h[i].at[mine, c], send_sem=send1.at[i, k], recv_sem=recv1.at[i, k],
                device_id=(*chips[k], c), device_id_type=MESH)

        def passed(i, k):
            return pltpu.make_async_remote_copy(
                src_ref=landed(i, k, c), dst_ref=landed(i, k, c), send_sem=send2.at[i, k], recv_sem=recv2.at[i, k],
                device_id=sib, device_id_type=MESH)

        def small(i, k):
            return pltpu.make_async_remote_copy(
                src_ref=ins_s[i], dst_ref=outs_s[i].at[mine], send_sem=send3.at[i, k], recv_sem=recv3.at[i, k],
                device_id=(*chips[k], c), device_id_type=MESH)

        for i in range(nh):
            for k in range(3):
                first(i, k).start()
        for i in range(ns):
            for k in range(3):
                small(i, k).start()
        for i in range(nh):
            for k in range(3):
                pltpu.make_async_remote_copy(
                    src_ref=ins_h[i].at[c], dst_ref=landed(i, k, c), send_sem=send1.at[i, k], recv_sem=recv1.at[i, k],
                    device_id=(*chips[k], c), device_id_type=MESH).wait_recv()
                passed(i, k).start()
        for i in range(nh):
            for k in range(3):
                pltpu.make_async_remote_copy(
                    src_ref=landed(i, k, c), dst_ref=landed(i, k, 1 - c), send_sem=send2.at[i, k],
                    recv_sem=recv2.at[i, k], device_id=sib, device_id_type=MESH).wait_recv()
        for i in range(ns):
            for k in range(3):
                pltpu.make_async_remote_copy(
                    src_ref=ins_s[i], dst_ref=outs_s[i].at[_chip_of(*chips[k])], send_sem=send3.at[i, k],
                    recv_sem=recv3.at[i, k], device_id=(*chips[k], c), device_id_type=MESH).wait_recv()
        for i in range(nh):
            for k in range(3):
                first(i, k).wait_send()
                passed(i, k).wait_send()
        for i in range(ns):
            for k in range(3):
                small(i, k).wait_send()

    out_shape = ([jax.ShapeDtypeStruct((4,) + a.shape, a.dtype) for a in halves]
                 + [jax.ShapeDtypeStruct((4,) + a.shape, a.dtype) for a in smalls])
    n = nh + ns
    res = _call(
        body, name="gather_weights", in_specs=[HBM] * n, out_specs=[HBM] * n, out_shape=out_shape,
        scratch_shapes=[pltpu.SemaphoreType.DMA((nh, 3)), pltpu.SemaphoreType.DMA((nh, 3)),
                        pltpu.SemaphoreType.DMA((nh, 3)), pltpu.SemaphoreType.DMA((nh, 3)),
                        pltpu.SemaphoreType.DMA((max(ns, 1), 3)), pltpu.SemaphoreType.DMA((max(ns, 1), 3))],
        compiler_params=_params(),
    )(*halves, *smalls)
    return res[:nh], res[nh:]


def _pair_exchange(grads):
    n = len(grads)

    def body(*refs):
        ins, got = refs[:n], refs[n:2 * n]
        send, recv = refs[2 * n:]
        x, y, c = _me()
        cps = []
        for i in range(n):
            for j in range(4):
                r = pltpu.make_async_remote_copy(
                    src_ref=ins[i].at[j, 1 - c], dst_ref=got[i].at[j], send_sem=send.at[i, j], recv_sem=recv.at[i, j],
                    device_id=(x, y, 1 - c), device_id_type=MESH)
                r.start()
                cps.append(r)
        for r in cps:
            r.wait()

    return _call(
        body, name="grad_pair_exchange", in_specs=[HBM] * n, out_specs=[HBM] * n,
        out_shape=[jax.ShapeDtypeStruct((4,) + g.shape[2:], g.dtype) for g in grads],
        scratch_shapes=[pltpu.SemaphoreType.DMA((n, 4)), pltpu.SemaphoreType.DMA((n, 4))],
        compiler_params=_params(),
    )(*grads)


def _chip_exchange(sums):
    n = len(sums)

    def body(*refs):
        ins, got = refs[:n], refs[n:2 * n]
        send, recv = refs[2 * n:]
        x, y, c = _me()
        chips = _other_chips(x, y)
        cps = []
        for i in range(n):
            for k in range(3):
                r = pltpu.make_async_remote_copy(
                    src_ref=ins[i].at[_chip_of(*chips[k])], dst_ref=got[i].at[k], send_sem=send.at[i, k],
                    recv_sem=recv.at[i, k], device_id=(*chips[k], c), device_id_type=MESH)
                r.start()
                cps.append(r)
        for r in cps:
            r.wait()

    return _call(
        body, name="grad_chip_exchange", in_specs=[HBM] * n, out_specs=[HBM] * n,
        out_shape=[jax.ShapeDtypeStruct((3,) + s.shape[1:], s.dtype) for s in sums],
        scratch_shapes=[pltpu.SemaphoreType.DMA((n, 3)), pltpu.SemaphoreType.DMA((n, 3))],
        compiler_params=_params(),
    )(*sums)


def _pair_share(halves):
    n = len(halves)

    def body(*refs):
        ins, outs = refs[:n], refs[n:2 * n]
        send, recv = refs[2 * n:]
        x, y, c = _me()
        cps = []
        for i in range(n):
            r = pltpu.make_async_remote_copy(
                src_ref=ins[i], dst_ref=outs[i], send_sem=send.at[i], recv_sem=recv.at[i],
                device_id=(x, y, 1 - c), device_id_type=MESH)
            r.start()
            cps.append(r)
        for r in cps:
            r.wait()

    return _call(
        body, name="grad_pair_share", in_specs=[HBM] * n, out_specs=[HBM] * n,
        out_shape=[jax.ShapeDtypeStruct(h.shape, h.dtype) for h in halves],
        scratch_shapes=[pltpu.SemaphoreType.DMA((n,)), pltpu.SemaphoreType.DMA((n,))],
        compiler_params=_params(),
    )(*halves)


def _allreduce_small(vec):
    P = vec.shape[1]

    def body(v_ref, sum_ref, all_ref, send, recv):
        x, y, c = _me()
        me = 4 * x + 2 * y + c
        all_ref[pl.ds(me, 1)] = v_ref[...][None]
        cps = []
        for d in range(1, 8):
            peer = (jnp.bitwise_xor(x, d >> 2), jnp.bitwise_xor(y, (d >> 1) & 1), jnp.bitwise_xor(c, d & 1))
            r = pltpu.make_async_remote_copy(
                src_ref=v_ref, dst_ref=all_ref.at[me], send_sem=send.at[d - 1], recv_sem=recv.at[d - 1],
                device_id=peer, device_id_type=MESH)
            r.start()
            cps.append(r)
        for d in range(1, 8):
            src = jnp.bitwise_xor(me, d)
            pltpu.make_async_remote_copy(
                src_ref=v_ref, dst_ref=all_ref.at[src], send_sem=send.at[d - 1], recv_sem=recv.at[d - 1],
                device_id=(x, y, c), device_id_type=MESH).wait_recv()
        for r in cps:
            r.wait_send()
        acc = all_ref[0]
        for i in range(1, 8):
            acc = acc + all_ref[i]
        sum_ref[...] = acc

    vm = pl.BlockSpec(memory_space=pltpu.VMEM)
    return _call(
        body, name="allreduce_small", in_specs=[vm], out_specs=[vm, vm],
        out_shape=[jax.ShapeDtypeStruct((8, P), F32), jax.ShapeDtypeStruct((8, 8, P), F32)],
        scratch_shapes=[pltpu.SemaphoreType.DMA((7,)), pltpu.SemaphoreType.DMA((7,))],
        compiler_params=_params(),
    )(vec)[0]


def _per_batch(mod, B, D):
    return [mod[:B, i * D:(i + 1) * D].reshape(B, 1, D) for i in range(3)]


def _pad_rows8(a):
    return jnp.concatenate([a, jnp.zeros((8 - a.shape[0],) + a.shape[1:], a.dtype)], axis=0)


def _layer_fwd(x, c8, w, S, fox, tag):
    T, D = x.shape
    B = T // S
    DI = w["w_out"].shape[0]
    H = DI // HEAD_DIM
    tq = _tile(S, ATT_BLOCK, 8)
    mod = _mod_fwd(c8, w["w_ada"], w["b_ada"], tag + "_mod_fwd")
    shift, scale, gate = _per_batch(mod, B, D)
    proj, h = _ln_proj(x, shift, scale, w["norm_g"], w["w_in"], S, tag + "_ln_proj")
    saved = dict(x=x, h=h, proj=proj, scale=scale, gate=gate)
    if fox:
        fl = _mm(h, w["w_f"], "nn", F32, tag + "_flogit").reshape(B, S, LANES)
        cum = _cum_fwd(fl, w["b_f"], tag + "_cum_fwd")
        cumrow = cum[:, :, :H].transpose(0, 2, 1).reshape(B, H, S // tq, 1, tq)
        o, stat = _fox_fwd(proj, cum, cumrow, tag + "_attn_fwd")
        saved.update(fl=fl, cum=cum, cumrow=cumrow)
    else:
        o, stat = _sb_fwd(proj, B, tq, tag + "_attn_fwd")
    xo, y, u = _gate_out(o, proj, w["w_out"], x, gate, S, tag + "_gate_out")
    saved.update(o=o, stat=stat, y=y, u=u)
    return xo, saved


def _layer_bwd(dxo, sv, w, cT, S, fox, tag):
    T, D = dxo.shape
    B = T // S
    DI = w["w_out"].shape[0]
    H = DI // HEAD_DIM
    tq = _tile(S, ATT_BLOCK, 8)
    dy, do, dzg, dgate = _out_bwd(dxo, sv["y"], sv["gate"], w["w_out"], sv["o"], sv["proj"], S, tag + "_out_bwd")
    g = {"w_out": _mm(sv["u"], dy, "tn", F32, tag + "_dw_out", tm=1024, tn=1024, tk=2048)}
    if fox:
        dqkv, dcs = _fox_bwd(sv["proj"], do, sv["o"], sv["stat"], sv["cum"], sv["cumrow"], tag + "_attn_bwd")
        dcs = dcs.reshape(B, H, S).transpose(0, 2, 1)
        dcs = jnp.concatenate([dcs, jnp.zeros((B, S, LANES - H), F32)], axis=-1)
        dfl, db_f = _cum_bwd(dcs, sv["fl"], w["b_f"], tag + "_cum_bwd")
        g["b_f"] = db_f[:, :H]
        tail = [dfl.reshape(T, LANES).astype(BF16)]
        w_in = jnp.concatenate([w["w_in"], w["w_f"]], axis=1)
    else:
        dqkv = _sb_bwd(sv["proj"], do, sv["stat"], B, tq, tag + "_attn_bwd")
        tail = []
        w_in = w["w_in"]
    dproj = jnp.concatenate([dqkv[0], dqkv[1], dqkv[2], dzg] + tail, axis=1)
    N = dproj.shape[1]
    q_cols = jnp.where(jnp.arange(N)[None, :] < DI, Q_SCALE, 1.0).astype(F32)
    dw_in = _mm(sv["h"], dproj, "tn", F32, tag + "_dw_in", tm=1024, tn=1664 if N % 1664 == 0 else 2048, tk=1024,
                col_scale=q_cols)
    g["w_in"] = dw_in[:, :4 * DI + H] if fox else dw_in
    dh = _mm(dproj, w_in, "nt", F32, tag + "_dh", tm=2048, tn=1024, tk=832 if N % 832 == 0 else 1024)
    dx, dshift, dscale, dg = _ln_bwd(dh, sv["x"], dxo, sv["scale"], w["norm_g"], S, tag + "_ln_bwd")
    g["norm_g"] = dg
    dmod = jnp.concatenate([dshift, dscale, dgate], axis=-1).reshape(B, 3 * D)
    g["w_ada"], g["b_ada"] = _mod_bwd(cT, _pad_rows8(dmod), B, tag + "_mod_bwd")
    return dx, g


def _local_step(x3, c, tgt3, wf, ws, final_g):
    B, S, D = x3.shape
    T = B * S
    x = x3.reshape(T, D)
    c8 = _pad_rows8(c)
    cT = c8.T
    x1, sv1 = _layer_fwd(x, c8, wf, S, True, "fox")
    x2, sv2 = _layer_fwd(x1, c8, ws, S, False, "sb")
    dx2, dgf, loss = _final_loss(x2, tgt3.reshape(T, D), final_g, S, "final_loss")
    dx1, gs = _layer_bwd(dx2, sv2, ws, cT, S, False, "sb")
    dx0, gf = _layer_bwd(dx1, sv1, wf, cT, S, True, "fox")
    return loss, dx0.reshape(B, S, D), gf, gs, dgf


def _cols_to_shards(a):
    R, C4 = a.shape
    return a.reshape(R, 4, C4 // 4).transpose(1, 0, 2)


def _shards_to_cols(a):
    n, R, C = a.shape
    return a.transpose(1, 0, 2).reshape(R, n * C)


def kernel(x, c, fox_norm_g, fox_w_ada, fox_b_ada, fox_w_in, fox_b_f, fox_w_out, sb_norm_g, sb_w_ada, sb_b_ada, sb_w_in, sb_w_out, final_norm_g, loss_target, m_fox_norm_g, m_fox_w_ada, m_fox_b_ada, m_fox_w_in, m_fox_b_f, m_fox_w_out, m_sb_norm_g, m_sb_w_ada, m_sb_b_ada, m_sb_w_in, m_sb_w_out, m_final_norm_g, v_fox_norm_g, v_fox_w_ada, v_fox_b_ada, v_fox_w_in, v_fox_b_f, v_fox_w_out, v_sb_norm_g, v_sb_w_ada, v_sb_b_ada, v_sb_w_in, v_sb_w_out, v_final_norm_g):
    B, S, D = x.shape
    DI = 4 * fox_w_out.shape[1]
    H = DI // HEAD_DIM
    chip = _chip_of(lax.axis_index("x"), lax.axis_index("y"))

    big_names = ["fox_w_ada", "fox_w_in", "fox_w_out", "sb_w_ada", "sb_w_in", "sb_w_out"]
    big = dict(fox_w_ada=fox_w_ada[0], fox_w_in=fox_w_in[0], fox_w_out=fox_w_out[0],
               sb_w_ada=sb_w_ada[0], sb_w_in=sb_w_in[0], sb_w_out=sb_w_out[0])
    for n in ("fox_w_in", "sb_w_in"):
        width = big[n].shape[1]
        is_q = chip * width + jnp.arange(width)[None, :] < DI
        big[n] = big[n] * jnp.where(is_q, Q_SCALE, 1.0).astype(F32)
    halves = [big[n].astype(BF16).reshape(2, big[n].shape[0] // 2, big[n].shape[1]) for n in big_names]
    gathered, gsmall = _gather_weights(halves, [sb_norm_g, sb_b_ada])
    gathered = [lax.dynamic_update_index_in_dim(a, own, chip, 0) for a, own in zip(gathered, halves)]
    gsmall = [lax.dynamic_update_index_in_dim(a, own, chip, 0) for a, own in zip(gsmall, [sb_norm_g, sb_b_ada])]
    full = {}
    for n, a in zip(big_names, gathered):
        a = a.reshape(4, a.shape[1] * a.shape[2], a.shape[3])
        full[n] = a.reshape(4 * a.shape[1], a.shape[2]) if n.endswith("w_out") else _shards_to_cols(a)
    sb_norm_full = gsmall[0].reshape(1, D)
    sb_b_ada_full = gsmall[1].reshape(1, 3 * D)
    w_f = jnp.concatenate([full["fox_w_in"][:, 4 * DI:], jnp.zeros((D, LANES - H), BF16)], axis=1)
    b_f = jnp.concatenate([fox_b_f, jnp.zeros((1, LANES - H), F32)], axis=1)
    wf = dict(w_ada=full["fox_w_ada"], b_ada=fox_b_ada, norm_g=fox_norm_g, w_in=full["fox_w_in"][:, :4 * DI],
              w_f=w_f, b_f=b_f, w_out=full["fox_w_out"])
    ws = dict(w_ada=full["sb_w_ada"], b_ada=sb_b_ada_full, norm_g=sb_norm_full, w_in=full["sb_w_in"],
              w_out=full["sb_w_out"])

    loss, grad_x, gf, gs, dgf = _local_step(x, c, loss_target, wf, ws, final_norm_g.reshape(1, D))

    part = dict(fox_w_ada=gf["w_ada"], fox_w_in=gf["w_in"], fox_w_out=gf["w_out"],
                sb_w_ada=gs["w_ada"], sb_w_in=gs["w_in"], sb_w_out=gs["w_out"])
    shard_major = []
    for n in big_names:
        a = part[n]
        a = a.reshape(4, a.shape[0] // 4, a.shape[1]) if n.endswith("w_out") else _cols_to_shards(a)
        shard_major.append(a.reshape(4, 2, a.shape[1] // 2, a.shape[2]))
    core = lax.axis_index("c")
    got = _pair_exchange(shard_major)
    pair_f32, pair_bf16 = [], []
    for n, g4, b in zip(big_names, shard_major, got):
        a = lax.dynamic_index_in_dim(g4, core, axis=1, keepdims=False)
        r, C = a.shape[1:]
        s32, s16 = _ew_sum([a.reshape(4 * r, C), b.reshape(4 * r, C)], n + "_pair_sum", also_bf16=True)
        pair_f32.append(s32.reshape(4, r, C))
        pair_bf16.append(s16.reshape(4, r, C))
    others = _chip_exchange(pair_bf16)
    reduced_halves = [_ew_sum([lax.dynamic_index_in_dim(a, chip, axis=0, keepdims=False), b[0], b[1], b[2]],
                              n + "_chip_sum")[0] for n, a, b in zip(big_names, pair_f32, others)]
    theirs = _pair_share(reduced_halves)
    grad_big = {}
    for n, a, b in zip(big_names, reduced_halves, theirs):
        grad_big[n] = jnp.concatenate([jnp.where(core == 0, a, b), jnp.where(core == 0, b, a)], axis=0)

    pieces = [loss, gf["norm_g"], gf["b_ada"], jnp.concatenate([gf["b_f"], jnp.zeros((1, LANES - H), F32)], axis=1),
              gs["norm_g"], gs["b_ada"], dgf]
    vec = jnp.concatenate(pieces, axis=1)
    red = _allreduce_small(_pad_rows8(vec))[0:1]
    offs = [0]
    for p in pieces:
        offs.append(offs[-1] + p.shape[1])
    r_loss, r_fng, r_fba, r_fbf, r_sng, r_sba, r_fin = [red[:, offs[i]:offs[i + 1]] for i in range(7)]
    small_grads = dict(
        fox_norm_g=r_fng, fox_b_ada=r_fba, fox_b_f=r_fbf[:, :H],
        sb_norm_g=lax.dynamic_slice_in_dim(r_sng, chip * (D // 4), D // 4, axis=1),
        sb_b_ada=lax.dynamic_slice_in_dim(r_sba, chip * (3 * D // 4), 3 * D // 4, axis=1),
        final_norm_g=r_fin)

    weights = dict(fox_norm_g=fox_norm_g, fox_w_ada=fox_w_ada, fox_b_ada=fox_b_ada, fox_w_in=fox_w_in, fox_b_f=fox_b_f,
                   fox_w_out=fox_w_out, sb_norm_g=sb_norm_g, sb_w_ada=sb_w_ada, sb_b_ada=sb_b_ada, sb_w_in=sb_w_in,
                   sb_w_out=sb_w_out, final_norm_g=final_norm_g)
    ms = dict(fox_norm_g=m_fox_norm_g, fox_w_ada=m_fox_w_ada, fox_b_ada=m_fox_b_ada, fox_w_in=m_fox_w_in,
              fox_b_f=m_fox_b_f, fox_w_out=m_fox_w_out, sb_norm_g=m_sb_norm_g, sb_w_ada=m_sb_w_ada,
              sb_b_ada=m_sb_b_ada, sb_w_in=m_sb_w_in, sb_w_out=m_sb_w_out, final_norm_g=m_final_norm_g)
    vs = dict(fox_norm_g=v_fox_norm_g, fox_w_ada=v_fox_w_ada, fox_b_ada=v_fox_b_ada, fox_w_in=v_fox_w_in,
              fox_b_f=v_fox_b_f, fox_w_out=v_fox_w_out, sb_norm_g=v_sb_norm_g, sb_w_ada=v_sb_w_ada,
              sb_b_ada=v_sb_b_ada, sb_w_in=v_sb_w_in, sb_w_out=v_sb_w_out, final_norm_g=v_final_norm_g)
    order = ["fox_norm_g", "fox_w_ada", "fox_b_ada", "fox_w_in", "fox_b_f", "fox_w_out", "sb_norm_g", "sb_w_ada",
             "sb_b_ada", "sb_w_in", "sb_w_out", "final_norm_g"]
    grads, deltas, new_m, new_v = {}, {}, {}, {}
    for n in big_names:
        shp = weights[n].shape
        g2 = grad_big[n]
        d, m2, v2 = _adamw(weights[n][0], g2, ms[n][0], vs[n][0], n + "_adamw")
        grads[n], deltas[n], new_m[n], new_v[n] = g2.reshape(shp), d.reshape(shp), m2.reshape(shp), v2.reshape(shp)
    small_names = [n for n in order if n not in big_names]
    sizes = [small_grads[n].shape[1] for n in small_names]
    total = sum(sizes)
    padn = (-total) % LANES

    def pack(d):
        return jnp.concatenate([d[n].reshape(1, -1) for n in small_names] + [jnp.ones((1, padn), F32)], axis=1)

    sd, sm, sv_ = _adamw(pack(weights), pack(small_grads), pack(ms), pack(vs), "small_adamw")
    o = 0
    for n, sz in zip(small_names, sizes):
        shp = weights[n].shape
        grads[n] = small_grads[n].reshape(shp)
        deltas[n], new_m[n], new_v[n] = (t[:, o:o + sz].reshape(shp) for t in (sd, sm, sv_))
        o += sz
    return (r_loss[0, 0], grad_x, *[grads[n] for n in order], *[deltas[n] for n in order],
            *[new_m[n] for n in order], *[new_v[n] for n in order])
```

```python
import functools

import jax
import jax.numpy as jnp
from jax import lax
from jax.experimental import pallas as pl
from jax.experimental.pallas import tpu as pltpu

F32 = jnp.float32
BF16 = jnp.bfloat16
HEAD_DIM = 64
LOG2E = 1.4426950408889634
LN2 = 0.6931471805599453
Q_SCALE = HEAD_DIM ** -0.5 * LOG2E
LANES = 128
NORM_EPS = 1e-6
ADAM_LR = 0.001
ADAM_B1 = 0.9
ADAM_B2 = 0.999
ADAM_EPS = 1e-08
ADAM_WD = 0.01
ADAM_STEP = 10
VMEM_LIMIT = 56 * 1024 * 1024
ATT_BLOCK = 256
MESH = pl.DeviceIdType.MESH
HBM = pl.BlockSpec(memory_space=pltpu.HBM)
NT = (((1,), (1,)), ((), ()))
TN = (((0,), (0,)), ((), ()))


def _call(body, **kw):
    return pl.pallas_call(body, **kw)


def _params(**kw):
    return pltpu.CompilerParams(vmem_limit_bytes=VMEM_LIMIT, **kw)


def _tile(dim, pref, mult=128):
    if dim <= pref:
        return dim
    t = (pref // mult) * mult
    while t >= mult:
        if dim % t == 0:
            return t
        t -= mult
    return dim


def _sigmoid(x):
    return 1.0 / (1.0 + jnp.exp(-x))


def _split3(x):
    hi = x.astype(BF16)
    r = x - hi.astype(F32)
    mid = r.astype(BF16)
    lo = (r - mid.astype(F32)).astype(BF16)
    return hi, mid, lo


def _mm(a, b, mode, out_dtype, name, tm=512, tn=512, tk=512, col_scale=None):
    a_slabs = a.shape[0] if a.ndim == 3 else 0
    b_slabs = b.shape[0] if b.ndim == 3 else 0
    if mode == "nn":
        (M, K), (_, N) = a.shape, b.shape
    elif mode == "nt":
        M, K = (a.shape[1], a_slabs * a.shape[2]) if a_slabs else a.shape
        N = b.shape[0]
    else:
        K, M = a.shape
        N = b_slabs * b.shape[2] if b_slabs else b.shape[1]
    tm, tn, tk = _tile(M, tm), _tile(N, tn), _tile(K, tk)
    if a_slabs:
        tk = _tile(a.shape[2], tk)
    if b_slabs:
        tn = _tile(b.shape[2], tn)
    nk = K // tk
    dims = {"nn": (((1,), (0,)), ((), ())), "nt": NT, "tn": TN}[mode]

    def body(a_ref, b_ref, *rest):
        o_ref, acc_ref = rest[-2:]
        k = pl.program_id(2)

        @pl.when(k == 0)
        def _():
            acc_ref[...] = jnp.zeros_like(acc_ref)

        acc_ref[...] += lax.dot_general(a_ref[...], b_ref[...], dims, preferred_element_type=F32)

        @pl.when(k == nk - 1)
        def _():
            acc = acc_ref[...]
            if col_scale is not None:
                acc = acc * rest[0][...]
            o_ref[...] = acc.astype(out_dtype)

    if a_slabs:
        per = a.shape[2] // tk
        a_spec = pl.BlockSpec((None, tm, tk), lambda i, j, k: (k // per, i, k % per))
    elif mode == "tn":
        a_spec = pl.BlockSpec((tk, tm), lambda i, j, k: (k, i))
    else:
        a_spec = pl.BlockSpec((tm, tk), lambda i, j, k: (i, k))
    if b_slabs:
        per_b = b.shape[2] // tn
        b_spec = pl.BlockSpec((None, tk, tn), lambda i, j, k: (j // per_b, k, j % per_b))
    elif mode == "nt":
        b_spec = pl.BlockSpec((tn, tk), lambda i, j, k: (j, k))
    else:
        b_spec = pl.BlockSpec((tk, tn), lambda i, j, k: (k, j))
    extra_specs = [] if col_scale is None else [pl.BlockSpec((1, tn), lambda i, j, k: (0, j))]
    extra = [] if col_scale is None else [col_scale]
    return _call(
        body, name=name, grid=(M // tm, N // tn, nk),
        in_specs=[a_spec, b_spec] + extra_specs, out_specs=pl.BlockSpec((tm, tn), lambda i, j, k: (i, j)),
        out_shape=jax.ShapeDtypeStruct((M, N), out_dtype),
        scratch_shapes=[pltpu.VMEM((tm, tn), F32)], compiler_params=_params(),
    )(a, b, *extra)


def _mod_fwd(c8, w_ada, b_ada, name):
    D, N = w_ada.shape
    tn = _tile(N, 512)

    def body(c_ref, w_ref, b_ref, o_ref):
        c = c_ref[...]
        sc = (c * _sigmoid(c)).astype(BF16)
        o_ref[...] = jnp.dot(sc, w_ref[...], preferred_element_type=F32) + b_ref[...]

    return _call(
        body, name=name, grid=(N // tn,),
        in_specs=[pl.BlockSpec((8, D), lambda j: (0, 0)), pl.BlockSpec((D, tn), lambda j: (0, j)),
                  pl.BlockSpec((1, tn), lambda j: (0, j))],
        out_specs=pl.BlockSpec((8, tn), lambda j: (0, j)),
        out_shape=jax.ShapeDtypeStruct((8, N), F32), compiler_params=_params(),
    )(c8, w_ada, b_ada)


def _mod_bwd(cT, dmod8, nb, name):
    D = cT.shape[0]
    N = dmod8.shape[1]
    tn = _tile(N, 512)

    def body(c_ref, d_ref, w_ref, b_ref):
        c = c_ref[...]
        sc = c * _sigmoid(c)
        d = d_ref[...]
        acc = sc[:, 0:1] * d[0:1, :]
        bsum = d[0:1, :]
        for b in range(1, nb):
            acc = acc + sc[:, b:b + 1] * d[b:b + 1, :]
            bsum = bsum + d[b:b + 1, :]
        w_ref[...] = acc
        b_ref[...] = bsum

    return _call(
        body, name=name, grid=(N // tn,),
        in_specs=[pl.BlockSpec((D, 8), lambda j: (0, 0)), pl.BlockSpec((8, tn), lambda j: (0, j))],
        out_specs=[pl.BlockSpec((D, tn), lambda j: (0, j)), pl.BlockSpec((1, tn), lambda j: (0, j))],
        out_shape=[jax.ShapeDtypeStruct((D, N), F32), jax.ShapeDtypeStruct((1, N), F32)],
        compiler_params=_params(),
    )(cT, dmod8)


def _ln_proj(x, shift, scale, g, w, S, name):
    T, D = x.shape
    N = w.shape[1]
    tm = _tile(S, 2048)
    tn = _tile(N, 1024)
    per_b = S // tm

    def body(x_ref, sh_ref, sc_ref, g_ref, w_ref, p_ref, h_ref):
        @pl.when(pl.program_id(1) == 0)
        def _():
            xv = x_ref[...]
            r = lax.rsqrt(jnp.mean(xv * xv, axis=-1, keepdims=True) + NORM_EPS)
            h = (xv * r) * g_ref[...] * (1.0 + sc_ref[0]) + sh_ref[0]
            h_ref[...] = h.astype(BF16)

        p_ref[...] = jnp.dot(h_ref[...], w_ref[...], preferred_element_type=F32).astype(BF16)

    return _call(
        body, name=name, grid=(T // tm, N // tn),
        in_specs=[pl.BlockSpec((tm, D), lambda i, j: (i, 0)),
                  pl.BlockSpec((1, 1, D), lambda i, j: (i // per_b, 0, 0)),
                  pl.BlockSpec((1, 1, D), lambda i, j: (i // per_b, 0, 0)),
                  pl.BlockSpec((1, D), lambda i, j: (0, 0)),
                  pl.BlockSpec((D, tn), lambda i, j: (0, j))],
        out_specs=[pl.BlockSpec((tm, tn), lambda i, j: (i, j)), pl.BlockSpec((tm, D), lambda i, j: (i, 0))],
        out_shape=[jax.ShapeDtypeStruct((T, N), BF16), jax.ShapeDtypeStruct((T, D), BF16)],
        compiler_params=_params(),
    )(x, shift, scale, g, w)


def _ln_bwd(dhs, x, dxo, scale, g, S, name):
    T, D = x.shape
    B = T // S
    tm = _tile(S, 512)
    per_b = S // tm

    nd = len(dhs)

    def body(*refs):
        x_ref, dxo_ref, sc_ref, g_ref, dx_ref, dsh_ref, dsc_ref, dg_ref = refs[nd:]
        i = pl.program_id(0)
        xv = x_ref[...]
        dh_v = refs[0][...]
        for r in refs[1:nd]:
            dh_v = dh_v + r[...]
        r = lax.rsqrt(jnp.mean(xv * xv, axis=-1, keepdims=True) + NORM_EPS)
        xn = xv * r
        gv = g_ref[...]
        one_sc = 1.0 + sc_ref[0]
        dhxn = dh_v * xn

        @pl.when(i % per_b == 0)
        def _():
            dsh_ref[...] = jnp.zeros_like(dsh_ref)
            dsc_ref[...] = jnp.zeros_like(dsc_ref)

        @pl.when(i == 0)
        def _():
            dg_ref[...] = jnp.zeros_like(dg_ref)

        dsh_ref[0] += jnp.sum(dh_v, axis=0, keepdims=True)
        dsc_ref[0] += jnp.sum(dhxn, axis=0, keepdims=True) * gv
        dg_ref[...] += jnp.sum(dhxn, axis=0, keepdims=True) * one_sc
        dxn = dh_v * (gv * one_sc)
        dx_ref[...] = r * (dxn - xn * jnp.mean(dxn * xn, axis=-1, keepdims=True)) + dxo_ref[...]

    row = pl.BlockSpec((tm, D), lambda i: (i, 0))
    per = pl.BlockSpec((1, 1, D), lambda i: (i // per_b, 0, 0))
    vec = pl.BlockSpec((1, D), lambda i: (0, 0))
    return _call(
        body, name=name, grid=(T // tm,),
        in_specs=[row] * (nd + 2) + [per, vec], out_specs=[row, per, per, vec],
        out_shape=[jax.ShapeDtypeStruct((T, D), F32), jax.ShapeDtypeStruct((B, 1, D), F32),
                   jax.ShapeDtypeStruct((B, 1, D), F32), jax.ShapeDtypeStruct((1, D), F32)],
        compiler_params=_params(),
    )(*dhs, x, dxo, scale, g)


def _gate_out(o, proj, w_out, x, gate, S, name):
    T, DI = o.shape
    D = w_out.shape[1]
    tm = _tile(S, 256)
    per_b = S // tm

    def body(o_ref, z_ref, w_ref, x_ref, g_ref, xo_ref, y_ref, u_ref):
        z = z_ref[...].astype(F32)
        u = (o_ref[...] * (z * _sigmoid(z))).astype(BF16)
        u_ref[...] = u
        y = jnp.dot(u, w_ref[...], preferred_element_type=F32)
        y_ref[...] = y
        xo_ref[...] = x_ref[...] + g_ref[0] * y

    wide = pl.BlockSpec((tm, DI), lambda i: (i, 0))
    row = pl.BlockSpec((tm, D), lambda i: (i, 0))
    return _call(
        body, name=name, grid=(T // tm,),
        in_specs=[wide, pl.BlockSpec((tm, DI), lambda i: (i, 3)), pl.BlockSpec((DI, D), lambda i: (0, 0)), row,
                  pl.BlockSpec((1, 1, D), lambda i: (i // per_b, 0, 0))],
        out_specs=[row, row, wide],
        out_shape=[jax.ShapeDtypeStruct((T, D), F32), jax.ShapeDtypeStruct((T, D), F32),
                   jax.ShapeDtypeStruct((T, DI), BF16)],
        compiler_params=_params(),
    )(o, proj, w_out, x, gate)


def _out_bwd(dxo, y, gate, w_out, o, proj, S, name):
    T, D = dxo.shape
    DI = o.shape[1]
    B = T // S
    tm = _tile(S, 256)
    per_b = S // tm

    def body(dxo_ref, y_ref, g_ref, w_ref, o_ref, z_ref, dy_ref, do_ref, dz_ref, dg_ref):
        dxo_v = dxo_ref[...]
        dy = (dxo_v * g_ref[0]).astype(BF16)
        dy_ref[...] = dy
        du = lax.dot_general(dy, w_ref[...], NT, preferred_element_type=F32)
        z = z_ref[...].astype(F32)
        sg = _sigmoid(z)
        do_ref[...] = (du * (z * sg)).astype(BF16)
        dz_ref[...] = (du * o_ref[...] * (sg * (1.0 + z * (1.0 - sg)))).astype(BF16)

        @pl.when(pl.program_id(0) % per_b == 0)
        def _():
            dg_ref[...] = jnp.zeros_like(dg_ref)

        dg_ref[0] += jnp.sum(dxo_v * y_ref[...], axis=0, keepdims=True)

    wide = pl.BlockSpec((tm, DI), lambda i: (i, 0))
    row = pl.BlockSpec((tm, D), lambda i: (i, 0))
    per = pl.BlockSpec((1, 1, D), lambda i: (i // per_b, 0, 0))
    return _call(
        body, name=name, grid=(T // tm,),
        in_specs=[row, row, per, pl.BlockSpec((DI, D), lambda i: (0, 0)), wide,
                  pl.BlockSpec((tm, DI), lambda i: (i, 3))],
        out_specs=[row, wide, wide, per],
        out_shape=[jax.ShapeDtypeStruct((T, D), BF16), jax.ShapeDtypeStruct((T, DI), BF16),
                   jax.ShapeDtypeStruct((T, DI), BF16), jax.ShapeDtypeStruct((B, 1, D), F32)],
        compiler_params=_params(),
    )(dxo, y, gate, w_out, o, proj)


def _final_loss(x, tgt, g, S, name):
    T, D = x.shape
    tm = _tile(S, 512)

    def body(x_ref, t_ref, g_ref, dx_ref, dg_ref, l_ref):
        @pl.when(pl.program_id(0) == 0)
        def _():
            dg_ref[...] = jnp.zeros_like(dg_ref)
            l_ref[...] = jnp.zeros_like(l_ref)

        xv = x_ref[...]
        gv = g_ref[...]
        r = lax.rsqrt(jnp.mean(xv * xv, axis=-1, keepdims=True) + NORM_EPS)
        xn = xv * r
        e = xn * gv - t_ref[...]
        part = jnp.sum(jnp.sum(e * e, axis=0, keepdims=True), axis=1, keepdims=True)
        l_ref[...] += (0.5 / D) * part
        dy = e * (1.0 / D)
        dg_ref[...] += jnp.sum(dy * xn, axis=0, keepdims=True)
        dxn = dy * gv
        dx_ref[...] = r * (dxn - xn * jnp.mean(dxn * xn, axis=-1, keepdims=True))

    row = pl.BlockSpec((tm, D), lambda i: (i, 0))
    return _call(
        body, name=name, grid=(T // tm,),
        in_specs=[row, row, pl.BlockSpec((1, D), lambda i: (0, 0))],
        out_specs=[row, pl.BlockSpec((1, D), lambda i: (0, 0)), pl.BlockSpec((1, LANES), lambda i: (0, 0))],
        out_shape=[jax.ShapeDtypeStruct((T, D), F32), jax.ShapeDtypeStruct((1, D), F32),
                   jax.ShapeDtypeStruct((1, LANES), F32)],
        compiler_params=_params(),
    )(x, tgt, g)


def _cum_fwd(fl, bf, name):
    B, S, _ = fl.shape
    ch = _tile(S, 256, 8)

    def body(fl_ref, b_ref, cum_ref):
        ri = lax.broadcasted_iota(jnp.int32, (ch, ch), 0)
        ci = lax.broadcasted_iota(jnp.int32, (ch, ch), 1)
        tri = jnp.where(ri >= ci, 1.0, 0.0).astype(BF16)

        def step(i, carry):
            r0 = pl.multiple_of(i * ch, ch)
            z = fl_ref[0, pl.ds(r0, ch), :] + b_ref[...]
            lf = (jnp.minimum(z, 0.0) - jnp.log(1.0 + jnp.exp(-jnp.abs(z)))) * LOG2E
            hi, mid, lo = _split3(lf)
            cs = (jnp.dot(tri, hi, preferred_element_type=F32) + jnp.dot(tri, mid, preferred_element_type=F32)
                  + jnp.dot(tri, lo, preferred_element_type=F32)) + carry
            cum_ref[0, pl.ds(r0, ch), :] = cs
            return cs[ch - 1:ch, :]

        lax.fori_loop(0, S // ch, step, jnp.zeros((1, LANES), F32))

    blk = pl.BlockSpec((1, S, LANES), lambda b: (b, 0, 0))
    return _call(
        body, name=name, grid=(B,), in_specs=[blk, pl.BlockSpec((1, LANES), lambda b: (0, 0))], out_specs=blk,
        out_shape=jax.ShapeDtypeStruct((B, S, LANES), F32), compiler_params=_params(),
    )(fl, bf)


def _cum_bwd(dcs, fl, bf, name):
    B, S, _ = fl.shape
    ch = _tile(S, 256, 8)
    n = S // ch

    def body(d_ref, fl_ref, b_ref, o_ref, db_ref):
        ri = lax.broadcasted_iota(jnp.int32, (ch, ch), 0)
        ci = lax.broadcasted_iota(jnp.int32, (ch, ch), 1)
        tri = jnp.where(ci >= ri, 1.0, 0.0).astype(BF16)

        @pl.when(pl.program_id(0) == 0)
        def _():
            db_ref[...] = jnp.zeros_like(db_ref)

        def step(t, carry):
            tail, dbsum = carry
            r0 = pl.multiple_of((n - 1 - t) * ch, ch)
            hi, mid, lo = _split3(d_ref[0, pl.ds(r0, ch), :])
            suf = (jnp.dot(tri, hi, preferred_element_type=F32) + jnp.dot(tri, mid, preferred_element_type=F32)
                   + jnp.dot(tri, lo, preferred_element_type=F32)) + tail
            z = fl_ref[0, pl.ds(r0, ch), :] + b_ref[...]
            dfl = -suf * _sigmoid(-z)
            o_ref[0, pl.ds(r0, ch), :] = dfl
            return suf[0:1, :], dbsum + jnp.sum(dfl, axis=0, keepdims=True)

        z1 = jnp.zeros((1, LANES), F32)
        _, dbsum = lax.fori_loop(0, n, step, (z1, z1))
        db_ref[...] += dbsum

    blk = pl.BlockSpec((1, S, LANES), lambda b: (b, 0, 0))
    vec = pl.BlockSpec((1, LANES), lambda b: (0, 0))
    return _call(
        body, name=name, grid=(B,), in_specs=[blk, blk, vec], out_specs=[blk, vec],
        out_shape=[jax.ShapeDtypeStruct((B, S, LANES), F32), jax.ShapeDtypeStruct((1, LANES), F32)],
        compiler_params=_params(),
    )(dcs, fl, bf)


HEADS_PER_STEP = 4
GROUP = 2 * HEAD_DIM


def _step_width():
    return HEAD_DIM * HEADS_PER_STEP


def _cols(S, offset_blocks=0):
    return pl.BlockSpec((S, _step_width()), lambda b, h: (b, offset_blocks + h))


def _row_spec(nq, tq):
    return pl.BlockSpec((1, HEADS_PER_STEP, nq, 1, tq), lambda b, h: (b, h, 0, 0, 0))


def _lanes(g):
    return slice(GROUP * (g // 2), GROUP * (g // 2) + GROUP)


def _hi_lo(x):
    hi = x.astype(BF16)
    return hi, (x - hi.astype(F32)).astype(BF16)


def _dot(a, b, dims=None):
    if dims is None:
        return jnp.dot(a, b, preferred_element_type=F32)
    return lax.dot_general(a, b, dims, preferred_element_type=F32)


def _causal_blocks(nq, prep, init, stages, finish, combine=None, descending=False):
    heads = range(HEADS_PER_STEP)

    def qloop(qi, _):
        ctx = [prep(g, qi) for g in heads]

        def step(kj, carry, masked):
            st = list(carry)
            for n, stage in enumerate(stages):
                if combine is not None and n == len(stages) - 1:
                    combine(kj, ctx, st)
                st = [stage(g, ctx[g], kj, masked, st[g]) for g in heads]
            return tuple(st)

        carry = tuple(init() for _ in heads)
        if descending:
            carry = step(qi, carry, True)
            carry = lax.fori_loop(0, qi, lambda t, cr: step(qi - 1 - t, cr, False), carry)
        else:
            carry = lax.fori_loop(0, qi, lambda kj, cr: step(kj, cr, False), carry)
            carry = step(qi, carry, True)
        finish(qi, ctx, carry)
        return 0

    lax.fori_loop(0, nq, qloop, 0)


class _Block:
    def __init__(self, tq):
        self.tq = tq
        self.lane = lax.broadcasted_iota(jnp.int32, (tq, GROUP), 1)
        self.low = self.lane < HEAD_DIM
        self.ri = lax.broadcasted_iota(jnp.int32, (tq, tq), 0)
        self.ci = lax.broadcasted_iota(jnp.int32, (tq, tq), 1)

    def rows(self, i):
        return pl.ds(pl.multiple_of(i * self.tq, self.tq), self.tq)

    def own(self, g, x):
        return jnp.where(self.low if g % 2 == 0 else jnp.logical_not(self.low), x, jnp.zeros_like(x))

    def pair(self, a, b):
        return jnp.where(self.low, a, b)

    def stat(self, g, x):
        return jnp.sum(jnp.where(self.lane == HEAD_DIM * (g % 2), x, 0.0), axis=1, keepdims=True)


def _fox_fwd(proj, cumcol, cumrow, name):
    T, DI = proj.shape[0], proj.shape[1] // 4
    B, H, nq, _, tq = cumrow.shape
    S = nq * tq
    nb = DI // _step_width()

    def body(q_ref, k_ref, v_ref, cc_ref, cr_ref, o_ref, st_ref, acc_scr):
        h0 = pl.program_id(1) * HEADS_PER_STEP
        blk = _Block(tq)

        def prep(g, qi):
            acc_scr[g] = jnp.zeros((tq, GROUP), F32)
            q = blk.own(g, q_ref[blk.rows(qi), _lanes(g)])
            ccol = jnp.sum(jnp.where(blk.lane == h0 + g, cc_ref[0, blk.rows(qi), :], 0.0), axis=1, keepdims=True)
            return q, ccol

        def init():
            return jnp.full((tq, 1), -jnp.inf, F32), jnp.zeros((tq, 1), F32)

        def scores(g, ctx, kj, masked, st):
            return st + (_dot(ctx[0], k_ref[blk.rows(kj), _lanes(g)], NT),)

        def softmax(g, ctx, kj, masked, st):
            m, l, s = st
            s = s + ctx[1] - cr_ref[0, g, kj]
            if masked:
                s = jnp.where(blk.ci <= blk.ri, s, -jnp.inf)
            m_new = jnp.maximum(m, jnp.max(s, axis=1, keepdims=True))
            alpha = jnp.exp2(m - m_new)
            p = jnp.exp2(s - m_new)
            return (m_new, alpha * l + jnp.sum(p, axis=1, keepdims=True), alpha) + _hi_lo(p)

        def values(g, ctx, kj, masked, st):
            m, l, alpha, hi, lo = st
            v = v_ref[blk.rows(kj), _lanes(g)]
            acc_scr[g] = alpha * acc_scr[g] + (_dot(hi, v) + _dot(lo, v))
            return m, l

        def finish(qi, ctx, carry):
            for g in range(0, HEADS_PER_STEP, 2):
                (m0, l0), (m1, l1) = carry[g], carry[g + 1]
                o_ref[blk.rows(qi), _lanes(g)] = blk.pair(acc_scr[g] / l0, acc_scr[g + 1] / l1)
                st_ref[blk.rows(qi), _lanes(g)] = blk.pair(m0 + jnp.log2(l0), m1 + jnp.log2(l1))

        _causal_blocks(nq, prep, init, [scores, softmax, values], finish)

    out = jax.ShapeDtypeStruct((T, DI), F32)
    return _call(
        body, name=name, grid=(B, H // HEADS_PER_STEP),
        in_specs=[_cols(S), _cols(S, nb), _cols(S, 2 * nb), pl.BlockSpec((1, S, LANES), lambda b, h: (b, 0, 0)),
                  _row_spec(nq, tq)],
        out_specs=[_cols(S), _cols(S)], out_shape=[out, out],
        scratch_shapes=[pltpu.VMEM((HEADS_PER_STEP, tq, GROUP), F32)], compiler_params=_params(),
    )(proj, proj, proj, cumcol, cumrow)


def _fox_bwd(proj, do, dzg, o, stat, cumcol, cumrow, name):
    T, DI = do.shape
    B, H, nq, _, tq = cumrow.shape
    S = nq * tq
    nb = DI // _step_width()

    def body(q_ref, k_ref, v_ref, do_ref, dz_ref, o_ref, st_ref, cc_ref, cr_ref, dqkv_ref, dcs_ref, dk_acc, dv_acc,
             dq_scr):
        h0 = pl.program_id(1) * HEADS_PER_STEP
        blk = _Block(tq)
        dk_acc[...] = jnp.zeros_like(dk_acc)
        dv_acc[...] = jnp.zeros_like(dv_acc)
        dcs_ref[...] = jnp.zeros_like(dcs_ref)

        def prep(g, qi):
            dq_scr[g] = jnp.zeros((tq, GROUP), F32)
            q = blk.own(g, q_ref[blk.rows(qi), _lanes(g)])
            dout = blk.own(g, do_ref[blk.rows(qi), _lanes(g)])
            delta = jnp.sum(o_ref[blk.rows(qi), _lanes(g)] * dout.astype(F32), axis=1, keepdims=True)
            lse = blk.stat(g, st_ref[blk.rows(qi), _lanes(g)])
            ccol = jnp.sum(jnp.where(blk.lane == h0 + g, cc_ref[0, blk.rows(qi), :], 0.0), axis=1, keepdims=True)
            return q, dout, lse, delta, ccol

        def init():
            return ()

        def scores(g, ctx, kj, masked, st):
            return (_dot(ctx[0], k_ref[blk.rows(kj), _lanes(g)], NT), _dot(ctx[1], v_ref[blk.rows(kj), _lanes(g)], NT))

        def softmax_bwd(g, ctx, kj, masked, st):
            s, dp = st
            _, _, lse, delta, ccol = ctx
            s = s + ccol - cr_ref[0, g, kj]
            if masked:
                s = jnp.where(blk.ci <= blk.ri, s, -jnp.inf)
            p = jnp.exp2(s - lse)
            ds = p * (dp - delta)
            return p.astype(BF16), ds.astype(BF16), jnp.sum(ds, axis=0, keepdims=True)

        def combine(kj, ctx, st):
            for g in range(0, HEADS_PER_STEP, 2):
                dv_acc[blk.rows(kj), _lanes(g)] += _dot(st[g][0], ctx[g][1], TN) + _dot(st[g + 1][0], ctx[g + 1][1], TN)
                dk_acc[blk.rows(kj), _lanes(g)] += _dot(st[g][1], ctx[g][0], TN) + _dot(st[g + 1][1], ctx[g + 1][0], TN)
            for g in range(HEADS_PER_STEP):
                dcs_ref[0, g, kj] += st[g][2]

        def queries(g, ctx, kj, masked, st):
            dq_scr[g] += _dot(st[1], blk.own(g, k_ref[blk.rows(kj), _lanes(g)]))
            return ()

        def finish(qi, ctx, carry):
            for g in range(0, HEADS_PER_STEP, 2):
                dqkv_ref[0, blk.rows(qi), _lanes(g)] = ((dq_scr[g] + dq_scr[g + 1]) * LN2).astype(BF16)

        _causal_blocks(nq, prep, init, [scores, softmax_bwd, queries], finish, combine=combine)
        dqkv_ref[1] = (dk_acc[...] * LN2).astype(BF16)
        dqkv_ref[2] = dv_acc[...].astype(BF16)
        dqkv_ref[3] = dz_ref[...]

    W = _step_width()
    return _call(
        body, name=name, grid=(B, H // HEADS_PER_STEP),
        in_specs=[_cols(S), _cols(S, nb), _cols(S, 2 * nb), _cols(S), _cols(S), _cols(S), _cols(S),
                  pl.BlockSpec((1, S, LANES), lambda b, h: (b, 0, 0)), _row_spec(nq, tq)],
        out_specs=[pl.BlockSpec((4, S, W), lambda b, h: (0, b, h)), _row_spec(nq, tq)],
        out_shape=[jax.ShapeDtypeStruct((4, T, DI), BF16), jax.ShapeDtypeStruct((B, H, nq, 1, tq), F32)],
        scratch_shapes=[pltpu.VMEM((S, W), F32), pltpu.VMEM((S, W), F32), pltpu.VMEM((HEADS_PER_STEP, tq, GROUP), F32)],
        compiler_params=_params(),
    )(proj, proj, proj, do, dzg, o, stat, cumcol, cumrow)


def _log2_keep(z2):
    nz = -z2
    e = jnp.exp2(jnp.minimum(z2, nz))
    return jnp.minimum(nz, 0.0) - jnp.log2(1.0 + e), e


def _sb_fwd(proj, B, tq, name):
    T, DI = proj.shape[0], proj.shape[1] // 4
    S = T // B
    H = DI // HEAD_DIM
    nq = S // tq
    nb = DI // _step_width()

    def body(q_ref, k_ref, v_ref, o_ref, st_ref, acc_scr, c_scr):
        blk = _Block(tq)
        strict = blk.ci < blk.ri
        above = jnp.where(blk.ri > blk.ci, 1.0, 0.0).astype(BF16)

        def prep(g, qi):
            acc_scr[g] = jnp.zeros((tq, GROUP), F32)
            c_scr[g] = jnp.zeros((tq, 1), F32)
            return blk.own(g, q_ref[blk.rows(qi), _lanes(g)])

        def init():
            return ()

        def scores(g, q, kj, masked, st):
            return (_dot(q, k_ref[blk.rows(kj), _lanes(g)], NT),)

        def logs(g, q, kj, masked, st):
            (z,) = st
            lk, _ = _log2_keep(z)
            lb = z + lk
            if masked:
                lk = jnp.where(strict, lk, 0.0)
            c = c_scr[g]
            c_scr[g] = c + jnp.sum(lk, axis=1, keepdims=True)
            return (lb + c,) + _hi_lo(lk)

        def suffix(g, q, kj, masked, st):
            lbc, hi, lo = st
            return lbc, _dot(hi, above) + _dot(lo, above)

        def weights(g, q, kj, masked, st):
            lbc, after = st
            a = jnp.exp2(lbc + after)
            if masked:
                a = jnp.where(strict, a, 0.0)
            return (a.astype(BF16),)

        def values(g, q, kj, masked, st):
            acc_scr[g] += _dot(st[0], v_ref[blk.rows(kj), _lanes(g)])
            return ()

        def finish(qi, ctx, carry):
            for g in range(0, HEADS_PER_STEP, 2):
                o_ref[blk.rows(qi), _lanes(g)] = blk.pair(acc_scr[g], acc_scr[g + 1])
                st_ref[blk.rows(qi), _lanes(g)] = blk.pair(c_scr[g], c_scr[g + 1])

        _causal_blocks(nq, prep, init, [scores, logs, suffix, weights, values], finish, descending=True)

    out = jax.ShapeDtypeStruct((T, DI), F32)
    return _call(
        body, name=name, grid=(B, H // HEADS_PER_STEP), in_specs=[_cols(S), _cols(S, nb), _cols(S, 2 * nb)],
        out_specs=[_cols(S), _cols(S)], out_shape=[out, out],
        scratch_shapes=[pltpu.VMEM((HEADS_PER_STEP, tq, GROUP), F32), pltpu.VMEM((HEADS_PER_STEP, tq, 1), F32)],
        compiler_params=_params(),
    )(proj, proj, proj)


def _sb_bwd(proj, do, dzg, stat, B, tq, name):
    T, DI = do.shape
    S = T // B
    H = DI // HEAD_DIM
    nq = S // tq
    nb = DI // _step_width()

    def body(q_ref, k_ref, v_ref, do_ref, dz_ref, st_ref, dqkv_ref, dk_acc, dv_acc, dq_scr):
        blk = _Block(tq)
        strict = blk.ci < blk.ri
        upto = jnp.where(blk.ri <= blk.ci, 1.0, 0.0).astype(BF16)
        before = jnp.where(blk.ri < blk.ci, 1.0, 0.0).astype(BF16)
        dk_acc[...] = jnp.zeros_like(dk_acc)
        dv_acc[...] = jnp.zeros_like(dv_acc)

        def prep(g, qi):
            dq_scr[g] = jnp.zeros((tq, GROUP), F32)
            return (blk.own(g, q_ref[blk.rows(qi), _lanes(g)]), blk.own(g, do_ref[blk.rows(qi), _lanes(g)]),
                    blk.stat(g, st_ref[blk.rows(qi), _lanes(g)]))

        def init():
            return jnp.zeros((tq, 1), F32), jnp.zeros((tq, 1), F32)

        def scores(g, ctx, kj, masked, st):
            return st + (_dot(ctx[0], k_ref[blk.rows(kj), _lanes(g)], NT),
                         _dot(ctx[1], v_ref[blk.rows(kj), _lanes(g)], NT))

        def logs(g, ctx, kj, masked, st):
            cpre, pg, z, da = st
            lk, e = _log2_keep(z)
            inv = 1.0 / (1.0 + e)
            sig = jnp.where(z >= 0.0, inv, e * inv)
            lbt = (z + lk) + (ctx[2] - cpre)
            if masked:
                lk = jnp.where(strict, lk, 0.0)
            return (cpre + jnp.sum(lk, axis=1, keepdims=True), pg, da, lbt, sig) + _hi_lo(lk)

        def prefix(g, ctx, kj, masked, st):
            cpre, pg, da, lbt, sig, hi, lo = st
            return cpre, pg, da, lbt, sig, _dot(hi, upto) + _dot(lo, upto)

        def weights(g, ctx, kj, masked, st):
            cpre, pg, da, lbt, sig, pre = st
            a = jnp.exp2(lbt - pre)
            if masked:
                a = jnp.where(strict, a, 0.0)
            gr = da * a
            return cpre, pg, sig, a.astype(BF16), gr, gr.astype(BF16)

        def grad_prefix(g, ctx, kj, masked, st):
            cpre, pg, sig, ab, gr, gb = st
            return cpre, pg, sig, ab, gr, _dot(gb, before)

        def dlogits(g, ctx, kj, masked, st):
            cpre, pg, sig, ab, gr, pfx = st
            dz = gr - sig * (gr + (pfx + pg))
            if masked:
                dz = jnp.where(strict, dz, 0.0)
            return cpre, pg + jnp.sum(gr, axis=1, keepdims=True), ab, dz.astype(BF16)

        def combine(kj, ctx, st):
            for g in range(0, HEADS_PER_STEP, 2):
                dv_acc[blk.rows(kj), _lanes(g)] += _dot(st[g][2], ctx[g][1], TN) + _dot(st[g + 1][2], ctx[g + 1][1], TN)
                dk_acc[blk.rows(kj), _lanes(g)] += _dot(st[g][3], ctx[g][0], TN) + _dot(st[g + 1][3], ctx[g + 1][0], TN)

        def queries(g, ctx, kj, masked, st):
            cpre, pg, _, dzb = st
            dq_scr[g] += _dot(dzb, blk.own(g, k_ref[blk.rows(kj), _lanes(g)]))
            return cpre, pg

        def finish(qi, ctx, carry):
            for g in range(0, HEADS_PER_STEP, 2):
                dqkv_ref[0, blk.rows(qi), _lanes(g)] = ((dq_scr[g] + dq_scr[g + 1]) * LN2).astype(BF16)

        _causal_blocks(nq, prep, init, [scores, logs, prefix, weights, grad_prefix, dlogits, queries], finish,
                       combine=combine)
        dqkv_ref[1] = (dk_acc[...] * LN2).astype(BF16)
        dqkv_ref[2] = dv_acc[...].astype(BF16)
        dqkv_ref[3] = dz_ref[...]

    W = _step_width()
    return _call(
        body, name=name, grid=(B, H // HEADS_PER_STEP),
        in_specs=[_cols(S), _cols(S, nb), _cols(S, 2 * nb), _cols(S), _cols(S), _cols(S)],
        out_specs=pl.BlockSpec((4, S, W), lambda b, h: (0, b, h)),
        out_shape=jax.ShapeDtypeStruct((4, T, DI), BF16),
        scratch_shapes=[pltpu.VMEM((S, W), F32), pltpu.VMEM((S, W), F32), pltpu.VMEM((HEADS_PER_STEP, tq, GROUP), F32)],
        compiler_params=_params(),
    )(proj, proj, proj, do, dzg, stat)


def _row_tile(R, C, n_arrays):
    budget = 24 * 1024 * 1024 // (2 * n_arrays * 4 * max(C, LANES))
    return _tile(R, max(8, budget), 8)


def _ew_sum(parts, name, also_bf16=False):
    R, C = parts[0].shape
    tr = _row_tile(R, C, len(parts) + 2)
    n = len(parts)

    def body(*refs):
        acc = refs[0][...].astype(F32) + refs[1][...].astype(F32)
        for r in refs[2:n]:
            acc = acc + r[...].astype(F32)
        refs[n][...] = acc
        if also_bf16:
            refs[n + 1][...] = acc.astype(BF16)

    blk = pl.BlockSpec((tr, C), lambda i: (i, 0))
    out_shape = [jax.ShapeDtypeStruct((R, C), F32)] + ([jax.ShapeDtypeStruct((R, C), BF16)] if also_bf16 else [])
    return _call(
        body, name=name, grid=(R // tr,), in_specs=[blk] * n, out_specs=[blk] * len(out_shape),
        out_shape=out_shape, compiler_params=_params(),
    )(*parts)


def _adamw(w, g, m, v, name):
    R, C = w.shape
    tr = _row_tile(R, C, 7)
    c1 = 1.0 / (1.0 - ADAM_B1 ** ADAM_STEP)
    c2 = 1.0 / (1.0 - ADAM_B2 ** ADAM_STEP)

    def body(w_ref, g_ref, m_ref, v_ref, d_ref, m2_ref, v2_ref):
        gv = g_ref[...]
        m2 = ADAM_B1 * m_ref[...] + (1.0 - ADAM_B1) * gv
        v2 = ADAM_B2 * v_ref[...] + (1.0 - ADAM_B2) * (gv * gv)
        m2_ref[...] = m2
        v2_ref[...] = v2
        d_ref[...] = -ADAM_LR * ((m2 * c1) / (jnp.sqrt(v2 * c2) + ADAM_EPS) + ADAM_WD * w_ref[...])

    blk = pl.BlockSpec((tr, C), lambda i: (i, 0))
    out = jax.ShapeDtypeStruct((R, C), F32)
    return _call(
        body, name=name, grid=(R // tr,), in_specs=[blk] * 4, out_specs=[blk] * 3, out_shape=[out] * 3,
        compiler_params=_params(),
    )(w, g, m, v)


def _me():
    return lax.axis_index("x"), lax.axis_index("y"), lax.axis_index("c")


def _chip_of(x, y):
    return 2 * x + y


def _other_chips(x, y):
    return [(x, 1 - y), (1 - x, y), (1 - x, 1 - y)]


def _gather_weights(halves, smalls):
    nh, ns = len(halves), len(smalls)

    def body(*refs):
        ins_h, ins_s = refs[:nh], refs[nh:nh + ns]
        outs_h, outs_s = refs[nh + ns:2 * nh + ns], refs[2 * nh + ns:2 * (nh + ns)]
        send1, recv1, send2, recv2, send3, recv3 = refs[2 * (nh + ns):]
        x, y, c = _me()
        mine = _chip_of(x, y)
        chips = _other_chips(x, y)
        sib = (x, y, 1 - c)

        def landed(i, k, half):
            return outs_h[i].at[_chip_of(*chips[k]), half]

        def first(i, k):
            return pltpu.make_async_remote_copy(
                src_ref=ins_h[i].at[c], dst_ref=outs_h[i].at[mine, c], send_sem=send1.at[i, k], recv_sem=recv1.at[i, k],
                device_id=(*chips[k], c), device_id_type=MESH)

        def passed(i, k):
            return pltpu.make_async_remote_copy(
                src_ref=landed(i, k, c), dst_ref=landed(i, k, c), send_sem=send2.at[i, k], recv_sem=recv2.at[i, k],
                device_id=sib, device_id_type=MESH)

        def small(i, k):
            return pltpu.make_async_remote_copy(
                src_ref=ins_s[i], dst_ref=outs_s[i].at[mine], send_sem=send3.at[i, k], recv_sem=recv3.at[i, k],
                device_id=(*chips[k], c), device_id_type=MESH)

        for i in range(nh):
            for k in range(3):
                first(i, k).start()
        for i in range(ns):
            for k in range(3):
                small(i, k).start()
        for i in range(nh):
            for k in range(3):
                pltpu.make_async_remote_copy(
                    src_ref=ins_h[i].at[c], dst_ref=landed(i, k, c), send_sem=send1.at[i, k], recv_sem=recv1.at[i, k],
                    device_id=(*chips[k], c), device_id_type=MESH).wait_recv()
                passed(i, k).start()
        for i in range(nh):
            for k in range(3):
                pltpu.make_async_remote_copy(
                    src_ref=landed(i, k, c), dst_ref=landed(i, k, 1 - c), send_sem=send2.at[i, k],
                    recv_sem=recv2.at[i, k], device_id=sib, device_id_type=MESH).wait_recv()
        for i in range(ns):
            for k in range(3):
                pltpu.make_async_remote_copy(
                    src_ref=ins_s[i], dst_ref=outs_s[i].at[_chip_of(*chips[k])], send_sem=send3.at[i, k],
                    recv_sem=recv3.at[i, k], device_id=(*chips[k], c), device_id_type=MESH).wait_recv()
        for i in range(nh):
            for k in range(3):
                first(i, k).wait_send()
                passed(i, k).wait_send()
        for i in range(ns):
            for k in range(3):
                small(i, k).wait_send()

    out_shape = ([jax.ShapeDtypeStruct((4,) + a.shape, a.dtype) for a in halves]
                 + [jax.ShapeDtypeStruct((4,) + a.shape, a.dtype) for a in smalls])
    n = nh + ns
    res = _call(
        body, name="gather_weights", in_specs=[HBM] * n, out_specs=[HBM] * n, out_shape=out_shape,
        scratch_shapes=[pltpu.SemaphoreType.DMA((nh, 3)), pltpu.SemaphoreType.DMA((nh, 3)),
                        pltpu.SemaphoreType.DMA((nh, 3)), pltpu.SemaphoreType.DMA((nh, 3)),
                        pltpu.SemaphoreType.DMA((max(ns, 1), 3)), pltpu.SemaphoreType.DMA((max(ns, 1), 3))],
        compiler_params=_params(),
    )(*halves, *smalls)
    return res[:nh], res[nh:]


def _pair_exchange(grads):
    n = len(grads)

    def body(*refs):
        ins, got = refs[:n], refs[n:2 * n]
        send, recv = refs[2 * n:]
        x, y, c = _me()
        cps = []
        for i in range(n):
            for j in range(4):
                r = pltpu.make_async_remote_copy(
                    src_ref=ins[i].at[j, 1 - c], dst_ref=got[i].at[j], send_sem=send.at[i, j], recv_sem=recv.at[i, j],
                    device_id=(x, y, 1 - c), device_id_type=MESH)
                r.start()
                cps.append(r)
        for r in cps:
            r.wait()

    return _call(
        body, name="grad_pair_exchange", in_specs=[HBM] * n, out_specs=[HBM] * n,
        out_shape=[jax.ShapeDtypeStruct((4,) + g.shape[2:], g.dtype) for g in grads],
        scratch_shapes=[pltpu.SemaphoreType.DMA((n, 4)), pltpu.SemaphoreType.DMA((n, 4))],
        compiler_params=_params(),
    )(*grads)


def _chip_exchange(sums):
    n = len(sums)

    def body(*refs):
        ins, got = refs[:n], refs[n:2 * n]
        send, recv = refs[2 * n:]
        x, y, c = _me()
        chips = _other_chips(x, y)
        cps = []
        for i in range(n):
            for k in range(3):
                r = pltpu.make_async_remote_copy(
                    src_ref=ins[i].at[_chip_of(*chips[k])], dst_ref=got[i].at[k], send_sem=send.at[i, k],
                    recv_sem=recv.at[i, k], device_id=(*chips[k], c), device_id_type=MESH)
                r.start()
                cps.append(r)
        for r in cps:
            r.wait()

    return _call(
        body, name="grad_chip_exchange", in_specs=[HBM] * n, out_specs=[HBM] * n,
        out_shape=[jax.ShapeDtypeStruct((3,) + s.shape[1:], s.dtype) for s in sums],
        scratch_shapes=[pltpu.SemaphoreType.DMA((n, 3)), pltpu.SemaphoreType.DMA((n, 3))],
        compiler_params=_params(),
    )(*sums)


def _pair_share(halves):
    n = len(halves)

    def body(*refs):
        ins, outs = refs[:n], refs[n:2 * n]
        send, recv = refs[2 * n:]
        x, y, c = _me()
        cps = []
        for i in range(n):
            r = pltpu.make_async_remote_copy(
                src_ref=ins[i], dst_ref=outs[i], send_sem=send.at[i], recv_sem=recv.at[i],
                device_id=(x, y, 1 - c), device_id_type=MESH)
            r.start()
            cps.append(r)
        for r in cps:
            r.wait()

    return _call(
        body, name="grad_pair_share", in_specs=[HBM] * n, out_specs=[HBM] * n,
        out_shape=[jax.ShapeDtypeStruct(h.shape, h.dtype) for h in halves],
        scratch_shapes=[pltpu.SemaphoreType.DMA((n,)), pltpu.SemaphoreType.DMA((n,))],
        compiler_params=_params(),
    )(*halves)


def _allreduce_small(vec):
    P = vec.shape[1]

    def body(v_ref, sum_ref, all_ref, send, recv):
        x, y, c = _me()
        me = 4 * x + 2 * y + c
        all_ref[pl.ds(me, 1)] = v_ref[...][None]
        cps = []
        for d in range(1, 8):
            peer = (jnp.bitwise_xor(x, d >> 2), jnp.bitwise_xor(y, (d >> 1) & 1), jnp.bitwise_xor(c, d & 1))
            r = pltpu.make_async_remote_copy(
                src_ref=v_ref, dst_ref=all_ref.at[me], send_sem=send.at[d - 1], recv_sem=recv.at[d - 1],
                device_id=peer, device_id_type=MESH)
            r.start()
            cps.append(r)
        for d in range(1, 8):
            src = jnp.bitwise_xor(me, d)
            pltpu.make_async_remote_copy(
                src_ref=v_ref, dst_ref=all_ref.at[src], send_sem=send.at[d - 1], recv_sem=recv.at[d - 1],
                device_id=(x, y, c), device_id_type=MESH).wait_recv()
        for r in cps:
            r.wait_send()
        acc = all_ref[0]
        for i in range(1, 8):
            acc = acc + all_ref[i]
        sum_ref[...] = acc

    vm = pl.BlockSpec(memory_space=pltpu.VMEM)
    return _call(
        body, name="allreduce_small", in_specs=[vm], out_specs=[vm, vm],
        out_shape=[jax.ShapeDtypeStruct((8, P), F32), jax.ShapeDtypeStruct((8, 8, P), F32)],
        scratch_shapes=[pltpu.SemaphoreType.DMA((7,)), pltpu.SemaphoreType.DMA((7,))],
        compiler_params=_params(),
    )(vec)[0]


def _per_batch(mod, B, D):
    return [mod[:B, i * D:(i + 1) * D].reshape(B, 1, D) for i in range(3)]


def _pad_rows8(a):
    return jnp.concatenate([a, jnp.zeros((8 - a.shape[0],) + a.shape[1:], a.dtype)], axis=0)


def _layer_fwd(x, c8, w, S, fox, tag):
    T, D = x.shape
    B = T // S
    DI = w["w_out"].shape[0]
    H = DI // HEAD_DIM
    tq = _tile(S, ATT_BLOCK, 8)
    mod = _mod_fwd(c8, w["w_ada"], w["b_ada"], tag + "_mod_fwd")
    shift, scale, gate = _per_batch(mod, B, D)
    proj, h = _ln_proj(x, shift, scale, w["norm_g"], w["w_in"], S, tag + "_ln_proj")
    saved = dict(x=x, h=h, proj=proj, scale=scale, gate=gate)
    if fox:
        fl = _mm(h, w["w_f"], "nn", F32, tag + "_flogit").reshape(B, S, LANES)
        cum = _cum_fwd(fl, w["b_f"], tag + "_cum_fwd")
        cumrow = cum[:, :, :H].transpose(0, 2, 1).reshape(B, H, S // tq, 1, tq)
        o, stat = _fox_fwd(proj, cum, cumrow, tag + "_attn_fwd")
        saved.update(fl=fl, cum=cum, cumrow=cumrow)
    else:
        o, stat = _sb_fwd(proj, B, tq, tag + "_attn_fwd")
    xo, y, u = _gate_out(o, proj, w["w_out"], x, gate, S, tag + "_gate_out")
    saved.update(o=o, stat=stat, y=y, u=u)
    return xo, saved


def _layer_bwd(dxo, sv, w, cT, S, fox, tag):
    T, D = dxo.shape
    B = T // S
    DI = w["w_out"].shape[0]
    H = DI // HEAD_DIM
    tq = _tile(S, ATT_BLOCK, 8)
    dy, do, dzg, dgate = _out_bwd(dxo, sv["y"], sv["gate"], w["w_out"], sv["o"], sv["proj"], S, tag + "_out_bwd")
    g = {"w_out": _mm(sv["u"], dy, "tn", F32, tag + "_dw_out", tm=1024, tn=1024, tk=2048)}
    q_cols = jnp.where(jnp.arange(4 * DI)[None, :] < DI, Q_SCALE, 1.0).astype(F32)
    if fox:
        dproj, dcs = _fox_bwd(sv["proj"], do, dzg, sv["o"], sv["stat"], sv["cum"], sv["cumrow"], tag + "_attn_bwd")
        dcs = dcs.reshape(B, H, S).transpose(0, 2, 1)
        dcs = jnp.concatenate([dcs, jnp.zeros((B, S, LANES - H), F32)], axis=-1)
        dfl, db_f = _cum_bwd(dcs, sv["fl"], w["b_f"], tag + "_cum_bwd")
        g["b_f"] = db_f[:, :H]
        dfl = dfl.reshape(T, LANES).astype(BF16)
    else:
        dproj = _sb_bwd(sv["proj"], do, dzg, sv["stat"], B, tq, tag + "_attn_bwd")
    g["w_in"] = _mm(sv["h"], dproj, "tn", F32, tag + "_dw_in", tm=1024, tn=2048, tk=1024, col_scale=q_cols)
    dhs = [_mm(dproj, w["w_in"], "nt", F32, tag + "_dh", tm=2048, tn=1024, tk=1024)]
    if fox:
        dw_f = _mm(sv["h"], dfl, "tn", F32, tag + "_dw_f", tm=1024, tn=LANES, tk=2048)
        g["w_in"] = jnp.concatenate([g["w_in"], dw_f[:, :H]], axis=1)
        dhs.append(_mm(dfl, w["w_f"], "nt", F32, tag + "_dh_f", tm=2048, tn=1024, tk=LANES))
    dx, dshift, dscale, dg = _ln_bwd(dhs, sv["x"], dxo, sv["scale"], w["norm_g"], S, tag + "_ln_bwd")
    g["norm_g"] = dg
    dmod = jnp.concatenate([dshift, dscale, dgate], axis=-1).reshape(B, 3 * D)
    g["w_ada"], g["b_ada"] = _mod_bwd(cT, _pad_rows8(dmod), B, tag + "_mod_bwd")
    return dx, g


def _local_step(x3, c, tgt3, wf, ws, final_g):
    B, S, D = x3.shape
    T = B * S
    x = x3.reshape(T, D)
    c8 = _pad_rows8(c)
    cT = c8.T
    x1, sv1 = _layer_fwd(x, c8, wf, S, True, "fox")
    x2, sv2 = _layer_fwd(x1, c8, ws, S, False, "sb")
    dx2, dgf, loss = _final_loss(x2, tgt3.reshape(T, D), final_g, S, "final_loss")
    dx1, gs = _layer_bwd(dx2, sv2, ws, cT, S, False, "sb")
    dx0, gf = _layer_bwd(dx1, sv1, wf, cT, S, True, "fox")
    return loss, dx0.reshape(B, S, D), gf, gs, dgf


def _cols_to_shards(a):
    R, C4 = a.shape
    return a.reshape(R, 4, C4 // 4).transpose(1, 0, 2)


def _shards_to_cols(a):
    n, R, C = a.shape
    return a.transpose(1, 0, 2).reshape(R, n * C)


def kernel(x, c, fox_norm_g, fox_w_ada, fox_b_ada, fox_w_in, fox_b_f, fox_w_out, sb_norm_g, sb_w_ada, sb_b_ada, sb_w_in, sb_w_out, final_norm_g, loss_target, m_fox_norm_g, m_fox_w_ada, m_fox_b_ada, m_fox_w_in, m_fox_b_f, m_fox_w_out, m_sb_norm_g, m_sb_w_ada, m_sb_b_ada, m_sb_w_in, m_sb_w_out, m_final_norm_g, v_fox_norm_g, v_fox_w_ada, v_fox_b_ada, v_fox_w_in, v_fox_b_f, v_fox_w_out, v_sb_norm_g, v_sb_w_ada, v_sb_b_ada, v_sb_w_in, v_sb_w_out, v_final_norm_g):
    B, S, D = x.shape
    DI = 4 * fox_w_out.shape[1]
    H = DI // HEAD_DIM
    chip = _chip_of(lax.axis_index("x"), lax.axis_index("y"))

    big_names = ["fox_w_ada", "fox_w_in", "fox_w_out", "sb_w_ada", "sb_w_in", "sb_w_out"]
    big = dict(fox_w_ada=fox_w_ada[0], fox_w_in=fox_w_in[0], fox_w_out=fox_w_out[0],
               sb_w_ada=sb_w_ada[0], sb_w_in=sb_w_in[0], sb_w_out=sb_w_out[0])
    for n in ("fox_w_in", "sb_w_in"):
        width = big[n].shape[1]
        is_q = chip * width + jnp.arange(width)[None, :] < DI
        big[n] = big[n] * jnp.where(is_q, Q_SCALE, 1.0).astype(F32)
    halves = [big[n].astype(BF16).reshape(2, big[n].shape[0] // 2, big[n].shape[1]) for n in big_names]
    gathered, gsmall = _gather_weights(halves, [sb_norm_g, sb_b_ada])
    gathered = [lax.dynamic_update_index_in_dim(a, own, chip, 0) for a, own in zip(gathered, halves)]
    gsmall = [lax.dynamic_update_index_in_dim(a, own, chip, 0) for a, own in zip(gsmall, [sb_norm_g, sb_b_ada])]
    full = {}
    for n, a in zip(big_names, gathered):
        a = a.reshape(4, a.shape[1] * a.shape[2], a.shape[3])
        full[n] = a.reshape(4 * a.shape[1], a.shape[2]) if n.endswith("w_out") else _shards_to_cols(a)
    sb_norm_full = gsmall[0].reshape(1, D)
    sb_b_ada_full = gsmall[1].reshape(1, 3 * D)
    w_f = jnp.concatenate([full["fox_w_in"][:, 4 * DI:], jnp.zeros((D, LANES - H), BF16)], axis=1)
    b_f = jnp.concatenate([fox_b_f, jnp.zeros((1, LANES - H), F32)], axis=1)
    wf = dict(w_ada=full["fox_w_ada"], b_ada=fox_b_ada, norm_g=fox_norm_g, w_in=full["fox_w_in"][:, :4 * DI],
              w_f=w_f, b_f=b_f, w_out=full["fox_w_out"])
    ws = dict(w_ada=full["sb_w_ada"], b_ada=sb_b_ada_full, norm_g=sb_norm_full, w_in=full["sb_w_in"],
              w_out=full["sb_w_out"])

    loss, grad_x, gf, gs, dgf = _local_step(x, c, loss_target, wf, ws, final_norm_g.reshape(1, D))

    part = dict(fox_w_ada=gf["w_ada"], fox_w_in=gf["w_in"], fox_w_out=gf["w_out"],
                sb_w_ada=gs["w_ada"], sb_w_in=gs["w_in"], sb_w_out=gs["w_out"])
    shard_major = []
    for n in big_names:
        a = part[n]
        a = a.reshape(4, a.shape[0] // 4, a.shape[1]) if n.endswith("w_out") else _cols_to_shards(a)
        shard_major.append(a.reshape(4, 2, a.shape[1] // 2, a.shape[2]))
    core = lax.axis_index("c")
    got = _pair_exchange(shard_major)
    pair_f32, pair_bf16 = [], []
    for n, g4, b in zip(big_names, shard_major, got):
        a = lax.dynamic_index_in_dim(g4, core, axis=1, keepdims=False)
        r, C = a.shape[1:]
        s32, s16 = _ew_sum([a.reshape(4 * r, C), b.reshape(4 * r, C)], n + "_pair_sum", also_bf16=True)
        pair_f32.append(s32.reshape(4, r, C))
        pair_bf16.append(s16.reshape(4, r, C))
    others = _chip_exchange(pair_bf16)
    reduced_halves = [_ew_sum([lax.dynamic_index_in_dim(a, chip, axis=0, keepdims=False), b[0], b[1], b[2]],
                              n + "_chip_sum")[0] for n, a, b in zip(big_names, pair_f32, others)]
    theirs = _pair_share(reduced_halves)
    grad_big = {}
    for n, a, b in zip(big_names, reduced_halves, theirs):
        grad_big[n] = jnp.concatenate([jnp.where(core == 0, a, b), jnp.where(core == 0, b, a)], axis=0)

    pieces = [loss, gf["norm_g"], gf["b_ada"], jnp.concatenate([gf["b_f"], jnp.zeros((1, LANES - H), F32)], axis=1),
              gs["norm_g"], gs["b_ada"], dgf]
    vec = jnp.concatenate(pieces, axis=1)
    red = _allreduce_small(_pad_rows8(vec))[0:1]
    offs = [0]
    for p in pieces:
        offs.append(offs[-1] + p.shape[1])
    r_loss, r_fng, r_fba, r_fbf, r_sng, r_sba, r_fin = [red[:, offs[i]:offs[i + 1]] for i in range(7)]
    small_grads = dict(
        fox_norm_g=r_fng, fox_b_ada=r_fba, fox_b_f=r_fbf[:, :H],
        sb_norm_g=lax.dynamic_slice_in_dim(r_sng, chip * (D // 4), D // 4, axis=1),
        sb_b_ada=lax.dynamic_slice_in_dim(r_sba, chip * (3 * D // 4), 3 * D // 4, axis=1),
        final_norm_g=r_fin)

    weights = dict(fox_norm_g=fox_norm_g, fox_w_ada=fox_w_ada, fox_b_ada=fox_b_ada, fox_w_in=fox_w_in, fox_b_f=fox_b_f,
                   fox_w_out=fox_w_out, sb_norm_g=sb_norm_g, sb_w_ada=sb_w_ada, sb_b_ada=sb_b_ada, sb_w_in=sb_w_in,
                   sb_w_out=sb_w_out, final_norm_g=final_norm_g)
    ms = dict(fox_norm_g=m_fox_norm_g, fox_w_ada=m_fox_w_ada, fox_b_ada=m_fox_b_ada, fox_w_in=m_fox_w_in,
              fox_b_f=m_fox_b_f, fox_w_out=m_fox_w_out, sb_norm_g=m_sb_norm_g, sb_w_ada=m_sb_w_ada,
              sb_b_ada=m_sb_b_ada, sb_w_in=m_sb_w_in, sb_w_out=m_sb_w_out, final_norm_g=m_final_norm_g)
    vs = dict(fox_norm_g=v_fox_norm_g, fox_w_ada=v_fox_w_ada, fox_b_ada=v_fox_b_ada, fox_w_in=v_fox_w_in,
              fox_b_f=v_fox_b_f, fox_w_out=v_fox_w_out, sb_norm_g=v_sb_norm_g, sb_w_ada=v_sb_w_ada,
              sb_b_ada=v_sb_b_ada, sb_w_in=v_sb_w_in, sb_w_out=v_sb_w_out, final_norm_g=v_final_norm_g)
    order = ["fox_norm_g", "fox_w_ada", "fox_b_ada", "fox_w_in", "fox_b_f", "fox_w_out", "sb_norm_g", "sb_w_ada",
             "sb_b_ada", "sb_w_in", "sb_w_out", "final_norm_g"]
    grads, deltas, new_m, new_v = {}, {}, {}, {}
    for n in big_names:
        shp = weights[n].shape
        g2 = grad_big[n]
        d, m2, v2 = _adamw(weights[n][0], g2, ms[n][0], vs[n][0], n + "_adamw")
        grads[n], deltas[n], new_m[n], new_v[n] = g2.reshape(shp), d.reshape(shp), m2.reshape(shp), v2.reshape(shp)
    small_names = [n for n in order if n not in big_names]
    sizes = [small_grads[n].shape[1] for n in small_names]
    total = sum(sizes)
    padn = (-total) % LANES

    def pack(d):
        return jnp.concatenate([d[n].reshape(1, -1) for n in small_names] + [jnp.ones((1, padn), F32)], axis=1)

    sd, sm, sv_ = _adamw(pack(weights), pack(small_grads), pack(ms), pack(vs), "small_adamw")
    o = 0
    for n, sz in zip(small_names, sizes):
        shp = weights[n].shape
        grads[n] = small_grads[n].reshape(shp)
        deltas[n], new_m[n], new_v[n] = (t[:, o:o + sz].reshape(shp) for t in (sd, sm, sv_))
        o += sz
    return (r_loss[0, 0], grad_x, *[grads[n] for n in order], *[deltas[n] for n in order],
            *[new_m[n] for n in order], *[new_v[n] for n in order])
```

```python
import functools

import jax
import jax.numpy as jnp
from jax import lax
from jax.experimental import pallas as pl
from jax.experimental.pallas import tpu as pltpu

F32 = jnp.float32
BF16 = jnp.bfloat16
HEAD_DIM = 64
LOG2E = 1.4426950408889634
LN2 = 0.6931471805599453
Q_SCALE = HEAD_DIM ** -0.5 * LOG2E
LANES = 128
NORM_EPS = 1e-6
ADAM_LR = 0.001
ADAM_B1 = 0.9
ADAM_B2 = 0.999
ADAM_EPS = 1e-08
ADAM_WD = 0.01
ADAM_STEP = 10
VMEM_LIMIT = 56 * 1024 * 1024
SB_BLOCK = 256
FOX_BLOCK = 512
MESH = pl.DeviceIdType.MESH
HBM = pl.BlockSpec(memory_space=pltpu.HBM)
NT = (((1,), (1,)), ((), ()))
TN = (((0,), (0,)), ((), ()))


def _call(body, **kw):
    return pl.pallas_call(body, **kw)


def _params(**kw):
    return pltpu.CompilerParams(vmem_limit_bytes=VMEM_LIMIT, **kw)


def _tile(dim, pref, mult=128):
    if dim <= pref:
        return dim
    t = (pref // mult) * mult
    while t >= mult:
        if dim % t == 0:
            return t
        t -= mult
    return dim


def _sigmoid(x):
    return 1.0 / (1.0 + jnp.exp(-x))


def _split3(x):
    hi = x.astype(BF16)
    r = x - hi.astype(F32)
    mid = r.astype(BF16)
    lo = (r - mid.astype(F32)).astype(BF16)
    return hi, mid, lo


def _mm(a, b, mode, out_dtype, name, tm=512, tn=512, tk=512, col_scale=None):
    a_slabs = a.shape[0] if a.ndim == 3 else 0
    b_slabs = b.shape[0] if b.ndim == 3 else 0
    if mode == "nn":
        (M, K), (_, N) = a.shape, b.shape
    elif mode == "nt":
        M, K = (a.shape[1], a_slabs * a.shape[2]) if a_slabs else a.shape
        N = b.shape[0]
    else:
        K, M = a.shape
        N = b_slabs * b.shape[2] if b_slabs else b.shape[1]
    tm, tn, tk = _tile(M, tm), _tile(N, tn), _tile(K, tk)
    if a_slabs:
        tk = _tile(a.shape[2], tk)
    if b_slabs:
        tn = _tile(b.shape[2], tn)
    nk = K // tk
    dims = {"nn": (((1,), (0,)), ((), ())), "nt": NT, "tn": TN}[mode]

    def body(a_ref, b_ref, *rest):
        o_ref, acc_ref = rest[-2:]
        k = pl.program_id(2)

        @pl.when(k == 0)
        def _():
            acc_ref[...] = jnp.zeros_like(acc_ref)

        acc_ref[...] += lax.dot_general(a_ref[...], b_ref[...], dims, preferred_element_type=F32)

        @pl.when(k == nk - 1)
        def _():
            acc = acc_ref[...]
            if col_scale is not None:
                acc = acc * rest[0][...]
            o_ref[...] = acc.astype(out_dtype)

    if a_slabs:
        per = a.shape[2] // tk
        a_spec = pl.BlockSpec((None, tm, tk), lambda i, j, k: (k // per, i, k % per))
    elif mode == "tn":
        a_spec = pl.BlockSpec((tk, tm), lambda i, j, k: (k, i))
    else:
        a_spec = pl.BlockSpec((tm, tk), lambda i, j, k: (i, k))
    if b_slabs:
        per_b = b.shape[2] // tn
        b_spec = pl.BlockSpec((None, tk, tn), lambda i, j, k: (j // per_b, k, j % per_b))
    elif mode == "nt":
        b_spec = pl.BlockSpec((tn, tk), lambda i, j, k: (j, k))
    else:
        b_spec = pl.BlockSpec((tk, tn), lambda i, j, k: (k, j))
    extra_specs = [] if col_scale is None else [pl.BlockSpec((1, tn), lambda i, j, k: (0, j))]
    extra = [] if col_scale is None else [col_scale]
    return _call(
        body, name=name, grid=(M // tm, N // tn, nk),
        in_specs=[a_spec, b_spec] + extra_specs, out_specs=pl.BlockSpec((tm, tn), lambda i, j, k: (i, j)),
        out_shape=jax.ShapeDtypeStruct((M, N), out_dtype),
        scratch_shapes=[pltpu.VMEM((tm, tn), F32)], compiler_params=_params(),
    )(a, b, *extra)


def _mod_fwd(c8, w_ada, b_ada, name):
    D, N = w_ada.shape
    tn = _tile(N, 512)

    def body(c_ref, w_ref, b_ref, o_ref):
        c = c_ref[...]
        sc = (c * _sigmoid(c)).astype(BF16)
        o_ref[...] = jnp.dot(sc, w_ref[...], preferred_element_type=F32) + b_ref[...]

    return _call(
        body, name=name, grid=(N // tn,),
        in_specs=[pl.BlockSpec((8, D), lambda j: (0, 0)), pl.BlockSpec((D, tn), lambda j: (0, j)),
                  pl.BlockSpec((1, tn), lambda j: (0, j))],
        out_specs=pl.BlockSpec((8, tn), lambda j: (0, j)),
        out_shape=jax.ShapeDtypeStruct((8, N), F32), compiler_params=_params(),
    )(c8, w_ada, b_ada)


def _mod_bwd(cT, dmod8, nb, name):
    D = cT.shape[0]
    N = dmod8.shape[1]
    tn = _tile(N, 512)

    def body(c_ref, d_ref, w_ref, b_ref):
        c = c_ref[...]
        sc = c * _sigmoid(c)
        d = d_ref[...]
        acc = sc[:, 0:1] * d[0:1, :]
        bsum = d[0:1, :]
        for b in range(1, nb):
            acc = acc + sc[:, b:b + 1] * d[b:b + 1, :]
            bsum = bsum + d[b:b + 1, :]
        w_ref[...] = acc
        b_ref[...] = bsum

    return _call(
        body, name=name, grid=(N // tn,),
        in_specs=[pl.BlockSpec((D, 8), lambda j: (0, 0)), pl.BlockSpec((8, tn), lambda j: (0, j))],
        out_specs=[pl.BlockSpec((D, tn), lambda j: (0, j)), pl.BlockSpec((1, tn), lambda j: (0, j))],
        out_shape=[jax.ShapeDtypeStruct((D, N), F32), jax.ShapeDtypeStruct((1, N), F32)],
        compiler_params=_params(),
    )(cT, dmod8)


def _ln_proj(x, shift, scale, g, w, S, name):
    T, D = x.shape
    N = w.shape[1]
    tm = _tile(S, 2048)
    tn = _tile(N, 1024)
    per_b = S // tm

    def body(x_ref, sh_ref, sc_ref, g_ref, w_ref, p_ref, h_ref):
        @pl.when(pl.program_id(1) == 0)
        def _():
            xv = x_ref[...]
            r = lax.rsqrt(jnp.mean(xv * xv, axis=-1, keepdims=True) + NORM_EPS)
            h = (xv * r) * g_ref[...] * (1.0 + sc_ref[0]) + sh_ref[0]
            h_ref[...] = h.astype(BF16)

        p_ref[...] = jnp.dot(h_ref[...], w_ref[...], preferred_element_type=F32).astype(BF16)

    return _call(
        body, name=name, grid=(T // tm, N // tn),
        in_specs=[pl.BlockSpec((tm, D), lambda i, j: (i, 0)),
                  pl.BlockSpec((1, 1, D), lambda i, j: (i // per_b, 0, 0)),
                  pl.BlockSpec((1, 1, D), lambda i, j: (i // per_b, 0, 0)),
                  pl.BlockSpec((1, D), lambda i, j: (0, 0)),
                  pl.BlockSpec((D, tn), lambda i, j: (0, j))],
        out_specs=[pl.BlockSpec((tm, tn), lambda i, j: (i, j)), pl.BlockSpec((tm, D), lambda i, j: (i, 0))],
        out_shape=[jax.ShapeDtypeStruct((T, N), BF16), jax.ShapeDtypeStruct((T, D), BF16)],
        compiler_params=_params(),
    )(x, shift, scale, g, w)


def _ln_bwd(dhs, x, dxo, scale, g, S, name):
    T, D = x.shape
    B = T // S
    tm = _tile(S, 512)
    per_b = S // tm

    nd = len(dhs)

    def body(*refs):
        x_ref, dxo_ref, sc_ref, g_ref, dx_ref, dsh_ref, dsc_ref, dg_ref = refs[nd:]
        i = pl.program_id(0)
        xv = x_ref[...]
        dh_v = refs[0][...]
        for r in refs[1:nd]:
            dh_v = dh_v + r[...]
        r = lax.rsqrt(jnp.mean(xv * xv, axis=-1, keepdims=True) + NORM_EPS)
        xn = xv * r
        gv = g_ref[...]
        one_sc = 1.0 + sc_ref[0]
        dhxn = dh_v * xn

        @pl.when(i % per_b == 0)
        def _():
            dsh_ref[...] = jnp.zeros_like(dsh_ref)
            dsc_ref[...] = jnp.zeros_like(dsc_ref)

        @pl.when(i == 0)
        def _():
            dg_ref[...] = jnp.zeros_like(dg_ref)

        dsh_ref[0] += jnp.sum(dh_v, axis=0, keepdims=True)
        dsc_ref[0] += jnp.sum(dhxn, axis=0, keepdims=True) * gv
        dg_ref[...] += jnp.sum(dhxn, axis=0, keepdims=True) * one_sc
        dxn = dh_v * (gv * one_sc)
        dx_ref[...] = r * (dxn - xn * jnp.mean(dxn * xn, axis=-1, keepdims=True)) + dxo_ref[...]

    row = pl.BlockSpec((tm, D), lambda i: (i, 0))
    per = pl.BlockSpec((1, 1, D), lambda i: (i // per_b, 0, 0))
    vec = pl.BlockSpec((1, D), lambda i: (0, 0))
    return _call(
        body, name=name, grid=(T // tm,),
        in_specs=[row] * (nd + 2) + [per, vec], out_specs=[row, per, per, vec],
        out_shape=[jax.ShapeDtypeStruct((T, D), F32), jax.ShapeDtypeStruct((B, 1, D), F32),
                   jax.ShapeDtypeStruct((B, 1, D), F32), jax.ShapeDtypeStruct((1, D), F32)],
        compiler_params=_params(),
    )(*dhs, x, dxo, scale, g)


def _gate_out(o, proj, w_out, x, gate, S, name):
    T, DI = o.shape
    D = w_out.shape[1]
    tm = _tile(S, 256)
    per_b = S // tm

    def body(o_ref, z_ref, w_ref, x_ref, g_ref, xo_ref, y_ref, u_ref):
        z = z_ref[...].astype(F32)
        u = (o_ref[...] * (z * _sigmoid(z))).astype(BF16)
        u_ref[...] = u
        y = jnp.dot(u, w_ref[...], preferred_element_type=F32)
        y_ref[...] = y
        xo_ref[...] = x_ref[...] + g_ref[0] * y

    wide = pl.BlockSpec((tm, DI), lambda i: (i, 0))
    row = pl.BlockSpec((tm, D), lambda i: (i, 0))
    return _call(
        body, name=name, grid=(T // tm,),
        in_specs=[wide, pl.BlockSpec((tm, DI), lambda i: (i, 3)), pl.BlockSpec((DI, D), lambda i: (0, 0)), row,
                  pl.BlockSpec((1, 1, D), lambda i: (i // per_b, 0, 0))],
        out_specs=[row, row, wide],
        out_shape=[jax.ShapeDtypeStruct((T, D), F32), jax.ShapeDtypeStruct((T, D), F32),
                   jax.ShapeDtypeStruct((T, DI), BF16)],
        compiler_params=_params(),
    )(o, proj, w_out, x, gate)


def _out_bwd(dxo, y, gate, w_out, o, proj, S, name):
    T, D = dxo.shape
    DI = o.shape[1]
    B = T // S
    tm = _tile(S, 256)
    per_b = S // tm

    def body(dxo_ref, y_ref, g_ref, w_ref, o_ref, z_ref, dy_ref, do_ref, dz_ref, dg_ref):
        dxo_v = dxo_ref[...]
        dy = (dxo_v * g_ref[0]).astype(BF16)
        dy_ref[...] = dy
        du = lax.dot_general(dy, w_ref[...], NT, preferred_element_type=F32)
        z = z_ref[...].astype(F32)
        sg = _sigmoid(z)
        do_ref[...] = (du * (z * sg)).astype(BF16)
        dz_ref[...] = (du * o_ref[...] * (sg * (1.0 + z * (1.0 - sg)))).astype(BF16)

        @pl.when(pl.program_id(0) % per_b == 0)
        def _():
            dg_ref[...] = jnp.zeros_like(dg_ref)

        dg_ref[0] += jnp.sum(dxo_v * y_ref[...], axis=0, keepdims=True)

    wide = pl.BlockSpec((tm, DI), lambda i: (i, 0))
    row = pl.BlockSpec((tm, D), lambda i: (i, 0))
    per = pl.BlockSpec((1, 1, D), lambda i: (i // per_b, 0, 0))
    return _call(
        body, name=name, grid=(T // tm,),
        in_specs=[row, row, per, pl.BlockSpec((DI, D), lambda i: (0, 0)), wide,
                  pl.BlockSpec((tm, DI), lambda i: (i, 3))],
        out_specs=[row, wide, wide, per],
        out_shape=[jax.ShapeDtypeStruct((T, D), BF16), jax.ShapeDtypeStruct((T, DI), BF16),
                   jax.ShapeDtypeStruct((T, DI), BF16), jax.ShapeDtypeStruct((B, 1, D), F32)],
        compiler_params=_params(),
    )(dxo, y, gate, w_out, o, proj)


def _final_loss(x, tgt, g, S, name):
    T, D = x.shape
    tm = _tile(S, 512)

    def body(x_ref, t_ref, g_ref, dx_ref, dg_ref, l_ref):
        @pl.when(pl.program_id(0) == 0)
        def _():
            dg_ref[...] = jnp.zeros_like(dg_ref)
            l_ref[...] = jnp.zeros_like(l_ref)

        xv = x_ref[...]
        gv = g_ref[...]
        r = lax.rsqrt(jnp.mean(xv * xv, axis=-1, keepdims=True) + NORM_EPS)
        xn = xv * r
        e = xn * gv - t_ref[...]
        part = jnp.sum(jnp.sum(e * e, axis=0, keepdims=True), axis=1, keepdims=True)
        l_ref[...] += (0.5 / D) * part
        dy = e * (1.0 / D)
        dg_ref[...] += jnp.sum(dy * xn, axis=0, keepdims=True)
        dxn = dy * gv
        dx_ref[...] = r * (dxn - xn * jnp.mean(dxn * xn, axis=-1, keepdims=True))

    row = pl.BlockSpec((tm, D), lambda i: (i, 0))
    return _call(
        body, name=name, grid=(T // tm,),
        in_specs=[row, row, pl.BlockSpec((1, D), lambda i: (0, 0))],
        out_specs=[row, pl.BlockSpec((1, D), lambda i: (0, 0)), pl.BlockSpec((1, LANES), lambda i: (0, 0))],
        out_shape=[jax.ShapeDtypeStruct((T, D), F32), jax.ShapeDtypeStruct((1, D), F32),
                   jax.ShapeDtypeStruct((1, LANES), F32)],
        compiler_params=_params(),
    )(x, tgt, g)


def _cum_fwd(fl, bf, name):
    B, S, _ = fl.shape
    ch = _tile(S, 256, 8)

    def body(fl_ref, b_ref, cum_ref):
        ri = lax.broadcasted_iota(jnp.int32, (ch, ch), 0)
        ci = lax.broadcasted_iota(jnp.int32, (ch, ch), 1)
        tri = jnp.where(ri >= ci, 1.0, 0.0).astype(BF16)

        def step(i, carry):
            r0 = pl.multiple_of(i * ch, ch)
            z = fl_ref[0, pl.ds(r0, ch), :] + b_ref[...]
            lf = (jnp.minimum(z, 0.0) - jnp.log(1.0 + jnp.exp(-jnp.abs(z)))) * LOG2E
            hi, mid, lo = _split3(lf)
            cs = (jnp.dot(tri, hi, preferred_element_type=F32) + jnp.dot(tri, mid, preferred_element_type=F32)
                  + jnp.dot(tri, lo, preferred_element_type=F32)) + carry
            cum_ref[0, pl.ds(r0, ch), :] = cs
            return cs[ch - 1:ch, :]

        lax.fori_loop(0, S // ch, step, jnp.zeros((1, LANES), F32))

    blk = pl.BlockSpec((1, S, LANES), lambda b: (b, 0, 0))
    return _call(
        body, name=name, grid=(B,), in_specs=[blk, pl.BlockSpec((1, LANES), lambda b: (0, 0))], out_specs=blk,
        out_shape=jax.ShapeDtypeStruct((B, S, LANES), F32), compiler_params=_params(),
    )(fl, bf)


def _cum_bwd(dcs, fl, bf, name):
    B, S, _ = fl.shape
    ch = _tile(S, 256, 8)
    n = S // ch

    def body(d_ref, fl_ref, b_ref, o_ref, db_ref):
        ri = lax.broadcasted_iota(jnp.int32, (ch, ch), 0)
        ci = lax.broadcasted_iota(jnp.int32, (ch, ch), 1)
        tri = jnp.where(ci >= ri, 1.0, 0.0).astype(BF16)

        @pl.when(pl.program_id(0) == 0)
        def _():
            db_ref[...] = jnp.zeros_like(db_ref)

        def step(t, carry):
            tail, dbsum = carry
            r0 = pl.multiple_of((n - 1 - t) * ch, ch)
            hi, mid, lo = _split3(d_ref[0, pl.ds(r0, ch), :])
            suf = (jnp.dot(tri, hi, preferred_element_type=F32) + jnp.dot(tri, mid, preferred_element_type=F32)
                   + jnp.dot(tri, lo, preferred_element_type=F32)) + tail
            z = fl_ref[0, pl.ds(r0, ch), :] + b_ref[...]
            dfl = -suf * _sigmoid(-z)
            o_ref[0, pl.ds(r0, ch), :] = dfl
            return suf[0:1, :], dbsum + jnp.sum(dfl, axis=0, keepdims=True)

        z1 = jnp.zeros((1, LANES), F32)
        _, dbsum = lax.fori_loop(0, n, step, (z1, z1))
        db_ref[...] += dbsum

    blk = pl.BlockSpec((1, S, LANES), lambda b: (b, 0, 0))
    vec = pl.BlockSpec((1, LANES), lambda b: (0, 0))
    return _call(
        body, name=name, grid=(B,), in_specs=[blk, blk, vec], out_specs=[blk, vec],
        out_shape=[jax.ShapeDtypeStruct((B, S, LANES), F32), jax.ShapeDtypeStruct((1, LANES), F32)],
        compiler_params=_params(),
    )(dcs, fl, bf)


HEADS_PER_STEP = 4
GROUP = 2 * HEAD_DIM


def _step_width():
    return HEAD_DIM * HEADS_PER_STEP


def _cols(S, offset_blocks=0):
    return pl.BlockSpec((S, _step_width()), lambda b, h: (b, offset_blocks + h))


def _row_spec(nq, tq):
    return pl.BlockSpec((1, HEADS_PER_STEP, nq, 1, tq), lambda b, h: (b, h, 0, 0, 0))


def _lanes(g):
    return slice(GROUP * (g // 2), GROUP * (g // 2) + GROUP)


def _hi_lo(x):
    hi = x.astype(BF16)
    return hi, (x - hi.astype(F32)).astype(BF16)


def _dot(a, b, dims=None):
    if dims is None:
        return jnp.dot(a, b, preferred_element_type=F32)
    return lax.dot_general(a, b, dims, preferred_element_type=F32)


def _causal_blocks(nq, prep, init, stages, finish, combine=None, descending=False):
    heads = range(HEADS_PER_STEP)

    def qloop(qi, _):
        ctx = [prep(g, qi) for g in heads]

        def step(kj, carry, masked):
            st = list(carry)
            for n, stage in enumerate(stages):
                if combine is not None and n == len(stages) - 1:
                    combine(kj, ctx, st)
                st = [stage(g, ctx[g], kj, masked, st[g]) for g in heads]
            return tuple(st)

        carry = tuple(init() for _ in heads)
        if descending:
            carry = step(qi, carry, True)
            carry = lax.fori_loop(0, qi, lambda t, cr: step(qi - 1 - t, cr, False), carry)
        else:
            carry = lax.fori_loop(0, qi, lambda kj, cr: step(kj, cr, False), carry)
            carry = step(qi, carry, True)
        finish(qi, ctx, carry)
        return 0

    lax.fori_loop(0, nq, qloop, 0)


class _Block:
    def __init__(self, tq):
        self.tq = tq
        self.lane = lax.broadcasted_iota(jnp.int32, (tq, GROUP), 1)
        self.low = self.lane < HEAD_DIM
        self.ri = lax.broadcasted_iota(jnp.int32, (tq, tq), 0)
        self.ci = lax.broadcasted_iota(jnp.int32, (tq, tq), 1)

    def rows(self, i):
        return pl.ds(pl.multiple_of(i * self.tq, self.tq), self.tq)

    def own(self, g, x):
        return jnp.where(self.low if g % 2 == 0 else jnp.logical_not(self.low), x, jnp.zeros_like(x))

    def pair(self, a, b):
        return jnp.where(self.low, a, b)

    def stat(self, g, x):
        return jnp.sum(jnp.where(self.lane == HEAD_DIM * (g % 2), x, 0.0), axis=1, keepdims=True)


def _fox_fwd(proj, cumcol, cumrow, name):
    T, DI = proj.shape[0], proj.shape[1] // 4
    B, H, nq, _, tq = cumrow.shape
    S = nq * tq
    nb = DI // _step_width()

    def body(q_ref, k_ref, v_ref, cc_ref, cr_ref, o_ref, st_ref, acc_scr):
        h0 = pl.program_id(1) * HEADS_PER_STEP
        blk = _Block(tq)

        def prep(g, qi):
            acc_scr[g] = jnp.zeros((tq, GROUP), F32)
            q = blk.own(g, q_ref[blk.rows(qi), _lanes(g)])
            ccol = jnp.sum(jnp.where(blk.lane == h0 + g, cc_ref[0, blk.rows(qi), :], 0.0), axis=1, keepdims=True)
            return q, ccol

        def init():
            return jnp.full((tq, 1), -jnp.inf, F32), jnp.zeros((tq, 1), F32)

        def scores(g, ctx, kj, masked, st):
            return st + (_dot(ctx[0], k_ref[blk.rows(kj), _lanes(g)], NT),)

        def softmax(g, ctx, kj, masked, st):
            m, l, s = st
            s = s + ctx[1] - cr_ref[0, g, kj]
            if masked:
                s = jnp.where(blk.ci <= blk.ri, s, -jnp.inf)
            m_new = jnp.maximum(m, jnp.max(s, axis=1, keepdims=True))
            alpha = jnp.exp2(m - m_new)
            p = jnp.exp2(s - m_new)
            return (m_new, alpha * l + jnp.sum(p, axis=1, keepdims=True), alpha) + _hi_lo(p)

        def values(g, ctx, kj, masked, st):
            m, l, alpha, hi, lo = st
            v = v_ref[blk.rows(kj), _lanes(g)]
            acc_scr[g] = alpha * acc_scr[g] + (_dot(hi, v) + _dot(lo, v))
            return m, l

        def finish(qi, ctx, carry):
            for g in range(0, HEADS_PER_STEP, 2):
                (m0, l0), (m1, l1) = carry[g], carry[g + 1]
                o_ref[blk.rows(qi), _lanes(g)] = blk.pair(acc_scr[g] / l0, acc_scr[g + 1] / l1)
                st_ref[blk.rows(qi), _lanes(g)] = blk.pair(m0 + jnp.log2(l0), m1 + jnp.log2(l1))

        _causal_blocks(nq, prep, init, [scores, softmax, values], finish)

    out = jax.ShapeDtypeStruct((T, DI), F32)
    return _call(
        body, name=name, grid=(B, H // HEADS_PER_STEP),
        in_specs=[_cols(S), _cols(S, nb), _cols(S, 2 * nb), pl.BlockSpec((1, S, LANES), lambda b, h: (b, 0, 0)),
                  _row_spec(nq, tq)],
        out_specs=[_cols(S), _cols(S)], out_shape=[out, out],
        scratch_shapes=[pltpu.VMEM((HEADS_PER_STEP, tq, GROUP), F32)], compiler_params=_params(),
    )(proj, proj, proj, cumcol, cumrow)


def _fox_bwd(proj, do, dzg, o, stat, cumcol, cumrow, name):
    T, DI = do.shape
    B, H, nq, _, tq = cumrow.shape
    S = nq * tq
    nb = DI // _step_width()

    def body(q_ref, k_ref, v_ref, do_ref, dz_ref, o_ref, st_ref, cc_ref, cr_ref, dqkv_ref, dcs_ref, dk_acc, dv_acc,
             dq_scr):
        h0 = pl.program_id(1) * HEADS_PER_STEP
        blk = _Block(tq)
        dk_acc[...] = jnp.zeros_like(dk_acc)
        dv_acc[...] = jnp.zeros_like(dv_acc)
        dcs_ref[...] = jnp.zeros_like(dcs_ref)

        def prep(g, qi):
            dq_scr[g] = jnp.zeros((tq, GROUP), F32)
            q = blk.own(g, q_ref[blk.rows(qi), _lanes(g)])
            dout = blk.own(g, do_ref[blk.rows(qi), _lanes(g)])
            delta = jnp.sum(o_ref[blk.rows(qi), _lanes(g)] * dout.astype(F32), axis=1, keepdims=True)
            lse = blk.stat(g, st_ref[blk.rows(qi), _lanes(g)])
            ccol = jnp.sum(jnp.where(blk.lane == h0 + g, cc_ref[0, blk.rows(qi), :], 0.0), axis=1, keepdims=True)
            return q, dout, lse, delta, ccol

        def init():
            return ()

        def scores(g, ctx, kj, masked, st):
            return (_dot(ctx[0], k_ref[blk.rows(kj), _lanes(g)], NT), _dot(ctx[1], v_ref[blk.rows(kj), _lanes(g)], NT))

        def softmax_bwd(g, ctx, kj, masked, st):
            s, dp = st
            _, _, lse, delta, ccol = ctx
            s = s + ccol - cr_ref[0, g, kj]
            if masked:
                s = jnp.where(blk.ci <= blk.ri, s, -jnp.inf)
            p = jnp.exp2(s - lse)
            ds = p * (dp - delta)
            return p.astype(BF16), ds.astype(BF16), jnp.sum(ds, axis=0, keepdims=True)

        def combine(kj, ctx, st):
            for g in range(0, HEADS_PER_STEP, 2):
                dv_acc[blk.rows(kj), _lanes(g)] += _dot(st[g][0], ctx[g][1], TN) + _dot(st[g + 1][0], ctx[g + 1][1], TN)
                dk_acc[blk.rows(kj), _lanes(g)] += _dot(st[g][1], ctx[g][0], TN) + _dot(st[g + 1][1], ctx[g + 1][0], TN)
            for g in range(HEADS_PER_STEP):
                dcs_ref[0, g, kj] += st[g][2]

        def queries(g, ctx, kj, masked, st):
            dq_scr[g] += _dot(st[1], blk.own(g, k_ref[blk.rows(kj), _lanes(g)]))
            return ()

        def finish(qi, ctx, carry):
            for g in range(0, HEADS_PER_STEP, 2):
                dqkv_ref[0, blk.rows(qi), _lanes(g)] = ((dq_scr[g] + dq_scr[g + 1]) * LN2).astype(BF16)

        _causal_blocks(nq, prep, init, [scores, softmax_bwd, queries], finish, combine=combine)
        dqkv_ref[1] = (dk_acc[...] * LN2).astype(BF16)
        dqkv_ref[2] = dv_acc[...].astype(BF16)
        dqkv_ref[3] = dz_ref[...]

    W = _step_width()
    return _call(
        body, name=name, grid=(B, H // HEADS_PER_STEP),
        in_specs=[_cols(S), _cols(S, nb), _cols(S, 2 * nb), _cols(S), _cols(S), _cols(S), _cols(S),
                  pl.BlockSpec((1, S, LANES), lambda b, h: (b, 0, 0)), _row_spec(nq, tq)],
        out_specs=[pl.BlockSpec((4, S, W), lambda b, h: (0, b, h)), _row_spec(nq, tq)],
        out_shape=[jax.ShapeDtypeStruct((4, T, DI), BF16), jax.ShapeDtypeStruct((B, H, nq, 1, tq), F32)],
        scratch_shapes=[pltpu.VMEM((S, W), F32), pltpu.VMEM((S, W), F32), pltpu.VMEM((HEADS_PER_STEP, tq, GROUP), F32)],
        compiler_params=_params(),
    )(proj, proj, proj, do, dzg, o, stat, cumcol, cumrow)


def _log2_keep(z2):
    nz = -z2
    e = jnp.exp2(jnp.minimum(z2, nz))
    return jnp.minimum(nz, 0.0) - jnp.log2(1.0 + e), e


def _sb_fwd(proj, B, tq, name):
    T, DI = proj.shape[0], proj.shape[1] // 4
    S = T // B
    H = DI // HEAD_DIM
    nq = S // tq
    nb = DI // _step_width()

    def body(q_ref, k_ref, v_ref, o_ref, st_ref, acc_scr, c_scr):
        blk = _Block(tq)
        strict = blk.ci < blk.ri
        above = jnp.where(blk.ri > blk.ci, 1.0, 0.0).astype(BF16)

        def prep(g, qi):
            acc_scr[g] = jnp.zeros((tq, GROUP), F32)
            c_scr[g] = jnp.zeros((tq, 1), F32)
            return blk.own(g, q_ref[blk.rows(qi), _lanes(g)])

        def init():
            return ()

        def scores(g, q, kj, masked, st):
            return (_dot(q, k_ref[blk.rows(kj), _lanes(g)], NT),)

        def logs(g, q, kj, masked, st):
            (z,) = st
            lk, _ = _log2_keep(z)
            lb = z + lk
            if masked:
                lk = jnp.where(strict, lk, 0.0)
            c = c_scr[g]
            c_scr[g] = c + jnp.sum(lk, axis=1, keepdims=True)
            return (lb + c,) + _hi_lo(lk)

        def suffix(g, q, kj, masked, st):
            lbc, hi, lo = st
            return lbc, _dot(hi, above) + _dot(lo, above)

        def weights(g, q, kj, masked, st):
            lbc, after = st
            a = jnp.exp2(lbc + after)
            if masked:
                a = jnp.where(strict, a, 0.0)
            return (a.astype(BF16),)

        def values(g, q, kj, masked, st):
            acc_scr[g] += _dot(st[0], v_ref[blk.rows(kj), _lanes(g)])
            return ()

        def finish(qi, ctx, carry):
            for g in range(0, HEADS_PER_STEP, 2):
                o_ref[blk.rows(qi), _lanes(g)] = blk.pair(acc_scr[g], acc_scr[g + 1])
                st_ref[blk.rows(qi), _lanes(g)] = blk.pair(c_scr[g], c_scr[g + 1])

        _causal_blocks(nq, prep, init, [scores, logs, suffix, weights, values], finish, descending=True)

    out = jax.ShapeDtypeStruct((T, DI), F32)
    return _call(
        body, name=name, grid=(B, H // HEADS_PER_STEP), in_specs=[_cols(S), _cols(S, nb), _cols(S, 2 * nb)],
        out_specs=[_cols(S), _cols(S)], out_shape=[out, out],
        scratch_shapes=[pltpu.VMEM((HEADS_PER_STEP, tq, GROUP), F32), pltpu.VMEM((HEADS_PER_STEP, tq, 1), F32)],
        compiler_params=_params(),
    )(proj, proj, proj)


def _sb_bwd(proj, do, dzg, stat, B, tq, name):
    T, DI = do.shape
    S = T // B
    H = DI // HEAD_DIM
    nq = S // tq
    nb = DI // _step_width()

    def body(q_ref, k_ref, v_ref, do_ref, dz_ref, st_ref, dqkv_ref, dk_acc, dv_acc, dq_scr):
        blk = _Block(tq)
        strict = blk.ci < blk.ri
        upto = jnp.where(blk.ri <= blk.ci, 1.0, 0.0).astype(BF16)
        before = jnp.where(blk.ri < blk.ci, 1.0, 0.0).astype(BF16)
        dk_acc[...] = jnp.zeros_like(dk_acc)
        dv_acc[...] = jnp.zeros_like(dv_acc)

        def prep(g, qi):
            dq_scr[g] = jnp.zeros((tq, GROUP), F32)
            return (blk.own(g, q_ref[blk.rows(qi), _lanes(g)]), blk.own(g, do_ref[blk.rows(qi), _lanes(g)]),
                    blk.stat(g, st_ref[blk.rows(qi), _lanes(g)]))

        def init():
            return jnp.zeros((tq, 1), F32), jnp.zeros((tq, 1), F32)

        def scores(g, ctx, kj, masked, st):
            return st + (_dot(ctx[0], k_ref[blk.rows(kj), _lanes(g)], NT),
                         _dot(ctx[1], v_ref[blk.rows(kj), _lanes(g)], NT))

        def logs(g, ctx, kj, masked, st):
            cpre, pg, z, da = st
            lk, e = _log2_keep(z)
            inv = 1.0 / (1.0 + e)
            sig = jnp.where(z >= 0.0, inv, e * inv)
            lbt = (z + lk) + (ctx[2] - cpre)
            if masked:
                lk = jnp.where(strict, lk, 0.0)
            return (cpre + jnp.sum(lk, axis=1, keepdims=True), pg, da, lbt, sig) + _hi_lo(lk)

        def prefix(g, ctx, kj, masked, st):
            cpre, pg, da, lbt, sig, hi, lo = st
            return cpre, pg, da, lbt, sig, _dot(hi, upto) + _dot(lo, upto)

        def weights(g, ctx, kj, masked, st):
            cpre, pg, da, lbt, sig, pre = st
            a = jnp.exp2(lbt - pre)
            if masked:
                a = jnp.where(strict, a, 0.0)
            gr = da * a
            return cpre, pg, sig, a.astype(BF16), gr, gr.astype(BF16)

        def grad_prefix(g, ctx, kj, masked, st):
            cpre, pg, sig, ab, gr, gb = st
            return cpre, pg, sig, ab, gr, _dot(gb, before)

        def dlogits(g, ctx, kj, masked, st):
            cpre, pg, sig, ab, gr, pfx = st
            dz = gr - sig * (gr + (pfx + pg))
            if masked:
                dz = jnp.where(strict, dz, 0.0)
            return cpre, pg + jnp.sum(gr, axis=1, keepdims=True), ab, dz.astype(BF16)

        def combine(kj, ctx, st):
            for g in range(0, HEADS_PER_STEP, 2):
                dv_acc[blk.rows(kj), _lanes(g)] += _dot(st[g][2], ctx[g][1], TN) + _dot(st[g + 1][2], ctx[g + 1][1], TN)
                dk_acc[blk.rows(kj), _lanes(g)] += _dot(st[g][3], ctx[g][0], TN) + _dot(st[g + 1][3], ctx[g + 1][0], TN)

        def queries(g, ctx, kj, masked, st):
            cpre, pg, _, dzb = st
            dq_scr[g] += _dot(dzb, blk.own(g, k_ref[blk.rows(kj), _lanes(g)]))
            return cpre, pg

        def finish(qi, ctx, carry):
            for g in range(0, HEADS_PER_STEP, 2):
                dqkv_ref[0, blk.rows(qi), _lanes(g)] = ((dq_scr[g] + dq_scr[g + 1]) * LN2).astype(BF16)

        _causal_blocks(nq, prep, init, [scores, logs, prefix, weights, grad_prefix, dlogits, queries], finish,
                       combine=combine)
        dqkv_ref[1] = (dk_acc[...] * LN2).astype(BF16)
        dqkv_ref[2] = dv_acc[...].astype(BF16)
        dqkv_ref[3] = dz_ref[...]

    W = _step_width()
    return _call(
        body, name=name, grid=(B, H // HEADS_PER_STEP),
        in_specs=[_cols(S), _cols(S, nb), _cols(S, 2 * nb), _cols(S), _cols(S), _cols(S)],
        out_specs=pl.BlockSpec((4, S, W), lambda b, h: (0, b, h)),
        out_shape=jax.ShapeDtypeStruct((4, T, DI), BF16),
        scratch_shapes=[pltpu.VMEM((S, W), F32), pltpu.VMEM((S, W), F32), pltpu.VMEM((HEADS_PER_STEP, tq, GROUP), F32)],
        compiler_params=_params(),
    )(proj, proj, proj, do, dzg, stat)


def _row_tile(R, C, n_arrays):
    budget = 24 * 1024 * 1024 // (2 * n_arrays * 4 * max(C, LANES))
    return _tile(R, max(8, budget), 8)


def _ew_sum(parts, name, also_bf16=False):
    R, C = parts[0].shape
    tr = _row_tile(R, C, len(parts) + 2)
    n = len(parts)

    def body(*refs):
        acc = refs[0][...].astype(F32) + refs[1][...].astype(F32)
        for r in refs[2:n]:
            acc = acc + r[...].astype(F32)
        refs[n][...] = acc
        if also_bf16:
            refs[n + 1][...] = acc.astype(BF16)

    blk = pl.BlockSpec((tr, C), lambda i: (i, 0))
    out_shape = [jax.ShapeDtypeStruct((R, C), F32)] + ([jax.ShapeDtypeStruct((R, C), BF16)] if also_bf16 else [])
    return _call(
        body, name=name, grid=(R // tr,), in_specs=[blk] * n, out_specs=[blk] * len(out_shape),
        out_shape=out_shape, compiler_params=_params(),
    )(*parts)


def _adamw(w, g, m, v, name):
    R, C = w.shape
    tr = _row_tile(R, C, 7)
    c1 = 1.0 / (1.0 - ADAM_B1 ** ADAM_STEP)
    c2 = 1.0 / (1.0 - ADAM_B2 ** ADAM_STEP)

    def body(w_ref, g_ref, m_ref, v_ref, d_ref, m2_ref, v2_ref):
        gv = g_ref[...]
        m2 = ADAM_B1 * m_ref[...] + (1.0 - ADAM_B1) * gv
        v2 = ADAM_B2 * v_ref[...] + (1.0 - ADAM_B2) * (gv * gv)
        m2_ref[...] = m2
        v2_ref[...] = v2
        d_ref[...] = -ADAM_LR * ((m2 * c1) / (jnp.sqrt(v2 * c2) + ADAM_EPS) + ADAM_WD * w_ref[...])

    blk = pl.BlockSpec((tr, C), lambda i: (i, 0))
    out = jax.ShapeDtypeStruct((R, C), F32)
    return _call(
        body, name=name, grid=(R // tr,), in_specs=[blk] * 4, out_specs=[blk] * 3, out_shape=[out] * 3,
        compiler_params=_params(),
    )(w, g, m, v)


def _me():
    return lax.axis_index("x"), lax.axis_index("y"), lax.axis_index("c")


def _chip_of(x, y):
    return 2 * x + y


def _other_chips(x, y):
    return [(x, 1 - y), (1 - x, y), (1 - x, 1 - y)]


def _gather_weights(halves, smalls):
    nh, ns = len(halves), len(smalls)

    def body(*refs):
        ins_h, ins_s = refs[:nh], refs[nh:nh + ns]
        outs_h, outs_s = refs[nh + ns:2 * nh + ns], refs[2 * nh + ns:2 * (nh + ns)]
        send1, recv1, send2, recv2, send3, recv3 = refs[2 * (nh + ns):]
        x, y, c = _me()
        mine = _chip_of(x, y)
        chips = _other_chips(x, y)
        sib = (x, y, 1 - c)

        def landed(i, k, half):
            return outs_h[i].at[_chip_of(*chips[k]), half]

        def first(i, k):
            return pltpu.make_async_remote_copy(
                src_ref=ins_h[i].at[c], dst_ref=outs_h[i].at[mine, c], send_sem=send1.at[i, k], recv_sem=recv1.at[i, k],
                device_id=(*chips[k], c), device_id_type=MESH)

        def passed(i, k):
            return pltpu.make_async_remote_copy(
                src_ref=landed(i, k, c), dst_ref=landed(i, k, c), send_sem=send2.at[i, k], recv_sem=recv2.at[i, k],
                device_id=sib, device_id_type=MESH)

        def small(i, k):
            return pltpu.make_async_remote_copy(
                src_ref=ins_s[i], dst_ref=outs_s[i].at[mine], send_sem=send3.at[i, k], recv_sem=recv3.at[i, k],
                device_id=(*chips[k], c), device_id_type=MESH)

        for i in range(nh):
            for k in range(3):
                first(i, k).start()
        for i in range(ns):
            for k in range(3):
                small(i, k).start()
        for i in range(nh):
            for k in range(3):
                pltpu.make_async_remote_copy(
                    src_ref=ins_h[i].at[c], dst_ref=landed(i, k, c), send_sem=send1.at[i, k], recv_sem=recv1.at[i, k],
                    device_id=(*chips[k], c), device_id_type=MESH).wait_recv()
                passed(i, k).start()
        for i in range(nh):
            for k in range(3):
                pltpu.make_async_remote_copy(
                    src_ref=landed(i, k, c), dst_ref=landed(i, k, 1 - c), send_sem=send2.at[i, k],
                    recv_sem=recv2.at[i, k], device_id=sib, device_id_type=MESH).wait_recv()
        for i in range(ns):
            for k in range(3):
                pltpu.make_async_remote_copy(
                    src_ref=ins_s[i], dst_ref=outs_s[i].at[_chip_of(*chips[k])], send_sem=send3.at[i, k],
                    recv_sem=recv3.at[i, k], device_id=(*chips[k], c), device_id_type=MESH).wait_recv()
        for i in range(nh):
            for k in range(3):
                first(i, k).wait_send()
                passed(i, k).wait_send()
        for i in range(ns):
            for k in range(3):
                small(i, k).wait_send()

    out_shape = ([jax.ShapeDtypeStruct((4,) + a.shape, a.dtype) for a in halves]
                 + [jax.ShapeDtypeStruct((4,) + a.shape, a.dtype) for a in smalls])
    n = nh + ns
    res = _call(
        body, name="gather_weights", in_specs=[HBM] * n, out_specs=[HBM] * n, out_shape=out_shape,
        scratch_shapes=[pltpu.SemaphoreType.DMA((nh, 3)), pltpu.SemaphoreType.DMA((nh, 3)),
                        pltpu.SemaphoreType.DMA((nh, 3)), pltpu.SemaphoreType.DMA((nh, 3)),
                        pltpu.SemaphoreType.DMA((max(ns, 1), 3)), pltpu.SemaphoreType.DMA((max(ns, 1), 3))],
        compiler_params=_params(),
    )(*halves, *smalls)
    return res[:nh], res[nh:]


def _pair_exchange(grads):
    n = len(grads)

    def body(*refs):
        ins, got = refs[:n], refs[n:2 * n]
        send, recv = refs[2 * n:]
        x, y, c = _me()
        cps = []
        for i in range(n):
            for j in range(4):
                r = pltpu.make_async_remote_copy(
                    src_ref=ins[i].at[j, 1 - c], dst_ref=got[i].at[j], send_sem=send.at[i, j], recv_sem=recv.at[i, j],
                    device_id=(x, y, 1 - c), device_id_type=MESH)
                r.start()
                cps.append(r)
        for r in cps:
            r.wait()

    return _call(
        body, name="grad_pair_exchange", in_specs=[HBM] * n, out_specs=[HBM] * n,
        out_shape=[jax.ShapeDtypeStruct((4,) + g.shape[2:], g.dtype) for g in grads],
        scratch_shapes=[pltpu.SemaphoreType.DMA((n, 4)), pltpu.SemaphoreType.DMA((n, 4))],
        compiler_params=_params(),
    )(*grads)


def _chip_exchange(sums):
    n = len(sums)

    def body(*refs):
        ins, got = refs[:n], refs[n:2 * n]
        send, recv = refs[2 * n:]
        x, y, c = _me()
        chips = _other_chips(x, y)
        cps = []
        for i in range(n):
            for k in range(3):
                r = pltpu.make_async_remote_copy(
                    src_ref=ins[i].at[_chip_of(*chips[k])], dst_ref=got[i].at[k], send_sem=send.at[i, k],
                    recv_sem=recv.at[i, k], device_id=(*chips[k], c), device_id_type=MESH)
                r.start()
                cps.append(r)
        for r in cps:
            r.wait()

    return _call(
        body, name="grad_chip_exchange", in_specs=[HBM] * n, out_specs=[HBM] * n,
        out_shape=[jax.ShapeDtypeStruct((3,) + s.shape[1:], s.dtype) for s in sums],
        scratch_shapes=[pltpu.SemaphoreType.DMA((n, 3)), pltpu.SemaphoreType.DMA((n, 3))],
        compiler_params=_params(),
    )(*sums)


def _pair_share(halves):
    n = len(halves)

    def body(*refs):
        ins, outs = refs[:n], refs[n:2 * n]
        send, recv = refs[2 * n:]
        x, y, c = _me()
        cps = []
        for i in range(n):
            r = pltpu.make_async_remote_copy(
                src_ref=ins[i], dst_ref=outs[i], send_sem=send.at[i], recv_sem=recv.at[i],
                device_id=(x, y, 1 - c), device_id_type=MESH)
            r.start()
            cps.append(r)
        for r in cps:
            r.wait()

    return _call(
        body, name="grad_pair_share", in_specs=[HBM] * n, out_specs=[HBM] * n,
        out_shape=[jax.ShapeDtypeStruct(h.shape, h.dtype) for h in halves],
        scratch_shapes=[pltpu.SemaphoreType.DMA((n,)), pltpu.SemaphoreType.DMA((n,))],
        compiler_params=_params(),
    )(*halves)


def _allreduce_small(vec):
    P = vec.shape[1]

    def body(v_ref, sum_ref, all_ref, send, recv):
        x, y, c = _me()
        me = 4 * x + 2 * y + c
        all_ref[pl.ds(me, 1)] = v_ref[...][None]
        cps = []
        for d in range(1, 8):
            peer = (jnp.bitwise_xor(x, d >> 2), jnp.bitwise_xor(y, (d >> 1) & 1), jnp.bitwise_xor(c, d & 1))
            r = pltpu.make_async_remote_copy(
                src_ref=v_ref, dst_ref=all_ref.at[me], send_sem=send.at[d - 1], recv_sem=recv.at[d - 1],
                device_id=peer, device_id_type=MESH)
            r.start()
            cps.append(r)
        for d in range(1, 8):
            src = jnp.bitwise_xor(me, d)
            pltpu.make_async_remote_copy(
                src_ref=v_ref, dst_ref=all_ref.at[src], send_sem=send.at[d - 1], recv_sem=recv.at[d - 1],
                device_id=(x, y, c), device_id_type=MESH).wait_recv()
        for r in cps:
            r.wait_send()
        acc = all_ref[0]
        for i in range(1, 8):
            acc = acc + all_ref[i]
        sum_ref[...] = acc

    vm = pl.BlockSpec(memory_space=pltpu.VMEM)
    return _call(
        body, name="allreduce_small", in_specs=[vm], out_specs=[vm, vm],
        out_shape=[jax.ShapeDtypeStruct((8, P), F32), jax.ShapeDtypeStruct((8, 8, P), F32)],
        scratch_shapes=[pltpu.SemaphoreType.DMA((7,)), pltpu.SemaphoreType.DMA((7,))],
        compiler_params=_params(),
    )(vec)[0]


def _per_batch(mod, B, D):
    return [mod[:B, i * D:(i + 1) * D].reshape(B, 1, D) for i in range(3)]


def _pad_rows8(a):
    return jnp.concatenate([a, jnp.zeros((8 - a.shape[0],) + a.shape[1:], a.dtype)], axis=0)


def _layer_fwd(x, c8, w, S, fox, tag):
    T, D = x.shape
    B = T // S
    DI = w["w_out"].shape[0]
    H = DI // HEAD_DIM
    tq = _tile(S, FOX_BLOCK if fox else SB_BLOCK, 8)
    mod = _mod_fwd(c8, w["w_ada"], w["b_ada"], tag + "_mod_fwd")
    shift, scale, gate = _per_batch(mod, B, D)
    proj, h = _ln_proj(x, shift, scale, w["norm_g"], w["w_in"], S, tag + "_ln_proj")
    saved = dict(x=x, h=h, proj=proj, scale=scale, gate=gate)
    if fox:
        fl = _mm(h, w["w_f"], "nn", F32, tag + "_flogit").reshape(B, S, LANES)
        cum = _cum_fwd(fl, w["b_f"], tag + "_cum_fwd")
        cumrow = cum[:, :, :H].transpose(0, 2, 1).reshape(B, H, S // tq, 1, tq)
        o, stat = _fox_fwd(proj, cum, cumrow, tag + "_attn_fwd")
        saved.update(fl=fl, cum=cum, cumrow=cumrow)
    else:
        o, stat = _sb_fwd(proj, B, tq, tag + "_attn_fwd")
    xo, y, u = _gate_out(o, proj, w["w_out"], x, gate, S, tag + "_gate_out")
    saved.update(o=o, stat=stat, y=y, u=u)
    return xo, saved


def _layer_bwd(dxo, sv, w, cT, S, fox, tag):
    T, D = dxo.shape
    B = T // S
    DI = w["w_out"].shape[0]
    H = DI // HEAD_DIM
    tq = _tile(S, FOX_BLOCK if fox else SB_BLOCK, 8)
    dy, do, dzg, dgate = _out_bwd(dxo, sv["y"], sv["gate"], w["w_out"], sv["o"], sv["proj"], S, tag + "_out_bwd")
    g = {"w_out": _mm(sv["u"], dy, "tn", F32, tag + "_dw_out", tm=1024, tn=1024, tk=2048)}
    q_cols = jnp.where(jnp.arange(4 * DI)[None, :] < DI, Q_SCALE, 1.0).astype(F32)
    if fox:
        dproj, dcs = _fox_bwd(sv["proj"], do, dzg, sv["o"], sv["stat"], sv["cum"], sv["cumrow"], tag + "_attn_bwd")
        dcs = dcs.reshape(B, H, S).transpose(0, 2, 1)
        dcs = jnp.concatenate([dcs, jnp.zeros((B, S, LANES - H), F32)], axis=-1)
        dfl, db_f = _cum_bwd(dcs, sv["fl"], w["b_f"], tag + "_cum_bwd")
        g["b_f"] = db_f[:, :H]
        dfl = dfl.reshape(T, LANES).astype(BF16)
    else:
        dproj = _sb_bwd(sv["proj"], do, dzg, sv["stat"], B, tq, tag + "_attn_bwd")
    g["w_in"] = _mm(sv["h"], dproj, "tn", F32, tag + "_dw_in", tm=1024, tn=2048, tk=1024, col_scale=q_cols)
    dhs = [_mm(dproj, w["w_in"], "nt", F32, tag + "_dh", tm=2048, tn=1024, tk=1024)]
    if fox:
        dw_f = _mm(sv["h"], dfl, "tn", F32, tag + "_dw_f", tm=1024, tn=LANES, tk=2048)
        g["w_in"] = jnp.concatenate([g["w_in"], dw_f[:, :H]], axis=1)
        dhs.append(_mm(dfl, w["w_f"], "nt", F32, tag + "_dh_f", tm=2048, tn=1024, tk=LANES))
    dx, dshift, dscale, dg = _ln_bwd(dhs, sv["x"], dxo, sv["scale"], w["norm_g"], S, tag + "_ln_bwd")
    g["norm_g"] = dg
    dmod = jnp.concatenate([dshift, dscale, dgate], axis=-1).reshape(B, 3 * D)
    g["w_ada"], g["b_ada"] = _mod_bwd(cT, _pad_rows8(dmod), B, tag + "_mod_bwd")
    return dx, g


def _local_step(x3, c, tgt3, wf, ws, final_g):
    B, S, D = x3.shape
    T = B * S
    x = x3.reshape(T, D)
    c8 = _pad_rows8(c)
    cT = c8.T
    x1, sv1 = _layer_fwd(x, c8, wf, S, True, "fox")
    x2, sv2 = _layer_fwd(x1, c8, ws, S, False, "sb")
    dx2, dgf, loss = _final_loss(x2, tgt3.reshape(T, D), final_g, S, "final_loss")
    dx1, gs = _layer_bwd(dx2, sv2, ws, cT, S, False, "sb")
    dx0, gf = _layer_bwd(dx1, sv1, wf, cT, S, True, "fox")
    return loss, dx0.reshape(B, S, D), gf, gs, dgf


def _cols_to_shards(a):
    R, C4 = a.shape
    return a.reshape(R, 4, C4 // 4).transpose(1, 0, 2)


def _shards_to_cols(a):
    n, R, C = a.shape
    return a.transpose(1, 0, 2).reshape(R, n * C)


def kernel(x, c, fox_norm_g, fox_w_ada, fox_b_ada, fox_w_in, fox_b_f, fox_w_out, sb_norm_g, sb_w_ada, sb_b_ada, sb_w_in, sb_w_out, final_norm_g, loss_target, m_fox_norm_g, m_fox_w_ada, m_fox_b_ada, m_fox_w_in, m_fox_b_f, m_fox_w_out, m_sb_norm_g, m_sb_w_ada, m_sb_b_ada, m_sb_w_in, m_sb_w_out, m_final_norm_g, v_fox_norm_g, v_fox_w_ada, v_fox_b_ada, v_fox_w_in, v_fox_b_f, v_fox_w_out, v_sb_norm_g, v_sb_w_ada, v_sb_b_ada, v_sb_w_in, v_sb_w_out, v_final_norm_g):
    B, S, D = x.shape
    DI = 4 * fox_w_out.shape[1]
    H = DI // HEAD_DIM
    chip = _chip_of(lax.axis_index("x"), lax.axis_index("y"))

    big_names = ["fox_w_ada", "fox_w_in", "fox_w_out", "sb_w_ada", "sb_w_in", "sb_w_out"]
    big = dict(fox_w_ada=fox_w_ada[0], fox_w_in=fox_w_in[0], fox_w_out=fox_w_out[0],
               sb_w_ada=sb_w_ada[0], sb_w_in=sb_w_in[0], sb_w_out=sb_w_out[0])
    for n in ("fox_w_in", "sb_w_in"):
        width = big[n].shape[1]
        is_q = chip * width + jnp.arange(width)[None, :] < DI
        big[n] = big[n] * jnp.where(is_q, Q_SCALE, 1.0).astype(F32)
    halves = [big[n].astype(BF16).reshape(2, big[n].shape[0] // 2, big[n].shape[1]) for n in big_names]
    gathered, gsmall = _gather_weights(halves, [sb_norm_g, sb_b_ada])
    gathered = [lax.dynamic_update_index_in_dim(a, own, chip, 0) for a, own in zip(gathered, halves)]
    gsmall = [lax.dynamic_update_index_in_dim(a, own, chip, 0) for a, own in zip(gsmall, [sb_norm_g, sb_b_ada])]
    full = {}
    for n, a in zip(big_names, gathered):
        a = a.reshape(4, a.shape[1] * a.shape[2], a.shape[3])
        full[n] = a.reshape(4 * a.shape[1], a.shape[2]) if n.endswith("w_out") else _shards_to_cols(a)
    sb_norm_full = gsmall[0].reshape(1, D)
    sb_b_ada_full = gsmall[1].reshape(1, 3 * D)
    w_f = jnp.concatenate([full["fox_w_in"][:, 4 * DI:], jnp.zeros((D, LANES - H), BF16)], axis=1)
    b_f = jnp.concatenate([fox_b_f, jnp.zeros((1, LANES - H), F32)], axis=1)
    wf = dict(w_ada=full["fox_w_ada"], b_ada=fox_b_ada, norm_g=fox_norm_g, w_in=full["fox_w_in"][:, :4 * DI],
              w_f=w_f, b_f=b_f, w_out=full["fox_w_out"])
    ws = dict(w_ada=full["sb_w_ada"], b_ada=sb_b_ada_full, norm_g=sb_norm_full, w_in=full["sb_w_in"],
              w_out=full["sb_w_out"])

    loss, grad_x, gf, gs, dgf = _local_step(x, c, loss_target, wf, ws, final_norm_g.reshape(1, D))

    part = dict(fox_w_ada=gf["w_ada"], fox_w_in=gf["w_in"], fox_w_out=gf["w_out"],
                sb_w_ada=gs["w_ada"], sb_w_in=gs["w_in"], sb_w_out=gs["w_out"])
    shard_major = []
    for n in big_names:
        a = part[n]
        a = a.reshape(4, a.shape[0] // 4, a.shape[1]) if n.endswith("w_out") else _cols_to_shards(a)
        shard_major.append(a.reshape(4, 2, a.shape[1] // 2, a.shape[2]))
    core = lax.axis_index("c")
    got = _pair_exchange(shard_major)
    pair_f32, pair_bf16 = [], []
    for n, g4, b in zip(big_names, shard_major, got):
        a = lax.dynamic_index_in_dim(g4, core, axis=1, keepdims=False)
        r, C = a.shape[1:]
        s32, s16 = _ew_sum([a.reshape(4 * r, C), b.reshape(4 * r, C)], n + "_pair_sum", also_bf16=True)
        pair_f32.append(s32.reshape(4, r, C))
        pair_bf16.append(s16.reshape(4, r, C))
    others = _chip_exchange(pair_bf16)
    reduced_halves = [_ew_sum([lax.dynamic_index_in_dim(a, chip, axis=0, keepdims=False), b[0], b[1], b[2]],
                              n + "_chip_sum")[0] for n, a, b in zip(big_names, pair_f32, others)]
    theirs = _pair_share(reduced_halves)
    grad_big = {}
    for n, a, b in zip(big_names, reduced_halves, theirs):
        grad_big[n] = jnp.concatenate([jnp.where(core == 0, a, b), jnp.where(core == 0, b, a)], axis=0)

    pieces = [loss, gf["norm_g"], gf["b_ada"], jnp.concatenate([gf["b_f"], jnp.zeros((1, LANES - H), F32)], axis=1),
              gs["norm_g"], gs["b_ada"], dgf]
    vec = jnp.concatenate(pieces, axis=1)
    red = _allreduce_small(_pad_rows8(vec))[0:1]
    offs = [0]
    for p in pieces:
        offs.append(offs[-1] + p.shape[1])
    r_loss, r_fng, r_fba, r_fbf, r_sng, r_sba, r_fin = [red[:, offs[i]:offs[i + 1]] for i in range(7)]
    small_grads = dict(
        fox_norm_g=r_fng, fox_b_ada=r_fba, fox_b_f=r_fbf[:, :H],
        sb_norm_g=lax.dynamic_slice_in_dim(r_sng, chip * (D // 4), D // 4, axis=1),
        sb_b_ada=lax.dynamic_slice_in_dim(r_sba, chip * (3 * D // 4), 3 * D // 4, axis=1),
        final_norm_g=r_fin)

    weights = dict(fox_norm_g=fox_norm_g, fox_w_ada=fox_w_ada, fox_b_ada=fox_b_ada, fox_w_in=fox_w_in, fox_b_f=fox_b_f,
                   fox_w_out=fox_w_out, sb_norm_g=sb_norm_g, sb_w_ada=sb_w_ada, sb_b_ada=sb_b_ada, sb_w_in=sb_w_in,
                   sb_w_out=sb_w_out, final_norm_g=final_norm_g)
    ms = dict(fox_norm_g=m_fox_norm_g, fox_w_ada=m_fox_w_ada, fox_b_ada=m_fox_b_ada, fox_w_in=m_fox_w_in,
              fox_b_f=m_fox_b_f, fox_w_out=m_fox_w_out, sb_norm_g=m_sb_norm_g, sb_w_ada=m_sb_w_ada,
              sb_b_ada=m_sb_b_ada, sb_w_in=m_sb_w_in, sb_w_out=m_sb_w_out, final_norm_g=m_final_norm_g)
    vs = dict(fox_norm_g=v_fox_norm_g, fox_w_ada=v_fox_w_ada, fox_b_ada=v_fox_b_ada, fox_w_in=v_fox_w_in,
              fox_b_f=v_fox_b_f, fox_w_out=v_fox_w_out, sb_norm_g=v_sb_norm_g, sb_w_ada=v_sb_w_ada,
              sb_b_ada=v_sb_b_ada, sb_w_in=v_sb_w_in, sb_w_out=v_sb_w_out, final_norm_g=v_final_norm_g)
    order = ["fox_norm_g", "fox_w_ada", "fox_b_ada", "fox_w_in", "fox_b_f", "fox_w_out", "sb_norm_g", "sb_w_ada",
             "sb_b_ada", "sb_w_in", "sb_w_out", "final_norm_g"]
    grads, deltas, new_m, new_v = {}, {}, {}, {}
    for n in big_names:
        shp = weights[n].shape
        g2 = grad_big[n]
        d, m2, v2 = _adamw(weights[n][0], g2, ms[n][0], vs[n][0], n + "_adamw")
        grads[n], deltas[n], new_m[n], new_v[n] = g2.reshape(shp), d.reshape(shp), m2.reshape(shp), v2.reshape(shp)
    small_names = [n for n in order if n not in big_names]
    sizes = [small_grads[n].shape[1] for n in small_names]
    total = sum(sizes)
    padn = (-total) % LANES

    def pack(d):
        return jnp.concatenate([d[n].reshape(1, -1) for n in small_names] + [jnp.ones((1, padn), F32)], axis=1)

    sd, sm, sv_ = _adamw(pack(weights), pack(small_grads), pack(ms), pack(vs), "small_adamw")
    o = 0
    for n, sz in zip(small_names, sizes):
        shp = weights[n].shape
        grads[n] = small_grads[n].reshape(shp)
        deltas[n], new_m[n], new_v[n] = (t[:, o:o + sz].reshape(shp) for t in (sd, sm, sv_))
        o += sz
    return (r_loss[0, 0], grad_x, *[grads[n] for n in order], *[deltas[n] for n in order],
            *[new_m[n] for n in order], *[new_v[n] for n in order])
```

```python
import functools

import jax
import jax.numpy as jnp
from jax import lax
from jax.experimental import pallas as pl
from jax.experimental.pallas import tpu as pltpu

F32 = jnp.float32
BF16 = jnp.bfloat16
HEAD_DIM = 64
LOG2E = 1.4426950408889634
LN2 = 0.6931471805599453
Q_SCALE = HEAD_DIM ** -0.5 * LOG2E
LANES = 128
NORM_EPS = 1e-6
ADAM_LR = 0.001
ADAM_B1 = 0.9
ADAM_B2 = 0.999
ADAM_EPS = 1e-08
ADAM_WD = 0.01
ADAM_STEP = 10
VMEM_LIMIT = 56 * 1024 * 1024
SB_BLOCK = 256
FOX_BLOCK = 512
MESH = pl.DeviceIdType.MESH
HBM = pl.BlockSpec(memory_space=pltpu.HBM)
NT = (((1,), (1,)), ((), ()))
TN = (((0,), (0,)), ((), ()))


def _call(body, **kw):
    return pl.pallas_call(body, **kw)


def _params(**kw):
    return pltpu.CompilerParams(vmem_limit_bytes=VMEM_LIMIT, **kw)


def _tile(dim, pref, mult=128):
    if dim <= pref:
        return dim
    t = (pref // mult) * mult
    while t >= mult:
        if dim % t == 0:
            return t
        t -= mult
    return dim


def _sigmoid(x):
    return 1.0 / (1.0 + jnp.exp(-x))


def _split3(x):
    hi = x.astype(BF16)
    r = x - hi.astype(F32)
    mid = r.astype(BF16)
    lo = (r - mid.astype(F32)).astype(BF16)
    return hi, mid, lo


def _mm(a, b, mode, out_dtype, name, tm=512, tn=512, tk=512, col_scale=None):
    a_slabs = a.shape[0] if a.ndim == 3 else 0
    b_slabs = b.shape[0] if b.ndim == 3 else 0
    if mode == "nn":
        (M, K), (_, N) = a.shape, b.shape
    elif mode == "nt":
        M, K = (a.shape[1], a_slabs * a.shape[2]) if a_slabs else a.shape
        N = b.shape[0]
    else:
        K, M = a.shape
        N = b_slabs * b.shape[2] if b_slabs else b.shape[1]
    tm, tn, tk = _tile(M, tm), _tile(N, tn), _tile(K, tk)
    if a_slabs:
        tk = _tile(a.shape[2], tk)
    if b_slabs:
        tn = _tile(b.shape[2], tn)
    nk = K // tk
    dims = {"nn": (((1,), (0,)), ((), ())), "nt": NT, "tn": TN}[mode]

    def body(a_ref, b_ref, *rest):
        o_ref, acc_ref = rest[-2:]
        k = pl.program_id(2)

        @pl.when(k == 0)
        def _():
            acc_ref[...] = jnp.zeros_like(acc_ref)

        acc_ref[...] += lax.dot_general(a_ref[...], b_ref[...], dims, preferred_element_type=F32)

        @pl.when(k == nk - 1)
        def _():
            acc = acc_ref[...]
            if col_scale is not None:
                acc = acc * rest[0][...]
            o_ref[...] = acc.astype(out_dtype)

    if a_slabs:
        per = a.shape[2] // tk
        a_spec = pl.BlockSpec((None, tm, tk), lambda i, j, k: (k // per, i, k % per))
    elif mode == "tn":
        a_spec = pl.BlockSpec((tk, tm), lambda i, j, k: (k, i))
    else:
        a_spec = pl.BlockSpec((tm, tk), lambda i, j, k: (i, k))
    if b_slabs:
        per_b = b.shape[2] // tn
        b_spec = pl.BlockSpec((None, tk, tn), lambda i, j, k: (j // per_b, k, j % per_b))
    elif mode == "nt":
        b_spec = pl.BlockSpec((tn, tk), lambda i, j, k: (j, k))
    else:
        b_spec = pl.BlockSpec((tk, tn), lambda i, j, k: (k, j))
    extra_specs = [] if col_scale is None else [pl.BlockSpec((1, tn), lambda i, j, k: (0, j))]
    extra = [] if col_scale is None else [col_scale]
    return _call(
        body, name=name, grid=(M // tm, N // tn, nk),
        in_specs=[a_spec, b_spec] + extra_specs, out_specs=pl.BlockSpec((tm, tn), lambda i, j, k: (i, j)),
        out_shape=jax.ShapeDtypeStruct((M, N), out_dtype),
        scratch_shapes=[pltpu.VMEM((tm, tn), F32)], compiler_params=_params(),
    )(a, b, *extra)


def _mod_fwd(c8, w_ada, b_ada, name):
    D, N = w_ada.shape
    tn = _tile(N, 512)

    def body(c_ref, w_ref, b_ref, o_ref):
        c = c_ref[...]
        sc = (c * _sigmoid(c)).astype(BF16)
        o_ref[...] = jnp.dot(sc, w_ref[...], preferred_element_type=F32) + b_ref[...]

    return _call(
        body, name=name, grid=(N // tn,),
        in_specs=[pl.BlockSpec((8, D), lambda j: (0, 0)), pl.BlockSpec((D, tn), lambda j: (0, j)),
                  pl.BlockSpec((1, tn), lambda j: (0, j))],
        out_specs=pl.BlockSpec((8, tn), lambda j: (0, j)),
        out_shape=jax.ShapeDtypeStruct((8, N), F32), compiler_params=_params(),
    )(c8, w_ada, b_ada)


def _mod_bwd(cT, dmod8, nb, name):
    D = cT.shape[0]
    N = dmod8.shape[1]
    tn = _tile(N, 512)

    def body(c_ref, d_ref, w_ref, b_ref):
        c = c_ref[...]
        sc = c * _sigmoid(c)
        d = d_ref[...]
        acc = sc[:, 0:1] * d[0:1, :]
        bsum = d[0:1, :]
        for b in range(1, nb):
            acc = acc + sc[:, b:b + 1] * d[b:b + 1, :]
            bsum = bsum + d[b:b + 1, :]
        w_ref[...] = acc
        b_ref[...] = bsum

    return _call(
        body, name=name, grid=(N // tn,),
        in_specs=[pl.BlockSpec((D, 8), lambda j: (0, 0)), pl.BlockSpec((8, tn), lambda j: (0, j))],
        out_specs=[pl.BlockSpec((D, tn), lambda j: (0, j)), pl.BlockSpec((1, tn), lambda j: (0, j))],
        out_shape=[jax.ShapeDtypeStruct((D, N), F32), jax.ShapeDtypeStruct((1, N), F32)],
        compiler_params=_params(),
    )(cT, dmod8)


def _ln_proj(x, shift, scale, g, w, S, name, gather=()):
    T, D = x.shape
    N = w.shape[1]
    tm = _tile(S, 2048)
    tn = _tile(N, 1024)
    per_b = S // tm
    ng = len(gather)
    n0, n1 = T // tm, N // tn

    def body(x_ref, sh_ref, sc_ref, g_ref, w_ref, *rest):
        ins_h, (p_ref, h_ref), outs_h, sems = rest[:ng], rest[ng:ng + 2], rest[ng + 2:2 * ng + 2], rest[2 * ng + 2:]
        i, j = pl.program_id(0), pl.program_id(1)
        if ng:
            start, finish = _gather_steps(ins_h, outs_h, *sems)
            pl.when(jnp.logical_and(i == 0, j == 0))(start)

        @pl.when(j == 0)
        def _():
            xv = x_ref[...]
            r = lax.rsqrt(jnp.mean(xv * xv, axis=-1, keepdims=True) + NORM_EPS)
            h = (xv * r) * g_ref[...] * (1.0 + sc_ref[0]) + sh_ref[0]
            h_ref[...] = h.astype(BF16)

        p_ref[...] = jnp.dot(h_ref[...], w_ref[...], preferred_element_type=F32).astype(BF16)
        if ng:
            pl.when(jnp.logical_and(i == n0 - 1, j == n1 - 1))(finish)

    res = _call(
        body, name=name, grid=(n0, n1),
        in_specs=[pl.BlockSpec((tm, D), lambda i, j: (i, 0)),
                  pl.BlockSpec((1, 1, D), lambda i, j: (i // per_b, 0, 0)),
                  pl.BlockSpec((1, 1, D), lambda i, j: (i // per_b, 0, 0)),
                  pl.BlockSpec((1, D), lambda i, j: (0, 0)),
                  pl.BlockSpec((D, tn), lambda i, j: (0, j))] + [HBM] * ng,
        out_specs=[pl.BlockSpec((tm, tn), lambda i, j: (i, j)), pl.BlockSpec((tm, D), lambda i, j: (i, 0))] + [HBM] * ng,
        out_shape=[jax.ShapeDtypeStruct((T, N), BF16), jax.ShapeDtypeStruct((T, D), BF16)]
        + [jax.ShapeDtypeStruct((4,) + a.shape, a.dtype) for a in gather],
        scratch_shapes=_gather_sems(ng) if ng else [], compiler_params=_params(),
    )(x, shift, scale, g, w, *gather)
    return res[0], res[1], res[2:]


def _ln_bwd(dhs, x, dxo, scale, g, S, name):
    T, D = x.shape
    B = T // S
    tm = _tile(S, 512)
    per_b = S // tm

    nd = len(dhs)

    def body(*refs):
        x_ref, dxo_ref, sc_ref, g_ref, dx_ref, dsh_ref, dsc_ref, dg_ref = refs[nd:]
        i = pl.program_id(0)
        xv = x_ref[...]
        dh_v = refs[0][...]
        for r in refs[1:nd]:
            dh_v = dh_v + r[...]
        r = lax.rsqrt(jnp.mean(xv * xv, axis=-1, keepdims=True) + NORM_EPS)
        xn = xv * r
        gv = g_ref[...]
        one_sc = 1.0 + sc_ref[0]
        dhxn = dh_v * xn

        @pl.when(i % per_b == 0)
        def _():
            dsh_ref[...] = jnp.zeros_like(dsh_ref)
            dsc_ref[...] = jnp.zeros_like(dsc_ref)

        @pl.when(i == 0)
        def _():
            dg_ref[...] = jnp.zeros_like(dg_ref)

        dsh_ref[0] += jnp.sum(dh_v, axis=0, keepdims=True)
        dsc_ref[0] += jnp.sum(dhxn, axis=0, keepdims=True) * gv
        dg_ref[...] += jnp.sum(dhxn, axis=0, keepdims=True) * one_sc
        dxn = dh_v * (gv * one_sc)
        dx_ref[...] = r * (dxn - xn * jnp.mean(dxn * xn, axis=-1, keepdims=True)) + dxo_ref[...]

    row = pl.BlockSpec((tm, D), lambda i: (i, 0))
    per = pl.BlockSpec((1, 1, D), lambda i: (i // per_b, 0, 0))
    vec = pl.BlockSpec((1, D), lambda i: (0, 0))
    return _call(
        body, name=name, grid=(T // tm,),
        in_specs=[row] * (nd + 2) + [per, vec], out_specs=[row, per, per, vec],
        out_shape=[jax.ShapeDtypeStruct((T, D), F32), jax.ShapeDtypeStruct((B, 1, D), F32),
                   jax.ShapeDtypeStruct((B, 1, D), F32), jax.ShapeDtypeStruct((1, D), F32)],
        compiler_params=_params(),
    )(*dhs, x, dxo, scale, g)


def _gate_out(o, proj, w_out, x, gate, S, name):
    T, DI = o.shape
    D = w_out.shape[1]
    tm = _tile(S, 256)
    per_b = S // tm

    def body(o_ref, z_ref, w_ref, x_ref, g_ref, xo_ref, y_ref, u_ref):
        z = z_ref[...].astype(F32)
        u = (o_ref[...] * (z * _sigmoid(z))).astype(BF16)
        u_ref[...] = u
        y = jnp.dot(u, w_ref[...], preferred_element_type=F32)
        y_ref[...] = y
        xo_ref[...] = x_ref[...] + g_ref[0] * y

    wide = pl.BlockSpec((tm, DI), lambda i: (i, 0))
    row = pl.BlockSpec((tm, D), lambda i: (i, 0))
    return _call(
        body, name=name, grid=(T // tm,),
        in_specs=[wide, pl.BlockSpec((tm, DI), lambda i: (i, 3)), pl.BlockSpec((DI, D), lambda i: (0, 0)), row,
                  pl.BlockSpec((1, 1, D), lambda i: (i // per_b, 0, 0))],
        out_specs=[row, row, wide],
        out_shape=[jax.ShapeDtypeStruct((T, D), F32), jax.ShapeDtypeStruct((T, D), F32),
                   jax.ShapeDtypeStruct((T, DI), BF16)],
        compiler_params=_params(),
    )(o, proj, w_out, x, gate)


def _out_bwd(dxo, y, gate, w_out, o, proj, S, name):
    T, D = dxo.shape
    DI = o.shape[1]
    B = T // S
    tm = _tile(S, 256)
    per_b = S // tm

    def body(dxo_ref, y_ref, g_ref, w_ref, o_ref, z_ref, dy_ref, do_ref, dz_ref, dg_ref):
        dxo_v = dxo_ref[...]
        dy = (dxo_v * g_ref[0]).astype(BF16)
        dy_ref[...] = dy
        du = lax.dot_general(dy, w_ref[...], NT, preferred_element_type=F32)
        z = z_ref[...].astype(F32)
        sg = _sigmoid(z)
        do_ref[...] = (du * (z * sg)).astype(BF16)
        dz_ref[...] = (du * o_ref[...] * (sg * (1.0 + z * (1.0 - sg)))).astype(BF16)

        @pl.when(pl.program_id(0) % per_b == 0)
        def _():
            dg_ref[...] = jnp.zeros_like(dg_ref)

        dg_ref[0] += jnp.sum(dxo_v * y_ref[...], axis=0, keepdims=True)

    wide = pl.BlockSpec((tm, DI), lambda i: (i, 0))
    row = pl.BlockSpec((tm, D), lambda i: (i, 0))
    per = pl.BlockSpec((1, 1, D), lambda i: (i // per_b, 0, 0))
    return _call(
        body, name=name, grid=(T // tm,),
        in_specs=[row, row, per, pl.BlockSpec((DI, D), lambda i: (0, 0)), wide,
                  pl.BlockSpec((tm, DI), lambda i: (i, 3))],
        out_specs=[row, wide, wide, per],
        out_shape=[jax.ShapeDtypeStruct((T, D), BF16), jax.ShapeDtypeStruct((T, DI), BF16),
                   jax.ShapeDtypeStruct((T, DI), BF16), jax.ShapeDtypeStruct((B, 1, D), F32)],
        compiler_params=_params(),
    )(dxo, y, gate, w_out, o, proj)


def _final_loss(x, tgt, g, S, name):
    T, D = x.shape
    tm = _tile(S, 512)

    def body(x_ref, t_ref, g_ref, dx_ref, dg_ref, l_ref):
        @pl.when(pl.program_id(0) == 0)
        def _():
            dg_ref[...] = jnp.zeros_like(dg_ref)
            l_ref[...] = jnp.zeros_like(l_ref)

        xv = x_ref[...]
        gv = g_ref[...]
        r = lax.rsqrt(jnp.mean(xv * xv, axis=-1, keepdims=True) + NORM_EPS)
        xn = xv * r
        e = xn * gv - t_ref[...]
        part = jnp.sum(jnp.sum(e * e, axis=0, keepdims=True), axis=1, keepdims=True)
        l_ref[...] += (0.5 / D) * part
        dy = e * (1.0 / D)
        dg_ref[...] += jnp.sum(dy * xn, axis=0, keepdims=True)
        dxn = dy * gv
        dx_ref[...] = r * (dxn - xn * jnp.mean(dxn * xn, axis=-1, keepdims=True))

    row = pl.BlockSpec((tm, D), lambda i: (i, 0))
    return _call(
        body, name=name, grid=(T // tm,),
        in_specs=[row, row, pl.BlockSpec((1, D), lambda i: (0, 0))],
        out_specs=[row, pl.BlockSpec((1, D), lambda i: (0, 0)), pl.BlockSpec((1, LANES), lambda i: (0, 0))],
        out_shape=[jax.ShapeDtypeStruct((T, D), F32), jax.ShapeDtypeStruct((1, D), F32),
                   jax.ShapeDtypeStruct((1, LANES), F32)],
        compiler_params=_params(),
    )(x, tgt, g)


def _cum_fwd(fl, bf, name):
    B, S, _ = fl.shape
    ch = _tile(S, 256, 8)

    def body(fl_ref, b_ref, cum_ref):
        ri = lax.broadcasted_iota(jnp.int32, (ch, ch), 0)
        ci = lax.broadcasted_iota(jnp.int32, (ch, ch), 1)
        tri = jnp.where(ri >= ci, 1.0, 0.0).astype(BF16)

        def step(i, carry):
            r0 = pl.multiple_of(i * ch, ch)
            z = fl_ref[0, pl.ds(r0, ch), :] + b_ref[...]
            lf = (jnp.minimum(z, 0.0) - jnp.log(1.0 + jnp.exp(-jnp.abs(z)))) * LOG2E
            hi, mid, lo = _split3(lf)
            cs = (jnp.dot(tri, hi, preferred_element_type=F32) + jnp.dot(tri, mid, preferred_element_type=F32)
                  + jnp.dot(tri, lo, preferred_element_type=F32)) + carry
            cum_ref[0, pl.ds(r0, ch), :] = cs
            return cs[ch - 1:ch, :]

        lax.fori_loop(0, S // ch, step, jnp.zeros((1, LANES), F32))

    blk = pl.BlockSpec((1, S, LANES), lambda b: (b, 0, 0))
    return _call(
        body, name=name, grid=(B,), in_specs=[blk, pl.BlockSpec((1, LANES), lambda b: (0, 0))], out_specs=blk,
        out_shape=jax.ShapeDtypeStruct((B, S, LANES), F32), compiler_params=_params(),
    )(fl, bf)


def _cum_bwd(dcs, fl, bf, name):
    B, S, _ = fl.shape
    ch = _tile(S, 256, 8)
    n = S // ch

    def body(d_ref, fl_ref, b_ref, o_ref, db_ref):
        ri = lax.broadcasted_iota(jnp.int32, (ch, ch), 0)
        ci = lax.broadcasted_iota(jnp.int32, (ch, ch), 1)
        tri = jnp.where(ci >= ri, 1.0, 0.0).astype(BF16)

        @pl.when(pl.program_id(0) == 0)
        def _():
            db_ref[...] = jnp.zeros_like(db_ref)

        def step(t, carry):
            tail, dbsum = carry
            r0 = pl.multiple_of((n - 1 - t) * ch, ch)
            hi, mid, lo = _split3(d_ref[0, pl.ds(r0, ch), :])
            suf = (jnp.dot(tri, hi, preferred_element_type=F32) + jnp.dot(tri, mid, preferred_element_type=F32)
                   + jnp.dot(tri, lo, preferred_element_type=F32)) + tail
            z = fl_ref[0, pl.ds(r0, ch), :] + b_ref[...]
            dfl = -suf * _sigmoid(-z)
            o_ref[0, pl.ds(r0, ch), :] = dfl
            return suf[0:1, :], dbsum + jnp.sum(dfl, axis=0, keepdims=True)

        z1 = jnp.zeros((1, LANES), F32)
        _, dbsum = lax.fori_loop(0, n, step, (z1, z1))
        db_ref[...] += dbsum

    blk = pl.BlockSpec((1, S, LANES), lambda b: (b, 0, 0))
    vec = pl.BlockSpec((1, LANES), lambda b: (0, 0))
    return _call(
        body, name=name, grid=(B,), in_specs=[blk, blk, vec], out_specs=[blk, vec],
        out_shape=[jax.ShapeDtypeStruct((B, S, LANES), F32), jax.ShapeDtypeStruct((1, LANES), F32)],
        compiler_params=_params(),
    )(dcs, fl, bf)


HEADS_PER_STEP = 4
GROUP = 2 * HEAD_DIM


def _step_width():
    return HEAD_DIM * HEADS_PER_STEP


def _cols(S, offset_blocks=0):
    return pl.BlockSpec((S, _step_width()), lambda b, h: (b, offset_blocks + h))


def _row_spec(nq, tq):
    return pl.BlockSpec((1, HEADS_PER_STEP, nq, 1, tq), lambda b, h: (b, h, 0, 0, 0))


def _lanes(g):
    return slice(GROUP * (g // 2), GROUP * (g // 2) + GROUP)


def _hi_lo(x):
    hi = x.astype(BF16)
    return hi, (x - hi.astype(F32)).astype(BF16)


def _dot(a, b, dims=None):
    if dims is None:
        return jnp.dot(a, b, preferred_element_type=F32)
    return lax.dot_general(a, b, dims, preferred_element_type=F32)


def _causal_blocks(nq, prep, init, stages, finish, combine=None, descending=False):
    heads = range(HEADS_PER_STEP)

    def qloop(qi, _):
        ctx = [prep(g, qi) for g in heads]

        def step(kj, carry, masked):
            st = list(carry)
            for n, stage in enumerate(stages):
                if combine is not None and n == len(stages) - 1:
                    combine(kj, ctx, st)
                st = [stage(g, ctx[g], kj, masked, st[g]) for g in heads]
            return tuple(st)

        carry = tuple(init() for _ in heads)
        if descending:
            carry = step(qi, carry, True)
            carry = lax.fori_loop(0, qi, lambda t, cr: step(qi - 1 - t, cr, False), carry)
        else:
            carry = lax.fori_loop(0, qi, lambda kj, cr: step(kj, cr, False), carry)
            carry = step(qi, carry, True)
        finish(qi, ctx, carry)
        return 0

    lax.fori_loop(0, nq, qloop, 0)


class _Block:
    def __init__(self, tq):
        self.tq = tq
        self.lane = lax.broadcasted_iota(jnp.int32, (tq, GROUP), 1)
        self.low = self.lane < HEAD_DIM
        self.ri = lax.broadcasted_iota(jnp.int32, (tq, tq), 0)
        self.ci = lax.broadcasted_iota(jnp.int32, (tq, tq), 1)

    def rows(self, i):
        return pl.ds(pl.multiple_of(i * self.tq, self.tq), self.tq)

    def own(self, g, x):
        return jnp.where(self.low if g % 2 == 0 else jnp.logical_not(self.low), x, jnp.zeros_like(x))

    def pair(self, a, b):
        return jnp.where(self.low, a, b)

    def stat(self, g, x):
        return jnp.sum(jnp.where(self.lane == HEAD_DIM * (g % 2), x, 0.0), axis=1, keepdims=True)


def _fox_fwd(proj, cumcol, cumrow, name):
    T, DI = proj.shape[0], proj.shape[1] // 4
    B, H, nq, _, tq = cumrow.shape
    S = nq * tq
    nb = DI // _step_width()

    def body(q_ref, k_ref, v_ref, cc_ref, cr_ref, o_ref, st_ref, acc_scr):
        h0 = pl.program_id(1) * HEADS_PER_STEP
        blk = _Block(tq)

        def prep(g, qi):
            acc_scr[g] = jnp.zeros((tq, GROUP), F32)
            q = blk.own(g, q_ref[blk.rows(qi), _lanes(g)])
            ccol = jnp.sum(jnp.where(blk.lane == h0 + g, cc_ref[0, blk.rows(qi), :], 0.0), axis=1, keepdims=True)
            return q, ccol

        def init():
            return jnp.full((tq, 1), -jnp.inf, F32), jnp.zeros((tq, 1), F32)

        def scores(g, ctx, kj, masked, st):
            return st + (_dot(ctx[0], k_ref[blk.rows(kj), _lanes(g)], NT),)

        def softmax(g, ctx, kj, masked, st):
            m, l, s = st
            s = s + ctx[1] - cr_ref[0, g, kj]
            if masked:
                s = jnp.where(blk.ci <= blk.ri, s, -jnp.inf)
            m_new = jnp.maximum(m, jnp.max(s, axis=1, keepdims=True))
            alpha = jnp.exp2(m - m_new)
            p = jnp.exp2(s - m_new)
            return (m_new, alpha * l + jnp.sum(p, axis=1, keepdims=True), alpha) + _hi_lo(p)

        def values(g, ctx, kj, masked, st):
            m, l, alpha, hi, lo = st
            v = v_ref[blk.rows(kj), _lanes(g)]
            acc_scr[g] = alpha * acc_scr[g] + (_dot(hi, v) + _dot(lo, v))
            return m, l

        def finish(qi, ctx, carry):
            for g in range(0, HEADS_PER_STEP, 2):
                (m0, l0), (m1, l1) = carry[g], carry[g + 1]
                o_ref[blk.rows(qi), _lanes(g)] = blk.pair(acc_scr[g] / l0, acc_scr[g + 1] / l1)
                st_ref[blk.rows(qi), _lanes(g)] = blk.pair(m0 + jnp.log2(l0), m1 + jnp.log2(l1))

        _causal_blocks(nq, prep, init, [scores, softmax, values], finish)

    out = jax.ShapeDtypeStruct((T, DI), F32)
    return _call(
        body, name=name, grid=(B, H // HEADS_PER_STEP),
        in_specs=[_cols(S), _cols(S, nb), _cols(S, 2 * nb), pl.BlockSpec((1, S, LANES), lambda b, h: (b, 0, 0)),
                  _row_spec(nq, tq)],
        out_specs=[_cols(S), _cols(S)], out_shape=[out, out],
        scratch_shapes=[pltpu.VMEM((HEADS_PER_STEP, tq, GROUP), F32)], compiler_params=_params(),
    )(proj, proj, proj, cumcol, cumrow)


def _fox_bwd(proj, do, dzg, o, stat, cumcol, cumrow, name):
    T, DI = do.shape
    B, H, nq, _, tq = cumrow.shape
    S = nq * tq
    nb = DI // _step_width()

    def body(q_ref, k_ref, v_ref, do_ref, dz_ref, o_ref, st_ref, cc_ref, cr_ref, dqkv_ref, dcs_ref, dk_acc, dv_acc,
             dq_scr):
        h0 = pl.program_id(1) * HEADS_PER_STEP
        blk = _Block(tq)
        dk_acc[...] = jnp.zeros_like(dk_acc)
        dv_acc[...] = jnp.zeros_like(dv_acc)
        dcs_ref[...] = jnp.zeros_like(dcs_ref)

        def prep(g, qi):
            dq_scr[g] = jnp.zeros((tq, GROUP), F32)
            q = blk.own(g, q_ref[blk.rows(qi), _lanes(g)])
            dout = blk.own(g, do_ref[blk.rows(qi), _lanes(g)])
            delta = jnp.sum(o_ref[blk.rows(qi), _lanes(g)] * dout.astype(F32), axis=1, keepdims=True)
            lse = blk.stat(g, st_ref[blk.rows(qi), _lanes(g)])
            ccol = jnp.sum(jnp.where(blk.lane == h0 + g, cc_ref[0, blk.rows(qi), :], 0.0), axis=1, keepdims=True)
            return q, dout, lse, delta, ccol

        def init():
            return ()

        def scores(g, ctx, kj, masked, st):
            return (_dot(ctx[0], k_ref[blk.rows(kj), _lanes(g)], NT), _dot(ctx[1], v_ref[blk.rows(kj), _lanes(g)], NT))

        def softmax_bwd(g, ctx, kj, masked, st):
            s, dp = st
            _, _, lse, delta, ccol = ctx
            s = s + ccol - cr_ref[0, g, kj]
            if masked:
                s = jnp.where(blk.ci <= blk.ri, s, -jnp.inf)
            p = jnp.exp2(s - lse)
            ds = p * (dp - delta)
            return p.astype(BF16), ds.astype(BF16), jnp.sum(ds, axis=0, keepdims=True)

        def combine(kj, ctx, st):
            for g in range(0, HEADS_PER_STEP, 2):
                dv_acc[blk.rows(kj), _lanes(g)] += _dot(st[g][0], ctx[g][1], TN) + _dot(st[g + 1][0], ctx[g + 1][1], TN)
                dk_acc[blk.rows(kj), _lanes(g)] += _dot(st[g][1], ctx[g][0], TN) + _dot(st[g + 1][1], ctx[g + 1][0], TN)
            for g in range(HEADS_PER_STEP):
                dcs_ref[0, g, kj] += st[g][2]

        def queries(g, ctx, kj, masked, st):
            dq_scr[g] += _dot(st[1], blk.own(g, k_ref[blk.rows(kj), _lanes(g)]))
            return ()

        def finish(qi, ctx, carry):
            for g in range(0, HEADS_PER_STEP, 2):
                dqkv_ref[0, blk.rows(qi), _lanes(g)] = ((dq_scr[g] + dq_scr[g + 1]) * LN2).astype(BF16)

        _causal_blocks(nq, prep, init, [scores, softmax_bwd, queries], finish, combine=combine)
        dqkv_ref[1] = (dk_acc[...] * LN2).astype(BF16)
        dqkv_ref[2] = dv_acc[...].astype(BF16)
        dqkv_ref[3] = dz_ref[...]

    W = _step_width()
    return _call(
        body, name=name, grid=(B, H // HEADS_PER_STEP),
        in_specs=[_cols(S), _cols(S, nb), _cols(S, 2 * nb), _cols(S), _cols(S), _cols(S), _cols(S),
                  pl.BlockSpec((1, S, LANES), lambda b, h: (b, 0, 0)), _row_spec(nq, tq)],
        out_specs=[pl.BlockSpec((4, S, W), lambda b, h: (0, b, h)), _row_spec(nq, tq)],
        out_shape=[jax.ShapeDtypeStruct((4, T, DI), BF16), jax.ShapeDtypeStruct((B, H, nq, 1, tq), F32)],
        scratch_shapes=[pltpu.VMEM((S, W), F32), pltpu.VMEM((S, W), F32), pltpu.VMEM((HEADS_PER_STEP, tq, GROUP), F32)],
        compiler_params=_params(),
    )(proj, proj, proj, do, dzg, o, stat, cumcol, cumrow)


def _log2_keep(z2):
    nz = -z2
    e = jnp.exp2(jnp.minimum(z2, nz))
    return jnp.minimum(nz, 0.0) - jnp.log2(1.0 + e), e


def _sb_fwd(proj, B, tq, name):
    T, DI = proj.shape[0], proj.shape[1] // 4
    S = T // B
    H = DI // HEAD_DIM
    nq = S // tq
    nb = DI // _step_width()

    def body(q_ref, k_ref, v_ref, o_ref, st_ref, acc_scr, c_scr):
        blk = _Block(tq)
        strict = blk.ci < blk.ri
        above = jnp.where(blk.ri > blk.ci, 1.0, 0.0).astype(BF16)

        def prep(g, qi):
            acc_scr[g] = jnp.zeros((tq, GROUP), F32)
            c_scr[g] = jnp.zeros((tq, 1), F32)
            return blk.own(g, q_ref[blk.rows(qi), _lanes(g)])

        def init():
            return ()

        def scores(g, q, kj, masked, st):
            return (_dot(q, k_ref[blk.rows(kj), _lanes(g)], NT),)

        def logs(g, q, kj, masked, st):
            (z,) = st
            lk, _ = _log2_keep(z)
            lb = z + lk
            if masked:
                lk = jnp.where(strict, lk, 0.0)
            c = c_scr[g]
            c_scr[g] = c + jnp.sum(lk, axis=1, keepdims=True)
            return (lb + c,) + _hi_lo(lk)

        def suffix(g, q, kj, masked, st):
            lbc, hi, lo = st
            return lbc, _dot(hi, above) + _dot(lo, above)

        def weights(g, q, kj, masked, st):
            lbc, after = st
            a = jnp.exp2(lbc + after)
            if masked:
                a = jnp.where(strict, a, 0.0)
            return (a.astype(BF16),)

        def values(g, q, kj, masked, st):
            acc_scr[g] += _dot(st[0], v_ref[blk.rows(kj), _lanes(g)])
            return ()

        def finish(qi, ctx, carry):
            for g in range(0, HEADS_PER_STEP, 2):
                o_ref[blk.rows(qi), _lanes(g)] = blk.pair(acc_scr[g], acc_scr[g + 1])
                st_ref[blk.rows(qi), _lanes(g)] = blk.pair(c_scr[g], c_scr[g + 1])

        _causal_blocks(nq, prep, init, [scores, logs, suffix, weights, values], finish, descending=True)

    out = jax.ShapeDtypeStruct((T, DI), F32)
    return _call(
        body, name=name, grid=(B, H // HEADS_PER_STEP), in_specs=[_cols(S), _cols(S, nb), _cols(S, 2 * nb)],
        out_specs=[_cols(S), _cols(S)], out_shape=[out, out],
        scratch_shapes=[pltpu.VMEM((HEADS_PER_STEP, tq, GROUP), F32), pltpu.VMEM((HEADS_PER_STEP, tq, 1), F32)],
        compiler_params=_params(),
    )(proj, proj, proj)


def _sb_bwd(proj, do, dzg, stat, B, tq, name):
    T, DI = do.shape
    S = T // B
    H = DI // HEAD_DIM
    nq = S // tq
    nb = DI // _step_width()

    def body(q_ref, k_ref, v_ref, do_ref, dz_ref, st_ref, dqkv_ref, dk_acc, dv_acc, dq_scr):
        blk = _Block(tq)
        strict = blk.ci < blk.ri
        upto = jnp.where(blk.ri <= blk.ci, 1.0, 0.0).astype(BF16)
        before = jnp.where(blk.ri < blk.ci, 1.0, 0.0).astype(BF16)
        dk_acc[...] = jnp.zeros_like(dk_acc)
        dv_acc[...] = jnp.zeros_like(dv_acc)

        def prep(g, qi):
            dq_scr[g] = jnp.zeros((tq, GROUP), F32)
            return (blk.own(g, q_ref[blk.rows(qi), _lanes(g)]), blk.own(g, do_ref[blk.rows(qi), _lanes(g)]),
                    blk.stat(g, st_ref[blk.rows(qi), _lanes(g)]))

        def init():
            return jnp.zeros((tq, 1), F32), jnp.zeros((tq, 1), F32)

        def scores(g, ctx, kj, masked, st):
            return st + (_dot(ctx[0], k_ref[blk.rows(kj), _lanes(g)], NT),
                         _dot(ctx[1], v_ref[blk.rows(kj), _lanes(g)], NT))

        def logs(g, ctx, kj, masked, st):
            cpre, pg, z, da = st
            lk, e = _log2_keep(z)
            inv = 1.0 / (1.0 + e)
            sig = jnp.where(z >= 0.0, inv, e * inv)
            lbt = (z + lk) + (ctx[2] - cpre)
            if masked:
                lk = jnp.where(strict, lk, 0.0)
            return (cpre + jnp.sum(lk, axis=1, keepdims=True), pg, da, lbt, sig) + _hi_lo(lk)

        def prefix(g, ctx, kj, masked, st):
            cpre, pg, da, lbt, sig, hi, lo = st
            return cpre, pg, da, lbt, sig, _dot(hi, upto) + _dot(lo, upto)

        def weights(g, ctx, kj, masked, st):
            cpre, pg, da, lbt, sig, pre = st
            a = jnp.exp2(lbt - pre)
            if masked:
                a = jnp.where(strict, a, 0.0)
            gr = da * a
            return cpre, pg, sig, a.astype(BF16), gr, gr.astype(BF16)

        def grad_prefix(g, ctx, kj, masked, st):
            cpre, pg, sig, ab, gr, gb = st
            return cpre, pg, sig, ab, gr, _dot(gb, before)

        def dlogits(g, ctx, kj, masked, st):
            cpre, pg, sig, ab, gr, pfx = st
            dz = gr - sig * (gr + (pfx + pg))
            if masked:
                dz = jnp.where(strict, dz, 0.0)
            return cpre, pg + jnp.sum(gr, axis=1, keepdims=True), ab, dz.astype(BF16)

        def combine(kj, ctx, st):
            for g in range(0, HEADS_PER_STEP, 2):
                dv_acc[blk.rows(kj), _lanes(g)] += _dot(st[g][2], ctx[g][1], TN) + _dot(st[g + 1][2], ctx[g + 1][1], TN)
                dk_acc[blk.rows(kj), _lanes(g)] += _dot(st[g][3], ctx[g][0], TN) + _dot(st[g + 1][3], ctx[g + 1][0], TN)

        def queries(g, ctx, kj, masked, st):
            cpre, pg, _, dzb = st
            dq_scr[g] += _dot(dzb, blk.own(g, k_ref[blk.rows(kj), _lanes(g)]))
            return cpre, pg

        def finish(qi, ctx, carry):
            for g in range(0, HEADS_PER_STEP, 2):
                dqkv_ref[0, blk.rows(qi), _lanes(g)] = ((dq_scr[g] + dq_scr[g + 1]) * LN2).astype(BF16)

        _causal_blocks(nq, prep, init, [scores, logs, prefix, weights, grad_prefix, dlogits, queries], finish,
                       combine=combine)
        dqkv_ref[1] = (dk_acc[...] * LN2).astype(BF16)
        dqkv_ref[2] = dv_acc[...].astype(BF16)
        dqkv_ref[3] = dz_ref[...]

    W = _step_width()
    return _call(
        body, name=name, grid=(B, H // HEADS_PER_STEP),
        in_specs=[_cols(S), _cols(S, nb), _cols(S, 2 * nb), _cols(S), _cols(S), _cols(S)],
        out_specs=pl.BlockSpec((4, S, W), lambda b, h: (0, b, h)),
        out_shape=jax.ShapeDtypeStruct((4, T, DI), BF16),
        scratch_shapes=[pltpu.VMEM((S, W), F32), pltpu.VMEM((S, W), F32), pltpu.VMEM((HEADS_PER_STEP, tq, GROUP), F32)],
        compiler_params=_params(),
    )(proj, proj, proj, do, dzg, stat)


def _row_tile(R, C, n_arrays):
    budget = 24 * 1024 * 1024 // (2 * n_arrays * 4 * max(C, LANES))
    return _tile(R, max(8, budget), 8)


def _ew_sum(parts, name, also_bf16=False):
    R, C = parts[0].shape
    tr = _row_tile(R, C, len(parts) + 2)
    n = len(parts)

    def body(*refs):
        acc = refs[0][...].astype(F32) + refs[1][...].astype(F32)
        for r in refs[2:n]:
            acc = acc + r[...].astype(F32)
        refs[n][...] = acc
        if also_bf16:
            refs[n + 1][...] = acc.astype(BF16)

    blk = pl.BlockSpec((tr, C), lambda i: (i, 0))
    out_shape = [jax.ShapeDtypeStruct((R, C), F32)] + ([jax.ShapeDtypeStruct((R, C), BF16)] if also_bf16 else [])
    return _call(
        body, name=name, grid=(R // tr,), in_specs=[blk] * n, out_specs=[blk] * len(out_shape),
        out_shape=out_shape, compiler_params=_params(),
    )(*parts)


def _adamw(w, g, m, v, name):
    R, C = w.shape
    tr = _row_tile(R, C, 7)
    c1 = 1.0 / (1.0 - ADAM_B1 ** ADAM_STEP)
    c2 = 1.0 / (1.0 - ADAM_B2 ** ADAM_STEP)

    def body(w_ref, g_ref, m_ref, v_ref, d_ref, m2_ref, v2_ref):
        gv = g_ref[...]
        m2 = ADAM_B1 * m_ref[...] + (1.0 - ADAM_B1) * gv
        v2 = ADAM_B2 * v_ref[...] + (1.0 - ADAM_B2) * (gv * gv)
        m2_ref[...] = m2
        v2_ref[...] = v2
        d_ref[...] = -ADAM_LR * ((m2 * c1) / (jnp.sqrt(v2 * c2) + ADAM_EPS) + ADAM_WD * w_ref[...])

    blk = pl.BlockSpec((tr, C), lambda i: (i, 0))
    out = jax.ShapeDtypeStruct((R, C), F32)
    return _call(
        body, name=name, grid=(R // tr,), in_specs=[blk] * 4, out_specs=[blk] * 3, out_shape=[out] * 3,
        compiler_params=_params(),
    )(w, g, m, v)


def _me():
    return lax.axis_index("x"), lax.axis_index("y"), lax.axis_index("c")


def _chip_of(x, y):
    return 2 * x + y


def _other_chips(x, y):
    return [(x, 1 - y), (1 - x, y), (1 - x, 1 - y)]


def _gather_steps(ins_h, outs_h, send1, recv1, send2, recv2):
    nh = len(ins_h)
    x, y, c = _me()
    mine = _chip_of(x, y)
    chips = _other_chips(x, y)
    sib = (x, y, 1 - c)

    def landed(i, k, half):
        return outs_h[i].at[_chip_of(*chips[k]), half]

    def first(i, k):
        return pltpu.make_async_remote_copy(
            src_ref=ins_h[i].at[c], dst_ref=outs_h[i].at[mine, c], send_sem=send1.at[i, k], recv_sem=recv1.at[i, k],
            device_id=(*chips[k], c), device_id_type=MESH)

    def passed(i, k):
        return pltpu.make_async_remote_copy(
            src_ref=landed(i, k, c), dst_ref=landed(i, k, c), send_sem=send2.at[i, k], recv_sem=recv2.at[i, k],
            device_id=sib, device_id_type=MESH)

    def start():
        for i in range(nh):
            for k in range(3):
                first(i, k).start()

    def finish():
        for i in range(nh):
            for k in range(3):
                pltpu.make_async_remote_copy(
                    src_ref=ins_h[i].at[c], dst_ref=landed(i, k, c), send_sem=send1.at[i, k], recv_sem=recv1.at[i, k],
                    device_id=(*chips[k], c), device_id_type=MESH).wait_recv()
                passed(i, k).start()
        for i in range(nh):
            for k in range(3):
                pltpu.make_async_remote_copy(
                    src_ref=landed(i, k, c), dst_ref=landed(i, k, 1 - c), send_sem=send2.at[i, k],
                    recv_sem=recv2.at[i, k], device_id=sib, device_id_type=MESH).wait_recv()
        for i in range(nh):
            for k in range(3):
                first(i, k).wait_send()
                passed(i, k).wait_send()

    return start, finish


def _gather_sems(nh):
    return [pltpu.SemaphoreType.DMA((nh, 3)) for _ in range(4)]


def _gather_weights(halves, smalls):
    nh, ns = len(halves), len(smalls)

    def body(*refs):
        ins_h, ins_s = refs[:nh], refs[nh:nh + ns]
        outs_h, outs_s = refs[nh + ns:2 * nh + ns], refs[2 * nh + ns:2 * (nh + ns)]
        send1, recv1, send2, recv2, send3, recv3 = refs[2 * (nh + ns):]
        x, y, c = _me()
        mine = _chip_of(x, y)
        chips = _other_chips(x, y)

        def small(i, k):
            return pltpu.make_async_remote_copy(
                src_ref=ins_s[i], dst_ref=outs_s[i].at[mine], send_sem=send3.at[i, k], recv_sem=recv3.at[i, k],
                device_id=(*chips[k], c), device_id_type=MESH)

        start, finish = _gather_steps(ins_h, outs_h, send1, recv1, send2, recv2)
        start()
        for i in range(ns):
            for k in range(3):
                small(i, k).start()
        finish()
        for i in range(ns):
            for k in range(3):
                pltpu.make_async_remote_copy(
                    src_ref=ins_s[i], dst_ref=outs_s[i].at[_chip_of(*chips[k])], send_sem=send3.at[i, k],
                    recv_sem=recv3.at[i, k], device_id=(*chips[k], c), device_id_type=MESH).wait_recv()
                small(i, k).wait_send()

    out_shape = ([jax.ShapeDtypeStruct((4,) + a.shape, a.dtype) for a in halves]
                 + [jax.ShapeDtypeStruct((4,) + a.shape, a.dtype) for a in smalls])
    n = nh + ns
    res = _call(
        body, name="gather_weights", in_specs=[HBM] * n, out_specs=[HBM] * n, out_shape=out_shape,
        scratch_shapes=_gather_sems(nh) + [pltpu.SemaphoreType.DMA((max(ns, 1), 3)),
                                           pltpu.SemaphoreType.DMA((max(ns, 1), 3))],
        compiler_params=_params(),
    )(*halves, *smalls)
    return res[:nh], res[nh:]


def _pair_exchange(grads):
    n = len(grads)

    def body(*refs):
        ins, got = refs[:n], refs[n:2 * n]
        send, recv = refs[2 * n:]
        x, y, c = _me()
        cps = []
        for i in range(n):
            for j in range(4):
                r = pltpu.make_async_remote_copy(
                    src_ref=ins[i].at[j, 1 - c], dst_ref=got[i].at[j], send_sem=send.at[i, j], recv_sem=recv.at[i, j],
                    device_id=(x, y, 1 - c), device_id_type=MESH)
                r.start()
                cps.append(r)
        for r in cps:
            r.wait()

    return _call(
        body, name="grad_pair_exchange", in_specs=[HBM] * n, out_specs=[HBM] * n,
        out_shape=[jax.ShapeDtypeStruct((4,) + g.shape[2:], g.dtype) for g in grads],
        scratch_shapes=[pltpu.SemaphoreType.DMA((n, 4)), pltpu.SemaphoreType.DMA((n, 4))],
        compiler_params=_params(),
    )(*grads)


def _chip_exchange(sums):
    n = len(sums)

    def body(*refs):
        ins, got = refs[:n], refs[n:2 * n]
        send, recv = refs[2 * n:]
        x, y, c = _me()
        chips = _other_chips(x, y)
        cps = []
        for i in range(n):
            for k in range(3):
                r = pltpu.make_async_remote_copy(
                    src_ref=ins[i].at[_chip_of(*chips[k])], dst_ref=got[i].at[k], send_sem=send.at[i, k],
                    recv_sem=recv.at[i, k], device_id=(*chips[k], c), device_id_type=MESH)
                r.start()
                cps.append(r)
        for r in cps:
            r.wait()

    return _call(
        body, name="grad_chip_exchange", in_specs=[HBM] * n, out_specs=[HBM] * n,
        out_shape=[jax.ShapeDtypeStruct((3,) + s.shape[1:], s.dtype) for s in sums],
        scratch_shapes=[pltpu.SemaphoreType.DMA((n, 3)), pltpu.SemaphoreType.DMA((n, 3))],
        compiler_params=_params(),
    )(*sums)


def _pair_share(halves):
    n = len(halves)

    def body(*refs):
        ins, outs = refs[:n], refs[n:2 * n]
        send, recv = refs[2 * n:]
        x, y, c = _me()
        cps = []
        for i in range(n):
            r = pltpu.make_async_remote_copy(
                src_ref=ins[i], dst_ref=outs[i], send_sem=send.at[i], recv_sem=recv.at[i],
                device_id=(x, y, 1 - c), device_id_type=MESH)
            r.start()
            cps.append(r)
        for r in cps:
            r.wait()

    return _call(
        body, name="grad_pair_share", in_specs=[HBM] * n, out_specs=[HBM] * n,
        out_shape=[jax.ShapeDtypeStruct(h.shape, h.dtype) for h in halves],
        scratch_shapes=[pltpu.SemaphoreType.DMA((n,)), pltpu.SemaphoreType.DMA((n,))],
        compiler_params=_params(),
    )(*halves)


def _allreduce_small(vec):
    P = vec.shape[1]

    def body(v_ref, sum_ref, all_ref, send, recv):
        x, y, c = _me()
        me = 4 * x + 2 * y + c
        all_ref[pl.ds(me, 1)] = v_ref[...][None]
        cps = []
        for d in range(1, 8):
            peer = (jnp.bitwise_xor(x, d >> 2), jnp.bitwise_xor(y, (d >> 1) & 1), jnp.bitwise_xor(c, d & 1))
            r = pltpu.make_async_remote_copy(
                src_ref=v_ref, dst_ref=all_ref.at[me], send_sem=send.at[d - 1], recv_sem=recv.at[d - 1],
                device_id=peer, device_id_type=MESH)
            r.start()
            cps.append(r)
        for d in range(1, 8):
            src = jnp.bitwise_xor(me, d)
            pltpu.make_async_remote_copy(
                src_ref=v_ref, dst_ref=all_ref.at[src], send_sem=send.at[d - 1], recv_sem=recv.at[d - 1],
                device_id=(x, y, c), device_id_type=MESH).wait_recv()
        for r in cps:
            r.wait_send()
        acc = all_ref[0]
        for i in range(1, 8):
            acc = acc + all_ref[i]
        sum_ref[...] = acc

    vm = pl.BlockSpec(memory_space=pltpu.VMEM)
    return _call(
        body, name="allreduce_small", in_specs=[vm], out_specs=[vm, vm],
        out_shape=[jax.ShapeDtypeStruct((8, P), F32), jax.ShapeDtypeStruct((8, 8, P), F32)],
        scratch_shapes=[pltpu.SemaphoreType.DMA((7,)), pltpu.SemaphoreType.DMA((7,))],
        compiler_params=_params(),
    )(vec)[0]


def _per_batch(mod, B, D):
    return [mod[:B, i * D:(i + 1) * D].reshape(B, 1, D) for i in range(3)]


def _pad_rows8(a):
    return jnp.concatenate([a, jnp.zeros((8 - a.shape[0],) + a.shape[1:], a.dtype)], axis=0)


def _layer_fwd(x, c8, w, S, fox, tag, gather=()):
    T, D = x.shape
    B = T // S
    DI = w["w_out"].shape[0]
    H = DI // HEAD_DIM
    tq = _tile(S, FOX_BLOCK if fox else SB_BLOCK, 8)
    mod = _mod_fwd(c8, w["w_ada"], w["b_ada"], tag + "_mod_fwd")
    shift, scale, gate = _per_batch(mod, B, D)
    proj, h, gathered = _ln_proj(x, shift, scale, w["norm_g"], w["w_in"], S, tag + "_ln_proj", gather)
    saved = dict(x=x, h=h, proj=proj, scale=scale, gate=gate, gathered=gathered)
    if fox:
        fl = _mm(h, w["w_f"], "nn", F32, tag + "_flogit").reshape(B, S, LANES)
        cum = _cum_fwd(fl, w["b_f"], tag + "_cum_fwd")
        cumrow = cum[:, :, :H].transpose(0, 2, 1).reshape(B, H, S // tq, 1, tq)
        o, stat = _fox_fwd(proj, cum, cumrow, tag + "_attn_fwd")
        saved.update(fl=fl, cum=cum, cumrow=cumrow)
    else:
        o, stat = _sb_fwd(proj, B, tq, tag + "_attn_fwd")
    xo, y, u = _gate_out(o, proj, w["w_out"], x, gate, S, tag + "_gate_out")
    saved.update(o=o, stat=stat, y=y, u=u)
    return xo, saved


def _layer_bwd(dxo, sv, w, cT, S, fox, tag):
    T, D = dxo.shape
    B = T // S
    DI = w["w_out"].shape[0]
    H = DI // HEAD_DIM
    tq = _tile(S, FOX_BLOCK if fox else SB_BLOCK, 8)
    dy, do, dzg, dgate = _out_bwd(dxo, sv["y"], sv["gate"], w["w_out"], sv["o"], sv["proj"], S, tag + "_out_bwd")
    g = {"w_out": _mm(sv["u"], dy, "tn", F32, tag + "_dw_out", tm=1024, tn=1024, tk=2048)}
    q_cols = jnp.where(jnp.arange(4 * DI)[None, :] < DI, Q_SCALE, 1.0).astype(F32)
    if fox:
        dproj, dcs = _fox_bwd(sv["proj"], do, dzg, sv["o"], sv["stat"], sv["cum"], sv["cumrow"], tag + "_attn_bwd")
        dcs = dcs.reshape(B, H, S).transpose(0, 2, 1)
        dcs = jnp.concatenate([dcs, jnp.zeros((B, S, LANES - H), F32)], axis=-1)
        dfl, db_f = _cum_bwd(dcs, sv["fl"], w["b_f"], tag + "_cum_bwd")
        g["b_f"] = db_f[:, :H]
        dfl = dfl.reshape(T, LANES).astype(BF16)
    else:
        dproj = _sb_bwd(sv["proj"], do, dzg, sv["stat"], B, tq, tag + "_attn_bwd")
    g["w_in"] = _mm(sv["h"], dproj, "tn", F32, tag + "_dw_in", tm=1024, tn=2048, tk=1024, col_scale=q_cols)
    dhs = [_mm(dproj, w["w_in"], "nt", F32, tag + "_dh", tm=2048, tn=1024, tk=1024)]
    if fox:
        dw_f = _mm(sv["h"], dfl, "tn", F32, tag + "_dw_f", tm=1024, tn=LANES, tk=2048)
        g["w_in"] = jnp.concatenate([g["w_in"], dw_f[:, :H]], axis=1)
        dhs.append(_mm(dfl, w["w_f"], "nt", F32, tag + "_dh_f", tm=2048, tn=1024, tk=LANES))
    dx, dshift, dscale, dg = _ln_bwd(dhs, sv["x"], dxo, sv["scale"], w["norm_g"], S, tag + "_ln_bwd")
    g["norm_g"] = dg
    dmod = jnp.concatenate([dshift, dscale, dgate], axis=-1).reshape(B, 3 * D)
    g["w_ada"], g["b_ada"] = _mod_bwd(cT, _pad_rows8(dmod), B, tag + "_mod_bwd")
    return dx, g


def _local_step(x3, c, tgt3, wf, ws, final_g, sb_halves=()):
    B, S, D = x3.shape
    T = B * S
    x = x3.reshape(T, D)
    c8 = _pad_rows8(c)
    cT = c8.T
    x1, sv1 = _layer_fwd(x, c8, wf, S, True, "fox", sb_halves)
    if sb_halves:
        ws = ws(sv1["gathered"])
    x2, sv2 = _layer_fwd(x1, c8, ws, S, False, "sb")
    dx2, dgf, loss = _final_loss(x2, tgt3.reshape(T, D), final_g, S, "final_loss")
    dx1, gs = _layer_bwd(dx2, sv2, ws, cT, S, False, "sb")
    dx0, gf = _layer_bwd(dx1, sv1, wf, cT, S, True, "fox")
    return loss, dx0.reshape(B, S, D), gf, gs, dgf


def _cols_to_shards(a):
    R, C4 = a.shape
    return a.reshape(R, 4, C4 // 4).transpose(1, 0, 2)


def _shards_to_cols(a):
    n, R, C = a.shape
    return a.transpose(1, 0, 2).reshape(R, n * C)


def kernel(x, c, fox_norm_g, fox_w_ada, fox_b_ada, fox_w_in, fox_b_f, fox_w_out, sb_norm_g, sb_w_ada, sb_b_ada, sb_w_in, sb_w_out, final_norm_g, loss_target, m_fox_norm_g, m_fox_w_ada, m_fox_b_ada, m_fox_w_in, m_fox_b_f, m_fox_w_out, m_sb_norm_g, m_sb_w_ada, m_sb_b_ada, m_sb_w_in, m_sb_w_out, m_final_norm_g, v_fox_norm_g, v_fox_w_ada, v_fox_b_ada, v_fox_w_in, v_fox_b_f, v_fox_w_out, v_sb_norm_g, v_sb_w_ada, v_sb_b_ada, v_sb_w_in, v_sb_w_out, v_final_norm_g):
    B, S, D = x.shape
    DI = 4 * fox_w_out.shape[1]
    H = DI // HEAD_DIM
    chip = _chip_of(lax.axis_index("x"), lax.axis_index("y"))

    big_names = ["fox_w_ada", "fox_w_in", "fox_w_out", "sb_w_ada", "sb_w_in", "sb_w_out"]
    big = dict(fox_w_ada=fox_w_ada[0], fox_w_in=fox_w_in[0], fox_w_out=fox_w_out[0],
               sb_w_ada=sb_w_ada[0], sb_w_in=sb_w_in[0], sb_w_out=sb_w_out[0])
    for n in ("fox_w_in", "sb_w_in"):
        width = big[n].shape[1]
        is_q = chip * width + jnp.arange(width)[None, :] < DI
        big[n] = big[n] * jnp.where(is_q, Q_SCALE, 1.0).astype(F32)
    halves = {n: big[n].astype(BF16).reshape(2, big[n].shape[0] // 2, big[n].shape[1]) for n in big_names}
    fox_names, sb_names = big_names[:3], big_names[3:]

    def assemble(names, gathered):
        full = {}
        for n, a in zip(names, gathered):
            a = lax.dynamic_update_index_in_dim(a, halves[n], chip, 0)
            a = a.reshape(4, a.shape[1] * a.shape[2], a.shape[3])
            full[n] = a.reshape(4 * a.shape[1], a.shape[2]) if n.endswith("w_out") else _shards_to_cols(a)
        return full

    gathered, gsmall = _gather_weights([halves[n] for n in fox_names], [sb_norm_g, sb_b_ada])
    gsmall = [lax.dynamic_update_index_in_dim(a, own, chip, 0) for a, own in zip(gsmall, [sb_norm_g, sb_b_ada])]
    full = assemble(fox_names, gathered)
    sb_norm_full = gsmall[0].reshape(1, D)
    sb_b_ada_full = gsmall[1].reshape(1, 3 * D)
    w_f = jnp.concatenate([full["fox_w_in"][:, 4 * DI:], jnp.zeros((D, LANES - H), BF16)], axis=1)
    b_f = jnp.concatenate([fox_b_f, jnp.zeros((1, LANES - H), F32)], axis=1)
    wf = dict(w_ada=full["fox_w_ada"], b_ada=fox_b_ada, norm_g=fox_norm_g, w_in=full["fox_w_in"][:, :4 * DI],
              w_f=w_f, b_f=b_f, w_out=full["fox_w_out"])

    def ws(gathered_sb):
        f = assemble(sb_names, gathered_sb)
        return dict(w_ada=f["sb_w_ada"], b_ada=sb_b_ada_full, norm_g=sb_norm_full, w_in=f["sb_w_in"], w_out=f["sb_w_out"])

    loss, grad_x, gf, gs, dgf = _local_step(x, c, loss_target, wf, ws, final_norm_g.reshape(1, D),
                                            [halves[n] for n in sb_names])

    part = dict(fox_w_ada=gf["w_ada"], fox_w_in=gf["w_in"], fox_w_out=gf["w_out"],
                sb_w_ada=gs["w_ada"], sb_w_in=gs["w_in"], sb_w_out=gs["w_out"])
    shard_major = []
    for n in big_names:
        a = part[n]
        a = a.reshape(4, a.shape[0] // 4, a.shape[1]) if n.endswith("w_out") else _cols_to_shards(a)
        shard_major.append(a.reshape(4, 2, a.shape[1] // 2, a.shape[2]))
    core = lax.axis_index("c")
    got = _pair_exchange(shard_major)
    pair_f32, pair_bf16 = [], []
    for n, g4, b in zip(big_names, shard_major, got):
        a = lax.dynamic_index_in_dim(g4, core, axis=1, keepdims=False)
        r, C = a.shape[1:]
        s32, s16 = _ew_sum([a.reshape(4 * r, C), b.reshape(4 * r, C)], n + "_pair_sum", also_bf16=True)
        pair_f32.append(s32.reshape(4, r, C))
        pair_bf16.append(s16.reshape(4, r, C))
    others = _chip_exchange(pair_bf16)
    reduced_halves = [_ew_sum([lax.dynamic_index_in_dim(a, chip, axis=0, keepdims=False), b[0], b[1], b[2]],
                              n + "_chip_sum")[0] for n, a, b in zip(big_names, pair_f32, others)]
    theirs = _pair_share(reduced_halves)
    grad_big = {}
    for n, a, b in zip(big_names, reduced_halves, theirs):
        grad_big[n] = jnp.concatenate([jnp.where(core == 0, a, b), jnp.where(core == 0, b, a)], axis=0)

    pieces = [loss, gf["norm_g"], gf["b_ada"], jnp.concatenate([gf["b_f"], jnp.zeros((1, LANES - H), F32)], axis=1),
              gs["norm_g"], gs["b_ada"], dgf]
    vec = jnp.concatenate(pieces, axis=1)
    red = _allreduce_small(_pad_rows8(vec))[0:1]
    offs = [0]
    for p in pieces:
        offs.append(offs[-1] + p.shape[1])
    r_loss, r_fng, r_fba, r_fbf, r_sng, r_sba, r_fin = [red[:, offs[i]:offs[i + 1]] for i in range(7)]
    small_grads = dict(
        fox_norm_g=r_fng, fox_b_ada=r_fba, fox_b_f=r_fbf[:, :H],
        sb_norm_g=lax.dynamic_slice_in_dim(r_sng, chip * (D // 4), D // 4, axis=1),
        sb_b_ada=lax.dynamic_slice_in_dim(r_sba, chip * (3 * D // 4), 3 * D // 4, axis=1),
        final_norm_g=r_fin)

    weights = dict(fox_norm_g=fox_norm_g, fox_w_ada=fox_w_ada, fox_b_ada=fox_b_ada, fox_w_in=fox_w_in, fox_b_f=fox_b_f,
                   fox_w_out=fox_w_out, sb_norm_g=sb_norm_g, sb_w_ada=sb_w_ada, sb_b_ada=sb_b_ada, sb_w_in=sb_w_in,
                   sb_w_out=sb_w_out, final_norm_g=final_norm_g)
    ms = dict(fox_norm_g=m_fox_norm_g, fox_w_ada=m_fox_w_ada, fox_b_ada=m_fox_b_ada, fox_w_in=m_fox_w_in,
              fox_b_f=m_fox_b_f, fox_w_out=m_fox_w_out, sb_norm_g=m_sb_norm_g, sb_w_ada=m_sb_w_ada,
              sb_b_ada=m_sb_b_ada, sb_w_in=m_sb_w_in, sb_w_out=m_sb_w_out, final_norm_g=m_final_norm_g)
    vs = dict(fox_norm_g=v_fox_norm_g, fox_w_ada=v_fox_w_ada, fox_b_ada=v_fox_b_ada, fox_w_in=v_fox_w_in,
              fox_b_f=v_fox_b_f, fox_w_out=v_fox_w_out, sb_norm_g=v_sb_norm_g, sb_w_ada=v_sb_w_ada,
              sb_b_ada=v_sb_b_ada, sb_w_in=v_sb_w_in, sb_w_out=v_sb_w_out, final_norm_g=v_final_norm_g)
    order = ["fox_norm_g", "fox_w_ada", "fox_b_ada", "fox_w_in", "fox_b_f", "fox_w_out", "sb_norm_g", "sb_w_ada",
             "sb_b_ada", "sb_w_in", "sb_w_out", "final_norm_g"]
    grads, deltas, new_m, new_v = {}, {}, {}, {}
    for n in big_names:
        shp = weights[n].shape
        g2 = grad_big[n]
        d, m2, v2 = _adamw(weights[n][0], g2, ms[n][0], vs[n][0], n + "_adamw")
        grads[n], deltas[n], new_m[n], new_v[n] = g2.reshape(shp), d.reshape(shp), m2.reshape(shp), v2.reshape(shp)
    small_names = [n for n in order if n not in big_names]
    sizes = [small_grads[n].shape[1] for n in small_names]
    total = sum(sizes)
    padn = (-total) % LANES

    def pack(d):
        return jnp.concatenate([d[n].reshape(1, -1) for n in small_names] + [jnp.ones((1, padn), F32)], axis=1)

    sd, sm, sv_ = _adamw(pack(weights), pack(small_grads), pack(ms), pack(vs), "small_adamw")
    o = 0
    for n, sz in zip(small_names, sizes):
        shp = weights[n].shape
        grads[n] = small_grads[n].reshape(shp)
        deltas[n], new_m[n], new_v[n] = (t[:, o:o + sz].reshape(shp) for t in (sd, sm, sv_))
        o += sz
    return (r_loss[0, 0], grad_x, *[grads[n] for n in order], *[deltas[n] for n in order],
            *[new_m[n] for n in order], *[new_v[n] for n in order])
```

```python
import functools

import jax
import jax.numpy as jnp
from jax import lax
from jax.experimental import pallas as pl
from jax.experimental.pallas import tpu as pltpu

F32 = jnp.float32
BF16 = jnp.bfloat16
HEAD_DIM = 64
LOG2E = 1.4426950408889634
LN2 = 0.6931471805599453
Q_SCALE = HEAD_DIM ** -0.5 * LOG2E
LANES = 128
NORM_EPS = 1e-6
ADAM_LR = 0.001
ADAM_B1 = 0.9
ADAM_B2 = 0.999
ADAM_EPS = 1e-08
ADAM_WD = 0.01
ADAM_STEP = 10
VMEM_LIMIT = 56 * 1024 * 1024
SB_BLOCK = 256
FOX_BLOCK = 512
MESH = pl.DeviceIdType.MESH
HBM = pl.BlockSpec(memory_space=pltpu.HBM)
NT = (((1,), (1,)), ((), ()))
TN = (((0,), (0,)), ((), ()))


def _call(body, **kw):
    return pl.pallas_call(body, **kw)


def _params(**kw):
    return pltpu.CompilerParams(vmem_limit_bytes=VMEM_LIMIT, **kw)


def _tile(dim, pref, mult=128):
    if dim <= pref:
        return dim
    t = (pref // mult) * mult
    while t >= mult:
        if dim % t == 0:
            return t
        t -= mult
    return dim


def _sigmoid(x):
    return 1.0 / (1.0 + jnp.exp(-x))


def _split3(x):
    hi = x.astype(BF16)
    r = x - hi.astype(F32)
    mid = r.astype(BF16)
    lo = (r - mid.astype(F32)).astype(BF16)
    return hi, mid, lo


def _mm(a, b, mode, out_dtype, name, tm=512, tn=512, tk=512, col_scale=None, rider=None):
    a_slabs = a.shape[0] if a.ndim == 3 else 0
    b_slabs = b.shape[0] if b.ndim == 3 else 0
    if mode == "nn":
        (M, K), (_, N) = a.shape, b.shape
    elif mode == "nt":
        M, K = (a.shape[1], a_slabs * a.shape[2]) if a_slabs else a.shape
        N = b.shape[0]
    else:
        K, M = a.shape
        N = b_slabs * b.shape[2] if b_slabs else b.shape[1]
    tm, tn, tk = _tile(M, tm), _tile(N, tn), _tile(K, tk)
    if a_slabs:
        tk = _tile(a.shape[2], tk)
    if b_slabs:
        tn = _tile(b.shape[2], tn)
    nk = K // tk
    dims = {"nn": (((1,), (0,)), ((), ())), "nt": NT, "tn": TN}[mode]

    r_ins = rider["ins"] if rider else []
    r_outs = rider["outs"] if rider else []
    r_sems = rider["sems"] if rider else []
    nc = 0 if col_scale is None else 1
    ni, no = len(r_ins), len(r_outs)
    grid = (M // tm, N // tn, nk)

    def body(a_ref, b_ref, *rest):
        o_ref = rest[nc + ni]
        acc_ref = rest[nc + ni + 1 + no]
        k = pl.program_id(2)
        if rider:
            start, finish = rider["steps"](rest[nc:nc + ni], rest[nc + ni + 1:nc + ni + 1 + no], *rest[nc + ni + 2 + no:])
            at = [pl.program_id(d) for d in range(3)]
            pl.when(jnp.logical_and(jnp.logical_and(at[0] == 0, at[1] == 0), at[2] == 0))(start)

        @pl.when(k == 0)
        def _():
            acc_ref[...] = jnp.zeros_like(acc_ref)

        acc_ref[...] += lax.dot_general(a_ref[...], b_ref[...], dims, preferred_element_type=F32)

        @pl.when(k == nk - 1)
        def _():
            acc = acc_ref[...]
            if col_scale is not None:
                acc = acc * rest[0][...]
            o_ref[...] = acc.astype(out_dtype)

        if rider:
            pl.when(jnp.logical_and(jnp.logical_and(at[0] == grid[0] - 1, at[1] == grid[1] - 1), at[2] == nk - 1))(finish)

    if a_slabs:
        per = a.shape[2] // tk
        a_spec = pl.BlockSpec((None, tm, tk), lambda i, j, k: (k // per, i, k % per))
    elif mode == "tn":
        a_spec = pl.BlockSpec((tk, tm), lambda i, j, k: (k, i))
    else:
        a_spec = pl.BlockSpec((tm, tk), lambda i, j, k: (i, k))
    if b_slabs:
        per_b = b.shape[2] // tn
        b_spec = pl.BlockSpec((None, tk, tn), lambda i, j, k: (j // per_b, k, j % per_b))
    elif mode == "nt":
        b_spec = pl.BlockSpec((tn, tk), lambda i, j, k: (j, k))
    else:
        b_spec = pl.BlockSpec((tk, tn), lambda i, j, k: (k, j))
    extra_specs = [] if col_scale is None else [pl.BlockSpec((1, tn), lambda i, j, k: (0, j))]
    extra = [] if col_scale is None else [col_scale]
    res = _call(
        body, name=name, grid=grid,
        in_specs=[a_spec, b_spec] + extra_specs + [HBM] * ni,
        out_specs=[pl.BlockSpec((tm, tn), lambda i, j, k: (i, j))] + [HBM] * no,
        out_shape=[jax.ShapeDtypeStruct((M, N), out_dtype)] + list(r_outs),
        scratch_shapes=[pltpu.VMEM((tm, tn), F32)] + list(r_sems), compiler_params=_params(),
    )(a, b, *extra, *r_ins)
    return (res[0], res[1:]) if rider else res[0]


def _mod_fwd(c8, w_ada, b_ada, name):
    D, N = w_ada.shape
    tn = _tile(N, 512)

    def body(c_ref, w_ref, b_ref, o_ref):
        c = c_ref[...]
        sc = (c * _sigmoid(c)).astype(BF16)
        o_ref[...] = jnp.dot(sc, w_ref[...], preferred_element_type=F32) + b_ref[...]

    return _call(
        body, name=name, grid=(N // tn,),
        in_specs=[pl.BlockSpec((8, D), lambda j: (0, 0)), pl.BlockSpec((D, tn), lambda j: (0, j)),
                  pl.BlockSpec((1, tn), lambda j: (0, j))],
        out_specs=pl.BlockSpec((8, tn), lambda j: (0, j)),
        out_shape=jax.ShapeDtypeStruct((8, N), F32), compiler_params=_params(),
    )(c8, w_ada, b_ada)


def _mod_bwd(cT, dmod8, nb, name):
    D = cT.shape[0]
    N = dmod8.shape[1]
    tn = _tile(N, 512)

    def body(c_ref, d_ref, w_ref, b_ref):
        c = c_ref[...]
        sc = c * _sigmoid(c)
        d = d_ref[...]
        acc = sc[:, 0:1] * d[0:1, :]
        bsum = d[0:1, :]
        for b in range(1, nb):
            acc = acc + sc[:, b:b + 1] * d[b:b + 1, :]
            bsum = bsum + d[b:b + 1, :]
        w_ref[...] = acc
        b_ref[...] = bsum

    return _call(
        body, name=name, grid=(N // tn,),
        in_specs=[pl.BlockSpec((D, 8), lambda j: (0, 0)), pl.BlockSpec((8, tn), lambda j: (0, j))],
        out_specs=[pl.BlockSpec((D, tn), lambda j: (0, j)), pl.BlockSpec((1, tn), lambda j: (0, j))],
        out_shape=[jax.ShapeDtypeStruct((D, N), F32), jax.ShapeDtypeStruct((1, N), F32)],
        compiler_params=_params(),
    )(cT, dmod8)


def _ln_proj(x, shift, scale, g, w, S, name, gather=()):
    T, D = x.shape
    N = w.shape[1]
    tm = _tile(S, 2048)
    tn = _tile(N, 1024)
    per_b = S // tm
    ng = len(gather)
    n0, n1 = T // tm, N // tn

    def body(x_ref, sh_ref, sc_ref, g_ref, w_ref, *rest):
        ins_h, (p_ref, h_ref), outs_h, sems = rest[:ng], rest[ng:ng + 2], rest[ng + 2:2 * ng + 2], rest[2 * ng + 2:]
        i, j = pl.program_id(0), pl.program_id(1)
        if ng:
            start, finish = _gather_steps(ins_h, outs_h, *sems)
            pl.when(jnp.logical_and(i == 0, j == 0))(start)

        @pl.when(j == 0)
        def _():
            xv = x_ref[...]
            r = lax.rsqrt(jnp.mean(xv * xv, axis=-1, keepdims=True) + NORM_EPS)
            h = (xv * r) * g_ref[...] * (1.0 + sc_ref[0]) + sh_ref[0]
            h_ref[...] = h.astype(BF16)

        p_ref[...] = jnp.dot(h_ref[...], w_ref[...], preferred_element_type=F32).astype(BF16)
        if ng:
            pl.when(jnp.logical_and(i == n0 - 1, j == n1 - 1))(finish)

    res = _call(
        body, name=name, grid=(n0, n1),
        in_specs=[pl.BlockSpec((tm, D), lambda i, j: (i, 0)),
                  pl.BlockSpec((1, 1, D), lambda i, j: (i // per_b, 0, 0)),
                  pl.BlockSpec((1, 1, D), lambda i, j: (i // per_b, 0, 0)),
                  pl.BlockSpec((1, D), lambda i, j: (0, 0)),
                  pl.BlockSpec((D, tn), lambda i, j: (0, j))] + [HBM] * ng,
        out_specs=[pl.BlockSpec((tm, tn), lambda i, j: (i, j)), pl.BlockSpec((tm, D), lambda i, j: (i, 0))] + [HBM] * ng,
        out_shape=[jax.ShapeDtypeStruct((T, N), BF16), jax.ShapeDtypeStruct((T, D), BF16)]
        + [jax.ShapeDtypeStruct((4,) + a.shape, a.dtype) for a in gather],
        scratch_shapes=_gather_sems(ng) if ng else [], compiler_params=_params(),
    )(x, shift, scale, g, w, *gather)
    return res[0], res[1], res[2:]


def _ln_bwd(dhs, x, dxo, scale, g, S, name):
    T, D = x.shape
    B = T // S
    tm = _tile(S, 512)
    per_b = S // tm

    nd = len(dhs)

    def body(*refs):
        x_ref, dxo_ref, sc_ref, g_ref, dx_ref, dsh_ref, dsc_ref, dg_ref = refs[nd:]
        i = pl.program_id(0)
        xv = x_ref[...]
        dh_v = refs[0][...]
        for r in refs[1:nd]:
            dh_v = dh_v + r[...]
        r = lax.rsqrt(jnp.mean(xv * xv, axis=-1, keepdims=True) + NORM_EPS)
        xn = xv * r
        gv = g_ref[...]
        one_sc = 1.0 + sc_ref[0]
        dhxn = dh_v * xn

        @pl.when(i % per_b == 0)
        def _():
            dsh_ref[...] = jnp.zeros_like(dsh_ref)
            dsc_ref[...] = jnp.zeros_like(dsc_ref)

        @pl.when(i == 0)
        def _():
            dg_ref[...] = jnp.zeros_like(dg_ref)

        dsh_ref[0] += jnp.sum(dh_v, axis=0, keepdims=True)
        dsc_ref[0] += jnp.sum(dhxn, axis=0, keepdims=True) * gv
        dg_ref[...] += jnp.sum(dhxn, axis=0, keepdims=True) * one_sc
        dxn = dh_v * (gv * one_sc)
        dx_ref[...] = r * (dxn - xn * jnp.mean(dxn * xn, axis=-1, keepdims=True)) + dxo_ref[...]

    row = pl.BlockSpec((tm, D), lambda i: (i, 0))
    per = pl.BlockSpec((1, 1, D), lambda i: (i // per_b, 0, 0))
    vec = pl.BlockSpec((1, D), lambda i: (0, 0))
    return _call(
        body, name=name, grid=(T // tm,),
        in_specs=[row] * (nd + 2) + [per, vec], out_specs=[row, per, per, vec],
        out_shape=[jax.ShapeDtypeStruct((T, D), F32), jax.ShapeDtypeStruct((B, 1, D), F32),
                   jax.ShapeDtypeStruct((B, 1, D), F32), jax.ShapeDtypeStruct((1, D), F32)],
        compiler_params=_params(),
    )(*dhs, x, dxo, scale, g)


def _gate_out(o, proj, w_out, x, gate, S, name):
    T, DI = o.shape
    D = w_out.shape[1]
    tm = _tile(S, 256)
    per_b = S // tm

    def body(o_ref, z_ref, w_ref, x_ref, g_ref, xo_ref, y_ref, u_ref):
        z = z_ref[...].astype(F32)
        u = (o_ref[...] * (z * _sigmoid(z))).astype(BF16)
        u_ref[...] = u
        y = jnp.dot(u, w_ref[...], preferred_element_type=F32)
        y_ref[...] = y
        xo_ref[...] = x_ref[...] + g_ref[0] * y

    wide = pl.BlockSpec((tm, DI), lambda i: (i, 0))
    row = pl.BlockSpec((tm, D), lambda i: (i, 0))
    return _call(
        body, name=name, grid=(T // tm,),
        in_specs=[wide, pl.BlockSpec((tm, DI), lambda i: (i, 3)), pl.BlockSpec((DI, D), lambda i: (0, 0)), row,
                  pl.BlockSpec((1, 1, D), lambda i: (i // per_b, 0, 0))],
        out_specs=[row, row, wide],
        out_shape=[jax.ShapeDtypeStruct((T, D), F32), jax.ShapeDtypeStruct((T, D), F32),
                   jax.ShapeDtypeStruct((T, DI), BF16)],
        compiler_params=_params(),
    )(o, proj, w_out, x, gate)


def _out_bwd(dxo, y, gate, w_out, o, proj, S, name):
    T, D = dxo.shape
    DI = o.shape[1]
    B = T // S
    tm = _tile(S, 256)
    per_b = S // tm

    def body(dxo_ref, y_ref, g_ref, w_ref, o_ref, z_ref, dy_ref, do_ref, dz_ref, dg_ref):
        dxo_v = dxo_ref[...]
        dy = (dxo_v * g_ref[0]).astype(BF16)
        dy_ref[...] = dy
        du = lax.dot_general(dy, w_ref[...], NT, preferred_element_type=F32)
        z = z_ref[...].astype(F32)
        sg = _sigmoid(z)
        do_ref[...] = (du * (z * sg)).astype(BF16)
        dz_ref[...] = (du * o_ref[...] * (sg * (1.0 + z * (1.0 - sg)))).astype(BF16)

        @pl.when(pl.program_id(0) % per_b == 0)
        def _():
            dg_ref[...] = jnp.zeros_like(dg_ref)

        dg_ref[0] += jnp.sum(dxo_v * y_ref[...], axis=0, keepdims=True)

    wide = pl.BlockSpec((tm, DI), lambda i: (i, 0))
    row = pl.BlockSpec((tm, D), lambda i: (i, 0))
    per = pl.BlockSpec((1, 1, D), lambda i: (i // per_b, 0, 0))
    return _call(
        body, name=name, grid=(T // tm,),
        in_specs=[row, row, per, pl.BlockSpec((DI, D), lambda i: (0, 0)), wide,
                  pl.BlockSpec((tm, DI), lambda i: (i, 3))],
        out_specs=[row, wide, wide, per],
        out_shape=[jax.ShapeDtypeStruct((T, D), BF16), jax.ShapeDtypeStruct((T, DI), BF16),
                   jax.ShapeDtypeStruct((T, DI), BF16), jax.ShapeDtypeStruct((B, 1, D), F32)],
        compiler_params=_params(),
    )(dxo, y, gate, w_out, o, proj)


def _final_loss(x, tgt, g, S, name):
    T, D = x.shape
    tm = _tile(S, 512)

    def body(x_ref, t_ref, g_ref, dx_ref, dg_ref, l_ref):
        @pl.when(pl.program_id(0) == 0)
        def _():
            dg_ref[...] = jnp.zeros_like(dg_ref)
            l_ref[...] = jnp.zeros_like(l_ref)

        xv = x_ref[...]
        gv = g_ref[...]
        r = lax.rsqrt(jnp.mean(xv * xv, axis=-1, keepdims=True) + NORM_EPS)
        xn = xv * r
        e = xn * gv - t_ref[...]
        part = jnp.sum(jnp.sum(e * e, axis=0, keepdims=True), axis=1, keepdims=True)
        l_ref[...] += (0.5 / D) * part
        dy = e * (1.0 / D)
        dg_ref[...] += jnp.sum(dy * xn, axis=0, keepdims=True)
        dxn = dy * gv
        dx_ref[...] = r * (dxn - xn * jnp.mean(dxn * xn, axis=-1, keepdims=True))

    row = pl.BlockSpec((tm, D), lambda i: (i, 0))
    return _call(
        body, name=name, grid=(T // tm,),
        in_specs=[row, row, pl.BlockSpec((1, D), lambda i: (0, 0))],
        out_specs=[row, pl.BlockSpec((1, D), lambda i: (0, 0)), pl.BlockSpec((1, LANES), lambda i: (0, 0))],
        out_shape=[jax.ShapeDtypeStruct((T, D), F32), jax.ShapeDtypeStruct((1, D), F32),
                   jax.ShapeDtypeStruct((1, LANES), F32)],
        compiler_params=_params(),
    )(x, tgt, g)


def _cum_fwd(fl, bf, name):
    B, S, _ = fl.shape
    ch = _tile(S, 256, 8)

    def body(fl_ref, b_ref, cum_ref):
        ri = lax.broadcasted_iota(jnp.int32, (ch, ch), 0)
        ci = lax.broadcasted_iota(jnp.int32, (ch, ch), 1)
        tri = jnp.where(ri >= ci, 1.0, 0.0).astype(BF16)

        def step(i, carry):
            r0 = pl.multiple_of(i * ch, ch)
            z = fl_ref[0, pl.ds(r0, ch), :] + b_ref[...]
            lf = (jnp.minimum(z, 0.0) - jnp.log(1.0 + jnp.exp(-jnp.abs(z)))) * LOG2E
            hi, mid, lo = _split3(lf)
            cs = (jnp.dot(tri, hi, preferred_element_type=F32) + jnp.dot(tri, mid, preferred_element_type=F32)
                  + jnp.dot(tri, lo, preferred_element_type=F32)) + carry
            cum_ref[0, pl.ds(r0, ch), :] = cs
            return cs[ch - 1:ch, :]

        lax.fori_loop(0, S // ch, step, jnp.zeros((1, LANES), F32))

    blk = pl.BlockSpec((1, S, LANES), lambda b: (b, 0, 0))
    return _call(
        body, name=name, grid=(B,), in_specs=[blk, pl.BlockSpec((1, LANES), lambda b: (0, 0))], out_specs=blk,
        out_shape=jax.ShapeDtypeStruct((B, S, LANES), F32), compiler_params=_params(),
    )(fl, bf)


def _cum_bwd(dcs, fl, bf, name):
    B, S, _ = fl.shape
    ch = _tile(S, 256, 8)
    n = S // ch

    def body(d_ref, fl_ref, b_ref, o_ref, db_ref):
        ri = lax.broadcasted_iota(jnp.int32, (ch, ch), 0)
        ci = lax.broadcasted_iota(jnp.int32, (ch, ch), 1)
        tri = jnp.where(ci >= ri, 1.0, 0.0).astype(BF16)

        @pl.when(pl.program_id(0) == 0)
        def _():
            db_ref[...] = jnp.zeros_like(db_ref)

        def step(t, carry):
            tail, dbsum = carry
            r0 = pl.multiple_of((n - 1 - t) * ch, ch)
            hi, mid, lo = _split3(d_ref[0, pl.ds(r0, ch), :])
            suf = (jnp.dot(tri, hi, preferred_element_type=F32) + jnp.dot(tri, mid, preferred_element_type=F32)
                   + jnp.dot(tri, lo, preferred_element_type=F32)) + tail
            z = fl_ref[0, pl.ds(r0, ch), :] + b_ref[...]
            dfl = -suf * _sigmoid(-z)
            o_ref[0, pl.ds(r0, ch), :] = dfl
            return suf[0:1, :], dbsum + jnp.sum(dfl, axis=0, keepdims=True)

        z1 = jnp.zeros((1, LANES), F32)
        _, dbsum = lax.fori_loop(0, n, step, (z1, z1))
        db_ref[...] += dbsum

    blk = pl.BlockSpec((1, S, LANES), lambda b: (b, 0, 0))
    vec = pl.BlockSpec((1, LANES), lambda b: (0, 0))
    return _call(
        body, name=name, grid=(B,), in_specs=[blk, blk, vec], out_specs=[blk, vec],
        out_shape=[jax.ShapeDtypeStruct((B, S, LANES), F32), jax.ShapeDtypeStruct((1, LANES), F32)],
        compiler_params=_params(),
    )(dcs, fl, bf)


HEADS_PER_STEP = 4
GROUP = 2 * HEAD_DIM


def _step_width():
    return HEAD_DIM * HEADS_PER_STEP


def _cols(S, offset_blocks=0):
    return pl.BlockSpec((S, _step_width()), lambda b, h: (b, offset_blocks + h))


def _row_spec(nq, tq):
    return pl.BlockSpec((1, HEADS_PER_STEP, nq, 1, tq), lambda b, h: (b, h, 0, 0, 0))


def _lanes(g):
    return slice(GROUP * (g // 2), GROUP * (g // 2) + GROUP)


def _hi_lo(x):
    hi = x.astype(BF16)
    return hi, (x - hi.astype(F32)).astype(BF16)


def _dot(a, b, dims=None):
    if dims is None:
        return jnp.dot(a, b, preferred_element_type=F32)
    return lax.dot_general(a, b, dims, preferred_element_type=F32)


def _causal_blocks(nq, prep, init, stages, finish, combine=None, descending=False):
    heads = range(HEADS_PER_STEP)

    def qloop(qi, _):
        ctx = [prep(g, qi) for g in heads]

        def step(kj, carry, masked):
            st = list(carry)
            for n, stage in enumerate(stages):
                if combine is not None and n == len(stages) - 1:
                    combine(kj, ctx, st)
                st = [stage(g, ctx[g], kj, masked, st[g]) for g in heads]
            return tuple(st)

        carry = tuple(init() for _ in heads)
        if descending:
            carry = step(qi, carry, True)
            carry = lax.fori_loop(0, qi, lambda t, cr: step(qi - 1 - t, cr, False), carry)
        else:
            carry = lax.fori_loop(0, qi, lambda kj, cr: step(kj, cr, False), carry)
            carry = step(qi, carry, True)
        finish(qi, ctx, carry)
        return 0

    lax.fori_loop(0, nq, qloop, 0)


class _Block:
    def __init__(self, tq):
        self.tq = tq
        self.lane = lax.broadcasted_iota(jnp.int32, (tq, GROUP), 1)
        self.low = self.lane < HEAD_DIM
        self.ri = lax.broadcasted_iota(jnp.int32, (tq, tq), 0)
        self.ci = lax.broadcasted_iota(jnp.int32, (tq, tq), 1)

    def rows(self, i):
        return pl.ds(pl.multiple_of(i * self.tq, self.tq), self.tq)

    def own(self, g, x):
        return jnp.where(self.low if g % 2 == 0 else jnp.logical_not(self.low), x, jnp.zeros_like(x))

    def pair(self, a, b):
        return jnp.where(self.low, a, b)

    def stat(self, g, x):
        return jnp.sum(jnp.where(self.lane == HEAD_DIM * (g % 2), x, 0.0), axis=1, keepdims=True)


def _fox_fwd(proj, cumcol, cumrow, name):
    T, DI = proj.shape[0], proj.shape[1] // 4
    B, H, nq, _, tq = cumrow.shape
    S = nq * tq
    nb = DI // _step_width()

    def body(q_ref, k_ref, v_ref, cc_ref, cr_ref, o_ref, st_ref, acc_scr):
        h0 = pl.program_id(1) * HEADS_PER_STEP
        blk = _Block(tq)

        def prep(g, qi):
            acc_scr[g] = jnp.zeros((tq, GROUP), F32)
            q = blk.own(g, q_ref[blk.rows(qi), _lanes(g)])
            ccol = jnp.sum(jnp.where(blk.lane == h0 + g, cc_ref[0, blk.rows(qi), :], 0.0), axis=1, keepdims=True)
            return q, ccol

        def init():
            return jnp.full((tq, 1), -jnp.inf, F32), jnp.zeros((tq, 1), F32)

        def scores(g, ctx, kj, masked, st):
            return st + (_dot(ctx[0], k_ref[blk.rows(kj), _lanes(g)], NT),)

        def softmax(g, ctx, kj, masked, st):
            m, l, s = st
            s = s + ctx[1] - cr_ref[0, g, kj]
            if masked:
                s = jnp.where(blk.ci <= blk.ri, s, -jnp.inf)
            m_new = jnp.maximum(m, jnp.max(s, axis=1, keepdims=True))
            alpha = jnp.exp2(m - m_new)
            p = jnp.exp2(s - m_new)
            return (m_new, alpha * l + jnp.sum(p, axis=1, keepdims=True), alpha) + _hi_lo(p)

        def values(g, ctx, kj, masked, st):
            m, l, alpha, hi, lo = st
            v = v_ref[blk.rows(kj), _lanes(g)]
            acc_scr[g] = alpha * acc_scr[g] + (_dot(hi, v) + _dot(lo, v))
            return m, l

        def finish(qi, ctx, carry):
            for g in range(0, HEADS_PER_STEP, 2):
                (m0, l0), (m1, l1) = carry[g], carry[g + 1]
                o_ref[blk.rows(qi), _lanes(g)] = blk.pair(acc_scr[g] / l0, acc_scr[g + 1] / l1)
                st_ref[blk.rows(qi), _lanes(g)] = blk.pair(m0 + jnp.log2(l0), m1 + jnp.log2(l1))

        _causal_blocks(nq, prep, init, [scores, softmax, values], finish)

    out = jax.ShapeDtypeStruct((T, DI), F32)
    return _call(
        body, name=name, grid=(B, H // HEADS_PER_STEP),
        in_specs=[_cols(S), _cols(S, nb), _cols(S, 2 * nb), pl.BlockSpec((1, S, LANES), lambda b, h: (b, 0, 0)),
                  _row_spec(nq, tq)],
        out_specs=[_cols(S), _cols(S)], out_shape=[out, out],
        scratch_shapes=[pltpu.VMEM((HEADS_PER_STEP, tq, GROUP), F32)], compiler_params=_params(),
    )(proj, proj, proj, cumcol, cumrow)


def _fox_bwd(proj, do, dzg, o, stat, cumcol, cumrow, name):
    T, DI = do.shape
    B, H, nq, _, tq = cumrow.shape
    S = nq * tq
    nb = DI // _step_width()

    def body(q_ref, k_ref, v_ref, do_ref, dz_ref, o_ref, st_ref, cc_ref, cr_ref, dqkv_ref, dcs_ref, dk_acc, dv_acc,
             dq_scr):
        h0 = pl.program_id(1) * HEADS_PER_STEP
        blk = _Block(tq)
        dk_acc[...] = jnp.zeros_like(dk_acc)
        dv_acc[...] = jnp.zeros_like(dv_acc)
        dcs_ref[...] = jnp.zeros_like(dcs_ref)

        def prep(g, qi):
            dq_scr[g] = jnp.zeros((tq, GROUP), F32)
            q = blk.own(g, q_ref[blk.rows(qi), _lanes(g)])
            dout = blk.own(g, do_ref[blk.rows(qi), _lanes(g)])
            delta = jnp.sum(o_ref[blk.rows(qi), _lanes(g)] * dout.astype(F32), axis=1, keepdims=True)
            lse = blk.stat(g, st_ref[blk.rows(qi), _lanes(g)])
            ccol = jnp.sum(jnp.where(blk.lane == h0 + g, cc_ref[0, blk.rows(qi), :], 0.0), axis=1, keepdims=True)
            return q, dout, lse, delta, ccol

        def init():
            return ()

        def scores(g, ctx, kj, masked, st):
            return (_dot(ctx[0], k_ref[blk.rows(kj), _lanes(g)], NT), _dot(ctx[1], v_ref[blk.rows(kj), _lanes(g)], NT))

        def softmax_bwd(g, ctx, kj, masked, st):
            s, dp = st
            _, _, lse, delta, ccol = ctx
            s = s + ccol - cr_ref[0, g, kj]
            if masked:
                s = jnp.where(blk.ci <= blk.ri, s, -jnp.inf)
            p = jnp.exp2(s - lse)
            ds = p * (dp - delta)
            return p.astype(BF16), ds.astype(BF16), jnp.sum(ds, axis=0, keepdims=True)

        def combine(kj, ctx, st):
            for g in range(0, HEADS_PER_STEP, 2):
                dv_acc[blk.rows(kj), _lanes(g)] += _dot(st[g][0], ctx[g][1], TN) + _dot(st[g + 1][0], ctx[g + 1][1], TN)
                dk_acc[blk.rows(kj), _lanes(g)] += _dot(st[g][1], ctx[g][0], TN) + _dot(st[g + 1][1], ctx[g + 1][0], TN)
            for g in range(HEADS_PER_STEP):
                dcs_ref[0, g, kj] += st[g][2]

        def queries(g, ctx, kj, masked, st):
            dq_scr[g] += _dot(st[1], blk.own(g, k_ref[blk.rows(kj), _lanes(g)]))
            return ()

        def finish(qi, ctx, carry):
            for g in range(0, HEADS_PER_STEP, 2):
                dqkv_ref[0, blk.rows(qi), _lanes(g)] = ((dq_scr[g] + dq_scr[g + 1]) * LN2).astype(BF16)

        _causal_blocks(nq, prep, init, [scores, softmax_bwd, queries], finish, combine=combine)
        dqkv_ref[1] = (dk_acc[...] * LN2).astype(BF16)
        dqkv_ref[2] = dv_acc[...].astype(BF16)
        dqkv_ref[3] = dz_ref[...]

    W = _step_width()
    return _call(
        body, name=name, grid=(B, H // HEADS_PER_STEP),
        in_specs=[_cols(S), _cols(S, nb), _cols(S, 2 * nb), _cols(S), _cols(S), _cols(S), _cols(S),
                  pl.BlockSpec((1, S, LANES), lambda b, h: (b, 0, 0)), _row_spec(nq, tq)],
        out_specs=[pl.BlockSpec((4, S, W), lambda b, h: (0, b, h)), _row_spec(nq, tq)],
        out_shape=[jax.ShapeDtypeStruct((4, T, DI), BF16), jax.ShapeDtypeStruct((B, H, nq, 1, tq), F32)],
        scratch_shapes=[pltpu.VMEM((S, W), F32), pltpu.VMEM((S, W), F32), pltpu.VMEM((HEADS_PER_STEP, tq, GROUP), F32)],
        compiler_params=_params(),
    )(proj, proj, proj, do, dzg, o, stat, cumcol, cumrow)


def _log2_keep(z2):
    nz = -z2
    e = jnp.exp2(jnp.minimum(z2, nz))
    return jnp.minimum(nz, 0.0) - jnp.log2(1.0 + e), e


def _sb_fwd(proj, B, tq, name):
    T, DI = proj.shape[0], proj.shape[1] // 4
    S = T // B
    H = DI // HEAD_DIM
    nq = S // tq
    nb = DI // _step_width()

    def body(q_ref, k_ref, v_ref, o_ref, st_ref, acc_scr, c_scr):
        blk = _Block(tq)
        strict = blk.ci < blk.ri
        above = jnp.where(blk.ri > blk.ci, 1.0, 0.0).astype(BF16)

        def prep(g, qi):
            acc_scr[g] = jnp.zeros((tq, GROUP), F32)
            c_scr[g] = jnp.zeros((tq, 1), F32)
            return blk.own(g, q_ref[blk.rows(qi), _lanes(g)])

        def init():
            return ()

        def scores(g, q, kj, masked, st):
            return (_dot(q, k_ref[blk.rows(kj), _lanes(g)], NT),)

        def logs(g, q, kj, masked, st):
            (z,) = st
            lk, _ = _log2_keep(z)
            lb = z + lk
            if masked:
                lk = jnp.where(strict, lk, 0.0)
            c = c_scr[g]
            c_scr[g] = c + jnp.sum(lk, axis=1, keepdims=True)
            return (lb + c,) + _hi_lo(lk)

        def suffix(g, q, kj, masked, st):
            lbc, hi, lo = st
            return lbc, _dot(hi, above) + _dot(lo, above)

        def weights(g, q, kj, masked, st):
            lbc, after = st
            a = jnp.exp2(lbc + after)
            if masked:
                a = jnp.where(strict, a, 0.0)
            return (a.astype(BF16),)

        def values(g, q, kj, masked, st):
            acc_scr[g] += _dot(st[0], v_ref[blk.rows(kj), _lanes(g)])
            return ()

        def finish(qi, ctx, carry):
            for g in range(0, HEADS_PER_STEP, 2):
                o_ref[blk.rows(qi), _lanes(g)] = blk.pair(acc_scr[g], acc_scr[g + 1])
                st_ref[blk.rows(qi), _lanes(g)] = blk.pair(c_scr[g], c_scr[g + 1])

        _causal_blocks(nq, prep, init, [scores, logs, suffix, weights, values], finish, descending=True)

    out = jax.ShapeDtypeStruct((T, DI), F32)
    return _call(
        body, name=name, grid=(B, H // HEADS_PER_STEP), in_specs=[_cols(S), _cols(S, nb), _cols(S, 2 * nb)],
        out_specs=[_cols(S), _cols(S)], out_shape=[out, out],
        scratch_shapes=[pltpu.VMEM((HEADS_PER_STEP, tq, GROUP), F32), pltpu.VMEM((HEADS_PER_STEP, tq, 1), F32)],
        compiler_params=_params(),
    )(proj, proj, proj)


def _sb_bwd(proj, do, dzg, stat, B, tq, name):
    T, DI = do.shape
    S = T // B
    H = DI // HEAD_DIM
    nq = S // tq
    nb = DI // _step_width()

    def body(q_ref, k_ref, v_ref, do_ref, dz_ref, st_ref, dqkv_ref, dk_acc, dv_acc, dq_scr):
        blk = _Block(tq)
        strict = blk.ci < blk.ri
        upto = jnp.where(blk.ri <= blk.ci, 1.0, 0.0).astype(BF16)
        before = jnp.where(blk.ri < blk.ci, 1.0, 0.0).astype(BF16)
        dk_acc[...] = jnp.zeros_like(dk_acc)
        dv_acc[...] = jnp.zeros_like(dv_acc)

        def prep(g, qi):
            dq_scr[g] = jnp.zeros((tq, GROUP), F32)
            return (blk.own(g, q_ref[blk.rows(qi), _lanes(g)]), blk.own(g, do_ref[blk.rows(qi), _lanes(g)]),
                    blk.stat(g, st_ref[blk.rows(qi), _lanes(g)]))

        def init():
            return jnp.zeros((tq, 1), F32), jnp.zeros((tq, 1), F32)

        def scores(g, ctx, kj, masked, st):
            return st + (_dot(ctx[0], k_ref[blk.rows(kj), _lanes(g)], NT),
                         _dot(ctx[1], v_ref[blk.rows(kj), _lanes(g)], NT))

        def logs(g, ctx, kj, masked, st):
            cpre, pg, z, da = st
            lk, e = _log2_keep(z)
            inv = 1.0 / (1.0 + e)
            sig = jnp.where(z >= 0.0, inv, e * inv)
            lbt = (z + lk) + (ctx[2] - cpre)
            if masked:
                lk = jnp.where(strict, lk, 0.0)
            return (cpre + jnp.sum(lk, axis=1, keepdims=True), pg, da, lbt, sig) + _hi_lo(lk)

        def prefix(g, ctx, kj, masked, st):
            cpre, pg, da, lbt, sig, hi, lo = st
            return cpre, pg, da, lbt, sig, _dot(hi, upto) + _dot(lo, upto)

        def weights(g, ctx, kj, masked, st):
            cpre, pg, da, lbt, sig, pre = st
            a = jnp.exp2(lbt - pre)
            if masked:
                a = jnp.where(strict, a, 0.0)
            gr = da * a
            return cpre, pg, sig, a.astype(BF16), gr, gr.astype(BF16)

        def grad_prefix(g, ctx, kj, masked, st):
            cpre, pg, sig, ab, gr, gb = st
            return cpre, pg, sig, ab, gr, _dot(gb, before)

        def dlogits(g, ctx, kj, masked, st):
            cpre, pg, sig, ab, gr, pfx = st
            dz = gr - sig * (gr + (pfx + pg))
            if masked:
                dz = jnp.where(strict, dz, 0.0)
            return cpre, pg + jnp.sum(gr, axis=1, keepdims=True), ab, dz.astype(BF16)

        def combine(kj, ctx, st):
            for g in range(0, HEADS_PER_STEP, 2):
                dv_acc[blk.rows(kj), _lanes(g)] += _dot(st[g][2], ctx[g][1], TN) + _dot(st[g + 1][2], ctx[g + 1][1], TN)
                dk_acc[blk.rows(kj), _lanes(g)] += _dot(st[g][3], ctx[g][0], TN) + _dot(st[g + 1][3], ctx[g + 1][0], TN)

        def queries(g, ctx, kj, masked, st):
            cpre, pg, _, dzb = st
            dq_scr[g] += _dot(dzb, blk.own(g, k_ref[blk.rows(kj), _lanes(g)]))
            return cpre, pg

        def finish(qi, ctx, carry):
            for g in range(0, HEADS_PER_STEP, 2):
                dqkv_ref[0, blk.rows(qi), _lanes(g)] = ((dq_scr[g] + dq_scr[g + 1]) * LN2).astype(BF16)

        _causal_blocks(nq, prep, init, [scores, logs, prefix, weights, grad_prefix, dlogits, queries], finish,
                       combine=combine)
        dqkv_ref[1] = (dk_acc[...] * LN2).astype(BF16)
        dqkv_ref[2] = dv_acc[...].astype(BF16)
        dqkv_ref[3] = dz_ref[...]

    W = _step_width()
    return _call(
        body, name=name, grid=(B, H // HEADS_PER_STEP),
        in_specs=[_cols(S), _cols(S, nb), _cols(S, 2 * nb), _cols(S), _cols(S), _cols(S)],
        out_specs=pl.BlockSpec((4, S, W), lambda b, h: (0, b, h)),
        out_shape=jax.ShapeDtypeStruct((4, T, DI), BF16),
        scratch_shapes=[pltpu.VMEM((S, W), F32), pltpu.VMEM((S, W), F32), pltpu.VMEM((HEADS_PER_STEP, tq, GROUP), F32)],
        compiler_params=_params(),
    )(proj, proj, proj, do, dzg, stat)


def _row_tile(R, C, n_arrays):
    budget = 24 * 1024 * 1024 // (2 * n_arrays * 4 * max(C, LANES))
    return _tile(R, max(8, budget), 8)


def _ew_sum(parts, name, also_bf16=False):
    R, C = parts[0].shape
    tr = _row_tile(R, C, len(parts) + 2)
    n = len(parts)

    def body(*refs):
        acc = refs[0][...].astype(F32) + refs[1][...].astype(F32)
        for r in refs[2:n]:
            acc = acc + r[...].astype(F32)
        refs[n][...] = acc
        if also_bf16:
            refs[n + 1][...] = acc.astype(BF16)

    blk = pl.BlockSpec((tr, C), lambda i: (i, 0))
    out_shape = [jax.ShapeDtypeStruct((R, C), F32)] + ([jax.ShapeDtypeStruct((R, C), BF16)] if also_bf16 else [])
    return _call(
        body, name=name, grid=(R // tr,), in_specs=[blk] * n, out_specs=[blk] * len(out_shape),
        out_shape=out_shape, compiler_params=_params(),
    )(*parts)


def _adamw(w, g, m, v, name):
    R, C = w.shape
    tr = _row_tile(R, C, 7)
    c1 = 1.0 / (1.0 - ADAM_B1 ** ADAM_STEP)
    c2 = 1.0 / (1.0 - ADAM_B2 ** ADAM_STEP)

    def body(w_ref, g_ref, m_ref, v_ref, d_ref, m2_ref, v2_ref):
        gv = g_ref[...]
        m2 = ADAM_B1 * m_ref[...] + (1.0 - ADAM_B1) * gv
        v2 = ADAM_B2 * v_ref[...] + (1.0 - ADAM_B2) * (gv * gv)
        m2_ref[...] = m2
        v2_ref[...] = v2
        d_ref[...] = -ADAM_LR * ((m2 * c1) / (jnp.sqrt(v2 * c2) + ADAM_EPS) + ADAM_WD * w_ref[...])

    blk = pl.BlockSpec((tr, C), lambda i: (i, 0))
    out = jax.ShapeDtypeStruct((R, C), F32)
    return _call(
        body, name=name, grid=(R // tr,), in_specs=[blk] * 4, out_specs=[blk] * 3, out_shape=[out] * 3,
        compiler_params=_params(),
    )(w, g, m, v)


def _me():
    return lax.axis_index("x"), lax.axis_index("y"), lax.axis_index("c")


def _chip_of(x, y):
    return 2 * x + y


def _other_chips(x, y):
    return [(x, 1 - y), (1 - x, y), (1 - x, 1 - y)]


def _gather_steps(ins_h, outs_h, send1, recv1, send2, recv2):
    nh = len(ins_h)
    x, y, c = _me()
    mine = _chip_of(x, y)
    chips = _other_chips(x, y)
    sib = (x, y, 1 - c)

    def landed(i, k, half):
        return outs_h[i].at[_chip_of(*chips[k]), half]

    def first(i, k):
        return pltpu.make_async_remote_copy(
            src_ref=ins_h[i].at[c], dst_ref=outs_h[i].at[mine, c], send_sem=send1.at[i, k], recv_sem=recv1.at[i, k],
            device_id=(*chips[k], c), device_id_type=MESH)

    def passed(i, k):
        return pltpu.make_async_remote_copy(
            src_ref=landed(i, k, c), dst_ref=landed(i, k, c), send_sem=send2.at[i, k], recv_sem=recv2.at[i, k],
            device_id=sib, device_id_type=MESH)

    def start():
        for i in range(nh):
            for k in range(3):
                first(i, k).start()

    def finish():
        for i in range(nh):
            for k in range(3):
                pltpu.make_async_remote_copy(
                    src_ref=ins_h[i].at[c], dst_ref=landed(i, k, c), send_sem=send1.at[i, k], recv_sem=recv1.at[i, k],
                    device_id=(*chips[k], c), device_id_type=MESH).wait_recv()
                passed(i, k).start()
        for i in range(nh):
            for k in range(3):
                pltpu.make_async_remote_copy(
                    src_ref=landed(i, k, c), dst_ref=landed(i, k, 1 - c), send_sem=send2.at[i, k],
                    recv_sem=recv2.at[i, k], device_id=sib, device_id_type=MESH).wait_recv()
        for i in range(nh):
            for k in range(3):
                first(i, k).wait_send()
                passed(i, k).wait_send()

    return start, finish


def _gather_sems(nh):
    return [pltpu.SemaphoreType.DMA((nh, 3)) for _ in range(4)]


def _gather_weights(halves, smalls):
    nh, ns = len(halves), len(smalls)

    def body(*refs):
        ins_h, ins_s = refs[:nh], refs[nh:nh + ns]
        outs_h, outs_s = refs[nh + ns:2 * nh + ns], refs[2 * nh + ns:2 * (nh + ns)]
        send1, recv1, send2, recv2, send3, recv3 = refs[2 * (nh + ns):]
        x, y, c = _me()
        mine = _chip_of(x, y)
        chips = _other_chips(x, y)

        def small(i, k):
            return pltpu.make_async_remote_copy(
                src_ref=ins_s[i], dst_ref=outs_s[i].at[mine], send_sem=send3.at[i, k], recv_sem=recv3.at[i, k],
                device_id=(*chips[k], c), device_id_type=MESH)

        start, finish = _gather_steps(ins_h, outs_h, send1, recv1, send2, recv2)
        start()
        for i in range(ns):
            for k in range(3):
                small(i, k).start()
        finish()
        for i in range(ns):
            for k in range(3):
                pltpu.make_async_remote_copy(
                    src_ref=ins_s[i], dst_ref=outs_s[i].at[_chip_of(*chips[k])], send_sem=send3.at[i, k],
                    recv_sem=recv3.at[i, k], device_id=(*chips[k], c), device_id_type=MESH).wait_recv()
                small(i, k).wait_send()

    out_shape = ([jax.ShapeDtypeStruct((4,) + a.shape, a.dtype) for a in halves]
                 + [jax.ShapeDtypeStruct((4,) + a.shape, a.dtype) for a in smalls])
    n = nh + ns
    res = _call(
        body, name="gather_weights", in_specs=[HBM] * n, out_specs=[HBM] * n, out_shape=out_shape,
        scratch_shapes=_gather_sems(nh) + [pltpu.SemaphoreType.DMA((max(ns, 1), 3)),
                                           pltpu.SemaphoreType.DMA((max(ns, 1), 3))],
        compiler_params=_params(),
    )(*halves, *smalls)
    return res[:nh], res[nh:]


def _plan(ins, outs, sems, copies):
    def steps(in_refs, out_refs, *sem_refs):
        def start():
            for cp in copies(in_refs, out_refs, *sem_refs):
                cp.start()

        def finish():
            for cp in copies(in_refs, out_refs, *sem_refs):
                cp.wait()

        return start, finish

    return dict(ins=list(ins), outs=list(outs), sems=list(sems), steps=steps)


def _pair_exchange_plan(grads):
    n = len(grads)

    def copies(ins, got, send, recv):
        x, y, c = _me()
        return [pltpu.make_async_remote_copy(
            src_ref=ins[i].at[j, 1 - c], dst_ref=got[i].at[j], send_sem=send.at[i, j], recv_sem=recv.at[i, j],
            device_id=(x, y, 1 - c), device_id_type=MESH) for i in range(n) for j in range(4)]

    return _plan(grads, [jax.ShapeDtypeStruct((4,) + g.shape[2:], g.dtype) for g in grads],
                 [pltpu.SemaphoreType.DMA((n, 4)), pltpu.SemaphoreType.DMA((n, 4))], copies)


def _chip_exchange_plan(sums):
    n = len(sums)

    def copies(ins, got, send, recv):
        x, y, c = _me()
        chips = _other_chips(x, y)
        return [pltpu.make_async_remote_copy(
            src_ref=ins[i].at[_chip_of(*chips[k])], dst_ref=got[i].at[k], send_sem=send.at[i, k],
            recv_sem=recv.at[i, k], device_id=(*chips[k], c), device_id_type=MESH) for i in range(n) for k in range(3)]

    return _plan(sums, [jax.ShapeDtypeStruct((3,) + a.shape[1:], a.dtype) for a in sums],
                 [pltpu.SemaphoreType.DMA((n, 3)), pltpu.SemaphoreType.DMA((n, 3))], copies)


def _pair_share_plan(halves):
    n = len(halves)

    def copies(ins, outs, send, recv):
        x, y, c = _me()
        return [pltpu.make_async_remote_copy(
            src_ref=ins[i], dst_ref=outs[i], send_sem=send.at[i], recv_sem=recv.at[i],
            device_id=(x, y, 1 - c), device_id_type=MESH) for i in range(n)]

    return _plan(halves, [jax.ShapeDtypeStruct(h.shape, h.dtype) for h in halves],
                 [pltpu.SemaphoreType.DMA((n,)), pltpu.SemaphoreType.DMA((n,))], copies)


def _run_exchange(plan, name):
    ni, no = len(plan["ins"]), len(plan["outs"])

    def body(*refs):
        start, finish = plan["steps"](refs[:ni], refs[ni:ni + no], *refs[ni + no:])
        start()
        finish()

    return _call(
        body, name=name, in_specs=[HBM] * ni, out_specs=[HBM] * no, out_shape=plan["outs"],
        scratch_shapes=plan["sems"], compiler_params=_params(),
    )(*plan["ins"])


def _allreduce_small(vec):
    P = vec.shape[1]

    def body(v_ref, sum_ref, all_ref, send, recv):
        x, y, c = _me()
        me = 4 * x + 2 * y + c
        all_ref[pl.ds(me, 1)] = v_ref[...][None]
        cps = []
        for d in range(1, 8):
            peer = (jnp.bitwise_xor(x, d >> 2), jnp.bitwise_xor(y, (d >> 1) & 1), jnp.bitwise_xor(c, d & 1))
            r = pltpu.make_async_remote_copy(
                src_ref=v_ref, dst_ref=all_ref.at[me], send_sem=send.at[d - 1], recv_sem=recv.at[d - 1],
                device_id=peer, device_id_type=MESH)
            r.start()
            cps.append(r)
        for d in range(1, 8):
            src = jnp.bitwise_xor(me, d)
            pltpu.make_async_remote_copy(
                src_ref=v_ref, dst_ref=all_ref.at[src], send_sem=send.at[d - 1], recv_sem=recv.at[d - 1],
                device_id=(x, y, c), device_id_type=MESH).wait_recv()
        for r in cps:
            r.wait_send()
        acc = all_ref[0]
        for i in range(1, 8):
            acc = acc + all_ref[i]
        sum_ref[...] = acc

    vm = pl.BlockSpec(memory_space=pltpu.VMEM)
    return _call(
        body, name="allreduce_small", in_specs=[vm], out_specs=[vm, vm],
        out_shape=[jax.ShapeDtypeStruct((8, P), F32), jax.ShapeDtypeStruct((8, 8, P), F32)],
        scratch_shapes=[pltpu.SemaphoreType.DMA((7,)), pltpu.SemaphoreType.DMA((7,))],
        compiler_params=_params(),
    )(vec)[0]


def _per_batch(mod, B, D):
    return [mod[:B, i * D:(i + 1) * D].reshape(B, 1, D) for i in range(3)]


def _pad_rows8(a):
    return jnp.concatenate([a, jnp.zeros((8 - a.shape[0],) + a.shape[1:], a.dtype)], axis=0)


def _layer_fwd(x, c8, w, S, fox, tag, gather=()):
    T, D = x.shape
    B = T // S
    DI = w["w_out"].shape[0]
    H = DI // HEAD_DIM
    tq = _tile(S, FOX_BLOCK if fox else SB_BLOCK, 8)
    mod = _mod_fwd(c8, w["w_ada"], w["b_ada"], tag + "_mod_fwd")
    shift, scale, gate = _per_batch(mod, B, D)
    proj, h, gathered = _ln_proj(x, shift, scale, w["norm_g"], w["w_in"], S, tag + "_ln_proj", gather)
    saved = dict(x=x, h=h, proj=proj, scale=scale, gate=gate, gathered=gathered)
    if fox:
        fl = _mm(h, w["w_f"], "nn", F32, tag + "_flogit").reshape(B, S, LANES)
        cum = _cum_fwd(fl, w["b_f"], tag + "_cum_fwd")
        cumrow = cum[:, :, :H].transpose(0, 2, 1).reshape(B, H, S // tq, 1, tq)
        o, stat = _fox_fwd(proj, cum, cumrow, tag + "_attn_fwd")
        saved.update(fl=fl, cum=cum, cumrow=cumrow)
    else:
        o, stat = _sb_fwd(proj, B, tq, tag + "_attn_fwd")
    xo, y, u = _gate_out(o, proj, w["w_out"], x, gate, S, tag + "_gate_out")
    saved.update(o=o, stat=stat, y=y, u=u)
    return xo, saved


def _hosted(side, sent, call):
    if side is None:
        return call(None), None
    plan, _ = next(side) if sent is None else side.send(sent)
    return call(plan)


def _layer_bwd(dxo, sv, w, cT, S, fox, tag, side=None):
    T, D = dxo.shape
    B = T // S
    DI = w["w_out"].shape[0]
    H = DI // HEAD_DIM
    tq = _tile(S, FOX_BLOCK if fox else SB_BLOCK, 8)
    dy, do, dzg, dgate = _out_bwd(dxo, sv["y"], sv["gate"], w["w_out"], sv["o"], sv["proj"], S, tag + "_out_bwd")
    dw_out, landed = _hosted(side, None, lambda r: _mm(sv["u"], dy, "tn", F32, tag + "_dw_out", tm=1024, tn=1024,
                                                      tk=2048, rider=r))
    g = {"w_out": dw_out}
    q_cols = jnp.where(jnp.arange(4 * DI)[None, :] < DI, Q_SCALE, 1.0).astype(F32)
    if fox:
        dproj, dcs = _fox_bwd(sv["proj"], do, dzg, sv["o"], sv["stat"], sv["cum"], sv["cumrow"], tag + "_attn_bwd")
        dcs = dcs.reshape(B, H, S).transpose(0, 2, 1)
        dcs = jnp.concatenate([dcs, jnp.zeros((B, S, LANES - H), F32)], axis=-1)
        dfl, db_f = _cum_bwd(dcs, sv["fl"], w["b_f"], tag + "_cum_bwd")
        g["b_f"] = db_f[:, :H]
        dfl = dfl.reshape(T, LANES).astype(BF16)
    else:
        dproj = _sb_bwd(sv["proj"], do, dzg, sv["stat"], B, tq, tag + "_attn_bwd")
    g["w_in"], landed = _hosted(side, landed, lambda r: _mm(sv["h"], dproj, "tn", F32, tag + "_dw_in", tm=1024, tn=2048,
                                                            tk=1024, col_scale=q_cols, rider=r))
    dh, landed = _hosted(side, landed, lambda r: _mm(dproj, w["w_in"], "nt", F32, tag + "_dh", tm=2048, tn=1024,
                                                     tk=1024, rider=r))
    dhs = [dh]
    if side is not None:
        try:
            side.send(landed)
        except StopIteration as done:
            g["side"] = done.value
    if fox:
        dw_f = _mm(sv["h"], dfl, "tn", F32, tag + "_dw_f", tm=1024, tn=LANES, tk=2048)
        g["w_in"] = jnp.concatenate([g["w_in"], dw_f[:, :H]], axis=1)
        dhs.append(_mm(dfl, w["w_f"], "nt", F32, tag + "_dh_f", tm=2048, tn=1024, tk=LANES))
    dx, dshift, dscale, dg = _ln_bwd(dhs, sv["x"], dxo, sv["scale"], w["norm_g"], S, tag + "_ln_bwd")
    g["norm_g"] = dg
    dmod = jnp.concatenate([dshift, dscale, dgate], axis=-1).reshape(B, 3 * D)
    g["w_ada"], g["b_ada"] = _mod_bwd(cT, _pad_rows8(dmod), B, tag + "_mod_bwd")
    return dx, g


def _local_step(x3, c, tgt3, wf, ws, final_g, sb_halves=(), sb_side=None):
    B, S, D = x3.shape
    T = B * S
    x = x3.reshape(T, D)
    c8 = _pad_rows8(c)
    cT = c8.T
    x1, sv1 = _layer_fwd(x, c8, wf, S, True, "fox", sb_halves)
    if sb_halves:
        ws = ws(sv1["gathered"])
    x2, sv2 = _layer_fwd(x1, c8, ws, S, False, "sb")
    dx2, dgf, loss = _final_loss(x2, tgt3.reshape(T, D), final_g, S, "final_loss")
    dx1, gs = _layer_bwd(dx2, sv2, ws, cT, S, False, "sb")
    dx0, gf = _layer_bwd(dx1, sv1, wf, cT, S, True, "fox", None if sb_side is None else sb_side(gs))
    return loss, dx0.reshape(B, S, D), gf, gs, dgf


def _cols_to_shards(a):
    R, C4 = a.shape
    return a.reshape(R, 4, C4 // 4).transpose(1, 0, 2)


def _shards_to_cols(a):
    n, R, C = a.shape
    return a.transpose(1, 0, 2).reshape(R, n * C)


def kernel(x, c, fox_norm_g, fox_w_ada, fox_b_ada, fox_w_in, fox_b_f, fox_w_out, sb_norm_g, sb_w_ada, sb_b_ada, sb_w_in, sb_w_out, final_norm_g, loss_target, m_fox_norm_g, m_fox_w_ada, m_fox_b_ada, m_fox_w_in, m_fox_b_f, m_fox_w_out, m_sb_norm_g, m_sb_w_ada, m_sb_b_ada, m_sb_w_in, m_sb_w_out, m_final_norm_g, v_fox_norm_g, v_fox_w_ada, v_fox_b_ada, v_fox_w_in, v_fox_b_f, v_fox_w_out, v_sb_norm_g, v_sb_w_ada, v_sb_b_ada, v_sb_w_in, v_sb_w_out, v_final_norm_g):
    B, S, D = x.shape
    DI = 4 * fox_w_out.shape[1]
    H = DI // HEAD_DIM
    chip = _chip_of(lax.axis_index("x"), lax.axis_index("y"))

    big_names = ["fox_w_ada", "fox_w_in", "fox_w_out", "sb_w_ada", "sb_w_in", "sb_w_out"]
    big = dict(fox_w_ada=fox_w_ada[0], fox_w_in=fox_w_in[0], fox_w_out=fox_w_out[0],
               sb_w_ada=sb_w_ada[0], sb_w_in=sb_w_in[0], sb_w_out=sb_w_out[0])
    for n in ("fox_w_in", "sb_w_in"):
        width = big[n].shape[1]
        is_q = chip * width + jnp.arange(width)[None, :] < DI
        big[n] = big[n] * jnp.where(is_q, Q_SCALE, 1.0).astype(F32)
    halves = {n: big[n].astype(BF16).reshape(2, big[n].shape[0] // 2, big[n].shape[1]) for n in big_names}
    fox_names, sb_names = big_names[:3], big_names[3:]

    def assemble(names, gathered):
        full = {}
        for n, a in zip(names, gathered):
            a = lax.dynamic_update_index_in_dim(a, halves[n], chip, 0)
            a = a.reshape(4, a.shape[1] * a.shape[2], a.shape[3])
            full[n] = a.reshape(4 * a.shape[1], a.shape[2]) if n.endswith("w_out") else _shards_to_cols(a)
        return full

    gathered, gsmall = _gather_weights([halves[n] for n in fox_names], [sb_norm_g, sb_b_ada])
    gsmall = [lax.dynamic_update_index_in_dim(a, own, chip, 0) for a, own in zip(gsmall, [sb_norm_g, sb_b_ada])]
    full = assemble(fox_names, gathered)
    sb_norm_full = gsmall[0].reshape(1, D)
    sb_b_ada_full = gsmall[1].reshape(1, 3 * D)
    w_f = jnp.concatenate([full["fox_w_in"][:, 4 * DI:], jnp.zeros((D, LANES - H), BF16)], axis=1)
    b_f = jnp.concatenate([fox_b_f, jnp.zeros((1, LANES - H), F32)], axis=1)
    wf = dict(w_ada=full["fox_w_ada"], b_ada=fox_b_ada, norm_g=fox_norm_g, w_in=full["fox_w_in"][:, :4 * DI],
              w_f=w_f, b_f=b_f, w_out=full["fox_w_out"])

    def ws(gathered_sb):
        f = assemble(sb_names, gathered_sb)
        return dict(w_ada=f["sb_w_ada"], b_ada=sb_b_ada_full, norm_g=sb_norm_full, w_in=f["sb_w_in"], w_out=f["sb_w_out"])

    core = lax.axis_index("c")

    def reduction(names, part, tag):
        shard_major = []
        for n in names:
            a = part[n]
            a = a.reshape(4, a.shape[0] // 4, a.shape[1]) if n.endswith("w_out") else _cols_to_shards(a)
            shard_major.append(a.reshape(4, 2, a.shape[1] // 2, a.shape[2]))
        got = yield _pair_exchange_plan(shard_major), tag + "_grad_pair_exchange"
        pair_f32, pair_bf16 = [], []
        for n, g4, b in zip(names, shard_major, got):
            a = lax.dynamic_index_in_dim(g4, core, axis=1, keepdims=False)
            r, C = a.shape[1:]
            s32, s16 = _ew_sum([a.reshape(4 * r, C), b.reshape(4 * r, C)], n + "_pair_sum", also_bf16=True)
            pair_f32.append(s32.reshape(4, r, C))
            pair_bf16.append(s16.reshape(4, r, C))
        others = yield _chip_exchange_plan(pair_bf16), tag + "_grad_chip_exchange"
        reduced_halves = [_ew_sum([lax.dynamic_index_in_dim(a, chip, axis=0, keepdims=False), b[0], b[1], b[2]],
                                  n + "_chip_sum")[0] for n, a, b in zip(names, pair_f32, others)]
        theirs = yield _pair_share_plan(reduced_halves), tag + "_grad_pair_share"
        return {n: jnp.concatenate([jnp.where(core == 0, a, b), jnp.where(core == 0, b, a)], axis=0)
                for n, a, b in zip(names, reduced_halves, theirs)}

    def sb_side(gs):
        return reduction(sb_names, dict(sb_w_ada=gs["w_ada"], sb_w_in=gs["w_in"], sb_w_out=gs["w_out"]), "sb")

    loss, grad_x, gf, gs, dgf = _local_step(x, c, loss_target, wf, ws, final_norm_g.reshape(1, D),
                                            [halves[n] for n in sb_names], sb_side)
    grad_big = dict(gf["side"])
    fox_red = reduction(fox_names, dict(fox_w_ada=gf["w_ada"], fox_w_in=gf["w_in"], fox_w_out=gf["w_out"]), "fox")
    try:
        plan, name = next(fox_red)
        while True:
            plan, name = fox_red.send(_run_exchange(plan, name))
    except StopIteration as done:
        grad_big.update(done.value)

    pieces = [loss, gf["norm_g"], gf["b_ada"], jnp.concatenate([gf["b_f"], jnp.zeros((1, LANES - H), F32)], axis=1),
              gs["norm_g"], gs["b_ada"], dgf]
    vec = jnp.concatenate(pieces, axis=1)
    red = _allreduce_small(_pad_rows8(vec))[0:1]
    offs = [0]
    for p in pieces:
        offs.append(offs[-1] + p.shape[1])
    r_loss, r_fng, r_fba, r_fbf, r_sng, r_sba, r_fin = [red[:, offs[i]:offs[i + 1]] for i in range(7)]
    small_grads = dict(
        fox_norm_g=r_fng, fox_b_ada=r_fba, fox_b_f=r_fbf[:, :H],
        sb_norm_g=lax.dynamic_slice_in_dim(r_sng, chip * (D // 4), D // 4, axis=1),
        sb_b_ada=lax.dynamic_slice_in_dim(r_sba, chip * (3 * D // 4), 3 * D // 4, axis=1),
        final_norm_g=r_fin)

    weights = dict(fox_norm_g=fox_norm_g, fox_w_ada=fox_w_ada, fox_b_ada=fox_b_ada, fox_w_in=fox_w_in, fox_b_f=fox_b_f,
                   fox_w_out=fox_w_out, sb_norm_g=sb_norm_g, sb_w_ada=sb_w_ada, sb_b_ada=sb_b_ada, sb_w_in=sb_w_in,
                   sb_w_out=sb_w_out, final_norm_g=final_norm_g)
    ms = dict(fox_norm_g=m_fox_norm_g, fox_w_ada=m_fox_w_ada, fox_b_ada=m_fox_b_ada, fox_w_in=m_fox_w_in,
              fox_b_f=m_fox_b_f, fox_w_out=m_fox_w_out, sb_norm_g=m_sb_norm_g, sb_w_ada=m_sb_w_ada,
              sb_b_ada=m_sb_b_ada, sb_w_in=m_sb_w_in, sb_w_out=m_sb_w_out, final_norm_g=m_final_norm_g)
    vs = dict(fox_norm_g=v_fox_norm_g, fox_w_ada=v_fox_w_ada, fox_b_ada=v_fox_b_ada, fox_w_in=v_fox_w_in,
              fox_b_f=v_fox_b_f, fox_w_out=v_fox_w_out, sb_norm_g=v_sb_norm_g, sb_w_ada=v_sb_w_ada,
              sb_b_ada=v_sb_b_ada, sb_w_in=v_sb_w_in, sb_w_out=v_sb_w_out, final_norm_g=v_final_norm_g)
    order = ["fox_norm_g", "fox_w_ada", "fox_b_ada", "fox_w_in", "fox_b_f", "fox_w_out", "sb_norm_g", "sb_w_ada",
             "sb_b_ada", "sb_w_in", "sb_w_out", "final_norm_g"]
    grads, deltas, new_m, new_v = {}, {}, {}, {}
    for n in big_names:
        shp = weights[n].shape
        g2 = grad_big[n]
        d, m2, v2 = _adamw(weights[n][0], g2, ms[n][0], vs[n][0], n + "_adamw")
        grads[n], deltas[n], new_m[n], new_v[n] = g2.reshape(shp), d.reshape(shp), m2.reshape(shp), v2.reshape(shp)
    small_names = [n for n in order if n not in big_names]
    sizes = [small_grads[n].shape[1] for n in small_names]
    total = sum(sizes)
    padn = (-total) % LANES

    def pack(d):
        return jnp.concatenate([d[n].reshape(1, -1) for n in small_names] + [jnp.ones((1, padn), F32)], axis=1)

    sd, sm, sv_ = _adamw(pack(weights), pack(small_grads), pack(ms), pack(vs), "small_adamw")
    o = 0
    for n, sz in zip(small_names, sizes):
        shp = weights[n].shape
        grads[n] = small_grads[n].reshape(shp)
        deltas[n], new_m[n], new_v[n] = (t[:, o:o + sz].reshape(shp) for t in (sd, sm, sv_))
        o += sz
    return (r_loss[0, 0], grad_x, *[grads[n] for n in order], *[deltas[n] for n in order],
            *[new_m[n] for n in order], *[new_v[n] for n in order])
```

```python
import jax
import jax.numpy as jnp
from jax import lax
from jax.experimental import pallas as pl
from jax.experimental.pallas import tpu as pltpu

F32 = jnp.float32
BF16 = jnp.bfloat16
HEAD_DIM = 64
LOG2E = 1.4426950408889634
LN2 = 0.6931471805599453
Q_SCALE = HEAD_DIM ** -0.5 * LOG2E
LANES = 128
NORM_EPS = 1e-6
ADAM_LR = 0.001
ADAM_B1 = 0.9
ADAM_B2 = 0.999
ADAM_EPS = 1e-08
ADAM_WD = 0.01
ADAM_STEP = 10
VMEM_LIMIT = 56 * 1024 * 1024
SB_BLOCK = 256
FOX_BLOCK = 512
MESH = pl.DeviceIdType.MESH
HBM = pl.BlockSpec(memory_space=pltpu.HBM)
NT = (((1,), (1,)), ((), ()))
TN = (((0,), (0,)), ((), ()))


def _call(body, **kw):
    return pl.pallas_call(body, **kw)


def _params(**kw):
    return pltpu.CompilerParams(vmem_limit_bytes=VMEM_LIMIT, **kw)


def _tile(dim, pref, mult=128):
    if dim <= pref:
        return dim
    t = (pref // mult) * mult
    while t >= mult:
        if dim % t == 0:
            return t
        t -= mult
    return dim


def _sigmoid(x):
    return 1.0 / (1.0 + jnp.exp(-x))


def _split3(x):
    hi = x.astype(BF16)
    r = x - hi.astype(F32)
    mid = r.astype(BF16)
    lo = (r - mid.astype(F32)).astype(BF16)
    return hi, mid, lo


def _mm(a, b, mode, out_dtype, name, tm=512, tn=512, tk=512, col_scale=None, rider=None):
    a_slabs = a.shape[0] if a.ndim == 3 else 0
    b_slabs = b.shape[0] if b.ndim == 3 else 0
    if mode == "nn":
        (M, K), (_, N) = a.shape, b.shape
    elif mode == "nt":
        M, K = (a.shape[1], a_slabs * a.shape[2]) if a_slabs else a.shape
        N = b.shape[0]
    else:
        K, M = a.shape
        N = b_slabs * b.shape[2] if b_slabs else b.shape[1]
    tm, tn, tk = _tile(M, tm), _tile(N, tn), _tile(K, tk)
    if a_slabs:
        tk = _tile(a.shape[2], tk)
    if b_slabs:
        tn = _tile(b.shape[2], tn)
    nk = K // tk
    dims = {"nn": (((1,), (0,)), ((), ())), "nt": NT, "tn": TN}[mode]

    r_ins = rider["ins"] if rider else []
    r_outs = rider["outs"] if rider else []
    r_sems = rider["sems"] if rider else []
    nc = 0 if col_scale is None else 1
    ni, no = len(r_ins), len(r_outs)
    grid = (M // tm, N // tn, nk)

    def body(a_ref, b_ref, *rest):
        o_ref = rest[nc + ni]
        acc_ref = rest[nc + ni + 1 + no]
        k = pl.program_id(2)
        if rider:
            start, finish = rider["steps"](rest[nc:nc + ni], rest[nc + ni + 1:nc + ni + 1 + no], *rest[nc + ni + 2 + no:])
            at = [pl.program_id(d) for d in range(3)]
            pl.when(jnp.logical_and(jnp.logical_and(at[0] == 0, at[1] == 0), at[2] == 0))(start)

        @pl.when(k == 0)
        def _():
            acc_ref[...] = jnp.zeros_like(acc_ref)

        acc_ref[...] += lax.dot_general(a_ref[...], b_ref[...], dims, preferred_element_type=F32)

        @pl.when(k == nk - 1)
        def _():
            acc = acc_ref[...]
            if col_scale is not None:
                acc = acc * rest[0][...]
            o_ref[...] = acc.astype(out_dtype)

        if rider:
            pl.when(jnp.logical_and(jnp.logical_and(at[0] == grid[0] - 1, at[1] == grid[1] - 1), at[2] == nk - 1))(finish)

    if a_slabs:
        per = a.shape[2] // tk
        a_spec = pl.BlockSpec((None, tm, tk), lambda i, j, k: (k // per, i, k % per))
    elif mode == "tn":
        a_spec = pl.BlockSpec((tk, tm), lambda i, j, k: (k, i))
    else:
        a_spec = pl.BlockSpec((tm, tk), lambda i, j, k: (i, k))
    if b_slabs:
        per_b = b.shape[2] // tn
        b_spec = pl.BlockSpec((None, tk, tn), lambda i, j, k: (j // per_b, k, j % per_b))
    elif mode == "nt":
        b_spec = pl.BlockSpec((tn, tk), lambda i, j, k: (j, k))
    else:
        b_spec = pl.BlockSpec((tk, tn), lambda i, j, k: (k, j))
    extra_specs = [] if col_scale is None else [pl.BlockSpec((1, tn), lambda i, j, k: (0, j))]
    extra = [] if col_scale is None else [col_scale]
    res = _call(
        body, name=name, grid=grid,
        in_specs=[a_spec, b_spec] + extra_specs + [HBM] * ni,
        out_specs=[pl.BlockSpec((tm, tn), lambda i, j, k: (i, j))] + [HBM] * no,
        out_shape=[jax.ShapeDtypeStruct((M, N), out_dtype)] + list(r_outs),
        scratch_shapes=[pltpu.VMEM((tm, tn), F32)] + list(r_sems), compiler_params=_params(),
    )(a, b, *extra, *r_ins)
    return (res[0], res[1:]) if rider else res[0]


def _mod_fwd(c8, w_ada, b_ada, name):
    D, N = w_ada.shape
    tn = _tile(N, 512)

    def body(c_ref, w_ref, b_ref, o_ref):
        c = c_ref[...]
        sc = (c * _sigmoid(c)).astype(BF16)
        o_ref[...] = jnp.dot(sc, w_ref[...], preferred_element_type=F32) + b_ref[...]

    return _call(
        body, name=name, grid=(N // tn,),
        in_specs=[pl.BlockSpec((8, D), lambda j: (0, 0)), pl.BlockSpec((D, tn), lambda j: (0, j)),
                  pl.BlockSpec((1, tn), lambda j: (0, j))],
        out_specs=pl.BlockSpec((8, tn), lambda j: (0, j)),
        out_shape=jax.ShapeDtypeStruct((8, N), F32), compiler_params=_params(),
    )(c8, w_ada, b_ada)


def _mod_bwd(cT, dmod8, nb, name):
    D = cT.shape[0]
    N = dmod8.shape[1]
    tn = _tile(N, 512)

    def body(c_ref, d_ref, w_ref, b_ref):
        c = c_ref[...]
        sc = c * _sigmoid(c)
        d = d_ref[...]
        acc = sc[:, 0:1] * d[0:1, :]
        bsum = d[0:1, :]
        for b in range(1, nb):
            acc = acc + sc[:, b:b + 1] * d[b:b + 1, :]
            bsum = bsum + d[b:b + 1, :]
        w_ref[...] = acc
        b_ref[...] = bsum

    return _call(
        body, name=name, grid=(N // tn,),
        in_specs=[pl.BlockSpec((D, 8), lambda j: (0, 0)), pl.BlockSpec((8, tn), lambda j: (0, j))],
        out_specs=[pl.BlockSpec((D, tn), lambda j: (0, j)), pl.BlockSpec((1, tn), lambda j: (0, j))],
        out_shape=[jax.ShapeDtypeStruct((D, N), F32), jax.ShapeDtypeStruct((1, N), F32)],
        compiler_params=_params(),
    )(cT, dmod8)


def _ln_proj(x, shift, scale, g, w, S, name, gather=()):
    T, D = x.shape
    N = w.shape[1]
    tm = _tile(S, 2048)
    tn = _tile(N, 1024)
    per_b = S // tm
    ng = len(gather)
    n0, n1 = T // tm, N // tn

    def body(x_ref, sh_ref, sc_ref, g_ref, w_ref, *rest):
        ins_h, (p_ref, h_ref), outs_h, sems = rest[:ng], rest[ng:ng + 2], rest[ng + 2:2 * ng + 2], rest[2 * ng + 2:]
        i, j = pl.program_id(0), pl.program_id(1)
        if ng:
            start, finish = _gather_steps(ins_h, outs_h, *sems)
            pl.when(jnp.logical_and(i == 0, j == 0))(start)

        @pl.when(j == 0)
        def _():
            xv = x_ref[...]
            r = lax.rsqrt(jnp.mean(xv * xv, axis=-1, keepdims=True) + NORM_EPS)
            h = (xv * r) * g_ref[...] * (1.0 + sc_ref[0]) + sh_ref[0]
            h_ref[...] = h.astype(BF16)

        p_ref[...] = jnp.dot(h_ref[...], w_ref[...], preferred_element_type=F32).astype(BF16)
        if ng:
            pl.when(jnp.logical_and(i == n0 - 1, j == n1 - 1))(finish)

    res = _call(
        body, name=name, grid=(n0, n1),
        in_specs=[pl.BlockSpec((tm, D), lambda i, j: (i, 0)),
                  pl.BlockSpec((1, 1, D), lambda i, j: (i // per_b, 0, 0)),
                  pl.BlockSpec((1, 1, D), lambda i, j: (i // per_b, 0, 0)),
                  pl.BlockSpec((1, D), lambda i, j: (0, 0)),
                  pl.BlockSpec((D, tn), lambda i, j: (0, j))] + [HBM] * ng,
        out_specs=[pl.BlockSpec((tm, tn), lambda i, j: (i, j)), pl.BlockSpec((tm, D), lambda i, j: (i, 0))] + [HBM] * ng,
        out_shape=[jax.ShapeDtypeStruct((T, N), BF16), jax.ShapeDtypeStruct((T, D), BF16)]
        + [jax.ShapeDtypeStruct((4,) + a.shape, a.dtype) for a in gather],
        scratch_shapes=_gather_sems(ng) if ng else [], compiler_params=_params(),
    )(x, shift, scale, g, w, *gather)
    return res[0], res[1], res[2:]


def _ln_bwd(dhs, x, dxo, scale, g, S, name):
    T, D = x.shape
    B = T // S
    tm = _tile(S, 512)
    per_b = S // tm

    nd = len(dhs)

    def body(*refs):
        x_ref, dxo_ref, sc_ref, g_ref, dx_ref, dsh_ref, dsc_ref, dg_ref = refs[nd:]
        i = pl.program_id(0)
        xv = x_ref[...]
        dh_v = refs[0][...]
        for r in refs[1:nd]:
            dh_v = dh_v + r[...]
        r = lax.rsqrt(jnp.mean(xv * xv, axis=-1, keepdims=True) + NORM_EPS)
        xn = xv * r
        gv = g_ref[...]
        one_sc = 1.0 + sc_ref[0]
        dhxn = dh_v * xn

        @pl.when(i % per_b == 0)
        def _():
            dsh_ref[...] = jnp.zeros_like(dsh_ref)
            dsc_ref[...] = jnp.zeros_like(dsc_ref)

        @pl.when(i == 0)
        def _():
            dg_ref[...] = jnp.zeros_like(dg_ref)

        dsh_ref[0] += jnp.sum(dh_v, axis=0, keepdims=True)
        dsc_ref[0] += jnp.sum(dhxn, axis=0, keepdims=True) * gv
        dg_ref[...] += jnp.sum(dhxn, axis=0, keepdims=True) * one_sc
        dxn = dh_v * (gv * one_sc)
        dx_ref[...] = r * (dxn - xn * jnp.mean(dxn * xn, axis=-1, keepdims=True)) + dxo_ref[...]

    row = pl.BlockSpec((tm, D), lambda i: (i, 0))
    per = pl.BlockSpec((1, 1, D), lambda i: (i // per_b, 0, 0))
    vec = pl.BlockSpec((1, D), lambda i: (0, 0))
    return _call(
        body, name=name, grid=(T // tm,),
        in_specs=[row] * (nd + 2) + [per, vec], out_specs=[row, per, per, vec],
        out_shape=[jax.ShapeDtypeStruct((T, D), F32), jax.ShapeDtypeStruct((B, 1, D), F32),
                   jax.ShapeDtypeStruct((B, 1, D), F32), jax.ShapeDtypeStruct((1, D), F32)],
        compiler_params=_params(),
    )(*dhs, x, dxo, scale, g)


def _gate_out(o, proj, w_out, x, gate, S, name):
    T, DI = o.shape
    D = w_out.shape[1]
    tm = _tile(S, 256)
    per_b = S // tm

    def body(o_ref, z_ref, w_ref, x_ref, g_ref, xo_ref, y_ref, u_ref):
        z = z_ref[...].astype(F32)
        u = (o_ref[...] * (z * _sigmoid(z))).astype(BF16)
        u_ref[...] = u
        y = jnp.dot(u, w_ref[...], preferred_element_type=F32)
        y_ref[...] = y
        xo_ref[...] = x_ref[...] + g_ref[0] * y

    wide = pl.BlockSpec((tm, DI), lambda i: (i, 0))
    row = pl.BlockSpec((tm, D), lambda i: (i, 0))
    return _call(
        body, name=name, grid=(T // tm,),
        in_specs=[wide, pl.BlockSpec((tm, DI), lambda i: (i, 3)), pl.BlockSpec((DI, D), lambda i: (0, 0)), row,
                  pl.BlockSpec((1, 1, D), lambda i: (i // per_b, 0, 0))],
        out_specs=[row, row, wide],
        out_shape=[jax.ShapeDtypeStruct((T, D), F32), jax.ShapeDtypeStruct((T, D), F32),
                   jax.ShapeDtypeStruct((T, DI), BF16)],
        compiler_params=_params(),
    )(o, proj, w_out, x, gate)


def _out_bwd(dxo, y, gate, w_out, o, proj, S, name):
    T, D = dxo.shape
    DI = o.shape[1]
    B = T // S
    tm = _tile(S, 256)
    per_b = S // tm

    def body(dxo_ref, y_ref, g_ref, w_ref, o_ref, z_ref, dy_ref, do_ref, dz_ref, dg_ref):
        dxo_v = dxo_ref[...]
        dy = (dxo_v * g_ref[0]).astype(BF16)
        dy_ref[...] = dy
        du = lax.dot_general(dy, w_ref[...], NT, preferred_element_type=F32)
        z = z_ref[...].astype(F32)
        sg = _sigmoid(z)
        do_ref[...] = (du * (z * sg)).astype(BF16)
        dz_ref[...] = (du * o_ref[...] * (sg * (1.0 + z * (1.0 - sg)))).astype(BF16)

        @pl.when(pl.program_id(0) % per_b == 0)
        def _():
            dg_ref[...] = jnp.zeros_like(dg_ref)

        dg_ref[0] += jnp.sum(dxo_v * y_ref[...], axis=0, keepdims=True)

    wide = pl.BlockSpec((tm, DI), lambda i: (i, 0))
    row = pl.BlockSpec((tm, D), lambda i: (i, 0))
    per = pl.BlockSpec((1, 1, D), lambda i: (i // per_b, 0, 0))
    return _call(
        body, name=name, grid=(T // tm,),
        in_specs=[row, row, per, pl.BlockSpec((DI, D), lambda i: (0, 0)), wide,
                  pl.BlockSpec((tm, DI), lambda i: (i, 3))],
        out_specs=[row, wide, wide, per],
        out_shape=[jax.ShapeDtypeStruct((T, D), BF16), jax.ShapeDtypeStruct((T, DI), BF16),
                   jax.ShapeDtypeStruct((T, DI), BF16), jax.ShapeDtypeStruct((B, 1, D), F32)],
        compiler_params=_params(),
    )(dxo, y, gate, w_out, o, proj)


def _final_loss(x, tgt, g, S, name):
    T, D = x.shape
    tm = _tile(S, 512)

    def body(x_ref, t_ref, g_ref, dx_ref, dg_ref, l_ref):
        @pl.when(pl.program_id(0) == 0)
        def _():
            dg_ref[...] = jnp.zeros_like(dg_ref)
            l_ref[...] = jnp.zeros_like(l_ref)

        xv = x_ref[...]
        gv = g_ref[...]
        r = lax.rsqrt(jnp.mean(xv * xv, axis=-1, keepdims=True) + NORM_EPS)
        xn = xv * r
        e = xn * gv - t_ref[...]
        part = jnp.sum(jnp.sum(e * e, axis=0, keepdims=True), axis=1, keepdims=True)
        l_ref[...] += (0.5 / D) * part
        dy = e * (1.0 / D)
        dg_ref[...] += jnp.sum(dy * xn, axis=0, keepdims=True)
        dxn = dy * gv
        dx_ref[...] = r * (dxn - xn * jnp.mean(dxn * xn, axis=-1, keepdims=True))

    row = pl.BlockSpec((tm, D), lambda i: (i, 0))
    return _call(
        body, name=name, grid=(T // tm,),
        in_specs=[row, row, pl.BlockSpec((1, D), lambda i: (0, 0))],
        out_specs=[row, pl.BlockSpec((1, D), lambda i: (0, 0)), pl.BlockSpec((1, LANES), lambda i: (0, 0))],
        out_shape=[jax.ShapeDtypeStruct((T, D), F32), jax.ShapeDtypeStruct((1, D), F32),
                   jax.ShapeDtypeStruct((1, LANES), F32)],
        compiler_params=_params(),
    )(x, tgt, g)


def _cum_fwd(fl, bf, name):
    B, S, _ = fl.shape
    ch = _tile(S, 256, 8)

    def body(fl_ref, b_ref, cum_ref):
        ri = lax.broadcasted_iota(jnp.int32, (ch, ch), 0)
        ci = lax.broadcasted_iota(jnp.int32, (ch, ch), 1)
        tri = jnp.where(ri >= ci, 1.0, 0.0).astype(BF16)

        def step(i, carry):
            r0 = pl.multiple_of(i * ch, ch)
            z = fl_ref[0, pl.ds(r0, ch), :] + b_ref[...]
            lf = (jnp.minimum(z, 0.0) - jnp.log(1.0 + jnp.exp(-jnp.abs(z)))) * LOG2E
            hi, mid, lo = _split3(lf)
            cs = (jnp.dot(tri, hi, preferred_element_type=F32) + jnp.dot(tri, mid, preferred_element_type=F32)
                  + jnp.dot(tri, lo, preferred_element_type=F32)) + carry
            cum_ref[0, pl.ds(r0, ch), :] = cs
            return cs[ch - 1:ch, :]

        lax.fori_loop(0, S // ch, step, jnp.zeros((1, LANES), F32))

    blk = pl.BlockSpec((1, S, LANES), lambda b: (b, 0, 0))
    return _call(
        body, name=name, grid=(B,), in_specs=[blk, pl.BlockSpec((1, LANES), lambda b: (0, 0))], out_specs=blk,
        out_shape=jax.ShapeDtypeStruct((B, S, LANES), F32), compiler_params=_params(),
    )(fl, bf)


def _cum_bwd(dcs, fl, bf, name):
    B, S, _ = fl.shape
    ch = _tile(S, 256, 8)
    n = S // ch

    def body(d_ref, fl_ref, b_ref, o_ref, db_ref):
        ri = lax.broadcasted_iota(jnp.int32, (ch, ch), 0)
        ci = lax.broadcasted_iota(jnp.int32, (ch, ch), 1)
        tri = jnp.where(ci >= ri, 1.0, 0.0).astype(BF16)

        @pl.when(pl.program_id(0) == 0)
        def _():
            db_ref[...] = jnp.zeros_like(db_ref)

        def step(t, carry):
            tail, dbsum = carry
            r0 = pl.multiple_of((n - 1 - t) * ch, ch)
            hi, mid, lo = _split3(d_ref[0, pl.ds(r0, ch), :])
            suf = (jnp.dot(tri, hi, preferred_element_type=F32) + jnp.dot(tri, mid, preferred_element_type=F32)
                   + jnp.dot(tri, lo, preferred_element_type=F32)) + tail
            z = fl_ref[0, pl.ds(r0, ch), :] + b_ref[...]
            dfl = -suf * _sigmoid(-z)
            o_ref[0, pl.ds(r0, ch), :] = dfl
            return suf[0:1, :], dbsum + jnp.sum(dfl, axis=0, keepdims=True)

        z1 = jnp.zeros((1, LANES), F32)
        _, dbsum = lax.fori_loop(0, n, step, (z1, z1))
        db_ref[...] += dbsum

    blk = pl.BlockSpec((1, S, LANES), lambda b: (b, 0, 0))
    vec = pl.BlockSpec((1, LANES), lambda b: (0, 0))
    return _call(
        body, name=name, grid=(B,), in_specs=[blk, blk, vec], out_specs=[blk, vec],
        out_shape=[jax.ShapeDtypeStruct((B, S, LANES), F32), jax.ShapeDtypeStruct((1, LANES), F32)],
        compiler_params=_params(),
    )(dcs, fl, bf)


HEADS_PER_STEP = 4
GROUP = 2 * HEAD_DIM


def _step_width():
    return HEAD_DIM * HEADS_PER_STEP


def _cols(S, offset_blocks=0):
    return pl.BlockSpec((S, _step_width()), lambda b, h: (b, offset_blocks + h))


def _row_spec(nq, tq):
    return pl.BlockSpec((1, HEADS_PER_STEP, nq, 1, tq), lambda b, h: (b, h, 0, 0, 0))


def _lanes(g):
    return slice(GROUP * (g // 2), GROUP * (g // 2) + GROUP)


def _hi_lo(x):
    hi = x.astype(BF16)
    return hi, (x - hi.astype(F32)).astype(BF16)


def _dot(a, b, dims=None):
    if dims is None:
        return jnp.dot(a, b, preferred_element_type=F32)
    return lax.dot_general(a, b, dims, preferred_element_type=F32)


def _causal_blocks(nq, prep, init, stages, finish, combine=None, descending=False, last=None):
    heads = range(HEADS_PER_STEP)

    def qloop(qi, _):
        ctx = [prep(g, qi) for g in heads]

        def step(kj, carry, masked):
            st = list(carry)
            for n, stage in enumerate(stages):
                if combine is not None and n == len(stages) - 1:
                    combine(kj, ctx, st)
                st = [stage(g, ctx[g], kj, masked, st[g]) for g in heads]
            return tuple(st)

        carry = tuple(init() for _ in heads)
        if descending:
            carry = step(qi, carry, True)
            carry = lax.fori_loop(0, qi, lambda t, cr: step(qi - 1 - t, cr, False), carry)
        else:
            carry = lax.fori_loop(0, qi, lambda kj, cr: step(kj, cr, False), carry)
            if last is not None:
                last(qi, ctx, carry)
                return 0
            carry = step(qi, carry, True)
        finish(qi, ctx, carry)
        return 0

    lax.fori_loop(0, nq, qloop, 0)


class _Block:
    def __init__(self, tq):
        self.tq = tq
        self.lane = lax.broadcasted_iota(jnp.int32, (tq, GROUP), 1)
        self.low = self.lane < HEAD_DIM
        self.ri = lax.broadcasted_iota(jnp.int32, (tq, tq), 0)
        self.ci = lax.broadcasted_iota(jnp.int32, (tq, tq), 1)

    def rows(self, i):
        return pl.ds(pl.multiple_of(i * self.tq, self.tq), self.tq)

    def own(self, g, x):
        low = self.low[:x.shape[0]]
        return jnp.where(low if g % 2 == 0 else jnp.logical_not(low), x, jnp.zeros_like(x))

    def pair(self, a, b):
        return jnp.where(self.low[:a.shape[0]], a, b)

    def halves(self):
        r = self.tq // 2
        out = []
        for rr in range(2):
            nc = r * (rr + 1)
            out.append((rr * r, r, nc, lax.broadcasted_iota(jnp.int32, (r, nc), 0) + rr * r,
                        lax.broadcasted_iota(jnp.int32, (r, nc), 1)))
        return out

    def stat(self, g, x):
        return jnp.sum(jnp.where(self.lane == HEAD_DIM * (g % 2), x, 0.0), axis=1, keepdims=True)


def _fox_fwd(proj, cumcol, cumrow, name):
    T, DI = proj.shape[0], proj.shape[1] // 4
    B, H, nq, _, tq = cumrow.shape
    S = nq * tq
    nb = DI // _step_width()

    def body(q_ref, k_ref, v_ref, cc_ref, cr_ref, o_ref, st_ref, acc_scr):
        h0 = pl.program_id(1) * HEADS_PER_STEP
        blk = _Block(tq)

        def prep(g, qi):
            acc_scr[g] = jnp.zeros((tq, GROUP), F32)
            q = blk.own(g, q_ref[blk.rows(qi), _lanes(g)])
            ccol = jnp.sum(jnp.where(blk.lane == h0 + g, cc_ref[0, blk.rows(qi), :], 0.0), axis=1, keepdims=True)
            return q, ccol

        def init():
            return jnp.full((tq, 1), -jnp.inf, F32), jnp.zeros((tq, 1), F32)

        def scores(g, ctx, kj, masked, st):
            return st + (_dot(ctx[0], k_ref[blk.rows(kj), _lanes(g)], NT),)

        def softmax(g, ctx, kj, masked, st):
            m, l, s = st
            s = s + ctx[1] - cr_ref[0, g, kj]
            if masked:
                s = jnp.where(blk.ci <= blk.ri, s, -jnp.inf)
            m_new = jnp.maximum(m, jnp.max(s, axis=1, keepdims=True))
            alpha = jnp.exp2(m - m_new)
            p = jnp.exp2(s - m_new)
            return (m_new, alpha * l + jnp.sum(p, axis=1, keepdims=True), alpha) + _hi_lo(p)

        def values(g, ctx, kj, masked, st):
            m, l, alpha, hi, lo = st
            v = v_ref[blk.rows(kj), _lanes(g)]
            acc_scr[g] = alpha * acc_scr[g] + (_dot(hi, v) + _dot(lo, v))
            return m, l

        def finish(qi, ctx, carry):
            for g in range(0, HEADS_PER_STEP, 2):
                (m0, l0), (m1, l1) = carry[g], carry[g + 1]
                o_ref[blk.rows(qi), _lanes(g)] = blk.pair(acc_scr[g] / l0, acc_scr[g + 1] / l1)
                st_ref[blk.rows(qi), _lanes(g)] = blk.pair(m0 + jnp.log2(l0), m1 + jnp.log2(l1))

        def last(qi, ctx, carry):
            k0 = pl.multiple_of(qi * tq, tq)
            pieces = [(g, h) for g in range(HEADS_PER_STEP) for h in blk.halves()]
            s_all = [_dot(ctx[g][0][r0:r0 + r], k_ref[pl.ds(k0, nc), _lanes(g)], NT) for g, (r0, r, nc, _, _) in pieces]
            soft = []
            for (g, (r0, r, nc, ri, ci)), s in zip(pieces, s_all):
                m, l = carry[g][0][r0:r0 + r], carry[g][1][r0:r0 + r]
                s = s + ctx[g][1][r0:r0 + r] - cr_ref[0, g, qi][:, :nc]
                s = jnp.where(ci <= ri, s, -jnp.inf)
                m_new = jnp.maximum(m, jnp.max(s, axis=1, keepdims=True))
                alpha = jnp.exp2(m - m_new)
                p = jnp.exp2(s - m_new)
                soft.append((m_new, alpha * l + jnp.sum(p, axis=1, keepdims=True), alpha) + _hi_lo(p))
            outs = {}
            for (g, (r0, r, nc, _, _)), (m, l, alpha, hi, lo) in zip(pieces, soft):
                v = v_ref[pl.ds(k0, nc), _lanes(g)]
                acc = alpha * acc_scr[g, pl.ds(r0, r)] + (_dot(hi, v) + _dot(lo, v))
                outs[g, r0] = (acc / l, m + jnp.log2(l))
            for g in range(0, HEADS_PER_STEP, 2):
                for r0, r, _, _, _ in blk.halves():
                    rows = pl.ds(pl.multiple_of(qi * tq + r0, r), r)
                    o_ref[rows, _lanes(g)] = blk.pair(outs[g, r0][0], outs[g + 1, r0][0])
                    st_ref[rows, _lanes(g)] = blk.pair(outs[g, r0][1], outs[g + 1, r0][1])

        _causal_blocks(nq, prep, init, [scores, softmax, values], finish, last=last)

    out = jax.ShapeDtypeStruct((T, DI), F32)
    return _call(
        body, name=name, grid=(B, H // HEADS_PER_STEP),
        in_specs=[_cols(S), _cols(S, nb), _cols(S, 2 * nb), pl.BlockSpec((1, S, LANES), lambda b, h: (b, 0, 0)),
                  _row_spec(nq, tq)],
        out_specs=[_cols(S), _cols(S)], out_shape=[out, out],
        scratch_shapes=[pltpu.VMEM((HEADS_PER_STEP, tq, GROUP), F32)], compiler_params=_params(),
    )(proj, proj, proj, cumcol, cumrow)


def _fox_bwd(proj, do, dzg, o, stat, cumcol, cumrow, name):
    T, DI = do.shape
    B, H, nq, _, tq = cumrow.shape
    S = nq * tq
    nb = DI // _step_width()

    def body(q_ref, k_ref, v_ref, do_ref, dz_ref, o_ref, st_ref, cc_ref, cr_ref, dqkv_ref, dcs_ref, dk_acc, dv_acc,
             dq_scr):
        h0 = pl.program_id(1) * HEADS_PER_STEP
        blk = _Block(tq)
        dk_acc[...] = jnp.zeros_like(dk_acc)
        dv_acc[...] = jnp.zeros_like(dv_acc)
        dcs_ref[...] = jnp.zeros_like(dcs_ref)

        def prep(g, qi):
            dq_scr[g] = jnp.zeros((tq, GROUP), F32)
            q = blk.own(g, q_ref[blk.rows(qi), _lanes(g)])
            dout = blk.own(g, do_ref[blk.rows(qi), _lanes(g)])
            delta = jnp.sum(o_ref[blk.rows(qi), _lanes(g)] * dout.astype(F32), axis=1, keepdims=True)
            lse = blk.stat(g, st_ref[blk.rows(qi), _lanes(g)])
            ccol = jnp.sum(jnp.where(blk.lane == h0 + g, cc_ref[0, blk.rows(qi), :], 0.0), axis=1, keepdims=True)
            return q, dout, lse, delta, ccol

        def init():
            return ()

        def scores(g, ctx, kj, masked, st):
            return (_dot(ctx[0], k_ref[blk.rows(kj), _lanes(g)], NT), _dot(ctx[1], v_ref[blk.rows(kj), _lanes(g)], NT))

        def softmax_bwd(g, ctx, kj, masked, st):
            s, dp = st
            _, _, lse, delta, ccol = ctx
            s = s + ccol - cr_ref[0, g, kj]
            if masked:
                s = jnp.where(blk.ci <= blk.ri, s, -jnp.inf)
            p = jnp.exp2(s - lse)
            ds = p * (dp - delta)
            return p.astype(BF16), ds.astype(BF16), jnp.sum(ds, axis=0, keepdims=True)

        def combine(kj, ctx, st):
            for g in range(0, HEADS_PER_STEP, 2):
                dv_acc[blk.rows(kj), _lanes(g)] += _dot(st[g][0], ctx[g][1], TN) + _dot(st[g + 1][0], ctx[g + 1][1], TN)
                dk_acc[blk.rows(kj), _lanes(g)] += _dot(st[g][1], ctx[g][0], TN) + _dot(st[g + 1][1], ctx[g + 1][0], TN)
            for g in range(HEADS_PER_STEP):
                dcs_ref[0, g, kj] += st[g][2]

        def queries(g, ctx, kj, masked, st):
            dq_scr[g] += _dot(st[1], blk.own(g, k_ref[blk.rows(kj), _lanes(g)]))
            return ()

        def finish(qi, ctx, carry):
            for g in range(0, HEADS_PER_STEP, 2):
                dqkv_ref[0, blk.rows(qi), _lanes(g)] = ((dq_scr[g] + dq_scr[g + 1]) * LN2).astype(BF16)

        def last(qi, ctx, carry):
            k0 = pl.multiple_of(qi * tq, tq)
            pieces = [(g, h) for g in range(HEADS_PER_STEP) for h in blk.halves()]
            mm = [(_dot(ctx[g][0][r0:r0 + r], k_ref[pl.ds(k0, nc), _lanes(g)], NT),
                   _dot(ctx[g][1][r0:r0 + r], v_ref[pl.ds(k0, nc), _lanes(g)], NT)) for g, (r0, r, nc, _, _) in pieces]
            soft = {}
            for (g, (r0, r, nc, ri, ci)), (s, dp) in zip(pieces, mm):
                _, _, lse, delta, ccol = ctx[g]
                s = s + ccol[r0:r0 + r] - cr_ref[0, g, qi][:, :nc]
                s = jnp.where(ci <= ri, s, -jnp.inf)
                p = jnp.exp2(s - lse[r0:r0 + r])
                ds = p * (dp - delta[r0:r0 + r])
                soft[g, r0] = (p.astype(BF16), ds.astype(BF16), jnp.sum(ds, axis=0, keepdims=True))
            for r0, r, nc, _, _ in blk.halves():
                for g in range(0, HEADS_PER_STEP, 2):
                    (p0, d0, _), (p1, d1, _) = soft[g, r0], soft[g + 1, r0]
                    q0, q1 = ctx[g][0][r0:r0 + r], ctx[g + 1][0][r0:r0 + r]
                    o0, o1 = ctx[g][1][r0:r0 + r], ctx[g + 1][1][r0:r0 + r]
                    dv_acc[pl.ds(k0, nc), _lanes(g)] += _dot(p0, o0, TN) + _dot(p1, o1, TN)
                    dk_acc[pl.ds(k0, nc), _lanes(g)] += _dot(d0, q0, TN) + _dot(d1, q1, TN)
                for g in range(HEADS_PER_STEP):
                    col = soft[g, r0][2]
                    if nc < tq:
                        col = jnp.concatenate([col, jnp.zeros((1, tq - nc), F32)], axis=1)
                    dcs_ref[0, g, qi] += col
            dq = {}
            for g, (r0, r, nc, _, _) in pieces:
                dq[g, r0] = dq_scr[g, pl.ds(r0, r)] + _dot(soft[g, r0][1], blk.own(g, k_ref[pl.ds(k0, nc), _lanes(g)]))
            for g in range(0, HEADS_PER_STEP, 2):
                for r0, r, _, _, _ in blk.halves():
                    rows = pl.ds(pl.multiple_of(qi * tq + r0, r), r)
                    dqkv_ref[0, rows, _lanes(g)] = ((dq[g, r0] + dq[g + 1, r0]) * LN2).astype(BF16)

        _causal_blocks(nq, prep, init, [scores, softmax_bwd, queries], finish, combine=combine, last=last)
        dqkv_ref[1] = (dk_acc[...] * LN2).astype(BF16)
        dqkv_ref[2] = dv_acc[...].astype(BF16)
        dqkv_ref[3] = dz_ref[...]

    W = _step_width()
    return _call(
        body, name=name, grid=(B, H // HEADS_PER_STEP),
        in_specs=[_cols(S), _cols(S, nb), _cols(S, 2 * nb), _cols(S), _cols(S), _cols(S), _cols(S),
                  pl.BlockSpec((1, S, LANES), lambda b, h: (b, 0, 0)), _row_spec(nq, tq)],
        out_specs=[pl.BlockSpec((4, S, W), lambda b, h: (0, b, h)), _row_spec(nq, tq)],
        out_shape=[jax.ShapeDtypeStruct((4, T, DI), BF16), jax.ShapeDtypeStruct((B, H, nq, 1, tq), F32)],
        scratch_shapes=[pltpu.VMEM((S, W), F32), pltpu.VMEM((S, W), F32), pltpu.VMEM((HEADS_PER_STEP, tq, GROUP), F32)],
        compiler_params=_params(),
    )(proj, proj, proj, do, dzg, o, stat, cumcol, cumrow)


def _log2_keep(z2):
    nz = -z2
    e = jnp.exp2(jnp.minimum(z2, nz))
    return jnp.minimum(nz, 0.0) - jnp.log2(1.0 + e), e


def _sb_fwd(proj, B, tq, name):
    T, DI = proj.shape[0], proj.shape[1] // 4
    S = T // B
    H = DI // HEAD_DIM
    nq = S // tq
    nb = DI // _step_width()

    def body(q_ref, k_ref, v_ref, o_ref, st_ref, acc_scr, c_scr):
        blk = _Block(tq)
        strict = blk.ci < blk.ri
        above = jnp.where(blk.ri > blk.ci, 1.0, 0.0).astype(BF16)

        def prep(g, qi):
            acc_scr[g] = jnp.zeros((tq, GROUP), F32)
            c_scr[g] = jnp.zeros((tq, 1), F32)
            return blk.own(g, q_ref[blk.rows(qi), _lanes(g)])

        def init():
            return ()

        def scores(g, q, kj, masked, st):
            return (_dot(q, k_ref[blk.rows(kj), _lanes(g)], NT),)

        def logs(g, q, kj, masked, st):
            (z,) = st
            lk, _ = _log2_keep(z)
            lb = z + lk
            if masked:
                lk = jnp.where(strict, lk, 0.0)
            c = c_scr[g]
            c_scr[g] = c + jnp.sum(lk, axis=1, keepdims=True)
            return (lb + c,) + _hi_lo(lk)

        def suffix(g, q, kj, masked, st):
            lbc, hi, lo = st
            return lbc, _dot(hi, above) + _dot(lo, above)

        def weights(g, q, kj, masked, st):
            lbc, after = st
            a = jnp.exp2(lbc + after)
            if masked:
                a = jnp.where(strict, a, 0.0)
            return (a.astype(BF16),)

        def values(g, q, kj, masked, st):
            acc_scr[g] += _dot(st[0], v_ref[blk.rows(kj), _lanes(g)])
            return ()

        def finish(qi, ctx, carry):
            for g in range(0, HEADS_PER_STEP, 2):
                o_ref[blk.rows(qi), _lanes(g)] = blk.pair(acc_scr[g], acc_scr[g + 1])
                st_ref[blk.rows(qi), _lanes(g)] = blk.pair(c_scr[g], c_scr[g + 1])

        _causal_blocks(nq, prep, init, [scores, logs, suffix, weights, values], finish, descending=True)

    out = jax.ShapeDtypeStruct((T, DI), F32)
    return _call(
        body, name=name, grid=(B, H // HEADS_PER_STEP), in_specs=[_cols(S), _cols(S, nb), _cols(S, 2 * nb)],
        out_specs=[_cols(S), _cols(S)], out_shape=[out, out],
        scratch_shapes=[pltpu.VMEM((HEADS_PER_STEP, tq, GROUP), F32), pltpu.VMEM((HEADS_PER_STEP, tq, 1), F32)],
        compiler_params=_params(),
    )(proj, proj, proj)


def _sb_bwd(proj, do, dzg, stat, B, tq, name):
    T, DI = do.shape
    S = T // B
    H = DI // HEAD_DIM
    nq = S // tq
    nb = DI // _step_width()

    def body(q_ref, k_ref, v_ref, do_ref, dz_ref, st_ref, dqkv_ref, dk_acc, dv_acc, dq_scr):
        blk = _Block(tq)
        strict = blk.ci < blk.ri
        upto = jnp.where(blk.ri <= blk.ci, 1.0, 0.0).astype(BF16)
        before = jnp.where(blk.ri < blk.ci, 1.0, 0.0).astype(BF16)
        dk_acc[...] = jnp.zeros_like(dk_acc)
        dv_acc[...] = jnp.zeros_like(dv_acc)

        def prep(g, qi):
            dq_scr[g] = jnp.zeros((tq, GROUP), F32)
            return (blk.own(g, q_ref[blk.rows(qi), _lanes(g)]), blk.own(g, do_ref[blk.rows(qi), _lanes(g)]),
                    blk.stat(g, st_ref[blk.rows(qi), _lanes(g)]))

        def init():
            return jnp.zeros((tq, 1), F32), jnp.zeros((tq, 1), F32)

        def scores(g, ctx, kj, masked, st):
            return st + (_dot(ctx[0], k_ref[blk.rows(kj), _lanes(g)], NT),
                         _dot(ctx[1], v_ref[blk.rows(kj), _lanes(g)], NT))

        def logs(g, ctx, kj, masked, st):
            cpre, pg, z, da = st
            lk, e = _log2_keep(z)
            inv = 1.0 / (1.0 + e)
            sig = jnp.where(z >= 0.0, inv, e * inv)
            lbt = (z + lk) + (ctx[2] - cpre)
            if masked:
                lk = jnp.where(strict, lk, 0.0)
            return (cpre + jnp.sum(lk, axis=1, keepdims=True), pg, da, lbt, sig) + _hi_lo(lk)

        def prefix(g, ctx, kj, masked, st):
            cpre, pg, da, lbt, sig, hi, lo = st
            return cpre, pg, da, lbt, sig, _dot(hi, upto) + _dot(lo, upto)

        def weights(g, ctx, kj, masked, st):
            cpre, pg, da, lbt, sig, pre = st
            a = jnp.exp2(lbt - pre)
            if masked:
                a = jnp.where(strict, a, 0.0)
            gr = da * a
            return cpre, pg, sig, a.astype(BF16), gr, gr.astype(BF16)

        def grad_prefix(g, ctx, kj, masked, st):
            cpre, pg, sig, ab, gr, gb = st
            return cpre, pg, sig, ab, gr, _dot(gb, before)

        def dlogits(g, ctx, kj, masked, st):
            cpre, pg, sig, ab, gr, pfx = st
            dz = gr - sig * (gr + (pfx + pg))
            if masked:
                dz = jnp.where(strict, dz, 0.0)
            return cpre, pg + jnp.sum(gr, axis=1, keepdims=True), ab, dz.astype(BF16)

        def combine(kj, ctx, st):
            for g in range(0, HEADS_PER_STEP, 2):
                dv_acc[blk.rows(kj), _lanes(g)] += _dot(st[g][2], ctx[g][1], TN) + _dot(st[g + 1][2], ctx[g + 1][1], TN)
                dk_acc[blk.rows(kj), _lanes(g)] += _dot(st[g][3], ctx[g][0], TN) + _dot(st[g + 1][3], ctx[g + 1][0], TN)

        def queries(g, ctx, kj, masked, st):
            cpre, pg, _, dzb = st
            dq_scr[g] += _dot(dzb, blk.own(g, k_ref[blk.rows(kj), _lanes(g)]))
            return cpre, pg

        def finish(qi, ctx, carry):
            for g in range(0, HEADS_PER_STEP, 2):
                dqkv_ref[0, blk.rows(qi), _lanes(g)] = ((dq_scr[g] + dq_scr[g + 1]) * LN2).astype(BF16)

        _causal_blocks(nq, prep, init, [scores, logs, prefix, weights, grad_prefix, dlogits, queries], finish,
                       combine=combine)
        dqkv_ref[1] = (dk_acc[...] * LN2).astype(BF16)
        dqkv_ref[2] = dv_acc[...].astype(BF16)
        dqkv_ref[3] = dz_ref[...]

    W = _step_width()
    return _call(
        body, name=name, grid=(B, H // HEADS_PER_STEP),
        in_specs=[_cols(S), _cols(S, nb), _cols(S, 2 * nb), _cols(S), _cols(S), _cols(S)],
        out_specs=pl.BlockSpec((4, S, W), lambda b, h: (0, b, h)),
        out_shape=jax.ShapeDtypeStruct((4, T, DI), BF16),
        scratch_shapes=[pltpu.VMEM((S, W), F32), pltpu.VMEM((S, W), F32), pltpu.VMEM((HEADS_PER_STEP, tq, GROUP), F32)],
        compiler_params=_params(),
    )(proj, proj, proj, do, dzg, stat)


def _row_tile(R, C, n_arrays):
    budget = 24 * 1024 * 1024 // (2 * n_arrays * 4 * max(C, LANES))
    return _tile(R, max(8, budget), 8)


def _ew_sum(parts, name, also_bf16=False):
    R, C = parts[0].shape
    tr = _row_tile(R, C, len(parts) + 2)
    n = len(parts)

    def body(*refs):
        acc = refs[0][...].astype(F32) + refs[1][...].astype(F32)
        for r in refs[2:n]:
            acc = acc + r[...].astype(F32)
        refs[n][...] = acc
        if also_bf16:
            refs[n + 1][...] = acc.astype(BF16)

    blk = pl.BlockSpec((tr, C), lambda i: (i, 0))
    out_shape = [jax.ShapeDtypeStruct((R, C), F32)] + ([jax.ShapeDtypeStruct((R, C), BF16)] if also_bf16 else [])
    return _call(
        body, name=name, grid=(R // tr,), in_specs=[blk] * n, out_specs=[blk] * len(out_shape),
        out_shape=out_shape, compiler_params=_params(),
    )(*parts)


def _adamw(w, g, m, v, name):
    R, C = w.shape
    tr = _row_tile(R, C, 7)
    c1 = 1.0 / (1.0 - ADAM_B1 ** ADAM_STEP)
    c2 = 1.0 / (1.0 - ADAM_B2 ** ADAM_STEP)

    def body(w_ref, g_ref, m_ref, v_ref, d_ref, m2_ref, v2_ref):
        gv = g_ref[...]
        m2 = ADAM_B1 * m_ref[...] + (1.0 - ADAM_B1) * gv
        v2 = ADAM_B2 * v_ref[...] + (1.0 - ADAM_B2) * (gv * gv)
        m2_ref[...] = m2
        v2_ref[...] = v2
        d_ref[...] = -ADAM_LR * ((m2 * c1) / (jnp.sqrt(v2 * c2) + ADAM_EPS) + ADAM_WD * w_ref[...])

    blk = pl.BlockSpec((tr, C), lambda i: (i, 0))
    out = jax.ShapeDtypeStruct((R, C), F32)
    return _call(
        body, name=name, grid=(R // tr,), in_specs=[blk] * 4, out_specs=[blk] * 3, out_shape=[out] * 3,
        compiler_params=_params(),
    )(w, g, m, v)


def _me():
    return lax.axis_index("x"), lax.axis_index("y"), lax.axis_index("c")


def _chip_of(x, y):
    return 2 * x + y


def _other_chips(x, y):
    return [(x, 1 - y), (1 - x, y), (1 - x, 1 - y)]


def _gather_steps(ins_h, outs_h, send1, recv1, send2, recv2):
    nh = len(ins_h)
    x, y, c = _me()
    mine = _chip_of(x, y)
    chips = _other_chips(x, y)
    sib = (x, y, 1 - c)

    def landed(i, k, half):
        return outs_h[i].at[_chip_of(*chips[k]), half]

    def first(i, k):
        return pltpu.make_async_remote_copy(
            src_ref=ins_h[i].at[c], dst_ref=outs_h[i].at[mine, c], send_sem=send1.at[i, k], recv_sem=recv1.at[i, k],
            device_id=(*chips[k], c), device_id_type=MESH)

    def passed(i, k):
        return pltpu.make_async_remote_copy(
            src_ref=landed(i, k, c), dst_ref=landed(i, k, c), send_sem=send2.at[i, k], recv_sem=recv2.at[i, k],
            device_id=sib, device_id_type=MESH)

    def start():
        for i in range(nh):
            for k in range(3):
                first(i, k).start()

    def finish():
        for i in range(nh):
            for k in range(3):
                pltpu.make_async_remote_copy(
                    src_ref=ins_h[i].at[c], dst_ref=landed(i, k, c), send_sem=send1.at[i, k], recv_sem=recv1.at[i, k],
                    device_id=(*chips[k], c), device_id_type=MESH).wait_recv()
                passed(i, k).start()
        for i in range(nh):
            for k in range(3):
                pltpu.make_async_remote_copy(
                    src_ref=landed(i, k, c), dst_ref=landed(i, k, 1 - c), send_sem=send2.at[i, k],
                    recv_sem=recv2.at[i, k], device_id=sib, device_id_type=MESH).wait_recv()
        for i in range(nh):
            for k in range(3):
                first(i, k).wait_send()
                passed(i, k).wait_send()

    return start, finish


def _gather_sems(nh):
    return [pltpu.SemaphoreType.DMA((nh, 3)) for _ in range(4)]


def _gather_weights(halves, smalls):
    nh, ns = len(halves), len(smalls)

    def body(*refs):
        ins_h, ins_s = refs[:nh], refs[nh:nh + ns]
        outs_h, outs_s = refs[nh + ns:2 * nh + ns], refs[2 * nh + ns:2 * (nh + ns)]
        send1, recv1, send2, recv2, send3, recv3 = refs[2 * (nh + ns):]
        x, y, c = _me()
        mine = _chip_of(x, y)
        chips = _other_chips(x, y)

        def small(i, k):
            return pltpu.make_async_remote_copy(
                src_ref=ins_s[i], dst_ref=outs_s[i].at[mine], send_sem=send3.at[i, k], recv_sem=recv3.at[i, k],
                device_id=(*chips[k], c), device_id_type=MESH)

        start, finish = _gather_steps(ins_h, outs_h, send1, recv1, send2, recv2)
        start()
        for i in range(ns):
            for k in range(3):
                small(i, k).start()
        finish()
        for i in range(ns):
            for k in range(3):
                pltpu.make_async_remote_copy(
                    src_ref=ins_s[i], dst_ref=outs_s[i].at[_chip_of(*chips[k])], send_sem=send3.at[i, k],
                    recv_sem=recv3.at[i, k], device_id=(*chips[k], c), device_id_type=MESH).wait_recv()
                small(i, k).wait_send()

    out_shape = ([jax.ShapeDtypeStruct((4,) + a.shape, a.dtype) for a in halves]
                 + [jax.ShapeDtypeStruct((4,) + a.shape, a.dtype) for a in smalls])
    n = nh + ns
    res = _call(
        body, name="gather_weights", in_specs=[HBM] * n, out_specs=[HBM] * n, out_shape=out_shape,
        scratch_shapes=_gather_sems(nh) + [pltpu.SemaphoreType.DMA((max(ns, 1), 3)),
                                           pltpu.SemaphoreType.DMA((max(ns, 1), 3))],
        compiler_params=_params(),
    )(*halves, *smalls)
    return res[:nh], res[nh:]


def _plan(ins, outs, sems, copies):
    def steps(in_refs, out_refs, *sem_refs):
        def start():
            for cp in copies(in_refs, out_refs, *sem_refs):
                cp.start()

        def finish():
            for cp in copies(in_refs, out_refs, *sem_refs):
                cp.wait()

        return start, finish

    return dict(ins=list(ins), outs=list(outs), sems=list(sems), steps=steps)


def _pair_exchange_plan(grads):
    n = len(grads)

    def copies(ins, got, send, recv):
        x, y, c = _me()
        return [pltpu.make_async_remote_copy(
            src_ref=ins[i].at[j, 1 - c], dst_ref=got[i].at[j], send_sem=send.at[i, j], recv_sem=recv.at[i, j],
            device_id=(x, y, 1 - c), device_id_type=MESH) for i in range(n) for j in range(4)]

    return _plan(grads, [jax.ShapeDtypeStruct((4,) + g.shape[2:], g.dtype) for g in grads],
                 [pltpu.SemaphoreType.DMA((n, 4)), pltpu.SemaphoreType.DMA((n, 4))], copies)


def _chip_exchange_plan(sums):
    n = len(sums)

    def copies(ins, got, send, recv):
        x, y, c = _me()
        chips = _other_chips(x, y)
        return [pltpu.make_async_remote_copy(
            src_ref=ins[i].at[_chip_of(*chips[k])], dst_ref=got[i].at[k], send_sem=send.at[i, k],
            recv_sem=recv.at[i, k], device_id=(*chips[k], c), device_id_type=MESH) for i in range(n) for k in range(3)]

    return _plan(sums, [jax.ShapeDtypeStruct((3,) + a.shape[1:], a.dtype) for a in sums],
                 [pltpu.SemaphoreType.DMA((n, 3)), pltpu.SemaphoreType.DMA((n, 3))], copies)


def _pair_share_plan(halves):
    n = len(halves)

    def copies(ins, outs, send, recv):
        x, y, c = _me()
        return [pltpu.make_async_remote_copy(
            src_ref=ins[i], dst_ref=outs[i], send_sem=send.at[i], recv_sem=recv.at[i],
            device_id=(x, y, 1 - c), device_id_type=MESH) for i in range(n)]

    return _plan(halves, [jax.ShapeDtypeStruct(h.shape, h.dtype) for h in halves],
                 [pltpu.SemaphoreType.DMA((n,)), pltpu.SemaphoreType.DMA((n,))], copies)


def _run_exchange(plan, name):
    ni, no = len(plan["ins"]), len(plan["outs"])

    def body(*refs):
        start, finish = plan["steps"](refs[:ni], refs[ni:ni + no], *refs[ni + no:])
        start()
        finish()

    return _call(
        body, name=name, in_specs=[HBM] * ni, out_specs=[HBM] * no, out_shape=plan["outs"],
        scratch_shapes=plan["sems"], compiler_params=_params(),
    )(*plan["ins"])


def _allreduce_small(vec):
    P = vec.shape[1]

    def body(v_ref, sum_ref, all_ref, send, recv):
        x, y, c = _me()
        me = 4 * x + 2 * y + c
        all_ref[pl.ds(me, 1)] = v_ref[...][None]
        cps = []
        for d in range(1, 8):
            peer = (jnp.bitwise_xor(x, d >> 2), jnp.bitwise_xor(y, (d >> 1) & 1), jnp.bitwise_xor(c, d & 1))
            r = pltpu.make_async_remote_copy(
                src_ref=v_ref, dst_ref=all_ref.at[me], send_sem=send.at[d - 1], recv_sem=recv.at[d - 1],
                device_id=peer, device_id_type=MESH)
            r.start()
            cps.append(r)
        for d in range(1, 8):
            src = jnp.bitwise_xor(me, d)
            pltpu.make_async_remote_copy(
                src_ref=v_ref, dst_ref=all_ref.at[src], send_sem=send.at[d - 1], recv_sem=recv.at[d - 1],
                device_id=(x, y, c), device_id_type=MESH).wait_recv()
        for r in cps:
            r.wait_send()
        acc = all_ref[0]
        for i in range(1, 8):
            acc = acc + all_ref[i]
        sum_ref[...] = acc

    vm = pl.BlockSpec(memory_space=pltpu.VMEM)
    return _call(
        body, name="allreduce_small", in_specs=[vm], out_specs=[vm, vm],
        out_shape=[jax.ShapeDtypeStruct((8, P), F32), jax.ShapeDtypeStruct((8, 8, P), F32)],
        scratch_shapes=[pltpu.SemaphoreType.DMA((7,)), pltpu.SemaphoreType.DMA((7,))],
        compiler_params=_params(),
    )(vec)[0]


def _per_batch(mod, B, D):
    return [mod[:B, i * D:(i + 1) * D].reshape(B, 1, D) for i in range(3)]


def _pad_rows8(a):
    return jnp.concatenate([a, jnp.zeros((8 - a.shape[0],) + a.shape[1:], a.dtype)], axis=0)


def _layer_fwd(x, c8, w, S, fox, tag, gather=()):
    T, D = x.shape
    B = T // S
    DI = w["w_out"].shape[0]
    H = DI // HEAD_DIM
    tq = _tile(S, FOX_BLOCK if fox else SB_BLOCK, 8)
    mod = _mod_fwd(c8, w["w_ada"], w["b_ada"], tag + "_mod_fwd")
    shift, scale, gate = _per_batch(mod, B, D)
    proj, h, gathered = _ln_proj(x, shift, scale, w["norm_g"], w["w_in"], S, tag + "_ln_proj", gather)
    saved = dict(x=x, h=h, proj=proj, scale=scale, gate=gate, gathered=gathered)
    if fox:
        fl = _mm(h, w["w_f"], "nn", F32, tag + "_flogit").reshape(B, S, LANES)
        cum = _cum_fwd(fl, w["b_f"], tag + "_cum_fwd")
        cumrow = cum[:, :, :H].transpose(0, 2, 1).reshape(B, H, S // tq, 1, tq)
        o, stat = _fox_fwd(proj, cum, cumrow, tag + "_attn_fwd")
        saved.update(fl=fl, cum=cum, cumrow=cumrow)
    else:
        o, stat = _sb_fwd(proj, B, tq, tag + "_attn_fwd")
    xo, y, u = _gate_out(o, proj, w["w_out"], x, gate, S, tag + "_gate_out")
    saved.update(o=o, stat=stat, y=y, u=u)
    return xo, saved


def _hosted(side, sent, call):
    if side is None:
        return call(None), None
    plan, _ = next(side) if sent is None else side.send(sent)
    return call(plan)


def _layer_bwd(dxo, sv, w, cT, S, fox, tag, side=None):
    T, D = dxo.shape
    B = T // S
    DI = w["w_out"].shape[0]
    H = DI // HEAD_DIM
    tq = _tile(S, FOX_BLOCK if fox else SB_BLOCK, 8)
    dy, do, dzg, dgate = _out_bwd(dxo, sv["y"], sv["gate"], w["w_out"], sv["o"], sv["proj"], S, tag + "_out_bwd")
    dw_out, landed = _hosted(side, None, lambda r: _mm(sv["u"], dy, "tn", F32, tag + "_dw_out", tm=1024, tn=1024,
                                                      tk=2048, rider=r))
    g = {"w_out": dw_out}
    q_cols = jnp.where(jnp.arange(4 * DI)[None, :] < DI, Q_SCALE, 1.0).astype(F32)
    if fox:
        dproj, dcs = _fox_bwd(sv["proj"], do, dzg, sv["o"], sv["stat"], sv["cum"], sv["cumrow"], tag + "_attn_bwd")
        dcs = dcs.reshape(B, H, S).transpose(0, 2, 1)
        dcs = jnp.concatenate([dcs, jnp.zeros((B, S, LANES - H), F32)], axis=-1)
        dfl, db_f = _cum_bwd(dcs, sv["fl"], w["b_f"], tag + "_cum_bwd")
        g["b_f"] = db_f[:, :H]
        dfl = dfl.reshape(T, LANES).astype(BF16)
    else:
        dproj = _sb_bwd(sv["proj"], do, dzg, sv["stat"], B, tq, tag + "_attn_bwd")
    g["w_in"], landed = _hosted(side, landed, lambda r: _mm(sv["h"], dproj, "tn", F32, tag + "_dw_in", tm=1024, tn=2048,
                                                            tk=1024, col_scale=q_cols, rider=r))
    dh, landed = _hosted(side, landed, lambda r: _mm(dproj, w["w_in"], "nt", F32, tag + "_dh", tm=2048, tn=1024,
                                                     tk=1024, rider=r))
    dhs = [dh]
    if side is not None:
        try:
            side.send(landed)
        except StopIteration as done:
            g["side"] = done.value
    if fox:
        dw_f = _mm(sv["h"], dfl, "tn", F32, tag + "_dw_f", tm=1024, tn=LANES, tk=2048)
        g["w_in"] = jnp.concatenate([g["w_in"], dw_f[:, :H]], axis=1)
        dhs.append(_mm(dfl, w["w_f"], "nt", F32, tag + "_dh_f", tm=2048, tn=1024, tk=LANES))
    dx, dshift, dscale, dg = _ln_bwd(dhs, sv["x"], dxo, sv["scale"], w["norm_g"], S, tag + "_ln_bwd")
    g["norm_g"] = dg
    dmod = jnp.concatenate([dshift, dscale, dgate], axis=-1).reshape(B, 3 * D)
    g["w_ada"], g["b_ada"] = _mod_bwd(cT, _pad_rows8(dmod), B, tag + "_mod_bwd")
    return dx, g


def _local_step(x3, c, tgt3, wf, ws, final_g, sb_halves=(), sb_side=None):
    B, S, D = x3.shape
    T = B * S
    x = x3.reshape(T, D)
    c8 = _pad_rows8(c)
    cT = c8.T
    x1, sv1 = _layer_fwd(x, c8, wf, S, True, "fox", sb_halves)
    if sb_halves:
        ws = ws(sv1["gathered"])
    x2, sv2 = _layer_fwd(x1, c8, ws, S, False, "sb")
    dx2, dgf, loss = _final_loss(x2, tgt3.reshape(T, D), final_g, S, "final_loss")
    dx1, gs = _layer_bwd(dx2, sv2, ws, cT, S, False, "sb")
    dx0, gf = _layer_bwd(dx1, sv1, wf, cT, S, True, "fox", None if sb_side is None else sb_side(gs))
    return loss, dx0.reshape(B, S, D), gf, gs, dgf


def _cols_to_shards(a):
    R, C4 = a.shape
    return a.reshape(R, 4, C4 // 4).transpose(1, 0, 2)


def _shards_to_cols(a):
    n, R, C = a.shape
    return a.transpose(1, 0, 2).reshape(R, n * C)


def kernel(x, c, fox_norm_g, fox_w_ada, fox_b_ada, fox_w_in, fox_b_f, fox_w_out, sb_norm_g, sb_w_ada, sb_b_ada, sb_w_in, sb_w_out, final_norm_g, loss_target, m_fox_norm_g, m_fox_w_ada, m_fox_b_ada, m_fox_w_in, m_fox_b_f, m_fox_w_out, m_sb_norm_g, m_sb_w_ada, m_sb_b_ada, m_sb_w_in, m_sb_w_out, m_final_norm_g, v_fox_norm_g, v_fox_w_ada, v_fox_b_ada, v_fox_w_in, v_fox_b_f, v_fox_w_out, v_sb_norm_g, v_sb_w_ada, v_sb_b_ada, v_sb_w_in, v_sb_w_out, v_final_norm_g):
    B, S, D = x.shape
    DI = 4 * fox_w_out.shape[1]
    H = DI // HEAD_DIM
    chip = _chip_of(lax.axis_index("x"), lax.axis_index("y"))

    big_names = ["fox_w_ada", "fox_w_in", "fox_w_out", "sb_w_ada", "sb_w_in", "sb_w_out"]
    big = dict(fox_w_ada=fox_w_ada[0], fox_w_in=fox_w_in[0], fox_w_out=fox_w_out[0],
               sb_w_ada=sb_w_ada[0], sb_w_in=sb_w_in[0], sb_w_out=sb_w_out[0])
    for n in ("fox_w_in", "sb_w_in"):
        width = big[n].shape[1]
        is_q = chip * width + jnp.arange(width)[None, :] < DI
        big[n] = big[n] * jnp.where(is_q, Q_SCALE, 1.0).astype(F32)
    halves = {n: big[n].astype(BF16).reshape(2, big[n].shape[0] // 2, big[n].shape[1]) for n in big_names}
    fox_names, sb_names = big_names[:3], big_names[3:]

    def assemble(names, gathered):
        full = {}
        for n, a in zip(names, gathered):
            a = lax.dynamic_update_index_in_dim(a, halves[n], chip, 0)
            a = a.reshape(4, a.shape[1] * a.shape[2], a.shape[3])
            full[n] = a.reshape(4 * a.shape[1], a.shape[2]) if n.endswith("w_out") else _shards_to_cols(a)
        return full

    gathered, gsmall = _gather_weights([halves[n] for n in fox_names], [sb_norm_g, sb_b_ada])
    gsmall = [lax.dynamic_update_index_in_dim(a, own, chip, 0) for a, own in zip(gsmall, [sb_norm_g, sb_b_ada])]
    full = assemble(fox_names, gathered)
    sb_norm_full = gsmall[0].reshape(1, D)
    sb_b_ada_full = gsmall[1].reshape(1, 3 * D)
    w_f = jnp.concatenate([full["fox_w_in"][:, 4 * DI:], jnp.zeros((D, LANES - H), BF16)], axis=1)
    b_f = jnp.concatenate([fox_b_f, jnp.zeros((1, LANES - H), F32)], axis=1)
    wf = dict(w_ada=full["fox_w_ada"], b_ada=fox_b_ada, norm_g=fox_norm_g, w_in=full["fox_w_in"][:, :4 * DI],
              w_f=w_f, b_f=b_f, w_out=full["fox_w_out"])

    def ws(gathered_sb):
        f = assemble(sb_names, gathered_sb)
        return dict(w_ada=f["sb_w_ada"], b_ada=sb_b_ada_full, norm_g=sb_norm_full, w_in=f["sb_w_in"], w_out=f["sb_w_out"])

    core = lax.axis_index("c")

    def reduction(names, part, tag):
        shard_major = []
        for n in names:
            a = part[n]
            a = a.reshape(4, a.shape[0] // 4, a.shape[1]) if n.endswith("w_out") else _cols_to_shards(a)
            shard_major.append(a.reshape(4, 2, a.shape[1] // 2, a.shape[2]))
        got = yield _pair_exchange_plan(shard_major), tag + "_grad_pair_exchange"
        pair_f32, pair_bf16 = [], []
        for n, g4, b in zip(names, shard_major, got):
            a = lax.dynamic_index_in_dim(g4, core, axis=1, keepdims=False)
            r, C = a.shape[1:]
            s32, s16 = _ew_sum([a.reshape(4 * r, C), b.reshape(4 * r, C)], n + "_pair_sum", also_bf16=True)
            pair_f32.append(s32.reshape(4, r, C))
            pair_bf16.append(s16.reshape(4, r, C))
        others = yield _chip_exchange_plan(pair_bf16), tag + "_grad_chip_exchange"
        reduced_halves = [_ew_sum([lax.dynamic_index_in_dim(a, chip, axis=0, keepdims=False), b[0], b[1], b[2]],
                                  n + "_chip_sum")[0] for n, a, b in zip(names, pair_f32, others)]
        theirs = yield _pair_share_plan(reduced_halves), tag + "_grad_pair_share"
        return {n: jnp.concatenate([jnp.where(core == 0, a, b), jnp.where(core == 0, b, a)], axis=0)
                for n, a, b in zip(names, reduced_halves, theirs)}

    def sb_side(gs):
        return reduction(sb_names, dict(sb_w_ada=gs["w_ada"], sb_w_in=gs["w_in"], sb_w_out=gs["w_out"]), "sb")

    loss, grad_x, gf, gs, dgf = _local_step(x, c, loss_target, wf, ws, final_norm_g.reshape(1, D),
                                            [halves[n] for n in sb_names], sb_side)
    grad_big = dict(gf["side"])
    fox_red = reduction(fox_names, dict(fox_w_ada=gf["w_ada"], fox_w_in=gf["w_in"], fox_w_out=gf["w_out"]), "fox")
    try:
        plan, name = next(fox_red)
        while True:
            plan, name = fox_red.send(_run_exchange(plan, name))
    except StopIteration as done:
        grad_big.update(done.value)

    pieces = [loss, gf["norm_g"], gf["b_ada"], jnp.concatenate([gf["b_f"], jnp.zeros((1, LANES - H), F32)], axis=1),
              gs["norm_g"], gs["b_ada"], dgf]
    vec = jnp.concatenate(pieces, axis=1)
    red = _allreduce_small(_pad_rows8(vec))[0:1]
    offs = [0]
    for p in pieces:
        offs.append(offs[-1] + p.shape[1])
    r_loss, r_fng, r_fba, r_fbf, r_sng, r_sba, r_fin = [red[:, offs[i]:offs[i + 1]] for i in range(7)]
    small_grads = dict(
        fox_norm_g=r_fng, fox_b_ada=r_fba, fox_b_f=r_fbf[:, :H],
        sb_norm_g=lax.dynamic_slice_in_dim(r_sng, chip * (D // 4), D // 4, axis=1),
        sb_b_ada=lax.dynamic_slice_in_dim(r_sba, chip * (3 * D // 4), 3 * D // 4, axis=1),
        final_norm_g=r_fin)

    weights = dict(fox_norm_g=fox_norm_g, fox_w_ada=fox_w_ada, fox_b_ada=fox_b_ada, fox_w_in=fox_w_in, fox_b_f=fox_b_f,
                   fox_w_out=fox_w_out, sb_norm_g=sb_norm_g, sb_w_ada=sb_w_ada, sb_b_ada=sb_b_ada, sb_w_in=sb_w_in,
                   sb_w_out=sb_w_out, final_norm_g=final_norm_g)
    ms = dict(fox_norm_g=m_fox_norm_g, fox_w_ada=m_fox_w_ada, fox_b_ada=m_fox_b_ada, fox_w_in=m_fox_w_in,
              fox_b_f=m_fox_b_f, fox_w_out=m_fox_w_out, sb_norm_g=m_sb_norm_g, sb_w_ada=m_sb_w_ada,
              sb_b_ada=m_sb_b_ada, sb_w_in=m_sb_w_in, sb_w_out=m_sb_w_out, final_norm_g=m_final_norm_g)
    vs = dict(fox_norm_g=v_fox_norm_g, fox_w_ada=v_fox_w_ada, fox_b_ada=v_fox_b_ada, fox_w_in=v_fox_w_in,
              fox_b_f=v_fox_b_f, fox_w_out=v_fox_w_out, sb_norm_g=v_sb_norm_g, sb_w_ada=v_sb_w_ada,
              sb_b_ada=v_sb_b_ada, sb_w_in=v_sb_w_in, sb_w_out=v_sb_w_out, final_norm_g=v_final_norm_g)
    order = ["fox_norm_g", "fox_w_ada", "fox_b_ada", "fox_w_in", "fox_b_f", "fox_w_out", "sb_norm_g", "sb_w_ada",
             "sb_b_ada", "sb_w_in", "sb_w_out", "final_norm_g"]
    grads, deltas, new_m, new_v = {}, {}, {}, {}
    for n in big_names:
        shp = weights[n].shape
        g2 = grad_big[n]
        d, m2, v2 = _adamw(weights[n][0], g2, ms[n][0], vs[n][0], n + "_adamw")
        grads[n], deltas[n], new_m[n], new_v[n] = g2.reshape(shp), d.reshape(shp), m2.reshape(shp), v2.reshape(shp)
    small_names = [n for n in order if n not in big_names]
    sizes = [small_grads[n].shape[1] for n in small_names]
    total = sum(sizes)
    padn = (-total) % LANES

    def pack(d):
        return jnp.concatenate([d[n].reshape(1, -1) for n in small_names] + [jnp.ones((1, padn), F32)], axis=1)

    sd, sm, sv_ = _adamw(pack(weights), pack(small_grads), pack(ms), pack(vs), "small_adamw")
    o = 0
    for n, sz in zip(small_names, sizes):
        shp = weights[n].shape
        grads[n] = small_grads[n].reshape(shp)
        deltas[n], new_m[n], new_v[n] = (t[:, o:o + sz].reshape(shp) for t in (sd, sm, sv_))
        o += sz
    return (r_loss[0, 0], grad_x, *[grads[n] for n in order], *[deltas[n] for n in order],
            *[new_m[n] for n in order], *[new_v[n] for n in order])
```

```python
import jax
import jax.numpy as jnp
from jax import lax
from jax.experimental import pallas as pl
from jax.experimental.pallas import tpu as pltpu

F32 = jnp.float32
BF16 = jnp.bfloat16
HEAD_DIM = 64
LOG2E = 1.4426950408889634
LN2 = 0.6931471805599453
Q_SCALE = HEAD_DIM ** -0.5 * LOG2E
LANES = 128
NORM_EPS = 1e-6
ADAM_LR = 0.001
ADAM_B1 = 0.9
ADAM_B2 = 0.999
ADAM_EPS = 1e-08
ADAM_WD = 0.01
ADAM_STEP = 10
VMEM_LIMIT = 56 * 1024 * 1024
SB_BLOCK = 256
FOX_BLOCK = 512
MESH = pl.DeviceIdType.MESH
HBM = pl.BlockSpec(memory_space=pltpu.HBM)
NT = (((1,), (1,)), ((), ()))
TN = (((0,), (0,)), ((), ()))


def _call(body, **kw):
    return pl.pallas_call(body, **kw)


def _params(**kw):
    return pltpu.CompilerParams(vmem_limit_bytes=VMEM_LIMIT, **kw)


def _tile(dim, pref, mult=128):
    if dim <= pref:
        return dim
    t = (pref // mult) * mult
    while t >= mult:
        if dim % t == 0:
            return t
        t -= mult
    return dim


def _sigmoid(x):
    return 1.0 / (1.0 + jnp.exp(-x))


def _split3(x):
    hi = x.astype(BF16)
    r = x - hi.astype(F32)
    mid = r.astype(BF16)
    lo = (r - mid.astype(F32)).astype(BF16)
    return hi, mid, lo


def _mm(a, b, mode, out_dtype, name, tm=512, tn=512, tk=512, col_scale=None, rider=None):
    a_slabs = a.shape[0] if a.ndim == 3 else 0
    b_slabs = b.shape[0] if b.ndim == 3 else 0
    if mode == "nn":
        (M, K), (_, N) = a.shape, b.shape
    elif mode == "nt":
        M, K = (a.shape[1], a_slabs * a.shape[2]) if a_slabs else a.shape
        N = b.shape[0]
    else:
        K, M = a.shape
        N = b_slabs * b.shape[2] if b_slabs else b.shape[1]
    tm, tn, tk = _tile(M, tm), _tile(N, tn), _tile(K, tk)
    if a_slabs:
        tk = _tile(a.shape[2], tk)
    if b_slabs:
        tn = _tile(b.shape[2], tn)
    nk = K // tk
    dims = {"nn": (((1,), (0,)), ((), ())), "nt": NT, "tn": TN}[mode]

    r_ins = rider["ins"] if rider else []
    r_outs = rider["outs"] if rider else []
    r_sems = rider["sems"] if rider else []
    nc = 0 if col_scale is None else 1
    ni, no = len(r_ins), len(r_outs)
    grid = (M // tm, N // tn, nk)

    def body(a_ref, b_ref, *rest):
        o_ref = rest[nc + ni]
        acc_ref = rest[nc + ni + 1 + no]
        k = pl.program_id(2)
        if rider:
            start, finish = rider["steps"](rest[nc:nc + ni], rest[nc + ni + 1:nc + ni + 1 + no], *rest[nc + ni + 2 + no:])
            at = [pl.program_id(d) for d in range(3)]
            pl.when(jnp.logical_and(jnp.logical_and(at[0] == 0, at[1] == 0), at[2] == 0))(start)

        @pl.when(k == 0)
        def _():
            acc_ref[...] = jnp.zeros_like(acc_ref)

        acc_ref[...] += lax.dot_general(a_ref[...], b_ref[...], dims, preferred_element_type=F32)

        @pl.when(k == nk - 1)
        def _():
            acc = acc_ref[...]
            if col_scale is not None:
                acc = acc * rest[0][...]
            o_ref[...] = acc.astype(out_dtype)

        if rider:
            pl.when(jnp.logical_and(jnp.logical_and(at[0] == grid[0] - 1, at[1] == grid[1] - 1), at[2] == nk - 1))(finish)

    if a_slabs:
        per = a.shape[2] // tk
        a_spec = pl.BlockSpec((None, tm, tk), lambda i, j, k: (k // per, i, k % per))
    elif mode == "tn":
        a_spec = pl.BlockSpec((tk, tm), lambda i, j, k: (k, i))
    else:
        a_spec = pl.BlockSpec((tm, tk), lambda i, j, k: (i, k))
    if b_slabs:
        per_b = b.shape[2] // tn
        b_spec = pl.BlockSpec((None, tk, tn), lambda i, j, k: (j // per_b, k, j % per_b))
    elif mode == "nt":
        b_spec = pl.BlockSpec((tn, tk), lambda i, j, k: (j, k))
    else:
        b_spec = pl.BlockSpec((tk, tn), lambda i, j, k: (k, j))
    extra_specs = [] if col_scale is None else [pl.BlockSpec((1, tn), lambda i, j, k: (0, j))]
    extra = [] if col_scale is None else [col_scale]
    res = _call(
        body, name=name, grid=grid,
        in_specs=[a_spec, b_spec] + extra_specs + [HBM] * ni,
        out_specs=[pl.BlockSpec((tm, tn), lambda i, j, k: (i, j))] + [HBM] * no,
        out_shape=[jax.ShapeDtypeStruct((M, N), out_dtype)] + list(r_outs),
        scratch_shapes=[pltpu.VMEM((tm, tn), F32)] + list(r_sems), compiler_params=_params(),
    )(a, b, *extra, *r_ins)
    return (res[0], res[1:]) if rider else res[0]


def _mod_fwd(c8, w_ada, b_ada, name):
    D, N = w_ada.shape
    tn = _tile(N, 512)

    def body(c_ref, w_ref, b_ref, o_ref):
        c = c_ref[...]
        sc = (c * _sigmoid(c)).astype(BF16)
        o_ref[...] = jnp.dot(sc, w_ref[...], preferred_element_type=F32) + b_ref[...]

    return _call(
        body, name=name, grid=(N // tn,),
        in_specs=[pl.BlockSpec((8, D), lambda j: (0, 0)), pl.BlockSpec((D, tn), lambda j: (0, j)),
                  pl.BlockSpec((1, tn), lambda j: (0, j))],
        out_specs=pl.BlockSpec((8, tn), lambda j: (0, j)),
        out_shape=jax.ShapeDtypeStruct((8, N), F32), compiler_params=_params(),
    )(c8, w_ada, b_ada)


def _mod_bwd(cT, dmod8, nb, name):
    D = cT.shape[0]
    N = dmod8.shape[1]
    tn = _tile(N, 512)

    def body(c_ref, d_ref, w_ref, b_ref):
        c = c_ref[...]
        sc = c * _sigmoid(c)
        d = d_ref[...]
        acc = sc[:, 0:1] * d[0:1, :]
        bsum = d[0:1, :]
        for b in range(1, nb):
            acc = acc + sc[:, b:b + 1] * d[b:b + 1, :]
            bsum = bsum + d[b:b + 1, :]
        w_ref[...] = acc
        b_ref[...] = bsum

    return _call(
        body, name=name, grid=(N // tn,),
        in_specs=[pl.BlockSpec((D, 8), lambda j: (0, 0)), pl.BlockSpec((8, tn), lambda j: (0, j))],
        out_specs=[pl.BlockSpec((D, tn), lambda j: (0, j)), pl.BlockSpec((1, tn), lambda j: (0, j))],
        out_shape=[jax.ShapeDtypeStruct((D, N), F32), jax.ShapeDtypeStruct((1, N), F32)],
        compiler_params=_params(),
    )(cT, dmod8)


def _ln_proj(x, shift, scale, g, w, S, name, gather=()):
    T, D = x.shape
    N = w.shape[1]
    tm = _tile(S, 2048)
    tn = _tile(N, 1024)
    per_b = S // tm
    ng = len(gather)
    n0, n1 = T // tm, N // tn

    def body(x_ref, sh_ref, sc_ref, g_ref, w_ref, *rest):
        ins_h, (p_ref, h_ref), outs_h, sems = rest[:ng], rest[ng:ng + 2], rest[ng + 2:2 * ng + 2], rest[2 * ng + 2:]
        i, j = pl.program_id(0), pl.program_id(1)
        if ng:
            start, finish = _gather_steps(ins_h, outs_h, *sems)
            pl.when(jnp.logical_and(i == 0, j == 0))(start)

        @pl.when(j == 0)
        def _():
            xv = x_ref[...]
            r = lax.rsqrt(jnp.mean(xv * xv, axis=-1, keepdims=True) + NORM_EPS)
            h = (xv * r) * g_ref[...] * (1.0 + sc_ref[0]) + sh_ref[0]
            h_ref[...] = h.astype(BF16)

        p_ref[...] = jnp.dot(h_ref[...], w_ref[...], preferred_element_type=F32).astype(BF16)
        if ng:
            pl.when(jnp.logical_and(i == n0 - 1, j == n1 - 1))(finish)

    res = _call(
        body, name=name, grid=(n0, n1),
        in_specs=[pl.BlockSpec((tm, D), lambda i, j: (i, 0)),
                  pl.BlockSpec((1, 1, D), lambda i, j: (i // per_b, 0, 0)),
                  pl.BlockSpec((1, 1, D), lambda i, j: (i // per_b, 0, 0)),
                  pl.BlockSpec((1, D), lambda i, j: (0, 0)),
                  pl.BlockSpec((D, tn), lambda i, j: (0, j))] + [HBM] * ng,
        out_specs=[pl.BlockSpec((tm, tn), lambda i, j: (i, j)), pl.BlockSpec((tm, D), lambda i, j: (i, 0))] + [HBM] * ng,
        out_shape=[jax.ShapeDtypeStruct((T, N), BF16), jax.ShapeDtypeStruct((T, D), BF16)]
        + [jax.ShapeDtypeStruct((4,) + a.shape, a.dtype) for a in gather],
        scratch_shapes=_gather_sems(ng) if ng else [], compiler_params=_params(),
    )(x, shift, scale, g, w, *gather)
    return res[0], res[1], res[2:]


def _ln_bwd(dhs, x, dxo, scale, g, S, name):
    T, D = x.shape
    B = T // S
    tm = _tile(S, 512)
    per_b = S // tm

    nd = len(dhs)

    def body(*refs):
        x_ref, dxo_ref, sc_ref, g_ref, dx_ref, dsh_ref, dsc_ref, dg_ref = refs[nd:]
        i = pl.program_id(0)
        xv = x_ref[...]
        dh_v = refs[0][...]
        for r in refs[1:nd]:
            dh_v = dh_v + r[...]
        r = lax.rsqrt(jnp.mean(xv * xv, axis=-1, keepdims=True) + NORM_EPS)
        xn = xv * r
        gv = g_ref[...]
        one_sc = 1.0 + sc_ref[0]
        dhxn = dh_v * xn

        @pl.when(i % per_b == 0)
        def _():
            dsh_ref[...] = jnp.zeros_like(dsh_ref)
            dsc_ref[...] = jnp.zeros_like(dsc_ref)

        @pl.when(i == 0)
        def _():
            dg_ref[...] = jnp.zeros_like(dg_ref)

        dsh_ref[0] += jnp.sum(dh_v, axis=0, keepdims=True)
        dsc_ref[0] += jnp.sum(dhxn, axis=0, keepdims=True) * gv
        dg_ref[...] += jnp.sum(dhxn, axis=0, keepdims=True) * one_sc
        dxn = dh_v * (gv * one_sc)
        dx_ref[...] = r * (dxn - xn * jnp.mean(dxn * xn, axis=-1, keepdims=True)) + dxo_ref[...]

    row = pl.BlockSpec((tm, D), lambda i: (i, 0))
    per = pl.BlockSpec((1, 1, D), lambda i: (i // per_b, 0, 0))
    vec = pl.BlockSpec((1, D), lambda i: (0, 0))
    return _call(
        body, name=name, grid=(T // tm,),
        in_specs=[row] * (nd + 2) + [per, vec], out_specs=[row, per, per, vec],
        out_shape=[jax.ShapeDtypeStruct((T, D), F32), jax.ShapeDtypeStruct((B, 1, D), F32),
                   jax.ShapeDtypeStruct((B, 1, D), F32), jax.ShapeDtypeStruct((1, D), F32)],
        compiler_params=_params(),
    )(*dhs, x, dxo, scale, g)


def _gate_out(o, proj, w_out, x, gate, S, name):
    T, DI = o.shape
    D = w_out.shape[1]
    tm = _tile(S, 256)
    per_b = S // tm

    def body(o_ref, z_ref, w_ref, x_ref, g_ref, xo_ref, y_ref, u_ref):
        z = z_ref[...].astype(F32)
        u = (o_ref[...] * (z * _sigmoid(z))).astype(BF16)
        u_ref[...] = u
        y = jnp.dot(u, w_ref[...], preferred_element_type=F32)
        y_ref[...] = y
        xo_ref[...] = x_ref[...] + g_ref[0] * y

    wide = pl.BlockSpec((tm, DI), lambda i: (i, 0))
    row = pl.BlockSpec((tm, D), lambda i: (i, 0))
    return _call(
        body, name=name, grid=(T // tm,),
        in_specs=[wide, pl.BlockSpec((tm, DI), lambda i: (i, 3)), pl.BlockSpec((DI, D), lambda i: (0, 0)), row,
                  pl.BlockSpec((1, 1, D), lambda i: (i // per_b, 0, 0))],
        out_specs=[row, row, wide],
        out_shape=[jax.ShapeDtypeStruct((T, D), F32), jax.ShapeDtypeStruct((T, D), F32),
                   jax.ShapeDtypeStruct((T, DI), BF16)],
        compiler_params=_params(),
    )(o, proj, w_out, x, gate)


def _out_bwd(dxo, y, gate, w_out, o, proj, S, name, rider=None):
    T, D = dxo.shape
    DI = o.shape[1]
    B = T // S
    tm = _tile(S, 256)
    per_b = S // tm

    r_ins = rider["ins"] if rider else []
    r_outs = rider["outs"] if rider else []
    ni, no = len(r_ins), len(r_outs)
    steps = T // tm

    def body(dxo_ref, y_ref, g_ref, w_ref, o_ref, z_ref, *rest):
        dy_ref, do_ref, dz_ref, dg_ref = rest[ni:ni + 4]
        if rider:
            start, finish = rider["steps"](rest[:ni], rest[ni + 4:ni + 4 + no], *rest[ni + 4 + no:])
            pl.when(pl.program_id(0) == 0)(start)
        dxo_v = dxo_ref[...]
        dy = (dxo_v * g_ref[0]).astype(BF16)
        dy_ref[...] = dy
        du = lax.dot_general(dy, w_ref[...], NT, preferred_element_type=F32)
        z = z_ref[...].astype(F32)
        sg = _sigmoid(z)
        do_ref[...] = (du * (z * sg)).astype(BF16)
        dz_ref[...] = (du * o_ref[...] * (sg * (1.0 + z * (1.0 - sg)))).astype(BF16)

        @pl.when(pl.program_id(0) % per_b == 0)
        def _():
            dg_ref[...] = jnp.zeros_like(dg_ref)

        dg_ref[0] += jnp.sum(dxo_v * y_ref[...], axis=0, keepdims=True)
        if rider:
            pl.when(pl.program_id(0) == steps - 1)(finish)

    wide = pl.BlockSpec((tm, DI), lambda i: (i, 0))
    row = pl.BlockSpec((tm, D), lambda i: (i, 0))
    per = pl.BlockSpec((1, 1, D), lambda i: (i // per_b, 0, 0))
    res = _call(
        body, name=name, grid=(steps,),
        in_specs=[row, row, per, pl.BlockSpec((DI, D), lambda i: (0, 0)), wide,
                  pl.BlockSpec((tm, DI), lambda i: (i, 3))] + [HBM] * ni,
        out_specs=[row, wide, wide, per] + [HBM] * no,
        out_shape=[jax.ShapeDtypeStruct((T, D), BF16), jax.ShapeDtypeStruct((T, DI), BF16),
                   jax.ShapeDtypeStruct((T, DI), BF16), jax.ShapeDtypeStruct((B, 1, D), F32)] + list(r_outs),
        scratch_shapes=list(rider["sems"]) if rider else [], compiler_params=_params(),
    )(dxo, y, gate, w_out, o, proj, *r_ins)
    return (res[:4], res[4:]) if rider else res


def _final_loss(x, tgt, g, S, name):
    T, D = x.shape
    tm = _tile(S, 512)

    def body(x_ref, t_ref, g_ref, dx_ref, dg_ref, l_ref):
        @pl.when(pl.program_id(0) == 0)
        def _():
            dg_ref[...] = jnp.zeros_like(dg_ref)
            l_ref[...] = jnp.zeros_like(l_ref)

        xv = x_ref[...]
        gv = g_ref[...]
        r = lax.rsqrt(jnp.mean(xv * xv, axis=-1, keepdims=True) + NORM_EPS)
        xn = xv * r
        e = xn * gv - t_ref[...]
        part = jnp.sum(jnp.sum(e * e, axis=0, keepdims=True), axis=1, keepdims=True)
        l_ref[...] += (0.5 / D) * part
        dy = e * (1.0 / D)
        dg_ref[...] += jnp.sum(dy * xn, axis=0, keepdims=True)
        dxn = dy * gv
        dx_ref[...] = r * (dxn - xn * jnp.mean(dxn * xn, axis=-1, keepdims=True))

    row = pl.BlockSpec((tm, D), lambda i: (i, 0))
    return _call(
        body, name=name, grid=(T // tm,),
        in_specs=[row, row, pl.BlockSpec((1, D), lambda i: (0, 0))],
        out_specs=[row, pl.BlockSpec((1, D), lambda i: (0, 0)), pl.BlockSpec((1, LANES), lambda i: (0, 0))],
        out_shape=[jax.ShapeDtypeStruct((T, D), F32), jax.ShapeDtypeStruct((1, D), F32),
                   jax.ShapeDtypeStruct((1, LANES), F32)],
        compiler_params=_params(),
    )(x, tgt, g)


def _cum_fwd(fl, bf, name):
    B, S, _ = fl.shape
    ch = _tile(S, 256, 8)

    def body(fl_ref, b_ref, cum_ref):
        ri = lax.broadcasted_iota(jnp.int32, (ch, ch), 0)
        ci = lax.broadcasted_iota(jnp.int32, (ch, ch), 1)
        tri = jnp.where(ri >= ci, 1.0, 0.0).astype(BF16)

        def step(i, carry):
            r0 = pl.multiple_of(i * ch, ch)
            z = fl_ref[0, pl.ds(r0, ch), :] + b_ref[...]
            lf = (jnp.minimum(z, 0.0) - jnp.log(1.0 + jnp.exp(-jnp.abs(z)))) * LOG2E
            hi, mid, lo = _split3(lf)
            cs = (jnp.dot(tri, hi, preferred_element_type=F32) + jnp.dot(tri, mid, preferred_element_type=F32)
                  + jnp.dot(tri, lo, preferred_element_type=F32)) + carry
            cum_ref[0, pl.ds(r0, ch), :] = cs
            return cs[ch - 1:ch, :]

        lax.fori_loop(0, S // ch, step, jnp.zeros((1, LANES), F32))

    blk = pl.BlockSpec((1, S, LANES), lambda b: (b, 0, 0))
    return _call(
        body, name=name, grid=(B,), in_specs=[blk, pl.BlockSpec((1, LANES), lambda b: (0, 0))], out_specs=blk,
        out_shape=jax.ShapeDtypeStruct((B, S, LANES), F32), compiler_params=_params(),
    )(fl, bf)


def _cum_bwd(dcs, fl, bf, name):
    B, S, _ = fl.shape
    ch = _tile(S, 256, 8)
    n = S // ch

    def body(d_ref, fl_ref, b_ref, o_ref, db_ref):
        ri = lax.broadcasted_iota(jnp.int32, (ch, ch), 0)
        ci = lax.broadcasted_iota(jnp.int32, (ch, ch), 1)
        tri = jnp.where(ci >= ri, 1.0, 0.0).astype(BF16)

        @pl.when(pl.program_id(0) == 0)
        def _():
            db_ref[...] = jnp.zeros_like(db_ref)

        def step(t, carry):
            tail, dbsum = carry
            r0 = pl.multiple_of((n - 1 - t) * ch, ch)
            hi, mid, lo = _split3(d_ref[0, pl.ds(r0, ch), :])
            suf = (jnp.dot(tri, hi, preferred_element_type=F32) + jnp.dot(tri, mid, preferred_element_type=F32)
                   + jnp.dot(tri, lo, preferred_element_type=F32)) + tail
            z = fl_ref[0, pl.ds(r0, ch), :] + b_ref[...]
            dfl = -suf * _sigmoid(-z)
            o_ref[0, pl.ds(r0, ch), :] = dfl
            return suf[0:1, :], dbsum + jnp.sum(dfl, axis=0, keepdims=True)

        z1 = jnp.zeros((1, LANES), F32)
        _, dbsum = lax.fori_loop(0, n, step, (z1, z1))
        db_ref[...] += dbsum

    blk = pl.BlockSpec((1, S, LANES), lambda b: (b, 0, 0))
    vec = pl.BlockSpec((1, LANES), lambda b: (0, 0))
    return _call(
        body, name=name, grid=(B,), in_specs=[blk, blk, vec], out_specs=[blk, vec],
        out_shape=[jax.ShapeDtypeStruct((B, S, LANES), F32), jax.ShapeDtypeStruct((1, LANES), F32)],
        compiler_params=_params(),
    )(dcs, fl, bf)


HEADS_PER_STEP = 4
GROUP = 2 * HEAD_DIM
DIAG_PIECES = 4


def _step_width():
    return HEAD_DIM * HEADS_PER_STEP


def _cols(S, offset_blocks=0):
    return pl.BlockSpec((S, _step_width()), lambda b, h: (b, offset_blocks + h))


def _row_spec(nq, tq):
    return pl.BlockSpec((1, HEADS_PER_STEP, nq, 1, tq), lambda b, h: (b, h, 0, 0, 0))


def _lanes(g):
    return slice(GROUP * (g // 2), GROUP * (g // 2) + GROUP)


def _hi_lo(x):
    hi = x.astype(BF16)
    return hi, (x - hi.astype(F32)).astype(BF16)


def _dot(a, b, dims=None):
    if dims is None:
        return jnp.dot(a, b, preferred_element_type=F32)
    return lax.dot_general(a, b, dims, preferred_element_type=F32)


def _causal_blocks(nq, prep, init, stages, finish, combine=None, descending=False, last=None):
    heads = range(HEADS_PER_STEP)

    def qloop(qi, _):
        ctx = [prep(g, qi) for g in heads]

        def step(kj, carry, masked):
            st = list(carry)
            for n, stage in enumerate(stages):
                if combine is not None and n == len(stages) - 1:
                    combine(kj, ctx, st)
                st = [stage(g, ctx[g], kj, masked, st[g]) for g in heads]
            return tuple(st)

        carry = tuple(init() for _ in heads)
        if descending:
            carry = step(qi, carry, True)
            carry = lax.fori_loop(0, qi, lambda t, cr: step(qi - 1 - t, cr, False), carry)
        else:
            carry = lax.fori_loop(0, qi, lambda kj, cr: step(kj, cr, False), carry)
            if last is not None:
                last(qi, ctx, carry)
                return 0
            carry = step(qi, carry, True)
        finish(qi, ctx, carry)
        return 0

    lax.fori_loop(0, nq, qloop, 0)


class _Block:
    def __init__(self, tq):
        self.tq = tq
        self.lane = lax.broadcasted_iota(jnp.int32, (tq, GROUP), 1)
        self.low = self.lane < HEAD_DIM
        self.ri = lax.broadcasted_iota(jnp.int32, (tq, tq), 0)
        self.ci = lax.broadcasted_iota(jnp.int32, (tq, tq), 1)

    def rows(self, i):
        return pl.ds(pl.multiple_of(i * self.tq, self.tq), self.tq)

    def own(self, g, x):
        low = self.low[:x.shape[0]]
        return jnp.where(low if g % 2 == 0 else jnp.logical_not(low), x, jnp.zeros_like(x))

    def pair(self, a, b):
        return jnp.where(self.low[:a.shape[0]], a, b)

    def halves(self):
        r = self.tq // DIAG_PIECES
        out = []
        for rr in range(DIAG_PIECES):
            nc = r * (rr + 1)
            out.append((rr * r, r, nc, lax.broadcasted_iota(jnp.int32, (r, nc), 0) + rr * r,
                        lax.broadcasted_iota(jnp.int32, (r, nc), 1)))
        return out

    def stat(self, g, x):
        return jnp.sum(jnp.where(self.lane == HEAD_DIM * (g % 2), x, 0.0), axis=1, keepdims=True)


def _fox_fwd(proj, cumcol, cumrow, name):
    T, DI = proj.shape[0], proj.shape[1] // 4
    B, H, nq, _, tq = cumrow.shape
    S = nq * tq
    nb = DI // _step_width()

    def body(q_ref, k_ref, v_ref, cc_ref, cr_ref, o_ref, st_ref, acc_scr):
        h0 = pl.program_id(1) * HEADS_PER_STEP
        blk = _Block(tq)

        def prep(g, qi):
            acc_scr[g] = jnp.zeros((tq, GROUP), F32)
            q = blk.own(g, q_ref[blk.rows(qi), _lanes(g)])
            ccol = jnp.sum(jnp.where(blk.lane == h0 + g, cc_ref[0, blk.rows(qi), :], 0.0), axis=1, keepdims=True)
            return q, ccol

        def init():
            return jnp.full((tq, 1), -jnp.inf, F32), jnp.zeros((tq, 1), F32)

        def scores(g, ctx, kj, masked, st):
            return st + (_dot(ctx[0], k_ref[blk.rows(kj), _lanes(g)], NT),)

        def softmax(g, ctx, kj, masked, st):
            m, l, s = st
            s = s + ctx[1] - cr_ref[0, g, kj]
            if masked:
                s = jnp.where(blk.ci <= blk.ri, s, -jnp.inf)
            m_new = jnp.maximum(m, jnp.max(s, axis=1, keepdims=True))
            alpha = jnp.exp2(m - m_new)
            p = jnp.exp2(s - m_new)
            return (m_new, alpha * l + jnp.sum(p, axis=1, keepdims=True), alpha) + _hi_lo(p)

        def values(g, ctx, kj, masked, st):
            m, l, alpha, hi, lo = st
            v = v_ref[blk.rows(kj), _lanes(g)]
            acc_scr[g] = alpha * acc_scr[g] + (_dot(hi, v) + _dot(lo, v))
            return m, l

        def finish(qi, ctx, carry):
            for g in range(0, HEADS_PER_STEP, 2):
                (m0, l0), (m1, l1) = carry[g], carry[g + 1]
                o_ref[blk.rows(qi), _lanes(g)] = blk.pair(acc_scr[g] / l0, acc_scr[g + 1] / l1)
                st_ref[blk.rows(qi), _lanes(g)] = blk.pair(m0 + jnp.log2(l0), m1 + jnp.log2(l1))

        def last(qi, ctx, carry):
            k0 = pl.multiple_of(qi * tq, tq)
            pieces = [(g, h) for g in range(HEADS_PER_STEP) for h in blk.halves()]
            s_all = [_dot(ctx[g][0][r0:r0 + r], k_ref[pl.ds(k0, nc), _lanes(g)], NT) for g, (r0, r, nc, _, _) in pieces]
            soft = []
            for (g, (r0, r, nc, ri, ci)), s in zip(pieces, s_all):
                m, l = carry[g][0][r0:r0 + r], carry[g][1][r0:r0 + r]
                s = s + ctx[g][1][r0:r0 + r] - cr_ref[0, g, qi][:, :nc]
                s = jnp.where(ci <= ri, s, -jnp.inf)
                m_new = jnp.maximum(m, jnp.max(s, axis=1, keepdims=True))
                alpha = jnp.exp2(m - m_new)
                p = jnp.exp2(s - m_new)
                soft.append((m_new, alpha * l + jnp.sum(p, axis=1, keepdims=True), alpha) + _hi_lo(p))
            outs = {}
            for (g, (r0, r, nc, _, _)), (m, l, alpha, hi, lo) in zip(pieces, soft):
                v = v_ref[pl.ds(k0, nc), _lanes(g)]
                acc = alpha * acc_scr[g, pl.ds(r0, r)] + (_dot(hi, v) + _dot(lo, v))
                outs[g, r0] = (acc / l, m + jnp.log2(l))
            for g in range(0, HEADS_PER_STEP, 2):
                for r0, r, _, _, _ in blk.halves():
                    rows = pl.ds(pl.multiple_of(qi * tq + r0, r), r)
                    o_ref[rows, _lanes(g)] = blk.pair(outs[g, r0][0], outs[g + 1, r0][0])
                    st_ref[rows, _lanes(g)] = blk.pair(outs[g, r0][1], outs[g + 1, r0][1])

        _causal_blocks(nq, prep, init, [scores, softmax, values], finish, last=last)

    out = jax.ShapeDtypeStruct((T, DI), F32)
    return _call(
        body, name=name, grid=(B, H // HEADS_PER_STEP),
        in_specs=[_cols(S), _cols(S, nb), _cols(S, 2 * nb), pl.BlockSpec((1, S, LANES), lambda b, h: (b, 0, 0)),
                  _row_spec(nq, tq)],
        out_specs=[_cols(S), _cols(S)], out_shape=[out, out],
        scratch_shapes=[pltpu.VMEM((HEADS_PER_STEP, tq, GROUP), F32)], compiler_params=_params(),
    )(proj, proj, proj, cumcol, cumrow)


def _fox_bwd(proj, do, dzg, o, stat, cumcol, cumrow, name):
    T, DI = do.shape
    B, H, nq, _, tq = cumrow.shape
    S = nq * tq
    nb = DI // _step_width()

    def body(q_ref, k_ref, v_ref, do_ref, dz_ref, o_ref, st_ref, cc_ref, cr_ref, dqkv_ref, dcs_ref, dk_acc, dv_acc,
             dq_scr):
        h0 = pl.program_id(1) * HEADS_PER_STEP
        blk = _Block(tq)
        dk_acc[...] = jnp.zeros_like(dk_acc)
        dv_acc[...] = jnp.zeros_like(dv_acc)
        dcs_ref[...] = jnp.zeros_like(dcs_ref)

        def prep(g, qi):
            dq_scr[g] = jnp.zeros((tq, GROUP), F32)
            q = blk.own(g, q_ref[blk.rows(qi), _lanes(g)])
            dout = blk.own(g, do_ref[blk.rows(qi), _lanes(g)])
            delta = jnp.sum(o_ref[blk.rows(qi), _lanes(g)] * dout.astype(F32), axis=1, keepdims=True)
            lse = blk.stat(g, st_ref[blk.rows(qi), _lanes(g)])
            ccol = jnp.sum(jnp.where(blk.lane == h0 + g, cc_ref[0, blk.rows(qi), :], 0.0), axis=1, keepdims=True)
            return q, dout, lse, delta, ccol

        def init():
            return ()

        def scores(g, ctx, kj, masked, st):
            return (_dot(ctx[0], k_ref[blk.rows(kj), _lanes(g)], NT), _dot(ctx[1], v_ref[blk.rows(kj), _lanes(g)], NT))

        def softmax_bwd(g, ctx, kj, masked, st):
            s, dp = st
            _, _, lse, delta, ccol = ctx
            s = s + ccol - cr_ref[0, g, kj]
            if masked:
                s = jnp.where(blk.ci <= blk.ri, s, -jnp.inf)
            p = jnp.exp2(s - lse)
            ds = p * (dp - delta)
            return p.astype(BF16), ds.astype(BF16), jnp.sum(ds, axis=0, keepdims=True)

        def combine(kj, ctx, st):
            for g in range(0, HEADS_PER_STEP, 2):
                dv_acc[blk.rows(kj), _lanes(g)] += _dot(st[g][0], ctx[g][1], TN) + _dot(st[g + 1][0], ctx[g + 1][1], TN)
                dk_acc[blk.rows(kj), _lanes(g)] += _dot(st[g][1], ctx[g][0], TN) + _dot(st[g + 1][1], ctx[g + 1][0], TN)
            for g in range(HEADS_PER_STEP):
                dcs_ref[0, g, kj] += st[g][2]

        def queries(g, ctx, kj, masked, st):
            dq_scr[g] += _dot(st[1], blk.own(g, k_ref[blk.rows(kj), _lanes(g)]))
            return ()

        def finish(qi, ctx, carry):
            for g in range(0, HEADS_PER_STEP, 2):
                dqkv_ref[0, blk.rows(qi), _lanes(g)] = ((dq_scr[g] + dq_scr[g + 1]) * LN2).astype(BF16)

        def last(qi, ctx, carry):
            k0 = pl.multiple_of(qi * tq, tq)
            pieces = [(g, h) for g in range(HEADS_PER_STEP) for h in blk.halves()]
            mm = [(_dot(ctx[g][0][r0:r0 + r], k_ref[pl.ds(k0, nc), _lanes(g)], NT),
                   _dot(ctx[g][1][r0:r0 + r], v_ref[pl.ds(k0, nc), _lanes(g)], NT)) for g, (r0, r, nc, _, _) in pieces]
            soft = {}
            for (g, (r0, r, nc, ri, ci)), (s, dp) in zip(pieces, mm):
                _, _, lse, delta, ccol = ctx[g]
                s = s + ccol[r0:r0 + r] - cr_ref[0, g, qi][:, :nc]
                s = jnp.where(ci <= ri, s, -jnp.inf)
                p = jnp.exp2(s - lse[r0:r0 + r])
                ds = p * (dp - delta[r0:r0 + r])
                soft[g, r0] = (p.astype(BF16), ds.astype(BF16), jnp.sum(ds, axis=0, keepdims=True))
            for r0, r, nc, _, _ in blk.halves():
                for g in range(0, HEADS_PER_STEP, 2):
                    (p0, d0, _), (p1, d1, _) = soft[g, r0], soft[g + 1, r0]
                    q0, q1 = ctx[g][0][r0:r0 + r], ctx[g + 1][0][r0:r0 + r]
                    o0, o1 = ctx[g][1][r0:r0 + r], ctx[g + 1][1][r0:r0 + r]
                    dv_acc[pl.ds(k0, nc), _lanes(g)] += _dot(p0, o0, TN) + _dot(p1, o1, TN)
                    dk_acc[pl.ds(k0, nc), _lanes(g)] += _dot(d0, q0, TN) + _dot(d1, q1, TN)
                for g in range(HEADS_PER_STEP):
                    col = soft[g, r0][2]
                    if nc < tq:
                        col = jnp.concatenate([col, jnp.zeros((1, tq - nc), F32)], axis=1)
                    dcs_ref[0, g, qi] += col
            dq = {}
            for g, (r0, r, nc, _, _) in pieces:
                dq[g, r0] = dq_scr[g, pl.ds(r0, r)] + _dot(soft[g, r0][1], blk.own(g, k_ref[pl.ds(k0, nc), _lanes(g)]))
            for g in range(0, HEADS_PER_STEP, 2):
                for r0, r, _, _, _ in blk.halves():
                    rows = pl.ds(pl.multiple_of(qi * tq + r0, r), r)
                    dqkv_ref[0, rows, _lanes(g)] = ((dq[g, r0] + dq[g + 1, r0]) * LN2).astype(BF16)

        _causal_blocks(nq, prep, init, [scores, softmax_bwd, queries], finish, combine=combine, last=last)
        dqkv_ref[1] = (dk_acc[...] * LN2).astype(BF16)
        dqkv_ref[2] = dv_acc[...].astype(BF16)
        dqkv_ref[3] = dz_ref[...]

    W = _step_width()
    return _call(
        body, name=name, grid=(B, H // HEADS_PER_STEP),
        in_specs=[_cols(S), _cols(S, nb), _cols(S, 2 * nb), _cols(S), _cols(S), _cols(S), _cols(S),
                  pl.BlockSpec((1, S, LANES), lambda b, h: (b, 0, 0)), _row_spec(nq, tq)],
        out_specs=[pl.BlockSpec((4, S, W), lambda b, h: (0, b, h)), _row_spec(nq, tq)],
        out_shape=[jax.ShapeDtypeStruct((4, T, DI), BF16), jax.ShapeDtypeStruct((B, H, nq, 1, tq), F32)],
        scratch_shapes=[pltpu.VMEM((S, W), F32), pltpu.VMEM((S, W), F32), pltpu.VMEM((HEADS_PER_STEP, tq, GROUP), F32)],
        compiler_params=_params(),
    )(proj, proj, proj, do, dzg, o, stat, cumcol, cumrow)


def _log2_keep(z2):
    nz = -z2
    e = jnp.exp2(jnp.minimum(z2, nz))
    return jnp.minimum(nz, 0.0) - jnp.log2(1.0 + e), e


def _sb_fwd(proj, B, tq, name):
    T, DI = proj.shape[0], proj.shape[1] // 4
    S = T // B
    H = DI // HEAD_DIM
    nq = S // tq
    nb = DI // _step_width()

    def body(q_ref, k_ref, v_ref, o_ref, st_ref, acc_scr, c_scr):
        blk = _Block(tq)
        strict = blk.ci < blk.ri
        above = jnp.where(blk.ri > blk.ci, 1.0, 0.0).astype(BF16)

        def prep(g, qi):
            acc_scr[g] = jnp.zeros((tq, GROUP), F32)
            c_scr[g] = jnp.zeros((tq, 1), F32)
            return blk.own(g, q_ref[blk.rows(qi), _lanes(g)])

        def init():
            return ()

        def scores(g, q, kj, masked, st):
            return (_dot(q, k_ref[blk.rows(kj), _lanes(g)], NT),)

        def logs(g, q, kj, masked, st):
            (z,) = st
            lk, _ = _log2_keep(z)
            lb = z + lk
            if masked:
                lk = jnp.where(strict, lk, 0.0)
            c = c_scr[g]
            c_scr[g] = c + jnp.sum(lk, axis=1, keepdims=True)
            return (lb + c,) + _hi_lo(lk)

        def suffix(g, q, kj, masked, st):
            lbc, hi, lo = st
            return lbc, _dot(hi, above) + _dot(lo, above)

        def weights(g, q, kj, masked, st):
            lbc, after = st
            a = jnp.exp2(lbc + after)
            if masked:
                a = jnp.where(strict, a, 0.0)
            return (a.astype(BF16),)

        def values(g, q, kj, masked, st):
            acc_scr[g] += _dot(st[0], v_ref[blk.rows(kj), _lanes(g)])
            return ()

        def finish(qi, ctx, carry):
            for g in range(0, HEADS_PER_STEP, 2):
                o_ref[blk.rows(qi), _lanes(g)] = blk.pair(acc_scr[g], acc_scr[g + 1])
                st_ref[blk.rows(qi), _lanes(g)] = blk.pair(c_scr[g], c_scr[g + 1])

        _causal_blocks(nq, prep, init, [scores, logs, suffix, weights, values], finish, descending=True)

    out = jax.ShapeDtypeStruct((T, DI), F32)
    return _call(
        body, name=name, grid=(B, H // HEADS_PER_STEP), in_specs=[_cols(S), _cols(S, nb), _cols(S, 2 * nb)],
        out_specs=[_cols(S), _cols(S)], out_shape=[out, out],
        scratch_shapes=[pltpu.VMEM((HEADS_PER_STEP, tq, GROUP), F32), pltpu.VMEM((HEADS_PER_STEP, tq, 1), F32)],
        compiler_params=_params(),
    )(proj, proj, proj)


def _sb_bwd(proj, do, dzg, stat, B, tq, name):
    T, DI = do.shape
    S = T // B
    H = DI // HEAD_DIM
    nq = S // tq
    nb = DI // _step_width()

    def body(q_ref, k_ref, v_ref, do_ref, dz_ref, st_ref, dqkv_ref, dk_acc, dv_acc, dq_scr):
        blk = _Block(tq)
        strict = blk.ci < blk.ri
        upto = jnp.where(blk.ri <= blk.ci, 1.0, 0.0).astype(BF16)
        before = jnp.where(blk.ri < blk.ci, 1.0, 0.0).astype(BF16)
        dk_acc[...] = jnp.zeros_like(dk_acc)
        dv_acc[...] = jnp.zeros_like(dv_acc)

        def prep(g, qi):
            dq_scr[g] = jnp.zeros((tq, GROUP), F32)
            return (blk.own(g, q_ref[blk.rows(qi), _lanes(g)]), blk.own(g, do_ref[blk.rows(qi), _lanes(g)]),
                    blk.stat(g, st_ref[blk.rows(qi), _lanes(g)]))

        def init():
            return jnp.zeros((tq, 1), F32), jnp.zeros((tq, 1), F32)

        def scores(g, ctx, kj, masked, st):
            return st + (_dot(ctx[0], k_ref[blk.rows(kj), _lanes(g)], NT),
                         _dot(ctx[1], v_ref[blk.rows(kj), _lanes(g)], NT))

        def logs(g, ctx, kj, masked, st):
            cpre, pg, z, da = st
            lk, e = _log2_keep(z)
            inv = 1.0 / (1.0 + e)
            sig = jnp.where(z >= 0.0, inv, e * inv)
            lbt = (z + lk) + (ctx[2] - cpre)
            if masked:
                lk = jnp.where(strict, lk, 0.0)
            return (cpre + jnp.sum(lk, axis=1, keepdims=True), pg, da, lbt, sig) + _hi_lo(lk)

        def prefix(g, ctx, kj, masked, st):
            cpre, pg, da, lbt, sig, hi, lo = st
            return cpre, pg, da, lbt, sig, _dot(hi, upto) + _dot(lo, upto)

        def weights(g, ctx, kj, masked, st):
            cpre, pg, da, lbt, sig, pre = st
            a = jnp.exp2(lbt - pre)
            if masked:
                a = jnp.where(strict, a, 0.0)
            gr = da * a
            return cpre, pg, sig, a.astype(BF16), gr, gr.astype(BF16)

        def grad_prefix(g, ctx, kj, masked, st):
            cpre, pg, sig, ab, gr, gb = st
            return cpre, pg, sig, ab, gr, _dot(gb, before)

        def dlogits(g, ctx, kj, masked, st):
            cpre, pg, sig, ab, gr, pfx = st
            dz = gr - sig * (gr + (pfx + pg))
            if masked:
                dz = jnp.where(strict, dz, 0.0)
            return cpre, pg + jnp.sum(gr, axis=1, keepdims=True), ab, dz.astype(BF16)

        def combine(kj, ctx, st):
            for g in range(0, HEADS_PER_STEP, 2):
                dv_acc[blk.rows(kj), _lanes(g)] += _dot(st[g][2], ctx[g][1], TN) + _dot(st[g + 1][2], ctx[g + 1][1], TN)
                dk_acc[blk.rows(kj), _lanes(g)] += _dot(st[g][3], ctx[g][0], TN) + _dot(st[g + 1][3], ctx[g + 1][0], TN)

        def queries(g, ctx, kj, masked, st):
            cpre, pg, _, dzb = st
            dq_scr[g] += _dot(dzb, blk.own(g, k_ref[blk.rows(kj), _lanes(g)]))
            return cpre, pg

        def finish(qi, ctx, carry):
            for g in range(0, HEADS_PER_STEP, 2):
                dqkv_ref[0, blk.rows(qi), _lanes(g)] = ((dq_scr[g] + dq_scr[g + 1]) * LN2).astype(BF16)

        _causal_blocks(nq, prep, init, [scores, logs, prefix, weights, grad_prefix, dlogits, queries], finish,
                       combine=combine)
        dqkv_ref[1] = (dk_acc[...] * LN2).astype(BF16)
        dqkv_ref[2] = dv_acc[...].astype(BF16)
        dqkv_ref[3] = dz_ref[...]

    W = _step_width()
    return _call(
        body, name=name, grid=(B, H // HEADS_PER_STEP),
        in_specs=[_cols(S), _cols(S, nb), _cols(S, 2 * nb), _cols(S), _cols(S), _cols(S)],
        out_specs=pl.BlockSpec((4, S, W), lambda b, h: (0, b, h)),
        out_shape=jax.ShapeDtypeStruct((4, T, DI), BF16),
        scratch_shapes=[pltpu.VMEM((S, W), F32), pltpu.VMEM((S, W), F32), pltpu.VMEM((HEADS_PER_STEP, tq, GROUP), F32)],
        compiler_params=_params(),
    )(proj, proj, proj, do, dzg, stat)


def _row_tile(R, C, n_arrays):
    budget = 24 * 1024 * 1024 // (2 * n_arrays * 4 * max(C, LANES))
    return _tile(R, max(8, budget), 8)


def _ew_sum(parts, name, also_bf16=False):
    R, C = parts[0].shape
    tr = _row_tile(R, C, len(parts) + 2)
    n = len(parts)

    def body(*refs):
        acc = refs[0][...].astype(F32) + refs[1][...].astype(F32)
        for r in refs[2:n]:
            acc = acc + r[...].astype(F32)
        refs[n][...] = acc
        if also_bf16:
            refs[n + 1][...] = acc.astype(BF16)

    blk = pl.BlockSpec((tr, C), lambda i: (i, 0))
    out_shape = [jax.ShapeDtypeStruct((R, C), F32)] + ([jax.ShapeDtypeStruct((R, C), BF16)] if also_bf16 else [])
    return _call(
        body, name=name, grid=(R // tr,), in_specs=[blk] * n, out_specs=[blk] * len(out_shape),
        out_shape=out_shape, compiler_params=_params(),
    )(*parts)


def _adamw(w, g, m, v, name):
    R, C = w.shape
    tr = _row_tile(R, C, 7)
    c1 = 1.0 / (1.0 - ADAM_B1 ** ADAM_STEP)
    c2 = 1.0 / (1.0 - ADAM_B2 ** ADAM_STEP)

    def body(w_ref, g_ref, m_ref, v_ref, d_ref, m2_ref, v2_ref):
        gv = g_ref[...]
        m2 = ADAM_B1 * m_ref[...] + (1.0 - ADAM_B1) * gv
        v2 = ADAM_B2 * v_ref[...] + (1.0 - ADAM_B2) * (gv * gv)
        m2_ref[...] = m2
        v2_ref[...] = v2
        d_ref[...] = -ADAM_LR * ((m2 * c1) / (jnp.sqrt(v2 * c2) + ADAM_EPS) + ADAM_WD * w_ref[...])

    blk = pl.BlockSpec((tr, C), lambda i: (i, 0))
    out = jax.ShapeDtypeStruct((R, C), F32)
    return _call(
        body, name=name, grid=(R // tr,), in_specs=[blk] * 4, out_specs=[blk] * 3, out_shape=[out] * 3,
        compiler_params=_params(),
    )(w, g, m, v)


def _me():
    return lax.axis_index("x"), lax.axis_index("y"), lax.axis_index("c")


def _chip_of(x, y):
    return 2 * x + y


def _other_chips(x, y):
    return [(x, 1 - y), (1 - x, y), (1 - x, 1 - y)]


def _gather_steps(ins_h, outs_h, send1, recv1, send2, recv2):
    nh = len(ins_h)
    x, y, c = _me()
    mine = _chip_of(x, y)
    chips = _other_chips(x, y)
    sib = (x, y, 1 - c)

    def landed(i, k, half):
        return outs_h[i].at[_chip_of(*chips[k]), half]

    def first(i, k):
        return pltpu.make_async_remote_copy(
            src_ref=ins_h[i].at[c], dst_ref=outs_h[i].at[mine, c], send_sem=send1.at[i, k], recv_sem=recv1.at[i, k],
            device_id=(*chips[k], c), device_id_type=MESH)

    def passed(i, k):
        return pltpu.make_async_remote_copy(
            src_ref=landed(i, k, c), dst_ref=landed(i, k, c), send_sem=send2.at[i, k], recv_sem=recv2.at[i, k],
            device_id=sib, device_id_type=MESH)

    def start():
        for i in range(nh):
            for k in range(3):
                first(i, k).start()

    def finish():
        for i in range(nh):
            for k in range(3):
                pltpu.make_async_remote_copy(
                    src_ref=ins_h[i].at[c], dst_ref=landed(i, k, c), send_sem=send1.at[i, k], recv_sem=recv1.at[i, k],
                    device_id=(*chips[k], c), device_id_type=MESH).wait_recv()
                passed(i, k).start()
        for i in range(nh):
            for k in range(3):
                pltpu.make_async_remote_copy(
                    src_ref=landed(i, k, c), dst_ref=landed(i, k, 1 - c), send_sem=send2.at[i, k],
                    recv_sem=recv2.at[i, k], device_id=sib, device_id_type=MESH).wait_recv()
        for i in range(nh):
            for k in range(3):
                first(i, k).wait_send()
                passed(i, k).wait_send()

    return start, finish


def _gather_sems(nh):
    return [pltpu.SemaphoreType.DMA((nh, 3)) for _ in range(4)]


def _gather_weights(halves, smalls):
    nh, ns = len(halves), len(smalls)

    def body(*refs):
        ins_h, ins_s = refs[:nh], refs[nh:nh + ns]
        outs_h, outs_s = refs[nh + ns:2 * nh + ns], refs[2 * nh + ns:2 * (nh + ns)]
        send1, recv1, send2, recv2, send3, recv3 = refs[2 * (nh + ns):]
        x, y, c = _me()
        mine = _chip_of(x, y)
        chips = _other_chips(x, y)

        def small(i, k):
            return pltpu.make_async_remote_copy(
                src_ref=ins_s[i], dst_ref=outs_s[i].at[mine], send_sem=send3.at[i, k], recv_sem=recv3.at[i, k],
                device_id=(*chips[k], c), device_id_type=MESH)

        start, finish = _gather_steps(ins_h, outs_h, send1, recv1, send2, recv2)
        start()
        for i in range(ns):
            for k in range(3):
                small(i, k).start()
        finish()
        for i in range(ns):
            for k in range(3):
                pltpu.make_async_remote_copy(
                    src_ref=ins_s[i], dst_ref=outs_s[i].at[_chip_of(*chips[k])], send_sem=send3.at[i, k],
                    recv_sem=recv3.at[i, k], device_id=(*chips[k], c), device_id_type=MESH).wait_recv()
                small(i, k).wait_send()

    out_shape = ([jax.ShapeDtypeStruct((4,) + a.shape, a.dtype) for a in halves]
                 + [jax.ShapeDtypeStruct((4,) + a.shape, a.dtype) for a in smalls])
    n = nh + ns
    res = _call(
        body, name="gather_weights", in_specs=[HBM] * n, out_specs=[HBM] * n, out_shape=out_shape,
        scratch_shapes=_gather_sems(nh) + [pltpu.SemaphoreType.DMA((max(ns, 1), 3)),
                                           pltpu.SemaphoreType.DMA((max(ns, 1), 3))],
        compiler_params=_params(),
    )(*halves, *smalls)
    return res[:nh], res[nh:]


def _plan(ins, outs, sems, copies):
    def steps(in_refs, out_refs, *sem_refs):
        def start():
            for cp in copies(in_refs, out_refs, *sem_refs):
                cp.start()

        def finish():
            for cp in copies(in_refs, out_refs, *sem_refs):
                cp.wait()

        return start, finish

    return dict(ins=list(ins), outs=list(outs), sems=list(sems), steps=steps)


def _pair_exchange_plan(grads):
    n = len(grads)

    def copies(ins, got, send, recv):
        x, y, c = _me()
        return [pltpu.make_async_remote_copy(
            src_ref=ins[i].at[j, 1 - c], dst_ref=got[i].at[j], send_sem=send.at[i, j], recv_sem=recv.at[i, j],
            device_id=(x, y, 1 - c), device_id_type=MESH) for i in range(n) for j in range(4)]

    return _plan(grads, [jax.ShapeDtypeStruct((4,) + g.shape[2:], g.dtype) for g in grads],
                 [pltpu.SemaphoreType.DMA((n, 4)), pltpu.SemaphoreType.DMA((n, 4))], copies)


def _chip_exchange_plan(sums):
    n = len(sums)

    def copies(ins, got, send, recv):
        x, y, c = _me()
        chips = _other_chips(x, y)
        return [pltpu.make_async_remote_copy(
            src_ref=ins[i].at[_chip_of(*chips[k])], dst_ref=got[i].at[k], send_sem=send.at[i, k],
            recv_sem=recv.at[i, k], device_id=(*chips[k], c), device_id_type=MESH) for i in range(n) for k in range(3)]

    return _plan(sums, [jax.ShapeDtypeStruct((3,) + a.shape[1:], a.dtype) for a in sums],
                 [pltpu.SemaphoreType.DMA((n, 3)), pltpu.SemaphoreType.DMA((n, 3))], copies)


def _pair_share_plan(halves):
    n = len(halves)

    def copies(ins, outs, send, recv):
        x, y, c = _me()
        return [pltpu.make_async_remote_copy(
            src_ref=ins[i], dst_ref=outs[i], send_sem=send.at[i], recv_sem=recv.at[i],
            device_id=(x, y, 1 - c), device_id_type=MESH) for i in range(n)]

    return _plan(halves, [jax.ShapeDtypeStruct(h.shape, h.dtype) for h in halves],
                 [pltpu.SemaphoreType.DMA((n,)), pltpu.SemaphoreType.DMA((n,))], copies)


def _run_exchange(plan, name):
    ni, no = len(plan["ins"]), len(plan["outs"])

    def body(*refs):
        start, finish = plan["steps"](refs[:ni], refs[ni:ni + no], *refs[ni + no:])
        start()
        finish()

    return _call(
        body, name=name, in_specs=[HBM] * ni, out_specs=[HBM] * no, out_shape=plan["outs"],
        scratch_shapes=plan["sems"], compiler_params=_params(),
    )(*plan["ins"])


def _allreduce_small(vec):
    P = vec.shape[1]

    def body(v_ref, sum_ref, all_ref, send, recv):
        x, y, c = _me()
        me = 4 * x + 2 * y + c
        all_ref[pl.ds(me, 1)] = v_ref[...][None]
        cps = []
        for d in range(1, 8):
            peer = (jnp.bitwise_xor(x, d >> 2), jnp.bitwise_xor(y, (d >> 1) & 1), jnp.bitwise_xor(c, d & 1))
            r = pltpu.make_async_remote_copy(
                src_ref=v_ref, dst_ref=all_ref.at[me], send_sem=send.at[d - 1], recv_sem=recv.at[d - 1],
                device_id=peer, device_id_type=MESH)
            r.start()
            cps.append(r)
        for d in range(1, 8):
            src = jnp.bitwise_xor(me, d)
            pltpu.make_async_remote_copy(
                src_ref=v_ref, dst_ref=all_ref.at[src], send_sem=send.at[d - 1], recv_sem=recv.at[d - 1],
                device_id=(x, y, c), device_id_type=MESH).wait_recv()
        for r in cps:
            r.wait_send()
        acc = all_ref[0]
        for i in range(1, 8):
            acc = acc + all_ref[i]
        sum_ref[...] = acc

    vm = pl.BlockSpec(memory_space=pltpu.VMEM)
    return _call(
        body, name="allreduce_small", in_specs=[vm], out_specs=[vm, vm],
        out_shape=[jax.ShapeDtypeStruct((8, P), F32), jax.ShapeDtypeStruct((8, 8, P), F32)],
        scratch_shapes=[pltpu.SemaphoreType.DMA((7,)), pltpu.SemaphoreType.DMA((7,))],
        compiler_params=_params(),
    )(vec)[0]


def _per_batch(mod, B, D):
    return [mod[:B, i * D:(i + 1) * D].reshape(B, 1, D) for i in range(3)]


def _pad_rows8(a):
    return jnp.concatenate([a, jnp.zeros((8 - a.shape[0],) + a.shape[1:], a.dtype)], axis=0)


def _layer_fwd(x, c8, w, S, fox, tag, gather=()):
    T, D = x.shape
    B = T // S
    DI = w["w_out"].shape[0]
    H = DI // HEAD_DIM
    tq = _tile(S, FOX_BLOCK if fox else SB_BLOCK, 8)
    mod = _mod_fwd(c8, w["w_ada"], w["b_ada"], tag + "_mod_fwd")
    shift, scale, gate = _per_batch(mod, B, D)
    proj, h, gathered = _ln_proj(x, shift, scale, w["norm_g"], w["w_in"], S, tag + "_ln_proj", gather)
    saved = dict(x=x, h=h, proj=proj, scale=scale, gate=gate, gathered=gathered)
    if fox:
        fl = _mm(h, w["w_f"], "nn", F32, tag + "_flogit").reshape(B, S, LANES)
        cum = _cum_fwd(fl, w["b_f"], tag + "_cum_fwd")
        cumrow = cum[:, :, :H].transpose(0, 2, 1).reshape(B, H, S // tq, 1, tq)
        o, stat = _fox_fwd(proj, cum, cumrow, tag + "_attn_fwd")
        saved.update(fl=fl, cum=cum, cumrow=cumrow)
    else:
        o, stat = _sb_fwd(proj, B, tq, tag + "_attn_fwd")
    xo, y, u = _gate_out(o, proj, w["w_out"], x, gate, S, tag + "_gate_out")
    saved.update(o=o, stat=stat, y=y, u=u)
    return xo, saved


def _hosted(side, sent, call):
    if side is None:
        return call(None), None
    plan, _ = next(side) if sent is None else side.send(sent)
    return call(plan)


def _layer_bwd(dxo, sv, w, cT, S, fox, tag, side=None):
    T, D = dxo.shape
    B = T // S
    DI = w["w_out"].shape[0]
    H = DI // HEAD_DIM
    tq = _tile(S, FOX_BLOCK if fox else SB_BLOCK, 8)
    (dy, do, dzg, dgate), landed = _hosted(side, None, lambda r: _out_bwd(
        dxo, sv["y"], sv["gate"], w["w_out"], sv["o"], sv["proj"], S, tag + "_out_bwd", rider=r))
    g = {"w_out": _mm(sv["u"], dy, "tn", F32, tag + "_dw_out", tm=1024, tn=1024, tk=2048)}
    q_cols = jnp.where(jnp.arange(4 * DI)[None, :] < DI, Q_SCALE, 1.0).astype(F32)
    if fox:
        dproj, dcs = _fox_bwd(sv["proj"], do, dzg, sv["o"], sv["stat"], sv["cum"], sv["cumrow"], tag + "_attn_bwd")
        dcs = dcs.reshape(B, H, S).transpose(0, 2, 1)
        dcs = jnp.concatenate([dcs, jnp.zeros((B, S, LANES - H), F32)], axis=-1)
        dfl, db_f = _cum_bwd(dcs, sv["fl"], w["b_f"], tag + "_cum_bwd")
        g["b_f"] = db_f[:, :H]
        dfl = dfl.reshape(T, LANES).astype(BF16)
    else:
        dproj = _sb_bwd(sv["proj"], do, dzg, sv["stat"], B, tq, tag + "_attn_bwd")
    g["w_in"], landed = _hosted(side, landed, lambda r: _mm(sv["h"], dproj, "tn", F32, tag + "_dw_in", tm=1024, tn=2048,
                                                            tk=1024, col_scale=q_cols, rider=r))
    dh, landed = _hosted(side, landed, lambda r: _mm(dproj, w["w_in"], "nt", F32, tag + "_dh", tm=2048, tn=1024,
                                                     tk=1024, rider=r))
    dhs = [dh]
    if side is not None:
        try:
            side.send(landed)
        except StopIteration as done:
            g["side"] = done.value
    if fox:
        dw_f = _mm(sv["h"], dfl, "tn", F32, tag + "_dw_f", tm=1024, tn=LANES, tk=2048)
        g["w_in"] = jnp.concatenate([g["w_in"], dw_f[:, :H]], axis=1)
        dhs.append(_mm(dfl, w["w_f"], "nt", F32, tag + "_dh_f", tm=2048, tn=1024, tk=LANES))
    dx, dshift, dscale, dg = _ln_bwd(dhs, sv["x"], dxo, sv["scale"], w["norm_g"], S, tag + "_ln_bwd")
    g["norm_g"] = dg
    dmod = jnp.concatenate([dshift, dscale, dgate], axis=-1).reshape(B, 3 * D)
    g["w_ada"], g["b_ada"] = _mod_bwd(cT, _pad_rows8(dmod), B, tag + "_mod_bwd")
    return dx, g


def _local_step(x3, c, tgt3, wf, ws, final_g, sb_halves=(), sb_side=None):
    B, S, D = x3.shape
    T = B * S
    x = x3.reshape(T, D)
    c8 = _pad_rows8(c)
    cT = c8.T
    x1, sv1 = _layer_fwd(x, c8, wf, S, True, "fox", sb_halves)
    if sb_halves:
        ws = ws(sv1["gathered"])
    x2, sv2 = _layer_fwd(x1, c8, ws, S, False, "sb")
    dx2, dgf, loss = _final_loss(x2, tgt3.reshape(T, D), final_g, S, "final_loss")
    dx1, gs = _layer_bwd(dx2, sv2, ws, cT, S, False, "sb")
    dx0, gf = _layer_bwd(dx1, sv1, wf, cT, S, True, "fox", None if sb_side is None else sb_side(gs))
    return loss, dx0.reshape(B, S, D), gf, gs, dgf


def _cols_to_shards(a):
    R, C4 = a.shape
    return a.reshape(R, 4, C4 // 4).transpose(1, 0, 2)


def _shards_to_cols(a):
    n, R, C = a.shape
    return a.transpose(1, 0, 2).reshape(R, n * C)


def kernel(x, c, fox_norm_g, fox_w_ada, fox_b_ada, fox_w_in, fox_b_f, fox_w_out, sb_norm_g, sb_w_ada, sb_b_ada, sb_w_in, sb_w_out, final_norm_g, loss_target, m_fox_norm_g, m_fox_w_ada, m_fox_b_ada, m_fox_w_in, m_fox_b_f, m_fox_w_out, m_sb_norm_g, m_sb_w_ada, m_sb_b_ada, m_sb_w_in, m_sb_w_out, m_final_norm_g, v_fox_norm_g, v_fox_w_ada, v_fox_b_ada, v_fox_w_in, v_fox_b_f, v_fox_w_out, v_sb_norm_g, v_sb_w_ada, v_sb_b_ada, v_sb_w_in, v_sb_w_out, v_final_norm_g):
    B, S, D = x.shape
    DI = 4 * fox_w_out.shape[1]
    H = DI // HEAD_DIM
    chip = _chip_of(lax.axis_index("x"), lax.axis_index("y"))

    big_names = ["fox_w_ada", "fox_w_in", "fox_w_out", "sb_w_ada", "sb_w_in", "sb_w_out"]
    big = dict(fox_w_ada=fox_w_ada[0], fox_w_in=fox_w_in[0], fox_w_out=fox_w_out[0],
               sb_w_ada=sb_w_ada[0], sb_w_in=sb_w_in[0], sb_w_out=sb_w_out[0])
    for n in ("fox_w_in", "sb_w_in"):
        width = big[n].shape[1]
        is_q = chip * width + jnp.arange(width)[None, :] < DI
        big[n] = big[n] * jnp.where(is_q, Q_SCALE, 1.0).astype(F32)
    halves = {n: big[n].astype(BF16).reshape(2, big[n].shape[0] // 2, big[n].shape[1]) for n in big_names}
    fox_names, sb_names = big_names[:3], big_names[3:]

    def assemble(names, gathered):
        full = {}
        for n, a in zip(names, gathered):
            a = lax.dynamic_update_index_in_dim(a, halves[n], chip, 0)
            a = a.reshape(4, a.shape[1] * a.shape[2], a.shape[3])
            full[n] = a.reshape(4 * a.shape[1], a.shape[2]) if n.endswith("w_out") else _shards_to_cols(a)
        return full

    gathered, gsmall = _gather_weights([halves[n] for n in fox_names], [sb_norm_g, sb_b_ada])
    gsmall = [lax.dynamic_update_index_in_dim(a, own, chip, 0) for a, own in zip(gsmall, [sb_norm_g, sb_b_ada])]
    full = assemble(fox_names, gathered)
    sb_norm_full = gsmall[0].reshape(1, D)
    sb_b_ada_full = gsmall[1].reshape(1, 3 * D)
    w_f = jnp.concatenate([full["fox_w_in"][:, 4 * DI:], jnp.zeros((D, LANES - H), BF16)], axis=1)
    b_f = jnp.concatenate([fox_b_f, jnp.zeros((1, LANES - H), F32)], axis=1)
    wf = dict(w_ada=full["fox_w_ada"], b_ada=fox_b_ada, norm_g=fox_norm_g, w_in=full["fox_w_in"][:, :4 * DI],
              w_f=w_f, b_f=b_f, w_out=full["fox_w_out"])

    def ws(gathered_sb):
        f = assemble(sb_names, gathered_sb)
        return dict(w_ada=f["sb_w_ada"], b_ada=sb_b_ada_full, norm_g=sb_norm_full, w_in=f["sb_w_in"], w_out=f["sb_w_out"])

    core = lax.axis_index("c")

    def reduction(names, part, tag):
        shard_major = []
        for n in names:
            a = part[n]
            a = a.reshape(4, a.shape[0] // 4, a.shape[1]) if n.endswith("w_out") else _cols_to_shards(a)
            shard_major.append(a.reshape(4, 2, a.shape[1] // 2, a.shape[2]))
        got = yield _pair_exchange_plan(shard_major), tag + "_grad_pair_exchange"
        pair_f32, pair_bf16 = [], []
        for n, g4, b in zip(names, shard_major, got):
            a = lax.dynamic_index_in_dim(g4, core, axis=1, keepdims=False)
            r, C = a.shape[1:]
            s32, s16 = _ew_sum([a.reshape(4 * r, C), b.reshape(4 * r, C)], n + "_pair_sum", also_bf16=True)
            pair_f32.append(s32.reshape(4, r, C))
            pair_bf16.append(s16.reshape(4, r, C))
        others = yield _chip_exchange_plan(pair_bf16), tag + "_grad_chip_exchange"
        reduced_halves = [_ew_sum([lax.dynamic_index_in_dim(a, chip, axis=0, keepdims=False), b[0], b[1], b[2]],
                                  n + "_chip_sum")[0] for n, a, b in zip(names, pair_f32, others)]
        theirs = yield _pair_share_plan(reduced_halves), tag + "_grad_pair_share"
        return {n: jnp.concatenate([jnp.where(core == 0, a, b), jnp.where(core == 0, b, a)], axis=0)
                for n, a, b in zip(names, reduced_halves, theirs)}

    def sb_side(gs):
        return reduction(sb_names, dict(sb_w_ada=gs["w_ada"], sb_w_in=gs["w_in"], sb_w_out=gs["w_out"]), "sb")

    loss, grad_x, gf, gs, dgf = _local_step(x, c, loss_target, wf, ws, final_norm_g.reshape(1, D),
                                            [halves[n] for n in sb_names], sb_side)
    grad_big = dict(gf["side"])
    fox_red = reduction(fox_names, dict(fox_w_ada=gf["w_ada"], fox_w_in=gf["w_in"], fox_w_out=gf["w_out"]), "fox")
    try:
        plan, name = next(fox_red)
        while True:
            plan, name = fox_red.send(_run_exchange(plan, name))
    except StopIteration as done:
        grad_big.update(done.value)

    pieces = [loss, gf["norm_g"], gf["b_ada"], jnp.concatenate([gf["b_f"], jnp.zeros((1, LANES - H), F32)], axis=1),
              gs["norm_g"], gs["b_ada"], dgf]
    vec = jnp.concatenate(pieces, axis=1)
    red = _allreduce_small(_pad_rows8(vec))[0:1]
    offs = [0]
    for p in pieces:
        offs.append(offs[-1] + p.shape[1])
    r_loss, r_fng, r_fba, r_fbf, r_sng, r_sba, r_fin = [red[:, offs[i]:offs[i + 1]] for i in range(7)]
    small_grads = dict(
        fox_norm_g=r_fng, fox_b_ada=r_fba, fox_b_f=r_fbf[:, :H],
        sb_norm_g=lax.dynamic_slice_in_dim(r_sng, chip * (D // 4), D // 4, axis=1),
        sb_b_ada=lax.dynamic_slice_in_dim(r_sba, chip * (3 * D // 4), 3 * D // 4, axis=1),
        final_norm_g=r_fin)

    weights = dict(fox_norm_g=fox_norm_g, fox_w_ada=fox_w_ada, fox_b_ada=fox_b_ada, fox_w_in=fox_w_in, fox_b_f=fox_b_f,
                   fox_w_out=fox_w_out, sb_norm_g=sb_norm_g, sb_w_ada=sb_w_ada, sb_b_ada=sb_b_ada, sb_w_in=sb_w_in,
                   sb_w_out=sb_w_out, final_norm_g=final_norm_g)
    ms = dict(fox_norm_g=m_fox_norm_g, fox_w_ada=m_fox_w_ada, fox_b_ada=m_fox_b_ada, fox_w_in=m_fox_w_in,
              fox_b_f=m_fox_b_f, fox_w_out=m_fox_w_out, sb_norm_g=m_sb_norm_g, sb_w_ada=m_sb_w_ada,
              sb_b_ada=m_sb_b_ada, sb_w_in=m_sb_w_in, sb_w_out=m_sb_w_out, final_norm_g=m_final_norm_g)
    vs = dict(fox_norm_g=v_fox_norm_g, fox_w_ada=v_fox_w_ada, fox_b_ada=v_fox_b_ada, fox_w_in=v_fox_w_in,
              fox_b_f=v_fox_b_f, fox_w_out=v_fox_w_out, sb_norm_g=v_sb_norm_g, sb_w_ada=v_sb_w_ada,
              sb_b_ada=v_sb_b_ada, sb_w_in=v_sb_w_in, sb_w_out=v_sb_w_out, final_norm_g=v_final_norm_g)
    order = ["fox_norm_g", "fox_w_ada", "fox_b_ada", "fox_w_in", "fox_b_f", "fox_w_out", "sb_norm_g", "sb_w_ada",
             "sb_b_ada", "sb_w_in", "sb_w_out", "final_norm_g"]
    grads, deltas, new_m, new_v = {}, {}, {}, {}
    for n in big_names:
        shp = weights[n].shape
        g2 = grad_big[n]
        d, m2, v2 = _adamw(weights[n][0], g2, ms[n][0], vs[n][0], n + "_adamw")
        grads[n], deltas[n], new_m[n], new_v[n] = g2.reshape(shp), d.reshape(shp), m2.reshape(shp), v2.reshape(shp)
    small_names = [n for n in order if n not in big_names]
    sizes = [small_grads[n].shape[1] for n in small_names]
    total = sum(sizes)
    padn = (-total) % LANES

    def pack(d):
        return jnp.concatenate([d[n].reshape(1, -1) for n in small_names] + [jnp.ones((1, padn), F32)], axis=1)

    sd, sm, sv_ = _adamw(pack(weights), pack(small_grads), pack(ms), pack(vs), "small_adamw")
    o = 0
    for n, sz in zip(small_names, sizes):
        shp = weights[n].shape
        grads[n] = small_grads[n].reshape(shp)
        deltas[n], new_m[n], new_v[n] = (t[:, o:o + sz].reshape(shp) for t in (sd, sm, sv_))
        o += sz
    return (r_loss[0, 0], grad_x, *[grads[n] for n in order], *[deltas[n] for n in order],
            *[new_m[n] for n in order], *[new_v[n] for n in order])
```

```python
import jax
import jax.numpy as jnp
from jax import lax
from jax.experimental import pallas as pl
from jax.experimental.pallas import tpu as pltpu

F32 = jnp.float32
BF16 = jnp.bfloat16
HEAD_DIM = 64
LOG2E = 1.4426950408889634
LN2 = 0.6931471805599453
Q_SCALE = HEAD_DIM ** -0.5 * LOG2E
LANES = 128
NORM_EPS = 1e-6
ADAM_LR = 0.001
ADAM_B1 = 0.9
ADAM_B2 = 0.999
ADAM_EPS = 1e-08
ADAM_WD = 0.01
ADAM_STEP = 10
VMEM_LIMIT = 56 * 1024 * 1024
SB_BLOCK = 256
FOX_BLOCK = 512
MESH = pl.DeviceIdType.MESH
HBM = pl.BlockSpec(memory_space=pltpu.HBM)
NT = (((1,), (1,)), ((), ()))
TN = (((0,), (0,)), ((), ()))


def _call(body, **kw):
    return pl.pallas_call(body, **kw)


def _params(**kw):
    return pltpu.CompilerParams(vmem_limit_bytes=VMEM_LIMIT, **kw)


def _tile(dim, pref, mult=128):
    if dim <= pref:
        return dim
    t = (pref // mult) * mult
    while t >= mult:
        if dim % t == 0:
            return t
        t -= mult
    return dim


def _sigmoid(x):
    return 1.0 / (1.0 + jnp.exp(-x))


def _split3(x):
    hi = x.astype(BF16)
    r = x - hi.astype(F32)
    mid = r.astype(BF16)
    lo = (r - mid.astype(F32)).astype(BF16)
    return hi, mid, lo


def _mm(a, b, mode, out_dtype, name, tm=512, tn=512, tk=512, col_scale=None, rider=None):
    a_slabs = a.shape[0] if a.ndim == 3 else 0
    b_slabs = b.shape[0] if b.ndim == 3 else 0
    if mode == "nn":
        (M, K), (_, N) = a.shape, b.shape
    elif mode == "nt":
        M, K = (a.shape[1], a_slabs * a.shape[2]) if a_slabs else a.shape
        N = b.shape[0]
    else:
        K, M = a.shape
        N = b_slabs * b.shape[2] if b_slabs else b.shape[1]
    tm, tn, tk = _tile(M, tm), _tile(N, tn), _tile(K, tk)
    if a_slabs:
        tk = _tile(a.shape[2], tk)
    if b_slabs:
        tn = _tile(b.shape[2], tn)
    nk = K // tk
    dims = {"nn": (((1,), (0,)), ((), ())), "nt": NT, "tn": TN}[mode]

    r_ins = rider["ins"] if rider else []
    r_outs = rider["outs"] if rider else []
    r_sems = rider["sems"] if rider else []
    nc = 0 if col_scale is None else 1
    ni, no = len(r_ins), len(r_outs)
    grid = (M // tm, N // tn, nk)

    def body(a_ref, b_ref, *rest):
        o_ref = rest[nc + ni]
        acc_ref = rest[nc + ni + 1 + no]
        k = pl.program_id(2)
        if rider:
            start, finish = rider["steps"](rest[nc:nc + ni], rest[nc + ni + 1:nc + ni + 1 + no], *rest[nc + ni + 2 + no:])
            at = [pl.program_id(d) for d in range(3)]
            pl.when(jnp.logical_and(jnp.logical_and(at[0] == 0, at[1] == 0), at[2] == 0))(start)

        @pl.when(k == 0)
        def _():
            acc_ref[...] = jnp.zeros_like(acc_ref)

        acc_ref[...] += lax.dot_general(a_ref[...], b_ref[...], dims, preferred_element_type=F32)

        @pl.when(k == nk - 1)
        def _():
            acc = acc_ref[...]
            if col_scale is not None:
                acc = acc * rest[0][...]
            o_ref[...] = acc.astype(out_dtype)

        if rider:
            pl.when(jnp.logical_and(jnp.logical_and(at[0] == grid[0] - 1, at[1] == grid[1] - 1), at[2] == nk - 1))(finish)

    if a_slabs:
        per = a.shape[2] // tk
        a_spec = pl.BlockSpec((None, tm, tk), lambda i, j, k: (k // per, i, k % per))
    elif mode == "tn":
        a_spec = pl.BlockSpec((tk, tm), lambda i, j, k: (k, i))
    else:
        a_spec = pl.BlockSpec((tm, tk), lambda i, j, k: (i, k))
    if b_slabs:
        per_b = b.shape[2] // tn
        b_spec = pl.BlockSpec((None, tk, tn), lambda i, j, k: (j // per_b, k, j % per_b))
    elif mode == "nt":
        b_spec = pl.BlockSpec((tn, tk), lambda i, j, k: (j, k))
    else:
        b_spec = pl.BlockSpec((tk, tn), lambda i, j, k: (k, j))
    extra_specs = [] if col_scale is None else [pl.BlockSpec((1, tn), lambda i, j, k: (0, j))]
    extra = [] if col_scale is None else [col_scale]
    res = _call(
        body, name=name, grid=grid,
        in_specs=[a_spec, b_spec] + extra_specs + [HBM] * ni,
        out_specs=[pl.BlockSpec((tm, tn), lambda i, j, k: (i, j))] + [HBM] * no,
        out_shape=[jax.ShapeDtypeStruct((M, N), out_dtype)] + list(r_outs),
        scratch_shapes=[pltpu.VMEM((tm, tn), F32)] + list(r_sems), compiler_params=_params(),
    )(a, b, *extra, *r_ins)
    return (res[0], res[1:]) if rider else res[0]


def _mod_fwd(c8, w_ada, b_ada, name):
    D, N = w_ada.shape
    tn = _tile(N, 512)

    def body(c_ref, w_ref, b_ref, o_ref):
        c = c_ref[...]
        sc = (c * _sigmoid(c)).astype(BF16)
        o_ref[...] = jnp.dot(sc, w_ref[...], preferred_element_type=F32) + b_ref[...]

    return _call(
        body, name=name, grid=(N // tn,),
        in_specs=[pl.BlockSpec((8, D), lambda j: (0, 0)), pl.BlockSpec((D, tn), lambda j: (0, j)),
                  pl.BlockSpec((1, tn), lambda j: (0, j))],
        out_specs=pl.BlockSpec((8, tn), lambda j: (0, j)),
        out_shape=jax.ShapeDtypeStruct((8, N), F32), compiler_params=_params(),
    )(c8, w_ada, b_ada)


def _mod_bwd(cT, dmod8, nb, name):
    D = cT.shape[0]
    N = dmod8.shape[1]
    tn = _tile(N, 512)

    def body(c_ref, d_ref, w_ref, b_ref):
        c = c_ref[...]
        sc = c * _sigmoid(c)
        d = d_ref[...]
        acc = sc[:, 0:1] * d[0:1, :]
        bsum = d[0:1, :]
        for b in range(1, nb):
            acc = acc + sc[:, b:b + 1] * d[b:b + 1, :]
            bsum = bsum + d[b:b + 1, :]
        w_ref[...] = acc
        b_ref[...] = bsum

    return _call(
        body, name=name, grid=(N // tn,),
        in_specs=[pl.BlockSpec((D, 8), lambda j: (0, 0)), pl.BlockSpec((8, tn), lambda j: (0, j))],
        out_specs=[pl.BlockSpec((D, tn), lambda j: (0, j)), pl.BlockSpec((1, tn), lambda j: (0, j))],
        out_shape=[jax.ShapeDtypeStruct((D, N), F32), jax.ShapeDtypeStruct((1, N), F32)],
        compiler_params=_params(),
    )(cT, dmod8)


def _ln_proj(x, shift, scale, g, w, S, name, gather=()):
    T, D = x.shape
    N = w.shape[1]
    tm = _tile(S, 2048)
    tn = _tile(N, 1024)
    per_b = S // tm
    ng = len(gather)
    n0, n1 = T // tm, N // tn

    def body(x_ref, sh_ref, sc_ref, g_ref, w_ref, *rest):
        ins_h, (p_ref, h_ref), outs_h, sems = rest[:ng], rest[ng:ng + 2], rest[ng + 2:2 * ng + 2], rest[2 * ng + 2:]
        i, j = pl.program_id(0), pl.program_id(1)
        if ng:
            start, finish = _gather_steps(ins_h, outs_h, *sems)
            pl.when(jnp.logical_and(i == 0, j == 0))(start)

        @pl.when(j == 0)
        def _():
            xv = x_ref[...]
            r = lax.rsqrt(jnp.mean(xv * xv, axis=-1, keepdims=True) + NORM_EPS)
            h = (xv * r) * g_ref[...] * (1.0 + sc_ref[0]) + sh_ref[0]
            h_ref[...] = h.astype(BF16)

        p_ref[...] = jnp.dot(h_ref[...], w_ref[...], preferred_element_type=F32).astype(BF16)
        if ng:
            pl.when(jnp.logical_and(i == n0 - 1, j == n1 - 1))(finish)

    res = _call(
        body, name=name, grid=(n0, n1),
        in_specs=[pl.BlockSpec((tm, D), lambda i, j: (i, 0)),
                  pl.BlockSpec((1, 1, D), lambda i, j: (i // per_b, 0, 0)),
                  pl.BlockSpec((1, 1, D), lambda i, j: (i // per_b, 0, 0)),
                  pl.BlockSpec((1, D), lambda i, j: (0, 0)),
                  pl.BlockSpec((D, tn), lambda i, j: (0, j))] + [HBM] * ng,
        out_specs=[pl.BlockSpec((tm, tn), lambda i, j: (i, j)), pl.BlockSpec((tm, D), lambda i, j: (i, 0))] + [HBM] * ng,
        out_shape=[jax.ShapeDtypeStruct((T, N), BF16), jax.ShapeDtypeStruct((T, D), BF16)]
        + [jax.ShapeDtypeStruct((4,) + a.shape, a.dtype) for a in gather],
        scratch_shapes=_gather_sems(ng) if ng else [], compiler_params=_params(),
    )(x, shift, scale, g, w, *gather)
    return res[0], res[1], res[2:]


def _ln_bwd(dhs, x, dxo, scale, g, S, name):
    T, D = x.shape
    B = T // S
    tm = _tile(S, 512)
    per_b = S // tm

    nd = len(dhs)

    def body(*refs):
        x_ref, dxo_ref, sc_ref, g_ref, dx_ref, dsh_ref, dsc_ref, dg_ref = refs[nd:]
        i = pl.program_id(0)
        xv = x_ref[...]
        dh_v = refs[0][...]
        for r in refs[1:nd]:
            dh_v = dh_v + r[...]
        r = lax.rsqrt(jnp.mean(xv * xv, axis=-1, keepdims=True) + NORM_EPS)
        xn = xv * r
        gv = g_ref[...]
        one_sc = 1.0 + sc_ref[0]
        dhxn = dh_v * xn

        @pl.when(i % per_b == 0)
        def _():
            dsh_ref[...] = jnp.zeros_like(dsh_ref)
            dsc_ref[...] = jnp.zeros_like(dsc_ref)

        @pl.when(i == 0)
        def _():
            dg_ref[...] = jnp.zeros_like(dg_ref)

        dsh_ref[0] += jnp.sum(dh_v, axis=0, keepdims=True)
        dsc_ref[0] += jnp.sum(dhxn, axis=0, keepdims=True) * gv
        dg_ref[...] += jnp.sum(dhxn, axis=0, keepdims=True) * one_sc
        dxn = dh_v * (gv * one_sc)
        dx_ref[...] = r * (dxn - xn * jnp.mean(dxn * xn, axis=-1, keepdims=True)) + dxo_ref[...]

    row = pl.BlockSpec((tm, D), lambda i: (i, 0))
    per = pl.BlockSpec((1, 1, D), lambda i: (i // per_b, 0, 0))
    vec = pl.BlockSpec((1, D), lambda i: (0, 0))
    return _call(
        body, name=name, grid=(T // tm,),
        in_specs=[row] * (nd + 2) + [per, vec], out_specs=[row, per, per, vec],
        out_shape=[jax.ShapeDtypeStruct((T, D), F32), jax.ShapeDtypeStruct((B, 1, D), F32),
                   jax.ShapeDtypeStruct((B, 1, D), F32), jax.ShapeDtypeStruct((1, D), F32)],
        compiler_params=_params(),
    )(*dhs, x, dxo, scale, g)


def _gate_out(o, proj, w_out, x, gate, S, name):
    T, DI = o.shape
    D = w_out.shape[1]
    tm = _tile(S, 256)
    per_b = S // tm

    def body(o_ref, z_ref, w_ref, x_ref, g_ref, xo_ref, y_ref, u_ref):
        z = z_ref[...].astype(F32)
        u = (o_ref[...] * (z * _sigmoid(z))).astype(BF16)
        u_ref[...] = u
        y = jnp.dot(u, w_ref[...], preferred_element_type=F32)
        y_ref[...] = y
        xo_ref[...] = x_ref[...] + g_ref[0] * y

    wide = pl.BlockSpec((tm, DI), lambda i: (i, 0))
    row = pl.BlockSpec((tm, D), lambda i: (i, 0))
    return _call(
        body, name=name, grid=(T // tm,),
        in_specs=[wide, pl.BlockSpec((tm, DI), lambda i: (i, 3)), pl.BlockSpec((DI, D), lambda i: (0, 0)), row,
                  pl.BlockSpec((1, 1, D), lambda i: (i // per_b, 0, 0))],
        out_specs=[row, row, wide],
        out_shape=[jax.ShapeDtypeStruct((T, D), F32), jax.ShapeDtypeStruct((T, D), F32),
                   jax.ShapeDtypeStruct((T, DI), BF16)],
        compiler_params=_params(),
    )(o, proj, w_out, x, gate)


def _out_bwd(dxo, y, gate, w_out, o, proj, S, name):
    T, D = dxo.shape
    DI = o.shape[1]
    B = T // S
    tm = _tile(S, 256)
    per_b = S // tm

    def body(dxo_ref, y_ref, g_ref, w_ref, o_ref, z_ref, dy_ref, do_ref, dz_ref, dg_ref):
        dxo_v = dxo_ref[...]
        dy = (dxo_v * g_ref[0]).astype(BF16)
        dy_ref[...] = dy
        du = lax.dot_general(dy, w_ref[...], NT, preferred_element_type=F32)
        z = z_ref[...].astype(F32)
        sg = _sigmoid(z)
        do_ref[...] = (du * (z * sg)).astype(BF16)
        dz_ref[...] = (du * o_ref[...] * (sg * (1.0 + z * (1.0 - sg)))).astype(BF16)

        @pl.when(pl.program_id(0) % per_b == 0)
        def _():
            dg_ref[...] = jnp.zeros_like(dg_ref)

        dg_ref[0] += jnp.sum(dxo_v * y_ref[...], axis=0, keepdims=True)

    wide = pl.BlockSpec((tm, DI), lambda i: (i, 0))
    row = pl.BlockSpec((tm, D), lambda i: (i, 0))
    per = pl.BlockSpec((1, 1, D), lambda i: (i // per_b, 0, 0))
    return _call(
        body, name=name, grid=(T // tm,),
        in_specs=[row, row, per, pl.BlockSpec((DI, D), lambda i: (0, 0)), wide,
                  pl.BlockSpec((tm, DI), lambda i: (i, 3))],
        out_specs=[row, wide, wide, per],
        out_shape=[jax.ShapeDtypeStruct((T, D), BF16), jax.ShapeDtypeStruct((T, DI), BF16),
                   jax.ShapeDtypeStruct((T, DI), BF16), jax.ShapeDtypeStruct((B, 1, D), F32)],
        compiler_params=_params(),
    )(dxo, y, gate, w_out, o, proj)


def _final_loss(x, tgt, g, S, name):
    T, D = x.shape
    tm = _tile(S, 512)

    def body(x_ref, t_ref, g_ref, dx_ref, dg_ref, l_ref):
        @pl.when(pl.program_id(0) == 0)
        def _():
            dg_ref[...] = jnp.zeros_like(dg_ref)
            l_ref[...] = jnp.zeros_like(l_ref)

        xv = x_ref[...]
        gv = g_ref[...]
        r = lax.rsqrt(jnp.mean(xv * xv, axis=-1, keepdims=True) + NORM_EPS)
        xn = xv * r
        e = xn * gv - t_ref[...]
        part = jnp.sum(jnp.sum(e * e, axis=0, keepdims=True), axis=1, keepdims=True)
        l_ref[...] += (0.5 / D) * part
        dy = e * (1.0 / D)
        dg_ref[...] += jnp.sum(dy * xn, axis=0, keepdims=True)
        dxn = dy * gv
        dx_ref[...] = r * (dxn - xn * jnp.mean(dxn * xn, axis=-1, keepdims=True))

    row = pl.BlockSpec((tm, D), lambda i: (i, 0))
    return _call(
        body, name=name, grid=(T // tm,),
        in_specs=[row, row, pl.BlockSpec((1, D), lambda i: (0, 0))],
        out_specs=[row, pl.BlockSpec((1, D), lambda i: (0, 0)), pl.BlockSpec((1, LANES), lambda i: (0, 0))],
        out_shape=[jax.ShapeDtypeStruct((T, D), F32), jax.ShapeDtypeStruct((1, D), F32),
                   jax.ShapeDtypeStruct((1, LANES), F32)],
        compiler_params=_params(),
    )(x, tgt, g)


def _cum_fwd(fl, bf, name):
    B, S, _ = fl.shape
    ch = _tile(S, 256, 8)

    def body(fl_ref, b_ref, cum_ref):
        ri = lax.broadcasted_iota(jnp.int32, (ch, ch), 0)
        ci = lax.broadcasted_iota(jnp.int32, (ch, ch), 1)
        tri = jnp.where(ri >= ci, 1.0, 0.0).astype(BF16)

        def step(i, carry):
            r0 = pl.multiple_of(i * ch, ch)
            z = fl_ref[0, pl.ds(r0, ch), :] + b_ref[...]
            lf = (jnp.minimum(z, 0.0) - jnp.log(1.0 + jnp.exp(-jnp.abs(z)))) * LOG2E
            hi, mid, lo = _split3(lf)
            cs = (jnp.dot(tri, hi, preferred_element_type=F32) + jnp.dot(tri, mid, preferred_element_type=F32)
                  + jnp.dot(tri, lo, preferred_element_type=F32)) + carry
            cum_ref[0, pl.ds(r0, ch), :] = cs
            return cs[ch - 1:ch, :]

        lax.fori_loop(0, S // ch, step, jnp.zeros((1, LANES), F32))

    blk = pl.BlockSpec((1, S, LANES), lambda b: (b, 0, 0))
    return _call(
        body, name=name, grid=(B,), in_specs=[blk, pl.BlockSpec((1, LANES), lambda b: (0, 0))], out_specs=blk,
        out_shape=jax.ShapeDtypeStruct((B, S, LANES), F32), compiler_params=_params(),
    )(fl, bf)


def _cum_bwd(dcs, fl, bf, name):
    B, S, _ = fl.shape
    ch = _tile(S, 256, 8)
    n = S // ch

    def body(d_ref, fl_ref, b_ref, o_ref, db_ref):
        ri = lax.broadcasted_iota(jnp.int32, (ch, ch), 0)
        ci = lax.broadcasted_iota(jnp.int32, (ch, ch), 1)
        tri = jnp.where(ci >= ri, 1.0, 0.0).astype(BF16)

        @pl.when(pl.program_id(0) == 0)
        def _():
            db_ref[...] = jnp.zeros_like(db_ref)

        def step(t, carry):
            tail, dbsum = carry
            r0 = pl.multiple_of((n - 1 - t) * ch, ch)
            hi, mid, lo = _split3(d_ref[0, pl.ds(r0, ch), :])
            suf = (jnp.dot(tri, hi, preferred_element_type=F32) + jnp.dot(tri, mid, preferred_element_type=F32)
                   + jnp.dot(tri, lo, preferred_element_type=F32)) + tail
            z = fl_ref[0, pl.ds(r0, ch), :] + b_ref[...]
            dfl = -suf * _sigmoid(-z)
            o_ref[0, pl.ds(r0, ch), :] = dfl
            return suf[0:1, :], dbsum + jnp.sum(dfl, axis=0, keepdims=True)

        z1 = jnp.zeros((1, LANES), F32)
        _, dbsum = lax.fori_loop(0, n, step, (z1, z1))
        db_ref[...] += dbsum

    blk = pl.BlockSpec((1, S, LANES), lambda b: (b, 0, 0))
    vec = pl.BlockSpec((1, LANES), lambda b: (0, 0))
    return _call(
        body, name=name, grid=(B,), in_specs=[blk, blk, vec], out_specs=[blk, vec],
        out_shape=[jax.ShapeDtypeStruct((B, S, LANES), F32), jax.ShapeDtypeStruct((1, LANES), F32)],
        compiler_params=_params(),
    )(dcs, fl, bf)


HEADS_PER_STEP = 4
GROUP = 2 * HEAD_DIM
DIAG_PIECES = 4


def _step_width():
    return HEAD_DIM * HEADS_PER_STEP


def _cols(S, offset_blocks=0):
    return pl.BlockSpec((S, _step_width()), lambda b, h: (b, offset_blocks + h))


def _row_spec(nq, tq):
    return pl.BlockSpec((1, HEADS_PER_STEP, nq, 1, tq), lambda b, h: (b, h, 0, 0, 0))


def _lanes(g):
    return slice(GROUP * (g // 2), GROUP * (g // 2) + GROUP)


def _hi_lo(x):
    hi = x.astype(BF16)
    return hi, (x - hi.astype(F32)).astype(BF16)


def _dot(a, b, dims=None):
    if dims is None:
        return jnp.dot(a, b, preferred_element_type=F32)
    return lax.dot_general(a, b, dims, preferred_element_type=F32)


def _causal_blocks(nq, prep, init, stages, finish, combine=None, descending=False, last=None):
    heads = range(HEADS_PER_STEP)

    def qloop(qi, _):
        ctx = [prep(g, qi) for g in heads]

        def step(kj, carry, masked):
            st = list(carry)
            for n, stage in enumerate(stages):
                if combine is not None and n == len(stages) - 1:
                    combine(kj, ctx, st)
                st = [stage(g, ctx[g], kj, masked, st[g]) for g in heads]
            return tuple(st)

        carry = tuple(init() for _ in heads)
        if descending:
            carry = step(qi, carry, True)
            carry = lax.fori_loop(0, qi, lambda t, cr: step(qi - 1 - t, cr, False), carry)
        else:
            carry = lax.fori_loop(0, qi, lambda kj, cr: step(kj, cr, False), carry)
            if last is not None:
                last(qi, ctx, carry)
                return 0
            carry = step(qi, carry, True)
        finish(qi, ctx, carry)
        return 0

    lax.fori_loop(0, nq, qloop, 0)


class _Block:
    def __init__(self, tq):
        self.tq = tq
        self.lane = lax.broadcasted_iota(jnp.int32, (tq, GROUP), 1)
        self.low = self.lane < HEAD_DIM
        self.ri = lax.broadcasted_iota(jnp.int32, (tq, tq), 0)
        self.ci = lax.broadcasted_iota(jnp.int32, (tq, tq), 1)

    def rows(self, i):
        return pl.ds(pl.multiple_of(i * self.tq, self.tq), self.tq)

    def own(self, g, x):
        low = self.low[:x.shape[0]]
        return jnp.where(low if g % 2 == 0 else jnp.logical_not(low), x, jnp.zeros_like(x))

    def pair(self, a, b):
        return jnp.where(self.low[:a.shape[0]], a, b)

    def halves(self):
        r = self.tq // DIAG_PIECES
        out = []
        for rr in range(DIAG_PIECES):
            nc = r * (rr + 1)
            out.append((rr * r, r, nc, lax.broadcasted_iota(jnp.int32, (r, nc), 0) + rr * r,
                        lax.broadcasted_iota(jnp.int32, (r, nc), 1)))
        return out

    def stat(self, g, x):
        return jnp.sum(jnp.where(self.lane == HEAD_DIM * (g % 2), x, 0.0), axis=1, keepdims=True)


def _fox_fwd(proj, cumcol, cumrow, name):
    T, DI = proj.shape[0], proj.shape[1] // 4
    B, H, nq, _, tq = cumrow.shape
    S = nq * tq
    nb = DI // _step_width()

    def body(q_ref, k_ref, v_ref, cc_ref, cr_ref, o_ref, st_ref, acc_scr):
        h0 = pl.program_id(1) * HEADS_PER_STEP
        blk = _Block(tq)

        def prep(g, qi):
            acc_scr[g] = jnp.zeros((tq, GROUP), F32)
            q = blk.own(g, q_ref[blk.rows(qi), _lanes(g)])
            ccol = jnp.sum(jnp.where(blk.lane == h0 + g, cc_ref[0, blk.rows(qi), :], 0.0), axis=1, keepdims=True)
            return q, ccol

        def init():
            return jnp.full((tq, 1), -jnp.inf, F32), jnp.zeros((tq, 1), F32)

        def scores(g, ctx, kj, masked, st):
            return st + (_dot(ctx[0], k_ref[blk.rows(kj), _lanes(g)], NT),)

        def softmax(g, ctx, kj, masked, st):
            m, l, s = st
            s = s + ctx[1] - cr_ref[0, g, kj]
            if masked:
                s = jnp.where(blk.ci <= blk.ri, s, -jnp.inf)
            m_new = jnp.maximum(m, jnp.max(s, axis=1, keepdims=True))
            alpha = jnp.exp2(m - m_new)
            p = jnp.exp2(s - m_new)
            return (m_new, alpha * l + jnp.sum(p, axis=1, keepdims=True), alpha) + _hi_lo(p)

        def values(g, ctx, kj, masked, st):
            m, l, alpha, hi, lo = st
            v = v_ref[blk.rows(kj), _lanes(g)]
            acc_scr[g] = alpha * acc_scr[g] + (_dot(hi, v) + _dot(lo, v))
            return m, l

        def finish(qi, ctx, carry):
            for g in range(0, HEADS_PER_STEP, 2):
                (m0, l0), (m1, l1) = carry[g], carry[g + 1]
                o_ref[blk.rows(qi), _lanes(g)] = blk.pair(acc_scr[g] / l0, acc_scr[g + 1] / l1)
                st_ref[blk.rows(qi), _lanes(g)] = blk.pair(m0 + jnp.log2(l0), m1 + jnp.log2(l1))

        def last(qi, ctx, carry):
            k0 = pl.multiple_of(qi * tq, tq)
            pieces = [(g, h) for g in range(HEADS_PER_STEP) for h in blk.halves()]
            s_all = [_dot(ctx[g][0][r0:r0 + r], k_ref[pl.ds(k0, nc), _lanes(g)], NT) for g, (r0, r, nc, _, _) in pieces]
            soft = []
            for (g, (r0, r, nc, ri, ci)), s in zip(pieces, s_all):
                m, l = carry[g][0][r0:r0 + r], carry[g][1][r0:r0 + r]
                s = s + ctx[g][1][r0:r0 + r] - cr_ref[0, g, qi][:, :nc]
                s = jnp.where(ci <= ri, s, -jnp.inf)
                m_new = jnp.maximum(m, jnp.max(s, axis=1, keepdims=True))
                alpha = jnp.exp2(m - m_new)
                p = jnp.exp2(s - m_new)
                soft.append((m_new, alpha * l + jnp.sum(p, axis=1, keepdims=True), alpha) + _hi_lo(p))
            outs = {}
            for (g, (r0, r, nc, _, _)), (m, l, alpha, hi, lo) in zip(pieces, soft):
                v = v_ref[pl.ds(k0, nc), _lanes(g)]
                acc = alpha * acc_scr[g, pl.ds(r0, r)] + (_dot(hi, v) + _dot(lo, v))
                outs[g, r0] = (acc / l, m + jnp.log2(l))
            for g in range(0, HEADS_PER_STEP, 2):
                for r0, r, _, _, _ in blk.halves():
                    rows = pl.ds(pl.multiple_of(qi * tq + r0, r), r)
                    o_ref[rows, _lanes(g)] = blk.pair(outs[g, r0][0], outs[g + 1, r0][0])
                    st_ref[rows, _lanes(g)] = blk.pair(outs[g, r0][1], outs[g + 1, r0][1])

        _causal_blocks(nq, prep, init, [scores, softmax, values], finish, last=last)

    out = jax.ShapeDtypeStruct((T, DI), F32)
    return _call(
        body, name=name, grid=(B, H // HEADS_PER_STEP),
        in_specs=[_cols(S), _cols(S, nb), _cols(S, 2 * nb), pl.BlockSpec((1, S, LANES), lambda b, h: (b, 0, 0)),
                  _row_spec(nq, tq)],
        out_specs=[_cols(S), _cols(S)], out_shape=[out, out],
        scratch_shapes=[pltpu.VMEM((HEADS_PER_STEP, tq, GROUP), F32)], compiler_params=_params(),
    )(proj, proj, proj, cumcol, cumrow)


def _fox_bwd(proj, do, dzg, o, stat, cumcol, cumrow, name):
    T, DI = do.shape
    B, H, nq, _, tq = cumrow.shape
    S = nq * tq
    nb = DI // _step_width()

    def body(q_ref, k_ref, v_ref, do_ref, dz_ref, o_ref, st_ref, cc_ref, cr_ref, dqkv_ref, dcs_ref, dk_acc, dv_acc,
             dq_scr):
        h0 = pl.program_id(1) * HEADS_PER_STEP
        blk = _Block(tq)
        dk_acc[...] = jnp.zeros_like(dk_acc)
        dv_acc[...] = jnp.zeros_like(dv_acc)
        dcs_ref[...] = jnp.zeros_like(dcs_ref)

        def prep(g, qi):
            dq_scr[g] = jnp.zeros((tq, GROUP), F32)
            q = blk.own(g, q_ref[blk.rows(qi), _lanes(g)])
            dout = blk.own(g, do_ref[blk.rows(qi), _lanes(g)])
            delta = jnp.sum(o_ref[blk.rows(qi), _lanes(g)] * dout.astype(F32), axis=1, keepdims=True)
            lse = blk.stat(g, st_ref[blk.rows(qi), _lanes(g)])
            ccol = jnp.sum(jnp.where(blk.lane == h0 + g, cc_ref[0, blk.rows(qi), :], 0.0), axis=1, keepdims=True)
            return q, dout, lse, delta, ccol

        def init():
            return ()

        def scores(g, ctx, kj, masked, st):
            return (_dot(ctx[0], k_ref[blk.rows(kj), _lanes(g)], NT), _dot(ctx[1], v_ref[blk.rows(kj), _lanes(g)], NT))

        def softmax_bwd(g, ctx, kj, masked, st):
            s, dp = st
            _, _, lse, delta, ccol = ctx
            s = s + ccol - cr_ref[0, g, kj]
            if masked:
                s = jnp.where(blk.ci <= blk.ri, s, -jnp.inf)
            p = jnp.exp2(s - lse)
            ds = p * (dp - delta)
            return p.astype(BF16), ds.astype(BF16), jnp.sum(ds, axis=0, keepdims=True)

        def combine(kj, ctx, st):
            for g in range(0, HEADS_PER_STEP, 2):
                dv_acc[blk.rows(kj), _lanes(g)] += _dot(st[g][0], ctx[g][1], TN) + _dot(st[g + 1][0], ctx[g + 1][1], TN)
                dk_acc[blk.rows(kj), _lanes(g)] += _dot(st[g][1], ctx[g][0], TN) + _dot(st[g + 1][1], ctx[g + 1][0], TN)
            for g in range(HEADS_PER_STEP):
                dcs_ref[0, g, kj] += st[g][2]

        def queries(g, ctx, kj, masked, st):
            dq_scr[g] += _dot(st[1], blk.own(g, k_ref[blk.rows(kj), _lanes(g)]))
            return ()

        def finish(qi, ctx, carry):
            for g in range(0, HEADS_PER_STEP, 2):
                dqkv_ref[0, blk.rows(qi), _lanes(g)] = ((dq_scr[g] + dq_scr[g + 1]) * LN2).astype(BF16)

        def last(qi, ctx, carry):
            k0 = pl.multiple_of(qi * tq, tq)
            pieces = [(g, h) for g in range(HEADS_PER_STEP) for h in blk.halves()]
            mm = [(_dot(ctx[g][0][r0:r0 + r], k_ref[pl.ds(k0, nc), _lanes(g)], NT),
                   _dot(ctx[g][1][r0:r0 + r], v_ref[pl.ds(k0, nc), _lanes(g)], NT)) for g, (r0, r, nc, _, _) in pieces]
            soft = {}
            for (g, (r0, r, nc, ri, ci)), (s, dp) in zip(pieces, mm):
                _, _, lse, delta, ccol = ctx[g]
                s = s + ccol[r0:r0 + r] - cr_ref[0, g, qi][:, :nc]
                s = jnp.where(ci <= ri, s, -jnp.inf)
                p = jnp.exp2(s - lse[r0:r0 + r])
                ds = p * (dp - delta[r0:r0 + r])
                soft[g, r0] = (p.astype(BF16), ds.astype(BF16), jnp.sum(ds, axis=0, keepdims=True))
            for r0, r, nc, _, _ in blk.halves():
                for g in range(0, HEADS_PER_STEP, 2):
                    (p0, d0, _), (p1, d1, _) = soft[g, r0], soft[g + 1, r0]
                    q0, q1 = ctx[g][0][r0:r0 + r], ctx[g + 1][0][r0:r0 + r]
                    o0, o1 = ctx[g][1][r0:r0 + r], ctx[g + 1][1][r0:r0 + r]
                    dv_acc[pl.ds(k0, nc), _lanes(g)] += _dot(p0, o0, TN) + _dot(p1, o1, TN)
                    dk_acc[pl.ds(k0, nc), _lanes(g)] += _dot(d0, q0, TN) + _dot(d1, q1, TN)
                for g in range(HEADS_PER_STEP):
                    col = soft[g, r0][2]
                    if nc < tq:
                        col = jnp.concatenate([col, jnp.zeros((1, tq - nc), F32)], axis=1)
                    dcs_ref[0, g, qi] += col
            dq = {}
            for g, (r0, r, nc, _, _) in pieces:
                dq[g, r0] = dq_scr[g, pl.ds(r0, r)] + _dot(soft[g, r0][1], blk.own(g, k_ref[pl.ds(k0, nc), _lanes(g)]))
            for g in range(0, HEADS_PER_STEP, 2):
                for r0, r, _, _, _ in blk.halves():
                    rows = pl.ds(pl.multiple_of(qi * tq + r0, r), r)
                    dqkv_ref[0, rows, _lanes(g)] = ((dq[g, r0] + dq[g + 1, r0]) * LN2).astype(BF16)

        _causal_blocks(nq, prep, init, [scores, softmax_bwd, queries], finish, combine=combine, last=last)
        dqkv_ref[1] = (dk_acc[...] * LN2).astype(BF16)
        dqkv_ref[2] = dv_acc[...].astype(BF16)
        dqkv_ref[3] = dz_ref[...]

    W = _step_width()
    return _call(
        body, name=name, grid=(B, H // HEADS_PER_STEP),
        in_specs=[_cols(S), _cols(S, nb), _cols(S, 2 * nb), _cols(S), _cols(S), _cols(S), _cols(S),
                  pl.BlockSpec((1, S, LANES), lambda b, h: (b, 0, 0)), _row_spec(nq, tq)],
        out_specs=[pl.BlockSpec((4, S, W), lambda b, h: (0, b, h)), _row_spec(nq, tq)],
        out_shape=[jax.ShapeDtypeStruct((4, T, DI), BF16), jax.ShapeDtypeStruct((B, H, nq, 1, tq), F32)],
        scratch_shapes=[pltpu.VMEM((S, W), F32), pltpu.VMEM((S, W), F32), pltpu.VMEM((HEADS_PER_STEP, tq, GROUP), F32)],
        compiler_params=_params(),
    )(proj, proj, proj, do, dzg, o, stat, cumcol, cumrow)


def _log2_keep(z2):
    nz = -z2
    e = jnp.exp2(jnp.minimum(z2, nz))
    return jnp.minimum(nz, 0.0) - jnp.log2(1.0 + e), e


def _sb_fwd(proj, B, tq, name):
    T, DI = proj.shape[0], proj.shape[1] // 4
    S = T // B
    H = DI // HEAD_DIM
    nq = S // tq
    nb = DI // _step_width()

    def body(q_ref, k_ref, v_ref, o_ref, st_ref, acc_scr, c_scr):
        blk = _Block(tq)
        strict = blk.ci < blk.ri
        above = jnp.where(blk.ri > blk.ci, 1.0, 0.0).astype(BF16)

        def prep(g, qi):
            acc_scr[g] = jnp.zeros((tq, GROUP), F32)
            c_scr[g] = jnp.zeros((tq, 1), F32)
            return blk.own(g, q_ref[blk.rows(qi), _lanes(g)])

        def init():
            return ()

        def scores(g, q, kj, masked, st):
            return (_dot(q, k_ref[blk.rows(kj), _lanes(g)], NT),)

        def logs(g, q, kj, masked, st):
            (z,) = st
            lk, _ = _log2_keep(z)
            lb = z + lk
            if masked:
                lk = jnp.where(strict, lk, 0.0)
            c = c_scr[g]
            c_scr[g] = c + jnp.sum(lk, axis=1, keepdims=True)
            return (lb + c,) + _hi_lo(lk)

        def suffix(g, q, kj, masked, st):
            lbc, hi, lo = st
            return lbc, _dot(hi, above) + _dot(lo, above)

        def weights(g, q, kj, masked, st):
            lbc, after = st
            a = jnp.exp2(lbc + after)
            if masked:
                a = jnp.where(strict, a, 0.0)
            return (a.astype(BF16),)

        def values(g, q, kj, masked, st):
            acc_scr[g] += _dot(st[0], v_ref[blk.rows(kj), _lanes(g)])
            return ()

        def finish(qi, ctx, carry):
            for g in range(0, HEADS_PER_STEP, 2):
                o_ref[blk.rows(qi), _lanes(g)] = blk.pair(acc_scr[g], acc_scr[g + 1])
                st_ref[blk.rows(qi), _lanes(g)] = blk.pair(c_scr[g], c_scr[g + 1])

        _causal_blocks(nq, prep, init, [scores, logs, suffix, weights, values], finish, descending=True)

    out = jax.ShapeDtypeStruct((T, DI), F32)
    return _call(
        body, name=name, grid=(B, H // HEADS_PER_STEP), in_specs=[_cols(S), _cols(S, nb), _cols(S, 2 * nb)],
        out_specs=[_cols(S), _cols(S)], out_shape=[out, out],
        scratch_shapes=[pltpu.VMEM((HEADS_PER_STEP, tq, GROUP), F32), pltpu.VMEM((HEADS_PER_STEP, tq, 1), F32)],
        compiler_params=_params(),
    )(proj, proj, proj)


def _sb_bwd(proj, do, dzg, stat, B, tq, name):
    T, DI = do.shape
    S = T // B
    H = DI // HEAD_DIM
    nq = S // tq
    nb = DI // _step_width()

    def body(q_ref, k_ref, v_ref, do_ref, dz_ref, st_ref, dqkv_ref, dk_acc, dv_acc, dq_scr):
        blk = _Block(tq)
        strict = blk.ci < blk.ri
        upto = jnp.where(blk.ri <= blk.ci, 1.0, 0.0).astype(BF16)
        before = jnp.where(blk.ri < blk.ci, 1.0, 0.0).astype(BF16)
        dk_acc[...] = jnp.zeros_like(dk_acc)
        dv_acc[...] = jnp.zeros_like(dv_acc)

        def prep(g, qi):
            dq_scr[g] = jnp.zeros((tq, GROUP), F32)
            return (blk.own(g, q_ref[blk.rows(qi), _lanes(g)]), blk.own(g, do_ref[blk.rows(qi), _lanes(g)]),
                    blk.stat(g, st_ref[blk.rows(qi), _lanes(g)]))

        def init():
            return jnp.zeros((tq, 1), F32), jnp.zeros((tq, 1), F32)

        def scores(g, ctx, kj, masked, st):
            return st + (_dot(ctx[0], k_ref[blk.rows(kj), _lanes(g)], NT),
                         _dot(ctx[1], v_ref[blk.rows(kj), _lanes(g)], NT))

        def logs(g, ctx, kj, masked, st):
            cpre, pg, z, da = st
            lk, e = _log2_keep(z)
            inv = 1.0 / (1.0 + e)
            sig = jnp.where(z >= 0.0, inv, e * inv)
            lbt = (z + lk) + (ctx[2] - cpre)
            if masked:
                lk = jnp.where(strict, lk, 0.0)
            return (cpre + jnp.sum(lk, axis=1, keepdims=True), pg, da, lbt, sig) + _hi_lo(lk)

        def prefix(g, ctx, kj, masked, st):
            cpre, pg, da, lbt, sig, hi, lo = st
            return cpre, pg, da, lbt, sig, _dot(hi, upto) + _dot(lo, upto)

        def weights(g, ctx, kj, masked, st):
            cpre, pg, da, lbt, sig, pre = st
            a = jnp.exp2(lbt - pre)
            if masked:
                a = jnp.where(strict, a, 0.0)
            gr = da * a
            return cpre, pg, sig, a.astype(BF16), gr, gr.astype(BF16)

        def grad_prefix(g, ctx, kj, masked, st):
            cpre, pg, sig, ab, gr, gb = st
            return cpre, pg, sig, ab, gr, _dot(gb, before)

        def dlogits(g, ctx, kj, masked, st):
            cpre, pg, sig, ab, gr, pfx = st
            dz = gr - sig * (gr + (pfx + pg))
            if masked:
                dz = jnp.where(strict, dz, 0.0)
            return cpre, pg + jnp.sum(gr, axis=1, keepdims=True), ab, dz.astype(BF16)

        def combine(kj, ctx, st):
            for g in range(0, HEADS_PER_STEP, 2):
                dv_acc[blk.rows(kj), _lanes(g)] += _dot(st[g][2], ctx[g][1], TN) + _dot(st[g + 1][2], ctx[g + 1][1], TN)
                dk_acc[blk.rows(kj), _lanes(g)] += _dot(st[g][3], ctx[g][0], TN) + _dot(st[g + 1][3], ctx[g + 1][0], TN)

        def queries(g, ctx, kj, masked, st):
            cpre, pg, _, dzb = st
            dq_scr[g] += _dot(dzb, blk.own(g, k_ref[blk.rows(kj), _lanes(g)]))
            return cpre, pg

        def finish(qi, ctx, carry):
            for g in range(0, HEADS_PER_STEP, 2):
                dqkv_ref[0, blk.rows(qi), _lanes(g)] = ((dq_scr[g] + dq_scr[g + 1]) * LN2).astype(BF16)

        _causal_blocks(nq, prep, init, [scores, logs, prefix, weights, grad_prefix, dlogits, queries], finish,
                       combine=combine)
        dqkv_ref[1] = (dk_acc[...] * LN2).astype(BF16)
        dqkv_ref[2] = dv_acc[...].astype(BF16)
        dqkv_ref[3] = dz_ref[...]

    W = _step_width()
    return _call(
        body, name=name, grid=(B, H // HEADS_PER_STEP),
        in_specs=[_cols(S), _cols(S, nb), _cols(S, 2 * nb), _cols(S), _cols(S), _cols(S)],
        out_specs=pl.BlockSpec((4, S, W), lambda b, h: (0, b, h)),
        out_shape=jax.ShapeDtypeStruct((4, T, DI), BF16),
        scratch_shapes=[pltpu.VMEM((S, W), F32), pltpu.VMEM((S, W), F32), pltpu.VMEM((HEADS_PER_STEP, tq, GROUP), F32)],
        compiler_params=_params(),
    )(proj, proj, proj, do, dzg, stat)


def _row_tile(R, C, n_arrays):
    budget = 24 * 1024 * 1024 // (2 * n_arrays * 4 * max(C, LANES))
    return _tile(R, max(8, budget), 8)


def _ew_sum(parts, name, also_bf16=False):
    R, C = parts[0].shape
    tr = _row_tile(R, C, len(parts) + 2)
    n = len(parts)

    def body(*refs):
        acc = refs[0][...].astype(F32) + refs[1][...].astype(F32)
        for r in refs[2:n]:
            acc = acc + r[...].astype(F32)
        refs[n][...] = acc
        if also_bf16:
            refs[n + 1][...] = acc.astype(BF16)

    blk = pl.BlockSpec((tr, C), lambda i: (i, 0))
    out_shape = [jax.ShapeDtypeStruct((R, C), F32)] + ([jax.ShapeDtypeStruct((R, C), BF16)] if also_bf16 else [])
    return _call(
        body, name=name, grid=(R // tr,), in_specs=[blk] * n, out_specs=[blk] * len(out_shape),
        out_shape=out_shape, compiler_params=_params(),
    )(*parts)


def _adamw(w, g, m, v, name):
    R, C = w.shape
    tr = _row_tile(R, C, 7)
    c1 = 1.0 / (1.0 - ADAM_B1 ** ADAM_STEP)
    c2 = 1.0 / (1.0 - ADAM_B2 ** ADAM_STEP)

    def body(w_ref, g_ref, m_ref, v_ref, d_ref, m2_ref, v2_ref):
        gv = g_ref[...]
        m2 = ADAM_B1 * m_ref[...] + (1.0 - ADAM_B1) * gv
        v2 = ADAM_B2 * v_ref[...] + (1.0 - ADAM_B2) * (gv * gv)
        m2_ref[...] = m2
        v2_ref[...] = v2
        d_ref[...] = -ADAM_LR * ((m2 * c1) / (jnp.sqrt(v2 * c2) + ADAM_EPS) + ADAM_WD * w_ref[...])

    blk = pl.BlockSpec((tr, C), lambda i: (i, 0))
    out = jax.ShapeDtypeStruct((R, C), F32)
    return _call(
        body, name=name, grid=(R // tr,), in_specs=[blk] * 4, out_specs=[blk] * 3, out_shape=[out] * 3,
        compiler_params=_params(),
    )(w, g, m, v)


def _me():
    return lax.axis_index("x"), lax.axis_index("y"), lax.axis_index("c")


def _chip_of(x, y):
    return 2 * x + y


def _other_chips(x, y):
    return [(x, 1 - y), (1 - x, y), (1 - x, 1 - y)]


def _gather_steps(ins_h, outs_h, send1, recv1, send2, recv2):
    nh = len(ins_h)
    x, y, c = _me()
    mine = _chip_of(x, y)
    chips = _other_chips(x, y)
    sib = (x, y, 1 - c)

    def landed(i, k, half):
        return outs_h[i].at[_chip_of(*chips[k]), half]

    def first(i, k):
        return pltpu.make_async_remote_copy(
            src_ref=ins_h[i].at[c], dst_ref=outs_h[i].at[mine, c], send_sem=send1.at[i, k], recv_sem=recv1.at[i, k],
            device_id=(*chips[k], c), device_id_type=MESH)

    def passed(i, k):
        return pltpu.make_async_remote_copy(
            src_ref=landed(i, k, c), dst_ref=landed(i, k, c), send_sem=send2.at[i, k], recv_sem=recv2.at[i, k],
            device_id=sib, device_id_type=MESH)

    def start():
        for i in range(nh):
            for k in range(3):
                first(i, k).start()

    def finish():
        for i in range(nh):
            for k in range(3):
                pltpu.make_async_remote_copy(
                    src_ref=ins_h[i].at[c], dst_ref=landed(i, k, c), send_sem=send1.at[i, k], recv_sem=recv1.at[i, k],
                    device_id=(*chips[k], c), device_id_type=MESH).wait_recv()
                passed(i, k).start()
        for i in range(nh):
            for k in range(3):
                pltpu.make_async_remote_copy(
                    src_ref=landed(i, k, c), dst_ref=landed(i, k, 1 - c), send_sem=send2.at[i, k],
                    recv_sem=recv2.at[i, k], device_id=sib, device_id_type=MESH).wait_recv()
        for i in range(nh):
            for k in range(3):
                first(i, k).wait_send()
                passed(i, k).wait_send()

    return start, finish


def _gather_sems(nh):
    return [pltpu.SemaphoreType.DMA((nh, 3)) for _ in range(4)]


def _gather_weights(halves, smalls):
    nh, ns = len(halves), len(smalls)

    def body(*refs):
        ins_h, ins_s = refs[:nh], refs[nh:nh + ns]
        outs_h, outs_s = refs[nh + ns:2 * nh + ns], refs[2 * nh + ns:2 * (nh + ns)]
        send1, recv1, send2, recv2, send3, recv3 = refs[2 * (nh + ns):]
        x, y, c = _me()
        mine = _chip_of(x, y)
        chips = _other_chips(x, y)

        def small(i, k):
            return pltpu.make_async_remote_copy(
                src_ref=ins_s[i], dst_ref=outs_s[i].at[mine], send_sem=send3.at[i, k], recv_sem=recv3.at[i, k],
                device_id=(*chips[k], c), device_id_type=MESH)

        start, finish = _gather_steps(ins_h, outs_h, send1, recv1, send2, recv2)
        start()
        for i in range(ns):
            for k in range(3):
                small(i, k).start()
        finish()
        for i in range(ns):
            for k in range(3):
                pltpu.make_async_remote_copy(
                    src_ref=ins_s[i], dst_ref=outs_s[i].at[_chip_of(*chips[k])], send_sem=send3.at[i, k],
                    recv_sem=recv3.at[i, k], device_id=(*chips[k], c), device_id_type=MESH).wait_recv()
                small(i, k).wait_send()

    out_shape = ([jax.ShapeDtypeStruct((4,) + a.shape, a.dtype) for a in halves]
                 + [jax.ShapeDtypeStruct((4,) + a.shape, a.dtype) for a in smalls])
    n = nh + ns
    res = _call(
        body, name="gather_weights", in_specs=[HBM] * n, out_specs=[HBM] * n, out_shape=out_shape,
        scratch_shapes=_gather_sems(nh) + [pltpu.SemaphoreType.DMA((max(ns, 1), 3)),
                                           pltpu.SemaphoreType.DMA((max(ns, 1), 3))],
        compiler_params=_params(),
    )(*halves, *smalls)
    return res[:nh], res[nh:]


def _plan(ins, outs, sems, copies):
    def steps(in_refs, out_refs, *sem_refs):
        def start():
            for cp in copies(in_refs, out_refs, *sem_refs):
                cp.start()

        def finish():
            for cp in copies(in_refs, out_refs, *sem_refs):
                cp.wait()

        return start, finish

    return dict(ins=list(ins), outs=list(outs), sems=list(sems), steps=steps)


def _pair_exchange_plan(grads):
    n = len(grads)

    def copies(ins, got, send, recv):
        x, y, c = _me()
        return [pltpu.make_async_remote_copy(
            src_ref=ins[i].at[j, 1 - c], dst_ref=got[i].at[j], send_sem=send.at[i, j], recv_sem=recv.at[i, j],
            device_id=(x, y, 1 - c), device_id_type=MESH) for i in range(n) for j in range(4)]

    return _plan(grads, [jax.ShapeDtypeStruct((4,) + g.shape[2:], g.dtype) for g in grads],
                 [pltpu.SemaphoreType.DMA((n, 4)), pltpu.SemaphoreType.DMA((n, 4))], copies)


def _chip_exchange_plan(sums):
    n = len(sums)

    def copies(ins, got, send, recv):
        x, y, c = _me()
        chips = _other_chips(x, y)
        return [pltpu.make_async_remote_copy(
            src_ref=ins[i].at[_chip_of(*chips[k])], dst_ref=got[i].at[k], send_sem=send.at[i, k],
            recv_sem=recv.at[i, k], device_id=(*chips[k], c), device_id_type=MESH) for i in range(n) for k in range(3)]

    return _plan(sums, [jax.ShapeDtypeStruct((3,) + a.shape[1:], a.dtype) for a in sums],
                 [pltpu.SemaphoreType.DMA((n, 3)), pltpu.SemaphoreType.DMA((n, 3))], copies)


def _pair_share_plan(halves):
    n = len(halves)

    def copies(ins, outs, send, recv):
        x, y, c = _me()
        return [pltpu.make_async_remote_copy(
            src_ref=ins[i], dst_ref=outs[i], send_sem=send.at[i], recv_sem=recv.at[i],
            device_id=(x, y, 1 - c), device_id_type=MESH) for i in range(n)]

    return _plan(halves, [jax.ShapeDtypeStruct(h.shape, h.dtype) for h in halves],
                 [pltpu.SemaphoreType.DMA((n,)), pltpu.SemaphoreType.DMA((n,))], copies)


def _run_exchange(plan, name):
    ni, no = len(plan["ins"]), len(plan["outs"])

    def body(*refs):
        start, finish = plan["steps"](refs[:ni], refs[ni:ni + no], *refs[ni + no:])
        start()
        finish()

    return _call(
        body, name=name, in_specs=[HBM] * ni, out_specs=[HBM] * no, out_shape=plan["outs"],
        scratch_shapes=plan["sems"], compiler_params=_params(),
    )(*plan["ins"])


def _allreduce_small(vec):
    P = vec.shape[1]

    def body(v_ref, sum_ref, all_ref, send, recv):
        x, y, c = _me()
        me = 4 * x + 2 * y + c
        all_ref[pl.ds(me, 1)] = v_ref[...][None]
        cps = []
        for d in range(1, 8):
            peer = (jnp.bitwise_xor(x, d >> 2), jnp.bitwise_xor(y, (d >> 1) & 1), jnp.bitwise_xor(c, d & 1))
            r = pltpu.make_async_remote_copy(
                src_ref=v_ref, dst_ref=all_ref.at[me], send_sem=send.at[d - 1], recv_sem=recv.at[d - 1],
                device_id=peer, device_id_type=MESH)
            r.start()
            cps.append(r)
        for d in range(1, 8):
            src = jnp.bitwise_xor(me, d)
            pltpu.make_async_remote_copy(
                src_ref=v_ref, dst_ref=all_ref.at[src], send_sem=send.at[d - 1], recv_sem=recv.at[d - 1],
                device_id=(x, y, c), device_id_type=MESH).wait_recv()
        for r in cps:
            r.wait_send()
        acc = all_ref[0]
        for i in range(1, 8):
            acc = acc + all_ref[i]
        sum_ref[...] = acc

    vm = pl.BlockSpec(memory_space=pltpu.VMEM)
    return _call(
        body, name="allreduce_small", in_specs=[vm], out_specs=[vm, vm],
        out_shape=[jax.ShapeDtypeStruct((8, P), F32), jax.ShapeDtypeStruct((8, 8, P), F32)],
        scratch_shapes=[pltpu.SemaphoreType.DMA((7,)), pltpu.SemaphoreType.DMA((7,))],
        compiler_params=_params(),
    )(vec)[0]


def _per_batch(mod, B, D):
    return [mod[:B, i * D:(i + 1) * D].reshape(B, 1, D) for i in range(3)]


def _pad_rows8(a):
    return jnp.concatenate([a, jnp.zeros((8 - a.shape[0],) + a.shape[1:], a.dtype)], axis=0)


def _layer_fwd(x, c8, w, S, fox, tag, gather=()):
    T, D = x.shape
    B = T // S
    DI = w["w_out"].shape[0]
    H = DI // HEAD_DIM
    tq = _tile(S, FOX_BLOCK if fox else SB_BLOCK, 8)
    mod = _mod_fwd(c8, w["w_ada"], w["b_ada"], tag + "_mod_fwd")
    shift, scale, gate = _per_batch(mod, B, D)
    proj, h, gathered = _ln_proj(x, shift, scale, w["norm_g"], w["w_in"], S, tag + "_ln_proj", gather)
    saved = dict(x=x, h=h, proj=proj, scale=scale, gate=gate, gathered=gathered)
    if fox:
        fl = _mm(h, w["w_f"], "nn", F32, tag + "_flogit").reshape(B, S, LANES)
        cum = _cum_fwd(fl, w["b_f"], tag + "_cum_fwd")
        cumrow = cum[:, :, :H].transpose(0, 2, 1).reshape(B, H, S // tq, 1, tq)
        o, stat = _fox_fwd(proj, cum, cumrow, tag + "_attn_fwd")
        saved.update(fl=fl, cum=cum, cumrow=cumrow)
    else:
        o, stat = _sb_fwd(proj, B, tq, tag + "_attn_fwd")
    xo, y, u = _gate_out(o, proj, w["w_out"], x, gate, S, tag + "_gate_out")
    saved.update(o=o, stat=stat, y=y, u=u)
    return xo, saved


def _hosted(side, sent, call):
    if side is None:
        return call(None), None
    plan, _ = next(side) if sent is None else side.send(sent)
    return call(plan)


def _layer_bwd(dxo, sv, w, cT, S, fox, tag, side=None):
    T, D = dxo.shape
    B = T // S
    DI = w["w_out"].shape[0]
    H = DI // HEAD_DIM
    tq = _tile(S, FOX_BLOCK if fox else SB_BLOCK, 8)
    dy, do, dzg, dgate = _out_bwd(dxo, sv["y"], sv["gate"], w["w_out"], sv["o"], sv["proj"], S, tag + "_out_bwd")
    dw_out, landed = _hosted(side, None, lambda r: _mm(sv["u"], dy, "tn", F32, tag + "_dw_out", tm=1024, tn=1024,
                                                      tk=2048, rider=r))
    g = {"w_out": dw_out}
    q_cols = jnp.where(jnp.arange(4 * DI)[None, :] < DI, Q_SCALE, 1.0).astype(F32)
    if fox:
        dproj, dcs = _fox_bwd(sv["proj"], do, dzg, sv["o"], sv["stat"], sv["cum"], sv["cumrow"], tag + "_attn_bwd")
        dcs = dcs.reshape(B, H, S).transpose(0, 2, 1)
        dcs = jnp.concatenate([dcs, jnp.zeros((B, S, LANES - H), F32)], axis=-1)
        dfl, db_f = _cum_bwd(dcs, sv["fl"], w["b_f"], tag + "_cum_bwd")
        g["b_f"] = db_f[:, :H]
        dfl = dfl.reshape(T, LANES).astype(BF16)
    else:
        dproj = _sb_bwd(sv["proj"], do, dzg, sv["stat"], B, tq, tag + "_attn_bwd")
    g["w_in"], landed = _hosted(side, landed, lambda r: _mm(sv["h"], dproj, "tn", F32, tag + "_dw_in", tm=1024, tn=2048,
                                                            tk=1024, col_scale=q_cols, rider=r))
    dh, landed = _hosted(side, landed, lambda r: _mm(dproj, w["w_in"], "nt", F32, tag + "_dh", tm=2048, tn=1024,
                                                     tk=1024, rider=r))
    dhs = [dh]
    if side is not None:
        try:
            side.send(landed)
        except StopIteration as done:
            g["side"] = done.value
    if fox:
        dw_f = _mm(sv["h"], dfl, "tn", F32, tag + "_dw_f", tm=1024, tn=LANES, tk=2048)
        g["w_in"] = jnp.concatenate([g["w_in"], dw_f[:, :H]], axis=1)
        dhs.append(_mm(dfl, w["w_f"], "nt", F32, tag + "_dh_f", tm=2048, tn=1024, tk=LANES))
    dx, dshift, dscale, dg = _ln_bwd(dhs, sv["x"], dxo, sv["scale"], w["norm_g"], S, tag + "_ln_bwd")
    g["norm_g"] = dg
    dmod = jnp.concatenate([dshift, dscale, dgate], axis=-1).reshape(B, 3 * D)
    g["w_ada"], g["b_ada"] = _mod_bwd(cT, _pad_rows8(dmod), B, tag + "_mod_bwd")
    return dx, g


def _local_step(x3, c, tgt3, wf, ws, final_g, sb_halves=(), sb_side=None):
    B, S, D = x3.shape
    T = B * S
    x = x3.reshape(T, D)
    c8 = _pad_rows8(c)
    cT = c8.T
    x1, sv1 = _layer_fwd(x, c8, wf, S, True, "fox", sb_halves)
    if sb_halves:
        ws = ws(sv1["gathered"])
    x2, sv2 = _layer_fwd(x1, c8, ws, S, False, "sb")
    dx2, dgf, loss = _final_loss(x2, tgt3.reshape(T, D), final_g, S, "final_loss")
    dx1, gs = _layer_bwd(dx2, sv2, ws, cT, S, False, "sb")
    dx0, gf = _layer_bwd(dx1, sv1, wf, cT, S, True, "fox", None if sb_side is None else sb_side(gs))
    return loss, dx0.reshape(B, S, D), gf, gs, dgf


def _cols_to_shards(a):
    R, C4 = a.shape
    return a.reshape(R, 4, C4 // 4).transpose(1, 0, 2)


def _shards_to_cols(a):
    n, R, C = a.shape
    return a.transpose(1, 0, 2).reshape(R, n * C)


def kernel(x, c, fox_norm_g, fox_w_ada, fox_b_ada, fox_w_in, fox_b_f, fox_w_out, sb_norm_g, sb_w_ada, sb_b_ada, sb_w_in, sb_w_out, final_norm_g, loss_target, m_fox_norm_g, m_fox_w_ada, m_fox_b_ada, m_fox_w_in, m_fox_b_f, m_fox_w_out, m_sb_norm_g, m_sb_w_ada, m_sb_b_ada, m_sb_w_in, m_sb_w_out, m_final_norm_g, v_fox_norm_g, v_fox_w_ada, v_fox_b_ada, v_fox_w_in, v_fox_b_f, v_fox_w_out, v_sb_norm_g, v_sb_w_ada, v_sb_b_ada, v_sb_w_in, v_sb_w_out, v_final_norm_g):
    B, S, D = x.shape
    DI = 4 * fox_w_out.shape[1]
    H = DI // HEAD_DIM
    chip = _chip_of(lax.axis_index("x"), lax.axis_index("y"))

    big_names = ["fox_w_ada", "fox_w_in", "fox_w_out", "sb_w_ada", "sb_w_in", "sb_w_out"]
    big = dict(fox_w_ada=fox_w_ada[0], fox_w_in=fox_w_in[0], fox_w_out=fox_w_out[0],
               sb_w_ada=sb_w_ada[0], sb_w_in=sb_w_in[0], sb_w_out=sb_w_out[0])
    for n in ("fox_w_in", "sb_w_in"):
        width = big[n].shape[1]
        is_q = chip * width + jnp.arange(width)[None, :] < DI
        big[n] = big[n] * jnp.where(is_q, Q_SCALE, 1.0).astype(F32)
    halves = {n: big[n].astype(BF16).reshape(2, big[n].shape[0] // 2, big[n].shape[1]) for n in big_names}
    fox_names, sb_names = big_names[:3], big_names[3:]

    def assemble(names, gathered):
        full = {}
        for n, a in zip(names, gathered):
            a = lax.dynamic_update_index_in_dim(a, halves[n], chip, 0)
            a = a.reshape(4, a.shape[1] * a.shape[2], a.shape[3])
            full[n] = a.reshape(4 * a.shape[1], a.shape[2]) if n.endswith("w_out") else _shards_to_cols(a)
        return full

    gathered, gsmall = _gather_weights([halves[n] for n in fox_names], [sb_norm_g, sb_b_ada])
    gsmall = [lax.dynamic_update_index_in_dim(a, own, chip, 0) for a, own in zip(gsmall, [sb_norm_g, sb_b_ada])]
    full = assemble(fox_names, gathered)
    sb_norm_full = gsmall[0].reshape(1, D)
    sb_b_ada_full = gsmall[1].reshape(1, 3 * D)
    w_f = jnp.concatenate([full["fox_w_in"][:, 4 * DI:], jnp.zeros((D, LANES - H), BF16)], axis=1)
    b_f = jnp.concatenate([fox_b_f, jnp.zeros((1, LANES - H), F32)], axis=1)
    wf = dict(w_ada=full["fox_w_ada"], b_ada=fox_b_ada, norm_g=fox_norm_g, w_in=full["fox_w_in"][:, :4 * DI],
              w_f=w_f, b_f=b_f, w_out=full["fox_w_out"])

    def ws(gathered_sb):
        f = assemble(sb_names, gathered_sb)
        return dict(w_ada=f["sb_w_ada"], b_ada=sb_b_ada_full, norm_g=sb_norm_full, w_in=f["sb_w_in"], w_out=f["sb_w_out"])

    core = lax.axis_index("c")

    def reduction(names, part, tag):
        shard_major = []
        for n in names:
            a = part[n]
            a = a.reshape(4, a.shape[0] // 4, a.shape[1]) if n.endswith("w_out") else _cols_to_shards(a)
            shard_major.append(a.reshape(4, 2, a.shape[1] // 2, a.shape[2]))
        got = yield _pair_exchange_plan(shard_major), tag + "_grad_pair_exchange"
        pair_f32, pair_bf16 = [], []
        for n, g4, b in zip(names, shard_major, got):
            a = lax.dynamic_index_in_dim(g4, core, axis=1, keepdims=False)
            r, C = a.shape[1:]
            s32, s16 = _ew_sum([a.reshape(4 * r, C), b.reshape(4 * r, C)], n + "_pair_sum", also_bf16=True)
            pair_f32.append(s32.reshape(4, r, C))
            pair_bf16.append(s16.reshape(4, r, C))
        others = yield _chip_exchange_plan(pair_bf16), tag + "_grad_chip_exchange"
        reduced_halves = [_ew_sum([lax.dynamic_index_in_dim(a, chip, axis=0, keepdims=False), b[0], b[1], b[2]],
                                  n + "_chip_sum")[0] for n, a, b in zip(names, pair_f32, others)]
        theirs = yield _pair_share_plan(reduced_halves), tag + "_grad_pair_share"
        return {n: jnp.concatenate([jnp.where(core == 0, a, b), jnp.where(core == 0, b, a)], axis=0)
                for n, a, b in zip(names, reduced_halves, theirs)}

    def sb_side(gs):
        return reduction(sb_names, dict(sb_w_ada=gs["w_ada"], sb_w_in=gs["w_in"], sb_w_out=gs["w_out"]), "sb")

    loss, grad_x, gf, gs, dgf = _local_step(x, c, loss_target, wf, ws, final_norm_g.reshape(1, D),
                                            [halves[n] for n in sb_names], sb_side)
    grad_big = dict(gf["side"])
    fox_red = reduction(fox_names, dict(fox_w_ada=gf["w_ada"], fox_w_in=gf["w_in"], fox_w_out=gf["w_out"]), "fox")
    try:
        plan, name = next(fox_red)
        while True:
            plan, name = fox_red.send(_run_exchange(plan, name))
    except StopIteration as done:
        grad_big.update(done.value)

    pieces = [loss, gf["norm_g"], gf["b_ada"], jnp.concatenate([gf["b_f"], jnp.zeros((1, LANES - H), F32)], axis=1),
              gs["norm_g"], gs["b_ada"], dgf]
    vec = jnp.concatenate(pieces, axis=1)
    red = _allreduce_small(_pad_rows8(vec))[0:1]
    offs = [0]
    for p in pieces:
        offs.append(offs[-1] + p.shape[1])
    r_loss, r_fng, r_fba, r_fbf, r_sng, r_sba, r_fin = [red[:, offs[i]:offs[i + 1]] for i in range(7)]
    small_grads = dict(
        fox_norm_g=r_fng, fox_b_ada=r_fba, fox_b_f=r_fbf[:, :H],
        sb_norm_g=lax.dynamic_slice_in_dim(r_sng, chip * (D // 4), D // 4, axis=1),
        sb_b_ada=lax.dynamic_slice_in_dim(r_sba, chip * (3 * D // 4), 3 * D // 4, axis=1),
        final_norm_g=r_fin)

    weights = dict(fox_norm_g=fox_norm_g, fox_w_ada=fox_w_ada, fox_b_ada=fox_b_ada, fox_w_in=fox_w_in, fox_b_f=fox_b_f,
                   fox_w_out=fox_w_out, sb_norm_g=sb_norm_g, sb_w_ada=sb_w_ada, sb_b_ada=sb_b_ada, sb_w_in=sb_w_in,
                   sb_w_out=sb_w_out, final_norm_g=final_norm_g)
    ms = dict(fox_norm_g=m_fox_norm_g, fox_w_ada=m_fox_w_ada, fox_b_ada=m_fox_b_ada, fox_w_in=m_fox_w_in,
              fox_b_f=m_fox_b_f, fox_w_out=m_fox_w_out, sb_norm_g=m_sb_norm_g, sb_w_ada=m_sb_w_ada,
              sb_b_ada=m_sb_b_ada, sb_w_in=m_sb_w_in, sb_w_out=m_sb_w_out, final_norm_g=m_final_norm_g)
    vs = dict(fox_norm_g=v_fox_norm_g, fox_w_ada=v_fox_w_ada, fox_b_ada=v_fox_b_ada, fox_w_in=v_fox_w_in,
              fox_b_f=v_fox_b_f, fox_w_out=v_fox_w_out, sb_norm_g=v_sb_norm_g, sb_w_ada=v_sb_w_ada,
              sb_b_ada=v_sb_b_ada, sb_w_in=v_sb_w_in, sb_w_out=v_sb_w_out, final_norm_g=v_final_norm_g)
    order = ["fox_norm_g", "fox_w_ada", "fox_b_ada", "fox_w_in", "fox_b_f", "fox_w_out", "sb_norm_g", "sb_w_ada",
             "sb_b_ada", "sb_w_in", "sb_w_out", "final_norm_g"]
    grads, deltas, new_m, new_v = {}, {}, {}, {}
    for n in big_names:
        shp = weights[n].shape
        g2 = grad_big[n]
        d, m2, v2 = _adamw(weights[n][0], g2, ms[n][0], vs[n][0], n + "_adamw")
        grads[n], deltas[n], new_m[n], new_v[n] = g2.reshape(shp), d.reshape(shp), m2.reshape(shp), v2.reshape(shp)
    small_names = [n for n in order if n not in big_names]
    sizes = [small_grads[n].shape[1] for n in small_names]
    total = sum(sizes)
    padn = (-total) % LANES

    def pack(d):
        return jnp.concatenate([d[n].reshape(1, -1) for n in small_names] + [jnp.ones((1, padn), F32)], axis=1)

    sd, sm, sv_ = _adamw(pack(weights), pack(small_grads), pack(ms), pack(vs), "small_adamw")
    o = 0
    for n, sz in zip(small_names, sizes):
        shp = weights[n].shape
        grads[n] = small_grads[n].reshape(shp)
        deltas[n], new_m[n], new_v[n] = (t[:, o:o + sz].reshape(shp) for t in (sd, sm, sv_))
        o += sz
    return (r_loss[0, 0], grad_x, *[grads[n] for n in order], *[deltas[n] for n in order],
            *[new_m[n] for n in order], *[new_v[n] for n in order])
```

```python
import jax
import jax.numpy as jnp
from jax import lax
from jax.experimental import pallas as pl
from jax.experimental.pallas import tpu as pltpu

F32 = jnp.float32
BF16 = jnp.bfloat16
HEAD_DIM = 64
LOG2E = 1.4426950408889634
LN2 = 0.6931471805599453
Q_SCALE = HEAD_DIM ** -0.5 * LOG2E
LANES = 128
NORM_EPS = 1e-6
ADAM_LR = 0.001
ADAM_B1 = 0.9
ADAM_B2 = 0.999
ADAM_EPS = 1e-08
ADAM_WD = 0.01
ADAM_STEP = 10
VMEM_LIMIT = 56 * 1024 * 1024
SB_BLOCK = 256
FOX_BLOCK = 512
MESH = pl.DeviceIdType.MESH
HBM = pl.BlockSpec(memory_space=pltpu.HBM)
NT = (((1,), (1,)), ((), ()))
TN = (((0,), (0,)), ((), ()))


def _call(body, **kw):
    return pl.pallas_call(body, **kw)


def _params(**kw):
    return pltpu.CompilerParams(vmem_limit_bytes=VMEM_LIMIT, **kw)


def _tile(dim, pref, mult=128):
    if dim <= pref:
        return dim
    t = (pref // mult) * mult
    while t >= mult:
        if dim % t == 0:
            return t
        t -= mult
    return dim


def _sigmoid(x):
    return 1.0 / (1.0 + jnp.exp(-x))


def _split3(x):
    hi = x.astype(BF16)
    r = x - hi.astype(F32)
    mid = r.astype(BF16)
    lo = (r - mid.astype(F32)).astype(BF16)
    return hi, mid, lo


def _mm(a, b, mode, out_dtype, name, tm=512, tn=512, tk=512, col_scale=None, rider=None):
    a_slabs = a.shape[0] if a.ndim == 3 else 0
    b_slabs = b.shape[0] if b.ndim == 3 else 0
    if mode == "nn":
        (M, K), (_, N) = a.shape, b.shape
    elif mode == "nt":
        M, K = (a.shape[1], a_slabs * a.shape[2]) if a_slabs else a.shape
        N = b.shape[0]
    else:
        K, M = a.shape
        N = b_slabs * b.shape[2] if b_slabs else b.shape[1]
    tm, tn, tk = _tile(M, tm), _tile(N, tn), _tile(K, tk)
    if a_slabs:
        tk = _tile(a.shape[2], tk)
    if b_slabs:
        tn = _tile(b.shape[2], tn)
    nk = K // tk
    dims = {"nn": (((1,), (0,)), ((), ())), "nt": NT, "tn": TN}[mode]

    r_ins = rider["ins"] if rider else []
    r_outs = rider["outs"] if rider else []
    r_sems = rider["sems"] if rider else []
    nc = 0 if col_scale is None else 1
    ni, no = len(r_ins), len(r_outs)
    grid = (M // tm, N // tn, nk)

    def body(a_ref, b_ref, *rest):
        o_ref = rest[nc + ni]
        acc_ref = rest[nc + ni + 1 + no]
        k = pl.program_id(2)
        if rider:
            start, finish = rider["steps"](rest[nc:nc + ni], rest[nc + ni + 1:nc + ni + 1 + no], *rest[nc + ni + 2 + no:])
            at = [pl.program_id(d) for d in range(3)]
            pl.when(jnp.logical_and(jnp.logical_and(at[0] == 0, at[1] == 0), at[2] == 0))(start)

        @pl.when(k == 0)
        def _():
            acc_ref[...] = jnp.zeros_like(acc_ref)

        acc_ref[...] += lax.dot_general(a_ref[...], b_ref[...], dims, preferred_element_type=F32)

        @pl.when(k == nk - 1)
        def _():
            acc = acc_ref[...]
            if col_scale is not None:
                acc = acc * rest[0][...]
            o_ref[...] = acc.astype(out_dtype)

        if rider:
            pl.when(jnp.logical_and(jnp.logical_and(at[0] == grid[0] - 1, at[1] == grid[1] - 1), at[2] == nk - 1))(finish)

    if a_slabs:
        per = a.shape[2] // tk
        a_spec = pl.BlockSpec((None, tm, tk), lambda i, j, k: (k // per, i, k % per))
    elif mode == "tn":
        a_spec = pl.BlockSpec((tk, tm), lambda i, j, k: (k, i))
    else:
        a_spec = pl.BlockSpec((tm, tk), lambda i, j, k: (i, k))
    if b_slabs:
        per_b = b.shape[2] // tn
        b_spec = pl.BlockSpec((None, tk, tn), lambda i, j, k: (j // per_b, k, j % per_b))
    elif mode == "nt":
        b_spec = pl.BlockSpec((tn, tk), lambda i, j, k: (j, k))
    else:
        b_spec = pl.BlockSpec((tk, tn), lambda i, j, k: (k, j))
    extra_specs = [] if col_scale is None else [pl.BlockSpec((1, tn), lambda i, j, k: (0, j))]
    extra = [] if col_scale is None else [col_scale]
    res = _call(
        body, name=name, grid=grid,
        in_specs=[a_spec, b_spec] + extra_specs + [HBM] * ni,
        out_specs=[pl.BlockSpec((tm, tn), lambda i, j, k: (i, j))] + [HBM] * no,
        out_shape=[jax.ShapeDtypeStruct((M, N), out_dtype)] + list(r_outs),
        scratch_shapes=[pltpu.VMEM((tm, tn), F32)] + list(r_sems), compiler_params=_params(),
    )(a, b, *extra, *r_ins)
    return (res[0], res[1:]) if rider else res[0]


def _mod_fwd(c8, w_ada, b_ada, name):
    D, N = w_ada.shape
    tn = _tile(N, 512)

    def body(c_ref, w_ref, b_ref, o_ref):
        c = c_ref[...]
        sc = (c * _sigmoid(c)).astype(BF16)
        o_ref[...] = jnp.dot(sc, w_ref[...], preferred_element_type=F32) + b_ref[...]

    return _call(
        body, name=name, grid=(N // tn,),
        in_specs=[pl.BlockSpec((8, D), lambda j: (0, 0)), pl.BlockSpec((D, tn), lambda j: (0, j)),
                  pl.BlockSpec((1, tn), lambda j: (0, j))],
        out_specs=pl.BlockSpec((8, tn), lambda j: (0, j)),
        out_shape=jax.ShapeDtypeStruct((8, N), F32), compiler_params=_params(),
    )(c8, w_ada, b_ada)


def _mod_bwd(cT, dmod8, nb, name):
    D = cT.shape[0]
    N = dmod8.shape[1]
    tn = _tile(N, 512)

    def body(c_ref, d_ref, w_ref, b_ref):
        c = c_ref[...]
        sc = c * _sigmoid(c)
        d = d_ref[...]
        acc = sc[:, 0:1] * d[0:1, :]
        bsum = d[0:1, :]
        for b in range(1, nb):
            acc = acc + sc[:, b:b + 1] * d[b:b + 1, :]
            bsum = bsum + d[b:b + 1, :]
        w_ref[...] = acc
        b_ref[...] = bsum

    return _call(
        body, name=name, grid=(N // tn,),
        in_specs=[pl.BlockSpec((D, 8), lambda j: (0, 0)), pl.BlockSpec((8, tn), lambda j: (0, j))],
        out_specs=[pl.BlockSpec((D, tn), lambda j: (0, j)), pl.BlockSpec((1, tn), lambda j: (0, j))],
        out_shape=[jax.ShapeDtypeStruct((D, N), F32), jax.ShapeDtypeStruct((1, N), F32)],
        compiler_params=_params(),
    )(cT, dmod8)


def _ln_proj(x, shift, scale, g, w, S, name, gather=()):
    T, D = x.shape
    N = w.shape[1]
    tm = _tile(S, 2048)
    tn = _tile(N, 1024)
    per_b = S // tm
    ng = len(gather)
    n0, n1 = T // tm, N // tn

    def body(x_ref, sh_ref, sc_ref, g_ref, w_ref, *rest):
        ins_h, (p_ref, h_ref), outs_h, sems = rest[:ng], rest[ng:ng + 2], rest[ng + 2:2 * ng + 2], rest[2 * ng + 2:]
        i, j = pl.program_id(0), pl.program_id(1)
        if ng:
            start, finish = _gather_steps(ins_h, outs_h, *sems)
            pl.when(jnp.logical_and(i == 0, j == 0))(start)

        @pl.when(j == 0)
        def _():
            xv = x_ref[...]
            r = lax.rsqrt(jnp.mean(xv * xv, axis=-1, keepdims=True) + NORM_EPS)
            h = (xv * r) * g_ref[...] * (1.0 + sc_ref[0]) + sh_ref[0]
            h_ref[...] = h.astype(BF16)

        p_ref[...] = jnp.dot(h_ref[...], w_ref[...], preferred_element_type=F32).astype(BF16)
        if ng:
            pl.when(jnp.logical_and(i == n0 - 1, j == n1 - 1))(finish)

    res = _call(
        body, name=name, grid=(n0, n1),
        in_specs=[pl.BlockSpec((tm, D), lambda i, j: (i, 0)),
                  pl.BlockSpec((1, 1, D), lambda i, j: (i // per_b, 0, 0)),
                  pl.BlockSpec((1, 1, D), lambda i, j: (i // per_b, 0, 0)),
                  pl.BlockSpec((1, D), lambda i, j: (0, 0)),
                  pl.BlockSpec((D, tn), lambda i, j: (0, j))] + [HBM] * ng,
        out_specs=[pl.BlockSpec((tm, tn), lambda i, j: (i, j)), pl.BlockSpec((tm, D), lambda i, j: (i, 0))] + [HBM] * ng,
        out_shape=[jax.ShapeDtypeStruct((T, N), BF16), jax.ShapeDtypeStruct((T, D), BF16)]
        + [jax.ShapeDtypeStruct((4,) + a.shape, a.dtype) for a in gather],
        scratch_shapes=_gather_sems(ng) if ng else [], compiler_params=_params(),
    )(x, shift, scale, g, w, *gather)
    return res[0], res[1], res[2:]


def _ln_bwd(dhs, x, dxo, scale, g, S, name):
    T, D = x.shape
    B = T // S
    tm = _tile(S, 512)
    per_b = S // tm

    nd = len(dhs)

    def body(*refs):
        x_ref, dxo_ref, sc_ref, g_ref, dx_ref, dsh_ref, dsc_ref, dg_ref = refs[nd:]
        i = pl.program_id(0)
        xv = x_ref[...]
        dh_v = refs[0][...]
        for r in refs[1:nd]:
            dh_v = dh_v + r[...]
        r = lax.rsqrt(jnp.mean(xv * xv, axis=-1, keepdims=True) + NORM_EPS)
        xn = xv * r
        gv = g_ref[...]
        one_sc = 1.0 + sc_ref[0]
        dhxn = dh_v * xn

        @pl.when(i % per_b == 0)
        def _():
            dsh_ref[...] = jnp.zeros_like(dsh_ref)
            dsc_ref[...] = jnp.zeros_like(dsc_ref)

        @pl.when(i == 0)
        def _():
            dg_ref[...] = jnp.zeros_like(dg_ref)

        dsh_ref[0] += jnp.sum(dh_v, axis=0, keepdims=True)
        dsc_ref[0] += jnp.sum(dhxn, axis=0, keepdims=True) * gv
        dg_ref[...] += jnp.sum(dhxn, axis=0, keepdims=True) * one_sc
        dxn = dh_v * (gv * one_sc)
        dx_ref[...] = r * (dxn - xn * jnp.mean(dxn * xn, axis=-1, keepdims=True)) + dxo_ref[...]

    row = pl.BlockSpec((tm, D), lambda i: (i, 0))
    per = pl.BlockSpec((1, 1, D), lambda i: (i // per_b, 0, 0))
    vec = pl.BlockSpec((1, D), lambda i: (0, 0))
    return _call(
        body, name=name, grid=(T // tm,),
        in_specs=[row] * (nd + 2) + [per, vec], out_specs=[row, per, per, vec],
        out_shape=[jax.ShapeDtypeStruct((T, D), F32), jax.ShapeDtypeStruct((B, 1, D), F32),
                   jax.ShapeDtypeStruct((B, 1, D), F32), jax.ShapeDtypeStruct((1, D), F32)],
        compiler_params=_params(),
    )(*dhs, x, dxo, scale, g)


def _gate_out(o, proj, w_out, x, gate, S, name):
    T, DI = o.shape
    D = w_out.shape[1]
    tm = _tile(S, 256)
    per_b = S // tm

    def body(o_ref, z_ref, w_ref, x_ref, g_ref, xo_ref, y_ref, u_ref):
        z = z_ref[...].astype(F32)
        u = (o_ref[...] * (z * _sigmoid(z))).astype(BF16)
        u_ref[...] = u
        y = jnp.dot(u, w_ref[...], preferred_element_type=F32)
        y_ref[...] = y
        xo_ref[...] = x_ref[...] + g_ref[0] * y

    wide = pl.BlockSpec((tm, DI), lambda i: (i, 0))
    row = pl.BlockSpec((tm, D), lambda i: (i, 0))
    return _call(
        body, name=name, grid=(T // tm,),
        in_specs=[wide, pl.BlockSpec((tm, DI), lambda i: (i, 3)), pl.BlockSpec((DI, D), lambda i: (0, 0)), row,
                  pl.BlockSpec((1, 1, D), lambda i: (i // per_b, 0, 0))],
        out_specs=[row, row, wide],
        out_shape=[jax.ShapeDtypeStruct((T, D), F32), jax.ShapeDtypeStruct((T, D), F32),
                   jax.ShapeDtypeStruct((T, DI), BF16)],
        compiler_params=_params(),
    )(o, proj, w_out, x, gate)


def _out_bwd(dxo, y, gate, w_out, o, proj, S, name):
    T, D = dxo.shape
    DI = o.shape[1]
    B = T // S
    tm = _tile(S, 256)
    per_b = S // tm

    def body(dxo_ref, y_ref, g_ref, w_ref, o_ref, z_ref, dy_ref, do_ref, dz_ref, dg_ref):
        dxo_v = dxo_ref[...]
        dy = (dxo_v * g_ref[0]).astype(BF16)
        dy_ref[...] = dy
        du = lax.dot_general(dy, w_ref[...], NT, preferred_element_type=F32)
        z = z_ref[...].astype(F32)
        sg = _sigmoid(z)
        do_ref[...] = (du * (z * sg)).astype(BF16)
        dz_ref[...] = (du * o_ref[...] * (sg * (1.0 + z * (1.0 - sg)))).astype(BF16)

        @pl.when(pl.program_id(0) % per_b == 0)
        def _():
            dg_ref[...] = jnp.zeros_like(dg_ref)

        dg_ref[0] += jnp.sum(dxo_v * y_ref[...], axis=0, keepdims=True)

    wide = pl.BlockSpec((tm, DI), lambda i: (i, 0))
    row = pl.BlockSpec((tm, D), lambda i: (i, 0))
    per = pl.BlockSpec((1, 1, D), lambda i: (i // per_b, 0, 0))
    return _call(
        body, name=name, grid=(T // tm,),
        in_specs=[row, row, per, pl.BlockSpec((DI, D), lambda i: (0, 0)), wide,
                  pl.BlockSpec((tm, DI), lambda i: (i, 3))],
        out_specs=[row, wide, wide, per],
        out_shape=[jax.ShapeDtypeStruct((T, D), BF16), jax.ShapeDtypeStruct((T, DI), BF16),
                   jax.ShapeDtypeStruct((T, DI), BF16), jax.ShapeDtypeStruct((B, 1, D), F32)],
        compiler_params=_params(),
    )(dxo, y, gate, w_out, o, proj)


def _final_loss(x, tgt, g, S, name):
    T, D = x.shape
    tm = _tile(S, 512)

    def body(x_ref, t_ref, g_ref, dx_ref, dg_ref, l_ref):
        @pl.when(pl.program_id(0) == 0)
        def _():
            dg_ref[...] = jnp.zeros_like(dg_ref)
            l_ref[...] = jnp.zeros_like(l_ref)

        xv = x_ref[...]
        gv = g_ref[...]
        r = lax.rsqrt(jnp.mean(xv * xv, axis=-1, keepdims=True) + NORM_EPS)
        xn = xv * r
        e = xn * gv - t_ref[...]
        part = jnp.sum(jnp.sum(e * e, axis=0, keepdims=True), axis=1, keepdims=True)
        l_ref[...] += (0.5 / D) * part
        dy = e * (1.0 / D)
        dg_ref[...] += jnp.sum(dy * xn, axis=0, keepdims=True)
        dxn = dy * gv
        dx_ref[...] = r * (dxn - xn * jnp.mean(dxn * xn, axis=-1, keepdims=True))

    row = pl.BlockSpec((tm, D), lambda i: (i, 0))
    return _call(
        body, name=name, grid=(T // tm,),
        in_specs=[row, row, pl.BlockSpec((1, D), lambda i: (0, 0))],
        out_specs=[row, pl.BlockSpec((1, D), lambda i: (0, 0)), pl.BlockSpec((1, LANES), lambda i: (0, 0))],
        out_shape=[jax.ShapeDtypeStruct((T, D), F32), jax.ShapeDtypeStruct((1, D), F32),
                   jax.ShapeDtypeStruct((1, LANES), F32)],
        compiler_params=_params(),
    )(x, tgt, g)


def _cum_fwd(fl, bf, name):
    B, S, _ = fl.shape
    ch = _tile(S, 256, 8)

    def body(fl_ref, b_ref, cum_ref):
        ri = lax.broadcasted_iota(jnp.int32, (ch, ch), 0)
        ci = lax.broadcasted_iota(jnp.int32, (ch, ch), 1)
        tri = jnp.where(ri >= ci, 1.0, 0.0).astype(BF16)

        def step(i, carry):
            r0 = pl.multiple_of(i * ch, ch)
            z = fl_ref[0, pl.ds(r0, ch), :] + b_ref[...]
            lf = (jnp.minimum(z, 0.0) - jnp.log(1.0 + jnp.exp(-jnp.abs(z)))) * LOG2E
            hi, mid, lo = _split3(lf)
            cs = (jnp.dot(tri, hi, preferred_element_type=F32) + jnp.dot(tri, mid, preferred_element_type=F32)
                  + jnp.dot(tri, lo, preferred_element_type=F32)) + carry
            cum_ref[0, pl.ds(r0, ch), :] = cs
            return cs[ch - 1:ch, :]

        lax.fori_loop(0, S // ch, step, jnp.zeros((1, LANES), F32))

    blk = pl.BlockSpec((1, S, LANES), lambda b: (b, 0, 0))
    return _call(
        body, name=name, grid=(B,), in_specs=[blk, pl.BlockSpec((1, LANES), lambda b: (0, 0))], out_specs=blk,
        out_shape=jax.ShapeDtypeStruct((B, S, LANES), F32), compiler_params=_params(),
    )(fl, bf)


def _cum_bwd(dcs, fl, bf, name):
    B, S, _ = fl.shape
    ch = _tile(S, 256, 8)
    n = S // ch

    def body(d_ref, fl_ref, b_ref, o_ref, db_ref):
        ri = lax.broadcasted_iota(jnp.int32, (ch, ch), 0)
        ci = lax.broadcasted_iota(jnp.int32, (ch, ch), 1)
        tri = jnp.where(ci >= ri, 1.0, 0.0).astype(BF16)

        @pl.when(pl.program_id(0) == 0)
        def _():
            db_ref[...] = jnp.zeros_like(db_ref)

        def step(t, carry):
            tail, dbsum = carry
            r0 = pl.multiple_of((n - 1 - t) * ch, ch)
            hi, mid, lo = _split3(d_ref[0, pl.ds(r0, ch), :])
            suf = (jnp.dot(tri, hi, preferred_element_type=F32) + jnp.dot(tri, mid, preferred_element_type=F32)
                   + jnp.dot(tri, lo, preferred_element_type=F32)) + tail
            z = fl_ref[0, pl.ds(r0, ch), :] + b_ref[...]
            dfl = -suf * _sigmoid(-z)
            o_ref[0, pl.ds(r0, ch), :] = dfl
            return suf[0:1, :], dbsum + jnp.sum(dfl, axis=0, keepdims=True)

        z1 = jnp.zeros((1, LANES), F32)
        _, dbsum = lax.fori_loop(0, n, step, (z1, z1))
        db_ref[...] += dbsum

    blk = pl.BlockSpec((1, S, LANES), lambda b: (b, 0, 0))
    vec = pl.BlockSpec((1, LANES), lambda b: (0, 0))
    return _call(
        body, name=name, grid=(B,), in_specs=[blk, blk, vec], out_specs=[blk, vec],
        out_shape=[jax.ShapeDtypeStruct((B, S, LANES), F32), jax.ShapeDtypeStruct((1, LANES), F32)],
        compiler_params=_params(),
    )(dcs, fl, bf)


HEADS_PER_STEP = 4
GROUP = 2 * HEAD_DIM
DIAG_PIECES = 4


def _step_width():
    return HEAD_DIM * HEADS_PER_STEP


def _cols(S, offset_blocks=0):
    return pl.BlockSpec((S, _step_width()), lambda b, h: (b, offset_blocks + h))


def _row_spec(nq, tq):
    return pl.BlockSpec((1, HEADS_PER_STEP, nq, 1, tq), lambda b, h: (b, h, 0, 0, 0))


def _lanes(g):
    return slice(GROUP * (g // 2), GROUP * (g // 2) + GROUP)


def _hi_lo(x):
    hi = x.astype(BF16)
    return hi, (x - hi.astype(F32)).astype(BF16)


def _dot(a, b, dims=None):
    if dims is None:
        return jnp.dot(a, b, preferred_element_type=F32)
    return lax.dot_general(a, b, dims, preferred_element_type=F32)


def _causal_blocks(nq, prep, init, stages, finish, combine=None, descending=False, last=None):
    heads = range(HEADS_PER_STEP)

    def qloop(qi, _):
        ctx = [prep(g, qi) for g in heads]

        def step(kj, carry, masked):
            st = list(carry)
            for n, stage in enumerate(stages):
                if combine is not None and n == len(stages) - 1:
                    combine(kj, ctx, st)
                st = [stage(g, ctx[g], kj, masked, st[g]) for g in heads]
            return tuple(st)

        carry = tuple(init() for _ in heads)
        if descending:
            carry = step(qi, carry, True)
            carry = lax.fori_loop(0, qi, lambda t, cr: step(qi - 1 - t, cr, False), carry)
        else:
            carry = lax.fori_loop(0, qi, lambda kj, cr: step(kj, cr, False), carry)
            if last is not None:
                last(qi, ctx, carry)
                return 0
            carry = step(qi, carry, True)
        finish(qi, ctx, carry)
        return 0

    lax.fori_loop(0, nq, qloop, 0)


class _Block:
    def __init__(self, tq):
        self.tq = tq
        self.lane = lax.broadcasted_iota(jnp.int32, (tq, GROUP), 1)
        self.low = self.lane < HEAD_DIM
        self.ri = lax.broadcasted_iota(jnp.int32, (tq, tq), 0)
        self.ci = lax.broadcasted_iota(jnp.int32, (tq, tq), 1)

    def rows(self, i):
        return pl.ds(pl.multiple_of(i * self.tq, self.tq), self.tq)

    def own(self, g, x):
        low = self.low[:x.shape[0]]
        return jnp.where(low if g % 2 == 0 else jnp.logical_not(low), x, jnp.zeros_like(x))

    def pair(self, a, b):
        return jnp.where(self.low[:a.shape[0]], a, b)

    def halves(self):
        r = self.tq // DIAG_PIECES
        out = []
        for rr in range(DIAG_PIECES):
            nc = r * (rr + 1)
            out.append((rr * r, r, nc, lax.broadcasted_iota(jnp.int32, (r, nc), 0) + rr * r,
                        lax.broadcasted_iota(jnp.int32, (r, nc), 1)))
        return out

    def stat(self, g, x):
        return jnp.sum(jnp.where(self.lane == HEAD_DIM * (g % 2), x, 0.0), axis=1, keepdims=True)


def _fox_fwd(proj, cumcol, cumrow, name):
    T, DI = proj.shape[0], proj.shape[1] // 4
    B, H, nq, _, tq = cumrow.shape
    S = nq * tq
    nb = DI // _step_width()

    def body(q_ref, k_ref, v_ref, cc_ref, cr_ref, o_ref, st_ref, acc_scr):
        h0 = pl.program_id(1) * HEADS_PER_STEP
        blk = _Block(tq)

        def prep(g, qi):
            acc_scr[g] = jnp.zeros((tq, GROUP), F32)
            q = blk.own(g, q_ref[blk.rows(qi), _lanes(g)])
            ccol = jnp.sum(jnp.where(blk.lane == h0 + g, cc_ref[0, blk.rows(qi), :], 0.0), axis=1, keepdims=True)
            return q, ccol

        def init():
            return jnp.full((tq, 1), -jnp.inf, F32), jnp.zeros((tq, 1), F32)

        def scores(g, ctx, kj, masked, st):
            return st + (_dot(ctx[0], k_ref[blk.rows(kj), _lanes(g)], NT),)

        def softmax(g, ctx, kj, masked, st):
            m, l, s = st
            s = s + ctx[1] - cr_ref[0, g, kj]
            if masked:
                s = jnp.where(blk.ci <= blk.ri, s, -jnp.inf)
            m_new = jnp.maximum(m, jnp.max(s, axis=1, keepdims=True))
            alpha = jnp.exp2(m - m_new)
            p = jnp.exp2(s - m_new)
            return (m_new, alpha * l + jnp.sum(p, axis=1, keepdims=True), alpha) + _hi_lo(p)

        def values(g, ctx, kj, masked, st):
            m, l, alpha, hi, lo = st
            v = v_ref[blk.rows(kj), _lanes(g)]
            acc_scr[g] = alpha * acc_scr[g] + (_dot(hi, v) + _dot(lo, v))
            return m, l

        def finish(qi, ctx, carry):
            for g in range(0, HEADS_PER_STEP, 2):
                (m0, l0), (m1, l1) = carry[g], carry[g + 1]
                o_ref[blk.rows(qi), _lanes(g)] = blk.pair(acc_scr[g] / l0, acc_scr[g + 1] / l1)
                st_ref[blk.rows(qi), _lanes(g)] = blk.pair(m0 + jnp.log2(l0), m1 + jnp.log2(l1))

        def last(qi, ctx, carry):
            k0 = pl.multiple_of(qi * tq, tq)
            pieces = [(g, h) for g in range(HEADS_PER_STEP) for h in blk.halves()]
            s_all = [_dot(ctx[g][0][r0:r0 + r], k_ref[pl.ds(k0, nc), _lanes(g)], NT) for g, (r0, r, nc, _, _) in pieces]
            soft = []
            for (g, (r0, r, nc, ri, ci)), s in zip(pieces, s_all):
                m, l = carry[g][0][r0:r0 + r], carry[g][1][r0:r0 + r]
                s = s + ctx[g][1][r0:r0 + r] - cr_ref[0, g, qi][:, :nc]
                s = jnp.where(ci <= ri, s, -jnp.inf)
                m_new = jnp.maximum(m, jnp.max(s, axis=1, keepdims=True))
                alpha = jnp.exp2(m - m_new)
                p = jnp.exp2(s - m_new)
                soft.append((m_new, alpha * l + jnp.sum(p, axis=1, keepdims=True), alpha) + _hi_lo(p))
            outs = {}
            for (g, (r0, r, nc, _, _)), (m, l, alpha, hi, lo) in zip(pieces, soft):
                v = v_ref[pl.ds(k0, nc), _lanes(g)]
                acc = alpha * acc_scr[g, pl.ds(r0, r)] + (_dot(hi, v) + _dot(lo, v))
                outs[g, r0] = (acc / l, m + jnp.log2(l))
            for g in range(0, HEADS_PER_STEP, 2):
                for r0, r, _, _, _ in blk.halves():
                    rows = pl.ds(pl.multiple_of(qi * tq + r0, r), r)
                    o_ref[rows, _lanes(g)] = blk.pair(outs[g, r0][0], outs[g + 1, r0][0])
                    st_ref[rows, _lanes(g)] = blk.pair(outs[g, r0][1], outs[g + 1, r0][1])

        _causal_blocks(nq, prep, init, [scores, softmax, values], finish, last=last)

    out = jax.ShapeDtypeStruct((T, DI), F32)
    return _call(
        body, name=name, grid=(B, H // HEADS_PER_STEP),
        in_specs=[_cols(S), _cols(S, nb), _cols(S, 2 * nb), pl.BlockSpec((1, S, LANES), lambda b, h: (b, 0, 0)),
                  _row_spec(nq, tq)],
        out_specs=[_cols(S), _cols(S)], out_shape=[out, out],
        scratch_shapes=[pltpu.VMEM((HEADS_PER_STEP, tq, GROUP), F32)], compiler_params=_params(),
    )(proj, proj, proj, cumcol, cumrow)


def _fox_bwd(proj, do, dzg, o, stat, cumcol, cumrow, name):
    T, DI = do.shape
    B, H, nq, _, tq = cumrow.shape
    S = nq * tq
    nb = DI // _step_width()

    def body(q_ref, k_ref, v_ref, do_ref, dz_ref, o_ref, st_ref, cc_ref, cr_ref, dqkv_ref, dcs_ref, dk_acc, dv_acc,
             dq_scr):
        h0 = pl.program_id(1) * HEADS_PER_STEP
        blk = _Block(tq)
        dk_acc[...] = jnp.zeros_like(dk_acc)
        dv_acc[...] = jnp.zeros_like(dv_acc)
        dcs_ref[...] = jnp.zeros_like(dcs_ref)

        def prep(g, qi):
            dq_scr[g] = jnp.zeros((tq, GROUP), F32)
            q = blk.own(g, q_ref[blk.rows(qi), _lanes(g)])
            dout = blk.own(g, do_ref[blk.rows(qi), _lanes(g)])
            delta = jnp.sum(o_ref[blk.rows(qi), _lanes(g)] * dout.astype(F32), axis=1, keepdims=True)
            lse = blk.stat(g, st_ref[blk.rows(qi), _lanes(g)])
            ccol = jnp.sum(jnp.where(blk.lane == h0 + g, cc_ref[0, blk.rows(qi), :], 0.0), axis=1, keepdims=True)
            return q, dout, lse, delta, ccol

        def init():
            return ()

        def scores(g, ctx, kj, masked, st):
            return (_dot(ctx[0], k_ref[blk.rows(kj), _lanes(g)], NT), _dot(ctx[1], v_ref[blk.rows(kj), _lanes(g)], NT))

        def softmax_bwd(g, ctx, kj, masked, st):
            s, dp = st
            _, _, lse, delta, ccol = ctx
            s = s + ccol - cr_ref[0, g, kj]
            if masked:
                s = jnp.where(blk.ci <= blk.ri, s, -jnp.inf)
            p = jnp.exp2(s - lse)
            ds = p * (dp - delta)
            return p.astype(BF16), ds.astype(BF16), jnp.sum(ds, axis=0, keepdims=True)

        def combine(kj, ctx, st):
            for g in range(0, HEADS_PER_STEP, 2):
                dv_acc[blk.rows(kj), _lanes(g)] += _dot(st[g][0], ctx[g][1], TN) + _dot(st[g + 1][0], ctx[g + 1][1], TN)
                dk_acc[blk.rows(kj), _lanes(g)] += _dot(st[g][1], ctx[g][0], TN) + _dot(st[g + 1][1], ctx[g + 1][0], TN)
            for g in range(HEADS_PER_STEP):
                dcs_ref[0, g, kj] += st[g][2]

        def queries(g, ctx, kj, masked, st):
            dq_scr[g] += _dot(st[1], blk.own(g, k_ref[blk.rows(kj), _lanes(g)]))
            return ()

        def finish(qi, ctx, carry):
            for g in range(0, HEADS_PER_STEP, 2):
                dqkv_ref[0, blk.rows(qi), _lanes(g)] = ((dq_scr[g] + dq_scr[g + 1]) * LN2).astype(BF16)

        def last(qi, ctx, carry):
            k0 = pl.multiple_of(qi * tq, tq)
            pieces = [(g, h) for g in range(HEADS_PER_STEP) for h in blk.halves()]
            mm = [(_dot(ctx[g][0][r0:r0 + r], k_ref[pl.ds(k0, nc), _lanes(g)], NT),
                   _dot(ctx[g][1][r0:r0 + r], v_ref[pl.ds(k0, nc), _lanes(g)], NT)) for g, (r0, r, nc, _, _) in pieces]
            soft = {}
            for (g, (r0, r, nc, ri, ci)), (s, dp) in zip(pieces, mm):
                _, _, lse, delta, ccol = ctx[g]
                s = s + ccol[r0:r0 + r] - cr_ref[0, g, qi][:, :nc]
                s = jnp.where(ci <= ri, s, -jnp.inf)
                p = jnp.exp2(s - lse[r0:r0 + r])
                ds = p * (dp - delta[r0:r0 + r])
                soft[g, r0] = (p.astype(BF16), ds.astype(BF16), jnp.sum(ds, axis=0, keepdims=True))
            for r0, r, nc, _, _ in blk.halves():
                for g in range(0, HEADS_PER_STEP, 2):
                    (p0, d0, _), (p1, d1, _) = soft[g, r0], soft[g + 1, r0]
                    q0, q1 = ctx[g][0][r0:r0 + r], ctx[g + 1][0][r0:r0 + r]
                    o0, o1 = ctx[g][1][r0:r0 + r], ctx[g + 1][1][r0:r0 + r]
                    dv_acc[pl.ds(k0, nc), _lanes(g)] += _dot(p0, o0, TN) + _dot(p1, o1, TN)
                    dk_acc[pl.ds(k0, nc), _lanes(g)] += _dot(d0, q0, TN) + _dot(d1, q1, TN)
                for g in range(HEADS_PER_STEP):
                    col = soft[g, r0][2]
                    if nc < tq:
                        col = jnp.concatenate([col, jnp.zeros((1, tq - nc), F32)], axis=1)
                    dcs_ref[0, g, qi] += col
            dq = {}
            for g, (r0, r, nc, _, _) in pieces:
                dq[g, r0] = dq_scr[g, pl.ds(r0, r)] + _dot(soft[g, r0][1], blk.own(g, k_ref[pl.ds(k0, nc), _lanes(g)]))
            for g in range(0, HEADS_PER_STEP, 2):
                for r0, r, _, _, _ in blk.halves():
                    rows = pl.ds(pl.multiple_of(qi * tq + r0, r), r)
                    dqkv_ref[0, rows, _lanes(g)] = ((dq[g, r0] + dq[g + 1, r0]) * LN2).astype(BF16)

        _causal_blocks(nq, prep, init, [scores, softmax_bwd, queries], finish, combine=combine, last=last)
        dqkv_ref[1] = (dk_acc[...] * LN2).astype(BF16)
        dqkv_ref[2] = dv_acc[...].astype(BF16)
        dqkv_ref[3] = dz_ref[...]

    W = _step_width()
    return _call(
        body, name=name, grid=(B, H // HEADS_PER_STEP),
        in_specs=[_cols(S), _cols(S, nb), _cols(S, 2 * nb), _cols(S), _cols(S), _cols(S), _cols(S),
                  pl.BlockSpec((1, S, LANES), lambda b, h: (b, 0, 0)), _row_spec(nq, tq)],
        out_specs=[pl.BlockSpec((4, S, W), lambda b, h: (0, b, h)), _row_spec(nq, tq)],
        out_shape=[jax.ShapeDtypeStruct((4, T, DI), BF16), jax.ShapeDtypeStruct((B, H, nq, 1, tq), F32)],
        scratch_shapes=[pltpu.VMEM((S, W), F32), pltpu.VMEM((S, W), F32), pltpu.VMEM((HEADS_PER_STEP, tq, GROUP), F32)],
        compiler_params=_params(),
    )(proj, proj, proj, do, dzg, o, stat, cumcol, cumrow)


def _log2_keep(z2):
    nz = -z2
    e = jnp.exp2(jnp.minimum(z2, nz))
    return jnp.minimum(nz, 0.0) - jnp.log2(1.0 + e), e


def _sb_fwd(proj, B, tq, name):
    T, DI = proj.shape[0], proj.shape[1] // 4
    S = T // B
    H = DI // HEAD_DIM
    nq = S // tq
    nb = DI // _step_width()

    def body(q_ref, k_ref, v_ref, o_ref, st_ref, acc_scr, c_scr):
        blk = _Block(tq)
        strict = blk.ci < blk.ri
        above = jnp.where(blk.ri > blk.ci, 1.0, 0.0).astype(BF16)

        def prep(g, qi):
            acc_scr[g] = jnp.zeros((tq, GROUP), F32)
            c_scr[g] = jnp.zeros((tq, 1), F32)
            return blk.own(g, q_ref[blk.rows(qi), _lanes(g)])

        def init():
            return ()

        def scores(g, q, kj, masked, st):
            return (_dot(q, k_ref[blk.rows(kj), _lanes(g)], NT),)

        def logs(g, q, kj, masked, st):
            (z,) = st
            lk, _ = _log2_keep(z)
            lb = z + lk
            if masked:
                lk = jnp.where(strict, lk, 0.0)
            c = c_scr[g]
            c_scr[g] = c + jnp.sum(lk, axis=1, keepdims=True)
            return (lb + c,) + _hi_lo(lk)

        def suffix(g, q, kj, masked, st):
            lbc, hi, lo = st
            return lbc, _dot(hi, above) + _dot(lo, above)

        def weights(g, q, kj, masked, st):
            lbc, after = st
            a = jnp.exp2(lbc + after)
            if masked:
                a = jnp.where(strict, a, 0.0)
            return (a.astype(BF16),)

        def values(g, q, kj, masked, st):
            acc_scr[g] += _dot(st[0], v_ref[blk.rows(kj), _lanes(g)])
            return ()

        def finish(qi, ctx, carry):
            for g in range(0, HEADS_PER_STEP, 2):
                o_ref[blk.rows(qi), _lanes(g)] = blk.pair(acc_scr[g], acc_scr[g + 1])
                st_ref[blk.rows(qi), _lanes(g)] = blk.pair(c_scr[g], c_scr[g + 1])

        _causal_blocks(nq, prep, init, [scores, logs, suffix, weights, values], finish, descending=True)

    out = jax.ShapeDtypeStruct((T, DI), F32)
    return _call(
        body, name=name, grid=(B, H // HEADS_PER_STEP), in_specs=[_cols(S), _cols(S, nb), _cols(S, 2 * nb)],
        out_specs=[_cols(S), _cols(S)], out_shape=[out, out],
        scratch_shapes=[pltpu.VMEM((HEADS_PER_STEP, tq, GROUP), F32), pltpu.VMEM((HEADS_PER_STEP, tq, 1), F32)],
        compiler_params=_params(),
    )(proj, proj, proj)


def _sb_bwd(proj, do, dzg, stat, B, tq, name):
    T, DI = do.shape
    S = T // B
    H = DI // HEAD_DIM
    nq = S // tq
    nb = DI // _step_width()

    def body(q_ref, k_ref, v_ref, do_ref, dz_ref, st_ref, dqkv_ref, dk_acc, dv_acc, dq_scr):
        blk = _Block(tq)
        strict = blk.ci < blk.ri
        upto = jnp.where(blk.ri <= blk.ci, 1.0, 0.0).astype(BF16)
        before = jnp.where(blk.ri < blk.ci, 1.0, 0.0).astype(BF16)
        dk_acc[...] = jnp.zeros_like(dk_acc)
        dv_acc[...] = jnp.zeros_like(dv_acc)

        def prep(g, qi):
            dq_scr[g] = jnp.zeros((tq, GROUP), F32)
            return (blk.own(g, q_ref[blk.rows(qi), _lanes(g)]), blk.own(g, do_ref[blk.rows(qi), _lanes(g)]),
                    blk.stat(g, st_ref[blk.rows(qi), _lanes(g)]))

        def init():
            return jnp.zeros((tq, 1), F32), jnp.zeros((tq, 1), F32)

        def scores(g, ctx, kj, masked, st):
            return st + (_dot(ctx[0], k_ref[blk.rows(kj), _lanes(g)], NT),
                         _dot(ctx[1], v_ref[blk.rows(kj), _lanes(g)], NT))

        def logs(g, ctx, kj, masked, st):
            cpre, pg, z, da = st
            lk, _ = _log2_keep(z)
            sig = 1.0 - jnp.exp2(lk)
            lbt = (z + lk) + (ctx[2] - cpre)
            if masked:
                lk = jnp.where(strict, lk, 0.0)
            return (cpre + jnp.sum(lk, axis=1, keepdims=True), pg, da, lbt, sig) + _hi_lo(lk)

        def prefix(g, ctx, kj, masked, st):
            cpre, pg, da, lbt, sig, hi, lo = st
            return cpre, pg, da, lbt, sig, _dot(hi, upto) + _dot(lo, upto)

        def weights(g, ctx, kj, masked, st):
            cpre, pg, da, lbt, sig, pre = st
            a = jnp.exp2(lbt - pre)
            if masked:
                a = jnp.where(strict, a, 0.0)
            gr = da * a
            return cpre, pg, sig, a.astype(BF16), gr, gr.astype(BF16)

        def grad_prefix(g, ctx, kj, masked, st):
            cpre, pg, sig, ab, gr, gb = st
            return cpre, pg, sig, ab, gr, _dot(gb, before)

        def dlogits(g, ctx, kj, masked, st):
            cpre, pg, sig, ab, gr, pfx = st
            dz = gr - sig * (gr + (pfx + pg))
            if masked:
                dz = jnp.where(strict, dz, 0.0)
            return cpre, pg + jnp.sum(gr, axis=1, keepdims=True), ab, dz.astype(BF16)

        def combine(kj, ctx, st):
            for g in range(0, HEADS_PER_STEP, 2):
                dv_acc[blk.rows(kj), _lanes(g)] += _dot(st[g][2], ctx[g][1], TN) + _dot(st[g + 1][2], ctx[g + 1][1], TN)
                dk_acc[blk.rows(kj), _lanes(g)] += _dot(st[g][3], ctx[g][0], TN) + _dot(st[g + 1][3], ctx[g + 1][0], TN)

        def queries(g, ctx, kj, masked, st):
            cpre, pg, _, dzb = st
            dq_scr[g] += _dot(dzb, blk.own(g, k_ref[blk.rows(kj), _lanes(g)]))
            return cpre, pg

        def finish(qi, ctx, carry):
            for g in range(0, HEADS_PER_STEP, 2):
                dqkv_ref[0, blk.rows(qi), _lanes(g)] = ((dq_scr[g] + dq_scr[g + 1]) * LN2).astype(BF16)

        _causal_blocks(nq, prep, init, [scores, logs, prefix, weights, grad_prefix, dlogits, queries], finish,
                       combine=combine)
        dqkv_ref[1] = (dk_acc[...] * LN2).astype(BF16)
        dqkv_ref[2] = dv_acc[...].astype(BF16)
        dqkv_ref[3] = dz_ref[...]

    W = _step_width()
    return _call(
        body, name=name, grid=(B, H // HEADS_PER_STEP),
        in_specs=[_cols(S), _cols(S, nb), _cols(S, 2 * nb), _cols(S), _cols(S), _cols(S)],
        out_specs=pl.BlockSpec((4, S, W), lambda b, h: (0, b, h)),
        out_shape=jax.ShapeDtypeStruct((4, T, DI), BF16),
        scratch_shapes=[pltpu.VMEM((S, W), F32), pltpu.VMEM((S, W), F32), pltpu.VMEM((HEADS_PER_STEP, tq, GROUP), F32)],
        compiler_params=_params(),
    )(proj, proj, proj, do, dzg, stat)


def _row_tile(R, C, n_arrays):
    budget = 24 * 1024 * 1024 // (2 * n_arrays * 4 * max(C, LANES))
    return _tile(R, max(8, budget), 8)


def _ew_sum(parts, name, also_bf16=False):
    R, C = parts[0].shape
    tr = _row_tile(R, C, len(parts) + 2)
    n = len(parts)

    def body(*refs):
        acc = refs[0][...].astype(F32) + refs[1][...].astype(F32)
        for r in refs[2:n]:
            acc = acc + r[...].astype(F32)
        refs[n][...] = acc
        if also_bf16:
            refs[n + 1][...] = acc.astype(BF16)

    blk = pl.BlockSpec((tr, C), lambda i: (i, 0))
    out_shape = [jax.ShapeDtypeStruct((R, C), F32)] + ([jax.ShapeDtypeStruct((R, C), BF16)] if also_bf16 else [])
    return _call(
        body, name=name, grid=(R // tr,), in_specs=[blk] * n, out_specs=[blk] * len(out_shape),
        out_shape=out_shape, compiler_params=_params(),
    )(*parts)


def _adamw(w, g, m, v, name):
    R, C = w.shape
    tr = _row_tile(R, C, 7)
    c1 = 1.0 / (1.0 - ADAM_B1 ** ADAM_STEP)
    c2 = 1.0 / (1.0 - ADAM_B2 ** ADAM_STEP)

    def body(w_ref, g_ref, m_ref, v_ref, d_ref, m2_ref, v2_ref):
        gv = g_ref[...]
        m2 = ADAM_B1 * m_ref[...] + (1.0 - ADAM_B1) * gv
        v2 = ADAM_B2 * v_ref[...] + (1.0 - ADAM_B2) * (gv * gv)
        m2_ref[...] = m2
        v2_ref[...] = v2
        d_ref[...] = -ADAM_LR * ((m2 * c1) / (jnp.sqrt(v2 * c2) + ADAM_EPS) + ADAM_WD * w_ref[...])

    blk = pl.BlockSpec((tr, C), lambda i: (i, 0))
    out = jax.ShapeDtypeStruct((R, C), F32)
    return _call(
        body, name=name, grid=(R // tr,), in_specs=[blk] * 4, out_specs=[blk] * 3, out_shape=[out] * 3,
        compiler_params=_params(),
    )(w, g, m, v)


def _me():
    return lax.axis_index("x"), lax.axis_index("y"), lax.axis_index("c")


def _chip_of(x, y):
    return 2 * x + y


def _other_chips(x, y):
    return [(x, 1 - y), (1 - x, y), (1 - x, 1 - y)]


def _gather_steps(ins_h, outs_h, send1, recv1, send2, recv2):
    nh = len(ins_h)
    x, y, c = _me()
    mine = _chip_of(x, y)
    chips = _other_chips(x, y)
    sib = (x, y, 1 - c)

    def landed(i, k, half):
        return outs_h[i].at[_chip_of(*chips[k]), half]

    def first(i, k):
        return pltpu.make_async_remote_copy(
            src_ref=ins_h[i].at[c], dst_ref=outs_h[i].at[mine, c], send_sem=send1.at[i, k], recv_sem=recv1.at[i, k],
            device_id=(*chips[k], c), device_id_type=MESH)

    def passed(i, k):
        return pltpu.make_async_remote_copy(
            src_ref=landed(i, k, c), dst_ref=landed(i, k, c), send_sem=send2.at[i, k], recv_sem=recv2.at[i, k],
            device_id=sib, device_id_type=MESH)

    def start():
        for i in range(nh):
            for k in range(3):
                first(i, k).start()

    def finish():
        for i in range(nh):
            for k in range(3):
                pltpu.make_async_remote_copy(
                    src_ref=ins_h[i].at[c], dst_ref=landed(i, k, c), send_sem=send1.at[i, k], recv_sem=recv1.at[i, k],
                    device_id=(*chips[k], c), device_id_type=MESH).wait_recv()
                passed(i, k).start()
        for i in range(nh):
            for k in range(3):
                pltpu.make_async_remote_copy(
                    src_ref=landed(i, k, c), dst_ref=landed(i, k, 1 - c), send_sem=send2.at[i, k],
                    recv_sem=recv2.at[i, k], device_id=sib, device_id_type=MESH).wait_recv()
        for i in range(nh):
            for k in range(3):
                first(i, k).wait_send()
                passed(i, k).wait_send()

    return start, finish


def _gather_sems(nh):
    return [pltpu.SemaphoreType.DMA((nh, 3)) for _ in range(4)]


def _gather_weights(halves, smalls):
    nh, ns = len(halves), len(smalls)

    def body(*refs):
        ins_h, ins_s = refs[:nh], refs[nh:nh + ns]
        outs_h, outs_s = refs[nh + ns:2 * nh + ns], refs[2 * nh + ns:2 * (nh + ns)]
        send1, recv1, send2, recv2, send3, recv3 = refs[2 * (nh + ns):]
        x, y, c = _me()
        mine = _chip_of(x, y)
        chips = _other_chips(x, y)

        def small(i, k):
            return pltpu.make_async_remote_copy(
                src_ref=ins_s[i], dst_ref=outs_s[i].at[mine], send_sem=send3.at[i, k], recv_sem=recv3.at[i, k],
                device_id=(*chips[k], c), device_id_type=MESH)

        start, finish = _gather_steps(ins_h, outs_h, send1, recv1, send2, recv2)
        start()
        for i in range(ns):
            for k in range(3):
                small(i, k).start()
        finish()
        for i in range(ns):
            for k in range(3):
                pltpu.make_async_remote_copy(
                    src_ref=ins_s[i], dst_ref=outs_s[i].at[_chip_of(*chips[k])], send_sem=send3.at[i, k],
                    recv_sem=recv3.at[i, k], device_id=(*chips[k], c), device_id_type=MESH).wait_recv()
                small(i, k).wait_send()

    out_shape = ([jax.ShapeDtypeStruct((4,) + a.shape, a.dtype) for a in halves]
                 + [jax.ShapeDtypeStruct((4,) + a.shape, a.dtype) for a in smalls])
    n = nh + ns
    res = _call(
        body, name="gather_weights", in_specs=[HBM] * n, out_specs=[HBM] * n, out_shape=out_shape,
        scratch_shapes=_gather_sems(nh) + [pltpu.SemaphoreType.DMA((max(ns, 1), 3)),
                                           pltpu.SemaphoreType.DMA((max(ns, 1), 3))],
        compiler_params=_params(),
    )(*halves, *smalls)
    return res[:nh], res[nh:]


def _plan(ins, outs, sems, copies):
    def steps(in_refs, out_refs, *sem_refs):
        def start():
            for cp in copies(in_refs, out_refs, *sem_refs):
                cp.start()

        def finish():
            for cp in copies(in_refs, out_refs, *sem_refs):
                cp.wait()

        return start, finish

    return dict(ins=list(ins), outs=list(outs), sems=list(sems), steps=steps)


def _pair_exchange_plan(grads):
    n = len(grads)

    def copies(ins, got, send, recv):
        x, y, c = _me()
        return [pltpu.make_async_remote_copy(
            src_ref=ins[i].at[j, 1 - c], dst_ref=got[i].at[j], send_sem=send.at[i, j], recv_sem=recv.at[i, j],
            device_id=(x, y, 1 - c), device_id_type=MESH) for i in range(n) for j in range(4)]

    return _plan(grads, [jax.ShapeDtypeStruct((4,) + g.shape[2:], g.dtype) for g in grads],
                 [pltpu.SemaphoreType.DMA((n, 4)), pltpu.SemaphoreType.DMA((n, 4))], copies)


def _chip_exchange_plan(sums):
    n = len(sums)

    def copies(ins, got, send, recv):
        x, y, c = _me()
        chips = _other_chips(x, y)
        return [pltpu.make_async_remote_copy(
            src_ref=ins[i].at[_chip_of(*chips[k])], dst_ref=got[i].at[k], send_sem=send.at[i, k],
            recv_sem=recv.at[i, k], device_id=(*chips[k], c), device_id_type=MESH) for i in range(n) for k in range(3)]

    return _plan(sums, [jax.ShapeDtypeStruct((3,) + a.shape[1:], a.dtype) for a in sums],
                 [pltpu.SemaphoreType.DMA((n, 3)), pltpu.SemaphoreType.DMA((n, 3))], copies)


def _pair_share_plan(halves):
    n = len(halves)

    def copies(ins, outs, send, recv):
        x, y, c = _me()
        return [pltpu.make_async_remote_copy(
            src_ref=ins[i], dst_ref=outs[i], send_sem=send.at[i], recv_sem=recv.at[i],
            device_id=(x, y, 1 - c), device_id_type=MESH) for i in range(n)]

    return _plan(halves, [jax.ShapeDtypeStruct(h.shape, h.dtype) for h in halves],
                 [pltpu.SemaphoreType.DMA((n,)), pltpu.SemaphoreType.DMA((n,))], copies)


def _run_exchange(plan, name):
    ni, no = len(plan["ins"]), len(plan["outs"])

    def body(*refs):
        start, finish = plan["steps"](refs[:ni], refs[ni:ni + no], *refs[ni + no:])
        start()
        finish()

    return _call(
        body, name=name, in_specs=[HBM] * ni, out_specs=[HBM] * no, out_shape=plan["outs"],
        scratch_shapes=plan["sems"], compiler_params=_params(),
    )(*plan["ins"])


def _allreduce_small(vec):
    P = vec.shape[1]

    def body(v_ref, sum_ref, all_ref, send, recv):
        x, y, c = _me()
        me = 4 * x + 2 * y + c
        all_ref[pl.ds(me, 1)] = v_ref[...][None]
        cps = []
        for d in range(1, 8):
            peer = (jnp.bitwise_xor(x, d >> 2), jnp.bitwise_xor(y, (d >> 1) & 1), jnp.bitwise_xor(c, d & 1))
            r = pltpu.make_async_remote_copy(
                src_ref=v_ref, dst_ref=all_ref.at[me], send_sem=send.at[d - 1], recv_sem=recv.at[d - 1],
                device_id=peer, device_id_type=MESH)
            r.start()
            cps.append(r)
        for d in range(1, 8):
            src = jnp.bitwise_xor(me, d)
            pltpu.make_async_remote_copy(
                src_ref=v_ref, dst_ref=all_ref.at[src], send_sem=send.at[d - 1], recv_sem=recv.at[d - 1],
                device_id=(x, y, c), device_id_type=MESH).wait_recv()
        for r in cps:
            r.wait_send()
        acc = all_ref[0]
        for i in range(1, 8):
            acc = acc + all_ref[i]
        sum_ref[...] = acc

    vm = pl.BlockSpec(memory_space=pltpu.VMEM)
    return _call(
        body, name="allreduce_small", in_specs=[vm], out_specs=[vm, vm],
        out_shape=[jax.ShapeDtypeStruct((8, P), F32), jax.ShapeDtypeStruct((8, 8, P), F32)],
        scratch_shapes=[pltpu.SemaphoreType.DMA((7,)), pltpu.SemaphoreType.DMA((7,))],
        compiler_params=_params(),
    )(vec)[0]


def _per_batch(mod, B, D):
    return [mod[:B, i * D:(i + 1) * D].reshape(B, 1, D) for i in range(3)]


def _pad_rows8(a):
    return jnp.concatenate([a, jnp.zeros((8 - a.shape[0],) + a.shape[1:], a.dtype)], axis=0)


def _layer_fwd(x, c8, w, S, fox, tag, gather=()):
    T, D = x.shape
    B = T // S
    DI = w["w_out"].shape[0]
    H = DI // HEAD_DIM
    tq = _tile(S, FOX_BLOCK if fox else SB_BLOCK, 8)
    mod = _mod_fwd(c8, w["w_ada"], w["b_ada"], tag + "_mod_fwd")
    shift, scale, gate = _per_batch(mod, B, D)
    proj, h, gathered = _ln_proj(x, shift, scale, w["norm_g"], w["w_in"], S, tag + "_ln_proj", gather)
    saved = dict(x=x, h=h, proj=proj, scale=scale, gate=gate, gathered=gathered)
    if fox:
        fl = _mm(h, w["w_f"], "nn", F32, tag + "_flogit").reshape(B, S, LANES)
        cum = _cum_fwd(fl, w["b_f"], tag + "_cum_fwd")
        cumrow = cum[:, :, :H].transpose(0, 2, 1).reshape(B, H, S // tq, 1, tq)
        o, stat = _fox_fwd(proj, cum, cumrow, tag + "_attn_fwd")
        saved.update(fl=fl, cum=cum, cumrow=cumrow)
    else:
        o, stat = _sb_fwd(proj, B, tq, tag + "_attn_fwd")
    xo, y, u = _gate_out(o, proj, w["w_out"], x, gate, S, tag + "_gate_out")
    saved.update(o=o, stat=stat, y=y, u=u)
    return xo, saved


def _hosted(side, sent, call):
    if side is None:
        return call(None), None
    plan, _ = next(side) if sent is None else side.send(sent)
    return call(plan)


def _layer_bwd(dxo, sv, w, cT, S, fox, tag, side=None):
    T, D = dxo.shape
    B = T // S
    DI = w["w_out"].shape[0]
    H = DI // HEAD_DIM
    tq = _tile(S, FOX_BLOCK if fox else SB_BLOCK, 8)
    dy, do, dzg, dgate = _out_bwd(dxo, sv["y"], sv["gate"], w["w_out"], sv["o"], sv["proj"], S, tag + "_out_bwd")
    dw_out, landed = _hosted(side, None, lambda r: _mm(sv["u"], dy, "tn", F32, tag + "_dw_out", tm=1024, tn=1024,
                                                      tk=2048, rider=r))
    g = {"w_out": dw_out}
    q_cols = jnp.where(jnp.arange(4 * DI)[None, :] < DI, Q_SCALE, 1.0).astype(F32)
    if fox:
        dproj, dcs = _fox_bwd(sv["proj"], do, dzg, sv["o"], sv["stat"], sv["cum"], sv["cumrow"], tag + "_attn_bwd")
        dcs = dcs.reshape(B, H, S).transpose(0, 2, 1)
        dcs = jnp.concatenate([dcs, jnp.zeros((B, S, LANES - H), F32)], axis=-1)
        dfl, db_f = _cum_bwd(dcs, sv["fl"], w["b_f"], tag + "_cum_bwd")
        g["b_f"] = db_f[:, :H]
        dfl = dfl.reshape(T, LANES).astype(BF16)
    else:
        dproj = _sb_bwd(sv["proj"], do, dzg, sv["stat"], B, tq, tag + "_attn_bwd")
    g["w_in"], landed = _hosted(side, landed, lambda r: _mm(sv["h"], dproj, "tn", F32, tag + "_dw_in", tm=1024, tn=2048,
                                                            tk=1024, col_scale=q_cols, rider=r))
    dh, landed = _hosted(side, landed, lambda r: _mm(dproj, w["w_in"], "nt", F32, tag + "_dh", tm=2048, tn=1024,
                                                     tk=1024, rider=r))
    dhs = [dh]
    if side is not None:
        try:
            side.send(landed)
        except StopIteration as done:
            g["side"] = done.value
    if fox:
        dw_f = _mm(sv["h"], dfl, "tn", F32, tag + "_dw_f", tm=1024, tn=LANES, tk=2048)
        g["w_in"] = jnp.concatenate([g["w_in"], dw_f[:, :H]], axis=1)
        dhs.append(_mm(dfl, w["w_f"], "nt", F32, tag + "_dh_f", tm=2048, tn=1024, tk=LANES))
    dx, dshift, dscale, dg = _ln_bwd(dhs, sv["x"], dxo, sv["scale"], w["norm_g"], S, tag + "_ln_bwd")
    g["norm_g"] = dg
    dmod = jnp.concatenate([dshift, dscale, dgate], axis=-1).reshape(B, 3 * D)
    g["w_ada"], g["b_ada"] = _mod_bwd(cT, _pad_rows8(dmod), B, tag + "_mod_bwd")
    return dx, g


def _local_step(x3, c, tgt3, wf, ws, final_g, sb_halves=(), sb_side=None):
    B, S, D = x3.shape
    T = B * S
    x = x3.reshape(T, D)
    c8 = _pad_rows8(c)
    cT = c8.T
    x1, sv1 = _layer_fwd(x, c8, wf, S, True, "fox", sb_halves)
    if sb_halves:
        ws = ws(sv1["gathered"])
    x2, sv2 = _layer_fwd(x1, c8, ws, S, False, "sb")
    dx2, dgf, loss = _final_loss(x2, tgt3.reshape(T, D), final_g, S, "final_loss")
    dx1, gs = _layer_bwd(dx2, sv2, ws, cT, S, False, "sb")
    dx0, gf = _layer_bwd(dx1, sv1, wf, cT, S, True, "fox", None if sb_side is None else sb_side(gs))
    return loss, dx0.reshape(B, S, D), gf, gs, dgf


def _cols_to_shards(a):
    R, C4 = a.shape
    return a.reshape(R, 4, C4 // 4).transpose(1, 0, 2)


def _shards_to_cols(a):
    n, R, C = a.shape
    return a.transpose(1, 0, 2).reshape(R, n * C)


def kernel(x, c, fox_norm_g, fox_w_ada, fox_b_ada, fox_w_in, fox_b_f, fox_w_out, sb_norm_g, sb_w_ada, sb_b_ada, sb_w_in, sb_w_out, final_norm_g, loss_target, m_fox_norm_g, m_fox_w_ada, m_fox_b_ada, m_fox_w_in, m_fox_b_f, m_fox_w_out, m_sb_norm_g, m_sb_w_ada, m_sb_b_ada, m_sb_w_in, m_sb_w_out, m_final_norm_g, v_fox_norm_g, v_fox_w_ada, v_fox_b_ada, v_fox_w_in, v_fox_b_f, v_fox_w_out, v_sb_norm_g, v_sb_w_ada, v_sb_b_ada, v_sb_w_in, v_sb_w_out, v_final_norm_g):
    B, S, D = x.shape
    DI = 4 * fox_w_out.shape[1]
    H = DI // HEAD_DIM
    chip = _chip_of(lax.axis_index("x"), lax.axis_index("y"))

    big_names = ["fox_w_ada", "fox_w_in", "fox_w_out", "sb_w_ada", "sb_w_in", "sb_w_out"]
    big = dict(fox_w_ada=fox_w_ada[0], fox_w_in=fox_w_in[0], fox_w_out=fox_w_out[0],
               sb_w_ada=sb_w_ada[0], sb_w_in=sb_w_in[0], sb_w_out=sb_w_out[0])
    for n in ("fox_w_in", "sb_w_in"):
        width = big[n].shape[1]
        is_q = chip * width + jnp.arange(width)[None, :] < DI
        big[n] = big[n] * jnp.where(is_q, Q_SCALE, 1.0).astype(F32)
    halves = {n: big[n].astype(BF16).reshape(2, big[n].shape[0] // 2, big[n].shape[1]) for n in big_names}
    fox_names, sb_names = big_names[:3], big_names[3:]

    def assemble(names, gathered):
        full = {}
        for n, a in zip(names, gathered):
            a = lax.dynamic_update_index_in_dim(a, halves[n], chip, 0)
            a = a.reshape(4, a.shape[1] * a.shape[2], a.shape[3])
            full[n] = a.reshape(4 * a.shape[1], a.shape[2]) if n.endswith("w_out") else _shards_to_cols(a)
        return full

    gathered, gsmall = _gather_weights([halves[n] for n in fox_names], [sb_norm_g, sb_b_ada])
    gsmall = [lax.dynamic_update_index_in_dim(a, own, chip, 0) for a, own in zip(gsmall, [sb_norm_g, sb_b_ada])]
    full = assemble(fox_names, gathered)
    sb_norm_full = gsmall[0].reshape(1, D)
    sb_b_ada_full = gsmall[1].reshape(1, 3 * D)
    w_f = jnp.concatenate([full["fox_w_in"][:, 4 * DI:], jnp.zeros((D, LANES - H), BF16)], axis=1)
    b_f = jnp.concatenate([fox_b_f, jnp.zeros((1, LANES - H), F32)], axis=1)
    wf = dict(w_ada=full["fox_w_ada"], b_ada=fox_b_ada, norm_g=fox_norm_g, w_in=full["fox_w_in"][:, :4 * DI],
              w_f=w_f, b_f=b_f, w_out=full["fox_w_out"])

    def ws(gathered_sb):
        f = assemble(sb_names, gathered_sb)
        return dict(w_ada=f["sb_w_ada"], b_ada=sb_b_ada_full, norm_g=sb_norm_full, w_in=f["sb_w_in"], w_out=f["sb_w_out"])

    core = lax.axis_index("c")

    def reduction(names, part, tag):
        shard_major = []
        for n in names:
            a = part[n]
            a = a.reshape(4, a.shape[0] // 4, a.shape[1]) if n.endswith("w_out") else _cols_to_shards(a)
            shard_major.append(a.reshape(4, 2, a.shape[1] // 2, a.shape[2]))
        got = yield _pair_exchange_plan(shard_major), tag + "_grad_pair_exchange"
        pair_f32, pair_bf16 = [], []
        for n, g4, b in zip(names, shard_major, got):
            a = lax.dynamic_index_in_dim(g4, core, axis=1, keepdims=False)
            r, C = a.shape[1:]
            s32, s16 = _ew_sum([a.reshape(4 * r, C), b.reshape(4 * r, C)], n + "_pair_sum", also_bf16=True)
            pair_f32.append(s32.reshape(4, r, C))
            pair_bf16.append(s16.reshape(4, r, C))
        others = yield _chip_exchange_plan(pair_bf16), tag + "_grad_chip_exchange"
        reduced_halves = [_ew_sum([lax.dynamic_index_in_dim(a, chip, axis=0, keepdims=False), b[0], b[1], b[2]],
                                  n + "_chip_sum")[0] for n, a, b in zip(names, pair_f32, others)]
        theirs = yield _pair_share_plan(reduced_halves), tag + "_grad_pair_share"
        return {n: jnp.concatenate([jnp.where(core == 0, a, b), jnp.where(core == 0, b, a)], axis=0)
                for n, a, b in zip(names, reduced_halves, theirs)}

    def sb_side(gs):
        return reduction(sb_names, dict(sb_w_ada=gs["w_ada"], sb_w_in=gs["w_in"], sb_w_out=gs["w_out"]), "sb")

    loss, grad_x, gf, gs, dgf = _local_step(x, c, loss_target, wf, ws, final_norm_g.reshape(1, D),
                                            [halves[n] for n in sb_names], sb_side)
    grad_big = dict(gf["side"])
    fox_red = reduction(fox_names, dict(fox_w_ada=gf["w_ada"], fox_w_in=gf["w_in"], fox_w_out=gf["w_out"]), "fox")
    try:
        plan, name = next(fox_red)
        while True:
            plan, name = fox_red.send(_run_exchange(plan, name))
    except StopIteration as done:
        grad_big.update(done.value)

    pieces = [loss, gf["norm_g"], gf["b_ada"], jnp.concatenate([gf["b_f"], jnp.zeros((1, LANES - H), F32)], axis=1),
              gs["norm_g"], gs["b_ada"], dgf]
    vec = jnp.concatenate(pieces, axis=1)
    red = _allreduce_small(_pad_rows8(vec))[0:1]
    offs = [0]
    for p in pieces:
        offs.append(offs[-1] + p.shape[1])
    r_loss, r_fng, r_fba, r_fbf, r_sng, r_sba, r_fin = [red[:, offs[i]:offs[i + 1]] for i in range(7)]
    small_grads = dict(
        fox_norm_g=r_fng, fox_b_ada=r_fba, fox_b_f=r_fbf[:, :H],
        sb_norm_g=lax.dynamic_slice_in_dim(r_sng, chip * (D // 4), D // 4, axis=1),
        sb_b_ada=lax.dynamic_slice_in_dim(r_sba, chip * (3 * D // 4), 3 * D // 4, axis=1),
        final_norm_g=r_fin)

    weights = dict(fox_norm_g=fox_norm_g, fox_w_ada=fox_w_ada, fox_b_ada=fox_b_ada, fox_w_in=fox_w_in, fox_b_f=fox_b_f,
                   fox_w_out=fox_w_out, sb_norm_g=sb_norm_g, sb_w_ada=sb_w_ada, sb_b_ada=sb_b_ada, sb_w_in=sb_w_in,
                   sb_w_out=sb_w_out, final_norm_g=final_norm_g)
    ms = dict(fox_norm_g=m_fox_norm_g, fox_w_ada=m_fox_w_ada, fox_b_ada=m_fox_b_ada, fox_w_in=m_fox_w_in,
              fox_b_f=m_fox_b_f, fox_w_out=m_fox_w_out, sb_norm_g=m_sb_norm_g, sb_w_ada=m_sb_w_ada,
              sb_b_ada=m_sb_b_ada, sb_w_in=m_sb_w_in, sb_w_out=m_sb_w_out, final_norm_g=m_final_norm_g)
    vs = dict(fox_norm_g=v_fox_norm_g, fox_w_ada=v_fox_w_ada, fox_b_ada=v_fox_b_ada, fox_w_in=v_fox_w_in,
              fox_b_f=v_fox_b_f, fox_w_out=v_fox_w_out, sb_norm_g=v_sb_norm_g, sb_w_ada=v_sb_w_ada,
              sb_b_ada=v_sb_b_ada, sb_w_in=v_sb_w_in, sb_w_out=v_sb_w_out, final_norm_g=v_final_norm_g)
    order = ["fox_norm_g", "fox_w_ada", "fox_b_ada", "fox_w_in", "fox_b_f", "fox_w_out", "sb_norm_g", "sb_w_ada",
             "sb_b_ada", "sb_w_in", "sb_w_out", "final_norm_g"]
    grads, deltas, new_m, new_v = {}, {}, {}, {}
    for n in big_names:
        shp = weights[n].shape
        g2 = grad_big[n]
        d, m2, v2 = _adamw(weights[n][0], g2, ms[n][0], vs[n][0], n + "_adamw")
        grads[n], deltas[n], new_m[n], new_v[n] = g2.reshape(shp), d.reshape(shp), m2.reshape(shp), v2.reshape(shp)
    small_names = [n for n in order if n not in big_names]
    sizes = [small_grads[n].shape[1] for n in small_names]
    total = sum(sizes)
    padn = (-total) % LANES

    def pack(d):
        return jnp.concatenate([d[n].reshape(1, -1) for n in small_names] + [jnp.ones((1, padn), F32)], axis=1)

    sd, sm, sv_ = _adamw(pack(weights), pack(small_grads), pack(ms), pack(vs), "small_adamw")
    o = 0
    for n, sz in zip(small_names, sizes):
        shp = weights[n].shape
        grads[n] = small_grads[n].reshape(shp)
        deltas[n], new_m[n], new_v[n] = (t[:, o:o + sz].reshape(shp) for t in (sd, sm, sv_))
        o += sz
    return (r_loss[0, 0], grad_x, *[grads[n] for n in order], *[deltas[n] for n in order],
            *[new_m[n] for n in order], *[new_v[n] for n in order])
```

```python
import jax
import jax.numpy as jnp
from jax import lax
from jax.experimental import pallas as pl
from jax.experimental.pallas import tpu as pltpu

F32 = jnp.float32
BF16 = jnp.bfloat16
HEAD_DIM = 64
LOG2E = 1.4426950408889634
LN2 = 0.6931471805599453
Q_SCALE = HEAD_DIM ** -0.5 * LOG2E
LANES = 128
NORM_EPS = 1e-6
ADAM_LR = 0.001
ADAM_B1 = 0.9
ADAM_B2 = 0.999
ADAM_EPS = 1e-08
ADAM_WD = 0.01
ADAM_STEP = 10
VMEM_LIMIT = 56 * 1024 * 1024
SB_BLOCK = 256
FOX_BLOCK = 512
MESH = pl.DeviceIdType.MESH
HBM = pl.BlockSpec(memory_space=pltpu.HBM)
NT = (((1,), (1,)), ((), ()))
TN = (((0,), (0,)), ((), ()))


def _call(body, **kw):
    return pl.pallas_call(body, **kw)


def _params(**kw):
    return pltpu.CompilerParams(vmem_limit_bytes=VMEM_LIMIT, **kw)


def _tile(dim, pref, mult=128):
    if dim <= pref:
        return dim
    t = (pref // mult) * mult
    while t >= mult:
        if dim % t == 0:
            return t
        t -= mult
    return dim


def _sigmoid(x):
    return 1.0 / (1.0 + jnp.exp(-x))


def _split3(x):
    hi = x.astype(BF16)
    r = x - hi.astype(F32)
    mid = r.astype(BF16)
    lo = (r - mid.astype(F32)).astype(BF16)
    return hi, mid, lo


def _mm(a, b, mode, out_dtype, name, tm=512, tn=512, tk=512, col_scale=None, rider=None):
    a_slabs = a.shape[0] if a.ndim == 3 else 0
    b_slabs = b.shape[0] if b.ndim == 3 else 0
    if mode == "nn":
        (M, K), (_, N) = a.shape, b.shape
    elif mode == "nt":
        M, K = (a.shape[1], a_slabs * a.shape[2]) if a_slabs else a.shape
        N = b.shape[0]
    else:
        K, M = a.shape
        N = b_slabs * b.shape[2] if b_slabs else b.shape[1]
    tm, tn, tk = _tile(M, tm), _tile(N, tn), _tile(K, tk)
    if a_slabs:
        tk = _tile(a.shape[2], tk)
    if b_slabs:
        tn = _tile(b.shape[2], tn)
    nk = K // tk
    dims = {"nn": (((1,), (0,)), ((), ())), "nt": NT, "tn": TN}[mode]

    r_ins = rider["ins"] if rider else []
    r_outs = rider["outs"] if rider else []
    r_sems = rider["sems"] if rider else []
    nc = 0 if col_scale is None else 1
    ni, no = len(r_ins), len(r_outs)
    grid = (M // tm, N // tn, nk)

    def body(a_ref, b_ref, *rest):
        o_ref = rest[nc + ni]
        acc_ref = rest[nc + ni + 1 + no]
        k = pl.program_id(2)
        if rider:
            start, finish = rider["steps"](rest[nc:nc + ni], rest[nc + ni + 1:nc + ni + 1 + no], *rest[nc + ni + 2 + no:])
            at = [pl.program_id(d) for d in range(3)]
            pl.when(jnp.logical_and(jnp.logical_and(at[0] == 0, at[1] == 0), at[2] == 0))(start)

        @pl.when(k == 0)
        def _():
            acc_ref[...] = jnp.zeros_like(acc_ref)

        acc_ref[...] += lax.dot_general(a_ref[...], b_ref[...], dims, preferred_element_type=F32)

        @pl.when(k == nk - 1)
        def _():
            acc = acc_ref[...]
            if col_scale is not None:
                acc = acc * rest[0][...]
            o_ref[...] = acc.astype(out_dtype)

        if rider:
            pl.when(jnp.logical_and(jnp.logical_and(at[0] == grid[0] - 1, at[1] == grid[1] - 1), at[2] == nk - 1))(finish)

    if a_slabs:
        per = a.shape[2] // tk
        a_spec = pl.BlockSpec((None, tm, tk), lambda i, j, k: (k // per, i, k % per))
    elif mode == "tn":
        a_spec = pl.BlockSpec((tk, tm), lambda i, j, k: (k, i))
    else:
        a_spec = pl.BlockSpec((tm, tk), lambda i, j, k: (i, k))
    if b_slabs:
        per_b = b.shape[2] // tn
        b_spec = pl.BlockSpec((None, tk, tn), lambda i, j, k: (j // per_b, k, j % per_b))
    elif mode == "nt":
        b_spec = pl.BlockSpec((tn, tk), lambda i, j, k: (j, k))
    else:
        b_spec = pl.BlockSpec((tk, tn), lambda i, j, k: (k, j))
    extra_specs = [] if col_scale is None else [pl.BlockSpec((1, tn), lambda i, j, k: (0, j))]
    extra = [] if col_scale is None else [col_scale]
    res = _call(
        body, name=name, grid=grid,
        in_specs=[a_spec, b_spec] + extra_specs + [HBM] * ni,
        out_specs=[pl.BlockSpec((tm, tn), lambda i, j, k: (i, j))] + [HBM] * no,
        out_shape=[jax.ShapeDtypeStruct((M, N), out_dtype)] + list(r_outs),
        scratch_shapes=[pltpu.VMEM((tm, tn), F32)] + list(r_sems), compiler_params=_params(),
    )(a, b, *extra, *r_ins)
    return (res[0], res[1:]) if rider else res[0]


def _mod_fwd(c8, w_ada, b_ada, name):
    D, N = w_ada.shape
    tn = _tile(N, 512)

    def body(c_ref, w_ref, b_ref, o_ref):
        c = c_ref[...]
        sc = (c * _sigmoid(c)).astype(BF16)
        o_ref[...] = jnp.dot(sc, w_ref[...], preferred_element_type=F32) + b_ref[...]

    return _call(
        body, name=name, grid=(N // tn,),
        in_specs=[pl.BlockSpec((8, D), lambda j: (0, 0)), pl.BlockSpec((D, tn), lambda j: (0, j)),
                  pl.BlockSpec((1, tn), lambda j: (0, j))],
        out_specs=pl.BlockSpec((8, tn), lambda j: (0, j)),
        out_shape=jax.ShapeDtypeStruct((8, N), F32), compiler_params=_params(),
    )(c8, w_ada, b_ada)


def _mod_bwd(cT, dmod8, nb, name):
    D = cT.shape[0]
    N = dmod8.shape[1]
    tn = _tile(N, 512)

    def body(c_ref, d_ref, w_ref, b_ref):
        c = c_ref[...]
        sc = c * _sigmoid(c)
        d = d_ref[...]
        acc = sc[:, 0:1] * d[0:1, :]
        bsum = d[0:1, :]
        for b in range(1, nb):
            acc = acc + sc[:, b:b + 1] * d[b:b + 1, :]
            bsum = bsum + d[b:b + 1, :]
        w_ref[...] = acc
        b_ref[...] = bsum

    return _call(
        body, name=name, grid=(N // tn,),
        in_specs=[pl.BlockSpec((D, 8), lambda j: (0, 0)), pl.BlockSpec((8, tn), lambda j: (0, j))],
        out_specs=[pl.BlockSpec((D, tn), lambda j: (0, j)), pl.BlockSpec((1, tn), lambda j: (0, j))],
        out_shape=[jax.ShapeDtypeStruct((D, N), F32), jax.ShapeDtypeStruct((1, N), F32)],
        compiler_params=_params(),
    )(cT, dmod8)


def _ln_proj(x, shift, scale, g, w, S, name, gather=()):
    T, D = x.shape
    N = w.shape[1]
    tm = _tile(S, 2048)
    tn = _tile(N, 1024)
    per_b = S // tm
    ng = len(gather)
    n0, n1 = T // tm, N // tn

    def body(x_ref, sh_ref, sc_ref, g_ref, w_ref, *rest):
        ins_h, (p_ref, h_ref), outs_h, sems = rest[:ng], rest[ng:ng + 2], rest[ng + 2:2 * ng + 2], rest[2 * ng + 2:]
        i, j = pl.program_id(0), pl.program_id(1)
        if ng:
            start, finish = _gather_steps(ins_h, outs_h, *sems)
            pl.when(jnp.logical_and(i == 0, j == 0))(start)

        @pl.when(j == 0)
        def _():
            xv = x_ref[...]
            r = lax.rsqrt(jnp.mean(xv * xv, axis=-1, keepdims=True) + NORM_EPS)
            h = (xv * r) * g_ref[...] * (1.0 + sc_ref[0]) + sh_ref[0]
            h_ref[...] = h.astype(BF16)

        p_ref[...] = jnp.dot(h_ref[...], w_ref[...], preferred_element_type=F32).astype(BF16)
        if ng:
            pl.when(jnp.logical_and(i == n0 - 1, j == n1 - 1))(finish)

    res = _call(
        body, name=name, grid=(n0, n1),
        in_specs=[pl.BlockSpec((tm, D), lambda i, j: (i, 0)),
                  pl.BlockSpec((1, 1, D), lambda i, j: (i // per_b, 0, 0)),
                  pl.BlockSpec((1, 1, D), lambda i, j: (i // per_b, 0, 0)),
                  pl.BlockSpec((1, D), lambda i, j: (0, 0)),
                  pl.BlockSpec((D, tn), lambda i, j: (0, j))] + [HBM] * ng,
        out_specs=[pl.BlockSpec((tm, tn), lambda i, j: (i, j)), pl.BlockSpec((tm, D), lambda i, j: (i, 0))] + [HBM] * ng,
        out_shape=[jax.ShapeDtypeStruct((T, N), BF16), jax.ShapeDtypeStruct((T, D), BF16)]
        + [jax.ShapeDtypeStruct((4,) + a.shape, a.dtype) for a in gather],
        scratch_shapes=_gather_sems(ng) if ng else [], compiler_params=_params(),
    )(x, shift, scale, g, w, *gather)
    return res[0], res[1], res[2:]


def _ln_bwd(dhs, x, dxo, scale, g, S, name):
    T, D = x.shape
    B = T // S
    tm = _tile(S, 512)
    per_b = S // tm

    nd = len(dhs)

    def body(*refs):
        x_ref, dxo_ref, sc_ref, g_ref, dx_ref, dsh_ref, dsc_ref, dg_ref = refs[nd:]
        i = pl.program_id(0)
        xv = x_ref[...]
        dh_v = refs[0][...]
        for r in refs[1:nd]:
            dh_v = dh_v + r[...]
        r = lax.rsqrt(jnp.mean(xv * xv, axis=-1, keepdims=True) + NORM_EPS)
        xn = xv * r
        gv = g_ref[...]
        one_sc = 1.0 + sc_ref[0]
        dhxn = dh_v * xn

        @pl.when(i % per_b == 0)
        def _():
            dsh_ref[...] = jnp.zeros_like(dsh_ref)
            dsc_ref[...] = jnp.zeros_like(dsc_ref)

        @pl.when(i == 0)
        def _():
            dg_ref[...] = jnp.zeros_like(dg_ref)

        dsh_ref[0] += jnp.sum(dh_v, axis=0, keepdims=True)
        dsc_ref[0] += jnp.sum(dhxn, axis=0, keepdims=True) * gv
        dg_ref[...] += jnp.sum(dhxn, axis=0, keepdims=True) * one_sc
        dxn = dh_v * (gv * one_sc)
        dx_ref[...] = r * (dxn - xn * jnp.mean(dxn * xn, axis=-1, keepdims=True)) + dxo_ref[...]

    row = pl.BlockSpec((tm, D), lambda i: (i, 0))
    per = pl.BlockSpec((1, 1, D), lambda i: (i // per_b, 0, 0))
    vec = pl.BlockSpec((1, D), lambda i: (0, 0))
    return _call(
        body, name=name, grid=(T // tm,),
        in_specs=[row] * (nd + 2) + [per, vec], out_specs=[row, per, per, vec],
        out_shape=[jax.ShapeDtypeStruct((T, D), F32), jax.ShapeDtypeStruct((B, 1, D), F32),
                   jax.ShapeDtypeStruct((B, 1, D), F32), jax.ShapeDtypeStruct((1, D), F32)],
        compiler_params=_params(),
    )(*dhs, x, dxo, scale, g)


def _gate_out(o, proj, w_out, x, gate, S, name):
    T, DI = o.shape
    D = w_out.shape[1]
    tm = _tile(S, 512)
    per_b = S // tm

    def body(o_ref, z_ref, w_ref, x_ref, g_ref, xo_ref, y_ref, u_ref):
        z = z_ref[...].astype(F32)
        u = (o_ref[...] * (z * _sigmoid(z))).astype(BF16)
        u_ref[...] = u
        y = jnp.dot(u, w_ref[...], preferred_element_type=F32)
        y_ref[...] = y
        xo_ref[...] = x_ref[...] + g_ref[0] * y

    wide = pl.BlockSpec((tm, DI), lambda i: (i, 0))
    row = pl.BlockSpec((tm, D), lambda i: (i, 0))
    return _call(
        body, name=name, grid=(T // tm,),
        in_specs=[wide, pl.BlockSpec((tm, DI), lambda i: (i, 3)), pl.BlockSpec((DI, D), lambda i: (0, 0)), row,
                  pl.BlockSpec((1, 1, D), lambda i: (i // per_b, 0, 0))],
        out_specs=[row, row, wide],
        out_shape=[jax.ShapeDtypeStruct((T, D), F32), jax.ShapeDtypeStruct((T, D), F32),
                   jax.ShapeDtypeStruct((T, DI), BF16)],
        compiler_params=_params(),
    )(o, proj, w_out, x, gate)


def _out_bwd(dxo, y, gate, w_out, o, proj, S, name):
    T, D = dxo.shape
    DI = o.shape[1]
    B = T // S
    tm = _tile(S, 512)
    per_b = S // tm

    def body(dxo_ref, y_ref, g_ref, w_ref, o_ref, z_ref, dy_ref, do_ref, dz_ref, dg_ref):
        dxo_v = dxo_ref[...]
        dy = (dxo_v * g_ref[0]).astype(BF16)
        dy_ref[...] = dy
        du = lax.dot_general(dy, w_ref[...], NT, preferred_element_type=F32)
        z = z_ref[...].astype(F32)
        sg = _sigmoid(z)
        do_ref[...] = (du * (z * sg)).astype(BF16)
        dz_ref[...] = (du * o_ref[...] * (sg * (1.0 + z * (1.0 - sg)))).astype(BF16)

        @pl.when(pl.program_id(0) % per_b == 0)
        def _():
            dg_ref[...] = jnp.zeros_like(dg_ref)

        dg_ref[0] += jnp.sum(dxo_v * y_ref[...], axis=0, keepdims=True)

    wide = pl.BlockSpec((tm, DI), lambda i: (i, 0))
    row = pl.BlockSpec((tm, D), lambda i: (i, 0))
    per = pl.BlockSpec((1, 1, D), lambda i: (i // per_b, 0, 0))
    return _call(
        body, name=name, grid=(T // tm,),
        in_specs=[row, row, per, pl.BlockSpec((DI, D), lambda i: (0, 0)), wide,
                  pl.BlockSpec((tm, DI), lambda i: (i, 3))],
        out_specs=[row, wide, wide, per],
        out_shape=[jax.ShapeDtypeStruct((T, D), BF16), jax.ShapeDtypeStruct((T, DI), BF16),
                   jax.ShapeDtypeStruct((T, DI), BF16), jax.ShapeDtypeStruct((B, 1, D), F32)],
        compiler_params=_params(),
    )(dxo, y, gate, w_out, o, proj)


def _final_loss(x, tgt, g, S, name):
    T, D = x.shape
    tm = _tile(S, 512)

    def body(x_ref, t_ref, g_ref, dx_ref, dg_ref, l_ref):
        @pl.when(pl.program_id(0) == 0)
        def _():
            dg_ref[...] = jnp.zeros_like(dg_ref)
            l_ref[...] = jnp.zeros_like(l_ref)

        xv = x_ref[...]
        gv = g_ref[...]
        r = lax.rsqrt(jnp.mean(xv * xv, axis=-1, keepdims=True) + NORM_EPS)
        xn = xv * r
        e = xn * gv - t_ref[...]
        part = jnp.sum(jnp.sum(e * e, axis=0, keepdims=True), axis=1, keepdims=True)
        l_ref[...] += (0.5 / D) * part
        dy = e * (1.0 / D)
        dg_ref[...] += jnp.sum(dy * xn, axis=0, keepdims=True)
        dxn = dy * gv
        dx_ref[...] = r * (dxn - xn * jnp.mean(dxn * xn, axis=-1, keepdims=True))

    row = pl.BlockSpec((tm, D), lambda i: (i, 0))
    return _call(
        body, name=name, grid=(T // tm,),
        in_specs=[row, row, pl.BlockSpec((1, D), lambda i: (0, 0))],
        out_specs=[row, pl.BlockSpec((1, D), lambda i: (0, 0)), pl.BlockSpec((1, LANES), lambda i: (0, 0))],
        out_shape=[jax.ShapeDtypeStruct((T, D), F32), jax.ShapeDtypeStruct((1, D), F32),
                   jax.ShapeDtypeStruct((1, LANES), F32)],
        compiler_params=_params(),
    )(x, tgt, g)


def _cum_fwd(fl, bf, name):
    B, S, _ = fl.shape
    ch = _tile(S, 256, 8)

    def body(fl_ref, b_ref, cum_ref):
        ri = lax.broadcasted_iota(jnp.int32, (ch, ch), 0)
        ci = lax.broadcasted_iota(jnp.int32, (ch, ch), 1)
        tri = jnp.where(ri >= ci, 1.0, 0.0).astype(BF16)

        def step(i, carry):
            r0 = pl.multiple_of(i * ch, ch)
            z = fl_ref[0, pl.ds(r0, ch), :] + b_ref[...]
            lf = (jnp.minimum(z, 0.0) - jnp.log(1.0 + jnp.exp(-jnp.abs(z)))) * LOG2E
            hi, mid, lo = _split3(lf)
            cs = (jnp.dot(tri, hi, preferred_element_type=F32) + jnp.dot(tri, mid, preferred_element_type=F32)
                  + jnp.dot(tri, lo, preferred_element_type=F32)) + carry
            cum_ref[0, pl.ds(r0, ch), :] = cs
            return cs[ch - 1:ch, :]

        lax.fori_loop(0, S // ch, step, jnp.zeros((1, LANES), F32))

    blk = pl.BlockSpec((1, S, LANES), lambda b: (b, 0, 0))
    return _call(
        body, name=name, grid=(B,), in_specs=[blk, pl.BlockSpec((1, LANES), lambda b: (0, 0))], out_specs=blk,
        out_shape=jax.ShapeDtypeStruct((B, S, LANES), F32), compiler_params=_params(),
    )(fl, bf)


def _cum_bwd(dcs, fl, bf, name):
    B, S, _ = fl.shape
    ch = _tile(S, 256, 8)
    n = S // ch

    def body(d_ref, fl_ref, b_ref, o_ref, db_ref):
        ri = lax.broadcasted_iota(jnp.int32, (ch, ch), 0)
        ci = lax.broadcasted_iota(jnp.int32, (ch, ch), 1)
        tri = jnp.where(ci >= ri, 1.0, 0.0).astype(BF16)

        @pl.when(pl.program_id(0) == 0)
        def _():
            db_ref[...] = jnp.zeros_like(db_ref)

        def step(t, carry):
            tail, dbsum = carry
            r0 = pl.multiple_of((n - 1 - t) * ch, ch)
            hi, mid, lo = _split3(d_ref[0, pl.ds(r0, ch), :])
            suf = (jnp.dot(tri, hi, preferred_element_type=F32) + jnp.dot(tri, mid, preferred_element_type=F32)
                   + jnp.dot(tri, lo, preferred_element_type=F32)) + tail
            z = fl_ref[0, pl.ds(r0, ch), :] + b_ref[...]
            dfl = -suf * _sigmoid(-z)
            o_ref[0, pl.ds(r0, ch), :] = dfl
            return suf[0:1, :], dbsum + jnp.sum(dfl, axis=0, keepdims=True)

        z1 = jnp.zeros((1, LANES), F32)
        _, dbsum = lax.fori_loop(0, n, step, (z1, z1))
        db_ref[...] += dbsum

    blk = pl.BlockSpec((1, S, LANES), lambda b: (b, 0, 0))
    vec = pl.BlockSpec((1, LANES), lambda b: (0, 0))
    return _call(
        body, name=name, grid=(B,), in_specs=[blk, blk, vec], out_specs=[blk, vec],
        out_shape=[jax.ShapeDtypeStruct((B, S, LANES), F32), jax.ShapeDtypeStruct((1, LANES), F32)],
        compiler_params=_params(),
    )(dcs, fl, bf)


HEADS_PER_STEP = 4
GROUP = 2 * HEAD_DIM
DIAG_PIECES = 4


def _step_width():
    return HEAD_DIM * HEADS_PER_STEP


def _cols(S, offset_blocks=0):
    return pl.BlockSpec((S, _step_width()), lambda b, h: (b, offset_blocks + h))


def _row_spec(nq, tq):
    return pl.BlockSpec((1, HEADS_PER_STEP, nq, 1, tq), lambda b, h: (b, h, 0, 0, 0))


def _lanes(g):
    return slice(GROUP * (g // 2), GROUP * (g // 2) + GROUP)


def _hi_lo(x):
    hi = x.astype(BF16)
    return hi, (x - hi.astype(F32)).astype(BF16)


def _dot(a, b, dims=None):
    if dims is None:
        return jnp.dot(a, b, preferred_element_type=F32)
    return lax.dot_general(a, b, dims, preferred_element_type=F32)


def _causal_blocks(nq, prep, init, stages, finish, combine=None, descending=False, last=None):
    heads = range(HEADS_PER_STEP)

    def qloop(qi, _):
        ctx = [prep(g, qi) for g in heads]

        def step(kj, carry, masked):
            st = list(carry)
            for n, stage in enumerate(stages):
                if combine is not None and n == len(stages) - 1:
                    combine(kj, ctx, st)
                st = [stage(g, ctx[g], kj, masked, st[g]) for g in heads]
            return tuple(st)

        carry = tuple(init() for _ in heads)
        if descending:
            carry = step(qi, carry, True)
            carry = lax.fori_loop(0, qi, lambda t, cr: step(qi - 1 - t, cr, False), carry)
        else:
            carry = lax.fori_loop(0, qi, lambda kj, cr: step(kj, cr, False), carry)
            if last is not None:
                last(qi, ctx, carry)
                return 0
            carry = step(qi, carry, True)
        finish(qi, ctx, carry)
        return 0

    lax.fori_loop(0, nq, qloop, 0)


class _Block:
    def __init__(self, tq):
        self.tq = tq
        self.lane = lax.broadcasted_iota(jnp.int32, (tq, GROUP), 1)
        self.low = self.lane < HEAD_DIM
        self.ri = lax.broadcasted_iota(jnp.int32, (tq, tq), 0)
        self.ci = lax.broadcasted_iota(jnp.int32, (tq, tq), 1)

    def rows(self, i):
        return pl.ds(pl.multiple_of(i * self.tq, self.tq), self.tq)

    def own(self, g, x):
        low = self.low[:x.shape[0]]
        return jnp.where(low if g % 2 == 0 else jnp.logical_not(low), x, jnp.zeros_like(x))

    def pair(self, a, b):
        return jnp.where(self.low[:a.shape[0]], a, b)

    def halves(self):
        r = self.tq // DIAG_PIECES
        out = []
        for rr in range(DIAG_PIECES):
            nc = r * (rr + 1)
            out.append((rr * r, r, nc, lax.broadcasted_iota(jnp.int32, (r, nc), 0) + rr * r,
                        lax.broadcasted_iota(jnp.int32, (r, nc), 1)))
        return out

    def stat(self, g, x):
        return jnp.sum(jnp.where(self.lane == HEAD_DIM * (g % 2), x, 0.0), axis=1, keepdims=True)


def _fox_fwd(proj, cumcol, cumrow, name):
    T, DI = proj.shape[0], proj.shape[1] // 4
    B, H, nq, _, tq = cumrow.shape
    S = nq * tq
    nb = DI // _step_width()

    def body(q_ref, k_ref, v_ref, cc_ref, cr_ref, o_ref, st_ref, acc_scr):
        h0 = pl.program_id(1) * HEADS_PER_STEP
        blk = _Block(tq)

        def prep(g, qi):
            acc_scr[g] = jnp.zeros((tq, GROUP), F32)
            q = blk.own(g, q_ref[blk.rows(qi), _lanes(g)])
            ccol = jnp.sum(jnp.where(blk.lane == h0 + g, cc_ref[0, blk.rows(qi), :], 0.0), axis=1, keepdims=True)
            return q, ccol

        def init():
            return jnp.full((tq, 1), -jnp.inf, F32), jnp.zeros((tq, 1), F32)

        def scores(g, ctx, kj, masked, st):
            return st + (_dot(ctx[0], k_ref[blk.rows(kj), _lanes(g)], NT),)

        def softmax(g, ctx, kj, masked, st):
            m, l, s = st
            s = s + ctx[1] - cr_ref[0, g, kj]
            if masked:
                s = jnp.where(blk.ci <= blk.ri, s, -jnp.inf)
            m_new = jnp.maximum(m, jnp.max(s, axis=1, keepdims=True))
            alpha = jnp.exp2(m - m_new)
            p = jnp.exp2(s - m_new)
            return (m_new, alpha * l + jnp.sum(p, axis=1, keepdims=True), alpha) + _hi_lo(p)

        def values(g, ctx, kj, masked, st):
            m, l, alpha, hi, lo = st
            v = v_ref[blk.rows(kj), _lanes(g)]
            acc_scr[g] = alpha * acc_scr[g] + (_dot(hi, v) + _dot(lo, v))
            return m, l

        def finish(qi, ctx, carry):
            for g in range(0, HEADS_PER_STEP, 2):
                (m0, l0), (m1, l1) = carry[g], carry[g + 1]
                o_ref[blk.rows(qi), _lanes(g)] = blk.pair(acc_scr[g] / l0, acc_scr[g + 1] / l1)
                st_ref[blk.rows(qi), _lanes(g)] = blk.pair(m0 + jnp.log2(l0), m1 + jnp.log2(l1))

        def last(qi, ctx, carry):
            k0 = pl.multiple_of(qi * tq, tq)
            pieces = [(g, h) for g in range(HEADS_PER_STEP) for h in blk.halves()]
            s_all = [_dot(ctx[g][0][r0:r0 + r], k_ref[pl.ds(k0, nc), _lanes(g)], NT) for g, (r0, r, nc, _, _) in pieces]
            soft = []
            for (g, (r0, r, nc, ri, ci)), s in zip(pieces, s_all):
                m, l = carry[g][0][r0:r0 + r], carry[g][1][r0:r0 + r]
                s = s + ctx[g][1][r0:r0 + r] - cr_ref[0, g, qi][:, :nc]
                s = jnp.where(ci <= ri, s, -jnp.inf)
                m_new = jnp.maximum(m, jnp.max(s, axis=1, keepdims=True))
                alpha = jnp.exp2(m - m_new)
                p = jnp.exp2(s - m_new)
                soft.append((m_new, alpha * l + jnp.sum(p, axis=1, keepdims=True), alpha) + _hi_lo(p))
            outs = {}
            for (g, (r0, r, nc, _, _)), (m, l, alpha, hi, lo) in zip(pieces, soft):
                v = v_ref[pl.ds(k0, nc), _lanes(g)]
                acc = alpha * acc_scr[g, pl.ds(r0, r)] + (_dot(hi, v) + _dot(lo, v))
                outs[g, r0] = (acc / l, m + jnp.log2(l))
            for g in range(0, HEADS_PER_STEP, 2):
                for r0, r, _, _, _ in blk.halves():
                    rows = pl.ds(pl.multiple_of(qi * tq + r0, r), r)
                    o_ref[rows, _lanes(g)] = blk.pair(outs[g, r0][0], outs[g + 1, r0][0])
                    st_ref[rows, _lanes(g)] = blk.pair(outs[g, r0][1], outs[g + 1, r0][1])

        _causal_blocks(nq, prep, init, [scores, softmax, values], finish, last=last)

    out = jax.ShapeDtypeStruct((T, DI), F32)
    return _call(
        body, name=name, grid=(B, H // HEADS_PER_STEP),
        in_specs=[_cols(S), _cols(S, nb), _cols(S, 2 * nb), pl.BlockSpec((1, S, LANES), lambda b, h: (b, 0, 0)),
                  _row_spec(nq, tq)],
        out_specs=[_cols(S), _cols(S)], out_shape=[out, out],
        scratch_shapes=[pltpu.VMEM((HEADS_PER_STEP, tq, GROUP), F32)], compiler_params=_params(),
    )(proj, proj, proj, cumcol, cumrow)


def _fox_bwd(proj, do, dzg, o, stat, cumcol, cumrow, name):
    T, DI = do.shape
    B, H, nq, _, tq = cumrow.shape
    S = nq * tq
    nb = DI // _step_width()

    def body(q_ref, k_ref, v_ref, do_ref, dz_ref, o_ref, st_ref, cc_ref, cr_ref, dqkv_ref, dcs_ref, dk_acc, dv_acc,
             dq_scr):
        h0 = pl.program_id(1) * HEADS_PER_STEP
        blk = _Block(tq)
        dk_acc[...] = jnp.zeros_like(dk_acc)
        dv_acc[...] = jnp.zeros_like(dv_acc)
        dcs_ref[...] = jnp.zeros_like(dcs_ref)

        def prep(g, qi):
            dq_scr[g] = jnp.zeros((tq, GROUP), F32)
            q = blk.own(g, q_ref[blk.rows(qi), _lanes(g)])
            dout = blk.own(g, do_ref[blk.rows(qi), _lanes(g)])
            delta = jnp.sum(o_ref[blk.rows(qi), _lanes(g)] * dout.astype(F32), axis=1, keepdims=True)
            lse = blk.stat(g, st_ref[blk.rows(qi), _lanes(g)])
            ccol = jnp.sum(jnp.where(blk.lane == h0 + g, cc_ref[0, blk.rows(qi), :], 0.0), axis=1, keepdims=True)
            return q, dout, lse, delta, ccol

        def init():
            return ()

        def scores(g, ctx, kj, masked, st):
            return (_dot(ctx[0], k_ref[blk.rows(kj), _lanes(g)], NT), _dot(ctx[1], v_ref[blk.rows(kj), _lanes(g)], NT))

        def softmax_bwd(g, ctx, kj, masked, st):
            s, dp = st
            _, _, lse, delta, ccol = ctx
            s = s + ccol - cr_ref[0, g, kj]
            if masked:
                s = jnp.where(blk.ci <= blk.ri, s, -jnp.inf)
            p = jnp.exp2(s - lse)
            ds = p * (dp - delta)
            return p.astype(BF16), ds.astype(BF16), jnp.sum(ds, axis=0, keepdims=True)

        def combine(kj, ctx, st):
            for g in range(0, HEADS_PER_STEP, 2):
                dv_acc[blk.rows(kj), _lanes(g)] += _dot(st[g][0], ctx[g][1], TN) + _dot(st[g + 1][0], ctx[g + 1][1], TN)
                dk_acc[blk.rows(kj), _lanes(g)] += _dot(st[g][1], ctx[g][0], TN) + _dot(st[g + 1][1], ctx[g + 1][0], TN)
            for g in range(HEADS_PER_STEP):
                dcs_ref[0, g, kj] += st[g][2]

        def queries(g, ctx, kj, masked, st):
            dq_scr[g] += _dot(st[1], blk.own(g, k_ref[blk.rows(kj), _lanes(g)]))
            return ()

        def finish(qi, ctx, carry):
            for g in range(0, HEADS_PER_STEP, 2):
                dqkv_ref[0, blk.rows(qi), _lanes(g)] = ((dq_scr[g] + dq_scr[g + 1]) * LN2).astype(BF16)

        def last(qi, ctx, carry):
            k0 = pl.multiple_of(qi * tq, tq)
            pieces = [(g, h) for g in range(HEADS_PER_STEP) for h in blk.halves()]
            mm = [(_dot(ctx[g][0][r0:r0 + r], k_ref[pl.ds(k0, nc), _lanes(g)], NT),
                   _dot(ctx[g][1][r0:r0 + r], v_ref[pl.ds(k0, nc), _lanes(g)], NT)) for g, (r0, r, nc, _, _) in pieces]
            soft = {}
            for (g, (r0, r, nc, ri, ci)), (s, dp) in zip(pieces, mm):
                _, _, lse, delta, ccol = ctx[g]
                s = s + ccol[r0:r0 + r] - cr_ref[0, g, qi][:, :nc]
                s = jnp.where(ci <= ri, s, -jnp.inf)
                p = jnp.exp2(s - lse[r0:r0 + r])
                ds = p * (dp - delta[r0:r0 + r])
                soft[g, r0] = (p.astype(BF16), ds.astype(BF16), jnp.sum(ds, axis=0, keepdims=True))
            for r0, r, nc, _, _ in blk.halves():
                for g in range(0, HEADS_PER_STEP, 2):
                    (p0, d0, _), (p1, d1, _) = soft[g, r0], soft[g + 1, r0]
                    q0, q1 = ctx[g][0][r0:r0 + r], ctx[g + 1][0][r0:r0 + r]
                    o0, o1 = ctx[g][1][r0:r0 + r], ctx[g + 1][1][r0:r0 + r]
                    dv_acc[pl.ds(k0, nc), _lanes(g)] += _dot(p0, o0, TN) + _dot(p1, o1, TN)
                    dk_acc[pl.ds(k0, nc), _lanes(g)] += _dot(d0, q0, TN) + _dot(d1, q1, TN)
                for g in range(HEADS_PER_STEP):
                    col = soft[g, r0][2]
                    if nc < tq:
                        col = jnp.concatenate([col, jnp.zeros((1, tq - nc), F32)], axis=1)
                    dcs_ref[0, g, qi] += col
            dq = {}
            for g, (r0, r, nc, _, _) in pieces:
                dq[g, r0] = dq_scr[g, pl.ds(r0, r)] + _dot(soft[g, r0][1], blk.own(g, k_ref[pl.ds(k0, nc), _lanes(g)]))
            for g in range(0, HEADS_PER_STEP, 2):
                for r0, r, _, _, _ in blk.halves():
                    rows = pl.ds(pl.multiple_of(qi * tq + r0, r), r)
                    dqkv_ref[0, rows, _lanes(g)] = ((dq[g, r0] + dq[g + 1, r0]) * LN2).astype(BF16)

        _causal_blocks(nq, prep, init, [scores, softmax_bwd, queries], finish, combine=combine, last=last)
        dqkv_ref[1] = (dk_acc[...] * LN2).astype(BF16)
        dqkv_ref[2] = dv_acc[...].astype(BF16)
        dqkv_ref[3] = dz_ref[...]

    W = _step_width()
    return _call(
        body, name=name, grid=(B, H // HEADS_PER_STEP),
        in_specs=[_cols(S), _cols(S, nb), _cols(S, 2 * nb), _cols(S), _cols(S), _cols(S), _cols(S),
                  pl.BlockSpec((1, S, LANES), lambda b, h: (b, 0, 0)), _row_spec(nq, tq)],
        out_specs=[pl.BlockSpec((4, S, W), lambda b, h: (0, b, h)), _row_spec(nq, tq)],
        out_shape=[jax.ShapeDtypeStruct((4, T, DI), BF16), jax.ShapeDtypeStruct((B, H, nq, 1, tq), F32)],
        scratch_shapes=[pltpu.VMEM((S, W), F32), pltpu.VMEM((S, W), F32), pltpu.VMEM((HEADS_PER_STEP, tq, GROUP), F32)],
        compiler_params=_params(),
    )(proj, proj, proj, do, dzg, o, stat, cumcol, cumrow)


def _log2_keep(z2):
    nz = -z2
    e = jnp.exp2(jnp.minimum(z2, nz))
    return jnp.minimum(nz, 0.0) - jnp.log2(1.0 + e), e


def _sb_fwd(proj, B, tq, name):
    T, DI = proj.shape[0], proj.shape[1] // 4
    S = T // B
    H = DI // HEAD_DIM
    nq = S // tq
    nb = DI // _step_width()

    def body(q_ref, k_ref, v_ref, o_ref, st_ref, acc_scr, c_scr):
        blk = _Block(tq)
        strict = blk.ci < blk.ri
        above = jnp.where(blk.ri > blk.ci, 1.0, 0.0).astype(BF16)

        def prep(g, qi):
            acc_scr[g] = jnp.zeros((tq, GROUP), F32)
            c_scr[g] = jnp.zeros((tq, 1), F32)
            return blk.own(g, q_ref[blk.rows(qi), _lanes(g)])

        def init():
            return ()

        def scores(g, q, kj, masked, st):
            return (_dot(q, k_ref[blk.rows(kj), _lanes(g)], NT),)

        def logs(g, q, kj, masked, st):
            (z,) = st
            lk, _ = _log2_keep(z)
            lb = z + lk
            if masked:
                lk = jnp.where(strict, lk, 0.0)
            c = c_scr[g]
            c_scr[g] = c + jnp.sum(lk, axis=1, keepdims=True)
            return (lb + c,) + _hi_lo(lk)

        def suffix(g, q, kj, masked, st):
            lbc, hi, lo = st
            return lbc, _dot(hi, above) + _dot(lo, above)

        def weights(g, q, kj, masked, st):
            lbc, after = st
            a = jnp.exp2(lbc + after)
            if masked:
                a = jnp.where(strict, a, 0.0)
            return (a.astype(BF16),)

        def values(g, q, kj, masked, st):
            acc_scr[g] += _dot(st[0], v_ref[blk.rows(kj), _lanes(g)])
            return ()

        def finish(qi, ctx, carry):
            for g in range(0, HEADS_PER_STEP, 2):
                o_ref[blk.rows(qi), _lanes(g)] = blk.pair(acc_scr[g], acc_scr[g + 1])
                st_ref[blk.rows(qi), _lanes(g)] = blk.pair(c_scr[g], c_scr[g + 1])

        _causal_blocks(nq, prep, init, [scores, logs, suffix, weights, values], finish, descending=True)

    out = jax.ShapeDtypeStruct((T, DI), F32)
    return _call(
        body, name=name, grid=(B, H // HEADS_PER_STEP), in_specs=[_cols(S), _cols(S, nb), _cols(S, 2 * nb)],
        out_specs=[_cols(S), _cols(S)], out_shape=[out, out],
        scratch_shapes=[pltpu.VMEM((HEADS_PER_STEP, tq, GROUP), F32), pltpu.VMEM((HEADS_PER_STEP, tq, 1), F32)],
        compiler_params=_params(),
    )(proj, proj, proj)


def _sb_bwd(proj, do, dzg, stat, B, tq, name):
    T, DI = do.shape
    S = T // B
    H = DI // HEAD_DIM
    nq = S // tq
    nb = DI // _step_width()

    def body(q_ref, k_ref, v_ref, do_ref, dz_ref, st_ref, dqkv_ref, dk_acc, dv_acc, dq_scr):
        blk = _Block(tq)
        strict = blk.ci < blk.ri
        upto = jnp.where(blk.ri <= blk.ci, 1.0, 0.0).astype(BF16)
        before = jnp.where(blk.ri < blk.ci, 1.0, 0.0).astype(BF16)
        dk_acc[...] = jnp.zeros_like(dk_acc)
        dv_acc[...] = jnp.zeros_like(dv_acc)

        def prep(g, qi):
            dq_scr[g] = jnp.zeros((tq, GROUP), F32)
            return (blk.own(g, q_ref[blk.rows(qi), _lanes(g)]), blk.own(g, do_ref[blk.rows(qi), _lanes(g)]),
                    blk.stat(g, st_ref[blk.rows(qi), _lanes(g)]))

        def init():
            return jnp.zeros((tq, 1), F32), jnp.zeros((tq, 1), F32)

        def scores(g, ctx, kj, masked, st):
            return st + (_dot(ctx[0], k_ref[blk.rows(kj), _lanes(g)], NT),
                         _dot(ctx[1], v_ref[blk.rows(kj), _lanes(g)], NT))

        def logs(g, ctx, kj, masked, st):
            cpre, pg, z, da = st
            lk, _ = _log2_keep(z)
            sig = 1.0 - jnp.exp2(lk)
            lbt = (z + lk) + (ctx[2] - cpre)
            if masked:
                lk = jnp.where(strict, lk, 0.0)
            return (cpre + jnp.sum(lk, axis=1, keepdims=True), pg, da, lbt, sig) + _hi_lo(lk)

        def prefix(g, ctx, kj, masked, st):
            cpre, pg, da, lbt, sig, hi, lo = st
            return cpre, pg, da, lbt, sig, _dot(hi, upto) + _dot(lo, upto)

        def weights(g, ctx, kj, masked, st):
            cpre, pg, da, lbt, sig, pre = st
            a = jnp.exp2(lbt - pre)
            if masked:
                a = jnp.where(strict, a, 0.0)
            gr = da * a
            return cpre, pg, sig, a.astype(BF16), gr, gr.astype(BF16)

        def grad_prefix(g, ctx, kj, masked, st):
            cpre, pg, sig, ab, gr, gb = st
            return cpre, pg, sig, ab, gr, _dot(gb, before)

        def dlogits(g, ctx, kj, masked, st):
            cpre, pg, sig, ab, gr, pfx = st
            dz = gr - sig * (gr + (pfx + pg))
            if masked:
                dz = jnp.where(strict, dz, 0.0)
            return cpre, pg + jnp.sum(gr, axis=1, keepdims=True), ab, dz.astype(BF16)

        def combine(kj, ctx, st):
            for g in range(0, HEADS_PER_STEP, 2):
                dv_acc[blk.rows(kj), _lanes(g)] += _dot(st[g][2], ctx[g][1], TN) + _dot(st[g + 1][2], ctx[g + 1][1], TN)
                dk_acc[blk.rows(kj), _lanes(g)] += _dot(st[g][3], ctx[g][0], TN) + _dot(st[g + 1][3], ctx[g + 1][0], TN)

        def queries(g, ctx, kj, masked, st):
            cpre, pg, _, dzb = st
            dq_scr[g] += _dot(dzb, blk.own(g, k_ref[blk.rows(kj), _lanes(g)]))
            return cpre, pg

        def finish(qi, ctx, carry):
            for g in range(0, HEADS_PER_STEP, 2):
                dqkv_ref[0, blk.rows(qi), _lanes(g)] = ((dq_scr[g] + dq_scr[g + 1]) * LN2).astype(BF16)

        _causal_blocks(nq, prep, init, [scores, logs, prefix, weights, grad_prefix, dlogits, queries], finish,
                       combine=combine)
        dqkv_ref[1] = (dk_acc[...] * LN2).astype(BF16)
        dqkv_ref[2] = dv_acc[...].astype(BF16)
        dqkv_ref[3] = dz_ref[...]

    W = _step_width()
    return _call(
        body, name=name, grid=(B, H // HEADS_PER_STEP),
        in_specs=[_cols(S), _cols(S, nb), _cols(S, 2 * nb), _cols(S), _cols(S), _cols(S)],
        out_specs=pl.BlockSpec((4, S, W), lambda b, h: (0, b, h)),
        out_shape=jax.ShapeDtypeStruct((4, T, DI), BF16),
        scratch_shapes=[pltpu.VMEM((S, W), F32), pltpu.VMEM((S, W), F32), pltpu.VMEM((HEADS_PER_STEP, tq, GROUP), F32)],
        compiler_params=_params(),
    )(proj, proj, proj, do, dzg, stat)


def _row_tile(R, C, n_arrays):
    budget = 24 * 1024 * 1024 // (2 * n_arrays * 4 * max(C, LANES))
    return _tile(R, max(8, budget), 8)


def _ew_sum(parts, name, also_bf16=False):
    R, C = parts[0].shape
    tr = _row_tile(R, C, len(parts) + 2)
    n = len(parts)

    def body(*refs):
        acc = refs[0][...].astype(F32) + refs[1][...].astype(F32)
        for r in refs[2:n]:
            acc = acc + r[...].astype(F32)
        refs[n][...] = acc
        if also_bf16:
            refs[n + 1][...] = acc.astype(BF16)

    blk = pl.BlockSpec((tr, C), lambda i: (i, 0))
    out_shape = [jax.ShapeDtypeStruct((R, C), F32)] + ([jax.ShapeDtypeStruct((R, C), BF16)] if also_bf16 else [])
    return _call(
        body, name=name, grid=(R // tr,), in_specs=[blk] * n, out_specs=[blk] * len(out_shape),
        out_shape=out_shape, compiler_params=_params(),
    )(*parts)


def _adamw(w, g, m, v, name):
    R, C = w.shape
    tr = _row_tile(R, C, 7)
    c1 = 1.0 / (1.0 - ADAM_B1 ** ADAM_STEP)
    c2 = 1.0 / (1.0 - ADAM_B2 ** ADAM_STEP)

    def body(w_ref, g_ref, m_ref, v_ref, d_ref, m2_ref, v2_ref):
        gv = g_ref[...]
        m2 = ADAM_B1 * m_ref[...] + (1.0 - ADAM_B1) * gv
        v2 = ADAM_B2 * v_ref[...] + (1.0 - ADAM_B2) * (gv * gv)
        m2_ref[...] = m2
        v2_ref[...] = v2
        d_ref[...] = -ADAM_LR * ((m2 * c1) / (jnp.sqrt(v2 * c2) + ADAM_EPS) + ADAM_WD * w_ref[...])

    blk = pl.BlockSpec((tr, C), lambda i: (i, 0))
    out = jax.ShapeDtypeStruct((R, C), F32)
    return _call(
        body, name=name, grid=(R // tr,), in_specs=[blk] * 4, out_specs=[blk] * 3, out_shape=[out] * 3,
        compiler_params=_params(),
    )(w, g, m, v)


def _me():
    return lax.axis_index("x"), lax.axis_index("y"), lax.axis_index("c")


def _chip_of(x, y):
    return 2 * x + y


def _other_chips(x, y):
    return [(x, 1 - y), (1 - x, y), (1 - x, 1 - y)]


def _gather_steps(ins_h, outs_h, send1, recv1, send2, recv2):
    nh = len(ins_h)
    x, y, c = _me()
    mine = _chip_of(x, y)
    chips = _other_chips(x, y)
    sib = (x, y, 1 - c)

    def landed(i, k, half):
        return outs_h[i].at[_chip_of(*chips[k]), half]

    def first(i, k):
        return pltpu.make_async_remote_copy(
            src_ref=ins_h[i].at[c], dst_ref=outs_h[i].at[mine, c], send_sem=send1.at[i, k], recv_sem=recv1.at[i, k],
            device_id=(*chips[k], c), device_id_type=MESH)

    def passed(i, k):
        return pltpu.make_async_remote_copy(
            src_ref=landed(i, k, c), dst_ref=landed(i, k, c), send_sem=send2.at[i, k], recv_sem=recv2.at[i, k],
            device_id=sib, device_id_type=MESH)

    def start():
        for i in range(nh):
            for k in range(3):
                first(i, k).start()

    def finish():
        for i in range(nh):
            for k in range(3):
                pltpu.make_async_remote_copy(
                    src_ref=ins_h[i].at[c], dst_ref=landed(i, k, c), send_sem=send1.at[i, k], recv_sem=recv1.at[i, k],
                    device_id=(*chips[k], c), device_id_type=MESH).wait_recv()
                passed(i, k).start()
        for i in range(nh):
            for k in range(3):
                pltpu.make_async_remote_copy(
                    src_ref=landed(i, k, c), dst_ref=landed(i, k, 1 - c), send_sem=send2.at[i, k],
                    recv_sem=recv2.at[i, k], device_id=sib, device_id_type=MESH).wait_recv()
        for i in range(nh):
            for k in range(3):
                first(i, k).wait_send()
                passed(i, k).wait_send()

    return start, finish


def _gather_sems(nh):
    return [pltpu.SemaphoreType.DMA((nh, 3)) for _ in range(4)]


def _gather_weights(halves, smalls):
    nh, ns = len(halves), len(smalls)

    def body(*refs):
        ins_h, ins_s = refs[:nh], refs[nh:nh + ns]
        outs_h, outs_s = refs[nh + ns:2 * nh + ns], refs[2 * nh + ns:2 * (nh + ns)]
        send1, recv1, send2, recv2, send3, recv3 = refs[2 * (nh + ns):]
        x, y, c = _me()
        mine = _chip_of(x, y)
        chips = _other_chips(x, y)

        def small(i, k):
            return pltpu.make_async_remote_copy(
                src_ref=ins_s[i], dst_ref=outs_s[i].at[mine], send_sem=send3.at[i, k], recv_sem=recv3.at[i, k],
                device_id=(*chips[k], c), device_id_type=MESH)

        start, finish = _gather_steps(ins_h, outs_h, send1, recv1, send2, recv2)
        start()
        for i in range(ns):
            for k in range(3):
                small(i, k).start()
        finish()
        for i in range(ns):
            for k in range(3):
                pltpu.make_async_remote_copy(
                    src_ref=ins_s[i], dst_ref=outs_s[i].at[_chip_of(*chips[k])], send_sem=send3.at[i, k],
                    recv_sem=recv3.at[i, k], device_id=(*chips[k], c), device_id_type=MESH).wait_recv()
                small(i, k).wait_send()

    out_shape = ([jax.ShapeDtypeStruct((4,) + a.shape, a.dtype) for a in halves]
                 + [jax.ShapeDtypeStruct((4,) + a.shape, a.dtype) for a in smalls])
    n = nh + ns
    res = _call(
        body, name="gather_weights", in_specs=[HBM] * n, out_specs=[HBM] * n, out_shape=out_shape,
        scratch_shapes=_gather_sems(nh) + [pltpu.SemaphoreType.DMA((max(ns, 1), 3)),
                                           pltpu.SemaphoreType.DMA((max(ns, 1), 3))],
        compiler_params=_params(),
    )(*halves, *smalls)
    return res[:nh], res[nh:]


def _plan(ins, outs, sems, copies):
    def steps(in_refs, out_refs, *sem_refs):
        def start():
            for cp in copies(in_refs, out_refs, *sem_refs):
                cp.start()

        def finish():
            for cp in copies(in_refs, out_refs, *sem_refs):
                cp.wait()

        return start, finish

    return dict(ins=list(ins), outs=list(outs), sems=list(sems), steps=steps)


def _pair_exchange_plan(grads):
    n = len(grads)

    def copies(ins, got, send, recv):
        x, y, c = _me()
        return [pltpu.make_async_remote_copy(
            src_ref=ins[i].at[j, 1 - c], dst_ref=got[i].at[j], send_sem=send.at[i, j], recv_sem=recv.at[i, j],
            device_id=(x, y, 1 - c), device_id_type=MESH) for i in range(n) for j in range(4)]

    return _plan(grads, [jax.ShapeDtypeStruct((4,) + g.shape[2:], g.dtype) for g in grads],
                 [pltpu.SemaphoreType.DMA((n, 4)), pltpu.SemaphoreType.DMA((n, 4))], copies)


def _chip_exchange_plan(sums):
    n = len(sums)

    def copies(ins, got, send, recv):
        x, y, c = _me()
        chips = _other_chips(x, y)
        return [pltpu.make_async_remote_copy(
            src_ref=ins[i].at[_chip_of(*chips[k])], dst_ref=got[i].at[k], send_sem=send.at[i, k],
            recv_sem=recv.at[i, k], device_id=(*chips[k], c), device_id_type=MESH) for i in range(n) for k in range(3)]

    return _plan(sums, [jax.ShapeDtypeStruct((3,) + a.shape[1:], a.dtype) for a in sums],
                 [pltpu.SemaphoreType.DMA((n, 3)), pltpu.SemaphoreType.DMA((n, 3))], copies)


def _pair_share_plan(halves):
    n = len(halves)

    def copies(ins, outs, send, recv):
        x, y, c = _me()
        return [pltpu.make_async_remote_copy(
            src_ref=ins[i], dst_ref=outs[i], send_sem=send.at[i], recv_sem=recv.at[i],
            device_id=(x, y, 1 - c), device_id_type=MESH) for i in range(n)]

    return _plan(halves, [jax.ShapeDtypeStruct(h.shape, h.dtype) for h in halves],
                 [pltpu.SemaphoreType.DMA((n,)), pltpu.SemaphoreType.DMA((n,))], copies)


def _run_exchange(plan, name):
    ni, no = len(plan["ins"]), len(plan["outs"])

    def body(*refs):
        start, finish = plan["steps"](refs[:ni], refs[ni:ni + no], *refs[ni + no:])
        start()
        finish()

    return _call(
        body, name=name, in_specs=[HBM] * ni, out_specs=[HBM] * no, out_shape=plan["outs"],
        scratch_shapes=plan["sems"], compiler_params=_params(),
    )(*plan["ins"])


def _allreduce_small(vec):
    P = vec.shape[1]

    def body(v_ref, sum_ref, all_ref, send, recv):
        x, y, c = _me()
        me = 4 * x + 2 * y + c
        all_ref[pl.ds(me, 1)] = v_ref[...][None]
        cps = []
        for d in range(1, 8):
            peer = (jnp.bitwise_xor(x, d >> 2), jnp.bitwise_xor(y, (d >> 1) & 1), jnp.bitwise_xor(c, d & 1))
            r = pltpu.make_async_remote_copy(
                src_ref=v_ref, dst_ref=all_ref.at[me], send_sem=send.at[d - 1], recv_sem=recv.at[d - 1],
                device_id=peer, device_id_type=MESH)
            r.start()
            cps.append(r)
        for d in range(1, 8):
            src = jnp.bitwise_xor(me, d)
            pltpu.make_async_remote_copy(
                src_ref=v_ref, dst_ref=all_ref.at[src], send_sem=send.at[d - 1], recv_sem=recv.at[d - 1],
                device_id=(x, y, c), device_id_type=MESH).wait_recv()
        for r in cps:
            r.wait_send()
        acc = all_ref[0]
        for i in range(1, 8):
            acc = acc + all_ref[i]
        sum_ref[...] = acc

    vm = pl.BlockSpec(memory_space=pltpu.VMEM)
    return _call(
        body, name="allreduce_small", in_specs=[vm], out_specs=[vm, vm],
        out_shape=[jax.ShapeDtypeStruct((8, P), F32), jax.ShapeDtypeStruct((8, 8, P), F32)],
        scratch_shapes=[pltpu.SemaphoreType.DMA((7,)), pltpu.SemaphoreType.DMA((7,))],
        compiler_params=_params(),
    )(vec)[0]


def _per_batch(mod, B, D):
    return [mod[:B, i * D:(i + 1) * D].reshape(B, 1, D) for i in range(3)]


def _pad_rows8(a):
    return jnp.concatenate([a, jnp.zeros((8 - a.shape[0],) + a.shape[1:], a.dtype)], axis=0)


def _layer_fwd(x, c8, w, S, fox, tag, gather=()):
    T, D = x.shape
    B = T // S
    DI = w["w_out"].shape[0]
    H = DI // HEAD_DIM
    tq = _tile(S, FOX_BLOCK if fox else SB_BLOCK, 8)
    mod = _mod_fwd(c8, w["w_ada"], w["b_ada"], tag + "_mod_fwd")
    shift, scale, gate = _per_batch(mod, B, D)
    proj, h, gathered = _ln_proj(x, shift, scale, w["norm_g"], w["w_in"], S, tag + "_ln_proj", gather)
    saved = dict(x=x, h=h, proj=proj, scale=scale, gate=gate, gathered=gathered)
    if fox:
        fl = _mm(h, w["w_f"], "nn", F32, tag + "_flogit").reshape(B, S, LANES)
        cum = _cum_fwd(fl, w["b_f"], tag + "_cum_fwd")
        cumrow = cum[:, :, :H].transpose(0, 2, 1).reshape(B, H, S // tq, 1, tq)
        o, stat = _fox_fwd(proj, cum, cumrow, tag + "_attn_fwd")
        saved.update(fl=fl, cum=cum, cumrow=cumrow)
    else:
        o, stat = _sb_fwd(proj, B, tq, tag + "_attn_fwd")
    xo, y, u = _gate_out(o, proj, w["w_out"], x, gate, S, tag + "_gate_out")
    saved.update(o=o, stat=stat, y=y, u=u)
    return xo, saved


def _hosted(side, sent, call):
    if side is None:
        return call(None), None
    plan, _ = next(side) if sent is None else side.send(sent)
    return call(plan)


def _layer_bwd(dxo, sv, w, cT, S, fox, tag, side=None):
    T, D = dxo.shape
    B = T // S
    DI = w["w_out"].shape[0]
    H = DI // HEAD_DIM
    tq = _tile(S, FOX_BLOCK if fox else SB_BLOCK, 8)
    dy, do, dzg, dgate = _out_bwd(dxo, sv["y"], sv["gate"], w["w_out"], sv["o"], sv["proj"], S, tag + "_out_bwd")
    dw_out, landed = _hosted(side, None, lambda r: _mm(sv["u"], dy, "tn", F32, tag + "_dw_out", tm=1024, tn=1024,
                                                      tk=2048, rider=r))
    g = {"w_out": dw_out}
    q_cols = jnp.where(jnp.arange(4 * DI)[None, :] < DI, Q_SCALE, 1.0).astype(F32)
    if fox:
        dproj, dcs = _fox_bwd(sv["proj"], do, dzg, sv["o"], sv["stat"], sv["cum"], sv["cumrow"], tag + "_attn_bwd")
        dcs = dcs.reshape(B, H, S).transpose(0, 2, 1)
        dcs = jnp.concatenate([dcs, jnp.zeros((B, S, LANES - H), F32)], axis=-1)
        dfl, db_f = _cum_bwd(dcs, sv["fl"], w["b_f"], tag + "_cum_bwd")
        g["b_f"] = db_f[:, :H]
        dfl = dfl.reshape(T, LANES).astype(BF16)
    else:
        dproj = _sb_bwd(sv["proj"], do, dzg, sv["stat"], B, tq, tag + "_attn_bwd")
    g["w_in"], landed = _hosted(side, landed, lambda r: _mm(sv["h"], dproj, "tn", F32, tag + "_dw_in", tm=1024, tn=2048,
                                                            tk=1024, col_scale=q_cols, rider=r))
    dh, landed = _hosted(side, landed, lambda r: _mm(dproj, w["w_in"], "nt", F32, tag + "_dh", tm=2048, tn=1024,
                                                     tk=1024, rider=r))
    dhs = [dh]
    if side is not None:
        try:
            side.send(landed)
        except StopIteration as done:
            g["side"] = done.value
    if fox:
        dw_f = _mm(sv["h"], dfl, "tn", F32, tag + "_dw_f", tm=1024, tn=LANES, tk=2048)
        g["w_in"] = jnp.concatenate([g["w_in"], dw_f[:, :H]], axis=1)
        dhs.append(_mm(dfl, w["w_f"], "nt", F32, tag + "_dh_f", tm=2048, tn=1024, tk=LANES))
    dx, dshift, dscale, dg = _ln_bwd(dhs, sv["x"], dxo, sv["scale"], w["norm_g"], S, tag + "_ln_bwd")
    g["norm_g"] = dg
    dmod = jnp.concatenate([dshift, dscale, dgate], axis=-1).reshape(B, 3 * D)
    g["w_ada"], g["b_ada"] = _mod_bwd(cT, _pad_rows8(dmod), B, tag + "_mod_bwd")
    return dx, g


def _local_step(x3, c, tgt3, wf, ws, final_g, sb_halves=(), sb_side=None):
    B, S, D = x3.shape
    T = B * S
    x = x3.reshape(T, D)
    c8 = _pad_rows8(c)
    cT = c8.T
    x1, sv1 = _layer_fwd(x, c8, wf, S, True, "fox", sb_halves)
    if sb_halves:
        ws = ws(sv1["gathered"])
    x2, sv2 = _layer_fwd(x1, c8, ws, S, False, "sb")
    dx2, dgf, loss = _final_loss(x2, tgt3.reshape(T, D), final_g, S, "final_loss")
    dx1, gs = _layer_bwd(dx2, sv2, ws, cT, S, False, "sb")
    dx0, gf = _layer_bwd(dx1, sv1, wf, cT, S, True, "fox", None if sb_side is None else sb_side(gs))
    return loss, dx0.reshape(B, S, D), gf, gs, dgf


def _cols_to_shards(a):
    R, C4 = a.shape
    return a.reshape(R, 4, C4 // 4).transpose(1, 0, 2)


def _shards_to_cols(a):
    n, R, C = a.shape
    return a.transpose(1, 0, 2).reshape(R, n * C)


def kernel(x, c, fox_norm_g, fox_w_ada, fox_b_ada, fox_w_in, fox_b_f, fox_w_out, sb_norm_g, sb_w_ada, sb_b_ada, sb_w_in, sb_w_out, final_norm_g, loss_target, m_fox_norm_g, m_fox_w_ada, m_fox_b_ada, m_fox_w_in, m_fox_b_f, m_fox_w_out, m_sb_norm_g, m_sb_w_ada, m_sb_b_ada, m_sb_w_in, m_sb_w_out, m_final_norm_g, v_fox_norm_g, v_fox_w_ada, v_fox_b_ada, v_fox_w_in, v_fox_b_f, v_fox_w_out, v_sb_norm_g, v_sb_w_ada, v_sb_b_ada, v_sb_w_in, v_sb_w_out, v_final_norm_g):
    B, S, D = x.shape
    DI = 4 * fox_w_out.shape[1]
    H = DI // HEAD_DIM
    chip = _chip_of(lax.axis_index("x"), lax.axis_index("y"))

    big_names = ["fox_w_ada", "fox_w_in", "fox_w_out", "sb_w_ada", "sb_w_in", "sb_w_out"]
    big = dict(fox_w_ada=fox_w_ada[0], fox_w_in=fox_w_in[0], fox_w_out=fox_w_out[0],
               sb_w_ada=sb_w_ada[0], sb_w_in=sb_w_in[0], sb_w_out=sb_w_out[0])
    for n in ("fox_w_in", "sb_w_in"):
        width = big[n].shape[1]
        is_q = chip * width + jnp.arange(width)[None, :] < DI
        big[n] = big[n] * jnp.where(is_q, Q_SCALE, 1.0).astype(F32)
    halves = {n: big[n].astype(BF16).reshape(2, big[n].shape[0] // 2, big[n].shape[1]) for n in big_names}
    fox_names, sb_names = big_names[:3], big_names[3:]

    def assemble(names, gathered):
        full = {}
        for n, a in zip(names, gathered):
            a = lax.dynamic_update_index_in_dim(a, halves[n], chip, 0)
            a = a.reshape(4, a.shape[1] * a.shape[2], a.shape[3])
            full[n] = a.reshape(4 * a.shape[1], a.shape[2]) if n.endswith("w_out") else _shards_to_cols(a)
        return full

    gathered, gsmall = _gather_weights([halves[n] for n in fox_names], [sb_norm_g, sb_b_ada])
    gsmall = [lax.dynamic_update_index_in_dim(a, own, chip, 0) for a, own in zip(gsmall, [sb_norm_g, sb_b_ada])]
    full = assemble(fox_names, gathered)
    sb_norm_full = gsmall[0].reshape(1, D)
    sb_b_ada_full = gsmall[1].reshape(1, 3 * D)
    w_f = jnp.concatenate([full["fox_w_in"][:, 4 * DI:], jnp.zeros((D, LANES - H), BF16)], axis=1)
    b_f = jnp.concatenate([fox_b_f, jnp.zeros((1, LANES - H), F32)], axis=1)
    wf = dict(w_ada=full["fox_w_ada"], b_ada=fox_b_ada, norm_g=fox_norm_g, w_in=full["fox_w_in"][:, :4 * DI],
              w_f=w_f, b_f=b_f, w_out=full["fox_w_out"])

    def ws(gathered_sb):
        f = assemble(sb_names, gathered_sb)
        return dict(w_ada=f["sb_w_ada"], b_ada=sb_b_ada_full, norm_g=sb_norm_full, w_in=f["sb_w_in"], w_out=f["sb_w_out"])

    core = lax.axis_index("c")

    def reduction(names, part, tag):
        shard_major = []
        for n in names:
            a = part[n]
            a = a.reshape(4, a.shape[0] // 4, a.shape[1]) if n.endswith("w_out") else _cols_to_shards(a)
            shard_major.append(a.reshape(4, 2, a.shape[1] // 2, a.shape[2]))
        got = yield _pair_exchange_plan(shard_major), tag + "_grad_pair_exchange"
        pair_f32, pair_bf16 = [], []
        for n, g4, b in zip(names, shard_major, got):
            a = lax.dynamic_index_in_dim(g4, core, axis=1, keepdims=False)
            r, C = a.shape[1:]
            s32, s16 = _ew_sum([a.reshape(4 * r, C), b.reshape(4 * r, C)], n + "_pair_sum", also_bf16=True)
            pair_f32.append(s32.reshape(4, r, C))
            pair_bf16.append(s16.reshape(4, r, C))
        others = yield _chip_exchange_plan(pair_bf16), tag + "_grad_chip_exchange"
        reduced_halves = [_ew_sum([lax.dynamic_index_in_dim(a, chip, axis=0, keepdims=False), b[0], b[1], b[2]],
                                  n + "_chip_sum")[0] for n, a, b in zip(names, pair_f32, others)]
        theirs = yield _pair_share_plan(reduced_halves), tag + "_grad_pair_share"
        return {n: jnp.concatenate([jnp.where(core == 0, a, b), jnp.where(core == 0, b, a)], axis=0)
                for n, a, b in zip(names, reduced_halves, theirs)}

    def sb_side(gs):
        return reduction(sb_names, dict(sb_w_ada=gs["w_ada"], sb_w_in=gs["w_in"], sb_w_out=gs["w_out"]), "sb")

    loss, grad_x, gf, gs, dgf = _local_step(x, c, loss_target, wf, ws, final_norm_g.reshape(1, D),
                                            [halves[n] for n in sb_names], sb_side)
    grad_big = dict(gf["side"])
    fox_red = reduction(fox_names, dict(fox_w_ada=gf["w_ada"], fox_w_in=gf["w_in"], fox_w_out=gf["w_out"]), "fox")
    try:
        plan, name = next(fox_red)
        while True:
            plan, name = fox_red.send(_run_exchange(plan, name))
    except StopIteration as done:
        grad_big.update(done.value)

    pieces = [loss, gf["norm_g"], gf["b_ada"], jnp.concatenate([gf["b_f"], jnp.zeros((1, LANES - H), F32)], axis=1),
              gs["norm_g"], gs["b_ada"], dgf]
    vec = jnp.concatenate(pieces, axis=1)
    red = _allreduce_small(_pad_rows8(vec))[0:1]
    offs = [0]
    for p in pieces:
        offs.append(offs[-1] + p.shape[1])
    r_loss, r_fng, r_fba, r_fbf, r_sng, r_sba, r_fin = [red[:, offs[i]:offs[i + 1]] for i in range(7)]
    small_grads = dict(
        fox_norm_g=r_fng, fox_b_ada=r_fba, fox_b_f=r_fbf[:, :H],
        sb_norm_g=lax.dynamic_slice_in_dim(r_sng, chip * (D // 4), D // 4, axis=1),
        sb_b_ada=lax.dynamic_slice_in_dim(r_sba, chip * (3 * D // 4), 3 * D // 4, axis=1),
        final_norm_g=r_fin)

    weights = dict(fox_norm_g=fox_norm_g, fox_w_ada=fox_w_ada, fox_b_ada=fox_b_ada, fox_w_in=fox_w_in, fox_b_f=fox_b_f,
                   fox_w_out=fox_w_out, sb_norm_g=sb_norm_g, sb_w_ada=sb_w_ada, sb_b_ada=sb_b_ada, sb_w_in=sb_w_in,
                   sb_w_out=sb_w_out, final_norm_g=final_norm_g)
    ms = dict(fox_norm_g=m_fox_norm_g, fox_w_ada=m_fox_w_ada, fox_b_ada=m_fox_b_ada, fox_w_in=m_fox_w_in,
              fox_b_f=m_fox_b_f, fox_w_out=m_fox_w_out, sb_norm_g=m_sb_norm_g, sb_w_ada=m_sb_w_ada,
              sb_b_ada=m_sb_b_ada, sb_w_in=m_sb_w_in, sb_w_out=m_sb_w_out, final_norm_g=m_final_norm_g)
    vs = dict(fox_norm_g=v_fox_norm_g, fox_w_ada=v_fox_w_ada, fox_b_ada=v_fox_b_ada, fox_w_in=v_fox_w_in,
              fox_b_f=v_fox_b_f, fox_w_out=v_fox_w_out, sb_norm_g=v_sb_norm_g, sb_w_ada=v_sb_w_ada,
              sb_b_ada=v_sb_b_ada, sb_w_in=v_sb_w_in, sb_w_out=v_sb_w_out, final_norm_g=v_final_norm_g)
    order = ["fox_norm_g", "fox_w_ada", "fox_b_ada", "fox_w_in", "fox_b_f", "fox_w_out", "sb_norm_g", "sb_w_ada",
             "sb_b_ada", "sb_w_in", "sb_w_out", "final_norm_g"]
    grads, deltas, new_m, new_v = {}, {}, {}, {}
    for n in big_names:
        shp = weights[n].shape
        g2 = grad_big[n]
        d, m2, v2 = _adamw(weights[n][0], g2, ms[n][0], vs[n][0], n + "_adamw")
        grads[n], deltas[n], new_m[n], new_v[n] = g2.reshape(shp), d.reshape(shp), m2.reshape(shp), v2.reshape(shp)
    small_names = [n for n in order if n not in big_names]
    sizes = [small_grads[n].shape[1] for n in small_names]
    total = sum(sizes)
    padn = (-total) % LANES

    def pack(d):
        return jnp.concatenate([d[n].reshape(1, -1) for n in small_names] + [jnp.ones((1, padn), F32)], axis=1)

    sd, sm, sv_ = _adamw(pack(weights), pack(small_grads), pack(ms), pack(vs), "small_adamw")
    o = 0
    for n, sz in zip(small_names, sizes):
        shp = weights[n].shape
        grads[n] = small_grads[n].reshape(shp)
        deltas[n], new_m[n], new_v[n] = (t[:, o:o + sz].reshape(shp) for t in (sd, sm, sv_))
        o += sz
    return (r_loss[0, 0], grad_x, *[grads[n] for n in order], *[deltas[n] for n in order],
            *[new_m[n] for n in order], *[new_v[n] for n in order])
```
